```python
import math
import jax, jax.numpy as jnp
from jax import lax
import numpy as np

D_MODEL = 1024
BATCH = 8
SEQ = 4096
DEPTH = 2

CHUNK = 64
D_FF = 2816
N_BRANCH = 4
BRANCH_W = D_MODEL // 2
POOL_WINDOWS = (2, 4, 8, 16)
N_POOL_GROUPS = len(POOL_WINDOWS)
POOL_GROUP_W = BRANCH_W // N_POOL_GROUPS
SCONV_K = 3
CCONV_K = 31
SGU_BLOCK = 128
SGU_HEADS = 4
SGU_HEAD_W = BRANCH_W // SGU_HEADS
COLS_A = BRANCH_W
COLS_B = 3 * BRANCH_W
COLS_C = 2 * BRANCH_W
COLS_D = 2 * BRANCH_W
COLS_G = N_BRANCH * D_MODEL
IN_COLS = COLS_A + COLS_B + COLS_C + COLS_D + COLS_G
SPLITS = (COLS_A, COLS_A + COLS_B, COLS_A + COLS_B + COLS_C, COLS_A + COLS_B + COLS_C + COLS_D)
EPS = 1e-6

kernel_name = "hybrid_gated_parallel_mixers_macaron"


def rms_norm(x, g):
    x32 = x.astype(jnp.float32)
    y = x32 * lax.rsqrt(jnp.mean(x32 * x32, axis=-1, keepdims=True) + EPS)
    return (y * g.astype(jnp.float32)).astype(x.dtype)


def layer_norm(x, g, b):
    x32 = x.astype(jnp.float32)
    mu = jnp.mean(x32, axis=-1, keepdims=True)
    var = jnp.mean(jnp.square(x32 - mu), axis=-1, keepdims=True)
    y = (x32 - mu) * lax.rsqrt(var + EPS)
    return (y * g.astype(jnp.float32) + b.astype(jnp.float32)).astype(x.dtype)


def swiglu_half(x, g, w13, w2):
    h = rms_norm(x, g)
    a, b = jnp.split(h @ w13, 2, axis=-1)
    return x + 0.5 * ((jax.nn.silu(a) * b) @ w2)


def causal_dwconv(x, w):
    k, c = w.shape
    return lax.conv_general_dilated(
        x, w[:, None, :].astype(x.dtype), window_strides=(1,), padding=[(k - 1, 0)],
        dimension_numbers=("NWC", "WIO", "NWC"), feature_group_count=c)


def pool_mixer(a, pool_w, pool_scale):
    bn, s, _ = a.shape
    a32 = a.astype(jnp.float32).reshape(bn, s, N_POOL_GROUPS, POOL_GROUP_W)
    cs = jnp.cumsum(a32, axis=1)
    t = jnp.arange(s)
    outs = []
    for gi, win in enumerate(POOL_WINDOWS):
        c = cs[:, :, gi]
        lag = jnp.pad(c, ((0, 0), (win, 0), (0, 0)))[:, :s]
        cnt = jnp.minimum(t + 1, win).astype(jnp.float32)[None, :, None]
        outs.append((c - lag) / cnt - a32[:, :, gi])
    d = jnp.stack(outs, axis=2).astype(a.dtype)
    y = jnp.einsum('bsgc,gcd->bsgd', d, pool_w).reshape(bn, s, BRANCH_W)
    return y * pool_scale


def short_conv_mixer(p, sconv_w):
    xin, bg, cg = jnp.split(p, 3, axis=-1)
    return bg * causal_dwconv(cg * xin, sconv_w)


def conformer_conv_mixer(p, cconv_w, ln_g, ln_b):
    a, b = jnp.split(p, 2, axis=-1)
    y = a * jax.nn.sigmoid(b)
    y = causal_dwconv(y, cconv_w)
    y = layer_norm(y, ln_g, ln_b)
    return jax.nn.silu(y)


def spatial_gating_mixer(p, ln_g, ln_b, sgu_w, sgu_b):
    u, v = jnp.split(jax.nn.gelu(p), 2, axis=-1)
    v = layer_norm(v, ln_g, ln_b)
    bn, s, _ = v.shape
    v = v.reshape(bn, s // SGU_BLOCK, SGU_BLOCK, SGU_HEADS, SGU_HEAD_W)
    i = jnp.arange(SGU_BLOCK)
    mask = (i[None, :] // CHUNK) <= (i[:, None] // CHUNK)
    w = jnp.where(mask[None], sgu_w, jnp.zeros((), sgu_w.dtype))
    z = jnp.einsum('hij,bnjhc->bnihc', w, v) + sgu_b.T[None, None, :, :, None]
    return u * z.reshape(bn, s, BRANCH_W)


def _fwd_setup_inputs(seed: int = 0) -> dict:
    key = jax.random.key(seed)
    ks = jax.random.split(key, 24)
    f32 = jnp.float32

    def nrm(k, shape, scale):
        return jax.random.normal(k, shape, f32) * scale

    def gain(k, shape):
        return 1.0 + 0.05 * jax.random.normal(k, shape, f32)

    L = DEPTH
    return {
        "x": jax.random.normal(ks[0], (BATCH, SEQ, D_MODEL), f32),
        "ffn1_norm": gain(ks[1], (L, D_MODEL)),
        "ffn1_w13": nrm(ks[2], (L, D_MODEL, 2 * D_FF), D_MODEL ** -0.5),
        "ffn1_w2": nrm(ks[3], (L, D_FF, D_MODEL), D_FF ** -0.5),
        "mix_norm": gain(ks[4], (L, D_MODEL)),
        "w_in": nrm(ks[5], (L, D_MODEL, IN_COLS), D_MODEL ** -0.5),
        "pool_w": nrm(ks[6], (L, N_POOL_GROUPS, POOL_GROUP_W, POOL_GROUP_W), POOL_GROUP_W ** -0.5),
        "pool_scale": gain(ks[7], (L, BRANCH_W)),
        "sconv_w": nrm(ks[8], (L, SCONV_K, BRANCH_W), SCONV_K ** -0.5),
        "cconv_w": nrm(ks[9], (L, CCONV_K, BRANCH_W), CCONV_K ** -0.5),
        "cconv_ln_g": gain(ks[10], (L, BRANCH_W)),
        "cconv_ln_b": nrm(ks[11], (L, BRANCH_W), 0.02),
        "sgu_ln_g": gain(ks[12], (L, BRANCH_W)),
        "sgu_ln_b": nrm(ks[13], (L, BRANCH_W), 0.02),
        "sgu_w": nrm(ks[14], (L, SGU_HEADS, SGU_BLOCK, SGU_BLOCK), SGU_BLOCK ** -0.5),
        "sgu_b": gain(ks[15], (L, SGU_HEADS, SGU_BLOCK)),
        "w_up": nrm(ks[16], (L, N_BRANCH, BRANCH_W, D_MODEL), BRANCH_W ** -0.5),
        "w_out": nrm(ks[17], (L, D_MODEL, D_MODEL), D_MODEL ** -0.5),
        "ffn2_norm": gain(ks[18], (L, D_MODEL)),
        "ffn2_w13": nrm(ks[19], (L, D_MODEL, 2 * D_FF), D_MODEL ** -0.5),
        "ffn2_w2": nrm(ks[20], (L, D_FF, D_MODEL), D_FF ** -0.5),
        "final_norm": gain(ks[21], (D_MODEL,)),
    }


def _fwd_reference(x, ffn1_norm, ffn1_w13, ffn1_w2, mix_norm, w_in, pool_w, pool_scale,
              sconv_w, cconv_w, cconv_ln_g, cconv_ln_b, sgu_ln_g, sgu_ln_b, sgu_w, sgu_b,
              w_up, w_out, ffn2_norm, ffn2_w13, ffn2_w2, final_norm):
    bn, s, d = x.shape
    for l in range(DEPTH):
        x = swiglu_half(x, ffn1_norm[l], ffn1_w13[l], ffn1_w2[l])
        h = rms_norm(x, mix_norm[l])
        proj = h @ w_in[l]
        pa, pb, pc, pd, pg = jnp.split(proj, SPLITS, axis=-1)
        ya = pool_mixer(pa, pool_w[l], pool_scale[l])
        yb = short_conv_mixer(pb, sconv_w[l])
        yc = conformer_conv_mixer(pc, cconv_w[l], cconv_ln_g[l], cconv_ln_b[l])
        yd = spatial_gating_mixer(pd, sgu_ln_g[l], sgu_ln_b[l], sgu_w[l], sgu_b[l])
        y = jnp.stack([ya, yb, yc, yd], axis=2)
        up = jnp.einsum('bsgc,gcd->bsgd', y, w_up[l])
        gates = jax.nn.sigmoid(pg.reshape(bn, s, N_BRANCH, d))
        merged = jnp.sum(gates * up, axis=2)
        x = x + merged @ w_out[l]
        x = swiglu_half(x, ffn2_norm[l], ffn2_w13[l], ffn2_w2[l])
    return rms_norm(x, final_norm)


import jax as _jax
import jax.numpy as _jnp

TWIN_FORMAT = 'train_step'
FWD_PARAMS = ['x', 'ffn1_norm', 'ffn1_w13', 'ffn1_w2', 'mix_norm', 'w_in', 'pool_w', 'pool_scale', 'sconv_w', 'cconv_w', 'cconv_ln_g', 'cconv_ln_b', 'sgu_ln_g', 'sgu_ln_b', 'sgu_w', 'sgu_b', 'w_up', 'w_out', 'ffn2_norm', 'ffn2_w13', 'ffn2_w2', 'final_norm']
TWIN_WEIGHTS = ['ffn1_norm', 'ffn1_w13', 'ffn1_w2', 'mix_norm', 'w_in', 'pool_w', 'pool_scale', 'sconv_w', 'cconv_w', 'cconv_ln_g', 'cconv_ln_b', 'sgu_ln_g', 'sgu_ln_b', 'sgu_w', 'sgu_b', 'w_up', 'w_out', 'ffn2_norm', 'ffn2_w13', 'ffn2_w2', 'final_norm']
TWIN_DIFF_INPUT = 'x'
TWIN_INPUTS = ['x', 'ffn1_norm', 'ffn1_w13', 'ffn1_w2', 'mix_norm', 'w_in', 'pool_w', 'pool_scale', 'sconv_w', 'cconv_w', 'cconv_ln_g', 'cconv_ln_b', 'sgu_ln_g', 'sgu_ln_b', 'sgu_w', 'sgu_b', 'w_up', 'w_out', 'ffn2_norm', 'ffn2_w13', 'ffn2_w2', 'final_norm', 'loss_target', 'm_ffn1_norm', 'm_ffn1_w13', 'm_ffn1_w2', 'm_mix_norm', 'm_w_in', 'm_pool_w', 'm_pool_scale', 'm_sconv_w', 'm_cconv_w', 'm_cconv_ln_g', 'm_cconv_ln_b', 'm_sgu_ln_g', 'm_sgu_ln_b', 'm_sgu_w', 'm_sgu_b', 'm_w_up', 'm_w_out', 'm_ffn2_norm', 'm_ffn2_w13', 'm_ffn2_w2', 'm_final_norm', 'v_ffn1_norm', 'v_ffn1_w13', 'v_ffn1_w2', 'v_mix_norm', 'v_w_in', 'v_pool_w', 'v_pool_scale', 'v_sconv_w', 'v_cconv_w', 'v_cconv_ln_g', 'v_cconv_ln_b', 'v_sgu_ln_g', 'v_sgu_ln_b', 'v_sgu_w', 'v_sgu_b', 'v_w_up', 'v_w_out', 'v_ffn2_norm', 'v_ffn2_w13', 'v_ffn2_w2', 'v_final_norm']
TWIN_OUTPUTS = ['loss', 'grad_x', 'grad_ffn1_norm', 'grad_ffn1_w13', 'grad_ffn1_w2', 'grad_mix_norm', 'grad_w_in', 'grad_pool_w', 'grad_pool_scale', 'grad_sconv_w', 'grad_cconv_w', 'grad_cconv_ln_g', 'grad_cconv_ln_b', 'grad_sgu_ln_g', 'grad_sgu_ln_b', 'grad_sgu_w', 'grad_sgu_b', 'grad_w_up', 'grad_w_out', 'grad_ffn2_norm', 'grad_ffn2_w13', 'grad_ffn2_w2', 'grad_final_norm', 'delta_ffn1_norm', 'delta_ffn1_w13', 'delta_ffn1_w2', 'delta_mix_norm', 'delta_w_in', 'delta_pool_w', 'delta_pool_scale', 'delta_sconv_w', 'delta_cconv_w', 'delta_cconv_ln_g', 'delta_cconv_ln_b', 'delta_sgu_ln_g', 'delta_sgu_ln_b', 'delta_sgu_w', 'delta_sgu_b', 'delta_w_up', 'delta_w_out', 'delta_ffn2_norm', 'delta_ffn2_w13', 'delta_ffn2_w2', 'delta_final_norm', 'new_m_ffn1_norm', 'new_m_ffn1_w13', 'new_m_ffn1_w2', 'new_m_mix_norm', 'new_m_w_in', 'new_m_pool_w', 'new_m_pool_scale', 'new_m_sconv_w', 'new_m_cconv_w', 'new_m_cconv_ln_g', 'new_m_cconv_ln_b', 'new_m_sgu_ln_g', 'new_m_sgu_ln_b', 'new_m_sgu_w', 'new_m_sgu_b', 'new_m_w_up', 'new_m_w_out', 'new_m_ffn2_norm', 'new_m_ffn2_w13', 'new_m_ffn2_w2', 'new_m_final_norm', 'new_v_ffn1_norm', 'new_v_ffn1_w13', 'new_v_ffn1_w2', 'new_v_mix_norm', 'new_v_w_in', 'new_v_pool_w', 'new_v_pool_scale', 'new_v_sconv_w', 'new_v_cconv_w', 'new_v_cconv_ln_g', 'new_v_cconv_ln_b', 'new_v_sgu_ln_g', 'new_v_sgu_ln_b', 'new_v_sgu_w', 'new_v_sgu_b', 'new_v_w_up', 'new_v_w_out', 'new_v_ffn2_norm', 'new_v_ffn2_w13', 'new_v_ffn2_w2', 'new_v_final_norm']
TWIN_LEAF_KINDS = {'loss': 'loss', 'grad_x': 'grad_x', 'grad_ffn1_norm': 'grad_w', 'grad_ffn1_w13': 'grad_w', 'grad_ffn1_w2': 'grad_w', 'grad_mix_norm': 'grad_w', 'grad_w_in': 'grad_w', 'grad_pool_w': 'grad_w', 'grad_pool_scale': 'grad_w', 'grad_sconv_w': 'grad_w', 'grad_cconv_w': 'grad_w', 'grad_cconv_ln_g': 'grad_w', 'grad_cconv_ln_b': 'grad_w', 'grad_sgu_ln_g': 'grad_w', 'grad_sgu_ln_b': 'grad_w', 'grad_sgu_w': 'grad_w', 'grad_sgu_b': 'grad_w', 'grad_w_up': 'grad_w', 'grad_w_out': 'grad_w', 'grad_ffn2_norm': 'grad_w', 'grad_ffn2_w13': 'grad_w', 'grad_ffn2_w2': 'grad_w', 'grad_final_norm': 'grad_w', 'delta_ffn1_norm': 'delta_w', 'delta_ffn1_w13': 'delta_w', 'delta_ffn1_w2': 'delta_w', 'delta_mix_norm': 'delta_w', 'delta_w_in': 'delta_w', 'delta_pool_w': 'delta_w', 'delta_pool_scale': 'delta_w', 'delta_sconv_w': 'delta_w', 'delta_cconv_w': 'delta_w', 'delta_cconv_ln_g': 'delta_w', 'delta_cconv_ln_b': 'delta_w', 'delta_sgu_ln_g': 'delta_w', 'delta_sgu_ln_b': 'delta_w', 'delta_sgu_w': 'delta_w', 'delta_sgu_b': 'delta_w', 'delta_w_up': 'delta_w', 'delta_w_out': 'delta_w', 'delta_ffn2_norm': 'delta_w', 'delta_ffn2_w13': 'delta_w', 'delta_ffn2_w2': 'delta_w', 'delta_final_norm': 'delta_w', 'new_m_ffn1_norm': 'new_m', 'new_m_ffn1_w13': 'new_m', 'new_m_ffn1_w2': 'new_m', 'new_m_mix_norm': 'new_m', 'new_m_w_in': 'new_m', 'new_m_pool_w': 'new_m', 'new_m_pool_scale': 'new_m', 'new_m_sconv_w': 'new_m', 'new_m_cconv_w': 'new_m', 'new_m_cconv_ln_g': 'new_m', 'new_m_cconv_ln_b': 'new_m', 'new_m_sgu_ln_g': 'new_m', 'new_m_sgu_ln_b': 'new_m', 'new_m_sgu_w': 'new_m', 'new_m_sgu_b': 'new_m', 'new_m_w_up': 'new_m', 'new_m_w_out': 'new_m', 'new_m_ffn2_norm': 'new_m', 'new_m_ffn2_w13': 'new_m', 'new_m_ffn2_w2': 'new_m', 'new_m_final_norm': 'new_m', 'new_v_ffn1_norm': 'new_v', 'new_v_ffn1_w13': 'new_v', 'new_v_ffn1_w2': 'new_v', 'new_v_mix_norm': 'new_v', 'new_v_w_in': 'new_v', 'new_v_pool_w': 'new_v', 'new_v_pool_scale': 'new_v', 'new_v_sconv_w': 'new_v', 'new_v_cconv_w': 'new_v', 'new_v_cconv_ln_g': 'new_v', 'new_v_cconv_ln_b': 'new_v', 'new_v_sgu_ln_g': 'new_v', 'new_v_sgu_ln_b': 'new_v', 'new_v_sgu_w': 'new_v', 'new_v_sgu_b': 'new_v', 'new_v_w_up': 'new_v', 'new_v_w_out': 'new_v', 'new_v_ffn2_norm': 'new_v', 'new_v_ffn2_w13': 'new_v', 'new_v_ffn2_w2': 'new_v', 'new_v_final_norm': 'new_v'}


def _forward(args):
    return _fwd_reference(*[args[k] for k in FWD_PARAMS])


def _output_shape():
    out = _jax.eval_shape(lambda: _forward(_fwd_setup_inputs(0)))
    return out.shape, out.dtype

N_MICROBATCH = 1
ADAM_LR = 0.001
ADAM_B1 = 0.9
ADAM_B2 = 0.999
ADAM_EPS = 1e-08
ADAM_WD = 0.01
ADAM_STEP = 10
PER_EXAMPLE_BATCH_AXIS = {'x': 0, 'loss_target': 0}
SHARED_INPUTS = []
_WEIGHT_DTYPES = {'ffn1_norm': _jnp.float32, 'ffn1_w13': _jnp.float32, 'ffn1_w2': _jnp.float32, 'mix_norm': _jnp.float32, 'w_in': _jnp.float32, 'pool_w': _jnp.float32, 'pool_scale': _jnp.float32, 'sconv_w': _jnp.float32, 'cconv_w': _jnp.float32, 'cconv_ln_g': _jnp.float32, 'cconv_ln_b': _jnp.float32, 'sgu_ln_g': _jnp.float32, 'sgu_ln_b': _jnp.float32, 'sgu_w': _jnp.float32, 'sgu_b': _jnp.float32, 'w_up': _jnp.float32, 'w_out': _jnp.float32, 'ffn2_norm': _jnp.float32, 'ffn2_w13': _jnp.float32, 'ffn2_w2': _jnp.float32, 'final_norm': _jnp.float32}
MOMENT_SCALE = {'ffn1_norm': 8.880528e-02, 'ffn1_w13': 3.737522e-02, 'ffn1_w2': 6.092959e-02, 'mix_norm': 1.745788e-01, 'w_in': 6.160885e-02, 'pool_w': 9.143391e-02, 'pool_scale': 9.279957e-02, 'sconv_w': 1.056336e-01, 'cconv_w': 6.468957e-02, 'cconv_ln_g': 7.689565e-02, 'cconv_ln_b': 7.754088e-02, 'sgu_ln_g': 5.751695e-02, 'sgu_ln_b': 5.864540e-02, 'sgu_w': 5.828989e-02, 'sgu_b': 6.931589e-02, 'w_up': 6.275628e-02, 'w_out': 1.262360e-01, 'ffn2_norm': 5.473154e-02, 'ffn2_w13': 2.364483e-02, 'ffn2_w2': 3.861921e-02, 'final_norm': 3.208870e+01}


def _to_microbatches(a, axis):
    t = _jnp.moveaxis(a, axis, 0)
    t = t.reshape((N_MICROBATCH, t.shape[0] // N_MICROBATCH) + t.shape[1:])
    return _jnp.moveaxis(t, 1, axis + 1)


def setup_inputs(seed: int = 0) -> dict:
    inp = _fwd_setup_inputs(seed)
    key = _jax.random.fold_in(_jax.random.key(seed), 7919)
    shape, _ = _output_shape()
    out = dict(inp)
    out["loss_target"] = _jax.random.normal(_jax.random.fold_in(key, 0), shape, _jnp.float32)
    for i, name in enumerate(TWIN_WEIGHTS):
        w = inp[name].astype(_jnp.float32)
        if MOMENT_SCALE is None:
            s = _jnp.sqrt(_jnp.mean(_jnp.square(w)) + 1e-30)
        else:
            s = MOMENT_SCALE[name]
        km, kv = _jax.random.split(_jax.random.fold_in(key, i + 1))
        out[name] = w
        out["m_" + name] = s * _jax.random.normal(km, w.shape, _jnp.float32)
        out["v_" + name] = (s * s) * _jax.random.uniform(kv, w.shape, _jnp.float32, 0.5, 1.5)
    if N_MICROBATCH > 1:
        for name, axis in PER_EXAMPLE_BATCH_AXIS.items():
            out[name] = _to_microbatches(out[name], axis)
    return {'x': out['x'], 'ffn1_norm': out['ffn1_norm'], 'ffn1_w13': out['ffn1_w13'], 'ffn1_w2': out['ffn1_w2'], 'mix_norm': out['mix_norm'], 'w_in': out['w_in'], 'pool_w': out['pool_w'], 'pool_scale': out['pool_scale'], 'sconv_w': out['sconv_w'], 'cconv_w': out['cconv_w'], 'cconv_ln_g': out['cconv_ln_g'], 'cconv_ln_b': out['cconv_ln_b'], 'sgu_ln_g': out['sgu_ln_g'], 'sgu_ln_b': out['sgu_ln_b'], 'sgu_w': out['sgu_w'], 'sgu_b': out['sgu_b'], 'w_up': out['w_up'], 'w_out': out['w_out'], 'ffn2_norm': out['ffn2_norm'], 'ffn2_w13': out['ffn2_w13'], 'ffn2_w2': out['ffn2_w2'], 'final_norm': out['final_norm'], 'loss_target': out['loss_target'], 'm_ffn1_norm': out['m_ffn1_norm'], 'm_ffn1_w13': out['m_ffn1_w13'], 'm_ffn1_w2': out['m_ffn1_w2'], 'm_mix_norm': out['m_mix_norm'], 'm_w_in': out['m_w_in'], 'm_pool_w': out['m_pool_w'], 'm_pool_scale': out['m_pool_scale'], 'm_sconv_w': out['m_sconv_w'], 'm_cconv_w': out['m_cconv_w'], 'm_cconv_ln_g': out['m_cconv_ln_g'], 'm_cconv_ln_b': out['m_cconv_ln_b'], 'm_sgu_ln_g': out['m_sgu_ln_g'], 'm_sgu_ln_b': out['m_sgu_ln_b'], 'm_sgu_w': out['m_sgu_w'], 'm_sgu_b': out['m_sgu_b'], 'm_w_up': out['m_w_up'], 'm_w_out': out['m_w_out'], 'm_ffn2_norm': out['m_ffn2_norm'], 'm_ffn2_w13': out['m_ffn2_w13'], 'm_ffn2_w2': out['m_ffn2_w2'], 'm_final_norm': out['m_final_norm'], 'v_ffn1_norm': out['v_ffn1_norm'], 'v_ffn1_w13': out['v_ffn1_w13'], 'v_ffn1_w2': out['v_ffn1_w2'], 'v_mix_norm': out['v_mix_norm'], 'v_w_in': out['v_w_in'], 'v_pool_w': out['v_pool_w'], 'v_pool_scale': out['v_pool_scale'], 'v_sconv_w': out['v_sconv_w'], 'v_cconv_w': out['v_cconv_w'], 'v_cconv_ln_g': out['v_cconv_ln_g'], 'v_cconv_ln_b': out['v_cconv_ln_b'], 'v_sgu_ln_g': out['v_sgu_ln_g'], 'v_sgu_ln_b': out['v_sgu_ln_b'], 'v_sgu_w': out['v_sgu_w'], 'v_sgu_b': out['v_sgu_b'], 'v_w_up': out['v_w_up'], 'v_w_out': out['v_w_out'], 'v_ffn2_norm': out['v_ffn2_norm'], 'v_ffn2_w13': out['v_ffn2_w13'], 'v_ffn2_w2': out['v_ffn2_w2'], 'v_final_norm': out['v_final_norm']}


def _loss(weights, diff, rest, loss_target):
    with _jax.named_scope("forward"):
        args = {**rest, TWIN_DIFF_INPUT: diff, **{k: w.astype(_WEIGHT_DTYPES[k]) for k, w in weights.items()}}
        y = _forward(args)
    with _jax.named_scope("loss_head"):
        err = _jnp.square(y.astype(_jnp.float32) - loss_target)
        return 0.5 * _jnp.sum(_jnp.mean(err, axis=-1)) if err.ndim else 0.5 * err


def _adamw(w, g, m, v):
    m = ADAM_B1 * m + (1.0 - ADAM_B1) * g
    v = ADAM_B2 * v + (1.0 - ADAM_B2) * _jnp.square(g)
    m_hat = m / (1.0 - ADAM_B1 ** ADAM_STEP)
    v_hat = v / (1.0 - ADAM_B2 ** ADAM_STEP)
    delta = -ADAM_LR * (m_hat / (_jnp.sqrt(v_hat) + ADAM_EPS) + ADAM_WD * w)
    return delta, m, v


def reference(x, ffn1_norm, ffn1_w13, ffn1_w2, mix_norm, w_in, pool_w, pool_scale, sconv_w, cconv_w, cconv_ln_g, cconv_ln_b, sgu_ln_g, sgu_ln_b, sgu_w, sgu_b, w_up, w_out, ffn2_norm, ffn2_w13, ffn2_w2, final_norm, loss_target, m_ffn1_norm, m_ffn1_w13, m_ffn1_w2, m_mix_norm, m_w_in, m_pool_w, m_pool_scale, m_sconv_w, m_cconv_w, m_cconv_ln_g, m_cconv_ln_b, m_sgu_ln_g, m_sgu_ln_b, m_sgu_w, m_sgu_b, m_w_up, m_w_out, m_ffn2_norm, m_ffn2_w13, m_ffn2_w2, m_final_norm, v_ffn1_norm, v_ffn1_w13, v_ffn1_w2, v_mix_norm, v_w_in, v_pool_w, v_pool_scale, v_sconv_w, v_cconv_w, v_cconv_ln_g, v_cconv_ln_b, v_sgu_ln_g, v_sgu_ln_b, v_sgu_w, v_sgu_b, v_w_up, v_w_out, v_ffn2_norm, v_ffn2_w13, v_ffn2_w2, v_final_norm):
    given = dict(x=x, ffn1_norm=ffn1_norm, ffn1_w13=ffn1_w13, ffn1_w2=ffn1_w2, mix_norm=mix_norm, w_in=w_in, pool_w=pool_w, pool_scale=pool_scale, sconv_w=sconv_w, cconv_w=cconv_w, cconv_ln_g=cconv_ln_g, cconv_ln_b=cconv_ln_b, sgu_ln_g=sgu_ln_g, sgu_ln_b=sgu_ln_b, sgu_w=sgu_w, sgu_b=sgu_b, w_up=w_up, w_out=w_out, ffn2_norm=ffn2_norm, ffn2_w13=ffn2_w13, ffn2_w2=ffn2_w2, final_norm=final_norm, loss_target=loss_target, m_ffn1_norm=m_ffn1_norm, m_ffn1_w13=m_ffn1_w13, m_ffn1_w2=m_ffn1_w2, m_mix_norm=m_mix_norm, m_w_in=m_w_in, m_pool_w=m_pool_w, m_pool_scale=m_pool_scale, m_sconv_w=m_sconv_w, m_cconv_w=m_cconv_w, m_cconv_ln_g=m_cconv_ln_g, m_cconv_ln_b=m_cconv_ln_b, m_sgu_ln_g=m_sgu_ln_g, m_sgu_ln_b=m_sgu_ln_b, m_sgu_w=m_sgu_w, m_sgu_b=m_sgu_b, m_w_up=m_w_up, m_w_out=m_w_out, m_ffn2_norm=m_ffn2_norm, m_ffn2_w13=m_ffn2_w13, m_ffn2_w2=m_ffn2_w2, m_final_norm=m_final_norm, v_ffn1_norm=v_ffn1_norm, v_ffn1_w13=v_ffn1_w13, v_ffn1_w2=v_ffn1_w2, v_mix_norm=v_mix_norm, v_w_in=v_w_in, v_pool_w=v_pool_w, v_pool_scale=v_pool_scale, v_sconv_w=v_sconv_w, v_cconv_w=v_cconv_w, v_cconv_ln_g=v_cconv_ln_g, v_cconv_ln_b=v_cconv_ln_b, v_sgu_ln_g=v_sgu_ln_g, v_sgu_ln_b=v_sgu_ln_b, v_sgu_w=v_sgu_w, v_sgu_b=v_sgu_b, v_w_up=v_w_up, v_w_out=v_w_out, v_ffn2_norm=v_ffn2_norm, v_ffn2_w13=v_ffn2_w13, v_ffn2_w2=v_ffn2_w2, v_final_norm=v_final_norm)
    weights = {n: given[n] for n in TWIN_WEIGHTS}
    shared = {n: given[n] for n in SHARED_INPUTS}
    per_example = {n: given[n] for n in ['x']}
    grad_fn = _jax.value_and_grad(_loss, argnums=(0, 1))

    def one_microbatch(ex, loss_target):
        ex = dict(ex)
        diff = ex.pop(TWIN_DIFF_INPUT)
        return grad_fn(weights, diff, {**shared, **ex}, loss_target)

    if N_MICROBATCH == 1:
        loss, (grad_w, grad_x) = one_microbatch(per_example, given["loss_target"])
    else:
        def body(carry, xs):
            loss_sum, grad_sum = carry
            l_k, (gw_k, gx_k) = one_microbatch(xs[0], xs[1])
            with _jax.named_scope("update"):
                return (loss_sum + l_k, _jax.tree.map(_jnp.add, grad_sum, gw_k)), gx_k

        init = (_jnp.zeros((), _jnp.float32), _jax.tree.map(_jnp.zeros_like, weights))
        (loss, grad_w), grad_x = _jax.lax.scan(body, init, (per_example, given["loss_target"]))
    with _jax.named_scope("update"):
        delta_w, new_m, new_v = {}, {}, {}
        for n in TWIN_WEIGHTS:
            delta_w[n], new_m[n], new_v[n] = _adamw(weights[n], grad_w[n], given["m_" + n], given["v_" + n])
    return (loss, grad_x, *[grad_w[n] for n in TWIN_WEIGHTS], *[delta_w[n] for n in TWIN_WEIGHTS],
            *[new_m[n] for n in TWIN_WEIGHTS], *[new_v[n] for n in TWIN_WEIGHTS])
```

```python
import functools
import math

import jax
import jax.numpy as jnp
from jax import lax
from jax.experimental import pallas as pl
from jax.experimental.pallas import tpu as pltpu

F32 = jnp.float32
BF16 = jnp.bfloat16
EPS = 1e-6
ADAM_LR = 0.001
ADAM_B1 = 0.9
ADAM_B2 = 0.999
ADAM_EPS = 1e-08
ADAM_WD = 0.01
ADAM_STEP = 10
SGU_BLOCK = 128
SGU_CHUNK = 64
SCONV_K = 3
CCONV_K = 31
HALO = 32
V7X_VMEM_LIMIT = 48 * 1024 * 1024
MESH_AXES = ("x", "y", "c")
N_DEV = 8
_GELU_C0 = math.sqrt(2.0 / math.pi)
_GELU_C1 = 0.044715

BS = pl.BlockSpec
SDS = jax.ShapeDtypeStruct
ANY = pl.BlockSpec(memory_space=pl.ANY)


def _params(sem=None):
    return pltpu.CompilerParams(dimension_semantics=sem, vmem_limit_bytes=V7X_VMEM_LIMIT)


def _tile(n, pref, align=128):
    if n <= pref:
        return n
    t = pref - pref % align
    while t > 0:
        if n % t == 0:
            return t
        t -= align
    return n


def _sig(v):
    return 1.0 / (1.0 + jnp.exp(-v))


def _gelu(v):
    t = jnp.tanh(_GELU_C0 * (v + _GELU_C1 * (v * v * v)))
    return 0.5 * v * (1.0 + t), t


def _gelu_grad(v, t):
    return 0.5 * (1.0 + t) + 0.5 * v * (1.0 - t * t) * (_GELU_C0 * (1.0 + 3.0 * _GELU_C1 * v * v))


def _ln_stats(v):
    mu = jnp.mean(v, axis=-1, keepdims=True)
    vc = v - mu
    var = jnp.mean(vc * vc, axis=-1, keepdims=True)
    rstd = lax.rsqrt(var + EPS)
    return vc * rstd, rstd


def _ln_bwd(dvh, vh, rstd):
    return rstd * (dvh - jnp.mean(dvh, axis=-1, keepdims=True) - vh * jnp.mean(dvh * vh, axis=-1, keepdims=True))


def _dot(a, b):
    return jnp.dot(a, b, preferred_element_type=F32)


def _dot_nt(a, b):
    return lax.dot_general(a, b, (((1,), (1,)), ((), ())), preferred_element_type=F32)


def _dot_tn(a, b):
    return lax.dot_general(a, b, (((0,), (0,)), ((), ())), preferred_element_type=F32)


def _mesh_pos():
    return lax.axis_index("x"), lax.axis_index("y"), lax.axis_index("c")


def _all_gather(shards, n_layers, name):
    n_in = len(shards)
    outs = []
    for t, s in enumerate(shards):
        if n_layers[t] is None:
            outs.append((t, None))
        else:
            outs.extend((t, l) for l in range(n_layers[t]))
    n_out = len(outs)

    def body(*refs):
        ins = refs[:n_in]
        dsts = refs[n_in:n_in + n_out]
        send_sems, recv_sems, local_sems = refs[n_in + n_out:]
        x, y, c = _mesh_pos()
        me, sibling = (x, y, c), (x, y, 1 - c)
        chips = [(1 - x, y), (x, 1 - y), (1 - x, 1 - y)]

        def src_of(o):
            t, l = outs[o]
            return ins[t] if l is None else ins[t].at[l]

        def row(o, p):
            return dsts[o].at[4 * p[0] + 2 * p[1] + p[2]]

        def copy(o, k, block, to, own=False):
            return pltpu.make_async_remote_copy(
                src_ref=src_of(o) if own else row(o, block), dst_ref=row(o, block),
                send_sem=send_sems.at[o * 7 + k], recv_sem=recv_sems.at[o * 7 + k],
                device_id=to, device_id_type=pl.DeviceIdType.MESH)

        mine, first, passed = [], [], []
        for o in range(n_out):
            cp = pltpu.make_async_copy(src_of(o), row(o, me), local_sems.at[o])
            cp.start()
            mine.append(cp)
            f = [copy(o, 1 + j, me, (*chip, c), own=True) for j, chip in enumerate(chips)]
            f.append(copy(o, 0, me, sibling, own=True))
            for cp in f:
                cp.start()
            first.extend(f)
        for o in range(n_out):
            for j, chip in enumerate(chips):
                copy(o, 1 + j, (*chip, c), me).wait_recv()
                cp = copy(o, 4 + j, (*chip, c), sibling)
                cp.start()
                passed.append(cp)
        for o in range(n_out):
            copy(o, 0, sibling, me).wait_recv()
            for j, chip in enumerate(chips):
                copy(o, 4 + j, (*chip, 1 - c), me).wait_recv()
        for cp in first + passed:
            cp.wait_send()
        for cp in mine:
            cp.wait()

    out_shape = []
    for t, l in outs:
        shp = shards[t].shape if l is None else shards[t].shape[1:]
        out_shape.append(SDS((N_DEV,) + tuple(shp), shards[t].dtype))
    return pl.pallas_call(
        body, name=name, out_shape=out_shape,
        in_specs=[ANY] * n_in, out_specs=[ANY] * n_out,
        scratch_shapes=[pltpu.SemaphoreType.DMA((7 * n_out,)), pltpu.SemaphoreType.DMA((7 * n_out,)),
                        pltpu.SemaphoreType.DMA((n_out,))],
        compiler_params=pltpu.CompilerParams(has_side_effects=True),
    )(*shards)


def _sibling_exchange(grads, name):
    n_t = len(grads)
    flat = [g for gl in grads for g in gl]
    n_l = [len(gl) for gl in grads]

    def body(*refs):
        ins = refs[:len(flat)]
        dsts = refs[len(flat):len(flat) + n_t]
        send_sems, recv_sems = refs[len(flat) + n_t:]
        x, y, c = _mesh_pos()
        copies = []
        i = 0
        s = 0
        for t in range(n_t):
            for l in range(n_l[t]):
                for q in range(4):
                    cp = pltpu.make_async_remote_copy(
                        src_ref=ins[i].at[2 * q + (1 - c)], dst_ref=dsts[t].at[q, l],
                        send_sem=send_sems.at[s], recv_sem=recv_sems.at[s],
                        device_id=(x, y, 1 - c), device_id_type=pl.DeviceIdType.MESH)
                    cp.start()
                    copies.append(cp)
                    s += 1
                i += 1
        for cp in copies:
            cp.wait()

    n_sem = 4 * len(flat)
    out_shape = [SDS((4, n_l[t]) + tuple(grads[t][0].shape[1:]), grads[t][0].dtype) for t in range(n_t)]
    return pl.pallas_call(
        body, name=name, out_shape=out_shape,
        in_specs=[ANY] * len(flat), out_specs=[ANY] * n_t,
        scratch_shapes=[pltpu.SemaphoreType.DMA((n_sem,)), pltpu.SemaphoreType.DMA((n_sem,))],
        compiler_params=pltpu.CompilerParams(has_side_effects=True),
    )(*flat)


def _chip_exchange(sums, name):
    n_t = len(sums)

    def body(*refs):
        ins = refs[:n_t]
        dsts = refs[n_t:2 * n_t]
        send_sems, recv_sems = refs[2 * n_t:]
        x, y, c = _mesh_pos()
        chips = [(1 - x, y), (x, 1 - y), (1 - x, 1 - y)]
        copies = []
        for t in range(n_t):
            for k, chip in enumerate(chips):
                cp = pltpu.make_async_remote_copy(
                    src_ref=ins[t].at[2 * chip[0] + chip[1]], dst_ref=dsts[t].at[k],
                    send_sem=send_sems.at[3 * t + k], recv_sem=recv_sems.at[3 * t + k],
                    device_id=(*chip, c), device_id_type=pl.DeviceIdType.MESH)
                cp.start()
                copies.append(cp)
        for cp in copies:
            cp.wait()

    out_shape = [SDS((3,) + tuple(s.shape[1:]), s.dtype) for s in sums]
    return pl.pallas_call(
        body, name=name, out_shape=out_shape,
        in_specs=[ANY] * n_t, out_specs=[ANY] * n_t,
        scratch_shapes=[pltpu.SemaphoreType.DMA((3 * n_t,)), pltpu.SemaphoreType.DMA((3 * n_t,))],
        compiler_params=pltpu.CompilerParams(has_side_effects=True),
    )(*sums)


def _rmsnorm_fwd(x, g, name):
    S, D = x.shape
    tm = _tile(S, 512, 16)

    def body(x_ref, g_ref, h_ref):
        xv = x_ref[...]
        r = lax.rsqrt(jnp.mean(xv * xv, axis=-1, keepdims=True) + EPS)
        h_ref[...] = (xv * r * g_ref[...]).astype(BF16)

    return pl.pallas_call(
        body, name=name, grid=(S // tm,),
        in_specs=[BS((tm, D), lambda i: (i, 0)), BS((1, D), lambda i: (0, 0))],
        out_specs=BS((tm, D), lambda i: (i, 0)),
        out_shape=SDS((S, D), BF16), compiler_params=_params(("parallel",)),
    )(x, g)


def _matmul_fwd(a, w, name):
    S, K = a.shape
    C, _, Fc = w.shape
    tn = _tile(Fc, 1408)
    tm = _tile(S, 512, 16)

    def body(a_ref, w_ref, o_ref):
        o_ref[0] = _dot(a_ref[...], w_ref[0])

    return pl.pallas_call(
        body, name=name, grid=(C, Fc // tn, S // tm),
        in_specs=[BS((tm, K), lambda c, n, i: (i, 0)), BS((1, K, tn), lambda c, n, i: (c, 0, n))],
        out_specs=BS((1, tm, tn), lambda c, n, i: (c, i, n)),
        out_shape=SDS((C, S, Fc), F32), compiler_params=_params(("parallel", "parallel", "parallel")),
    )(a, w)


def _swiglu_down(ab, w2, x, name):
    _, S, F = ab.shape
    D = w2.shape[1]
    tk = _tile(F, 256)
    tm = _tile(S, 1024, 16)

    def body(ab_ref, w_ref, x_ref, o_ref):
        k = pl.program_id(1)
        a = ab_ref[0]
        s = a * _sig(a) * ab_ref[1]
        p = 0.5 * _dot(s.astype(BF16), w_ref[...])

        @pl.when(k == 0)
        def _():
            o_ref[...] = x_ref[...] + p

        @pl.when(k > 0)
        def _():
            o_ref[...] += p

    return pl.pallas_call(
        body, name=name, grid=(S // tm, F // tk),
        in_specs=[BS((2, tm, tk), lambda i, k: (0, i, k)), BS((tk, D), lambda i, k: (k, 0)),
                  BS((tm, D), lambda i, k: (i, 0))],
        out_specs=BS((tm, D), lambda i, k: (i, 0)),
        out_shape=SDS((S, D), F32), compiler_params=_params(("parallel", "arbitrary")),
    )(ab, w2, x)


def _ffn_bwd_hidden(dy, w2, ab, name):
    S, D = dy.shape
    F = w2.shape[0]
    tk = _tile(F, 1408)
    tm = _tile(S, 256, 16)

    def body(dy_ref, w_ref, ab_ref, dab_ref, s_ref):
        ds = 0.5 * _dot_nt(dy_ref[...].astype(BF16), w_ref[...])
        a = ab_ref[0]
        b = ab_ref[1]
        sg = _sig(a)
        sa = a * sg
        dab_ref[0] = (ds * b * (sg * (1.0 + a * (1.0 - sg)))).astype(BF16)
        dab_ref[1] = (ds * sa).astype(BF16)
        s_ref[...] = (0.5 * (sa * b)).astype(BF16)

    return pl.pallas_call(
        body, name=name, grid=(F // tk, S // tm),
        in_specs=[BS((tm, D), lambda k, i: (i, 0)), BS((tk, D), lambda k, i: (k, 0)),
                  BS((2, tm, tk), lambda k, i: (0, i, k))],
        out_specs=[BS((2, tm, tk), lambda k, i: (0, i, k)), BS((tm, tk), lambda k, i: (i, k))],
        out_shape=[SDS((2, S, F), BF16), SDS((S, F), BF16)],
        compiler_params=_params(("parallel", "parallel")),
    )(dy, w2, ab)


def _matmul_tn(a, b, n_c, name):
    G, S, M = a.shape
    _, _, Fc = b.shape
    C = n_c
    tM = _tile(M, 1408)
    tn = _tile(Fc, 1408)
    ts = _tile(S, 512, 16)
    n_s = S // ts

    def body(a_ref, b_ref, o_ref, acc):
        s = pl.program_id(4)
        p = _dot_tn(a_ref[0].astype(BF16), b_ref[0].astype(BF16))

        @pl.when(s == 0)
        def _():
            acc[...] = p

        @pl.when(s > 0)
        def _():
            acc[...] += p

        @pl.when(s == n_s - 1)
        def _():
            o_ref[0] = acc[...].astype(BF16)

    return pl.pallas_call(
        body, name=name, grid=(G, M // tM, C, Fc // tn, n_s),
        in_specs=[BS((1, ts, tM), lambda g, m, c, n, s: (g, s, m)),
                  BS((1, ts, tn), lambda g, m, c, n, s: (g * C + c, s, n))],
        out_specs=BS((1, tM, tn), lambda g, m, c, n, s: (g * C + c, m, n)),
        out_shape=SDS((G * C, M, Fc), BF16),
        scratch_shapes=[pltpu.VMEM((tM, tn), F32)],
        compiler_params=_params(("parallel", "parallel", "parallel", "parallel", "arbitrary")),
    )(a, b)


def _matmul_nt_normbwd(b, w, x, gam, dres, name):
    C, S, Fc = b.shape
    D = w.shape[1]
    tk = _tile(Fc, 512)
    tm = _tile(S, 512, 16)
    nk = Fc // tk

    def body(b_ref, w_ref, x_ref, g_ref, r_ref, dx_ref, dg_ref, acc):
        i, c, k = pl.program_id(0), pl.program_id(1), pl.program_id(2)
        p = _dot_nt(b_ref[0], w_ref[0])
        first = jnp.logical_and(c == 0, k == 0)

        @pl.when(first)
        def _():
            acc[...] = p

        @pl.when(jnp.logical_not(first))
        def _():
            acc[...] += p

        @pl.when(jnp.logical_and(c == C - 1, k == nk - 1))
        def _():
            xv = x_ref[...]
            r = lax.rsqrt(jnp.mean(xv * xv, axis=-1, keepdims=True) + EPS)
            xn = xv * r
            dh = acc[...]
            dxn = dh * g_ref[...]
            dx_ref[...] = r_ref[...] + r * (dxn - xn * jnp.mean(dxn * xn, axis=-1, keepdims=True))
            dgp = jnp.sum(dh * xn, axis=0, keepdims=True)

            @pl.when(i == 0)
            def _():
                dg_ref[...] = dgp

            @pl.when(i > 0)
            def _():
                dg_ref[...] += dgp

    return pl.pallas_call(
        body, name=name, grid=(S // tm, C, nk),
        in_specs=[BS((1, tm, tk), lambda i, c, k: (c, i, k)), BS((1, D, tk), lambda i, c, k: (c, 0, k)),
                  BS((tm, D), lambda i, c, k: (i, 0)), BS((1, D), lambda i, c, k: (0, 0)),
                  BS((tm, D), lambda i, c, k: (i, 0))],
        out_specs=[BS((tm, D), lambda i, c, k: (i, 0)), BS((1, D), lambda i, c, k: (0, 0))],
        out_shape=[SDS((S, D), F32), SDS((1, D), F32)],
        scratch_shapes=[pltpu.VMEM((tm, D), F32)],
        compiler_params=_params(("arbitrary", "arbitrary", "arbitrary")),
    )(b, w, x, gam, dres)


def _final_loss(x, gam, target, name):
    S, D = x.shape
    tm = _tile(S, 512, 8)

    def body(x_ref, g_ref, t_ref, loss_ref, dx_ref, dg_ref):
        i = pl.program_id(0)
        xv = x_ref[...]
        r = lax.rsqrt(jnp.mean(xv * xv, axis=-1, keepdims=True) + EPS)
        xn = xv * r
        err = xn * g_ref[...] - t_ref[...]
        part = 0.5 * jnp.sum(jnp.mean(err * err, axis=-1, keepdims=True), axis=0, keepdims=True)
        dy = err * (1.0 / D)
        dxn = dy * g_ref[...]
        dx_ref[...] = r * (dxn - xn * jnp.mean(dxn * xn, axis=-1, keepdims=True))
        dgp = jnp.sum(dy * xn, axis=0, keepdims=True)
        lp = jnp.broadcast_to(part, loss_ref.shape)

        @pl.when(i == 0)
        def _():
            dg_ref[...] = dgp
            loss_ref[...] = lp

        @pl.when(i > 0)
        def _():
            dg_ref[...] += dgp
            loss_ref[...] += lp

    return pl.pallas_call(
        body, name=name, grid=(S // tm,),
        in_specs=[BS((tm, D), lambda i: (i, 0)), BS((1, D), lambda i: (0, 0)), BS((tm, D), lambda i: (i, 0))],
        out_specs=[BS((8, 128), lambda i: (0, 0)), BS((tm, D), lambda i: (i, 0)), BS((1, D), lambda i: (0, 0))],
        out_shape=[SDS((8, 128), F32), SDS((S, D), F32), SDS((1, D), F32)],
        compiler_params=_params(("arbitrary",)),
    )(x, gam, target)


def _sgu_masks():
    ii = lax.broadcasted_iota(jnp.int32, (SGU_BLOCK, SGU_BLOCK), 0) // SGU_CHUNK
    jj = lax.broadcasted_iota(jnp.int32, (SGU_BLOCK, SGU_BLOCK), 1) // SGU_CHUNK
    return jj <= ii, ii <= jj


def _mixers_fwd(proj, pool_w, pool_scale, sconv_w, cconv_w, cln_g, cln_b, sln_g, sln_b, sgu_w, sgu_bias, name):
    _, S, D = proj.shape
    BW = D // 2
    GW = BW // 4
    TS = _tile(S, 256, SGU_BLOCK)
    H = HALO
    hb = TS // H

    def main(blk, col):
        return BS((1, TS, BW), lambda i: (blk, i, col))

    def back(blk, col):
        return BS((1, H, BW), lambda i: (blk, jnp.maximum(i * hb - 1, 0), col))

    def full(a):
        nd = a.ndim
        return BS(a.shape, lambda i: (0,) * nd)

    def body(pa_m, pa_b, xi_m, xi_b, bg_m, cg_m, cg_b, ca_m, ca_b, cb_m, cb_b, du_m, dv_m,
             pw, ps, sw, cw, clg, clb, slg, slb, gw, gbias, y_ref, e1, e2, e3):
        i = pl.program_id(0)
        nb = jnp.where(i > 0, 1.0, 0.0).astype(F32)
        rows = i * TS + lax.broadcasted_iota(jnp.int32, (TS, 1), 0)

        e1[0:H, :] = pa_b[0] * nb
        e1[H:H + TS, :] = pa_m[0]
        for g in range(4):
            cols = slice(g * GW, (g + 1) * GW)
            win = 2 << g
            wsum = e1[H:H + TS, cols]
            for k in range(1, win):
                wsum = wsum + e1[H - k:H - k + TS, cols]
            cnt = jnp.minimum(rows + 1, win).astype(F32)
            d = wsum / cnt - e1[H:H + TS, cols]
            yg = _dot(d.astype(BF16), pw[g].astype(BF16)) * ps[:, cols]
            y_ref[0, :, cols] = yg.astype(BF16)

        e2[0:H, :] = cg_b[0] * xi_b[0] * nb
        e2[H:H + TS, :] = cg_m[0] * xi_m[0]
        cz = sw[0:1, :] * e2[H - 2:H - 2 + TS, :]
        for k in range(1, SCONV_K):
            cz = cz + sw[k:k + 1, :] * e2[H - 2 + k:H - 2 + k + TS, :]
        y_ref[1] = (bg_m[0] * cz).astype(BF16)

        e3[0:H, :] = ca_b[0] * _sig(cb_b[0]) * nb
        e3[H:H + TS, :] = ca_m[0] * _sig(cb_m[0])
        o = H - (CCONV_K - 1)
        y1 = cw[0:1, :] * e3[o:o + TS, :]
        for k in range(1, CCONV_K):
            y1 = y1 + cw[k:k + 1, :] * e3[o + k:o + k + TS, :]
        yh, _ = _ln_stats(y1)
        y2 = yh * clg[...] + clb[...]
        y_ref[2] = (y2 * _sig(y2)).astype(BF16)

        u, _ = _gelu(du_m[0])
        v, _ = _gelu(dv_m[0])
        vh, _ = _ln_stats(v)
        vn = vh * slg[...] + slb[...]
        mask, _ = _sgu_masks()
        for h in range(4):
            wm = jnp.where(mask, gw[h], 0.0).astype(BF16)
            cs = slice(h * GW, (h + 1) * GW)
            for n in range(TS // SGU_BLOCK):
                rs = slice(n * SGU_BLOCK, (n + 1) * SGU_BLOCK)
                z = _dot(wm, vn[rs, cs].astype(BF16)) + gbias[h]
                y_ref[3, rs, cs] = (u[rs, cs] * z).astype(BF16)

    args = [proj] * 13 + [pool_w, pool_scale, sconv_w, cconv_w, cln_g, cln_b, sln_g, sln_b, sgu_w, sgu_bias]
    in_specs = [main(0, 0), back(0, 0), main(0, 1), back(0, 1), main(1, 0), main(1, 1), back(1, 1),
                main(2, 0), back(2, 0), main(2, 1), back(2, 1), main(3, 0), main(3, 1)]
    in_specs += [full(a) for a in args[13:]]
    return pl.pallas_call(
        body, name=name, grid=(S // TS,),
        in_specs=in_specs, out_specs=BS((4, TS, BW), lambda i: (0, i, 0)),
        out_shape=SDS((4, S, BW), BF16),
        scratch_shapes=[pltpu.VMEM((H + TS, BW), F32)] * 3,
        compiler_params=_params(("parallel",)),
    )(*args)


def _mixers_bwd(proj, dy, dproj_gates, pool_w, pool_wt, pool_scale, sconv_w, cconv_w, cln_g, cln_b, sln_g, sln_b,
                sgu_w, sgu_wt, sgu_bias, name):
    _, S, D = proj.shape
    BW = D // 2
    GW = BW // 4
    TS = _tile(S, 256, SGU_BLOCK)
    H = HALO
    hb = TS // H
    n_t = S // TS
    E = TS + H

    def main(blk, col):
        return BS((1, TS, BW), lambda i: (blk, i, col))

    def back(blk, col):
        return BS((1, H, BW), lambda i: (blk, jnp.maximum(i * hb - 1, 0), col))

    def front(blk, col):
        return BS((1, H, BW), lambda i: (blk, jnp.minimum((i + 1) * hb, S // H - 1), col))

    def full(a):
        nd = a.ndim
        return BS(a.shape, lambda i: (0,) * nd)

    def body(pa_b, pa_m, xi_b, xi_m, bg_m, bg_f, cg_b, cg_m, ca_b, ca_m, ca_f, cb_b, cb_m, cb_f, du_m, dv_m,
             dya_m, dya_f, dyb_m, dyb_f, dyc_m, dyc_f, dyd_m,
             pw, pwt, ps, sw, cw, clg, clb, slg, slb, gw, gwt, gbias, _gates_in,
             dp_ref, dpw, dps, dsw, dcw, dclg, dclb, dslg, dslb, dgw, dgb,
             e1, e2, e3, e4, e5, e6):
        i = pl.program_id(0)
        nb = jnp.where(i > 0, 1.0, 0.0).astype(F32)
        nf = jnp.where(i < n_t - 1, 1.0, 0.0).astype(F32)
        rows_m = i * TS + lax.broadcasted_iota(jnp.int32, (TS, 1), 0)
        rows_e = i * TS + lax.broadcasted_iota(jnp.int32, (E, 1), 0)

        @pl.when(i == 0)
        def _():
            for r in (dpw, dps, dsw, dcw, dclg, dclb, dslg, dslb, dgw, dgb):
                r[...] = jnp.zeros(r.shape, F32)

        e1[0:H, :] = pa_b[0] * nb
        e1[H:H + TS, :] = pa_m[0]
        e2[0:TS, :] = dya_m[0] * ps[...]
        e2[TS:E, :] = dya_f[0] * ps[...] * nf
        for g in range(4):
            cols = slice(g * GW, (g + 1) * GW)
            win = 2 << g
            a_m = e1[H:H + TS, cols]
            wsum = a_m
            for k in range(1, win):
                wsum = wsum + e1[H - k:H - k + TS, cols]
            d = wsum / jnp.minimum(rows_m + 1, win).astype(F32) - a_m
            d16 = d.astype(BF16)
            dyp = e2[0:E, cols].astype(BF16)
            dd = _dot(dyp, pwt[g].astype(BF16))
            e3[0:E, cols] = dd / jnp.minimum(rows_e + 1, win).astype(F32)
            da = e3[0:TS, cols] - dd[0:TS]
            for k in range(1, win):
                da = da + e3[k:k + TS, cols]
            dp_ref[0, :, cols] = da.astype(BF16)
            ypre = _dot(d16, pw[g].astype(BF16))
            dps[:, cols] += jnp.sum(dya_m[0][:, cols] * ypre, axis=0, keepdims=True)
            dpw[g] += _dot(jnp.transpose(d).astype(BF16), dyp[0:TS])

        e4[0:H, :] = cg_b[0] * xi_b[0] * nb
        e4[H:H + TS, :] = cg_m[0] * xi_m[0]
        dyb = dyb_m[0]
        e5[0:TS, :] = dyb * bg_m[0]
        e5[TS:E, :] = dyb_f[0] * bg_f[0] * nf
        dcz = e5[0:TS, :]
        cz = None
        dz = None
        for k in range(SCONV_K):
            zk = e4[H - 2 + k:H - 2 + k + TS, :]
            wk = sw[k:k + 1, :]
            cz = wk * zk if cz is None else cz + wk * zk
            t = wk * e5[2 - k:2 - k + TS, :]
            dz = t if dz is None else dz + t
            dsw[k:k + 1, :] += jnp.sum(dcz * zk, axis=0, keepdims=True)
        dp_ref[0, :, BW:2 * BW] = (dz * cg_m[0]).astype(BF16)
        dp_ref[1, :, 0:BW] = (dyb * cz).astype(BF16)
        dp_ref[1, :, BW:2 * BW] = (dz * xi_m[0]).astype(BF16)

        sgm = _sig(cb_m[0])
        e6[0:H, :] = ca_b[0] * _sig(cb_b[0]) * nb
        e6[H:H + TS, :] = ca_m[0] * sgm
        e6[H + TS:H + E, :] = ca_f[0] * _sig(cb_f[0]) * nf
        o = H - (CCONV_K - 1)
        y1 = cw[0:1, :] * e6[o:o + E, :]
        for k in range(1, CCONV_K):
            y1 = y1 + cw[k:k + 1, :] * e6[o + k:o + k + E, :]
        yh, rstd = _ln_stats(y1)
        y2 = yh * clg[...] + clb[...]
        s2 = _sig(y2)
        e1[0:TS, :] = dyc_m[0]
        e1[TS:E, :] = dyc_f[0] * nf
        dy2 = e1[0:E, :] * (s2 * (1.0 + y2 * (1.0 - s2)))
        dclg[...] += jnp.sum((dy2 * yh)[0:TS], axis=0, keepdims=True)
        dclb[...] += jnp.sum(dy2[0:TS], axis=0, keepdims=True)
        e2[0:E, :] = _ln_bwd(dy2 * clg[...], yh, rstd)
        dy1_m = e2[0:TS, :]
        dy0 = None
        for k in range(CCONV_K):
            t = cw[k:k + 1, :] * e2[CCONV_K - 1 - k:CCONV_K - 1 - k + TS, :]
            dy0 = t if dy0 is None else dy0 + t
            dcw[k:k + 1, :] += jnp.sum(dy1_m * e6[o + k:o + k + TS, :], axis=0, keepdims=True)
        dp_ref[2, :, 0:BW] = (dy0 * sgm).astype(BF16)
        dp_ref[2, :, BW:2 * BW] = (dy0 * ca_m[0] * (sgm * (1.0 - sgm))).astype(BF16)

        pu = du_m[0]
        pv = dv_m[0]
        u, tu = _gelu(pu)
        v, tv = _gelu(pv)
        vh, vr = _ln_stats(v)
        vn = vh * slg[...] + slb[...]
        dyd = dyd_m[0]
        mask, mask_t = _sgu_masks()
        for h in range(4):
            wm = jnp.where(mask, gw[h], 0.0).astype(BF16)
            wmt = jnp.where(mask_t, gwt[h], 0.0).astype(BF16)
            cs = slice(h * GW, (h + 1) * GW)
            for n in range(TS // SGU_BLOCK):
                rs = slice(n * SGU_BLOCK, (n + 1) * SGU_BLOCK)
                vb = vn[rs, cs].astype(BF16)
                z = _dot(wm, vb) + gbias[h]
                dzb = dyd[rs, cs] * u[rs, cs]
                dz16 = dzb.astype(BF16)
                e3[rs, cs] = dyd[rs, cs] * z
                e4[rs, cs] = _dot(wmt, dz16)
                dgw[h] += jnp.where(mask, _dot_nt(dz16, vb), 0.0)
                dgb[h] += dzb
        dvn = e4[0:TS, :]
        dslg[...] += jnp.sum(dvn * vh, axis=0, keepdims=True)
        dslb[...] += jnp.sum(dvn, axis=0, keepdims=True)
        dv = _ln_bwd(dvn * slg[...], vh, vr)
        dp_ref[3, :, 0:BW] = (e3[0:TS, :] * _gelu_grad(pu, tu)).astype(BF16)
        dp_ref[3, :, BW:2 * BW] = (dv * _gelu_grad(pv, tv)).astype(BF16)

        @pl.when(i == n_t - 1)
        def _():
            for h in range(4):
                dgb[h] = jnp.broadcast_to(jnp.sum(dgb[h], axis=1, keepdims=True), dgb.shape[1:])

    params = [pool_w, pool_wt, pool_scale, sconv_w, cconv_w, cln_g, cln_b, sln_g, sln_b, sgu_w, sgu_wt, sgu_bias]
    args = [proj] * 16 + [dy] * 7 + params + [dproj_gates]
    in_specs = [back(0, 0), main(0, 0), back(0, 1), main(0, 1), main(1, 0), front(1, 0), back(1, 1), main(1, 1),
                back(2, 0), main(2, 0), front(2, 0), back(2, 1), main(2, 1), front(2, 1), main(3, 0), main(3, 1),
                main(0, 0), front(0, 0), main(1, 0), front(1, 0), main(2, 0), front(2, 0), main(3, 0)]
    in_specs += [full(a) for a in params] + [ANY]
    small = [SDS(pool_w.shape, F32), SDS(pool_scale.shape, F32), SDS(sconv_w.shape, F32), SDS(cconv_w.shape, F32),
             SDS(cln_g.shape, F32), SDS(cln_b.shape, F32), SDS(sln_g.shape, F32), SDS(sln_b.shape, F32),
             SDS(sgu_w.shape, F32), SDS(sgu_bias.shape, F32)]
    out_specs = [BS((4, TS, D), lambda i: (0, i, 0))] + [full(s) for s in small]
    return pl.pallas_call(
        body, name=name, grid=(n_t,),
        in_specs=in_specs, out_specs=out_specs,
        out_shape=[SDS(dproj_gates.shape, BF16)] + small,
        input_output_aliases={len(args) - 1: 0},
        scratch_shapes=[pltpu.VMEM((TS + 2 * H, BW), F32)] * 6,
        compiler_params=_params(("arbitrary",)),
    )(*args)


def _merge_fwd(y, proj, w_up, w_out, x, name):
    _, S, BW = y.shape
    D = x.shape[1]
    tm = _tile(S, 256, 16)

    def body(y_ref, pg_ref, wu_ref, wo_ref, x_ref, o_ref, m_ref):
        merged = None
        for g in range(4):
            t = _sig(pg_ref[g]) * _dot(y_ref[g], wu_ref[g])
            merged = t if merged is None else merged + t
        m16 = merged.astype(BF16)
        m_ref[...] = m16
        o_ref[...] = x_ref[...] + _dot(m16, wo_ref[...])

    return pl.pallas_call(
        body, name=name, grid=(S // tm,),
        in_specs=[BS((4, tm, BW), lambda i: (0, i, 0)), BS((4, tm, D), lambda i: (1, i, 0)),
                  BS((4, BW, D), lambda i: (0, 0, 0)), BS((D, D), lambda i: (0, 0)), BS((tm, D), lambda i: (i, 0))],
        out_specs=[BS((tm, D), lambda i: (i, 0)), BS((tm, D), lambda i: (i, 0))],
        out_shape=[SDS((S, D), F32), SDS((S, D), BF16)],
        compiler_params=_params(("parallel",)),
    )(y, proj, w_up, w_out, x)


def _merge_bwd(dx, y, proj, w_up, w_out, name):
    _, S, BW = y.shape
    D = dx.shape[1]
    tm = _tile(S, 256, 16)

    def body(dx_ref, y_ref, pg_ref, wu_ref, wo_ref, dup_ref, dp_ref, dy_ref):
        dm = _dot_nt(dx_ref[...].astype(BF16), wo_ref[...])
        for g in range(4):
            gate = _sig(pg_ref[g])
            up = _dot(y_ref[g], wu_ref[g])
            dup = (dm * gate).astype(BF16)
            dup_ref[g] = dup
            dp_ref[g] = (dm * up * (gate * (1.0 - gate))).astype(BF16)
            dy_ref[g] = _dot_nt(dup, wu_ref[g])

    return pl.pallas_call(
        body, name=name, grid=(S // tm,),
        in_specs=[BS((tm, D), lambda i: (i, 0)), BS((4, tm, BW), lambda i: (0, i, 0)),
                  BS((4, tm, D), lambda i: (1, i, 0)), BS((4, BW, D), lambda i: (0, 0, 0)),
                  BS((D, D), lambda i: (0, 0))],
        out_specs=[BS((4, tm, D), lambda i: (0, i, 0)), BS((4, tm, D), lambda i: (1, i, 0)),
                   BS((4, tm, BW), lambda i: (0, i, 0))],
        out_shape=[SDS((4, S, D), BF16), SDS((8, S, D), BF16), SDS((4, S, BW), F32)],
        compiler_params=_params(("parallel",)),
    )(dx, y, proj, w_up, w_out)


def _adamw(w, g, m, v):
    m = ADAM_B1 * m + (1.0 - ADAM_B1) * g
    v = ADAM_B2 * v + (1.0 - ADAM_B2) * (g * g)
    m_hat = m / (1.0 - ADAM_B1 ** ADAM_STEP)
    v_hat = v / (1.0 - ADAM_B2 ** ADAM_STEP)
    delta = -ADAM_LR * (m_hat / (jnp.sqrt(v_hat) + ADAM_EPS) + ADAM_WD * w)
    return delta, m, v


def _pair_sum(g, recv, c_idx, name):
    l, recv = recv
    _, R, C = g.shape
    tr = _tile(R, 512, 16)

    def body(c_ref, g_ref, r_ref, o_ref):
        o_ref[0] = (g_ref[0].astype(F32) + r_ref[0, 0].astype(F32)).astype(BF16)

    grid_spec = pltpu.PrefetchScalarGridSpec(
        num_scalar_prefetch=1, grid=(4, R // tr),
        in_specs=[BS((1, tr, C), lambda q, r, c: (2 * q + c[0], r, 0)),
                  BS((1, 1, tr, C), lambda q, r, c: (q, l, r, 0))],
        out_specs=BS((1, tr, C), lambda q, r, c: (q, r, 0)))
    return pl.pallas_call(
        body, name=name, grid_spec=grid_spec, out_shape=SDS((4, R, C), BF16),
        compiler_params=_params(("parallel", "parallel")),
    )(c_idx, g, recv)


def _adamw_sharded(own, recv, w, m, v, q_idx, name):
    L, R, C = w.shape
    tr = _tile(R, 256, 16)
    n_own = len(own)

    def body(q_ref, *refs):
        own_refs = refs[:n_own]
        r_ref, w_ref, m_ref, v_ref, g_out, d_out, m_out, v_out = refs[n_own:]
        l = pl.program_id(0)
        o = own_refs[0][0].astype(F32)
        for j in range(1, n_own):
            o = jnp.where(l == j, own_refs[j][0].astype(F32), o)
        g = o + r_ref[0, 0].astype(F32) + r_ref[1, 0].astype(F32) + r_ref[2, 0].astype(F32)
        d, mn, vn = _adamw(w_ref[0], g, m_ref[0], v_ref[0])
        g_out[0] = g
        d_out[0] = d
        m_out[0] = mn
        v_out[0] = vn

    blk = BS((1, tr, C), lambda l, r, q: (l, r, 0))
    grid_spec = pltpu.PrefetchScalarGridSpec(
        num_scalar_prefetch=1, grid=(L, R // tr),
        in_specs=[BS((1, tr, C), lambda l, r, q: (q[0], r, 0))] * n_own
        + [BS((3, 1, tr, C), lambda l, r, q: (0, l, r, 0)), blk, blk, blk],
        out_specs=[blk] * 4)
    return pl.pallas_call(
        body, name=name, grid_spec=grid_spec, out_shape=[SDS((L, R, C), F32)] * 4,
        compiler_params=_params(("parallel", "parallel")),
    )(q_idx, *own, recv, w, m, v)


def _adamw_replicated(gathered, layout, wmv, name):
    n_b = len(gathered)
    n_p = len(layout)

    def body(*refs):
        bufs = refs[:n_b]
        prm = refs[n_b:n_b + 3 * n_p]
        outs = refs[n_b + 3 * n_p:n_b + 7 * n_p]
        sums = refs[n_b + 7 * n_p:]
        for b in range(n_b):
            s = bufs[b][0]
            for d in range(1, N_DEV):
                s = s + bufs[b][d]
            sums[b][...] = s
        for p, (b, r0, nr) in enumerate(layout):
            g = sums[b][r0:r0 + nr, :]
            d, mn, vn = _adamw(prm[3 * p][...], g, prm[3 * p + 1][...], prm[3 * p + 2][...])
            outs[4 * p][...] = g
            outs[4 * p + 1][...] = d
            outs[4 * p + 2][...] = mn
            outs[4 * p + 3][...] = vn

    flat = [a for t in wmv for a in t]
    out_shape = []
    for (w, _, _) in wmv:
        out_shape += [SDS(w.shape, F32)] * 4
    out_shape += [SDS(g.shape[1:], F32) for g in gathered]
    return pl.pallas_call(
        body, name=name, out_shape=out_shape,
        compiler_params=pltpu.CompilerParams(vmem_limit_bytes=V7X_VMEM_LIMIT),
    )(*gathered, *flat)


def _adamw_small(g, w, m, v, name):
    def body(g_ref, w_ref, m_ref, v_ref, d_out, m_out, v_out):
        d, mn, vn = _adamw(w_ref[...], g_ref[...], m_ref[...], v_ref[...])
        d_out[...] = d
        m_out[...] = mn
        v_out[...] = vn

    return pl.pallas_call(body, name=name, out_shape=[SDS(w.shape, F32)] * 3)(g, w, m, v)


def _pad_rows(a, rows):
    return jnp.pad(a, ((0, rows - a.shape[0]), (0, 0)))


def kernel(x, ffn1_norm, ffn1_w13, ffn1_w2, mix_norm, w_in, pool_w, pool_scale, sconv_w, cconv_w, cconv_ln_g, cconv_ln_b, sgu_ln_g, sgu_ln_b, sgu_w, sgu_b, w_up, w_out, ffn2_norm, ffn2_w13, ffn2_w2, final_norm, loss_target, m_ffn1_norm, m_ffn1_w13, m_ffn1_w2, m_mix_norm, m_w_in, m_pool_w, m_pool_scale, m_sconv_w, m_cconv_w, m_cconv_ln_g, m_cconv_ln_b, m_sgu_ln_g, m_sgu_ln_b, m_sgu_w, m_sgu_b, m_w_up, m_w_out, m_ffn2_norm, m_ffn2_w13, m_ffn2_w2, m_final_norm, v_ffn1_norm, v_ffn1_w13, v_ffn1_w2, v_mix_norm, v_w_in, v_pool_w, v_pool_scale, v_sconv_w, v_cconv_w, v_cconv_ln_g, v_cconv_ln_b, v_sgu_ln_g, v_sgu_ln_b, v_sgu_w, v_sgu_b, v_w_up, v_w_out, v_ffn2_norm, v_ffn2_w13, v_ffn2_w2, v_final_norm):
    P = dict(locals())
    L = ffn1_norm.shape[0]
    S, D = x.shape[1], x.shape[2]
    BW = D // 2
    GW = BW // 4
    F = ffn1_w2.shape[1] * N_DEV
    fs = ffn1_w13.shape[2]
    cw = sconv_w.shape[2]
    ux = lax.axis_index("x")
    uy = lax.axis_index("y")
    uc = lax.axis_index("c")
    me = 4 * ux + 2 * uy + uc
    c_idx = jnp.reshape(uc, (1,)).astype(jnp.int32)
    q_idx = jnp.reshape(2 * ux + uy, (1,)).astype(jnp.int32)

    big = ["ffn1_w13", "ffn1_w2", "w_in", "w_up", "w_out", "ffn2_w13", "ffn2_w2"]
    conv_local = jnp.concatenate([sconv_w, cconv_w], axis=1)
    gathered = _all_gather([P[n].astype(BF16) for n in big] + [conv_local], [L] * 7 + [None], "gather_weights")
    gw = {n: gathered[L * t:L * (t + 1)] for t, n in enumerate(big)}
    conv_full = jnp.transpose(gathered[-1], (1, 2, 0, 3)).reshape(L, SCONV_K + CCONV_K, N_DEV * cw)
    sconv_full = conv_full[:, :SCONV_K]
    cconv_full = conv_full[:, SCONV_K:]

    def w13_of(g):
        return jnp.transpose(g.reshape(2, 4, D, fs), (0, 2, 1, 3)).reshape(2, D, F)

    def w_up_of(g):
        return jnp.transpose(g, (1, 2, 0, 3)).reshape(4, BW, D)

    W = []
    for l in range(L):
        W.append(dict(
            ffn1_w13=w13_of(gw["ffn1_w13"][l]), ffn1_w2=gw["ffn1_w2"][l].reshape(F, D),
            w_in=gw["w_in"][l], w_up=w_up_of(gw["w_up"][l]), w_out=gw["w_out"][l].reshape(D, D),
            ffn2_w13=w13_of(gw["ffn2_w13"][l]), ffn2_w2=gw["ffn2_w2"][l].reshape(F, D)))

    sgu_bias = jnp.broadcast_to(sgu_b[:, :, :, None], sgu_b.shape + (GW,))
    pool_wt = jnp.swapaxes(pool_w, 2, 3)
    sgu_wt = jnp.swapaxes(sgu_w, 2, 3)

    def row(a, l):
        return a[l][None, :]

    saved = []
    xc = x[0]
    for l in range(L):
        sv = {}
        for tag in ("ffn1", None, "ffn2"):
            if tag is None:
                sv["x_mix"] = xc
                h = _rmsnorm_fwd(xc, row(mix_norm, l), "mix_norm_fwd")
                proj = _matmul_fwd(h, W[l]["w_in"], "proj_fwd")
                y = _mixers_fwd(proj, pool_w[l], row(pool_scale, l), sconv_full[l], cconv_full[l],
                                row(cconv_ln_g, l), row(cconv_ln_b, l), row(sgu_ln_g, l), row(sgu_ln_b, l),
                                sgu_w[l], sgu_bias[l], "mixers_fwd")
                xc, merged = _merge_fwd(y, proj, W[l]["w_up"], W[l]["w_out"], xc, "merge_fwd")
                sv.update(h_mix=h, proj=proj, y=y, merged=merged)
            else:
                sv["x_" + tag] = xc
                h = _rmsnorm_fwd(xc, row(P[tag + "_norm"], l), "ffn_norm_fwd")
                ab = _matmul_fwd(h, W[l][tag + "_w13"], "ffn_up_fwd")
                xc = _swiglu_down(ab, W[l][tag + "_w2"], xc, "ffn_down_fwd")
                sv.update({"h_" + tag: h, "ab_" + tag: ab})
        saved.append(sv)

    loss_part, dx, d_final = _final_loss(xc, final_norm[None, :], loss_target[0], "loss_head")
    loss = lax.psum(loss_part[0, 0], MESH_AXES)

    G = {n: [None] * L for n in big}
    small_g = [None] * L
    for l in reversed(range(L)):
        sv = saved[l]
        sg = {}
        for tag in ("ffn2", None, "ffn1"):
            if tag is None:
                dup, dproj, dy = _merge_bwd(dx, sv["y"], sv["proj"], W[l]["w_up"], W[l]["w_out"], "merge_bwd")
                g_out = _matmul_tn(sv["merged"][None], dx[None], 1, "w_out_grad")
                g_up = _matmul_tn(sv["y"], dup, 1, "w_up_grad")
                res = _mixers_bwd(sv["proj"], dy, dproj, pool_w[l], pool_wt[l], row(pool_scale, l), sconv_full[l],
                                  cconv_full[l], row(cconv_ln_g, l), row(cconv_ln_b, l), row(sgu_ln_g, l),
                                  row(sgu_ln_b, l), sgu_w[l], sgu_wt[l], sgu_bias[l], "mixers_bwd")
                dproj = res[0]
                (sg["pool_w"], sg["pool_scale"], sg["sconv_w"], sg["cconv_w"], sg["cconv_ln_g"], sg["cconv_ln_b"],
                 sg["sgu_ln_g"], sg["sgu_ln_b"], sg["sgu_w"], dgb) = res[1:]
                sg["sgu_b"] = dgb[:, :, 0]
                g_in = _matmul_tn(sv["h_mix"][None], dproj, N_DEV, "w_in_grad")
                dx, sg["mix_norm"] = _matmul_nt_normbwd(dproj, W[l]["w_in"], sv["x_mix"], row(mix_norm, l), dx,
                                                        "proj_bwd")
                G["w_out"][l] = g_out.reshape(N_DEV, D // N_DEV, D)
                G["w_up"][l] = jnp.transpose(g_up.reshape(4, BW, N_DEV, D // N_DEV), (2, 0, 1, 3)).reshape(
                    N_DEV, 4 * BW, D // N_DEV)
                G["w_in"][l] = g_in
            else:
                dab, sh = _ffn_bwd_hidden(dx, W[l][tag + "_w2"], sv["ab_" + tag], "ffn_hidden_bwd")
                g_w2 = _matmul_tn(sh[None], dx[None], 1, "ffn_w2_grad")
                g_w13 = _matmul_tn(sv["h_" + tag][None], dab, 2, "ffn_w13_grad")
                dx, sg[tag + "_norm"] = _matmul_nt_normbwd(dab, W[l][tag + "_w13"], sv["x_" + tag],
                                                           row(P[tag + "_norm"], l), dx, "ffn_up_bwd")
                G[tag + "_w2"][l] = g_w2.reshape(N_DEV, F // N_DEV, D)
                G[tag + "_w13"][l] = jnp.transpose(g_w13.reshape(2, D, 4, fs), (0, 2, 1, 3)).reshape(N_DEV, D, fs)
        small_g[l] = sg
    grad_x = dx[None]

    from_sibling = _sibling_exchange([G[n] for n in big], "grads_to_sibling")
    chip_sums = []
    for t, n in enumerate(big):
        chip_sums.append([_pair_sum(G[n][l], (l, from_sibling[t]), c_idx, "grad_pair_sum") for l in range(L)])
    stacked = [jnp.stack(cs, axis=1) for cs in chip_sums]
    from_chips = _chip_exchange(stacked, "grads_to_owner")
    out = {}
    for t, n in enumerate(big):
        shp = P[n].shape
        R, C = math.prod(shp[1:-1]), shp[-1]
        flat = lambda a: a.reshape(L, R, C)
        g, d, mn, vn = _adamw_sharded(chip_sums[t], from_chips[t], flat(P[n]), flat(P["m_" + n]), flat(P["v_" + n]),
                                      q_idx, "adamw_sharded")
        out[n] = tuple(a.reshape(shp) for a in (g, d, mn, vn))

    wide = ["ffn1_norm", "mix_norm", "ffn2_norm", "final_norm"]
    half = ["pool_scale", "cconv_ln_g", "cconv_ln_b", "sgu_ln_g", "sgu_ln_b"]
    narrow = ["pool_w", "sgu_w", "sgu_b"]

    def stack_layers(n):
        return jnp.stack([small_g[l][n] for l in range(L)], axis=0)

    def as2d(n, a):
        if n == "final_norm":
            return a.reshape(1, D)
        return a.reshape(-1, a.shape[-1])

    sg2 = {n: stack_layers(n).reshape(-1, stack_layers(n).shape[-1]) for n in wide[:3] + half + narrow}
    sg2["final_norm"] = d_final
    conv_g = jnp.concatenate([stack_layers("sconv_w"), stack_layers("cconv_w")], axis=1)
    conv_g = conv_g.reshape(L * (SCONV_K + CCONV_K), N_DEV * cw)
    small_names = wide + half + narrow
    widths = []
    for n in small_names:
        if sg2[n].shape[1] not in widths:
            widths.append(sg2[n].shape[1])
    layout = {}
    bufs = []
    for b, width in enumerate(widths):
        parts = []
        r0 = 0
        for n in small_names:
            if sg2[n].shape[1] != width:
                continue
            nr = sg2[n].shape[0]
            pr = -(-nr // 8) * 8
            layout[n] = (b, r0, nr)
            parts.append(_pad_rows(sg2[n], pr))
            r0 += pr
        if width == conv_g.shape[1]:
            conv_b, conv_r0 = b, r0
            parts.append(_pad_rows(conv_g, -(-conv_g.shape[0] // 8) * 8))
        bufs.append(jnp.concatenate(parts, axis=0))
    gathered_small = _all_gather(bufs, [None] * len(bufs), "gather_small_grads")
    res = _adamw_replicated(gathered_small, [layout[n] for n in small_names],
                            [(as2d(n, P[n]), as2d(n, P["m_" + n]), as2d(n, P["v_" + n])) for n in small_names],
                            "adamw_replicated")
    for p, n in enumerate(small_names):
        out[n] = tuple(a.reshape(P[n].shape) for a in res[4 * p:4 * p + 4])
    conv_sum = res[4 * len(small_names) + conv_b][conv_r0:conv_r0 + conv_g.shape[0]]
    conv_mine = lax.dynamic_slice_in_dim(conv_sum, me * cw, cw, axis=1)

    def conv2d(a, b):
        return jnp.concatenate([a, b], axis=1).reshape(L * (SCONV_K + CCONV_K), cw)

    cd, cm, cv = _adamw_small(conv_mine, conv2d(sconv_w, cconv_w), conv2d(m_sconv_w, m_cconv_w),
                              conv2d(v_sconv_w, v_cconv_w), "adamw_conv")
    for n, sl in (("sconv_w", slice(0, SCONV_K)), ("cconv_w", slice(SCONV_K, SCONV_K + CCONV_K))):
        out[n] = tuple(a.reshape(L, SCONV_K + CCONV_K, cw)[:, sl] for a in (conv_mine, cd, cm, cv))

    order = ["ffn1_norm", "ffn1_w13", "ffn1_w2", "mix_norm", "w_in", "pool_w", "pool_scale", "sconv_w", "cconv_w",
             "cconv_ln_g", "cconv_ln_b", "sgu_ln_g", "sgu_ln_b", "sgu_w", "sgu_b", "w_up", "w_out", "ffn2_norm",
             "ffn2_w13", "ffn2_w2", "final_norm"]
    return (loss, grad_x, *[out[n][0] for n in order], *[out[n][1] for n in order],
            *[out[n][2] for n in order], *[out[n][3] for n in order])
```

```python
import functools
import math

import jax
import jax.numpy as jnp
from jax import lax
from jax.experimental import pallas as pl
from jax.experimental.pallas import tpu as pltpu

F32 = jnp.float32
BF16 = jnp.bfloat16
EPS = 1e-6
ADAM_LR = 0.001
ADAM_B1 = 0.9
ADAM_B2 = 0.999
ADAM_EPS = 1e-08
ADAM_WD = 0.01
ADAM_STEP = 10
SGU_BLOCK = 128
SGU_CHUNK = 64
SCONV_K = 3
CCONV_K = 31
HALO = 32
V7X_VMEM_LIMIT = 48 * 1024 * 1024
MESH_AXES = ("x", "y", "c")
N_DEV = 8
_GELU_C0 = math.sqrt(2.0 / math.pi)
_GELU_C1 = 0.044715

BS = pl.BlockSpec
SDS = jax.ShapeDtypeStruct
ANY = pl.BlockSpec(memory_space=pl.ANY)


def _tile(n, pref, align=128):
    if n <= pref:
        return n
    t = pref - pref % align
    while t > 0:
        if n % t == 0:
            return t
        t -= align
    return n


def _sig(v):
    return 1.0 / (1.0 + jnp.exp(-v))


def _gelu(v):
    t = jnp.tanh(_GELU_C0 * (v + _GELU_C1 * (v * v * v)))
    return 0.5 * v * (1.0 + t), t


def _gelu_grad(v, t):
    return 0.5 * (1.0 + t) + 0.5 * v * (1.0 - t * t) * (_GELU_C0 * (1.0 + 3.0 * _GELU_C1 * v * v))


def _ln_stats(v):
    mu = jnp.mean(v, axis=-1, keepdims=True)
    vc = v - mu
    var = jnp.mean(vc * vc, axis=-1, keepdims=True)
    rstd = lax.rsqrt(var + EPS)
    return vc * rstd, rstd


def _ln_bwd(dvh, vh, rstd):
    return rstd * (dvh - jnp.mean(dvh, axis=-1, keepdims=True) - vh * jnp.mean(dvh * vh, axis=-1, keepdims=True))


def _dot(a, b):
    return jnp.dot(a, b, preferred_element_type=F32)


def _dot_nt(a, b):
    return lax.dot_general(a, b, (((1,), (1,)), ((), ())), preferred_element_type=F32)


def _dot_tn(a, b):
    return lax.dot_general(a, b, (((0,), (0,)), ((), ())), preferred_element_type=F32)


def _mesh_pos():
    return lax.axis_index("x"), lax.axis_index("y"), lax.axis_index("c")


class _Comm:
    def __init__(self, ins, out_shape, sems, start, finish):
        self.ins, self.out_shape, self.sems, self.start, self.finish = ins, out_shape, sems, start, finish


def _gather_comm(shards, units):
    n_u = len(units)

    def tools(ins, dsts, sems):
        send_sems, recv_sems, local_sems = sems
        x, y, c = _mesh_pos()
        chips = [(1 - x, y), (x, 1 - y), (1 - x, 1 - y)]

        def src_of(o):
            t, l = units[o]
            return ins[t] if l is None else ins[t].at[l]

        def row(o, p):
            return dsts[o].at[4 * p[0] + 2 * p[1] + p[2]]

        def copy(o, k, block, to, own=False):
            return pltpu.make_async_remote_copy(
                src_ref=src_of(o) if own else row(o, block), dst_ref=row(o, block),
                send_sem=send_sems.at[o * 7 + k], recv_sem=recv_sems.at[o * 7 + k],
                device_id=to, device_id_type=pl.DeviceIdType.MESH)

        def local(o):
            return pltpu.make_async_copy(src_of(o), row(o, (x, y, c)), local_sems.at[o])

        def first(o):
            return [copy(o, 1 + j, (x, y, c), (*chip, c), own=True) for j, chip in enumerate(chips)] + [
                copy(o, 0, (x, y, c), (x, y, 1 - c), own=True)]

        return (x, y, c), chips, copy, local, first

    def start(ins, dsts, sems):
        _, _, _, local, first = tools(ins, dsts, sems)
        for o in range(n_u):
            local(o).start()
            for cp in first(o):
                cp.start()

    def finish(ins, dsts, sems):
        (x, y, c), chips, copy, local, first = tools(ins, dsts, sems)
        me, sibling = (x, y, c), (x, y, 1 - c)
        passed = []
        for o in range(n_u):
            for j, chip in enumerate(chips):
                copy(o, 1 + j, (*chip, c), me).wait_recv()
                cp = copy(o, 4 + j, (*chip, c), sibling)
                cp.start()
                passed.append(cp)
        for o in range(n_u):
            copy(o, 0, sibling, me).wait_recv()
            for j, chip in enumerate(chips):
                copy(o, 4 + j, (*chip, 1 - c), me).wait_recv()
        for o in range(n_u):
            for cp in first(o):
                cp.wait_send()
        for cp in passed:
            cp.wait_send()
        for o in range(n_u):
            local(o).wait()

    out_shape = []
    for t, l in units:
        shp = shards[t].shape if l is None else shards[t].shape[1:]
        out_shape.append(SDS((N_DEV,) + tuple(shp), shards[t].dtype))
    sems = [pltpu.SemaphoreType.DMA((7 * n_u,)), pltpu.SemaphoreType.DMA((7 * n_u,)), pltpu.SemaphoreType.DMA((n_u,))]
    return _Comm(list(shards), out_shape, sems, start, finish)


def _scatter_comm(parts):
    n_u = len(parts)

    def tools(ins, dsts, sems):
        send_sems, recv_sems, local_sems = sems
        x, y, c = _mesh_pos()
        me = 4 * x + 2 * y + c

        def peer(k):
            return ((x + ((k >> 2) & 1)) % 2, (y + ((k >> 1) & 1)) % 2, (c + (k & 1)) % 2)

        def copy(u, k, wait=False):
            p = peer(k)
            pi = 4 * p[0] + 2 * p[1] + p[2]
            return pltpu.make_async_remote_copy(
                src_ref=ins[u].at[pi], dst_ref=dsts[u].at[pi if wait else me],
                send_sem=send_sems.at[u * 7 + k - 1], recv_sem=recv_sems.at[u * 7 + k - 1],
                device_id=p, device_id_type=pl.DeviceIdType.MESH)

        def local(u):
            return pltpu.make_async_copy(ins[u].at[me], dsts[u].at[me], local_sems.at[u])

        return copy, local

    def start(ins, dsts, sems):
        copy, local = tools(ins, dsts, sems)
        for u in range(n_u):
            local(u).start()
            for k in range(1, N_DEV):
                copy(u, k).start()

    def finish(ins, dsts, sems):
        copy, local = tools(ins, dsts, sems)
        for u in range(n_u):
            for k in range(1, N_DEV):
                copy(u, k, wait=True).wait()
            local(u).wait()

    sems = [pltpu.SemaphoreType.DMA((7 * n_u,)), pltpu.SemaphoreType.DMA((7 * n_u,)), pltpu.SemaphoreType.DMA((n_u,))]
    return _Comm(list(parts), [SDS(p.shape, p.dtype) for p in parts], sems, start, finish)


def _run_comm(comm, name):
    n_i, n_o = len(comm.ins), len(comm.out_shape)

    def body(*refs):
        comm.start(refs[:n_i], refs[n_i:n_i + n_o], refs[n_i + n_o:])
        comm.finish(refs[:n_i], refs[n_i:n_i + n_o], refs[n_i + n_o:])

    return pl.pallas_call(
        body, name=name, out_shape=comm.out_shape, in_specs=[ANY] * n_i, out_specs=[ANY] * n_o,
        scratch_shapes=comm.sems,
    )(*comm.ins)


def _pcall(body, name, grid, in_specs, out_specs, out_shape, scratch, sem, args, comm=None):
    n_i, n_o, n_s = len(in_specs), len(out_specs), len(scratch)
    if comm is None:
        res = pl.pallas_call(
            body, name=name, grid=grid, in_specs=in_specs, out_specs=out_specs, out_shape=out_shape,
            scratch_shapes=scratch,
            compiler_params=pltpu.CompilerParams(dimension_semantics=sem, vmem_limit_bytes=V7X_VMEM_LIMIT),
        )(*args)
        return res, []
    n_ci, n_co = len(comm.ins), len(comm.out_shape)

    def wrapped(*refs):
        ins = refs[:n_i]
        cins = refs[n_i:n_i + n_ci]
        outs = refs[n_i + n_ci:n_i + n_ci + n_o]
        couts = refs[n_i + n_ci + n_o:n_i + n_ci + n_o + n_co]
        rest = refs[n_i + n_ci + n_o + n_co:]
        ids = [pl.program_id(d) for d in range(len(grid))]
        first = functools.reduce(jnp.logical_and, [i == 0 for i in ids])
        last = functools.reduce(jnp.logical_and, [i == g - 1 for i, g in zip(ids, grid)])

        @pl.when(first)
        def _():
            comm.start(cins, couts, rest[n_s:])

        body(*ins, *outs, *rest[:n_s])

        @pl.when(last)
        def _():
            comm.finish(cins, couts, rest[n_s:])

    res = pl.pallas_call(
        wrapped, name=name, grid=grid, in_specs=list(in_specs) + [ANY] * n_ci,
        out_specs=list(out_specs) + [ANY] * n_co, out_shape=list(out_shape) + list(comm.out_shape),
        scratch_shapes=list(scratch) + list(comm.sems),
        compiler_params=pltpu.CompilerParams(dimension_semantics=("arbitrary",) * len(grid),
                                             vmem_limit_bytes=V7X_VMEM_LIMIT),
    )(*args, *comm.ins)
    return res[:n_o], res[n_o:]


def _rmsnorm_fwd(x, g, name):
    S, D = x.shape
    tm = _tile(S, 512, 16)

    def body(x_ref, g_ref, h_ref):
        xv = x_ref[...]
        r = lax.rsqrt(jnp.mean(xv * xv, axis=-1, keepdims=True) + EPS)
        h_ref[...] = (xv * r * g_ref[...]).astype(BF16)

    (h,), _ = _pcall(body, name, (S // tm,), [BS((tm, D), lambda i: (i, 0)), BS((1, D), lambda i: (0, 0))],
                     [BS((tm, D), lambda i: (i, 0))], [SDS((S, D), BF16)], [], ("parallel",), (x, g))
    return h


def _matmul_fwd(a, w, name, comm=None):
    S, K = a.shape
    C, _, Fc = w.shape
    tn = _tile(Fc, 1408)
    tm = _tile(S, 512, 16)

    def body(a_ref, w_ref, o_ref):
        o_ref[0] = _dot(a_ref[...], w_ref[0])

    (o,), extra = _pcall(
        body, name, (C, Fc // tn, S // tm),
        [BS((tm, K), lambda c, n, i: (i, 0)), BS((1, K, tn), lambda c, n, i: (c, 0, n))],
        [BS((1, tm, tn), lambda c, n, i: (c, i, n))], [SDS((C, S, Fc), F32)], [],
        ("parallel", "parallel", "parallel"), (a, w), comm)
    return o, extra


def _swiglu_down(ab, w2, x, name, comm=None):
    _, S, F = ab.shape
    D = w2.shape[1]
    tk = _tile(F, 1408)
    tm = _tile(S, 512, 16)

    def body(ab_ref, w_ref, x_ref, o_ref):
        k = pl.program_id(1)
        a = ab_ref[0]
        s = a * _sig(a) * ab_ref[1]
        p = 0.5 * _dot(s.astype(BF16), w_ref[...])

        @pl.when(k == 0)
        def _():
            o_ref[...] = x_ref[...] + p

        @pl.when(k > 0)
        def _():
            o_ref[...] += p

    (o,), extra = _pcall(
        body, name, (S // tm, F // tk),
        [BS((2, tm, tk), lambda i, k: (0, i, k)), BS((tk, D), lambda i, k: (k, 0)), BS((tm, D), lambda i, k: (i, 0))],
        [BS((tm, D), lambda i, k: (i, 0))], [SDS((S, D), F32)], [], ("parallel", "arbitrary"), (ab, w2, x), comm)
    return o, extra


def _ffn_bwd_hidden(dy, w2, ab, name):
    S, D = dy.shape
    F = w2.shape[0]
    tk = _tile(F, 1408)
    tm = _tile(S, 256, 16)

    def body(dy_ref, w_ref, ab_ref, dab_ref, s_ref):
        ds = 0.5 * _dot_nt(dy_ref[...].astype(BF16), w_ref[...])
        a = ab_ref[0]
        b = ab_ref[1]
        sg = _sig(a)
        sa = a * sg
        dab_ref[0] = (ds * b * (sg * (1.0 + a * (1.0 - sg)))).astype(BF16)
        dab_ref[1] = (ds * sa).astype(BF16)
        s_ref[...] = (0.5 * (sa * b)).astype(BF16)

    (dab, sh), _ = _pcall(
        body, name, (F // tk, S // tm),
        [BS((tm, D), lambda k, i: (i, 0)), BS((tk, D), lambda k, i: (k, 0)), BS((2, tm, tk), lambda k, i: (0, i, k))],
        [BS((2, tm, tk), lambda k, i: (0, i, k)), BS((tm, tk), lambda k, i: (i, k))],
        [SDS((2, S, F), BF16), SDS((S, F), BF16)], [], ("parallel", "parallel"), (dy, w2, ab))
    return dab, sh


def _matmul_tn(a, b, n_c, name, comm=None):
    G, S, M = a.shape
    _, _, Fc = b.shape
    C = n_c
    tM = _tile(M, 1408)
    tn = _tile(Fc, 1408)
    ts = _tile(S, 512, 16)
    n_s = S // ts

    def body(a_ref, b_ref, o_ref, acc):
        s = pl.program_id(4)
        p = _dot_tn(a_ref[0].astype(BF16), b_ref[0].astype(BF16))

        @pl.when(s == 0)
        def _():
            acc[...] = p

        @pl.when(s > 0)
        def _():
            acc[...] += p

        @pl.when(s == n_s - 1)
        def _():
            o_ref[0] = acc[...].astype(BF16)

    (o,), extra = _pcall(
        body, name, (G, M // tM, C, Fc // tn, n_s),
        [BS((1, ts, tM), lambda g, m, c, n, s: (g, s, m)), BS((1, ts, tn), lambda g, m, c, n, s: (g * C + c, s, n))],
        [BS((1, tM, tn), lambda g, m, c, n, s: (g * C + c, m, n))], [SDS((G * C, M, Fc), BF16)],
        [pltpu.VMEM((tM, tn), F32)], ("parallel", "parallel", "parallel", "parallel", "arbitrary"), (a, b), comm)
    return o, extra


def _matmul_nt_normbwd(b, w, x, gam, dres, name, comm=None):
    C, S, Fc = b.shape
    D = w.shape[1]
    tk = _tile(Fc, 1408)
    tm = _tile(S, 512, 16)
    nk = Fc // tk

    def body(b_ref, w_ref, x_ref, g_ref, r_ref, dx_ref, dg_ref, acc):
        i, c, k = pl.program_id(0), pl.program_id(1), pl.program_id(2)
        p = _dot_nt(b_ref[0], w_ref[0])
        first = jnp.logical_and(c == 0, k == 0)

        @pl.when(first)
        def _():
            acc[...] = p

        @pl.when(jnp.logical_not(first))
        def _():
            acc[...] += p

        @pl.when(jnp.logical_and(c == C - 1, k == nk - 1))
        def _():
            xv = x_ref[...]
            r = lax.rsqrt(jnp.mean(xv * xv, axis=-1, keepdims=True) + EPS)
            xn = xv * r
            dh = acc[...]
            dxn = dh * g_ref[...]
            dx_ref[...] = r_ref[...] + r * (dxn - xn * jnp.mean(dxn * xn, axis=-1, keepdims=True))
            dgp = jnp.sum(dh * xn, axis=0, keepdims=True)

            @pl.when(i == 0)
            def _():
                dg_ref[...] = dgp

            @pl.when(i > 0)
            def _():
                dg_ref[...] += dgp

    (dx, dg), extra = _pcall(
        body, name, (S // tm, C, nk),
        [BS((1, tm, tk), lambda i, c, k: (c, i, k)), BS((1, D, tk), lambda i, c, k: (c, 0, k)),
         BS((tm, D), lambda i, c, k: (i, 0)), BS((1, D), lambda i, c, k: (0, 0)), BS((tm, D), lambda i, c, k: (i, 0))],
        [BS((tm, D), lambda i, c, k: (i, 0)), BS((1, D), lambda i, c, k: (0, 0))],
        [SDS((S, D), F32), SDS((1, D), F32)], [pltpu.VMEM((tm, D), F32)],
        ("arbitrary", "arbitrary", "arbitrary"), (b, w, x, gam, dres), comm)
    return dx, dg, extra


def _final_loss(x, gam, target, name):
    S, D = x.shape
    tm = _tile(S, 512, 8)

    def body(x_ref, g_ref, t_ref, loss_ref, dx_ref, dg_ref):
        i = pl.program_id(0)
        xv = x_ref[...]
        r = lax.rsqrt(jnp.mean(xv * xv, axis=-1, keepdims=True) + EPS)
        xn = xv * r
        err = xn * g_ref[...] - t_ref[...]
        part = 0.5 * jnp.sum(jnp.mean(err * err, axis=-1, keepdims=True), axis=0, keepdims=True)
        dy = err * (1.0 / D)
        dxn = dy * g_ref[...]
        dx_ref[...] = r * (dxn - xn * jnp.mean(dxn * xn, axis=-1, keepdims=True))
        dgp = jnp.sum(dy * xn, axis=0, keepdims=True)
        lp = jnp.broadcast_to(part, loss_ref.shape)

        @pl.when(i == 0)
        def _():
            dg_ref[...] = dgp
            loss_ref[...] = lp

        @pl.when(i > 0)
        def _():
            dg_ref[...] += dgp
            loss_ref[...] += lp

    res, _ = _pcall(
        body, name, (S // tm,),
        [BS((tm, D), lambda i: (i, 0)), BS((1, D), lambda i: (0, 0)), BS((tm, D), lambda i: (i, 0))],
        [BS((8, 128), lambda i: (0, 0)), BS((tm, D), lambda i: (i, 0)), BS((1, D), lambda i: (0, 0))],
        [SDS((8, 128), F32), SDS((S, D), F32), SDS((1, D), F32)], [], ("arbitrary",), (x, gam, target))
    return res


def _sgu_masks():
    ii = lax.broadcasted_iota(jnp.int32, (SGU_BLOCK, SGU_BLOCK), 0) // SGU_CHUNK
    jj = lax.broadcasted_iota(jnp.int32, (SGU_BLOCK, SGU_BLOCK), 1) // SGU_CHUNK
    return jj <= ii, ii <= jj


def _mixers_fwd(proj, pool_w, pool_scale, sconv_w, cconv_w, cln_g, cln_b, sln_g, sln_b, sgu_w, sgu_bias, name):
    _, S, D = proj.shape
    BW = D // 2
    GW = BW // 4
    TS = _tile(S, 256, SGU_BLOCK)
    H = HALO
    hb = TS // H

    def main(blk, col):
        return BS((1, TS, BW), lambda i: (blk, i, col))

    def back(blk, col):
        return BS((1, H, BW), lambda i: (blk, jnp.maximum(i * hb - 1, 0), col))

    def full(a):
        nd = a.ndim
        return BS(a.shape, lambda i: (0,) * nd)

    def body(pa_m, pa_b, xi_m, xi_b, bg_m, cg_m, cg_b, ca_m, ca_b, cb_m, cb_b, du_m, dv_m,
             pw, ps, sw, cw, clg, clb, slg, slb, gw, gbias, y_ref, e1, e2, e3):
        i = pl.program_id(0)
        nb = jnp.where(i > 0, 1.0, 0.0).astype(F32)
        rows = i * TS + lax.broadcasted_iota(jnp.int32, (TS, 1), 0)

        e1[0:H, :] = pa_b[0] * nb
        e1[H:H + TS, :] = pa_m[0]
        for g in range(4):
            cols = slice(g * GW, (g + 1) * GW)
            win = 2 << g
            wsum = e1[H:H + TS, cols]
            for k in range(1, win):
                wsum = wsum + e1[H - k:H - k + TS, cols]
            cnt = jnp.minimum(rows + 1, win).astype(F32)
            d = wsum / cnt - e1[H:H + TS, cols]
            yg = _dot(d.astype(BF16), pw[g].astype(BF16)) * ps[:, cols]
            y_ref[0, :, cols] = yg.astype(BF16)

        e2[0:H, :] = cg_b[0] * xi_b[0] * nb
        e2[H:H + TS, :] = cg_m[0] * xi_m[0]
        cz = sw[0:1, :] * e2[H - 2:H - 2 + TS, :]
        for k in range(1, SCONV_K):
            cz = cz + sw[k:k + 1, :] * e2[H - 2 + k:H - 2 + k + TS, :]
        y_ref[1] = (bg_m[0] * cz).astype(BF16)

        e3[0:H, :] = ca_b[0] * _sig(cb_b[0]) * nb
        e3[H:H + TS, :] = ca_m[0] * _sig(cb_m[0])
        o = H - (CCONV_K - 1)
        y1 = cw[0:1, :] * e3[o:o + TS, :]
        for k in range(1, CCONV_K):
            y1 = y1 + cw[k:k + 1, :] * e3[o + k:o + k + TS, :]
        yh, _ = _ln_stats(y1)
        y2 = yh * clg[...] + clb[...]
        y_ref[2] = (y2 * _sig(y2)).astype(BF16)

        u, _ = _gelu(du_m[0])
        v, _ = _gelu(dv_m[0])
        vh, _ = _ln_stats(v)
        vn = vh * slg[...] + slb[...]
        mask, _ = _sgu_masks()
        for h in range(4):
            wm = jnp.where(mask, gw[h], 0.0).astype(BF16)
            cs = slice(h * GW, (h + 1) * GW)
            for n in range(TS // SGU_BLOCK):
                rs = slice(n * SGU_BLOCK, (n + 1) * SGU_BLOCK)
                z = _dot(wm, vn[rs, cs].astype(BF16)) + gbias[h]
                y_ref[3, rs, cs] = (u[rs, cs] * z).astype(BF16)

    args = [proj] * 13 + [pool_w, pool_scale, sconv_w, cconv_w, cln_g, cln_b, sln_g, sln_b, sgu_w, sgu_bias]
    in_specs = [main(0, 0), back(0, 0), main(0, 1), back(0, 1), main(1, 0), main(1, 1), back(1, 1),
                main(2, 0), back(2, 0), main(2, 1), back(2, 1), main(3, 0), main(3, 1)]
    in_specs += [full(a) for a in args[13:]]
    (y,), _ = _pcall(body, name, (S // TS,), in_specs, [BS((4, TS, BW), lambda i: (0, i, 0))],
                     [SDS((4, S, BW), BF16)], [pltpu.VMEM((H + TS, BW), F32)] * 3, ("parallel",), args)
    return y


def _mixers_bwd(proj, dy, dproj_gates, pool_w, pool_wt, pool_scale, sconv_w, cconv_w, cln_g, cln_b, sln_g, sln_b,
                sgu_w, sgu_wt, sgu_bias, name):
    _, S, D = proj.shape
    BW = D // 2
    GW = BW // 4
    TS = _tile(S, 256, SGU_BLOCK)
    H = HALO
    hb = TS // H
    n_t = S // TS
    E = TS + H

    def main(blk, col):
        return BS((1, TS, BW), lambda i: (blk, i, col))

    def back(blk, col):
        return BS((1, H, BW), lambda i: (blk, jnp.maximum(i * hb - 1, 0), col))

    def front(blk, col):
        return BS((1, H, BW), lambda i: (blk, jnp.minimum((i + 1) * hb, S // H - 1), col))

    def full(a):
        nd = a.ndim
        return BS(a.shape, lambda i: (0,) * nd)

    def body(pa_b, pa_m, xi_b, xi_m, bg_m, bg_f, cg_b, cg_m, ca_b, ca_m, ca_f, cb_b, cb_m, cb_f, du_m, dv_m,
             dya_m, dya_f, dyb_m, dyb_f, dyc_m, dyc_f, dyd_m,
             pw, pwt, ps, sw, cw, clg, clb, slg, slb, gw, gwt, gbias, _gates_in,
             dp_ref, dpw, dps, dsw, dcw, dclg, dclb, dslg, dslb, dgw, dgb,
             e1, e2, e3, e4, e5, e6):
        i = pl.program_id(0)
        nb = jnp.where(i > 0, 1.0, 0.0).astype(F32)
        nf = jnp.where(i < n_t - 1, 1.0, 0.0).astype(F32)
        rows_m = i * TS + lax.broadcasted_iota(jnp.int32, (TS, 1), 0)
        rows_e = i * TS + lax.broadcasted_iota(jnp.int32, (E, 1), 0)

        @pl.when(i == 0)
        def _():
            for r in (dpw, dps, dsw, dcw, dclg, dclb, dslg, dslb, dgw, dgb):
                r[...] = jnp.zeros(r.shape, F32)

        e1[0:H, :] = pa_b[0] * nb
        e1[H:H + TS, :] = pa_m[0]
        e2[0:TS, :] = dya_m[0] * ps[...]
        e2[TS:E, :] = dya_f[0] * ps[...] * nf
        for g in range(4):
            cols = slice(g * GW, (g + 1) * GW)
            win = 2 << g
            a_m = e1[H:H + TS, cols]
            wsum = a_m
            for k in range(1, win):
                wsum = wsum + e1[H - k:H - k + TS, cols]
            d = wsum / jnp.minimum(rows_m + 1, win).astype(F32) - a_m
            d16 = d.astype(BF16)
            dyp = e2[0:E, cols].astype(BF16)
            dd = _dot(dyp, pwt[g].astype(BF16))
            e3[0:E, cols] = dd / jnp.minimum(rows_e + 1, win).astype(F32)
            da = e3[0:TS, cols] - dd[0:TS]
            for k in range(1, win):
                da = da + e3[k:k + TS, cols]
            dp_ref[0, :, cols] = da.astype(BF16)
            ypre = _dot(d16, pw[g].astype(BF16))
            dps[:, cols] += jnp.sum(dya_m[0][:, cols] * ypre, axis=0, keepdims=True)
            dpw[g] += _dot(jnp.transpose(d).astype(BF16), dyp[0:TS])

        e4[0:H, :] = cg_b[0] * xi_b[0] * nb
        e4[H:H + TS, :] = cg_m[0] * xi_m[0]
        dyb = dyb_m[0]
        e5[0:TS, :] = dyb * bg_m[0]
        e5[TS:E, :] = dyb_f[0] * bg_f[0] * nf
        dcz = e5[0:TS, :]
        cz = None
        dz = None
        for k in range(SCONV_K):
            zk = e4[H - 2 + k:H - 2 + k + TS, :]
            wk = sw[k:k + 1, :]
            cz = wk * zk if cz is None else cz + wk * zk
            t = wk * e5[2 - k:2 - k + TS, :]
            dz = t if dz is None else dz + t
            dsw[k:k + 1, :] += jnp.sum(dcz * zk, axis=0, keepdims=True)
        dp_ref[0, :, BW:2 * BW] = (dz * cg_m[0]).astype(BF16)
        dp_ref[1, :, 0:BW] = (dyb * cz).astype(BF16)
        dp_ref[1, :, BW:2 * BW] = (dz * xi_m[0]).astype(BF16)

        sgm = _sig(cb_m[0])
        e6[0:H, :] = ca_b[0] * _sig(cb_b[0]) * nb
        e6[H:H + TS, :] = ca_m[0] * sgm
        e6[H + TS:H + E, :] = ca_f[0] * _sig(cb_f[0]) * nf
        o = H - (CCONV_K - 1)
        y1 = cw[0:1, :] * e6[o:o + E, :]
        for k in range(1, CCONV_K):
            y1 = y1 + cw[k:k + 1, :] * e6[o + k:o + k + E, :]
        yh, rstd = _ln_stats(y1)
        y2 = yh * clg[...] + clb[...]
        s2 = _sig(y2)
        e1[0:TS, :] = dyc_m[0]
        e1[TS:E, :] = dyc_f[0] * nf
        dy2 = e1[0:E, :] * (s2 * (1.0 + y2 * (1.0 - s2)))
        dclg[...] += jnp.sum((dy2 * yh)[0:TS], axis=0, keepdims=True)
        dclb[...] += jnp.sum(dy2[0:TS], axis=0, keepdims=True)
        e2[0:E, :] = _ln_bwd(dy2 * clg[...], yh, rstd)
        dy1_m = e2[0:TS, :]
        dy0 = None
        for k in range(CCONV_K):
            t = cw[k:k + 1, :] * e2[CCONV_K - 1 - k:CCONV_K - 1 - k + TS, :]
            dy0 = t if dy0 is None else dy0 + t
            dcw[k:k + 1, :] += jnp.sum(dy1_m * e6[o + k:o + k + TS, :], axis=0, keepdims=True)
        dp_ref[2, :, 0:BW] = (dy0 * sgm).astype(BF16)
        dp_ref[2, :, BW:2 * BW] = (dy0 * ca_m[0] * (sgm * (1.0 - sgm))).astype(BF16)

        pu = du_m[0]
        pv = dv_m[0]
        u, tu = _gelu(pu)
        v, tv = _gelu(pv)
        vh, vr = _ln_stats(v)
        vn = vh * slg[...] + slb[...]
        dyd = dyd_m[0]
        mask, mask_t = _sgu_masks()
        for h in range(4):
            wm = jnp.where(mask, gw[h], 0.0).astype(BF16)
            wmt = jnp.where(mask_t, gwt[h], 0.0).astype(BF16)
            cs = slice(h * GW, (h + 1) * GW)
            for n in range(TS // SGU_BLOCK):
                rs = slice(n * SGU_BLOCK, (n + 1) * SGU_BLOCK)
                vb = vn[rs, cs].astype(BF16)
                z = _dot(wm, vb) + gbias[h]
                dzb = dyd[rs, cs] * u[rs, cs]
                dz16 = dzb.astype(BF16)
                e3[rs, cs] = dyd[rs, cs] * z
                e4[rs, cs] = _dot(wmt, dz16)
                dgw[h] += jnp.where(mask, _dot_nt(dz16, vb), 0.0)
                dgb[h] += dzb
        dvn = e4[0:TS, :]
        dslg[...] += jnp.sum(dvn * vh, axis=0, keepdims=True)
        dslb[...] += jnp.sum(dvn, axis=0, keepdims=True)
        dv = _ln_bwd(dvn * slg[...], vh, vr)
        dp_ref[3, :, 0:BW] = (e3[0:TS, :] * _gelu_grad(pu, tu)).astype(BF16)
        dp_ref[3, :, BW:2 * BW] = (dv * _gelu_grad(pv, tv)).astype(BF16)

        @pl.when(i == n_t - 1)
        def _():
            for h in range(4):
                dgb[h] = jnp.broadcast_to(jnp.sum(dgb[h], axis=1, keepdims=True), dgb.shape[1:])

    params = [pool_w, pool_wt, pool_scale, sconv_w, cconv_w, cln_g, cln_b, sln_g, sln_b, sgu_w, sgu_wt, sgu_bias]
    args = [proj] * 16 + [dy] * 7 + params + [dproj_gates]
    in_specs = [back(0, 0), main(0, 0), back(0, 1), main(0, 1), main(1, 0), front(1, 0), back(1, 1), main(1, 1),
                back(2, 0), main(2, 0), front(2, 0), back(2, 1), main(2, 1), front(2, 1), main(3, 0), main(3, 1),
                main(0, 0), front(0, 0), main(1, 0), front(1, 0), main(2, 0), front(2, 0), main(3, 0)]
    in_specs += [full(a) for a in params] + [ANY]
    small = [SDS(pool_w.shape, F32), SDS(pool_scale.shape, F32), SDS(sconv_w.shape, F32), SDS(cconv_w.shape, F32),
             SDS(cln_g.shape, F32), SDS(cln_b.shape, F32), SDS(sln_g.shape, F32), SDS(sln_b.shape, F32),
             SDS(sgu_w.shape, F32), SDS(sgu_bias.shape, F32)]
    out_specs = [BS((4, TS, D), lambda i: (0, i, 0))] + [full(s) for s in small]
    return pl.pallas_call(
        body, name=name, grid=(n_t,),
        in_specs=in_specs, out_specs=out_specs,
        out_shape=[SDS(dproj_gates.shape, BF16)] + small,
        input_output_aliases={len(args) - 1: 0},
        scratch_shapes=[pltpu.VMEM((TS + 2 * H, BW), F32)] * 6,
        compiler_params=pltpu.CompilerParams(dimension_semantics=("arbitrary",), vmem_limit_bytes=V7X_VMEM_LIMIT),
    )(*args)


def _merge_fwd(y, proj, w_up, w_out, x, name):
    _, S, BW = y.shape
    D = x.shape[1]
    tm = _tile(S, 256, 16)

    def body(y_ref, pg_ref, wu_ref, wo_ref, x_ref, o_ref, m_ref):
        merged = None
        for g in range(4):
            t = _sig(pg_ref[g]) * _dot(y_ref[g], wu_ref[g])
            merged = t if merged is None else merged + t
        m16 = merged.astype(BF16)
        m_ref[...] = m16
        o_ref[...] = x_ref[...] + _dot(m16, wo_ref[...])

    (o, m), _ = _pcall(
        body, name, (S // tm,),
        [BS((4, tm, BW), lambda i: (0, i, 0)), BS((4, tm, D), lambda i: (1, i, 0)),
         BS((4, BW, D), lambda i: (0, 0, 0)), BS((D, D), lambda i: (0, 0)), BS((tm, D), lambda i: (i, 0))],
        [BS((tm, D), lambda i: (i, 0)), BS((tm, D), lambda i: (i, 0))],
        [SDS((S, D), F32), SDS((S, D), BF16)], [], ("parallel",), (y, proj, w_up, w_out, x))
    return o, m


def _merge_bwd(dx, y, proj, w_up, w_out, name):
    _, S, BW = y.shape
    D = dx.shape[1]
    tm = _tile(S, 256, 16)

    def body(dx_ref, y_ref, pg_ref, wu_ref, wo_ref, dup_ref, dp_ref, dy_ref):
        dm = _dot_nt(dx_ref[...].astype(BF16), wo_ref[...])
        for g in range(4):
            gate = _sig(pg_ref[g])
            up = _dot(y_ref[g], wu_ref[g])
            dup = (dm * gate).astype(BF16)
            dup_ref[g] = dup
            dp_ref[g] = (dm * up * (gate * (1.0 - gate))).astype(BF16)
            dy_ref[g] = _dot_nt(dup, wu_ref[g])

    res, _ = _pcall(
        body, name, (S // tm,),
        [BS((tm, D), lambda i: (i, 0)), BS((4, tm, BW), lambda i: (0, i, 0)), BS((4, tm, D), lambda i: (1, i, 0)),
         BS((4, BW, D), lambda i: (0, 0, 0)), BS((D, D), lambda i: (0, 0))],
        [BS((4, tm, D), lambda i: (0, i, 0)), BS((4, tm, D), lambda i: (1, i, 0)), BS((4, tm, BW), lambda i: (0, i, 0))],
        [SDS((4, S, D), BF16), SDS((8, S, D), BF16), SDS((4, S, BW), F32)], [], ("parallel",),
        (dx, y, proj, w_up, w_out))
    return res


def _adamw(w, g, m, v):
    m = ADAM_B1 * m + (1.0 - ADAM_B1) * g
    v = ADAM_B2 * v + (1.0 - ADAM_B2) * (g * g)
    m_hat = m / (1.0 - ADAM_B1 ** ADAM_STEP)
    v_hat = v / (1.0 - ADAM_B2 ** ADAM_STEP)
    delta = -ADAM_LR * (m_hat / (jnp.sqrt(v_hat) + ADAM_EPS) + ADAM_WD * w)
    return delta, m, v


def _adamw_sharded(parts, w, m, v, name):
    L, R, C = w.shape
    tr = _tile(R, 256, 16)

    def body(*refs):
        p_refs = refs[:L]
        w_ref, m_ref, v_ref, g_out, d_out, m_out, v_out = refs[L:]
        l = pl.program_id(0)
        g = None
        for d in range(N_DEV):
            t = p_refs[0][d].astype(F32)
            for j in range(1, L):
                t = jnp.where(l == j, p_refs[j][d].astype(F32), t)
            g = t if g is None else g + t
        dl, mn, vn = _adamw(w_ref[0], g, m_ref[0], v_ref[0])
        g_out[0] = g
        d_out[0] = dl
        m_out[0] = mn
        v_out[0] = vn

    def part_spec(j):
        return BS((N_DEV, tr, C), lambda l, r: (0, jnp.where(l == j, r, 0), 0))

    blk = BS((1, tr, C), lambda l, r: (l, r, 0))
    res, _ = _pcall(body, name, (L, R // tr), [part_spec(j) for j in range(L)] + [blk, blk, blk], [blk] * 4,
                    [SDS((L, R, C), F32)] * 4, [], ("parallel", "parallel"), (*parts, w, m, v))
    return res


def _adamw_replicated(gathered, layout, wmv, name):
    n_b = len(gathered)
    n_p = len(layout)

    def body(*refs):
        bufs = refs[:n_b]
        prm = refs[n_b:n_b + 3 * n_p]
        outs = refs[n_b + 3 * n_p:n_b + 7 * n_p]
        sums = refs[n_b + 7 * n_p:]
        for b in range(n_b):
            s = bufs[b][0]
            for d in range(1, N_DEV):
                s = s + bufs[b][d]
            sums[b][...] = s
        for p, (b, r0, nr) in enumerate(layout):
            g = sums[b][r0:r0 + nr, :]
            d, mn, vn = _adamw(prm[3 * p][...], g, prm[3 * p + 1][...], prm[3 * p + 2][...])
            outs[4 * p][...] = g
            outs[4 * p + 1][...] = d
            outs[4 * p + 2][...] = mn
            outs[4 * p + 3][...] = vn

    flat = [a for t in wmv for a in t]
    out_shape = []
    for (w, _, _) in wmv:
        out_shape += [SDS(w.shape, F32)] * 4
    out_shape += [SDS(g.shape[1:], F32) for g in gathered]
    return pl.pallas_call(
        body, name=name, out_shape=out_shape,
        compiler_params=pltpu.CompilerParams(vmem_limit_bytes=V7X_VMEM_LIMIT),
    )(*gathered, *flat)


def _adamw_small(g, w, m, v, name):
    def body(g_ref, w_ref, m_ref, v_ref, d_out, m_out, v_out):
        d, mn, vn = _adamw(w_ref[...], g_ref[...], m_ref[...], v_ref[...])
        d_out[...] = d
        m_out[...] = mn
        v_out[...] = vn

    return pl.pallas_call(body, name=name, out_shape=[SDS(w.shape, F32)] * 3)(g, w, m, v)


def _pad_rows(a, rows):
    return jnp.pad(a, ((0, rows - a.shape[0]), (0, 0)))


def kernel(x, ffn1_norm, ffn1_w13, ffn1_w2, mix_norm, w_in, pool_w, pool_scale, sconv_w, cconv_w, cconv_ln_g, cconv_ln_b, sgu_ln_g, sgu_ln_b, sgu_w, sgu_b, w_up, w_out, ffn2_norm, ffn2_w13, ffn2_w2, final_norm, loss_target, m_ffn1_norm, m_ffn1_w13, m_ffn1_w2, m_mix_norm, m_w_in, m_pool_w, m_pool_scale, m_sconv_w, m_cconv_w, m_cconv_ln_g, m_cconv_ln_b, m_sgu_ln_g, m_sgu_ln_b, m_sgu_w, m_sgu_b, m_w_up, m_w_out, m_ffn2_norm, m_ffn2_w13, m_ffn2_w2, m_final_norm, v_ffn1_norm, v_ffn1_w13, v_ffn1_w2, v_mix_norm, v_w_in, v_pool_w, v_pool_scale, v_sconv_w, v_cconv_w, v_cconv_ln_g, v_cconv_ln_b, v_sgu_ln_g, v_sgu_ln_b, v_sgu_w, v_sgu_b, v_w_up, v_w_out, v_ffn2_norm, v_ffn2_w13, v_ffn2_w2, v_final_norm):
    P = dict(locals())
    L = ffn1_norm.shape[0]
    S, D = x.shape[1], x.shape[2]
    BW = D // 2
    GW = BW // 4
    F = ffn1_w2.shape[1] * N_DEV
    fs = ffn1_w13.shape[2]
    cw = sconv_w.shape[2]
    me = 4 * lax.axis_index("x") + 2 * lax.axis_index("y") + lax.axis_index("c")

    big = ["ffn1_w13", "ffn1_w2", "w_in", "w_up", "w_out", "ffn2_w13", "ffn2_w2"]
    shards = [P[n].astype(BF16) for n in big]
    conv_local = jnp.concatenate([sconv_w, cconv_w], axis=1)

    def gather_of(units):
        return _gather_comm(shards, [(big.index(n), l) for n, l in units])

    def ready(n, g):
        if n.endswith("w13"):
            return jnp.transpose(g.reshape(2, 4, D, fs), (0, 2, 1, 3)).reshape(2, D, F)
        if n.endswith("w2"):
            return g.reshape(F, D)
        if n == "w_up":
            return jnp.transpose(g, (1, 2, 0, 3)).reshape(4, BW, D)
        if n == "w_out":
            return g.reshape(D, D)
        return g

    W = {}

    def take(units, arrays):
        for (n, l), g in zip(units, arrays):
            W[n, l] = ready(n, g)

    first_units = [("ffn1_w13", 0), ("ffn1_w2", 0)]
    plan = {("ffn1_up", 0): [("w_in", 0)],
            ("ffn1_down", 0): [("w_up", 0), ("w_out", 0), ("ffn2_w13", 0)],
            ("proj", 0): [("ffn2_w2", 0)] + ([("ffn1_w13", 1)] if L > 1 else []),
            ("ffn2_up", 0): [("ffn1_w2", 1)], ("ffn2_down", 0): [("w_in", 1)],
            ("ffn1_up", 1): [("w_up", 1), ("w_out", 1)], ("ffn1_down", 1): [("ffn2_w13", 1)],
            ("proj", 1): [("ffn2_w2", 1)]}
    assert L <= 2

    def carried(key):
        units = [u for u in plan.get(key, []) if u[1] < L]
        return units, (gather_of(units) if units else None)

    first = _gather_comm(shards + [conv_local], [(big.index(n), l) for n, l in first_units] + [(len(big), None)])
    got = _run_comm(first, "gather_first_weights")
    take(first_units, got[:2])
    conv_full = jnp.transpose(got[2], (1, 2, 0, 3)).reshape(L, SCONV_K + CCONV_K, N_DEV * cw)
    sconv_full = conv_full[:, :SCONV_K]
    cconv_full = conv_full[:, SCONV_K:]

    sgu_bias = jnp.broadcast_to(sgu_b[:, :, :, None], sgu_b.shape + (GW,))
    pool_wt = jnp.swapaxes(pool_w, 2, 3)
    sgu_wt = jnp.swapaxes(sgu_w, 2, 3)

    def row(a, l):
        return a[l][None, :]

    saved = []
    xc = x[0]
    for l in range(L):
        sv = {}
        for tag in ("ffn1", None, "ffn2"):
            if tag is None:
                sv["x_mix"] = xc
                h = _rmsnorm_fwd(xc, row(mix_norm, l), "mix_norm_fwd")
                units, comm = carried(("proj", l))
                proj, extra = _matmul_fwd(h, W["w_in", l], "proj_fwd", comm)
                take(units, extra)
                y = _mixers_fwd(proj, pool_w[l], row(pool_scale, l), sconv_full[l], cconv_full[l],
                                row(cconv_ln_g, l), row(cconv_ln_b, l), row(sgu_ln_g, l), row(sgu_ln_b, l),
                                sgu_w[l], sgu_bias[l], "mixers_fwd")
                xc, merged = _merge_fwd(y, proj, W["w_up", l], W["w_out", l], xc, "merge_fwd")
                sv.update(h_mix=h, proj=proj, y=y, merged=merged)
            else:
                sv["x_" + tag] = xc
                h = _rmsnorm_fwd(xc, row(P[tag + "_norm"], l), "ffn_norm_fwd")
                units, comm = carried((tag + "_up", l))
                ab, extra = _matmul_fwd(h, W[tag + "_w13", l], "ffn_up_fwd", comm)
                take(units, extra)
                units, comm = carried((tag + "_down", l))
                xc, extra = _swiglu_down(ab, W[tag + "_w2", l], xc, "ffn_down_fwd", comm)
                take(units, extra)
                sv.update({"h_" + tag: h, "ab_" + tag: ab})
        saved.append(sv)

    loss_part, dx, d_final = _final_loss(xc, final_norm[None, :], loss_target[0], "loss_head")
    loss = lax.psum(loss_part[0, 0], MESH_AXES)

    R = {}
    small_g = [None] * L
    for l in reversed(range(L)):
        sv = saved[l]
        sg = {}
        for tag in ("ffn2", None, "ffn1"):
            if tag is None:
                dup, dproj, dy = _merge_bwd(dx, sv["y"], sv["proj"], W["w_up", l], W["w_out", l], "merge_bwd")
                g_out, _ = _matmul_tn(sv["merged"][None], dx[None], 1, "w_out_grad")
                g_up, _ = _matmul_tn(sv["y"], dup, 1, "w_up_grad")
                res = _mixers_bwd(sv["proj"], dy, dproj, pool_w[l], pool_wt[l], row(pool_scale, l), sconv_full[l],
                                  cconv_full[l], row(cconv_ln_g, l), row(cconv_ln_b, l), row(sgu_ln_g, l),
                                  row(sgu_ln_b, l), sgu_w[l], sgu_wt[l], sgu_bias[l], "mixers_bwd")
                dproj = res[0]
                (sg["pool_w"], sg["pool_scale"], sg["sconv_w"], sg["cconv_w"], sg["cconv_ln_g"], sg["cconv_ln_b"],
                 sg["sgu_ln_g"], sg["sgu_ln_b"], sg["sgu_w"], dgb) = res[1:]
                sg["sgu_b"] = dgb[:, :, 0]
                g_out = g_out.reshape(N_DEV, D // N_DEV, D)
                g_up = jnp.transpose(g_up.reshape(4, BW, N_DEV, D // N_DEV), (2, 0, 1, 3)).reshape(
                    N_DEV, 4 * BW, D // N_DEV)
                g_in, (R["w_out", l], R["w_up", l]) = _matmul_tn(sv["h_mix"][None], dproj, N_DEV, "w_in_grad",
                                                                 _scatter_comm([g_out, g_up]))
                dx, sg["mix_norm"], (R["w_in", l],) = _matmul_nt_normbwd(
                    dproj, W["w_in", l], sv["x_mix"], row(mix_norm, l), dx, "proj_bwd", _scatter_comm([g_in]))
            else:
                dab, sh = _ffn_bwd_hidden(dx, W[tag + "_w2", l], sv["ab_" + tag], "ffn_hidden_bwd")
                g_w2, _ = _matmul_tn(sh[None], dx[None], 1, "ffn_w2_grad")
                g_w13, _ = _matmul_tn(sv["h_" + tag][None], dab, 2, "ffn_w13_grad")
                g_w2 = g_w2.reshape(N_DEV, F // N_DEV, D)
                g_w13 = jnp.transpose(g_w13.reshape(2, D, 4, fs), (0, 2, 1, 3)).reshape(N_DEV, D, fs)
                dx, sg[tag + "_norm"], (R[tag + "_w2", l], R[tag + "_w13", l]) = _matmul_nt_normbwd(
                    dab, W[tag + "_w13", l], sv["x_" + tag], row(P[tag + "_norm"], l), dx, "ffn_up_bwd",
                    _scatter_comm([g_w2, g_w13]))
        small_g[l] = sg
    grad_x = dx[None]

    out = {}
    for n in big:
        shp = P[n].shape
        rows, cols = math.prod(shp[1:-1]), shp[-1]
        flat = lambda a: a.reshape(L, rows, cols)
        res = _adamw_sharded([R[n, l] for l in range(L)], flat(P[n]), flat(P["m_" + n]), flat(P["v_" + n]),
                             "adamw_sharded")
        out[n] = tuple(a.reshape(shp) for a in res)

    wide = ["ffn1_norm", "mix_norm", "ffn2_norm", "final_norm"]
    half = ["pool_scale", "cconv_ln_g", "cconv_ln_b", "sgu_ln_g", "sgu_ln_b"]
    narrow = ["pool_w", "sgu_w", "sgu_b"]

    def stack_layers(n):
        return jnp.stack([small_g[l][n] for l in range(L)], axis=0)

    def as2d(n, a):
        if n == "final_norm":
            return a.reshape(1, D)
        return a.reshape(-1, a.shape[-1])

    sg2 = {n: stack_layers(n).reshape(-1, stack_layers(n).shape[-1]) for n in wide[:3] + half + narrow}
    sg2["final_norm"] = d_final
    conv_g = jnp.concatenate([stack_layers("sconv_w"), stack_layers("cconv_w")], axis=1)
    conv_g = conv_g.reshape(L * (SCONV_K + CCONV_K), N_DEV * cw)
    small_names = wide + half + narrow
    widths = []
    for n in small_names:
        if sg2[n].shape[1] not in widths:
            widths.append(sg2[n].shape[1])
    layout = {}
    bufs = []
    for b, width in enumerate(widths):
        parts = []
        r0 = 0
        for n in small_names:
            if sg2[n].shape[1] != width:
                continue
            nr = sg2[n].shape[0]
            pr = -(-nr // 8) * 8
            layout[n] = (b, r0, nr)
            parts.append(_pad_rows(sg2[n], pr))
            r0 += pr
        if width == conv_g.shape[1]:
            conv_b, conv_r0 = b, r0
            parts.append(_pad_rows(conv_g, -(-conv_g.shape[0] // 8) * 8))
        bufs.append(jnp.concatenate(parts, axis=0))
    gathered_small = _run_comm(_gather_comm(bufs, [(b, None) for b in range(len(bufs))]), "gather_small_grads")
    res = _adamw_replicated(gathered_small, [layout[n] for n in small_names],
                            [(as2d(n, P[n]), as2d(n, P["m_" + n]), as2d(n, P["v_" + n])) for n in small_names],
                            "adamw_replicated")
    for p, n in enumerate(small_names):
        out[n] = tuple(a.reshape(P[n].shape) for a in res[4 * p:4 * p + 4])
    conv_sum = res[4 * len(small_names) + conv_b][conv_r0:conv_r0 + conv_g.shape[0]]
    conv_mine = lax.dynamic_slice_in_dim(conv_sum, me * cw, cw, axis=1)

    def conv2d(a, b):
        return jnp.concatenate([a, b], axis=1).reshape(L * (SCONV_K + CCONV_K), cw)

    cd, cm, cv = _adamw_small(conv_mine, conv2d(sconv_w, cconv_w), conv2d(m_sconv_w, m_cconv_w),
                              conv2d(v_sconv_w, v_cconv_w), "adamw_conv")
    for n, sl in (("sconv_w", slice(0, SCONV_K)), ("cconv_w", slice(SCONV_K, SCONV_K + CCONV_K))):
        out[n] = tuple(a.reshape(L, SCONV_K + CCONV_K, cw)[:, sl] for a in (conv_mine, cd, cm, cv))

    order = ["ffn1_norm", "ffn1_w13", "ffn1_w2", "mix_norm", "w_in", "pool_w", "pool_scale", "sconv_w", "cconv_w",
             "cconv_ln_g", "cconv_ln_b", "sgu_ln_g", "sgu_ln_b", "sgu_w", "sgu_b", "w_up", "w_out", "ffn2_norm",
             "ffn2_w13", "ffn2_w2", "final_norm"]
    return (loss, grad_x, *[out[n][0] for n in order], *[out[n][1] for n in order],
            *[out[n][2] for n in order], *[out[n][3] for n in order])
```

```python
import functools
import math

import jax
import jax.numpy as jnp
from jax import lax
from jax.experimental import pallas as pl
from jax.experimental.pallas import tpu as pltpu

F32 = jnp.float32
BF16 = jnp.bfloat16
EPS = 1e-6
ADAM_LR = 0.001
ADAM_B1 = 0.9
ADAM_B2 = 0.999
ADAM_EPS = 1e-08
ADAM_WD = 0.01
ADAM_STEP = 10
SGU_BLOCK = 128
SGU_CHUNK = 64
SCONV_K = 3
CCONV_K = 31
HALO = 32
V7X_VMEM_LIMIT = 48 * 1024 * 1024
MESH_AXES = ("x", "y", "c")
N_DEV = 8
_GELU_C0 = math.sqrt(2.0 / math.pi)
_GELU_C1 = 0.044715

BS = pl.BlockSpec
SDS = jax.ShapeDtypeStruct
ANY = pl.BlockSpec(memory_space=pl.ANY)


def _tile(n, pref, align=128):
    if n <= pref:
        return n
    t = pref - pref % align
    while t > 0:
        if n % t == 0:
            return t
        t -= align
    return n


def _sig(v):
    return 1.0 / (1.0 + jnp.exp(-v))


def _gelu(v):
    t = jnp.tanh(_GELU_C0 * (v + _GELU_C1 * (v * v * v)))
    return 0.5 * v * (1.0 + t), t


def _gelu_grad(v, t):
    return 0.5 * (1.0 + t) + 0.5 * v * (1.0 - t * t) * (_GELU_C0 * (1.0 + 3.0 * _GELU_C1 * v * v))


def _ln_stats(v):
    mu = jnp.mean(v, axis=-1, keepdims=True)
    vc = v - mu
    var = jnp.mean(vc * vc, axis=-1, keepdims=True)
    rstd = lax.rsqrt(var + EPS)
    return vc * rstd, rstd


def _ln_bwd(dvh, vh, rstd):
    return rstd * (dvh - jnp.mean(dvh, axis=-1, keepdims=True) - vh * jnp.mean(dvh * vh, axis=-1, keepdims=True))


def _dot(a, b):
    return jnp.dot(a, b, preferred_element_type=F32)


def _dot_nt(a, b):
    return lax.dot_general(a, b, (((1,), (1,)), ((), ())), preferred_element_type=F32)


def _dot_tn(a, b):
    return lax.dot_general(a, b, (((0,), (0,)), ((), ())), preferred_element_type=F32)


def _mesh_pos():
    return lax.axis_index("x"), lax.axis_index("y"), lax.axis_index("c")


class _Comm:
    def __init__(self, ins, out_shape, sems, start, finish):
        self.ins, self.out_shape, self.sems, self.start, self.finish = ins, out_shape, sems, start, finish


def _gather_comm(shards, units):
    n_u = len(units)

    def tools(ins, dsts, sems):
        send_sems, recv_sems, local_sems = sems
        x, y, c = _mesh_pos()
        chips = [(1 - x, y), (x, 1 - y), (1 - x, 1 - y)]

        def src_of(o):
            t, l = units[o]
            return ins[t] if l is None else ins[t].at[l]

        def row(o, p):
            return dsts[o].at[4 * p[0] + 2 * p[1] + p[2]]

        def copy(o, k, block, to, own=False):
            return pltpu.make_async_remote_copy(
                src_ref=src_of(o) if own else row(o, block), dst_ref=row(o, block),
                send_sem=send_sems.at[o * 7 + k], recv_sem=recv_sems.at[o * 7 + k],
                device_id=to, device_id_type=pl.DeviceIdType.MESH)

        def local(o):
            return pltpu.make_async_copy(src_of(o), row(o, (x, y, c)), local_sems.at[o])

        def first(o):
            return [copy(o, 1 + j, (x, y, c), (*chip, c), own=True) for j, chip in enumerate(chips)] + [
                copy(o, 0, (x, y, c), (x, y, 1 - c), own=True)]

        return (x, y, c), chips, copy, local, first

    def start(ins, dsts, sems):
        _, _, _, local, first = tools(ins, dsts, sems)
        for o in range(n_u):
            local(o).start()
            for cp in first(o):
                cp.start()

    def finish(ins, dsts, sems):
        (x, y, c), chips, copy, local, first = tools(ins, dsts, sems)
        me, sibling = (x, y, c), (x, y, 1 - c)
        passed = []
        for o in range(n_u):
            for j, chip in enumerate(chips):
                copy(o, 1 + j, (*chip, c), me).wait_recv()
                cp = copy(o, 4 + j, (*chip, c), sibling)
                cp.start()
                passed.append(cp)
        for o in range(n_u):
            copy(o, 0, sibling, me).wait_recv()
            for j, chip in enumerate(chips):
                copy(o, 4 + j, (*chip, 1 - c), me).wait_recv()
        for o in range(n_u):
            for cp in first(o):
                cp.wait_send()
        for cp in passed:
            cp.wait_send()
        for o in range(n_u):
            local(o).wait()

    out_shape = []
    for t, l in units:
        shp = shards[t].shape if l is None else shards[t].shape[1:]
        out_shape.append(SDS((N_DEV,) + tuple(shp), shards[t].dtype))
    sems = [pltpu.SemaphoreType.DMA((7 * n_u,)), pltpu.SemaphoreType.DMA((7 * n_u,)), pltpu.SemaphoreType.DMA((n_u,))]
    return _Comm(list(shards), out_shape, sems, start, finish)


def _scatter_comm(parts):
    n_u = len(parts)

    def tools(ins, dsts, sems):
        send_sems, recv_sems, local_sems = sems
        x, y, c = _mesh_pos()
        me = 4 * x + 2 * y + c

        def peer(k):
            return ((x + ((k >> 2) & 1)) % 2, (y + ((k >> 1) & 1)) % 2, (c + (k & 1)) % 2)

        def copy(u, k, wait=False):
            p = peer(k)
            pi = 4 * p[0] + 2 * p[1] + p[2]
            return pltpu.make_async_remote_copy(
                src_ref=ins[u].at[pi], dst_ref=dsts[u].at[pi if wait else me],
                send_sem=send_sems.at[u * 7 + k - 1], recv_sem=recv_sems.at[u * 7 + k - 1],
                device_id=p, device_id_type=pl.DeviceIdType.MESH)

        def local(u):
            return pltpu.make_async_copy(ins[u].at[me], dsts[u].at[me], local_sems.at[u])

        return copy, local

    def start(ins, dsts, sems):
        copy, local = tools(ins, dsts, sems)
        for u in range(n_u):
            local(u).start()
            for k in range(1, N_DEV):
                copy(u, k).start()

    def finish(ins, dsts, sems):
        copy, local = tools(ins, dsts, sems)
        for u in range(n_u):
            for k in range(1, N_DEV):
                copy(u, k, wait=True).wait()
            local(u).wait()

    sems = [pltpu.SemaphoreType.DMA((7 * n_u,)), pltpu.SemaphoreType.DMA((7 * n_u,)), pltpu.SemaphoreType.DMA((n_u,))]
    return _Comm(list(parts), [SDS(p.shape, p.dtype) for p in parts], sems, start, finish)


def _run_comm(comm, name):
    n_i, n_o = len(comm.ins), len(comm.out_shape)

    def body(*refs):
        comm.start(refs[:n_i], refs[n_i:n_i + n_o], refs[n_i + n_o:])
        comm.finish(refs[:n_i], refs[n_i:n_i + n_o], refs[n_i + n_o:])

    return pl.pallas_call(
        body, name=name, out_shape=comm.out_shape, in_specs=[ANY] * n_i, out_specs=[ANY] * n_o,
        scratch_shapes=comm.sems,
    )(*comm.ins)


def _pcall(body, name, grid, in_specs, out_specs, out_shape, scratch, sem, args, comm=None, aliases=None):
    n_i, n_o, n_s = len(in_specs), len(out_specs), len(scratch)
    aliases = aliases or {}
    if comm is None:
        res = pl.pallas_call(
            body, name=name, grid=grid, in_specs=in_specs, out_specs=out_specs, out_shape=out_shape,
            scratch_shapes=scratch, input_output_aliases=aliases,
            compiler_params=pltpu.CompilerParams(dimension_semantics=sem, vmem_limit_bytes=V7X_VMEM_LIMIT),
        )(*args)
        return res, []
    n_ci, n_co = len(comm.ins), len(comm.out_shape)

    def wrapped(*refs):
        ins = refs[:n_i]
        cins = refs[n_i:n_i + n_ci]
        outs = refs[n_i + n_ci:n_i + n_ci + n_o]
        couts = refs[n_i + n_ci + n_o:n_i + n_ci + n_o + n_co]
        rest = refs[n_i + n_ci + n_o + n_co:]
        ids = [pl.program_id(d) for d in range(len(grid))]
        first = functools.reduce(jnp.logical_and, [i == 0 for i in ids])
        last = functools.reduce(jnp.logical_and, [i == g - 1 for i, g in zip(ids, grid)])

        @pl.when(first)
        def _():
            comm.start(cins, couts, rest[n_s:])

        body(*ins, *outs, *rest[:n_s])

        @pl.when(last)
        def _():
            comm.finish(cins, couts, rest[n_s:])

    res = pl.pallas_call(
        wrapped, name=name, grid=grid, in_specs=list(in_specs) + [ANY] * n_ci,
        out_specs=list(out_specs) + [ANY] * n_co, out_shape=list(out_shape) + list(comm.out_shape),
        scratch_shapes=list(scratch) + list(comm.sems), input_output_aliases=aliases,
        compiler_params=pltpu.CompilerParams(dimension_semantics=("arbitrary",) * len(grid),
                                             vmem_limit_bytes=V7X_VMEM_LIMIT),
    )(*args, *comm.ins)
    return res[:n_o], res[n_o:]


def _rmsnorm_fwd(x, g, name):
    S, D = x.shape
    tm = _tile(S, 512, 16)

    def body(x_ref, g_ref, h_ref):
        xv = x_ref[...]
        r = lax.rsqrt(jnp.mean(xv * xv, axis=-1, keepdims=True) + EPS)
        h_ref[...] = (xv * r * g_ref[...]).astype(BF16)

    (h,), _ = _pcall(body, name, (S // tm,), [BS((tm, D), lambda i: (i, 0)), BS((1, D), lambda i: (0, 0))],
                     [BS((tm, D), lambda i: (i, 0))], [SDS((S, D), BF16)], [], ("parallel",), (x, g))
    return h


def _matmul_fwd(a, w, name, comm=None):
    S, K = a.shape
    C, _, Fc = w.shape
    tn = _tile(Fc, 1408)
    tm = _tile(S, 512, 16)

    def body(a_ref, w_ref, o_ref):
        o_ref[0] = _dot(a_ref[...], w_ref[0])

    (o,), extra = _pcall(
        body, name, (C, Fc // tn, S // tm),
        [BS((tm, K), lambda c, n, i: (i, 0)), BS((1, K, tn), lambda c, n, i: (c, 0, n))],
        [BS((1, tm, tn), lambda c, n, i: (c, i, n))], [SDS((C, S, Fc), F32)], [],
        ("parallel", "parallel", "parallel"), (a, w), comm)
    return o, extra


def _swiglu_down(ab, w2, x, name, comm=None):
    _, S, F = ab.shape
    D = w2.shape[1]
    tk = _tile(F, 1408)
    tm = _tile(S, 512, 16)

    def body(ab_ref, w_ref, x_ref, o_ref):
        k = pl.program_id(1)
        a = ab_ref[0]
        s = a * _sig(a) * ab_ref[1]
        p = 0.5 * _dot(s.astype(BF16), w_ref[...])

        @pl.when(k == 0)
        def _():
            o_ref[...] = x_ref[...] + p

        @pl.when(k > 0)
        def _():
            o_ref[...] += p

    (o,), extra = _pcall(
        body, name, (S // tm, F // tk),
        [BS((2, tm, tk), lambda i, k: (0, i, k)), BS((tk, D), lambda i, k: (k, 0)), BS((tm, D), lambda i, k: (i, 0))],
        [BS((tm, D), lambda i, k: (i, 0))], [SDS((S, D), F32)], [], ("parallel", "arbitrary"), (ab, w2, x), comm)
    return o, extra


def _ffn_bwd_hidden(dy, w2, ab, name):
    S, D = dy.shape
    F = w2.shape[0]
    tk = _tile(F, 1408)
    tm = _tile(S, 256, 16)

    def body(dy_ref, w_ref, ab_ref, dab_ref, s_ref):
        ds = 0.5 * _dot_nt(dy_ref[...].astype(BF16), w_ref[...])
        a = ab_ref[0]
        b = ab_ref[1]
        sg = _sig(a)
        sa = a * sg
        dab_ref[0] = (ds * b * (sg * (1.0 + a * (1.0 - sg)))).astype(BF16)
        dab_ref[1] = (ds * sa).astype(BF16)
        s_ref[...] = (0.5 * (sa * b)).astype(BF16)

    (dab, sh), _ = _pcall(
        body, name, (F // tk, S // tm),
        [BS((tm, D), lambda k, i: (i, 0)), BS((tk, D), lambda k, i: (k, 0)), BS((2, tm, tk), lambda k, i: (0, i, k))],
        [BS((2, tm, tk), lambda k, i: (0, i, k)), BS((tm, tk), lambda k, i: (i, k))],
        [SDS((2, S, F), BF16), SDS((S, F), BF16)], [], ("parallel", "parallel"), (dy, w2, ab))
    return dab, sh


def _matmul_tn(a, b, n_c, name, comm=None):
    G, S, M = a.shape
    _, _, Fc = b.shape
    C = n_c
    tM = _tile(M, 1408)
    tn = _tile(Fc, 1408)
    ts = _tile(S, 512, 16)
    n_s = S // ts

    def body(a_ref, b_ref, o_ref, acc):
        s = pl.program_id(4)
        p = _dot_tn(a_ref[0].astype(BF16), b_ref[0].astype(BF16))

        @pl.when(s == 0)
        def _():
            acc[...] = p

        @pl.when(s > 0)
        def _():
            acc[...] += p

        @pl.when(s == n_s - 1)
        def _():
            o_ref[0] = acc[...].astype(BF16)

    (o,), extra = _pcall(
        body, name, (G, M // tM, C, Fc // tn, n_s),
        [BS((1, ts, tM), lambda g, m, c, n, s: (g, s, m)), BS((1, ts, tn), lambda g, m, c, n, s: (g * C + c, s, n))],
        [BS((1, tM, tn), lambda g, m, c, n, s: (g * C + c, m, n))], [SDS((G * C, M, Fc), BF16)],
        [pltpu.VMEM((tM, tn), F32)], ("parallel", "parallel", "parallel", "parallel", "arbitrary"), (a, b), comm)
    return o, extra


def _matmul_nt_normbwd(b, w, x, gam, dres, name, comm=None):
    C, S, Fc = b.shape
    D = w.shape[1]
    tk = _tile(Fc, 1408)
    tm = _tile(S, 512, 16)
    nk = Fc // tk

    def body(b_ref, w_ref, x_ref, g_ref, r_ref, dx_ref, dg_ref, acc):
        i, c, k = pl.program_id(0), pl.program_id(1), pl.program_id(2)
        p = _dot_nt(b_ref[0], w_ref[0])
        first = jnp.logical_and(c == 0, k == 0)

        @pl.when(first)
        def _():
            acc[...] = p

        @pl.when(jnp.logical_not(first))
        def _():
            acc[...] += p

        @pl.when(jnp.logical_and(c == C - 1, k == nk - 1))
        def _():
            xv = x_ref[...]
            r = lax.rsqrt(jnp.mean(xv * xv, axis=-1, keepdims=True) + EPS)
            xn = xv * r
            dh = acc[...]
            dxn = dh * g_ref[...]
            dx_ref[...] = r_ref[...] + r * (dxn - xn * jnp.mean(dxn * xn, axis=-1, keepdims=True))
            dgp = jnp.sum(dh * xn, axis=0, keepdims=True)

            @pl.when(i == 0)
            def _():
                dg_ref[...] = dgp

            @pl.when(i > 0)
            def _():
                dg_ref[...] += dgp

    (dx, dg), extra = _pcall(
        body, name, (S // tm, C, nk),
        [BS((1, tm, tk), lambda i, c, k: (c, i, k)), BS((1, D, tk), lambda i, c, k: (c, 0, k)),
         BS((tm, D), lambda i, c, k: (i, 0)), BS((1, D), lambda i, c, k: (0, 0)), BS((tm, D), lambda i, c, k: (i, 0))],
        [BS((tm, D), lambda i, c, k: (i, 0)), BS((1, D), lambda i, c, k: (0, 0))],
        [SDS((S, D), F32), SDS((1, D), F32)], [pltpu.VMEM((tm, D), F32)],
        ("arbitrary", "arbitrary", "arbitrary"), (b, w, x, gam, dres), comm)
    return dx, dg, extra


def _final_loss(x, gam, target, name):
    S, D = x.shape
    tm = _tile(S, 512, 8)

    def body(x_ref, g_ref, t_ref, loss_ref, dx_ref, dg_ref):
        i = pl.program_id(0)
        xv = x_ref[...]
        r = lax.rsqrt(jnp.mean(xv * xv, axis=-1, keepdims=True) + EPS)
        xn = xv * r
        err = xn * g_ref[...] - t_ref[...]
        part = 0.5 * jnp.sum(jnp.mean(err * err, axis=-1, keepdims=True), axis=0, keepdims=True)
        dy = err * (1.0 / D)
        dxn = dy * g_ref[...]
        dx_ref[...] = r * (dxn - xn * jnp.mean(dxn * xn, axis=-1, keepdims=True))
        dgp = jnp.sum(dy * xn, axis=0, keepdims=True)
        lp = jnp.broadcast_to(part, loss_ref.shape)

        @pl.when(i == 0)
        def _():
            dg_ref[...] = dgp
            loss_ref[...] = lp

        @pl.when(i > 0)
        def _():
            dg_ref[...] += dgp
            loss_ref[...] += lp

    res, _ = _pcall(
        body, name, (S // tm,),
        [BS((tm, D), lambda i: (i, 0)), BS((1, D), lambda i: (0, 0)), BS((tm, D), lambda i: (i, 0))],
        [BS((8, 128), lambda i: (0, 0)), BS((tm, D), lambda i: (i, 0)), BS((1, D), lambda i: (0, 0))],
        [SDS((8, 128), F32), SDS((S, D), F32), SDS((1, D), F32)], [], ("arbitrary",), (x, gam, target))
    return res


def _sgu_masks():
    ii = lax.broadcasted_iota(jnp.int32, (SGU_BLOCK, SGU_BLOCK), 0) // SGU_CHUNK
    jj = lax.broadcasted_iota(jnp.int32, (SGU_BLOCK, SGU_BLOCK), 1) // SGU_CHUNK
    return jj <= ii, ii <= jj


def _mixers_fwd(proj, pool_w, pool_scale, sconv_w, cconv_w, cln_g, cln_b, sln_g, sln_b, sgu_w, sgu_bias, name,
                comm=None):
    _, S, D = proj.shape
    BW = D // 2
    GW = BW // 4
    TS = _tile(S, 256, SGU_BLOCK)
    H = HALO
    hb = TS // H

    def main(blk, col):
        return BS((1, TS, BW), lambda i: (blk, i, col))

    def back(blk, col):
        return BS((1, H, BW), lambda i: (blk, jnp.maximum(i * hb - 1, 0), col))

    def full(a):
        nd = a.ndim
        return BS(a.shape, lambda i: (0,) * nd)

    def body(pa_m, pa_b, xi_m, xi_b, bg_m, cg_m, cg_b, ca_m, ca_b, cb_m, cb_b, du_m, dv_m,
             pw, ps, sw, cw, clg, clb, slg, slb, gw, gbias, y_ref, e1, e2, e3):
        i = pl.program_id(0)
        nb = jnp.where(i > 0, 1.0, 0.0).astype(F32)
        rows = i * TS + lax.broadcasted_iota(jnp.int32, (TS, 1), 0)

        e1[0:H, :] = pa_b[0] * nb
        e1[H:H + TS, :] = pa_m[0]
        for g in range(4):
            cols = slice(g * GW, (g + 1) * GW)
            win = 2 << g
            wsum = e1[H:H + TS, cols]
            for k in range(1, win):
                wsum = wsum + e1[H - k:H - k + TS, cols]
            cnt = jnp.minimum(rows + 1, win).astype(F32)
            d = wsum / cnt - e1[H:H + TS, cols]
            yg = _dot(d.astype(BF16), pw[g].astype(BF16)) * ps[:, cols]
            y_ref[0, :, cols] = yg.astype(BF16)

        e2[0:H, :] = cg_b[0] * xi_b[0] * nb
        e2[H:H + TS, :] = cg_m[0] * xi_m[0]
        cz = sw[0:1, :] * e2[H - 2:H - 2 + TS, :]
        for k in range(1, SCONV_K):
            cz = cz + sw[k:k + 1, :] * e2[H - 2 + k:H - 2 + k + TS, :]
        y_ref[1] = (bg_m[0] * cz).astype(BF16)

        e3[0:H, :] = ca_b[0] * _sig(cb_b[0]) * nb
        e3[H:H + TS, :] = ca_m[0] * _sig(cb_m[0])
        o = H - (CCONV_K - 1)
        y1 = cw[0:1, :] * e3[o:o + TS, :]
        for k in range(1, CCONV_K):
            y1 = y1 + cw[k:k + 1, :] * e3[o + k:o + k + TS, :]
        yh, _ = _ln_stats(y1)
        y2 = yh * clg[...] + clb[...]
        y_ref[2] = (y2 * _sig(y2)).astype(BF16)

        u, _ = _gelu(du_m[0])
        v, _ = _gelu(dv_m[0])
        vh, _ = _ln_stats(v)
        vn = vh * slg[...] + slb[...]
        mask, _ = _sgu_masks()
        for h in range(4):
            wm = jnp.where(mask, gw[h], 0.0).astype(BF16)
            cs = slice(h * GW, (h + 1) * GW)
            for n in range(TS // SGU_BLOCK):
                rs = slice(n * SGU_BLOCK, (n + 1) * SGU_BLOCK)
                z = _dot(wm, vn[rs, cs].astype(BF16)) + gbias[h]
                y_ref[3, rs, cs] = (u[rs, cs] * z).astype(BF16)

    args = [proj] * 13 + [pool_w, pool_scale, sconv_w, cconv_w, cln_g, cln_b, sln_g, sln_b, sgu_w, sgu_bias]
    in_specs = [main(0, 0), back(0, 0), main(0, 1), back(0, 1), main(1, 0), main(1, 1), back(1, 1),
                main(2, 0), back(2, 0), main(2, 1), back(2, 1), main(3, 0), main(3, 1)]
    in_specs += [full(a) for a in args[13:]]
    (y,), extra = _pcall(body, name, (S // TS,), in_specs, [BS((4, TS, BW), lambda i: (0, i, 0))],
                         [SDS((4, S, BW), BF16)], [pltpu.VMEM((H + TS, BW), F32)] * 3, ("parallel",), args, comm)
    return y, extra


def _mixers_bwd(proj, dy, dproj_gates, pool_w, pool_wt, pool_scale, sconv_w, cconv_w, cln_g, cln_b, sln_g, sln_b,
                sgu_w, sgu_wt, sgu_bias, name, comm=None):
    _, S, D = proj.shape
    BW = D // 2
    GW = BW // 4
    TS = _tile(S, 256, SGU_BLOCK)
    H = HALO
    hb = TS // H
    n_t = S // TS
    E = TS + H

    def main(blk, col):
        return BS((1, TS, BW), lambda i: (blk, i, col))

    def back(blk, col):
        return BS((1, H, BW), lambda i: (blk, jnp.maximum(i * hb - 1, 0), col))

    def front(blk, col):
        return BS((1, H, BW), lambda i: (blk, jnp.minimum((i + 1) * hb, S // H - 1), col))

    def full(a):
        nd = a.ndim
        return BS(a.shape, lambda i: (0,) * nd)

    def body(pa_b, pa_m, xi_b, xi_m, bg_m, bg_f, cg_b, cg_m, ca_b, ca_m, ca_f, cb_b, cb_m, cb_f, du_m, dv_m,
             dya_m, dya_f, dyb_m, dyb_f, dyc_m, dyc_f, dyd_m,
             pw, pwt, ps, sw, cw, clg, clb, slg, slb, gw, gwt, gbias, _gates_in,
             dp_ref, dpw, dps, dsw, dcw, dclg, dclb, dslg, dslb, dgw, dgb,
             e1, e2, e3, e4, e5, e6):
        i = pl.program_id(0)
        nb = jnp.where(i > 0, 1.0, 0.0).astype(F32)
        nf = jnp.where(i < n_t - 1, 1.0, 0.0).astype(F32)
        rows_m = i * TS + lax.broadcasted_iota(jnp.int32, (TS, 1), 0)
        rows_e = i * TS + lax.broadcasted_iota(jnp.int32, (E, 1), 0)

        @pl.when(i == 0)
        def _():
            for r in (dpw, dps, dsw, dcw, dclg, dclb, dslg, dslb, dgw, dgb):
                r[...] = jnp.zeros(r.shape, F32)

        e1[0:H, :] = pa_b[0] * nb
        e1[H:H + TS, :] = pa_m[0]
        e2[0:TS, :] = dya_m[0] * ps[...]
        e2[TS:E, :] = dya_f[0] * ps[...] * nf
        for g in range(4):
            cols = slice(g * GW, (g + 1) * GW)
            win = 2 << g
            a_m = e1[H:H + TS, cols]
            wsum = a_m
            for k in range(1, win):
                wsum = wsum + e1[H - k:H - k + TS, cols]
            d = wsum / jnp.minimum(rows_m + 1, win).astype(F32) - a_m
            d16 = d.astype(BF16)
            dyp = e2[0:E, cols].astype(BF16)
            dd = _dot(dyp, pwt[g].astype(BF16))
            e3[0:E, cols] = dd / jnp.minimum(rows_e + 1, win).astype(F32)
            da = e3[0:TS, cols] - dd[0:TS]
            for k in range(1, win):
                da = da + e3[k:k + TS, cols]
            dp_ref[0, :, cols] = da.astype(BF16)
            ypre = _dot(d16, pw[g].astype(BF16))
            dps[:, cols] += jnp.sum(dya_m[0][:, cols] * ypre, axis=0, keepdims=True)
            dpw[g] += _dot(jnp.transpose(d).astype(BF16), dyp[0:TS])

        e4[0:H, :] = cg_b[0] * xi_b[0] * nb
        e4[H:H + TS, :] = cg_m[0] * xi_m[0]
        dyb = dyb_m[0]
        e5[0:TS, :] = dyb * bg_m[0]
        e5[TS:E, :] = dyb_f[0] * bg_f[0] * nf
        dcz = e5[0:TS, :]
        cz = None
        dz = None
        for k in range(SCONV_K):
            zk = e4[H - 2 + k:H - 2 + k + TS, :]
            wk = sw[k:k + 1, :]
            cz = wk * zk if cz is None else cz + wk * zk
            t = wk * e5[2 - k:2 - k + TS, :]
            dz = t if dz is None else dz + t
            dsw[k:k + 1, :] += jnp.sum(dcz * zk, axis=0, keepdims=True)
        dp_ref[0, :, BW:2 * BW] = (dz * cg_m[0]).astype(BF16)
        dp_ref[1, :, 0:BW] = (dyb * cz).astype(BF16)
        dp_ref[1, :, BW:2 * BW] = (dz * xi_m[0]).astype(BF16)

        sgm = _sig(cb_m[0])
        e6[0:H, :] = ca_b[0] * _sig(cb_b[0]) * nb
        e6[H:H + TS, :] = ca_m[0] * sgm
        e6[H + TS:H + E, :] = ca_f[0] * _sig(cb_f[0]) * nf
        o = H - (CCONV_K - 1)
        y1 = cw[0:1, :] * e6[o:o + E, :]
        for k in range(1, CCONV_K):
            y1 = y1 + cw[k:k + 1, :] * e6[o + k:o + k + E, :]
        yh, rstd = _ln_stats(y1)
        y2 = yh * clg[...] + clb[...]
        s2 = _sig(y2)
        e1[0:TS, :] = dyc_m[0]
        e1[TS:E, :] = dyc_f[0] * nf
        dy2 = e1[0:E, :] * (s2 * (1.0 + y2 * (1.0 - s2)))
        dclg[...] += jnp.sum((dy2 * yh)[0:TS], axis=0, keepdims=True)
        dclb[...] += jnp.sum(dy2[0:TS], axis=0, keepdims=True)
        e2[0:E, :] = _ln_bwd(dy2 * clg[...], yh, rstd)
        dy1_m = e2[0:TS, :]
        dy0 = None
        for k in range(CCONV_K):
            t = cw[k:k + 1, :] * e2[CCONV_K - 1 - k:CCONV_K - 1 - k + TS, :]
            dy0 = t if dy0 is None else dy0 + t
            dcw[k:k + 1, :] += jnp.sum(dy1_m * e6[o + k:o + k + TS, :], axis=0, keepdims=True)
        dp_ref[2, :, 0:BW] = (dy0 * sgm).astype(BF16)
        dp_ref[2, :, BW:2 * BW] = (dy0 * ca_m[0] * (sgm * (1.0 - sgm))).astype(BF16)

        pu = du_m[0]
        pv = dv_m[0]
        u, tu = _gelu(pu)
        v, tv = _gelu(pv)
        vh, vr = _ln_stats(v)
        vn = vh * slg[...] + slb[...]
        dyd = dyd_m[0]
        mask, mask_t = _sgu_masks()
        for h in range(4):
            wm = jnp.where(mask, gw[h], 0.0).astype(BF16)
            wmt = jnp.where(mask_t, gwt[h], 0.0).astype(BF16)
            cs = slice(h * GW, (h + 1) * GW)
            for n in range(TS // SGU_BLOCK):
                rs = slice(n * SGU_BLOCK, (n + 1) * SGU_BLOCK)
                vb = vn[rs, cs].astype(BF16)
                z = _dot(wm, vb) + gbias[h]
                dzb = dyd[rs, cs] * u[rs, cs]
                dz16 = dzb.astype(BF16)
                e3[rs, cs] = dyd[rs, cs] * z
                e4[rs, cs] = _dot(wmt, dz16)
                dgw[h] += jnp.where(mask, _dot_nt(dz16, vb), 0.0)
                dgb[h] += dzb
        dvn = e4[0:TS, :]
        dslg[...] += jnp.sum(dvn * vh, axis=0, keepdims=True)
        dslb[...] += jnp.sum(dvn, axis=0, keepdims=True)
        dv = _ln_bwd(dvn * slg[...], vh, vr)
        dp_ref[3, :, 0:BW] = (e3[0:TS, :] * _gelu_grad(pu, tu)).astype(BF16)
        dp_ref[3, :, BW:2 * BW] = (dv * _gelu_grad(pv, tv)).astype(BF16)

        @pl.when(i == n_t - 1)
        def _():
            for h in range(4):
                dgb[h] = jnp.broadcast_to(jnp.sum(dgb[h], axis=1, keepdims=True), dgb.shape[1:])

    params = [pool_w, pool_wt, pool_scale, sconv_w, cconv_w, cln_g, cln_b, sln_g, sln_b, sgu_w, sgu_wt, sgu_bias]
    args = [proj] * 16 + [dy] * 7 + params + [dproj_gates]
    in_specs = [back(0, 0), main(0, 0), back(0, 1), main(0, 1), main(1, 0), front(1, 0), back(1, 1), main(1, 1),
                back(2, 0), main(2, 0), front(2, 0), back(2, 1), main(2, 1), front(2, 1), main(3, 0), main(3, 1),
                main(0, 0), front(0, 0), main(1, 0), front(1, 0), main(2, 0), front(2, 0), main(3, 0)]
    in_specs += [full(a) for a in params] + [ANY]
    small = [SDS(pool_w.shape, F32), SDS(pool_scale.shape, F32), SDS(sconv_w.shape, F32), SDS(cconv_w.shape, F32),
             SDS(cln_g.shape, F32), SDS(cln_b.shape, F32), SDS(sln_g.shape, F32), SDS(sln_b.shape, F32),
             SDS(sgu_w.shape, F32), SDS(sgu_bias.shape, F32)]
    out_specs = [BS((4, TS, D), lambda i: (0, i, 0))] + [full(s) for s in small]
    return _pcall(body, name, (n_t,), in_specs, out_specs, [SDS(dproj_gates.shape, BF16)] + small,
                  [pltpu.VMEM((TS + 2 * H, BW), F32)] * 6, ("arbitrary",), args, comm, aliases={len(args) - 1: 0})


def _merge_fwd(y, proj, w_up, w_out, x, name, comm=None):
    _, S, BW = y.shape
    D = x.shape[1]
    tm = _tile(S, 256, 16)

    def body(y_ref, pg_ref, wu_ref, wo_ref, x_ref, o_ref, m_ref):
        merged = None
        for g in range(4):
            t = _sig(pg_ref[g]) * _dot(y_ref[g], wu_ref[g])
            merged = t if merged is None else merged + t
        m16 = merged.astype(BF16)
        m_ref[...] = m16
        o_ref[...] = x_ref[...] + _dot(m16, wo_ref[...])

    (o, m), extra = _pcall(
        body, name, (S // tm,),
        [BS((4, tm, BW), lambda i: (0, i, 0)), BS((4, tm, D), lambda i: (1, i, 0)),
         BS((4, BW, D), lambda i: (0, 0, 0)), BS((D, D), lambda i: (0, 0)), BS((tm, D), lambda i: (i, 0))],
        [BS((tm, D), lambda i: (i, 0)), BS((tm, D), lambda i: (i, 0))],
        [SDS((S, D), F32), SDS((S, D), BF16)], [], ("parallel",), (y, proj, w_up, w_out, x), comm)
    return o, m, extra


def _merge_bwd(dx, y, proj, w_up, w_out, name):
    _, S, BW = y.shape
    D = dx.shape[1]
    tm = _tile(S, 256, 16)

    def body(dx_ref, y_ref, pg_ref, wu_ref, wo_ref, dup_ref, dp_ref, dy_ref):
        dm = _dot_nt(dx_ref[...].astype(BF16), wo_ref[...])
        for g in range(4):
            gate = _sig(pg_ref[g])
            up = _dot(y_ref[g], wu_ref[g])
            dup = (dm * gate).astype(BF16)
            dup_ref[g] = dup
            dp_ref[g] = (dm * up * (gate * (1.0 - gate))).astype(BF16)
            dy_ref[g] = _dot_nt(dup, wu_ref[g])

    res, _ = _pcall(
        body, name, (S // tm,),
        [BS((tm, D), lambda i: (i, 0)), BS((4, tm, BW), lambda i: (0, i, 0)), BS((4, tm, D), lambda i: (1, i, 0)),
         BS((4, BW, D), lambda i: (0, 0, 0)), BS((D, D), lambda i: (0, 0))],
        [BS((4, tm, D), lambda i: (0, i, 0)), BS((4, tm, D), lambda i: (1, i, 0)), BS((4, tm, BW), lambda i: (0, i, 0))],
        [SDS((4, S, D), BF16), SDS((8, S, D), BF16), SDS((4, S, BW), F32)], [], ("parallel",),
        (dx, y, proj, w_up, w_out))
    return res


def _adamw(w, g, m, v):
    m = ADAM_B1 * m + (1.0 - ADAM_B1) * g
    v = ADAM_B2 * v + (1.0 - ADAM_B2) * (g * g)
    m_hat = m / (1.0 - ADAM_B1 ** ADAM_STEP)
    v_hat = v / (1.0 - ADAM_B2 ** ADAM_STEP)
    delta = -ADAM_LR * (m_hat / (jnp.sqrt(v_hat) + ADAM_EPS) + ADAM_WD * w)
    return delta, m, v


def _adamw_sharded(parts, w, m, v, name):
    L, R, C = w.shape
    tr = _tile(R, 256, 16)

    def body(*refs):
        p_refs = refs[:L]
        w_ref, m_ref, v_ref, g_out, d_out, m_out, v_out = refs[L:]
        l = pl.program_id(0)
        g = None
        for d in range(N_DEV):
            t = p_refs[0][d].astype(F32)
            for j in range(1, L):
                t = jnp.where(l == j, p_refs[j][d].astype(F32), t)
            g = t if g is None else g + t
        dl, mn, vn = _adamw(w_ref[0], g, m_ref[0], v_ref[0])
        g_out[0] = g
        d_out[0] = dl
        m_out[0] = mn
        v_out[0] = vn

    def part_spec(j):
        return BS((N_DEV, tr, C), lambda l, r: (0, jnp.where(l == j, r, 0), 0))

    blk = BS((1, tr, C), lambda l, r: (l, r, 0))
    res, _ = _pcall(body, name, (L, R // tr), [part_spec(j) for j in range(L)] + [blk, blk, blk], [blk] * 4,
                    [SDS((L, R, C), F32)] * 4, [], ("parallel", "parallel"), (*parts, w, m, v))
    return res


def _adamw_replicated(gathered, layout, wmv, name):
    n_b = len(gathered)
    n_p = len(layout)

    def body(*refs):
        bufs = refs[:n_b]
        prm = refs[n_b:n_b + 3 * n_p]
        outs = refs[n_b + 3 * n_p:n_b + 7 * n_p]
        sums = refs[n_b + 7 * n_p:]
        for b in range(n_b):
            s = bufs[b][0]
            for d in range(1, N_DEV):
                s = s + bufs[b][d]
            sums[b][...] = s
        for p, (b, r0, nr) in enumerate(layout):
            g = sums[b][r0:r0 + nr, :]
            d, mn, vn = _adamw(prm[3 * p][...], g, prm[3 * p + 1][...], prm[3 * p + 2][...])
            outs[4 * p][...] = g
            outs[4 * p + 1][...] = d
            outs[4 * p + 2][...] = mn
            outs[4 * p + 3][...] = vn

    flat = [a for t in wmv for a in t]
    out_shape = []
    for (w, _, _) in wmv:
        out_shape += [SDS(w.shape, F32)] * 4
    out_shape += [SDS(g.shape[1:], F32) for g in gathered]
    return pl.pallas_call(
        body, name=name, out_shape=out_shape,
        compiler_params=pltpu.CompilerParams(vmem_limit_bytes=V7X_VMEM_LIMIT),
    )(*gathered, *flat)


def _adamw_small(g, w, m, v, name):
    def body(g_ref, w_ref, m_ref, v_ref, d_out, m_out, v_out):
        d, mn, vn = _adamw(w_ref[...], g_ref[...], m_ref[...], v_ref[...])
        d_out[...] = d
        m_out[...] = mn
        v_out[...] = vn

    return pl.pallas_call(body, name=name, out_shape=[SDS(w.shape, F32)] * 3)(g, w, m, v)


def _pad_rows(a, rows):
    return jnp.pad(a, ((0, rows - a.shape[0]), (0, 0)))


def kernel(x, ffn1_norm, ffn1_w13, ffn1_w2, mix_norm, w_in, pool_w, pool_scale, sconv_w, cconv_w, cconv_ln_g, cconv_ln_b, sgu_ln_g, sgu_ln_b, sgu_w, sgu_b, w_up, w_out, ffn2_norm, ffn2_w13, ffn2_w2, final_norm, loss_target, m_ffn1_norm, m_ffn1_w13, m_ffn1_w2, m_mix_norm, m_w_in, m_pool_w, m_pool_scale, m_sconv_w, m_cconv_w, m_cconv_ln_g, m_cconv_ln_b, m_sgu_ln_g, m_sgu_ln_b, m_sgu_w, m_sgu_b, m_w_up, m_w_out, m_ffn2_norm, m_ffn2_w13, m_ffn2_w2, m_final_norm, v_ffn1_norm, v_ffn1_w13, v_ffn1_w2, v_mix_norm, v_w_in, v_pool_w, v_pool_scale, v_sconv_w, v_cconv_w, v_cconv_ln_g, v_cconv_ln_b, v_sgu_ln_g, v_sgu_ln_b, v_sgu_w, v_sgu_b, v_w_up, v_w_out, v_ffn2_norm, v_ffn2_w13, v_ffn2_w2, v_final_norm):
    P = dict(locals())
    L = ffn1_norm.shape[0]
    S, D = x.shape[1], x.shape[2]
    BW = D // 2
    GW = BW // 4
    F = ffn1_w2.shape[1] * N_DEV
    fs = ffn1_w13.shape[2]
    cw = sconv_w.shape[2]
    me = 4 * lax.axis_index("x") + 2 * lax.axis_index("y") + lax.axis_index("c")

    big = ["ffn1_w13", "ffn1_w2", "w_in", "w_up", "w_out", "ffn2_w13", "ffn2_w2"]
    shards = [P[n].astype(BF16) for n in big]
    conv_local = jnp.concatenate([sconv_w, cconv_w], axis=1)

    def gather_of(units):
        return _gather_comm(shards, [(big.index(n), l) for n, l in units])

    def ready(n, g):
        if n.endswith("w13"):
            return jnp.transpose(g.reshape(2, 4, D, fs), (0, 2, 1, 3)).reshape(2, D, F)
        if n.endswith("w2"):
            return g.reshape(F, D)
        if n == "w_up":
            return jnp.transpose(g, (1, 2, 0, 3)).reshape(4, BW, D)
        if n == "w_out":
            return g.reshape(D, D)
        return g

    W = {}

    def take(units, arrays):
        for (n, l), g in zip(units, arrays):
            W[n, l] = ready(n, g)

    first_units = [("ffn1_w13", 0), ("ffn1_w2", 0)]
    plan = {("ffn1_up", 0): [("w_in", 0)],
            ("ffn1_down", 0): [("w_up", 0), ("w_out", 0)],
            ("proj", 0): [("ffn2_w13", 0), ("ffn2_w2", 0)],
            ("mixers", 0): [("ffn1_w13", 1)], ("merge", 0): [("ffn1_w2", 1)],
            ("ffn2_up", 0): [("w_in", 1)], ("ffn2_down", 0): [("w_up", 1), ("w_out", 1)],
            ("ffn1_up", 1): [("ffn2_w13", 1)], ("ffn1_down", 1): [("ffn2_w2", 1)]}
    assert L <= 2

    def carried(key):
        units = [u for u in plan.get(key, []) if u[1] < L]
        return units, (gather_of(units) if units else None)

    first = _gather_comm(shards + [conv_local], [(big.index(n), l) for n, l in first_units] + [(len(big), None)])
    got = _run_comm(first, "gather_first_weights")
    take(first_units, got[:2])
    conv_full = jnp.transpose(got[2], (1, 2, 0, 3)).reshape(L, SCONV_K + CCONV_K, N_DEV * cw)
    sconv_full = conv_full[:, :SCONV_K]
    cconv_full = conv_full[:, SCONV_K:]

    sgu_bias = jnp.broadcast_to(sgu_b[:, :, :, None], sgu_b.shape + (GW,))
    pool_wt = jnp.swapaxes(pool_w, 2, 3)
    sgu_wt = jnp.swapaxes(sgu_w, 2, 3)

    def row(a, l):
        return a[l][None, :]

    saved = []
    xc = x[0]
    for l in range(L):
        sv = {}
        for tag in ("ffn1", None, "ffn2"):
            if tag is None:
                sv["x_mix"] = xc
                h = _rmsnorm_fwd(xc, row(mix_norm, l), "mix_norm_fwd")
                units, comm = carried(("proj", l))
                proj, extra = _matmul_fwd(h, W["w_in", l], "proj_fwd", comm)
                take(units, extra)
                units, comm = carried(("mixers", l))
                y, extra = _mixers_fwd(proj, pool_w[l], row(pool_scale, l), sconv_full[l], cconv_full[l],
                                       row(cconv_ln_g, l), row(cconv_ln_b, l), row(sgu_ln_g, l), row(sgu_ln_b, l),
                                       sgu_w[l], sgu_bias[l], "mixers_fwd", comm)
                take(units, extra)
                units, comm = carried(("merge", l))
                xc, merged, extra = _merge_fwd(y, proj, W["w_up", l], W["w_out", l], xc, "merge_fwd", comm)
                take(units, extra)
                sv.update(h_mix=h, proj=proj, y=y, merged=merged)
            else:
                sv["x_" + tag] = xc
                h = _rmsnorm_fwd(xc, row(P[tag + "_norm"], l), "ffn_norm_fwd")
                units, comm = carried((tag + "_up", l))
                ab, extra = _matmul_fwd(h, W[tag + "_w13", l], "ffn_up_fwd", comm)
                take(units, extra)
                units, comm = carried((tag + "_down", l))
                xc, extra = _swiglu_down(ab, W[tag + "_w2", l], xc, "ffn_down_fwd", comm)
                take(units, extra)
                sv.update({"h_" + tag: h, "ab_" + tag: ab})
        saved.append(sv)

    loss_part, dx, d_final = _final_loss(xc, final_norm[None, :], loss_target[0], "loss_head")
    loss = lax.psum(loss_part[0, 0], MESH_AXES)

    R = {}
    small_g = [None] * L
    for l in reversed(range(L)):
        sv = saved[l]
        sg = {}
        for tag in ("ffn2", None, "ffn1"):
            if tag is None:
                dup, dproj, dy = _merge_bwd(dx, sv["y"], sv["proj"], W["w_up", l], W["w_out", l], "merge_bwd")
                g_out, _ = _matmul_tn(sv["merged"][None], dx[None], 1, "w_out_grad")
                g_up, _ = _matmul_tn(sv["y"], dup, 1, "w_up_grad")
                g_out = g_out.reshape(N_DEV, D // N_DEV, D)
                g_up = jnp.transpose(g_up.reshape(4, BW, N_DEV, D // N_DEV), (2, 0, 1, 3)).reshape(
                    N_DEV, 4 * BW, D // N_DEV)
                res, (R["w_out", l], R["w_up", l]) = _mixers_bwd(
                    sv["proj"], dy, dproj, pool_w[l], pool_wt[l], row(pool_scale, l), sconv_full[l], cconv_full[l],
                    row(cconv_ln_g, l), row(cconv_ln_b, l), row(sgu_ln_g, l), row(sgu_ln_b, l), sgu_w[l], sgu_wt[l],
                    sgu_bias[l], "mixers_bwd", _scatter_comm([g_out, g_up]))
                dproj = res[0]
                (sg["pool_w"], sg["pool_scale"], sg["sconv_w"], sg["cconv_w"], sg["cconv_ln_g"], sg["cconv_ln_b"],
                 sg["sgu_ln_g"], sg["sgu_ln_b"], sg["sgu_w"], dgb) = res[1:]
                sg["sgu_b"] = dgb[:, :, 0]
                g_in, _ = _matmul_tn(sv["h_mix"][None], dproj, N_DEV, "w_in_grad")
                dx, sg["mix_norm"], (R["w_in", l],) = _matmul_nt_normbwd(
                    dproj, W["w_in", l], sv["x_mix"], row(mix_norm, l), dx, "proj_bwd", _scatter_comm([g_in]))
            else:
                dab, sh = _ffn_bwd_hidden(dx, W[tag + "_w2", l], sv["ab_" + tag], "ffn_hidden_bwd")
                g_w2, _ = _matmul_tn(sh[None], dx[None], 1, "ffn_w2_grad")
                g_w2 = g_w2.reshape(N_DEV, F // N_DEV, D)
                g_w13, (R[tag + "_w2", l],) = _matmul_tn(sv["h_" + tag][None], dab, 2, "ffn_w13_grad",
                                                         _scatter_comm([g_w2]))
                g_w13 = jnp.transpose(g_w13.reshape(2, D, 4, fs), (0, 2, 1, 3)).reshape(N_DEV, D, fs)
                dx, sg[tag + "_norm"], (R[tag + "_w13", l],) = _matmul_nt_normbwd(
                    dab, W[tag + "_w13", l], sv["x_" + tag], row(P[tag + "_norm"], l), dx, "ffn_up_bwd",
                    _scatter_comm([g_w13]))
        small_g[l] = sg
    grad_x = dx[None]

    out = {}
    for n in big:
        shp = P[n].shape
        rows, cols = math.prod(shp[1:-1]), shp[-1]
        flat = lambda a: a.reshape(L, rows, cols)
        res = _adamw_sharded([R[n, l] for l in range(L)], flat(P[n]), flat(P["m_" + n]), flat(P["v_" + n]),
                             "adamw_sharded")
        out[n] = tuple(a.reshape(shp) for a in res)

    wide = ["ffn1_norm", "mix_norm", "ffn2_norm", "final_norm"]
    half = ["pool_scale", "cconv_ln_g", "cconv_ln_b", "sgu_ln_g", "sgu_ln_b"]
    narrow = ["pool_w", "sgu_w", "sgu_b"]

    def stack_layers(n):
        return jnp.stack([small_g[l][n] for l in range(L)], axis=0)

    def as2d(n, a):
        if n == "final_norm":
            return a.reshape(1, D)
        return a.reshape(-1, a.shape[-1])

    sg2 = {n: stack_layers(n).reshape(-1, stack_layers(n).shape[-1]) for n in wide[:3] + half + narrow}
    sg2["final_norm"] = d_final
    conv_g = jnp.concatenate([stack_layers("sconv_w"), stack_layers("cconv_w")], axis=1)
    conv_g = conv_g.reshape(L * (SCONV_K + CCONV_K), N_DEV * cw)
    small_names = wide + half + narrow
    widths = []
    for n in small_names:
        if sg2[n].shape[1] not in widths:
            widths.append(sg2[n].shape[1])
    layout = {}
    bufs = []
    for b, width in enumerate(widths):
        parts = []
        r0 = 0
        for n in small_names:
            if sg2[n].shape[1] != width:
                continue
            nr = sg2[n].shape[0]
            pr = -(-nr // 8) * 8
            layout[n] = (b, r0, nr)
            parts.append(_pad_rows(sg2[n], pr))
            r0 += pr
        if width == conv_g.shape[1]:
            conv_b, conv_r0 = b, r0
            parts.append(_pad_rows(conv_g, -(-conv_g.shape[0] // 8) * 8))
        bufs.append(jnp.concatenate(parts, axis=0))
    gathered_small = _run_comm(_gather_comm(bufs, [(b, None) for b in range(len(bufs))]), "gather_small_grads")
    res = _adamw_replicated(gathered_small, [layout[n] for n in small_names],
                            [(as2d(n, P[n]), as2d(n, P["m_" + n]), as2d(n, P["v_" + n])) for n in small_names],
                            "adamw_replicated")
    for p, n in enumerate(small_names):
        out[n] = tuple(a.reshape(P[n].shape) for a in res[4 * p:4 * p + 4])
    conv_sum = res[4 * len(small_names) + conv_b][conv_r0:conv_r0 + conv_g.shape[0]]
    conv_mine = lax.dynamic_slice_in_dim(conv_sum, me * cw, cw, axis=1)

    def conv2d(a, b):
        return jnp.concatenate([a, b], axis=1).reshape(L * (SCONV_K + CCONV_K), cw)

    cd, cm, cv = _adamw_small(conv_mine, conv2d(sconv_w, cconv_w), conv2d(m_sconv_w, m_cconv_w),
                              conv2d(v_sconv_w, v_cconv_w), "adamw_conv")
    for n, sl in (("sconv_w", slice(0, SCONV_K)), ("cconv_w", slice(SCONV_K, SCONV_K + CCONV_K))):
        out[n] = tuple(a.reshape(L, SCONV_K + CCONV_K, cw)[:, sl] for a in (conv_mine, cd, cm, cv))

    order = ["ffn1_norm", "ffn1_w13", "ffn1_w2", "mix_norm", "w_in", "pool_w", "pool_scale", "sconv_w", "cconv_w",
             "cconv_ln_g", "cconv_ln_b", "sgu_ln_g", "sgu_ln_b", "sgu_w", "sgu_b", "w_up", "w_out", "ffn2_norm",
             "ffn2_w13", "ffn2_w2", "final_norm"]
    return (loss, grad_x, *[out[n][0] for n in order], *[out[n][1] for n in order],
            *[out[n][2] for n in order], *[out[n][3] for n in order])
```

```python
import functools
import math

import jax
import jax.numpy as jnp
from jax import lax
from jax.experimental import pallas as pl
from jax.experimental.pallas import tpu as pltpu

F32 = jnp.float32
BF16 = jnp.bfloat16
EPS = 1e-6
ADAM_LR = 0.001
ADAM_B1 = 0.9
ADAM_B2 = 0.999
ADAM_EPS = 1e-08
ADAM_WD = 0.01
ADAM_STEP = 10
SGU_BLOCK = 128
SGU_CHUNK = 64
SCONV_K = 3
CCONV_K = 31
HALO = 32
V7X_VMEM_LIMIT = 48 * 1024 * 1024
MESH_AXES = ("x", "y", "c")
N_DEV = 8
_GELU_C0 = math.sqrt(2.0 / math.pi)
_GELU_C1 = 0.044715

BS = pl.BlockSpec
SDS = jax.ShapeDtypeStruct
ANY = pl.BlockSpec(memory_space=pl.ANY)


def _tile(n, pref, align=128):
    if n <= pref:
        return n
    t = pref - pref % align
    while t > 0:
        if n % t == 0:
            return t
        t -= align
    return n


def _sig(v):
    return 1.0 / (1.0 + jnp.exp(-v))


def _gelu(v):
    t = jnp.tanh(_GELU_C0 * (v + _GELU_C1 * (v * v * v)))
    return 0.5 * v * (1.0 + t), t


def _gelu_grad(v, t):
    return 0.5 * (1.0 + t) + 0.5 * v * (1.0 - t * t) * (_GELU_C0 * (1.0 + 3.0 * _GELU_C1 * v * v))


def _ln_stats(v):
    mu = jnp.mean(v, axis=-1, keepdims=True)
    vc = v - mu
    var = jnp.mean(vc * vc, axis=-1, keepdims=True)
    rstd = lax.rsqrt(var + EPS)
    return vc * rstd, rstd


def _ln_bwd(dvh, vh, rstd):
    return rstd * (dvh - jnp.mean(dvh, axis=-1, keepdims=True) - vh * jnp.mean(dvh * vh, axis=-1, keepdims=True))


def _dot(a, b):
    return jnp.dot(a, b, preferred_element_type=F32)


def _dot_nt(a, b):
    return lax.dot_general(a, b, (((1,), (1,)), ((), ())), preferred_element_type=F32)


def _dot_tn(a, b):
    return lax.dot_general(a, b, (((0,), (0,)), ((), ())), preferred_element_type=F32)


def _mesh_pos():
    return lax.axis_index("x"), lax.axis_index("y"), lax.axis_index("c")


class _Comm:
    def __init__(self, ins, out_shape, sems, start, finish, aliases=None):
        self.ins, self.out_shape, self.sems, self.start, self.finish = ins, out_shape, sems, start, finish
        self.aliases = aliases or {}


def _gather_comm(shards, units):
    n_u = len(units)

    def tools(ins, dsts, sems):
        send_sems, recv_sems, local_sems = sems
        x, y, c = _mesh_pos()
        chips = [(1 - x, y), (x, 1 - y), (1 - x, 1 - y)]

        def src_of(o):
            t, l = units[o]
            return ins[t] if l is None else ins[t].at[l]

        def row(o, p):
            return dsts[o].at[4 * p[0] + 2 * p[1] + p[2]]

        def copy(o, k, block, to, own=False):
            return pltpu.make_async_remote_copy(
                src_ref=src_of(o) if own else row(o, block), dst_ref=row(o, block),
                send_sem=send_sems.at[o * 7 + k], recv_sem=recv_sems.at[o * 7 + k],
                device_id=to, device_id_type=pl.DeviceIdType.MESH)

        def local(o):
            return pltpu.make_async_copy(src_of(o), row(o, (x, y, c)), local_sems.at[o])

        def first(o):
            return [copy(o, 1 + j, (x, y, c), (*chip, c), own=True) for j, chip in enumerate(chips)] + [
                copy(o, 0, (x, y, c), (x, y, 1 - c), own=True)]

        return (x, y, c), chips, copy, local, first

    def start(ins, dsts, sems):
        _, _, _, local, first = tools(ins, dsts, sems)
        for o in range(n_u):
            local(o).start()
            for cp in first(o):
                cp.start()

    def finish(ins, dsts, sems):
        (x, y, c), chips, copy, local, first = tools(ins, dsts, sems)
        me, sibling = (x, y, c), (x, y, 1 - c)
        passed = []
        for o in range(n_u):
            for j, chip in enumerate(chips):
                copy(o, 1 + j, (*chip, c), me).wait_recv()
                cp = copy(o, 4 + j, (*chip, c), sibling)
                cp.start()
                passed.append(cp)
        for o in range(n_u):
            copy(o, 0, sibling, me).wait_recv()
            for j, chip in enumerate(chips):
                copy(o, 4 + j, (*chip, 1 - c), me).wait_recv()
        for o in range(n_u):
            for cp in first(o):
                cp.wait_send()
        for cp in passed:
            cp.wait_send()
        for o in range(n_u):
            local(o).wait()

    out_shape = []
    for t, l in units:
        shp = shards[t].shape if l is None else shards[t].shape[1:]
        out_shape.append(SDS((N_DEV,) + tuple(shp), shards[t].dtype))
    sems = [pltpu.SemaphoreType.DMA((7 * n_u,)), pltpu.SemaphoreType.DMA((7 * n_u,)), pltpu.SemaphoreType.DMA((n_u,))]
    return _Comm(list(shards), out_shape, sems, start, finish)


PEERS_ALL = (1, 2, 3, 4, 5, 6, 7)
PEERS_SAME_CORE = (1, 2, 4, 6)
PEERS_OTHER_CORE = (3, 5, 7)


def _scatter_comm(parts, peers=PEERS_ALL, into=None):
    n_u = len(parts)

    def tools(ins, dsts, sems):
        send_sems, recv_sems, local_sems = sems
        x, y, c = _mesh_pos()
        me = 4 * x + 2 * y + c

        def peer(k):
            return ((x + ((k >> 2) & 1)) % 2, (y + ((k >> 1) & 1)) % 2, (c + (k & 1)) % 2)

        def copy(u, k, wait=False):
            p = peer(k)
            pi = 4 * p[0] + 2 * p[1] + p[2]
            return pltpu.make_async_remote_copy(
                src_ref=ins[u].at[pi], dst_ref=dsts[u].at[pi if wait else me],
                send_sem=send_sems.at[u * 7 + k - 1], recv_sem=recv_sems.at[u * 7 + k - 1],
                device_id=p, device_id_type=pl.DeviceIdType.MESH)

        def local(u):
            return pltpu.make_async_copy(ins[u].at[me], dsts[u].at[me], local_sems.at[u])

        return copy, local

    def start(ins, dsts, sems):
        copy, local = tools(ins, dsts, sems)
        for u in range(n_u):
            if into is None:
                local(u).start()
            for k in peers:
                copy(u, k).start()

    def finish(ins, dsts, sems):
        copy, local = tools(ins, dsts, sems)
        for u in range(n_u):
            for k in peers:
                copy(u, k, wait=True).wait()
            if into is None:
                local(u).wait()

    sems = [pltpu.SemaphoreType.DMA((7 * n_u,)), pltpu.SemaphoreType.DMA((7 * n_u,)), pltpu.SemaphoreType.DMA((n_u,))]
    aliases = {} if into is None else {n_u + u: u for u in range(n_u)}
    return _Comm(list(parts) + list(into or []), [SDS(p.shape, p.dtype) for p in parts], sems, start, finish, aliases)


def _run_comm(comm, name):
    n_i, n_o = len(comm.ins), len(comm.out_shape)

    def body(*refs):
        comm.start(refs[:n_i], refs[n_i:n_i + n_o], refs[n_i + n_o:])
        comm.finish(refs[:n_i], refs[n_i:n_i + n_o], refs[n_i + n_o:])

    return pl.pallas_call(
        body, name=name, out_shape=comm.out_shape, in_specs=[ANY] * n_i, out_specs=[ANY] * n_o,
        scratch_shapes=comm.sems,
    )(*comm.ins)


def _pcall(body, name, grid, in_specs, out_specs, out_shape, scratch, sem, args, comm=None, aliases=None):
    n_i, n_o, n_s = len(in_specs), len(out_specs), len(scratch)
    aliases = aliases or {}
    if comm is None:
        res = pl.pallas_call(
            body, name=name, grid=grid, in_specs=in_specs, out_specs=out_specs, out_shape=out_shape,
            scratch_shapes=scratch, input_output_aliases=aliases,
            compiler_params=pltpu.CompilerParams(dimension_semantics=sem, vmem_limit_bytes=V7X_VMEM_LIMIT),
        )(*args)
        return res, []
    n_ci, n_co = len(comm.ins), len(comm.out_shape)

    def wrapped(*refs):
        ins = refs[:n_i]
        cins = refs[n_i:n_i + n_ci]
        outs = refs[n_i + n_ci:n_i + n_ci + n_o]
        couts = refs[n_i + n_ci + n_o:n_i + n_ci + n_o + n_co]
        rest = refs[n_i + n_ci + n_o + n_co:]
        ids = [pl.program_id(d) for d in range(len(grid))]
        first = functools.reduce(jnp.logical_and, [i == 0 for i in ids])
        last = functools.reduce(jnp.logical_and, [i == g - 1 for i, g in zip(ids, grid)])

        @pl.when(first)
        def _():
            comm.start(cins, couts, rest[n_s:])

        body(*ins, *outs, *rest[:n_s])

        @pl.when(last)
        def _():
            comm.finish(cins, couts, rest[n_s:])

    res = pl.pallas_call(
        wrapped, name=name, grid=grid, in_specs=list(in_specs) + [ANY] * n_ci,
        out_specs=list(out_specs) + [ANY] * n_co, out_shape=list(out_shape) + list(comm.out_shape),
        scratch_shapes=list(scratch) + list(comm.sems),
        input_output_aliases={**aliases, **{n_i + ci: n_o + co for ci, co in comm.aliases.items()}},
        compiler_params=pltpu.CompilerParams(dimension_semantics=("arbitrary",) * len(grid),
                                             vmem_limit_bytes=V7X_VMEM_LIMIT),
    )(*args, *comm.ins)
    return res[:n_o], res[n_o:]


def _rmsnorm_fwd(x, g, name):
    S, D = x.shape
    tm = _tile(S, 512, 16)

    def body(x_ref, g_ref, h_ref):
        xv = x_ref[...]
        r = lax.rsqrt(jnp.mean(xv * xv, axis=-1, keepdims=True) + EPS)
        h_ref[...] = (xv * r * g_ref[...]).astype(BF16)

    (h,), _ = _pcall(body, name, (S // tm,), [BS((tm, D), lambda i: (i, 0)), BS((1, D), lambda i: (0, 0))],
                     [BS((tm, D), lambda i: (i, 0))], [SDS((S, D), BF16)], [], ("parallel",), (x, g))
    return h


def _matmul_fwd(a, w, name, comm=None):
    S, K = a.shape
    C, _, Fc = w.shape
    tn = _tile(Fc, 1408)
    tm = _tile(S, 512, 16)

    def body(a_ref, w_ref, o_ref):
        o_ref[0] = _dot(a_ref[...], w_ref[0])

    (o,), extra = _pcall(
        body, name, (C, Fc // tn, S // tm),
        [BS((tm, K), lambda c, n, i: (i, 0)), BS((1, K, tn), lambda c, n, i: (c, 0, n))],
        [BS((1, tm, tn), lambda c, n, i: (c, i, n))], [SDS((C, S, Fc), F32)], [],
        ("parallel", "parallel", "parallel"), (a, w), comm)
    return o, extra


def _swiglu_down(ab, w2, x, name, comm=None):
    _, S, F = ab.shape
    D = w2.shape[1]
    tk = _tile(F, 1408)
    tm = _tile(S, 512, 16)

    def body(ab_ref, w_ref, x_ref, o_ref):
        k = pl.program_id(1)
        a = ab_ref[0]
        s = a * _sig(a) * ab_ref[1]
        p = 0.5 * _dot(s.astype(BF16), w_ref[...])

        @pl.when(k == 0)
        def _():
            o_ref[...] = x_ref[...] + p

        @pl.when(k > 0)
        def _():
            o_ref[...] += p

    (o,), extra = _pcall(
        body, name, (S // tm, F // tk),
        [BS((2, tm, tk), lambda i, k: (0, i, k)), BS((tk, D), lambda i, k: (k, 0)), BS((tm, D), lambda i, k: (i, 0))],
        [BS((tm, D), lambda i, k: (i, 0))], [SDS((S, D), F32)], [], ("parallel", "arbitrary"), (ab, w2, x), comm)
    return o, extra


def _ffn_bwd_hidden(dy, w2, ab, name, comm=None):
    S, D = dy.shape
    F = w2.shape[0]
    tk = _tile(F, 1408)
    tm = _tile(S, 256, 16)

    def body(dy_ref, w_ref, ab_ref, dab_ref, s_ref):
        ds = 0.5 * _dot_nt(dy_ref[...].astype(BF16), w_ref[...])
        a = ab_ref[0]
        b = ab_ref[1]
        sg = _sig(a)
        sa = a * sg
        dab_ref[0] = (ds * b * (sg * (1.0 + a * (1.0 - sg)))).astype(BF16)
        dab_ref[1] = (ds * sa).astype(BF16)
        s_ref[...] = (0.5 * (sa * b)).astype(BF16)

    (dab, sh), extra = _pcall(
        body, name, (F // tk, S // tm),
        [BS((tm, D), lambda k, i: (i, 0)), BS((tk, D), lambda k, i: (k, 0)), BS((2, tm, tk), lambda k, i: (0, i, k))],
        [BS((2, tm, tk), lambda k, i: (0, i, k)), BS((tm, tk), lambda k, i: (i, k))],
        [SDS((2, S, F), BF16), SDS((S, F), BF16)], [], ("parallel", "parallel"), (dy, w2, ab), comm)
    return dab, sh, extra


def _matmul_tn(a, b, n_c, name, comm=None):
    G, S, M = a.shape
    _, _, Fc = b.shape
    C = n_c
    tM = _tile(M, 1408)
    tn = _tile(Fc, 1408)
    ts = _tile(S, 512, 16)
    n_s = S // ts

    def body(a_ref, b_ref, o_ref, acc):
        s = pl.program_id(4)
        p = _dot_tn(a_ref[0].astype(BF16), b_ref[0].astype(BF16))

        @pl.when(s == 0)
        def _():
            acc[...] = p

        @pl.when(s > 0)
        def _():
            acc[...] += p

        @pl.when(s == n_s - 1)
        def _():
            o_ref[0] = acc[...].astype(BF16)

    (o,), extra = _pcall(
        body, name, (G, M // tM, C, Fc // tn, n_s),
        [BS((1, ts, tM), lambda g, m, c, n, s: (g, s, m)), BS((1, ts, tn), lambda g, m, c, n, s: (g * C + c, s, n))],
        [BS((1, tM, tn), lambda g, m, c, n, s: (g * C + c, m, n))], [SDS((G * C, M, Fc), BF16)],
        [pltpu.VMEM((tM, tn), F32)], ("parallel", "parallel", "parallel", "parallel", "arbitrary"), (a, b), comm)
    return o, extra


def _matmul_nt_normbwd(b, w, x, gam, dres, name, comm=None):
    C, S, Fc = b.shape
    D = w.shape[1]
    tk = _tile(Fc, 1408)
    tm = _tile(S, 512, 16)
    nk = Fc // tk

    def body(b_ref, w_ref, x_ref, g_ref, r_ref, dx_ref, dg_ref, acc):
        i, c, k = pl.program_id(0), pl.program_id(1), pl.program_id(2)
        p = _dot_nt(b_ref[0], w_ref[0])
        first = jnp.logical_and(c == 0, k == 0)

        @pl.when(first)
        def _():
            acc[...] = p

        @pl.when(jnp.logical_not(first))
        def _():
            acc[...] += p

        @pl.when(jnp.logical_and(c == C - 1, k == nk - 1))
        def _():
            xv = x_ref[...]
            r = lax.rsqrt(jnp.mean(xv * xv, axis=-1, keepdims=True) + EPS)
            xn = xv * r
            dh = acc[...]
            dxn = dh * g_ref[...]
            dx_ref[...] = r_ref[...] + r * (dxn - xn * jnp.mean(dxn * xn, axis=-1, keepdims=True))
            dgp = jnp.sum(dh * xn, axis=0, keepdims=True)

            @pl.when(i == 0)
            def _():
                dg_ref[...] = dgp

            @pl.when(i > 0)
            def _():
                dg_ref[...] += dgp

    (dx, dg), extra = _pcall(
        body, name, (S // tm, C, nk),
        [BS((1, tm, tk), lambda i, c, k: (c, i, k)), BS((1, D, tk), lambda i, c, k: (c, 0, k)),
         BS((tm, D), lambda i, c, k: (i, 0)), BS((1, D), lambda i, c, k: (0, 0)), BS((tm, D), lambda i, c, k: (i, 0))],
        [BS((tm, D), lambda i, c, k: (i, 0)), BS((1, D), lambda i, c, k: (0, 0))],
        [SDS((S, D), F32), SDS((1, D), F32)], [pltpu.VMEM((tm, D), F32)],
        ("arbitrary", "arbitrary", "arbitrary"), (b, w, x, gam, dres), comm)
    return dx, dg, extra


def _final_loss(x, gam, target, name):
    S, D = x.shape
    tm = _tile(S, 512, 8)

    def body(x_ref, g_ref, t_ref, loss_ref, dx_ref, dg_ref):
        i = pl.program_id(0)
        xv = x_ref[...]
        r = lax.rsqrt(jnp.mean(xv * xv, axis=-1, keepdims=True) + EPS)
        xn = xv * r
        err = xn * g_ref[...] - t_ref[...]
        part = 0.5 * jnp.sum(jnp.mean(err * err, axis=-1, keepdims=True), axis=0, keepdims=True)
        dy = err * (1.0 / D)
        dxn = dy * g_ref[...]
        dx_ref[...] = r * (dxn - xn * jnp.mean(dxn * xn, axis=-1, keepdims=True))
        dgp = jnp.sum(dy * xn, axis=0, keepdims=True)
        lp = jnp.broadcast_to(part, loss_ref.shape)

        @pl.when(i == 0)
        def _():
            dg_ref[...] = dgp
            loss_ref[...] = lp

        @pl.when(i > 0)
        def _():
            dg_ref[...] += dgp
            loss_ref[...] += lp

    res, _ = _pcall(
        body, name, (S // tm,),
        [BS((tm, D), lambda i: (i, 0)), BS((1, D), lambda i: (0, 0)), BS((tm, D), lambda i: (i, 0))],
        [BS((8, 128), lambda i: (0, 0)), BS((tm, D), lambda i: (i, 0)), BS((1, D), lambda i: (0, 0))],
        [SDS((8, 128), F32), SDS((S, D), F32), SDS((1, D), F32)], [], ("arbitrary",), (x, gam, target))
    return res


def _sgu_masks():
    ii = lax.broadcasted_iota(jnp.int32, (SGU_BLOCK, SGU_BLOCK), 0) // SGU_CHUNK
    jj = lax.broadcasted_iota(jnp.int32, (SGU_BLOCK, SGU_BLOCK), 1) // SGU_CHUNK
    return jj <= ii, ii <= jj


def _mixers_fwd(proj, pool_w, pool_scale, sconv_w, cconv_w, cln_g, cln_b, sln_g, sln_b, sgu_w, sgu_bias, name,
                comm=None):
    _, S, D = proj.shape
    BW = D // 2
    GW = BW // 4
    TS = _tile(S, 256, SGU_BLOCK)
    H = HALO
    hb = TS // H

    def main(blk, col):
        return BS((1, TS, BW), lambda i: (blk, i, col))

    def back(blk, col):
        return BS((1, H, BW), lambda i: (blk, jnp.maximum(i * hb - 1, 0), col))

    def full(a):
        nd = a.ndim
        return BS(a.shape, lambda i: (0,) * nd)

    def body(pa_m, pa_b, xi_m, xi_b, bg_m, cg_m, cg_b, ca_m, ca_b, cb_m, cb_b, du_m, dv_m,
             pw, ps, sw, cw, clg, clb, slg, slb, gw, gbias, y_ref, e1, e2, e3):
        i = pl.program_id(0)
        nb = jnp.where(i > 0, 1.0, 0.0).astype(F32)
        rows = i * TS + lax.broadcasted_iota(jnp.int32, (TS, 1), 0)

        e1[0:H, :] = pa_b[0] * nb
        e1[H:H + TS, :] = pa_m[0]
        for g in range(4):
            cols = slice(g * GW, (g + 1) * GW)
            win = 2 << g
            wsum = e1[H:H + TS, cols]
            for k in range(1, win):
                wsum = wsum + e1[H - k:H - k + TS, cols]
            cnt = jnp.minimum(rows + 1, win).astype(F32)
            d = wsum / cnt - e1[H:H + TS, cols]
            yg = _dot(d.astype(BF16), pw[g].astype(BF16)) * ps[:, cols]
            y_ref[0, :, cols] = yg.astype(BF16)

        e2[0:H, :] = cg_b[0] * xi_b[0] * nb
        e2[H:H + TS, :] = cg_m[0] * xi_m[0]
        cz = sw[0:1, :] * e2[H - 2:H - 2 + TS, :]
        for k in range(1, SCONV_K):
            cz = cz + sw[k:k + 1, :] * e2[H - 2 + k:H - 2 + k + TS, :]
        y_ref[1] = (bg_m[0] * cz).astype(BF16)

        e3[0:H, :] = ca_b[0] * _sig(cb_b[0]) * nb
        e3[H:H + TS, :] = ca_m[0] * _sig(cb_m[0])
        o = H - (CCONV_K - 1)
        y1 = cw[0:1, :] * e3[o:o + TS, :]
        for k in range(1, CCONV_K):
            y1 = y1 + cw[k:k + 1, :] * e3[o + k:o + k + TS, :]
        yh, _ = _ln_stats(y1)
        y2 = yh * clg[...] + clb[...]
        y_ref[2] = (y2 * _sig(y2)).astype(BF16)

        u, _ = _gelu(du_m[0])
        v, _ = _gelu(dv_m[0])
        vh, _ = _ln_stats(v)
        vn = vh * slg[...] + slb[...]
        mask, _ = _sgu_masks()
        for h in range(4):
            wm = jnp.where(mask, gw[h], 0.0).astype(BF16)
            cs = slice(h * GW, (h + 1) * GW)
            for n in range(TS // SGU_BLOCK):
                rs = slice(n * SGU_BLOCK, (n + 1) * SGU_BLOCK)
                z = _dot(wm, vn[rs, cs].astype(BF16)) + gbias[h]
                y_ref[3, rs, cs] = (u[rs, cs] * z).astype(BF16)

    args = [proj] * 13 + [pool_w, pool_scale, sconv_w, cconv_w, cln_g, cln_b, sln_g, sln_b, sgu_w, sgu_bias]
    in_specs = [main(0, 0), back(0, 0), main(0, 1), back(0, 1), main(1, 0), main(1, 1), back(1, 1),
                main(2, 0), back(2, 0), main(2, 1), back(2, 1), main(3, 0), main(3, 1)]
    in_specs += [full(a) for a in args[13:]]
    (y,), extra = _pcall(body, name, (S // TS,), in_specs, [BS((4, TS, BW), lambda i: (0, i, 0))],
                         [SDS((4, S, BW), BF16)], [pltpu.VMEM((H + TS, BW), F32)] * 3, ("parallel",), args, comm)
    return y, extra


def _mixers_bwd(proj, dy, dproj_gates, pool_w, pool_wt, pool_scale, sconv_w, cconv_w, cln_g, cln_b, sln_g, sln_b,
                sgu_w, sgu_wt, sgu_bias, name, comm=None):
    _, S, D = proj.shape
    BW = D // 2
    GW = BW // 4
    TS = _tile(S, 256, SGU_BLOCK)
    H = HALO
    hb = TS // H
    n_t = S // TS
    E = TS + H

    def main(blk, col):
        return BS((1, TS, BW), lambda i: (blk, i, col))

    def back(blk, col):
        return BS((1, H, BW), lambda i: (blk, jnp.maximum(i * hb - 1, 0), col))

    def front(blk, col):
        return BS((1, H, BW), lambda i: (blk, jnp.minimum((i + 1) * hb, S // H - 1), col))

    def full(a):
        nd = a.ndim
        return BS(a.shape, lambda i: (0,) * nd)

    def body(pa_b, pa_m, xi_b, xi_m, bg_m, bg_f, cg_b, cg_m, ca_b, ca_m, ca_f, cb_b, cb_m, cb_f, du_m, dv_m,
             dya_m, dya_f, dyb_m, dyb_f, dyc_m, dyc_f, dyd_m,
             pw, pwt, ps, sw, cw, clg, clb, slg, slb, gw, gwt, gbias, _gates_in,
             dp_ref, dpw, dps, dsw, dcw, dclg, dclb, dslg, dslb, dgw, dgb,
             e1, e2, e3, e4, e5, e6):
        i = pl.program_id(0)
        nb = jnp.where(i > 0, 1.0, 0.0).astype(F32)
        nf = jnp.where(i < n_t - 1, 1.0, 0.0).astype(F32)
        rows_m = i * TS + lax.broadcasted_iota(jnp.int32, (TS, 1), 0)
        rows_e = i * TS + lax.broadcasted_iota(jnp.int32, (E, 1), 0)

        @pl.when(i == 0)
        def _():
            for r in (dpw, dps, dsw, dcw, dclg, dclb, dslg, dslb, dgw, dgb):
                r[...] = jnp.zeros(r.shape, F32)

        e1[0:H, :] = pa_b[0] * nb
        e1[H:H + TS, :] = pa_m[0]
        e2[0:TS, :] = dya_m[0] * ps[...]
        e2[TS:E, :] = dya_f[0] * ps[...] * nf
        for g in range(4):
            cols = slice(g * GW, (g + 1) * GW)
            win = 2 << g
            a_m = e1[H:H + TS, cols]
            wsum = a_m
            for k in range(1, win):
                wsum = wsum + e1[H - k:H - k + TS, cols]
            d = wsum / jnp.minimum(rows_m + 1, win).astype(F32) - a_m
            d16 = d.astype(BF16)
            dyp = e2[0:E, cols].astype(BF16)
            dd = _dot(dyp, pwt[g].astype(BF16))
            e3[0:E, cols] = dd / jnp.minimum(rows_e + 1, win).astype(F32)
            da = e3[0:TS, cols] - dd[0:TS]
            for k in range(1, win):
                da = da + e3[k:k + TS, cols]
            dp_ref[0, :, cols] = da.astype(BF16)
            ypre = _dot(d16, pw[g].astype(BF16))
            dps[:, cols] += jnp.sum(dya_m[0][:, cols] * ypre, axis=0, keepdims=True)
            dpw[g] += _dot(jnp.transpose(d).astype(BF16), dyp[0:TS])

        e4[0:H, :] = cg_b[0] * xi_b[0] * nb
        e4[H:H + TS, :] = cg_m[0] * xi_m[0]
        dyb = dyb_m[0]
        e5[0:TS, :] = dyb * bg_m[0]
        e5[TS:E, :] = dyb_f[0] * bg_f[0] * nf
        dcz = e5[0:TS, :]
        cz = None
        dz = None
        for k in range(SCONV_K):
            zk = e4[H - 2 + k:H - 2 + k + TS, :]
            wk = sw[k:k + 1, :]
            cz = wk * zk if cz is None else cz + wk * zk
            t = wk * e5[2 - k:2 - k + TS, :]
            dz = t if dz is None else dz + t
            dsw[k:k + 1, :] += jnp.sum(dcz * zk, axis=0, keepdims=True)
        dp_ref[0, :, BW:2 * BW] = (dz * cg_m[0]).astype(BF16)
        dp_ref[1, :, 0:BW] = (dyb * cz).astype(BF16)
        dp_ref[1, :, BW:2 * BW] = (dz * xi_m[0]).astype(BF16)

        sgm = _sig(cb_m[0])
        e6[0:H, :] = ca_b[0] * _sig(cb_b[0]) * nb
        e6[H:H + TS, :] = ca_m[0] * sgm
        e6[H + TS:H + E, :] = ca_f[0] * _sig(cb_f[0]) * nf
        o = H - (CCONV_K - 1)
        y1 = cw[0:1, :] * e6[o:o + E, :]
        for k in range(1, CCONV_K):
            y1 = y1 + cw[k:k + 1, :] * e6[o + k:o + k + E, :]
        yh, rstd = _ln_stats(y1)
        y2 = yh * clg[...] + clb[...]
        s2 = _sig(y2)
        e1[0:TS, :] = dyc_m[0]
        e1[TS:E, :] = dyc_f[0] * nf
        dy2 = e1[0:E, :] * (s2 * (1.0 + y2 * (1.0 - s2)))
        dclg[...] += jnp.sum((dy2 * yh)[0:TS], axis=0, keepdims=True)
        dclb[...] += jnp.sum(dy2[0:TS], axis=0, keepdims=True)
        e2[0:E, :] = _ln_bwd(dy2 * clg[...], yh, rstd)
        dy1_m = e2[0:TS, :]
        dy0 = None
        for k in range(CCONV_K):
            t = cw[k:k + 1, :] * e2[CCONV_K - 1 - k:CCONV_K - 1 - k + TS, :]
            dy0 = t if dy0 is None else dy0 + t
            dcw[k:k + 1, :] += jnp.sum(dy1_m * e6[o + k:o + k + TS, :], axis=0, keepdims=True)
        dp_ref[2, :, 0:BW] = (dy0 * sgm).astype(BF16)
        dp_ref[2, :, BW:2 * BW] = (dy0 * ca_m[0] * (sgm * (1.0 - sgm))).astype(BF16)

        pu = du_m[0]
        pv = dv_m[0]
        u, tu = _gelu(pu)
        v, tv = _gelu(pv)
        vh, vr = _ln_stats(v)
        vn = vh * slg[...] + slb[...]
        dyd = dyd_m[0]
        mask, mask_t = _sgu_masks()
        for h in range(4):
            wm = jnp.where(mask, gw[h], 0.0).astype(BF16)
            wmt = jnp.where(mask_t, gwt[h], 0.0).astype(BF16)
            cs = slice(h * GW, (h + 1) * GW)
            for n in range(TS // SGU_BLOCK):
                rs = slice(n * SGU_BLOCK, (n + 1) * SGU_BLOCK)
                vb = vn[rs, cs].astype(BF16)
                z = _dot(wm, vb) + gbias[h]
                dzb = dyd[rs, cs] * u[rs, cs]
                dz16 = dzb.astype(BF16)
                e3[rs, cs] = dyd[rs, cs] * z
                e4[rs, cs] = _dot(wmt, dz16)
                dgw[h] += jnp.where(mask, _dot_nt(dz16, vb), 0.0)
                dgb[h] += dzb
        dvn = e4[0:TS, :]
        dslg[...] += jnp.sum(dvn * vh, axis=0, keepdims=True)
        dslb[...] += jnp.sum(dvn, axis=0, keepdims=True)
        dv = _ln_bwd(dvn * slg[...], vh, vr)
        dp_ref[3, :, 0:BW] = (e3[0:TS, :] * _gelu_grad(pu, tu)).astype(BF16)
        dp_ref[3, :, BW:2 * BW] = (dv * _gelu_grad(pv, tv)).astype(BF16)

        @pl.when(i == n_t - 1)
        def _():
            for h in range(4):
                dgb[h] = jnp.broadcast_to(jnp.sum(dgb[h], axis=1, keepdims=True), dgb.shape[1:])

    params = [pool_w, pool_wt, pool_scale, sconv_w, cconv_w, cln_g, cln_b, sln_g, sln_b, sgu_w, sgu_wt, sgu_bias]
    args = [proj] * 16 + [dy] * 7 + params + [dproj_gates]
    in_specs = [back(0, 0), main(0, 0), back(0, 1), main(0, 1), main(1, 0), front(1, 0), back(1, 1), main(1, 1),
                back(2, 0), main(2, 0), front(2, 0), back(2, 1), main(2, 1), front(2, 1), main(3, 0), main(3, 1),
                main(0, 0), front(0, 0), main(1, 0), front(1, 0), main(2, 0), front(2, 0), main(3, 0)]
    in_specs += [full(a) for a in params] + [ANY]
    small = [SDS(pool_w.shape, F32), SDS(pool_scale.shape, F32), SDS(sconv_w.shape, F32), SDS(cconv_w.shape, F32),
             SDS(cln_g.shape, F32), SDS(cln_b.shape, F32), SDS(sln_g.shape, F32), SDS(sln_b.shape, F32),
             SDS(sgu_w.shape, F32), SDS(sgu_bias.shape, F32)]
    out_specs = [BS((4, TS, D), lambda i: (0, i, 0))] + [full(s) for s in small]
    return _pcall(body, name, (n_t,), in_specs, out_specs, [SDS(dproj_gates.shape, BF16)] + small,
                  [pltpu.VMEM((TS + 2 * H, BW), F32)] * 6, ("arbitrary",), args, comm, aliases={len(args) - 1: 0})


def _merge_fwd(y, proj, w_up, w_out, x, name, comm=None):
    _, S, BW = y.shape
    D = x.shape[1]
    tm = _tile(S, 256, 16)

    def body(y_ref, pg_ref, wu_ref, wo_ref, x_ref, o_ref, m_ref):
        merged = None
        for g in range(4):
            t = _sig(pg_ref[g]) * _dot(y_ref[g], wu_ref[g])
            merged = t if merged is None else merged + t
        m16 = merged.astype(BF16)
        m_ref[...] = m16
        o_ref[...] = x_ref[...] + _dot(m16, wo_ref[...])

    (o, m), extra = _pcall(
        body, name, (S // tm,),
        [BS((4, tm, BW), lambda i: (0, i, 0)), BS((4, tm, D), lambda i: (1, i, 0)),
         BS((4, BW, D), lambda i: (0, 0, 0)), BS((D, D), lambda i: (0, 0)), BS((tm, D), lambda i: (i, 0))],
        [BS((tm, D), lambda i: (i, 0)), BS((tm, D), lambda i: (i, 0))],
        [SDS((S, D), F32), SDS((S, D), BF16)], [], ("parallel",), (y, proj, w_up, w_out, x), comm)
    return o, m, extra


def _merge_bwd(dx, y, proj, w_up, w_out, name, comm=None):
    _, S, BW = y.shape
    D = dx.shape[1]
    tm = _tile(S, 256, 16)

    def body(dx_ref, y_ref, pg_ref, wu_ref, wo_ref, dup_ref, dp_ref, dy_ref):
        dm = _dot_nt(dx_ref[...].astype(BF16), wo_ref[...])
        for g in range(4):
            gate = _sig(pg_ref[g])
            up = _dot(y_ref[g], wu_ref[g])
            dup = (dm * gate).astype(BF16)
            dup_ref[g] = dup
            dp_ref[g] = (dm * up * (gate * (1.0 - gate))).astype(BF16)
            dy_ref[g] = _dot_nt(dup, wu_ref[g])

    res, extra = _pcall(
        body, name, (S // tm,),
        [BS((tm, D), lambda i: (i, 0)), BS((4, tm, BW), lambda i: (0, i, 0)), BS((4, tm, D), lambda i: (1, i, 0)),
         BS((4, BW, D), lambda i: (0, 0, 0)), BS((D, D), lambda i: (0, 0))],
        [BS((4, tm, D), lambda i: (0, i, 0)), BS((4, tm, D), lambda i: (1, i, 0)), BS((4, tm, BW), lambda i: (0, i, 0))],
        [SDS((4, S, D), BF16), SDS((8, S, D), BF16), SDS((4, S, BW), F32)], [], ("parallel",),
        (dx, y, proj, w_up, w_out), comm)
    return res, extra


def _adamw(w, g, m, v):
    m = ADAM_B1 * m + (1.0 - ADAM_B1) * g
    v = ADAM_B2 * v + (1.0 - ADAM_B2) * (g * g)
    m_hat = m / (1.0 - ADAM_B1 ** ADAM_STEP)
    v_hat = v / (1.0 - ADAM_B2 ** ADAM_STEP)
    delta = -ADAM_LR * (m_hat / (jnp.sqrt(v_hat) + ADAM_EPS) + ADAM_WD * w)
    return delta, m, v


def _adamw_sharded(parts, w, m, v, name, comm=None):
    L, R, C = w.shape
    tr = _tile(R, 256, 16)

    def body(*refs):
        p_refs = refs[:L]
        w_ref, m_ref, v_ref, g_out, d_out, m_out, v_out = refs[L:]
        l = pl.program_id(0)
        g = None
        for d in range(N_DEV):
            t = p_refs[0][d].astype(F32)
            for j in range(1, L):
                t = jnp.where(l == j, p_refs[j][d].astype(F32), t)
            g = t if g is None else g + t
        dl, mn, vn = _adamw(w_ref[0], g, m_ref[0], v_ref[0])
        g_out[0] = g
        d_out[0] = dl
        m_out[0] = mn
        v_out[0] = vn

    def part_spec(j):
        return BS((N_DEV, tr, C), lambda l, r: (0, jnp.where(l == j, r, 0), 0))

    blk = BS((1, tr, C), lambda l, r: (l, r, 0))
    return _pcall(body, name, (L, R // tr), [part_spec(j) for j in range(L)] + [blk, blk, blk], [blk] * 4,
                  [SDS((L, R, C), F32)] * 4, [], ("parallel", "parallel"), (*parts, w, m, v), comm)


def _adamw_replicated(gathered, layout, wmv, name):
    n_b = len(gathered)
    n_p = len(layout)

    def body(*refs):
        bufs = refs[:n_b]
        prm = refs[n_b:n_b + 3 * n_p]
        outs = refs[n_b + 3 * n_p:n_b + 7 * n_p]
        sums = refs[n_b + 7 * n_p:]
        for b in range(n_b):
            s = bufs[b][0]
            for d in range(1, N_DEV):
                s = s + bufs[b][d]
            sums[b][...] = s
        for p, (b, r0, nr) in enumerate(layout):
            g = sums[b][r0:r0 + nr, :]
            d, mn, vn = _adamw(prm[3 * p][...], g, prm[3 * p + 1][...], prm[3 * p + 2][...])
            outs[4 * p][...] = g
            outs[4 * p + 1][...] = d
            outs[4 * p + 2][...] = mn
            outs[4 * p + 3][...] = vn

    flat = [a for t in wmv for a in t]
    out_shape = []
    for (w, _, _) in wmv:
        out_shape += [SDS(w.shape, F32)] * 4
    out_shape += [SDS(g.shape[1:], F32) for g in gathered]
    return pl.pallas_call(
        body, name=name, out_shape=out_shape,
        compiler_params=pltpu.CompilerParams(vmem_limit_bytes=V7X_VMEM_LIMIT),
    )(*gathered, *flat)


def _adamw_small(g, w, m, v, name):
    def body(g_ref, w_ref, m_ref, v_ref, d_out, m_out, v_out):
        d, mn, vn = _adamw(w_ref[...], g_ref[...], m_ref[...], v_ref[...])
        d_out[...] = d
        m_out[...] = mn
        v_out[...] = vn

    return pl.pallas_call(body, name=name, out_shape=[SDS(w.shape, F32)] * 3)(g, w, m, v)


def _pad_rows(a, rows):
    return jnp.pad(a, ((0, rows - a.shape[0]), (0, 0)))


def kernel(x, ffn1_norm, ffn1_w13, ffn1_w2, mix_norm, w_in, pool_w, pool_scale, sconv_w, cconv_w, cconv_ln_g, cconv_ln_b, sgu_ln_g, sgu_ln_b, sgu_w, sgu_b, w_up, w_out, ffn2_norm, ffn2_w13, ffn2_w2, final_norm, loss_target, m_ffn1_norm, m_ffn1_w13, m_ffn1_w2, m_mix_norm, m_w_in, m_pool_w, m_pool_scale, m_sconv_w, m_cconv_w, m_cconv_ln_g, m_cconv_ln_b, m_sgu_ln_g, m_sgu_ln_b, m_sgu_w, m_sgu_b, m_w_up, m_w_out, m_ffn2_norm, m_ffn2_w13, m_ffn2_w2, m_final_norm, v_ffn1_norm, v_ffn1_w13, v_ffn1_w2, v_mix_norm, v_w_in, v_pool_w, v_pool_scale, v_sconv_w, v_cconv_w, v_cconv_ln_g, v_cconv_ln_b, v_sgu_ln_g, v_sgu_ln_b, v_sgu_w, v_sgu_b, v_w_up, v_w_out, v_ffn2_norm, v_ffn2_w13, v_ffn2_w2, v_final_norm):
    P = dict(locals())
    L = ffn1_norm.shape[0]
    S, D = x.shape[1], x.shape[2]
    BW = D // 2
    GW = BW // 4
    F = ffn1_w2.shape[1] * N_DEV
    fs = ffn1_w13.shape[2]
    cw = sconv_w.shape[2]
    me = 4 * lax.axis_index("x") + 2 * lax.axis_index("y") + lax.axis_index("c")

    big = ["ffn1_w13", "ffn1_w2", "w_in", "w_up", "w_out", "ffn2_w13", "ffn2_w2"]
    shards = [P[n].astype(BF16) for n in big]
    conv_local = jnp.concatenate([sconv_w, cconv_w], axis=1)

    def gather_of(units):
        return _gather_comm(shards, [(big.index(n), l) for n, l in units])

    def ready(n, g):
        if n.endswith("w13"):
            return jnp.transpose(g.reshape(2, 4, D, fs), (0, 2, 1, 3)).reshape(2, D, F)
        if n.endswith("w2"):
            return g.reshape(F, D)
        if n == "w_up":
            return jnp.transpose(g, (1, 2, 0, 3)).reshape(4, BW, D)
        if n == "w_out":
            return g.reshape(D, D)
        return g

    W = {}

    def take(units, arrays):
        for (n, l), g in zip(units, arrays):
            W[n, l] = ready(n, g)

    first_units = [("ffn1_w13", 0), ("ffn1_w2", 0)]
    plan = {("ffn1_up", 0): [("w_in", 0)],
            ("ffn1_down", 0): [("w_up", 0), ("w_out", 0)],
            ("proj", 0): [("ffn2_w13", 0), ("ffn2_w2", 0)],
            ("mixers", 0): [("ffn1_w13", 1)], ("merge", 0): [("ffn1_w2", 1)],
            ("ffn2_up", 0): [("w_in", 1)], ("ffn2_down", 0): [("w_up", 1), ("w_out", 1)],
            ("ffn1_up", 1): [("ffn2_w13", 1)], ("ffn1_down", 1): [("ffn2_w2", 1)]}
    assert L <= 2

    def carried(key):
        units = [u for u in plan.get(key, []) if u[1] < L]
        return units, (gather_of(units) if units else None)

    first = _gather_comm(shards + [conv_local], [(big.index(n), l) for n, l in first_units] + [(len(big), None)])
    got = _run_comm(first, "gather_first_weights")
    take(first_units, got[:2])
    conv_full = jnp.transpose(got[2], (1, 2, 0, 3)).reshape(L, SCONV_K + CCONV_K, N_DEV * cw)
    sconv_full = conv_full[:, :SCONV_K]
    cconv_full = conv_full[:, SCONV_K:]

    sgu_bias = jnp.broadcast_to(sgu_b[:, :, :, None], sgu_b.shape + (GW,))
    pool_wt = jnp.swapaxes(pool_w, 2, 3)
    sgu_wt = jnp.swapaxes(sgu_w, 2, 3)

    def row(a, l):
        return a[l][None, :]

    saved = []
    xc = x[0]
    for l in range(L):
        sv = {}
        for tag in ("ffn1", None, "ffn2"):
            if tag is None:
                sv["x_mix"] = xc
                h = _rmsnorm_fwd(xc, row(mix_norm, l), "mix_norm_fwd")
                units, comm = carried(("proj", l))
                proj, extra = _matmul_fwd(h, W["w_in", l], "proj_fwd", comm)
                take(units, extra)
                units, comm = carried(("mixers", l))
                y, extra = _mixers_fwd(proj, pool_w[l], row(pool_scale, l), sconv_full[l], cconv_full[l],
                                       row(cconv_ln_g, l), row(cconv_ln_b, l), row(sgu_ln_g, l), row(sgu_ln_b, l),
                                       sgu_w[l], sgu_bias[l], "mixers_fwd", comm)
                take(units, extra)
                units, comm = carried(("merge", l))
                xc, merged, extra = _merge_fwd(y, proj, W["w_up", l], W["w_out", l], xc, "merge_fwd", comm)
                take(units, extra)
                sv.update(h_mix=h, proj=proj, y=y, merged=merged)
            else:
                sv["x_" + tag] = xc
                h = _rmsnorm_fwd(xc, row(P[tag + "_norm"], l), "ffn_norm_fwd")
                units, comm = carried((tag + "_up", l))
                ab, extra = _matmul_fwd(h, W[tag + "_w13", l], "ffn_up_fwd", comm)
                take(units, extra)
                units, comm = carried((tag + "_down", l))
                xc, extra = _swiglu_down(ab, W[tag + "_w2", l], xc, "ffn_down_fwd", comm)
                take(units, extra)
                sv.update({"h_" + tag: h, "ab_" + tag: ab})
        saved.append(sv)

    loss_part, dx, d_final = _final_loss(xc, final_norm[None, :], loss_target[0], "loss_head")
    loss = lax.psum(loss_part[0, 0], MESH_AXES)

    R = {}
    second = []

    def rest_of_sends():
        keys = [k for k, _, _ in second]
        comm = None
        if second:
            comm = _scatter_comm([g for _, g, _ in second], PEERS_OTHER_CORE, [r for _, _, r in second])
        second.clear()
        return keys, comm

    def settle(keys, arrays):
        for k, a in zip(keys, arrays):
            R[k] = a

    small_g = [None] * L
    for l in reversed(range(L)):
        sv = saved[l]
        sg = {}
        for tag in ("ffn2", None, "ffn1"):
            if tag is None:
                keys, comm = rest_of_sends()
                (dup, dproj, dy), extra = _merge_bwd(dx, sv["y"], sv["proj"], W["w_up", l], W["w_out", l],
                                                     "merge_bwd", comm)
                settle(keys, extra)
                g_out, _ = _matmul_tn(sv["merged"][None], dx[None], 1, "w_out_grad")
                g_up, _ = _matmul_tn(sv["y"], dup, 1, "w_up_grad")
                g_out = g_out.reshape(N_DEV, D // N_DEV, D)
                g_up = jnp.transpose(g_up.reshape(4, BW, N_DEV, D // N_DEV), (2, 0, 1, 3)).reshape(
                    N_DEV, 4 * BW, D // N_DEV)
                res, (R["w_out", l], R["w_up", l]) = _mixers_bwd(
                    sv["proj"], dy, dproj, pool_w[l], pool_wt[l], row(pool_scale, l), sconv_full[l], cconv_full[l],
                    row(cconv_ln_g, l), row(cconv_ln_b, l), row(sgu_ln_g, l), row(sgu_ln_b, l), sgu_w[l], sgu_wt[l],
                    sgu_bias[l], "mixers_bwd", _scatter_comm([g_out, g_up]))
                dproj = res[0]
                (sg["pool_w"], sg["pool_scale"], sg["sconv_w"], sg["cconv_w"], sg["cconv_ln_g"], sg["cconv_ln_b"],
                 sg["sgu_ln_g"], sg["sgu_ln_b"], sg["sgu_w"], dgb) = res[1:]
                sg["sgu_b"] = dgb[:, :, 0]
                g_in, _ = _matmul_tn(sv["h_mix"][None], dproj, N_DEV, "w_in_grad")
                dx, sg["mix_norm"], (r_in,) = _matmul_nt_normbwd(
                    dproj, W["w_in", l], sv["x_mix"], row(mix_norm, l), dx, "proj_bwd",
                    _scatter_comm([g_in], PEERS_SAME_CORE))
                second.append((("w_in", l), g_in, r_in))
            else:
                keys, comm = rest_of_sends()
                dab, sh, extra = _ffn_bwd_hidden(dx, W[tag + "_w2", l], sv["ab_" + tag], "ffn_hidden_bwd", comm)
                settle(keys, extra)
                g_w2, _ = _matmul_tn(sh[None], dx[None], 1, "ffn_w2_grad")
                g_w2 = g_w2.reshape(N_DEV, F // N_DEV, D)
                g_w13, (R[tag + "_w2", l],) = _matmul_tn(sv["h_" + tag][None], dab, 2, "ffn_w13_grad",
                                                         _scatter_comm([g_w2]))
                g_w13 = jnp.transpose(g_w13.reshape(2, D, 4, fs), (0, 2, 1, 3)).reshape(N_DEV, D, fs)
                dx, sg[tag + "_norm"], (r_w13,) = _matmul_nt_normbwd(
                    dab, W[tag + "_w13", l], sv["x_" + tag], row(P[tag + "_norm"], l), dx, "ffn_up_bwd",
                    _scatter_comm([g_w13], PEERS_SAME_CORE))
                second.append(((tag + "_w13", l), g_w13, r_w13))
        small_g[l] = sg
    grad_x = dx[None]
    out = {}

    wide = ["ffn1_norm", "mix_norm", "ffn2_norm", "final_norm"]
    half = ["pool_scale", "cconv_ln_g", "cconv_ln_b", "sgu_ln_g", "sgu_ln_b"]
    narrow = ["pool_w", "sgu_w", "sgu_b"]

    def stack_layers(n):
        return jnp.stack([small_g[l][n] for l in range(L)], axis=0)

    def as2d(n, a):
        if n == "final_norm":
            return a.reshape(1, D)
        return a.reshape(-1, a.shape[-1])

    sg2 = {n: stack_layers(n).reshape(-1, stack_layers(n).shape[-1]) for n in wide[:3] + half + narrow}
    sg2["final_norm"] = d_final
    conv_g = jnp.concatenate([stack_layers("sconv_w"), stack_layers("cconv_w")], axis=1)
    conv_g = conv_g.reshape(L * (SCONV_K + CCONV_K), N_DEV * cw)
    small_names = wide + half + narrow
    widths = []
    for n in small_names:
        if sg2[n].shape[1] not in widths:
            widths.append(sg2[n].shape[1])
    layout = {}
    bufs = []
    for b, width in enumerate(widths):
        parts = []
        r0 = 0
        for n in small_names:
            if sg2[n].shape[1] != width:
                continue
            nr = sg2[n].shape[0]
            pr = -(-nr // 8) * 8
            layout[n] = (b, r0, nr)
            parts.append(_pad_rows(sg2[n], pr))
            r0 += pr
        if width == conv_g.shape[1]:
            conv_b, conv_r0 = b, r0
            parts.append(_pad_rows(conv_g, -(-conv_g.shape[0] // 8) * 8))
        bufs.append(jnp.concatenate(parts, axis=0))

    gathered_small = None
    for i, n in enumerate(["ffn2_w13", "w_in", "ffn2_w2", "w_up", "w_out", "ffn1_w2", "ffn1_w13"]):
        shp = P[n].shape
        rows, cols = math.prod(shp[1:-1]), shp[-1]
        flat = lambda a: a.reshape(L, rows, cols)
        keys, comm = [], None
        if i == 0:
            keys, comm = rest_of_sends()
        elif i == 1:
            comm = _gather_comm(bufs, [(b, None) for b in range(len(bufs))])
        res, extra = _adamw_sharded([R[n, l] for l in range(L)], flat(P[n]), flat(P["m_" + n]), flat(P["v_" + n]),
                                    "adamw_sharded", comm)
        if i == 0:
            settle(keys, extra)
        elif i == 1:
            gathered_small = extra
        out[n] = tuple(a.reshape(shp) for a in res)

    res = _adamw_replicated(gathered_small, [layout[n] for n in small_names],
                            [(as2d(n, P[n]), as2d(n, P["m_" + n]), as2d(n, P["v_" + n])) for n in small_names],
                            "adamw_replicated")
    for p, n in enumerate(small_names):
        out[n] = tuple(a.reshape(P[n].shape) for a in res[4 * p:4 * p + 4])
    conv_sum = res[4 * len(small_names) + conv_b][conv_r0:conv_r0 + conv_g.shape[0]]
    conv_mine = lax.dynamic_slice_in_dim(conv_sum, me * cw, cw, axis=1)

    def conv2d(a, b):
        return jnp.concatenate([a, b], axis=1).reshape(L * (SCONV_K + CCONV_K), cw)

    cd, cm, cv = _adamw_small(conv_mine, conv2d(sconv_w, cconv_w), conv2d(m_sconv_w, m_cconv_w),
                              conv2d(v_sconv_w, v_cconv_w), "adamw_conv")
    for n, sl in (("sconv_w", slice(0, SCONV_K)), ("cconv_w", slice(SCONV_K, SCONV_K + CCONV_K))):
        out[n] = tuple(a.reshape(L, SCONV_K + CCONV_K, cw)[:, sl] for a in (conv_mine, cd, cm, cv))

    order = ["ffn1_norm", "ffn1_w13", "ffn1_w2", "mix_norm", "w_in", "pool_w", "pool_scale", "sconv_w", "cconv_w",
             "cconv_ln_g", "cconv_ln_b", "sgu_ln_g", "sgu_ln_b", "sgu_w", "sgu_b", "w_up", "w_out", "ffn2_norm",
             "ffn2_w13", "ffn2_w2", "final_norm"]
    return (loss, grad_x, *[out[n][0] for n in order], *[out[n][1] for n in order],
            *[out[n][2] for n in order], *[out[n][3] for n in order])
```

```python
import functools
import math

import jax
import jax.numpy as jnp
from jax import lax
from jax.experimental import pallas as pl
from jax.experimental.pallas import tpu as pltpu

F32 = jnp.float32
BF16 = jnp.bfloat16
EPS = 1e-6
ADAM_LR = 0.001
ADAM_B1 = 0.9
ADAM_B2 = 0.999
ADAM_EPS = 1e-08
ADAM_WD = 0.01
ADAM_STEP = 10
SGU_BLOCK = 128
SGU_CHUNK = 64
SCONV_K = 3
CCONV_K = 31
HALO = 32
V7X_VMEM_LIMIT = 48 * 1024 * 1024
MESH_AXES = ("x", "y", "c")
N_DEV = 8
_GELU_C0 = math.sqrt(2.0 / math.pi)
_GELU_C1 = 0.044715

BS = pl.BlockSpec
SDS = jax.ShapeDtypeStruct
ANY = pl.BlockSpec(memory_space=pl.ANY)


def _tile(n, pref, align=128):
    if n <= pref:
        return n
    t = pref - pref % align
    while t > 0:
        if n % t == 0:
            return t
        t -= align
    return n


def _sig(v):
    return 1.0 / (1.0 + jnp.exp(-v))


def _gelu(v):
    t = jnp.tanh(_GELU_C0 * (v + _GELU_C1 * (v * v * v)))
    return 0.5 * v * (1.0 + t), t


def _gelu_grad(v, t):
    return 0.5 * (1.0 + t) + 0.5 * v * (1.0 - t * t) * (_GELU_C0 * (1.0 + 3.0 * _GELU_C1 * v * v))


def _ln_stats(v):
    mu = jnp.mean(v, axis=-1, keepdims=True)
    vc = v - mu
    var = jnp.mean(vc * vc, axis=-1, keepdims=True)
    rstd = lax.rsqrt(var + EPS)
    return vc * rstd, rstd


def _ln_bwd(dvh, vh, rstd):
    return rstd * (dvh - jnp.mean(dvh, axis=-1, keepdims=True) - vh * jnp.mean(dvh * vh, axis=-1, keepdims=True))


def _dot(a, b):
    return jnp.dot(a, b, preferred_element_type=F32)


def _dot_nt(a, b):
    return lax.dot_general(a, b, (((1,), (1,)), ((), ())), preferred_element_type=F32)


def _dot_tn(a, b):
    return lax.dot_general(a, b, (((0,), (0,)), ((), ())), preferred_element_type=F32)


def _mesh_pos():
    return lax.axis_index("x"), lax.axis_index("y"), lax.axis_index("c")


class _Comm:
    def __init__(self, ins, out_shape, sems, start, finish, aliases=None):
        self.ins, self.out_shape, self.sems, self.start, self.finish = ins, out_shape, sems, start, finish
        self.aliases = aliases or {}


def _gather_comm(shards, units):
    n_u = len(units)

    def tools(ins, dsts, sems):
        send_sems, recv_sems, local_sems = sems
        x, y, c = _mesh_pos()
        chips = [(1 - x, y), (x, 1 - y), (1 - x, 1 - y)]

        def src_of(o):
            t, l = units[o]
            return ins[t] if l is None else ins[t].at[l]

        def row(o, p):
            return dsts[o].at[4 * p[0] + 2 * p[1] + p[2]]

        def copy(o, k, block, to, own=False):
            return pltpu.make_async_remote_copy(
                src_ref=src_of(o) if own else row(o, block), dst_ref=row(o, block),
                send_sem=send_sems.at[o * 7 + k], recv_sem=recv_sems.at[o * 7 + k],
                device_id=to, device_id_type=pl.DeviceIdType.MESH)

        def local(o):
            return pltpu.make_async_copy(src_of(o), row(o, (x, y, c)), local_sems.at[o])

        def first(o):
            return [copy(o, 1 + j, (x, y, c), (*chip, c), own=True) for j, chip in enumerate(chips)] + [
                copy(o, 0, (x, y, c), (x, y, 1 - c), own=True)]

        return (x, y, c), chips, copy, local, first

    def start(ins, dsts, sems):
        _, _, _, local, first = tools(ins, dsts, sems)
        for o in range(n_u):
            local(o).start()
            for cp in first(o):
                cp.start()

    def finish(ins, dsts, sems):
        (x, y, c), chips, copy, local, first = tools(ins, dsts, sems)
        me, sibling = (x, y, c), (x, y, 1 - c)
        passed = []
        for o in range(n_u):
            for j, chip in enumerate(chips):
                copy(o, 1 + j, (*chip, c), me).wait_recv()
                cp = copy(o, 4 + j, (*chip, c), sibling)
                cp.start()
                passed.append(cp)
        for o in range(n_u):
            copy(o, 0, sibling, me).wait_recv()
            for j, chip in enumerate(chips):
                copy(o, 4 + j, (*chip, 1 - c), me).wait_recv()
        for o in range(n_u):
            for cp in first(o):
                cp.wait_send()
        for cp in passed:
            cp.wait_send()
        for o in range(n_u):
            local(o).wait()

    out_shape = []
    for t, l in units:
        shp = shards[t].shape if l is None else shards[t].shape[1:]
        out_shape.append(SDS((N_DEV,) + tuple(shp), shards[t].dtype))
    sems = [pltpu.SemaphoreType.DMA((7 * n_u,)), pltpu.SemaphoreType.DMA((7 * n_u,)), pltpu.SemaphoreType.DMA((n_u,))]
    return _Comm(list(shards), out_shape, sems, start, finish)


PEERS_ALL = (1, 2, 3, 4, 5, 6, 7)
PEERS_SAME_CORE = (1, 2, 4, 6)
PEERS_OTHER_CORE = (3, 5, 7)


def _scatter_comm(parts, peers=PEERS_ALL, into=None):
    n_u = len(parts)

    def tools(ins, dsts, sems):
        send_sems, recv_sems, local_sems = sems
        x, y, c = _mesh_pos()
        me = 4 * x + 2 * y + c

        def peer(k):
            return ((x + ((k >> 2) & 1)) % 2, (y + ((k >> 1) & 1)) % 2, (c + (k & 1)) % 2)

        def copy(u, k, wait=False):
            p = peer(k)
            pi = 4 * p[0] + 2 * p[1] + p[2]
            return pltpu.make_async_remote_copy(
                src_ref=ins[u].at[pi], dst_ref=dsts[u].at[pi if wait else me],
                send_sem=send_sems.at[u * 7 + k - 1], recv_sem=recv_sems.at[u * 7 + k - 1],
                device_id=p, device_id_type=pl.DeviceIdType.MESH)

        def local(u):
            return pltpu.make_async_copy(ins[u].at[me], dsts[u].at[me], local_sems.at[u])

        return copy, local

    def start(ins, dsts, sems):
        copy, local = tools(ins, dsts, sems)
        for u in range(n_u):
            if into is None:
                local(u).start()
            for k in peers:
                copy(u, k).start()

    def finish(ins, dsts, sems):
        copy, local = tools(ins, dsts, sems)
        for u in range(n_u):
            for k in peers:
                copy(u, k, wait=True).wait()
            if into is None:
                local(u).wait()

    sems = [pltpu.SemaphoreType.DMA((7 * n_u,)), pltpu.SemaphoreType.DMA((7 * n_u,)), pltpu.SemaphoreType.DMA((n_u,))]
    aliases = {} if into is None else {n_u + u: u for u in range(n_u)}
    return _Comm(list(parts) + list(into or []), [SDS(p.shape, p.dtype) for p in parts], sems, start, finish, aliases)


def _run_comm(comm, name):
    n_i, n_o = len(comm.ins), len(comm.out_shape)

    def body(*refs):
        comm.start(refs[:n_i], refs[n_i:n_i + n_o], refs[n_i + n_o:])
        comm.finish(refs[:n_i], refs[n_i:n_i + n_o], refs[n_i + n_o:])

    return pl.pallas_call(
        body, name=name, out_shape=comm.out_shape, in_specs=[ANY] * n_i, out_specs=[ANY] * n_o,
        scratch_shapes=comm.sems,
    )(*comm.ins)


def _pcall(body, name, grid, in_specs, out_specs, out_shape, scratch, sem, args, comm=None, aliases=None):
    n_i, n_o, n_s = len(in_specs), len(out_specs), len(scratch)
    aliases = aliases or {}
    if comm is None:
        res = pl.pallas_call(
            body, name=name, grid=grid, in_specs=in_specs, out_specs=out_specs, out_shape=out_shape,
            scratch_shapes=scratch, input_output_aliases=aliases,
            compiler_params=pltpu.CompilerParams(dimension_semantics=sem, vmem_limit_bytes=V7X_VMEM_LIMIT),
        )(*args)
        return res, []
    n_ci, n_co = len(comm.ins), len(comm.out_shape)

    def wrapped(*refs):
        ins = refs[:n_i]
        cins = refs[n_i:n_i + n_ci]
        outs = refs[n_i + n_ci:n_i + n_ci + n_o]
        couts = refs[n_i + n_ci + n_o:n_i + n_ci + n_o + n_co]
        rest = refs[n_i + n_ci + n_o + n_co:]
        ids = [pl.program_id(d) for d in range(len(grid))]
        first = functools.reduce(jnp.logical_and, [i == 0 for i in ids])
        last = functools.reduce(jnp.logical_and, [i == g - 1 for i, g in zip(ids, grid)])

        @pl.when(first)
        def _():
            comm.start(cins, couts, rest[n_s:])

        body(*ins, *outs, *rest[:n_s])

        @pl.when(last)
        def _():
            comm.finish(cins, couts, rest[n_s:])

    res = pl.pallas_call(
        wrapped, name=name, grid=grid, in_specs=list(in_specs) + [ANY] * n_ci,
        out_specs=list(out_specs) + [ANY] * n_co, out_shape=list(out_shape) + list(comm.out_shape),
        scratch_shapes=list(scratch) + list(comm.sems),
        input_output_aliases={**aliases, **{n_i + ci: n_o + co for ci, co in comm.aliases.items()}},
        compiler_params=pltpu.CompilerParams(dimension_semantics=("arbitrary",) * len(grid),
                                             vmem_limit_bytes=V7X_VMEM_LIMIT),
    )(*args, *comm.ins)
    return res[:n_o], res[n_o:]


def _rmsnorm_fwd(x, g, name):
    S, D = x.shape
    tm = _tile(S, 512, 16)

    def body(x_ref, g_ref, h_ref):
        xv = x_ref[...]
        r = lax.rsqrt(jnp.mean(xv * xv, axis=-1, keepdims=True) + EPS)
        h_ref[...] = (xv * r * g_ref[...]).astype(BF16)

    (h,), _ = _pcall(body, name, (S // tm,), [BS((tm, D), lambda i: (i, 0)), BS((1, D), lambda i: (0, 0))],
                     [BS((tm, D), lambda i: (i, 0))], [SDS((S, D), BF16)], [], ("parallel",), (x, g))
    return h


def _matmul_fwd(a, w, name, comm=None, w_t=False, out_dtype=F32):
    S, K = a.shape
    C = w.shape[0]
    Fc = w.shape[1] if w_t else w.shape[2]
    tn = _tile(Fc, 1408)
    tm = _tile(S, 512, 16)

    def body(a_ref, w_ref, o_ref):
        p = _dot_nt(a_ref[...], w_ref[0]) if w_t else _dot(a_ref[...], w_ref[0])
        o_ref[0] = p.astype(out_dtype)

    w_spec = BS((1, tn, K), lambda c, n, i: (c, n, 0)) if w_t else BS((1, K, tn), lambda c, n, i: (c, 0, n))
    (o,), extra = _pcall(
        body, name, (C, Fc // tn, S // tm), [BS((tm, K), lambda c, n, i: (i, 0)), w_spec],
        [BS((1, tm, tn), lambda c, n, i: (c, i, n))], [SDS((C, S, Fc), out_dtype)], [],
        ("parallel", "parallel", "parallel"), (a, w), comm)
    return o, extra


def _swiglu_down(ab, w2, x, name, comm=None):
    _, S, F = ab.shape
    D = w2.shape[1]
    tk = _tile(F, 1408)
    tm = _tile(S, 512, 16)

    def body(ab_ref, w_ref, x_ref, o_ref):
        k = pl.program_id(1)
        a = ab_ref[0].astype(F32)
        s = a * _sig(a) * ab_ref[1].astype(F32)
        p = 0.5 * _dot(s.astype(BF16), w_ref[...])

        @pl.when(k == 0)
        def _():
            o_ref[...] = x_ref[...] + p

        @pl.when(k > 0)
        def _():
            o_ref[...] += p

    (o,), extra = _pcall(
        body, name, (S // tm, F // tk),
        [BS((2, tm, tk), lambda i, k: (0, i, k)), BS((tk, D), lambda i, k: (k, 0)), BS((tm, D), lambda i, k: (i, 0))],
        [BS((tm, D), lambda i, k: (i, 0))], [SDS((S, D), F32)], [], ("parallel", "arbitrary"), (ab, w2, x), comm)
    return o, extra


def _ffn_bwd_hidden(dy, w2, ab, name, comm=None):
    S, D = dy.shape
    F = w2.shape[0]
    tk = _tile(F, 1408)
    tm = _tile(S, 256, 16)

    def body(dy_ref, w_ref, ab_ref, dab_ref, s_ref):
        ds = 0.5 * _dot_nt(dy_ref[...].astype(BF16), w_ref[...])
        a = ab_ref[0].astype(F32)
        b = ab_ref[1].astype(F32)
        sg = _sig(a)
        sa = a * sg
        dab_ref[0] = (ds * b * (sg * (1.0 + a * (1.0 - sg)))).astype(BF16)
        dab_ref[1] = (ds * sa).astype(BF16)
        s_ref[...] = (0.5 * (sa * b)).astype(BF16)

    (dab, sh), extra = _pcall(
        body, name, (F // tk, S // tm),
        [BS((tm, D), lambda k, i: (i, 0)), BS((tk, D), lambda k, i: (k, 0)), BS((2, tm, tk), lambda k, i: (0, i, k))],
        [BS((2, tm, tk), lambda k, i: (0, i, k)), BS((tm, tk), lambda k, i: (i, k))],
        [SDS((2, S, F), BF16), SDS((S, F), BF16)], [], ("parallel", "parallel"), (dy, w2, ab), comm)
    return dab, sh, extra


def _matmul_tn(a, b, n_c, name, comm=None, b_shared=False):
    G, S, M = a.shape
    _, _, Fc = b.shape
    C = n_c
    tM = _tile(M, 1408)
    tn = _tile(Fc, 1408)
    ts = _tile(S, 512, 16)
    n_s = S // ts

    def body(a_ref, b_ref, o_ref, acc):
        s = pl.program_id(4)
        p = _dot_tn(a_ref[0].astype(BF16), b_ref[0].astype(BF16))

        @pl.when(s == 0)
        def _():
            acc[...] = p

        @pl.when(s > 0)
        def _():
            acc[...] += p

        @pl.when(s == n_s - 1)
        def _():
            o_ref[0] = acc[...].astype(BF16)

    (o,), extra = _pcall(
        body, name, (G, M // tM, C, Fc // tn, n_s),
        [BS((1, ts, tM), lambda g, m, c, n, s: (g, s, m)), BS((1, ts, tn), lambda g, m, c, n, s: (c if b_shared else g * C + c, s, n))],
        [BS((1, tM, tn), lambda g, m, c, n, s: (g * C + c, m, n))], [SDS((G * C, M, Fc), BF16)],
        [pltpu.VMEM((tM, tn), F32)], ("parallel", "parallel", "parallel", "parallel", "arbitrary"), (a, b), comm)
    return o, extra


def _matmul_nt_normbwd(b, w, x, gam, dres, name, comm=None, w_t=False):
    C, S, Fc = b.shape
    D = w.shape[2] if w_t else w.shape[1]
    tk = _tile(Fc, 1408)
    tm = _tile(S, 512, 16)
    nk = Fc // tk

    def body(b_ref, w_ref, x_ref, g_ref, r_ref, dx_ref, dg_ref, acc):
        i, c, k = pl.program_id(0), pl.program_id(1), pl.program_id(2)
        p = _dot(b_ref[0], w_ref[0]) if w_t else _dot_nt(b_ref[0], w_ref[0])
        first = jnp.logical_and(c == 0, k == 0)

        @pl.when(first)
        def _():
            acc[...] = p

        @pl.when(jnp.logical_not(first))
        def _():
            acc[...] += p

        @pl.when(jnp.logical_and(c == C - 1, k == nk - 1))
        def _():
            xv = x_ref[...]
            r = lax.rsqrt(jnp.mean(xv * xv, axis=-1, keepdims=True) + EPS)
            xn = xv * r
            dh = acc[...]
            dxn = dh * g_ref[...]
            dx_ref[...] = r_ref[...] + r * (dxn - xn * jnp.mean(dxn * xn, axis=-1, keepdims=True))
            dgp = jnp.sum(dh * xn, axis=0, keepdims=True)

            @pl.when(i == 0)
            def _():
                dg_ref[...] = dgp

            @pl.when(i > 0)
            def _():
                dg_ref[...] += dgp

    (dx, dg), extra = _pcall(
        body, name, (S // tm, C, nk),
        [BS((1, tm, tk), lambda i, c, k: (c, i, k)),
         BS((1, tk, D), lambda i, c, k: (c, k, 0)) if w_t else BS((1, D, tk), lambda i, c, k: (c, 0, k)),
         BS((tm, D), lambda i, c, k: (i, 0)), BS((1, D), lambda i, c, k: (0, 0)), BS((tm, D), lambda i, c, k: (i, 0))],
        [BS((tm, D), lambda i, c, k: (i, 0)), BS((1, D), lambda i, c, k: (0, 0))],
        [SDS((S, D), F32), SDS((1, D), F32)], [pltpu.VMEM((tm, D), F32)],
        ("arbitrary", "arbitrary", "arbitrary"), (b, w, x, gam, dres), comm)
    return dx, dg, extra


def _final_loss(x, gam, target, name):
    S, D = x.shape
    tm = _tile(S, 512, 8)

    def body(x_ref, g_ref, t_ref, loss_ref, dx_ref, dg_ref):
        i = pl.program_id(0)
        xv = x_ref[...]
        r = lax.rsqrt(jnp.mean(xv * xv, axis=-1, keepdims=True) + EPS)
        xn = xv * r
        err = xn * g_ref[...] - t_ref[...]
        part = 0.5 * jnp.sum(jnp.mean(err * err, axis=-1, keepdims=True), axis=0, keepdims=True)
        dy = err * (1.0 / D)
        dxn = dy * g_ref[...]
        dx_ref[...] = r * (dxn - xn * jnp.mean(dxn * xn, axis=-1, keepdims=True))
        dgp = jnp.sum(dy * xn, axis=0, keepdims=True)
        lp = jnp.broadcast_to(part, loss_ref.shape)

        @pl.when(i == 0)
        def _():
            dg_ref[...] = dgp
            loss_ref[...] = lp

        @pl.when(i > 0)
        def _():
            dg_ref[...] += dgp
            loss_ref[...] += lp

    res, _ = _pcall(
        body, name, (S // tm,),
        [BS((tm, D), lambda i: (i, 0)), BS((1, D), lambda i: (0, 0)), BS((tm, D), lambda i: (i, 0))],
        [BS((8, 128), lambda i: (0, 0)), BS((tm, D), lambda i: (i, 0)), BS((1, D), lambda i: (0, 0))],
        [SDS((8, 128), F32), SDS((S, D), F32), SDS((1, D), F32)], [], ("arbitrary",), (x, gam, target))
    return res


def _sgu_masks():
    ii = lax.broadcasted_iota(jnp.int32, (SGU_BLOCK, SGU_BLOCK), 0) // SGU_CHUNK
    jj = lax.broadcasted_iota(jnp.int32, (SGU_BLOCK, SGU_BLOCK), 1) // SGU_CHUNK
    return jj <= ii, ii <= jj


def _mixers_fwd(proj, pool_w, pool_scale, sconv_w, cconv_w, cln_g, cln_b, sln_g, sln_b, sgu_w, sgu_bias, name,
                comm=None):
    _, S, D = proj.shape
    BW = D // 2
    GW = BW // 4
    TS = _tile(S, 256, SGU_BLOCK)
    H = HALO
    hb = TS // H

    def main(blk, col):
        return BS((1, TS, BW), lambda i: (blk, i, col))

    def back(blk, col):
        return BS((1, H, BW), lambda i: (blk, jnp.maximum(i * hb - 1, 0), col))

    def full(a):
        nd = a.ndim
        return BS(a.shape, lambda i: (0,) * nd)

    def body(pa_m, pa_b, xi_m, xi_b, bg_m, cg_m, cg_b, ca_m, ca_b, cb_m, cb_b, du_m, dv_m,
             pw, ps, sw, cw, clg, clb, slg, slb, gw, gbias, y_ref, e1, e2, e3):
        i = pl.program_id(0)
        nb = jnp.where(i > 0, 1.0, 0.0).astype(F32)
        rows = i * TS + lax.broadcasted_iota(jnp.int32, (TS, 1), 0)

        e1[0:H, :] = pa_b[0] * nb
        e1[H:H + TS, :] = pa_m[0]
        for g in range(4):
            cols = slice(g * GW, (g + 1) * GW)
            win = 2 << g
            wsum = e1[H:H + TS, cols]
            for k in range(1, win):
                wsum = wsum + e1[H - k:H - k + TS, cols]
            cnt = jnp.minimum(rows + 1, win).astype(F32)
            d = wsum / cnt - e1[H:H + TS, cols]
            yg = _dot(d.astype(BF16), pw[g].astype(BF16)) * ps[:, cols]
            y_ref[0, :, cols] = yg.astype(BF16)

        e2[0:H, :] = cg_b[0] * xi_b[0] * nb
        e2[H:H + TS, :] = cg_m[0] * xi_m[0]
        cz = sw[0:1, :] * e2[H - 2:H - 2 + TS, :]
        for k in range(1, SCONV_K):
            cz = cz + sw[k:k + 1, :] * e2[H - 2 + k:H - 2 + k + TS, :]
        y_ref[1] = (bg_m[0] * cz).astype(BF16)

        e3[0:H, :] = ca_b[0] * _sig(cb_b[0]) * nb
        e3[H:H + TS, :] = ca_m[0] * _sig(cb_m[0])
        o = H - (CCONV_K - 1)
        y1 = cw[0:1, :] * e3[o:o + TS, :]
        for k in range(1, CCONV_K):
            y1 = y1 + cw[k:k + 1, :] * e3[o + k:o + k + TS, :]
        yh, _ = _ln_stats(y1)
        y2 = yh * clg[...] + clb[...]
        y_ref[2] = (y2 * _sig(y2)).astype(BF16)

        u, _ = _gelu(du_m[0])
        v, _ = _gelu(dv_m[0])
        vh, _ = _ln_stats(v)
        vn = vh * slg[...] + slb[...]
        mask, _ = _sgu_masks()
        for h in range(4):
            wm = jnp.where(mask, gw[h], 0.0).astype(BF16)
            cs = slice(h * GW, (h + 1) * GW)
            for n in range(TS // SGU_BLOCK):
                rs = slice(n * SGU_BLOCK, (n + 1) * SGU_BLOCK)
                z = _dot(wm, vn[rs, cs].astype(BF16)) + gbias[h]
                y_ref[3, rs, cs] = (u[rs, cs] * z).astype(BF16)

    args = [proj] * 13 + [pool_w, pool_scale, sconv_w, cconv_w, cln_g, cln_b, sln_g, sln_b, sgu_w, sgu_bias]
    in_specs = [main(0, 0), back(0, 0), main(0, 1), back(0, 1), main(1, 0), main(1, 1), back(1, 1),
                main(2, 0), back(2, 0), main(2, 1), back(2, 1), main(3, 0), main(3, 1)]
    in_specs += [full(a) for a in args[13:]]
    (y,), extra = _pcall(body, name, (S // TS,), in_specs, [BS((4, TS, BW), lambda i: (0, i, 0))],
                         [SDS((4, S, BW), BF16)], [pltpu.VMEM((H + TS, BW), F32)] * 3, ("parallel",), args, comm)
    return y, extra


def _mixers_bwd(proj, dy, dproj_gates, pool_w, pool_wt, pool_scale, sconv_w, cconv_w, cln_g, cln_b, sln_g, sln_b,
                sgu_w, sgu_wt, sgu_bias, name, comm=None):
    _, S, D = proj.shape
    BW = D // 2
    GW = BW // 4
    TS = _tile(S, 256, SGU_BLOCK)
    H = HALO
    hb = TS // H
    n_t = S // TS
    E = TS + H

    def main(blk, col):
        return BS((1, TS, BW), lambda i: (blk, i, col))

    def back(blk, col):
        return BS((1, H, BW), lambda i: (blk, jnp.maximum(i * hb - 1, 0), col))

    def front(blk, col):
        return BS((1, H, BW), lambda i: (blk, jnp.minimum((i + 1) * hb, S // H - 1), col))

    def full(a):
        nd = a.ndim
        return BS(a.shape, lambda i: (0,) * nd)

    def body(pa_b, pa_m, xi_b, xi_m, bg_m, bg_f, cg_b, cg_m, ca_b, ca_m, ca_f, cb_b, cb_m, cb_f, du_m, dv_m,
             dya_m, dya_f, dyb_m, dyb_f, dyc_m, dyc_f, dyd_m,
             pw, pwt, ps, sw, cw, clg, clb, slg, slb, gw, gwt, gbias, _gates_in,
             dp_ref, dpw, dps, dsw, dcw, dclg, dclb, dslg, dslb, dgw, dgb,
             e1, e2, e3, e4, e5, e6):
        i = pl.program_id(0)
        nb = jnp.where(i > 0, 1.0, 0.0).astype(F32)
        nf = jnp.where(i < n_t - 1, 1.0, 0.0).astype(F32)
        rows_m = i * TS + lax.broadcasted_iota(jnp.int32, (TS, 1), 0)
        rows_e = i * TS + lax.broadcasted_iota(jnp.int32, (E, 1), 0)

        @pl.when(i == 0)
        def _():
            for r in (dpw, dps, dsw, dcw, dclg, dclb, dslg, dslb, dgw, dgb):
                r[...] = jnp.zeros(r.shape, F32)

        e1[0:H, :] = pa_b[0] * nb
        e1[H:H + TS, :] = pa_m[0]
        e2[0:TS, :] = dya_m[0] * ps[...]
        e2[TS:E, :] = dya_f[0] * ps[...] * nf
        for g in range(4):
            cols = slice(g * GW, (g + 1) * GW)
            win = 2 << g
            a_m = e1[H:H + TS, cols]
            wsum = a_m
            for k in range(1, win):
                wsum = wsum + e1[H - k:H - k + TS, cols]
            d = wsum / jnp.minimum(rows_m + 1, win).astype(F32) - a_m
            d16 = d.astype(BF16)
            dyp = e2[0:E, cols].astype(BF16)
            dd = _dot(dyp, pwt[g].astype(BF16))
            e3[0:E, cols] = dd / jnp.minimum(rows_e + 1, win).astype(F32)
            da = e3[0:TS, cols] - dd[0:TS]
            for k in range(1, win):
                da = da + e3[k:k + TS, cols]
            dp_ref[0, :, cols] = da.astype(BF16)
            ypre = _dot(d16, pw[g].astype(BF16))
            dps[:, cols] += jnp.sum(dya_m[0][:, cols] * ypre, axis=0, keepdims=True)
            dpw[g] += _dot(jnp.transpose(d).astype(BF16), dyp[0:TS])

        e4[0:H, :] = cg_b[0] * xi_b[0] * nb
        e4[H:H + TS, :] = cg_m[0] * xi_m[0]
        dyb = dyb_m[0]
        e5[0:TS, :] = dyb * bg_m[0]
        e5[TS:E, :] = dyb_f[0] * bg_f[0] * nf
        dcz = e5[0:TS, :]
        cz = None
        dz = None
        for k in range(SCONV_K):
            zk = e4[H - 2 + k:H - 2 + k + TS, :]
            wk = sw[k:k + 1, :]
            cz = wk * zk if cz is None else cz + wk * zk
            t = wk * e5[2 - k:2 - k + TS, :]
            dz = t if dz is None else dz + t
            dsw[k:k + 1, :] += jnp.sum(dcz * zk, axis=0, keepdims=True)
        dp_ref[0, :, BW:2 * BW] = (dz * cg_m[0]).astype(BF16)
        dp_ref[1, :, 0:BW] = (dyb * cz).astype(BF16)
        dp_ref[1, :, BW:2 * BW] = (dz * xi_m[0]).astype(BF16)

        sgm = _sig(cb_m[0])
        e6[0:H, :] = ca_b[0] * _sig(cb_b[0]) * nb
        e6[H:H + TS, :] = ca_m[0] * sgm
        e6[H + TS:H + E, :] = ca_f[0] * _sig(cb_f[0]) * nf
        o = H - (CCONV_K - 1)
        y1 = cw[0:1, :] * e6[o:o + E, :]
        for k in range(1, CCONV_K):
            y1 = y1 + cw[k:k + 1, :] * e6[o + k:o + k + E, :]
        yh, rstd = _ln_stats(y1)
        y2 = yh * clg[...] + clb[...]
        s2 = _sig(y2)
        e1[0:TS, :] = dyc_m[0]
        e1[TS:E, :] = dyc_f[0] * nf
        dy2 = e1[0:E, :] * (s2 * (1.0 + y2 * (1.0 - s2)))
        dclg[...] += jnp.sum((dy2 * yh)[0:TS], axis=0, keepdims=True)
        dclb[...] += jnp.sum(dy2[0:TS], axis=0, keepdims=True)
        e2[0:E, :] = _ln_bwd(dy2 * clg[...], yh, rstd)
        dy1_m = e2[0:TS, :]
        dy0 = None
        for k in range(CCONV_K):
            t = cw[k:k + 1, :] * e2[CCONV_K - 1 - k:CCONV_K - 1 - k + TS, :]
            dy0 = t if dy0 is None else dy0 + t
            dcw[k:k + 1, :] += jnp.sum(dy1_m * e6[o + k:o + k + TS, :], axis=0, keepdims=True)
        dp_ref[2, :, 0:BW] = (dy0 * sgm).astype(BF16)
        dp_ref[2, :, BW:2 * BW] = (dy0 * ca_m[0] * (sgm * (1.0 - sgm))).astype(BF16)

        pu = du_m[0]
        pv = dv_m[0]
        u, tu = _gelu(pu)
        v, tv = _gelu(pv)
        vh, vr = _ln_stats(v)
        vn = vh * slg[...] + slb[...]
        dyd = dyd_m[0]
        mask, mask_t = _sgu_masks()
        for h in range(4):
            wm = jnp.where(mask, gw[h], 0.0).astype(BF16)
            wmt = jnp.where(mask_t, gwt[h], 0.0).astype(BF16)
            cs = slice(h * GW, (h + 1) * GW)
            for n in range(TS // SGU_BLOCK):
                rs = slice(n * SGU_BLOCK, (n + 1) * SGU_BLOCK)
                vb = vn[rs, cs].astype(BF16)
                z = _dot(wm, vb) + gbias[h]
                dzb = dyd[rs, cs] * u[rs, cs]
                dz16 = dzb.astype(BF16)
                e3[rs, cs] = dyd[rs, cs] * z
                e4[rs, cs] = _dot(wmt, dz16)
                dgw[h] += jnp.where(mask, _dot_nt(dz16, vb), 0.0)
                dgb[h] += dzb
        dvn = e4[0:TS, :]
        dslg[...] += jnp.sum(dvn * vh, axis=0, keepdims=True)
        dslb[...] += jnp.sum(dvn, axis=0, keepdims=True)
        dv = _ln_bwd(dvn * slg[...], vh, vr)
        dp_ref[3, :, 0:BW] = (e3[0:TS, :] * _gelu_grad(pu, tu)).astype(BF16)
        dp_ref[3, :, BW:2 * BW] = (dv * _gelu_grad(pv, tv)).astype(BF16)

        @pl.when(i == n_t - 1)
        def _():
            for h in range(4):
                dgb[h] = jnp.broadcast_to(jnp.sum(dgb[h], axis=1, keepdims=True), dgb.shape[1:])

    params = [pool_w, pool_wt, pool_scale, sconv_w, cconv_w, cln_g, cln_b, sln_g, sln_b, sgu_w, sgu_wt, sgu_bias]
    args = [proj] * 16 + [dy] * 7 + params + [dproj_gates]
    in_specs = [back(0, 0), main(0, 0), back(0, 1), main(0, 1), main(1, 0), front(1, 0), back(1, 1), main(1, 1),
                back(2, 0), main(2, 0), front(2, 0), back(2, 1), main(2, 1), front(2, 1), main(3, 0), main(3, 1),
                main(0, 0), front(0, 0), main(1, 0), front(1, 0), main(2, 0), front(2, 0), main(3, 0)]
    in_specs += [full(a) for a in params] + [ANY]
    small = [SDS(pool_w.shape, F32), SDS(pool_scale.shape, F32), SDS(sconv_w.shape, F32), SDS(cconv_w.shape, F32),
             SDS(cln_g.shape, F32), SDS(cln_b.shape, F32), SDS(sln_g.shape, F32), SDS(sln_b.shape, F32),
             SDS(sgu_w.shape, F32), SDS(sgu_bias.shape, F32)]
    out_specs = [BS((4, TS, D), lambda i: (0, i, 0))] + [full(s) for s in small]
    return _pcall(body, name, (n_t,), in_specs, out_specs, [SDS(dproj_gates.shape, BF16)] + small,
                  [pltpu.VMEM((TS + 2 * H, BW), F32)] * 6, ("arbitrary",), args, comm, aliases={len(args) - 1: 0})


def _merge_fwd(y, proj, w_up, w_out, x, name, comm=None):
    _, S, BW = y.shape
    D = x.shape[1]
    tm = _tile(S, 256, 16)

    def body(y_ref, pg_ref, wu_ref, wo_ref, x_ref, o_ref, m_ref):
        merged = None
        for g in range(4):
            t = _sig(pg_ref[g]) * _dot(y_ref[g], wu_ref[g])
            merged = t if merged is None else merged + t
        m16 = merged.astype(BF16)
        m_ref[...] = m16
        o_ref[...] = x_ref[...] + _dot(m16, wo_ref[...])

    (o, m), extra = _pcall(
        body, name, (S // tm,),
        [BS((4, tm, BW), lambda i: (0, i, 0)), BS((4, tm, D), lambda i: (1, i, 0)),
         BS((4, BW, D), lambda i: (0, 0, 0)), BS((D, D), lambda i: (0, 0)), BS((tm, D), lambda i: (i, 0))],
        [BS((tm, D), lambda i: (i, 0)), BS((tm, D), lambda i: (i, 0))],
        [SDS((S, D), F32), SDS((S, D), BF16)], [], ("parallel",), (y, proj, w_up, w_out, x), comm)
    return o, m, extra


def _merge_bwd(dx, y, proj, w_up, w_out, name, comm=None):
    _, S, BW = y.shape
    D = dx.shape[1]
    tm = _tile(S, 256, 16)

    def body(dx_ref, y_ref, pg_ref, wu_ref, wo_ref, dup_ref, dp_ref, dy_ref):
        dm = _dot_nt(dx_ref[...].astype(BF16), wo_ref[...])
        for g in range(4):
            gate = _sig(pg_ref[g])
            up = _dot(y_ref[g], wu_ref[g])
            dup = (dm * gate).astype(BF16)
            dup_ref[g] = dup
            dp_ref[g] = (dm * up * (gate * (1.0 - gate))).astype(BF16)
            dy_ref[g] = _dot_nt(dup, wu_ref[g])

    res, extra = _pcall(
        body, name, (S // tm,),
        [BS((tm, D), lambda i: (i, 0)), BS((4, tm, BW), lambda i: (0, i, 0)), BS((4, tm, D), lambda i: (1, i, 0)),
         BS((4, BW, D), lambda i: (0, 0, 0)), BS((D, D), lambda i: (0, 0))],
        [BS((4, tm, D), lambda i: (0, i, 0)), BS((4, tm, D), lambda i: (1, i, 0)), BS((4, tm, BW), lambda i: (0, i, 0))],
        [SDS((4, S, D), BF16), SDS((8, S, D), BF16), SDS((4, S, BW), F32)], [], ("parallel",),
        (dx, y, proj, w_up, w_out), comm)
    return res, extra


def _adamw(w, g, m, v):
    m = ADAM_B1 * m + (1.0 - ADAM_B1) * g
    v = ADAM_B2 * v + (1.0 - ADAM_B2) * (g * g)
    m_hat = m / (1.0 - ADAM_B1 ** ADAM_STEP)
    v_hat = v / (1.0 - ADAM_B2 ** ADAM_STEP)
    delta = -ADAM_LR * (m_hat / (jnp.sqrt(v_hat) + ADAM_EPS) + ADAM_WD * w)
    return delta, m, v


def _adamw_sharded(parts, w, m, v, name, comm=None):
    L, R, C = w.shape
    tr = _tile(R, 256, 16)

    def body(*refs):
        p_refs = refs[:L]
        w_ref, m_ref, v_ref, g_out, d_out, m_out, v_out = refs[L:]
        l = pl.program_id(0)
        g = None
        for d in range(N_DEV):
            t = p_refs[0][d].astype(F32)
            for j in range(1, L):
                t = jnp.where(l == j, p_refs[j][d].astype(F32), t)
            g = t if g is None else g + t
        dl, mn, vn = _adamw(w_ref[0], g, m_ref[0], v_ref[0])
        g_out[0] = g
        d_out[0] = dl
        m_out[0] = mn
        v_out[0] = vn

    def part_spec(j):
        return BS((N_DEV, tr, C), lambda l, r: (0, jnp.where(l == j, r, 0), 0))

    blk = BS((1, tr, C), lambda l, r: (l, r, 0))
    return _pcall(body, name, (L, R // tr), [part_spec(j) for j in range(L)] + [blk, blk, blk], [blk] * 4,
                  [SDS((L, R, C), F32)] * 4, [], ("parallel", "parallel"), (*parts, w, m, v), comm)


def _adamw_replicated(gathered, layout, wmv, name):
    n_b = len(gathered)
    n_p = len(layout)

    def body(*refs):
        bufs = refs[:n_b]
        prm = refs[n_b:n_b + 3 * n_p]
        outs = refs[n_b + 3 * n_p:n_b + 7 * n_p]
        sums = refs[n_b + 7 * n_p:]
        for b in range(n_b):
            s = bufs[b][0]
            for d in range(1, N_DEV):
                s = s + bufs[b][d]
            sums[b][...] = s
        for p, (b, r0, nr) in enumerate(layout):
            g = sums[b][r0:r0 + nr, :]
            d, mn, vn = _adamw(prm[3 * p][...], g, prm[3 * p + 1][...], prm[3 * p + 2][...])
            outs[4 * p][...] = g
            outs[4 * p + 1][...] = d
            outs[4 * p + 2][...] = mn
            outs[4 * p + 3][...] = vn

    flat = [a for t in wmv for a in t]
    out_shape = []
    for (w, _, _) in wmv:
        out_shape += [SDS(w.shape, F32)] * 4
    out_shape += [SDS(g.shape[1:], F32) for g in gathered]
    return pl.pallas_call(
        body, name=name, out_shape=out_shape,
        compiler_params=pltpu.CompilerParams(vmem_limit_bytes=V7X_VMEM_LIMIT),
    )(*gathered, *flat)


def _adamw_small(g, w, m, v, name):
    def body(g_ref, w_ref, m_ref, v_ref, d_out, m_out, v_out):
        d, mn, vn = _adamw(w_ref[...], g_ref[...], m_ref[...], v_ref[...])
        d_out[...] = d
        m_out[...] = mn
        v_out[...] = vn

    return pl.pallas_call(body, name=name, out_shape=[SDS(w.shape, F32)] * 3)(g, w, m, v)


def _pad_rows(a, rows):
    return jnp.pad(a, ((0, rows - a.shape[0]), (0, 0)))


def kernel(x, ffn1_norm, ffn1_w13, ffn1_w2, mix_norm, w_in, pool_w, pool_scale, sconv_w, cconv_w, cconv_ln_g, cconv_ln_b, sgu_ln_g, sgu_ln_b, sgu_w, sgu_b, w_up, w_out, ffn2_norm, ffn2_w13, ffn2_w2, final_norm, loss_target, m_ffn1_norm, m_ffn1_w13, m_ffn1_w2, m_mix_norm, m_w_in, m_pool_w, m_pool_scale, m_sconv_w, m_cconv_w, m_cconv_ln_g, m_cconv_ln_b, m_sgu_ln_g, m_sgu_ln_b, m_sgu_w, m_sgu_b, m_w_up, m_w_out, m_ffn2_norm, m_ffn2_w13, m_ffn2_w2, m_final_norm, v_ffn1_norm, v_ffn1_w13, v_ffn1_w2, v_mix_norm, v_w_in, v_pool_w, v_pool_scale, v_sconv_w, v_cconv_w, v_cconv_ln_g, v_cconv_ln_b, v_sgu_ln_g, v_sgu_ln_b, v_sgu_w, v_sgu_b, v_w_up, v_w_out, v_ffn2_norm, v_ffn2_w13, v_ffn2_w2, v_final_norm):
    P = dict(locals())
    L = ffn1_norm.shape[0]
    S, D = x.shape[1], x.shape[2]
    BW = D // 2
    GW = BW // 4
    F = ffn1_w2.shape[1] * N_DEV
    fs = ffn1_w13.shape[2]
    cw = sconv_w.shape[2]
    me = 4 * lax.axis_index("x") + 2 * lax.axis_index("y") + lax.axis_index("c")

    big = ["ffn1_w13", "ffn1_w2", "w_in", "w_up", "w_out", "ffn2_w13", "ffn2_w2"]
    shards = [(jnp.swapaxes(P[n], 1, 2) if n.endswith("w13") else P[n]).astype(BF16) for n in big]
    conv_local = jnp.concatenate([sconv_w, cconv_w], axis=1)

    def gather_of(units):
        return _gather_comm(shards, [(big.index(n), l) for n, l in units])

    def ready(n, g):
        if n.endswith("w13"):
            return g.reshape(2, F, D)
        if n.endswith("w2"):
            return g.reshape(F, D)
        if n == "w_up":
            return jnp.transpose(g, (1, 2, 0, 3)).reshape(4, BW, D)
        if n == "w_out":
            return g.reshape(D, D)
        return g

    W = {}

    def take(units, arrays):
        for (n, l), g in zip(units, arrays):
            W[n, l] = ready(n, g)

    first_units = [("ffn1_w13", 0), ("ffn1_w2", 0)]
    plan = {("ffn1_up", 0): [("w_in", 0)],
            ("ffn1_down", 0): [("w_up", 0), ("w_out", 0)],
            ("proj", 0): [("ffn2_w13", 0), ("ffn2_w2", 0)],
            ("mixers", 0): [("ffn1_w13", 1)], ("merge", 0): [("ffn1_w2", 1)],
            ("ffn2_up", 0): [("w_in", 1)], ("ffn2_down", 0): [("w_up", 1), ("w_out", 1)],
            ("ffn1_up", 1): [("ffn2_w13", 1)], ("ffn1_down", 1): [("ffn2_w2", 1)]}
    assert L <= 2

    def carried(key):
        units = [u for u in plan.get(key, []) if u[1] < L]
        return units, (gather_of(units) if units else None)

    first = _gather_comm(shards + [conv_local], [(big.index(n), l) for n, l in first_units] + [(len(big), None)])
    got = _run_comm(first, "gather_first_weights")
    take(first_units, got[:2])
    conv_full = jnp.transpose(got[2], (1, 2, 0, 3)).reshape(L, SCONV_K + CCONV_K, N_DEV * cw)
    sconv_full = conv_full[:, :SCONV_K]
    cconv_full = conv_full[:, SCONV_K:]

    sgu_bias = jnp.broadcast_to(sgu_b[:, :, :, None], sgu_b.shape + (GW,))
    pool_wt = jnp.swapaxes(pool_w, 2, 3)
    sgu_wt = jnp.swapaxes(sgu_w, 2, 3)

    def row(a, l):
        return a[l][None, :]

    saved = []
    xc = x[0]
    for l in range(L):
        sv = {}
        for tag in ("ffn1", None, "ffn2"):
            if tag is None:
                sv["x_mix"] = xc
                h = _rmsnorm_fwd(xc, row(mix_norm, l), "mix_norm_fwd")
                units, comm = carried(("proj", l))
                proj, extra = _matmul_fwd(h, W["w_in", l], "proj_fwd", comm)
                take(units, extra)
                units, comm = carried(("mixers", l))
                y, extra = _mixers_fwd(proj, pool_w[l], row(pool_scale, l), sconv_full[l], cconv_full[l],
                                       row(cconv_ln_g, l), row(cconv_ln_b, l), row(sgu_ln_g, l), row(sgu_ln_b, l),
                                       sgu_w[l], sgu_bias[l], "mixers_fwd", comm)
                take(units, extra)
                units, comm = carried(("merge", l))
                xc, merged, extra = _merge_fwd(y, proj, W["w_up", l], W["w_out", l], xc, "merge_fwd", comm)
                take(units, extra)
                sv.update(h_mix=h, proj=proj, y=y, merged=merged)
            else:
                sv["x_" + tag] = xc
                h = _rmsnorm_fwd(xc, row(P[tag + "_norm"], l), "ffn_norm_fwd")
                units, comm = carried((tag + "_up", l))
                ab, extra = _matmul_fwd(h, W[tag + "_w13", l], "ffn_up_fwd", comm, w_t=True, out_dtype=BF16)
                take(units, extra)
                units, comm = carried((tag + "_down", l))
                xc, extra = _swiglu_down(ab, W[tag + "_w2", l], xc, "ffn_down_fwd", comm)
                take(units, extra)
                sv.update({"h_" + tag: h, "ab_" + tag: ab})
        saved.append(sv)

    loss_part, dx, d_final = _final_loss(xc, final_norm[None, :], loss_target[0], "loss_head")
    loss = lax.psum(loss_part[0, 0], MESH_AXES)

    R = {}
    second = []

    def rest_of_sends():
        keys = [k for k, _, _ in second]
        comm = None
        if second:
            comm = _scatter_comm([g for _, g, _ in second], PEERS_OTHER_CORE, [r for _, _, r in second])
        second.clear()
        return keys, comm

    def settle(keys, arrays):
        for k, a in zip(keys, arrays):
            R[k] = a

    small_g = [None] * L
    for l in reversed(range(L)):
        sv = saved[l]
        sg = {}
        for tag in ("ffn2", None, "ffn1"):
            if tag is None:
                keys, comm = rest_of_sends()
                (dup, dproj, dy), extra = _merge_bwd(dx, sv["y"], sv["proj"], W["w_up", l], W["w_out", l],
                                                     "merge_bwd", comm)
                settle(keys, extra)
                g_out, _ = _matmul_tn(sv["merged"][None], dx[None], 1, "w_out_grad")
                g_up, _ = _matmul_tn(sv["y"], dup, 1, "w_up_grad")
                g_out = g_out.reshape(N_DEV, D // N_DEV, D)
                g_up = jnp.transpose(g_up.reshape(4, BW, N_DEV, D // N_DEV), (2, 0, 1, 3)).reshape(
                    N_DEV, 4 * BW, D // N_DEV)
                res, (R["w_out", l], R["w_up", l]) = _mixers_bwd(
                    sv["proj"], dy, dproj, pool_w[l], pool_wt[l], row(pool_scale, l), sconv_full[l], cconv_full[l],
                    row(cconv_ln_g, l), row(cconv_ln_b, l), row(sgu_ln_g, l), row(sgu_ln_b, l), sgu_w[l], sgu_wt[l],
                    sgu_bias[l], "mixers_bwd", _scatter_comm([g_out, g_up]))
                dproj = res[0]
                (sg["pool_w"], sg["pool_scale"], sg["sconv_w"], sg["cconv_w"], sg["cconv_ln_g"], sg["cconv_ln_b"],
                 sg["sgu_ln_g"], sg["sgu_ln_b"], sg["sgu_w"], dgb) = res[1:]
                sg["sgu_b"] = dgb[:, :, 0]
                g_in, _ = _matmul_tn(sv["h_mix"][None], dproj, N_DEV, "w_in_grad")
                dx, sg["mix_norm"], (r_in,) = _matmul_nt_normbwd(
                    dproj, W["w_in", l], sv["x_mix"], row(mix_norm, l), dx, "proj_bwd",
                    _scatter_comm([g_in], PEERS_SAME_CORE))
                second.append((("w_in", l), g_in, r_in))
            else:
                keys, comm = rest_of_sends()
                dab, sh, extra = _ffn_bwd_hidden(dx, W[tag + "_w2", l], sv["ab_" + tag], "ffn_hidden_bwd", comm)
                settle(keys, extra)
                g_w2, _ = _matmul_tn(sh[None], dx[None], 1, "ffn_w2_grad")
                g_w2 = g_w2.reshape(N_DEV, F // N_DEV, D)
                g_w13, (R[tag + "_w2", l],) = _matmul_tn(dab, sv["h_" + tag][None], 1, "ffn_w13_grad",
                                                         _scatter_comm([g_w2]), b_shared=True)
                g_w13 = g_w13.reshape(N_DEV, fs, D)
                dx, sg[tag + "_norm"], (r_w13,) = _matmul_nt_normbwd(
                    dab, W[tag + "_w13", l], sv["x_" + tag], row(P[tag + "_norm"], l), dx, "ffn_up_bwd",
                    _scatter_comm([g_w13], PEERS_SAME_CORE), w_t=True)
                second.append(((tag + "_w13", l), g_w13, r_w13))
        small_g[l] = sg
    grad_x = dx[None]
    out = {}

    wide = ["ffn1_norm", "mix_norm", "ffn2_norm", "final_norm"]
    half = ["pool_scale", "cconv_ln_g", "cconv_ln_b", "sgu_ln_g", "sgu_ln_b"]
    narrow = ["pool_w", "sgu_w", "sgu_b"]

    def stack_layers(n):
        return jnp.stack([small_g[l][n] for l in range(L)], axis=0)

    def as2d(n, a):
        if n == "final_norm":
            return a.reshape(1, D)
        return a.reshape(-1, a.shape[-1])

    sg2 = {n: stack_layers(n).reshape(-1, stack_layers(n).shape[-1]) for n in wide[:3] + half + narrow}
    sg2["final_norm"] = d_final
    conv_g = jnp.concatenate([stack_layers("sconv_w"), stack_layers("cconv_w")], axis=1)
    conv_g = conv_g.reshape(L * (SCONV_K + CCONV_K), N_DEV * cw)
    small_names = wide + half + narrow
    widths = []
    for n in small_names:
        if sg2[n].shape[1] not in widths:
            widths.append(sg2[n].shape[1])
    layout = {}
    bufs = []
    for b, width in enumerate(widths):
        parts = []
        r0 = 0
        for n in small_names:
            if sg2[n].shape[1] != width:
                continue
            nr = sg2[n].shape[0]
            pr = -(-nr // 8) * 8
            layout[n] = (b, r0, nr)
            parts.append(_pad_rows(sg2[n], pr))
            r0 += pr
        if width == conv_g.shape[1]:
            conv_b, conv_r0 = b, r0
            parts.append(_pad_rows(conv_g, -(-conv_g.shape[0] // 8) * 8))
        bufs.append(jnp.concatenate(parts, axis=0))

    gathered_small = None
    for i, n in enumerate(["ffn2_w13", "w_in", "ffn2_w2", "w_up", "w_out", "ffn1_w2", "ffn1_w13"]):
        shp = P[n].shape
        if n.endswith("w13"):
            flat, back = (lambda a: jnp.swapaxes(a, 1, 2)), (lambda a: jnp.swapaxes(a, 1, 2))
        else:
            rows, cols = math.prod(shp[1:-1]), shp[-1]
            flat, back = (lambda a: a.reshape(L, rows, cols)), (lambda a: a.reshape(shp))
        keys, comm = [], None
        if i == 0:
            keys, comm = rest_of_sends()
        elif i == 1:
            comm = _gather_comm(bufs, [(b, None) for b in range(len(bufs))])
        res, extra = _adamw_sharded([R[n, l] for l in range(L)], flat(P[n]), flat(P["m_" + n]), flat(P["v_" + n]),
                                    "adamw_sharded", comm)
        if i == 0:
            settle(keys, extra)
        elif i == 1:
            gathered_small = extra
        out[n] = tuple(back(a) for a in res)

    res = _adamw_replicated(gathered_small, [layout[n] for n in small_names],
                            [(as2d(n, P[n]), as2d(n, P["m_" + n]), as2d(n, P["v_" + n])) for n in small_names],
                            "adamw_replicated")
    for p, n in enumerate(small_names):
        out[n] = tuple(a.reshape(P[n].shape) for a in res[4 * p:4 * p + 4])
    conv_sum = res[4 * len(small_names) + conv_b][conv_r0:conv_r0 + conv_g.shape[0]]
    conv_mine = lax.dynamic_slice_in_dim(conv_sum, me * cw, cw, axis=1)

    def conv2d(a, b):
        return jnp.concatenate([a, b], axis=1).reshape(L * (SCONV_K + CCONV_K), cw)

    cd, cm, cv = _adamw_small(conv_mine, conv2d(sconv_w, cconv_w), conv2d(m_sconv_w, m_cconv_w),
                              conv2d(v_sconv_w, v_cconv_w), "adamw_conv")
    for n, sl in (("sconv_w", slice(0, SCONV_K)), ("cconv_w", slice(SCONV_K, SCONV_K + CCONV_K))):
        out[n] = tuple(a.reshape(L, SCONV_K + CCONV_K, cw)[:, sl] for a in (conv_mine, cd, cm, cv))

    order = ["ffn1_norm", "ffn1_w13", "ffn1_w2", "mix_norm", "w_in", "pool_w", "pool_scale", "sconv_w", "cconv_w",
             "cconv_ln_g", "cconv_ln_b", "sgu_ln_g", "sgu_ln_b", "sgu_w", "sgu_b", "w_up", "w_out", "ffn2_norm",
             "ffn2_w13", "ffn2_w2", "final_norm"]
    return (loss, grad_x, *[out[n][0] for n in order], *[out[n][1] for n in order],
            *[out[n][2] for n in order], *[out[n][3] for n in order])
```

```python
import functools
import math

import jax
import jax.numpy as jnp
from jax import lax
from jax.experimental import pallas as pl
from jax.experimental.pallas import tpu as pltpu

F32 = jnp.float32
BF16 = jnp.bfloat16
EPS = 1e-6
ADAM_LR = 0.001
ADAM_B1 = 0.9
ADAM_B2 = 0.999
ADAM_EPS = 1e-08
ADAM_WD = 0.01
ADAM_STEP = 10
SGU_BLOCK = 128
SGU_CHUNK = 64
SCONV_K = 3
CCONV_K = 31
HALO = 32
V7X_VMEM_LIMIT = 48 * 1024 * 1024
MESH_AXES = ("x", "y", "c")
N_DEV = 8
_GELU_C0 = math.sqrt(2.0 / math.pi)
_GELU_C1 = 0.044715

BS = pl.BlockSpec
SDS = jax.ShapeDtypeStruct
ANY = pl.BlockSpec(memory_space=pl.ANY)


def _tile(n, pref, align=128):
    if n <= pref:
        return n
    t = pref - pref % align
    while t > 0:
        if n % t == 0:
            return t
        t -= align
    return n


def _sig(v):
    return 1.0 / (1.0 + jnp.exp(-v))


def _gelu(v):
    t = jnp.tanh(_GELU_C0 * (v + _GELU_C1 * (v * v * v)))
    return 0.5 * v * (1.0 + t), t


def _gelu_grad(v, t):
    return 0.5 * (1.0 + t) + 0.5 * v * (1.0 - t * t) * (_GELU_C0 * (1.0 + 3.0 * _GELU_C1 * v * v))


def _ln_stats(v):
    mu = jnp.mean(v, axis=-1, keepdims=True)
    vc = v - mu
    var = jnp.mean(vc * vc, axis=-1, keepdims=True)
    rstd = lax.rsqrt(var + EPS)
    return vc * rstd, rstd


def _ln_bwd(dvh, vh, rstd):
    return rstd * (dvh - jnp.mean(dvh, axis=-1, keepdims=True) - vh * jnp.mean(dvh * vh, axis=-1, keepdims=True))


def _dot(a, b):
    return jnp.dot(a, b, preferred_element_type=F32)


def _dot_nt(a, b):
    return lax.dot_general(a, b, (((1,), (1,)), ((), ())), preferred_element_type=F32)


def _dot_tn(a, b):
    return lax.dot_general(a, b, (((0,), (0,)), ((), ())), preferred_element_type=F32)


def _mesh_pos():
    return lax.axis_index("x"), lax.axis_index("y"), lax.axis_index("c")


class _Comm:
    def __init__(self, ins, out_shape, sems, start, finish, aliases=None):
        self.ins, self.out_shape, self.sems, self.start, self.finish = ins, out_shape, sems, start, finish
        self.aliases = aliases or {}


def _gather_comm(shards, units):
    n_u = len(units)

    def tools(ins, dsts, sems):
        send_sems, recv_sems, local_sems = sems
        x, y, c = _mesh_pos()
        chips = [(1 - x, y), (x, 1 - y), (1 - x, 1 - y)]

        def src_of(o):
            t, l = units[o]
            return ins[t] if l is None else ins[t].at[l]

        def row(o, p):
            return dsts[o].at[4 * p[0] + 2 * p[1] + p[2]]

        def copy(o, k, block, to, own=False):
            return pltpu.make_async_remote_copy(
                src_ref=src_of(o) if own else row(o, block), dst_ref=row(o, block),
                send_sem=send_sems.at[o * 7 + k], recv_sem=recv_sems.at[o * 7 + k],
                device_id=to, device_id_type=pl.DeviceIdType.MESH)

        def local(o):
            return pltpu.make_async_copy(src_of(o), row(o, (x, y, c)), local_sems.at[o])

        def first(o):
            return [copy(o, 1 + j, (x, y, c), (*chip, c), own=True) for j, chip in enumerate(chips)] + [
                copy(o, 0, (x, y, c), (x, y, 1 - c), own=True)]

        return (x, y, c), chips, copy, local, first

    def start(ins, dsts, sems):
        _, _, _, local, first = tools(ins, dsts, sems)
        for o in range(n_u):
            local(o).start()
            for cp in first(o):
                cp.start()

    def finish(ins, dsts, sems):
        (x, y, c), chips, copy, local, first = tools(ins, dsts, sems)
        me, sibling = (x, y, c), (x, y, 1 - c)
        passed = []
        for o in range(n_u):
            for j, chip in enumerate(chips):
                copy(o, 1 + j, (*chip, c), me).wait_recv()
                cp = copy(o, 4 + j, (*chip, c), sibling)
                cp.start()
                passed.append(cp)
        for o in range(n_u):
            copy(o, 0, sibling, me).wait_recv()
            for j, chip in enumerate(chips):
                copy(o, 4 + j, (*chip, 1 - c), me).wait_recv()
        for o in range(n_u):
            for cp in first(o):
                cp.wait_send()
        for cp in passed:
            cp.wait_send()
        for o in range(n_u):
            local(o).wait()

    out_shape = []
    for t, l in units:
        shp = shards[t].shape if l is None else shards[t].shape[1:]
        out_shape.append(SDS((N_DEV,) + tuple(shp), shards[t].dtype))
    sems = [pltpu.SemaphoreType.DMA((7 * n_u,)), pltpu.SemaphoreType.DMA((7 * n_u,)), pltpu.SemaphoreType.DMA((n_u,))]
    return _Comm(list(shards), out_shape, sems, start, finish)


PEERS_ALL = (1, 2, 3, 4, 5, 6, 7)
PEERS_SAME_CORE = (1, 2, 4, 6)
PEERS_OTHER_CORE = (3, 5, 7)


def _scatter_comm(parts, peers=PEERS_ALL, into=None):
    n_u = len(parts)

    def tools(ins, dsts, sems):
        send_sems, recv_sems, local_sems = sems
        x, y, c = _mesh_pos()
        me = 4 * x + 2 * y + c

        def peer(k):
            return ((x + ((k >> 2) & 1)) % 2, (y + ((k >> 1) & 1)) % 2, (c + (k & 1)) % 2)

        def copy(u, k, wait=False):
            p = peer(k)
            pi = 4 * p[0] + 2 * p[1] + p[2]
            return pltpu.make_async_remote_copy(
                src_ref=ins[u].at[pi], dst_ref=dsts[u].at[pi if wait else me],
                send_sem=send_sems.at[u * 7 + k - 1], recv_sem=recv_sems.at[u * 7 + k - 1],
                device_id=p, device_id_type=pl.DeviceIdType.MESH)

        def local(u):
            return pltpu.make_async_copy(ins[u].at[me], dsts[u].at[me], local_sems.at[u])

        return copy, local

    def start(ins, dsts, sems):
        copy, local = tools(ins, dsts, sems)
        for u in range(n_u):
            if into is None:
                local(u).start()
            for k in peers:
                copy(u, k).start()

    def finish(ins, dsts, sems):
        copy, local = tools(ins, dsts, sems)
        for u in range(n_u):
            for k in peers:
                copy(u, k, wait=True).wait()
            if into is None:
                local(u).wait()

    sems = [pltpu.SemaphoreType.DMA((7 * n_u,)), pltpu.SemaphoreType.DMA((7 * n_u,)), pltpu.SemaphoreType.DMA((n_u,))]
    aliases = {} if into is None else {n_u + u: u for u in range(n_u)}
    return _Comm(list(parts) + list(into or []), [SDS(p.shape, p.dtype) for p in parts], sems, start, finish, aliases)


def _run_comm(comm, name):
    n_i, n_o = len(comm.ins), len(comm.out_shape)

    def body(*refs):
        comm.start(refs[:n_i], refs[n_i:n_i + n_o], refs[n_i + n_o:])
        comm.finish(refs[:n_i], refs[n_i:n_i + n_o], refs[n_i + n_o:])

    return pl.pallas_call(
        body, name=name, out_shape=comm.out_shape, in_specs=[ANY] * n_i, out_specs=[ANY] * n_o,
        scratch_shapes=comm.sems,
    )(*comm.ins)


def _pcall(body, name, grid, in_specs, out_specs, out_shape, scratch, sem, args, comm=None, aliases=None):
    n_i, n_o, n_s = len(in_specs), len(out_specs), len(scratch)
    aliases = aliases or {}
    if comm is None:
        res = pl.pallas_call(
            body, name=name, grid=grid, in_specs=in_specs, out_specs=out_specs, out_shape=out_shape,
            scratch_shapes=scratch, input_output_aliases=aliases,
            compiler_params=pltpu.CompilerParams(dimension_semantics=sem, vmem_limit_bytes=V7X_VMEM_LIMIT),
        )(*args)
        return res, []
    n_ci, n_co = len(comm.ins), len(comm.out_shape)

    def wrapped(*refs):
        ins = refs[:n_i]
        cins = refs[n_i:n_i + n_ci]
        outs = refs[n_i + n_ci:n_i + n_ci + n_o]
        couts = refs[n_i + n_ci + n_o:n_i + n_ci + n_o + n_co]
        rest = refs[n_i + n_ci + n_o + n_co:]
        ids = [pl.program_id(d) for d in range(len(grid))]
        first = functools.reduce(jnp.logical_and, [i == 0 for i in ids])
        last = functools.reduce(jnp.logical_and, [i == g - 1 for i, g in zip(ids, grid)])

        @pl.when(first)
        def _():
            comm.start(cins, couts, rest[n_s:])

        body(*ins, *outs, *rest[:n_s])

        @pl.when(last)
        def _():
            comm.finish(cins, couts, rest[n_s:])

    res = pl.pallas_call(
        wrapped, name=name, grid=grid, in_specs=list(in_specs) + [ANY] * n_ci,
        out_specs=list(out_specs) + [ANY] * n_co, out_shape=list(out_shape) + list(comm.out_shape),
        scratch_shapes=list(scratch) + list(comm.sems),
        input_output_aliases={**aliases, **{n_i + ci: n_o + co for ci, co in comm.aliases.items()}},
        compiler_params=pltpu.CompilerParams(dimension_semantics=("arbitrary",) * len(grid),
                                             vmem_limit_bytes=V7X_VMEM_LIMIT),
    )(*args, *comm.ins)
    return res[:n_o], res[n_o:]


def _rmsnorm_fwd(x, g, name):
    S, D = x.shape
    tm = _tile(S, 512, 16)

    def body(x_ref, g_ref, h_ref):
        xv = x_ref[...]
        r = lax.rsqrt(jnp.mean(xv * xv, axis=-1, keepdims=True) + EPS)
        h_ref[...] = (xv * r * g_ref[...]).astype(BF16)

    (h,), _ = _pcall(body, name, (S // tm,), [BS((tm, D), lambda i: (i, 0)), BS((1, D), lambda i: (0, 0))],
                     [BS((tm, D), lambda i: (i, 0))], [SDS((S, D), BF16)], [], ("parallel",), (x, g))
    return h


def _matmul_fwd(a, w, name, comm=None, w_t=False, out_dtype=F32):
    S, K = a.shape
    C = w.shape[0]
    Fc = w.shape[1] if w_t else w.shape[2]
    tn = _tile(Fc, 1408)
    tm = _tile(S, 1024, 16)

    def body(a_ref, w_ref, o_ref):
        p = _dot_nt(a_ref[...], w_ref[0]) if w_t else _dot(a_ref[...], w_ref[0])
        o_ref[0] = p.astype(out_dtype)

    w_spec = BS((1, tn, K), lambda c, n, i: (c, n, 0)) if w_t else BS((1, K, tn), lambda c, n, i: (c, 0, n))
    (o,), extra = _pcall(
        body, name, (C, Fc // tn, S // tm), [BS((tm, K), lambda c, n, i: (i, 0)), w_spec],
        [BS((1, tm, tn), lambda c, n, i: (c, i, n))], [SDS((C, S, Fc), out_dtype)], [],
        ("parallel", "parallel", "parallel"), (a, w), comm)
    return o, extra


def _swiglu_down(ab, w2, x, name, comm=None):
    _, S, F = ab.shape
    D = w2.shape[1]
    tk = _tile(F, 1408)
    tm = _tile(S, 512, 16)

    def body(ab_ref, w_ref, x_ref, o_ref):
        k = pl.program_id(1)
        a = ab_ref[0].astype(F32)
        s = a * _sig(a) * ab_ref[1].astype(F32)
        p = 0.5 * _dot(s.astype(BF16), w_ref[...])

        @pl.when(k == 0)
        def _():
            o_ref[...] = x_ref[...] + p

        @pl.when(k > 0)
        def _():
            o_ref[...] += p

    (o,), extra = _pcall(
        body, name, (S // tm, F // tk),
        [BS((2, tm, tk), lambda i, k: (0, i, k)), BS((tk, D), lambda i, k: (k, 0)), BS((tm, D), lambda i, k: (i, 0))],
        [BS((tm, D), lambda i, k: (i, 0))], [SDS((S, D), F32)], [], ("parallel", "arbitrary"), (ab, w2, x), comm)
    return o, extra


def _ffn_bwd_hidden(dy, w2, ab, name, comm=None):
    S, D = dy.shape
    F = w2.shape[0]
    tk = _tile(F, 1408)
    tm = _tile(S, 256, 16)

    def body(dy_ref, w_ref, ab_ref, dab_ref, s_ref):
        ds = 0.5 * _dot_nt(dy_ref[...].astype(BF16), w_ref[...])
        a = ab_ref[0].astype(F32)
        b = ab_ref[1].astype(F32)
        sg = _sig(a)
        sa = a * sg
        dab_ref[0] = (ds * b * (sg * (1.0 + a * (1.0 - sg)))).astype(BF16)
        dab_ref[1] = (ds * sa).astype(BF16)
        s_ref[...] = (0.5 * (sa * b)).astype(BF16)

    (dab, sh), extra = _pcall(
        body, name, (F // tk, S // tm),
        [BS((tm, D), lambda k, i: (i, 0)), BS((tk, D), lambda k, i: (k, 0)), BS((2, tm, tk), lambda k, i: (0, i, k))],
        [BS((2, tm, tk), lambda k, i: (0, i, k)), BS((tm, tk), lambda k, i: (i, k))],
        [SDS((2, S, F), BF16), SDS((S, F), BF16)], [], ("parallel", "parallel"), (dy, w2, ab), comm)
    return dab, sh, extra


def _matmul_tn(a, b, n_c, name, comm=None, b_shared=False):
    G, S, M = a.shape
    _, _, Fc = b.shape
    C = n_c
    tM = _tile(M, 1408)
    tn = _tile(Fc, 1408)
    ts = _tile(S, 512, 16)
    n_s = S // ts

    def body(a_ref, b_ref, o_ref, acc):
        s = pl.program_id(4)
        p = _dot_tn(a_ref[0].astype(BF16), b_ref[0].astype(BF16))

        @pl.when(s == 0)
        def _():
            acc[...] = p

        @pl.when(s > 0)
        def _():
            acc[...] += p

        @pl.when(s == n_s - 1)
        def _():
            o_ref[0] = acc[...].astype(BF16)

    (o,), extra = _pcall(
        body, name, (G, M // tM, C, Fc // tn, n_s),
        [BS((1, ts, tM), lambda g, m, c, n, s: (g, s, m)), BS((1, ts, tn), lambda g, m, c, n, s: (c if b_shared else g * C + c, s, n))],
        [BS((1, tM, tn), lambda g, m, c, n, s: (g * C + c, m, n))], [SDS((G * C, M, Fc), BF16)],
        [pltpu.VMEM((tM, tn), F32)], ("parallel", "parallel", "parallel", "parallel", "arbitrary"), (a, b), comm)
    return o, extra


def _matmul_nt_normbwd(b, w, x, gam, dres, name, comm=None, w_t=False):
    C, S, Fc = b.shape
    D = w.shape[2] if w_t else w.shape[1]
    tk = _tile(Fc, 1408)
    tm = _tile(S, 512, 16)
    nk = Fc // tk

    def body(b_ref, w_ref, x_ref, g_ref, r_ref, dx_ref, dg_ref, acc):
        i, c, k = pl.program_id(0), pl.program_id(1), pl.program_id(2)
        p = _dot(b_ref[0], w_ref[0]) if w_t else _dot_nt(b_ref[0], w_ref[0])
        first = jnp.logical_and(c == 0, k == 0)

        @pl.when(first)
        def _():
            acc[...] = p

        @pl.when(jnp.logical_not(first))
        def _():
            acc[...] += p

        @pl.when(jnp.logical_and(c == C - 1, k == nk - 1))
        def _():
            xv = x_ref[...]
            r = lax.rsqrt(jnp.mean(xv * xv, axis=-1, keepdims=True) + EPS)
            xn = xv * r
            dh = acc[...]
            dxn = dh * g_ref[...]
            dx_ref[...] = r_ref[...] + r * (dxn - xn * jnp.mean(dxn * xn, axis=-1, keepdims=True))
            dgp = jnp.sum(dh * xn, axis=0, keepdims=True)

            @pl.when(i == 0)
            def _():
                dg_ref[...] = dgp

            @pl.when(i > 0)
            def _():
                dg_ref[...] += dgp

    (dx, dg), extra = _pcall(
        body, name, (S // tm, C, nk),
        [BS((1, tm, tk), lambda i, c, k: (c, i, k)),
         BS((1, tk, D), lambda i, c, k: (c, k, 0)) if w_t else BS((1, D, tk), lambda i, c, k: (c, 0, k)),
         BS((tm, D), lambda i, c, k: (i, 0)), BS((1, D), lambda i, c, k: (0, 0)), BS((tm, D), lambda i, c, k: (i, 0))],
        [BS((tm, D), lambda i, c, k: (i, 0)), BS((1, D), lambda i, c, k: (0, 0))],
        [SDS((S, D), F32), SDS((1, D), F32)], [pltpu.VMEM((tm, D), F32)],
        ("arbitrary", "arbitrary", "arbitrary"), (b, w, x, gam, dres), comm)
    return dx, dg, extra


def _final_loss(x, gam, target, name):
    S, D = x.shape
    tm = _tile(S, 512, 8)

    def body(x_ref, g_ref, t_ref, loss_ref, dx_ref, dg_ref):
        i = pl.program_id(0)
        xv = x_ref[...]
        r = lax.rsqrt(jnp.mean(xv * xv, axis=-1, keepdims=True) + EPS)
        xn = xv * r
        err = xn * g_ref[...] - t_ref[...]
        part = 0.5 * jnp.sum(jnp.mean(err * err, axis=-1, keepdims=True), axis=0, keepdims=True)
        dy = err * (1.0 / D)
        dxn = dy * g_ref[...]
        dx_ref[...] = r * (dxn - xn * jnp.mean(dxn * xn, axis=-1, keepdims=True))
        dgp = jnp.sum(dy * xn, axis=0, keepdims=True)
        lp = jnp.broadcast_to(part, loss_ref.shape)

        @pl.when(i == 0)
        def _():
            dg_ref[...] = dgp
            loss_ref[...] = lp

        @pl.when(i > 0)
        def _():
            dg_ref[...] += dgp
            loss_ref[...] += lp

    res, _ = _pcall(
        body, name, (S // tm,),
        [BS((tm, D), lambda i: (i, 0)), BS((1, D), lambda i: (0, 0)), BS((tm, D), lambda i: (i, 0))],
        [BS((8, 128), lambda i: (0, 0)), BS((tm, D), lambda i: (i, 0)), BS((1, D), lambda i: (0, 0))],
        [SDS((8, 128), F32), SDS((S, D), F32), SDS((1, D), F32)], [], ("arbitrary",), (x, gam, target))
    return res


CONV_CHUNK = 32


def _fill_shifted(rot, n):
    for b in range(1, 8):
        rot[b, 0:n - 8, :] = rot[0, b:b + n - 8, :]


def _window(rot, off, r0, rows):
    b = off % 8
    return rot[b, off - b + r0:off - b + r0 + rows, :]


def _taps(rot, w_ref, offs, n_rows, out):
    for r0 in range(0, n_rows, CONV_CHUNK):
        acc = None
        for k, off in enumerate(offs):
            t = w_ref[k:k + 1, :] * _window(rot, off, r0, CONV_CHUNK)
            acc = t if acc is None else acc + t
        out[r0:r0 + CONV_CHUNK, :] = acc


def _tap_grads(rot, offs, g_plane, n_rows, dw_ref):
    for k, off in enumerate(offs):
        acc = None
        for r0 in range(0, n_rows, CONV_CHUNK):
            p = g_plane[0, r0:r0 + CONV_CHUNK, :] * _window(rot, off, r0, CONV_CHUNK)
            acc = p if acc is None else acc + p
        dw_ref[k:k + 1, :] += jnp.sum(acc, axis=0, keepdims=True)


def _sgu_masks():
    ii = lax.broadcasted_iota(jnp.int32, (SGU_BLOCK, SGU_BLOCK), 0) // SGU_CHUNK
    jj = lax.broadcasted_iota(jnp.int32, (SGU_BLOCK, SGU_BLOCK), 1) // SGU_CHUNK
    return jj <= ii, ii <= jj


def _mixers_fwd(proj, pool_w, pool_scale, sconv_w, cconv_w, cln_g, cln_b, sln_g, sln_b, sgu_w, sgu_bias, name,
                comm=None):
    _, S, D = proj.shape
    BW = D // 2
    GW = BW // 4
    TS = _tile(S, 256, SGU_BLOCK)
    H = HALO
    hb = TS // H

    def main(blk, col):
        return BS((1, TS, BW), lambda i: (blk, i, col))

    def back(blk, col):
        return BS((1, H, BW), lambda i: (blk, jnp.maximum(i * hb - 1, 0), col))

    def full(a):
        nd = a.ndim
        return BS(a.shape, lambda i: (0,) * nd)

    def body(pa_m, pa_b, xi_m, xi_b, bg_m, cg_m, cg_b, ca_m, ca_b, cb_m, cb_b, du_m, dv_m,
             pw, ps, sw, cw, clg, clb, slg, slb, gw, gbias, y_ref, e1, e2, e3):
        i = pl.program_id(0)
        nb = jnp.where(i > 0, 1.0, 0.0).astype(F32)
        rows = i * TS + lax.broadcasted_iota(jnp.int32, (TS, 1), 0)

        e1[0:H, :] = pa_b[0] * nb
        e1[H:H + TS, :] = pa_m[0]
        for g in range(4):
            cols = slice(g * GW, (g + 1) * GW)
            win = 2 << g
            wsum = e1[H:H + TS, cols]
            for k in range(1, win):
                wsum = wsum + e1[H - k:H - k + TS, cols]
            cnt = jnp.minimum(rows + 1, win).astype(F32)
            d = wsum / cnt - e1[H:H + TS, cols]
            yg = _dot(d.astype(BF16), pw[g].astype(BF16)) * ps[:, cols]
            y_ref[0, :, cols] = yg.astype(BF16)

        e2[0:H, :] = cg_b[0] * xi_b[0] * nb
        e2[H:H + TS, :] = cg_m[0] * xi_m[0]
        cz = sw[0:1, :] * e2[H - 2:H - 2 + TS, :]
        for k in range(1, SCONV_K):
            cz = cz + sw[k:k + 1, :] * e2[H - 2 + k:H - 2 + k + TS, :]
        y_ref[1] = (bg_m[0] * cz).astype(BF16)

        e3[0, 0:H, :] = ca_b[0] * _sig(cb_b[0]) * nb
        e3[0, H:H + TS, :] = ca_m[0] * _sig(cb_m[0])
        _fill_shifted(e3, H + TS)
        _taps(e3, cw, [H - (CCONV_K - 1) + k for k in range(CCONV_K)], TS, e1)
        yh, _ = _ln_stats(e1[0:TS, :])
        y2 = yh * clg[...] + clb[...]
        y_ref[2] = (y2 * _sig(y2)).astype(BF16)

        u, _ = _gelu(du_m[0])
        v, _ = _gelu(dv_m[0])
        vh, _ = _ln_stats(v)
        vn = vh * slg[...] + slb[...]
        mask, _ = _sgu_masks()
        for h in range(4):
            wm = jnp.where(mask, gw[h], 0.0).astype(BF16)
            cs = slice(h * GW, (h + 1) * GW)
            for n in range(TS // SGU_BLOCK):
                rs = slice(n * SGU_BLOCK, (n + 1) * SGU_BLOCK)
                z = _dot(wm, vn[rs, cs].astype(BF16)) + gbias[h]
                y_ref[3, rs, cs] = (u[rs, cs] * z).astype(BF16)

    args = [proj] * 13 + [pool_w, pool_scale, sconv_w, cconv_w, cln_g, cln_b, sln_g, sln_b, sgu_w, sgu_bias]
    in_specs = [main(0, 0), back(0, 0), main(0, 1), back(0, 1), main(1, 0), main(1, 1), back(1, 1),
                main(2, 0), back(2, 0), main(2, 1), back(2, 1), main(3, 0), main(3, 1)]
    in_specs += [full(a) for a in args[13:]]
    (y,), extra = _pcall(body, name, (S // TS,), in_specs, [BS((4, TS, BW), lambda i: (0, i, 0))],
                         [SDS((4, S, BW), BF16)],
                         [pltpu.VMEM((H + TS, BW), F32)] * 2 + [pltpu.VMEM((8, H + TS, BW), F32)], ("parallel",),
                         args, comm)
    return y, extra


def _mixers_bwd(proj, dy, dproj_gates, pool_w, pool_wt, pool_scale, sconv_w, cconv_w, cln_g, cln_b, sln_g, sln_b,
                sgu_w, sgu_wt, sgu_bias, name, comm=None):
    _, S, D = proj.shape
    BW = D // 2
    GW = BW // 4
    TS = _tile(S, 256, SGU_BLOCK)
    H = HALO
    hb = TS // H
    n_t = S // TS
    E = TS + H

    def main(blk, col):
        return BS((1, TS, BW), lambda i: (blk, i, col))

    def back(blk, col):
        return BS((1, H, BW), lambda i: (blk, jnp.maximum(i * hb - 1, 0), col))

    def front(blk, col):
        return BS((1, H, BW), lambda i: (blk, jnp.minimum((i + 1) * hb, S // H - 1), col))

    def full(a):
        nd = a.ndim
        return BS(a.shape, lambda i: (0,) * nd)

    def body(pa_b, pa_m, xi_b, xi_m, bg_m, bg_f, cg_b, cg_m, ca_b, ca_m, ca_f, cb_b, cb_m, cb_f, du_m, dv_m,
             dya_m, dya_f, dyb_m, dyb_f, dyc_m, dyc_f, dyd_m,
             pw, pwt, ps, sw, cw, clg, clb, slg, slb, gw, gwt, gbias, _gates_in,
             dp_ref, dpw, dps, dsw, dcw, dclg, dclb, dslg, dslb, dgw, dgb,
             e1, e2, e3, e4, e5, ra, rb):
        i = pl.program_id(0)
        nb = jnp.where(i > 0, 1.0, 0.0).astype(F32)
        nf = jnp.where(i < n_t - 1, 1.0, 0.0).astype(F32)
        rows_m = i * TS + lax.broadcasted_iota(jnp.int32, (TS, 1), 0)
        rows_e = i * TS + lax.broadcasted_iota(jnp.int32, (E, 1), 0)

        @pl.when(i == 0)
        def _():
            for r in (dpw, dps, dsw, dcw, dclg, dclb, dslg, dslb, dgw, dgb):
                r[...] = jnp.zeros(r.shape, F32)

        e1[0:H, :] = pa_b[0] * nb
        e1[H:H + TS, :] = pa_m[0]
        e2[0:TS, :] = dya_m[0] * ps[...]
        e2[TS:E, :] = dya_f[0] * ps[...] * nf
        for g in range(4):
            cols = slice(g * GW, (g + 1) * GW)
            win = 2 << g
            a_m = e1[H:H + TS, cols]
            wsum = a_m
            for k in range(1, win):
                wsum = wsum + e1[H - k:H - k + TS, cols]
            d = wsum / jnp.minimum(rows_m + 1, win).astype(F32) - a_m
            d16 = d.astype(BF16)
            dyp = e2[0:E, cols].astype(BF16)
            dd = _dot(dyp, pwt[g].astype(BF16))
            e3[0:E, cols] = dd / jnp.minimum(rows_e + 1, win).astype(F32)
            da = e3[0:TS, cols] - dd[0:TS]
            for k in range(1, win):
                da = da + e3[k:k + TS, cols]
            dp_ref[0, :, cols] = da.astype(BF16)
            ypre = _dot(d16, pw[g].astype(BF16))
            dps[:, cols] += jnp.sum(dya_m[0][:, cols] * ypre, axis=0, keepdims=True)
            dpw[g] += _dot(jnp.transpose(d).astype(BF16), dyp[0:TS])

        e4[0:H, :] = cg_b[0] * xi_b[0] * nb
        e4[H:H + TS, :] = cg_m[0] * xi_m[0]
        dyb = dyb_m[0]
        e5[0:TS, :] = dyb * bg_m[0]
        e5[TS:E, :] = dyb_f[0] * bg_f[0] * nf
        dcz = e5[0:TS, :]
        cz = None
        dz = None
        for k in range(SCONV_K):
            zk = e4[H - 2 + k:H - 2 + k + TS, :]
            wk = sw[k:k + 1, :]
            cz = wk * zk if cz is None else cz + wk * zk
            t = wk * e5[2 - k:2 - k + TS, :]
            dz = t if dz is None else dz + t
            dsw[k:k + 1, :] += jnp.sum(dcz * zk, axis=0, keepdims=True)
        dp_ref[0, :, BW:2 * BW] = (dz * cg_m[0]).astype(BF16)
        dp_ref[1, :, 0:BW] = (dyb * cz).astype(BF16)
        dp_ref[1, :, BW:2 * BW] = (dz * xi_m[0]).astype(BF16)

        sgm = _sig(cb_m[0])
        ra[0, 0:H, :] = ca_b[0] * _sig(cb_b[0]) * nb
        ra[0, H:H + TS, :] = ca_m[0] * sgm
        ra[0, H + TS:H + E, :] = ca_f[0] * _sig(cb_f[0]) * nf
        _fill_shifted(ra, H + E)
        fwd_offs = [H - (CCONV_K - 1) + k for k in range(CCONV_K)]
        _taps(ra, cw, fwd_offs, E, e4)
        yh, rstd = _ln_stats(e4[0:E, :])
        y2 = yh * clg[...] + clb[...]
        s2 = _sig(y2)
        e1[0:TS, :] = dyc_m[0]
        e1[TS:E, :] = dyc_f[0] * nf
        dy2 = e1[0:E, :] * (s2 * (1.0 + y2 * (1.0 - s2)))
        dclg[...] += jnp.sum((dy2 * yh)[0:TS], axis=0, keepdims=True)
        dclb[...] += jnp.sum(dy2[0:TS], axis=0, keepdims=True)
        rb[0, 0:E, :] = _ln_bwd(dy2 * clg[...], yh, rstd)
        _fill_shifted(rb, E)
        _taps(rb, cw, [CCONV_K - 1 - k for k in range(CCONV_K)], TS, e5)
        _tap_grads(ra, fwd_offs, rb, TS, dcw)
        dy0 = e5[0:TS, :]
        dp_ref[2, :, 0:BW] = (dy0 * sgm).astype(BF16)
        dp_ref[2, :, BW:2 * BW] = (dy0 * ca_m[0] * (sgm * (1.0 - sgm))).astype(BF16)

        pu = du_m[0]
        pv = dv_m[0]
        u, tu = _gelu(pu)
        v, tv = _gelu(pv)
        vh, vr = _ln_stats(v)
        vn = vh * slg[...] + slb[...]
        dyd = dyd_m[0]
        mask, mask_t = _sgu_masks()
        for h in range(4):
            wm = jnp.where(mask, gw[h], 0.0).astype(BF16)
            wmt = jnp.where(mask_t, gwt[h], 0.0).astype(BF16)
            cs = slice(h * GW, (h + 1) * GW)
            for n in range(TS // SGU_BLOCK):
                rs = slice(n * SGU_BLOCK, (n + 1) * SGU_BLOCK)
                vb = vn[rs, cs].astype(BF16)
                z = _dot(wm, vb) + gbias[h]
                dzb = dyd[rs, cs] * u[rs, cs]
                dz16 = dzb.astype(BF16)
                e3[rs, cs] = dyd[rs, cs] * z
                e4[rs, cs] = _dot(wmt, dz16)
                dgw[h] += jnp.where(mask, _dot_nt(dz16, vb), 0.0)
                dgb[h] += dzb
        dvn = e4[0:TS, :]
        dslg[...] += jnp.sum(dvn * vh, axis=0, keepdims=True)
        dslb[...] += jnp.sum(dvn, axis=0, keepdims=True)
        dv = _ln_bwd(dvn * slg[...], vh, vr)
        dp_ref[3, :, 0:BW] = (e3[0:TS, :] * _gelu_grad(pu, tu)).astype(BF16)
        dp_ref[3, :, BW:2 * BW] = (dv * _gelu_grad(pv, tv)).astype(BF16)

        @pl.when(i == n_t - 1)
        def _():
            for h in range(4):
                dgb[h] = jnp.broadcast_to(jnp.sum(dgb[h], axis=1, keepdims=True), dgb.shape[1:])

    params = [pool_w, pool_wt, pool_scale, sconv_w, cconv_w, cln_g, cln_b, sln_g, sln_b, sgu_w, sgu_wt, sgu_bias]
    args = [proj] * 16 + [dy] * 7 + params + [dproj_gates]
    in_specs = [back(0, 0), main(0, 0), back(0, 1), main(0, 1), main(1, 0), front(1, 0), back(1, 1), main(1, 1),
                back(2, 0), main(2, 0), front(2, 0), back(2, 1), main(2, 1), front(2, 1), main(3, 0), main(3, 1),
                main(0, 0), front(0, 0), main(1, 0), front(1, 0), main(2, 0), front(2, 0), main(3, 0)]
    in_specs += [full(a) for a in params] + [ANY]
    small = [SDS(pool_w.shape, F32), SDS(pool_scale.shape, F32), SDS(sconv_w.shape, F32), SDS(cconv_w.shape, F32),
             SDS(cln_g.shape, F32), SDS(cln_b.shape, F32), SDS(sln_g.shape, F32), SDS(sln_b.shape, F32),
             SDS(sgu_w.shape, F32), SDS(sgu_bias.shape, F32)]
    out_specs = [BS((4, TS, D), lambda i: (0, i, 0))] + [full(s) for s in small]
    return _pcall(body, name, (n_t,), in_specs, out_specs, [SDS(dproj_gates.shape, BF16)] + small,
                  [pltpu.VMEM((TS + 2 * H, BW), F32)] * 5 + [pltpu.VMEM((8, TS + 2 * H, BW), F32)] * 2,
                  ("arbitrary",), args, comm, aliases={len(args) - 1: 0})


def _merge_fwd(y, proj, w_up, w_out, x, name, comm=None):
    _, S, BW = y.shape
    D = x.shape[1]
    tm = _tile(S, 256, 16)

    def body(y_ref, pg_ref, wu_ref, wo_ref, x_ref, o_ref, m_ref):
        merged = None
        for g in range(4):
            t = _sig(pg_ref[g]) * _dot(y_ref[g], wu_ref[g])
            merged = t if merged is None else merged + t
        m16 = merged.astype(BF16)
        m_ref[...] = m16
        o_ref[...] = x_ref[...] + _dot(m16, wo_ref[...])

    (o, m), extra = _pcall(
        body, name, (S // tm,),
        [BS((4, tm, BW), lambda i: (0, i, 0)), BS((4, tm, D), lambda i: (1, i, 0)),
         BS((4, BW, D), lambda i: (0, 0, 0)), BS((D, D), lambda i: (0, 0)), BS((tm, D), lambda i: (i, 0))],
        [BS((tm, D), lambda i: (i, 0)), BS((tm, D), lambda i: (i, 0))],
        [SDS((S, D), F32), SDS((S, D), BF16)], [], ("parallel",), (y, proj, w_up, w_out, x), comm)
    return o, m, extra


def _merge_bwd(dx, y, proj, w_up, w_out, name, comm=None):
    _, S, BW = y.shape
    D = dx.shape[1]
    tm = _tile(S, 256, 16)

    def body(dx_ref, y_ref, pg_ref, wu_ref, wo_ref, dup_ref, dp_ref, dy_ref):
        dm = _dot_nt(dx_ref[...].astype(BF16), wo_ref[...])
        for g in range(4):
            gate = _sig(pg_ref[g])
            up = _dot(y_ref[g], wu_ref[g])
            dup = (dm * gate).astype(BF16)
            dup_ref[g] = dup
            dp_ref[g] = (dm * up * (gate * (1.0 - gate))).astype(BF16)
            dy_ref[g] = _dot_nt(dup, wu_ref[g])

    res, extra = _pcall(
        body, name, (S // tm,),
        [BS((tm, D), lambda i: (i, 0)), BS((4, tm, BW), lambda i: (0, i, 0)), BS((4, tm, D), lambda i: (1, i, 0)),
         BS((4, BW, D), lambda i: (0, 0, 0)), BS((D, D), lambda i: (0, 0))],
        [BS((4, tm, D), lambda i: (0, i, 0)), BS((4, tm, D), lambda i: (1, i, 0)), BS((4, tm, BW), lambda i: (0, i, 0))],
        [SDS((4, S, D), BF16), SDS((8, S, D), BF16), SDS((4, S, BW), F32)], [], ("parallel",),
        (dx, y, proj, w_up, w_out), comm)
    return res, extra


def _adamw(w, g, m, v):
    m = ADAM_B1 * m + (1.0 - ADAM_B1) * g
    v = ADAM_B2 * v + (1.0 - ADAM_B2) * (g * g)
    m_hat = m / (1.0 - ADAM_B1 ** ADAM_STEP)
    v_hat = v / (1.0 - ADAM_B2 ** ADAM_STEP)
    delta = -ADAM_LR * (m_hat / (jnp.sqrt(v_hat) + ADAM_EPS) + ADAM_WD * w)
    return delta, m, v


def _adamw_sharded(parts, w, m, v, name, comm=None):
    L, R, C = w.shape
    tr = _tile(R, 256, 16)

    def body(*refs):
        p_refs = refs[:L]
        w_ref, m_ref, v_ref, g_out, d_out, m_out, v_out = refs[L:]
        l = pl.program_id(0)
        g = None
        for d in range(N_DEV):
            t = p_refs[0][d].astype(F32)
            for j in range(1, L):
                t = jnp.where(l == j, p_refs[j][d].astype(F32), t)
            g = t if g is None else g + t
        dl, mn, vn = _adamw(w_ref[0], g, m_ref[0], v_ref[0])
        g_out[0] = g
        d_out[0] = dl
        m_out[0] = mn
        v_out[0] = vn

    def part_spec(j):
        return BS((N_DEV, tr, C), lambda l, r: (0, jnp.where(l == j, r, 0), 0))

    blk = BS((1, tr, C), lambda l, r: (l, r, 0))
    return _pcall(body, name, (L, R // tr), [part_spec(j) for j in range(L)] + [blk, blk, blk], [blk] * 4,
                  [SDS((L, R, C), F32)] * 4, [], ("parallel", "parallel"), (*parts, w, m, v), comm)


def _adamw_replicated(gathered, layout, wmv, name):
    n_b = len(gathered)
    n_p = len(layout)

    def body(*refs):
        bufs = refs[:n_b]
        prm = refs[n_b:n_b + 3 * n_p]
        outs = refs[n_b + 3 * n_p:n_b + 7 * n_p]
        sums = refs[n_b + 7 * n_p:]
        for b in range(n_b):
            s = bufs[b][0]
            for d in range(1, N_DEV):
                s = s + bufs[b][d]
            sums[b][...] = s
        for p, (b, r0, nr) in enumerate(layout):
            g = sums[b][r0:r0 + nr, :]
            d, mn, vn = _adamw(prm[3 * p][...], g, prm[3 * p + 1][...], prm[3 * p + 2][...])
            outs[4 * p][...] = g
            outs[4 * p + 1][...] = d
            outs[4 * p + 2][...] = mn
            outs[4 * p + 3][...] = vn

    flat = [a for t in wmv for a in t]
    out_shape = []
    for (w, _, _) in wmv:
        out_shape += [SDS(w.shape, F32)] * 4
    out_shape += [SDS(g.shape[1:], F32) for g in gathered]
    return pl.pallas_call(
        body, name=name, out_shape=out_shape,
        compiler_params=pltpu.CompilerParams(vmem_limit_bytes=V7X_VMEM_LIMIT),
    )(*gathered, *flat)


def _adamw_small(g, w, m, v, name):
    def body(g_ref, w_ref, m_ref, v_ref, d_out, m_out, v_out):
        d, mn, vn = _adamw(w_ref[...], g_ref[...], m_ref[...], v_ref[...])
        d_out[...] = d
        m_out[...] = mn
        v_out[...] = vn

    return pl.pallas_call(body, name=name, out_shape=[SDS(w.shape, F32)] * 3)(g, w, m, v)


def _pad_rows(a, rows):
    return jnp.pad(a, ((0, rows - a.shape[0]), (0, 0)))


def kernel(x, ffn1_norm, ffn1_w13, ffn1_w2, mix_norm, w_in, pool_w, pool_scale, sconv_w, cconv_w, cconv_ln_g, cconv_ln_b, sgu_ln_g, sgu_ln_b, sgu_w, sgu_b, w_up, w_out, ffn2_norm, ffn2_w13, ffn2_w2, final_norm, loss_target, m_ffn1_norm, m_ffn1_w13, m_ffn1_w2, m_mix_norm, m_w_in, m_pool_w, m_pool_scale, m_sconv_w, m_cconv_w, m_cconv_ln_g, m_cconv_ln_b, m_sgu_ln_g, m_sgu_ln_b, m_sgu_w, m_sgu_b, m_w_up, m_w_out, m_ffn2_norm, m_ffn2_w13, m_ffn2_w2, m_final_norm, v_ffn1_norm, v_ffn1_w13, v_ffn1_w2, v_mix_norm, v_w_in, v_pool_w, v_pool_scale, v_sconv_w, v_cconv_w, v_cconv_ln_g, v_cconv_ln_b, v_sgu_ln_g, v_sgu_ln_b, v_sgu_w, v_sgu_b, v_w_up, v_w_out, v_ffn2_norm, v_ffn2_w13, v_ffn2_w2, v_final_norm):
    P = dict(locals())
    L = ffn1_norm.shape[0]
    S, D = x.shape[1], x.shape[2]
    BW = D // 2
    GW = BW // 4
    F = ffn1_w2.shape[1] * N_DEV
    fs = ffn1_w13.shape[2]
    cw = sconv_w.shape[2]
    me = 4 * lax.axis_index("x") + 2 * lax.axis_index("y") + lax.axis_index("c")

    big = ["ffn1_w13", "ffn1_w2", "w_in", "w_up", "w_out", "ffn2_w13", "ffn2_w2"]
    shards = [(jnp.swapaxes(P[n], 1, 2) if n.endswith("w13") else P[n]).astype(BF16) for n in big]
    conv_local = jnp.concatenate([sconv_w, cconv_w], axis=1)

    def gather_of(units):
        return _gather_comm(shards, [(big.index(n), l) for n, l in units])

    def ready(n, g):
        if n.endswith("w13"):
            return g.reshape(2, F, D)
        if n.endswith("w2"):
            return g.reshape(F, D)
        if n == "w_up":
            return jnp.transpose(g, (1, 2, 0, 3)).reshape(4, BW, D)
        if n == "w_out":
            return g.reshape(D, D)
        return g

    W = {}

    def take(units, arrays):
        for (n, l), g in zip(units, arrays):
            W[n, l] = ready(n, g)

    first_units = [("ffn1_w13", 0), ("ffn1_w2", 0)]
    plan = {("ffn1_up", 0): [("w_in", 0)],
            ("ffn1_down", 0): [("w_up", 0), ("w_out", 0)],
            ("proj", 0): [("ffn2_w13", 0), ("ffn2_w2", 0)],
            ("mixers", 0): [("ffn1_w13", 1)], ("merge", 0): [("ffn1_w2", 1)],
            ("ffn2_up", 0): [("w_in", 1)], ("ffn2_down", 0): [("w_up", 1), ("w_out", 1)],
            ("ffn1_up", 1): [("ffn2_w13", 1)], ("ffn1_down", 1): [("ffn2_w2", 1)]}
    assert L <= 2

    def carried(key):
        units = [u for u in plan.get(key, []) if u[1] < L]
        return units, (gather_of(units) if units else None)

    first = _gather_comm(shards + [conv_local], [(big.index(n), l) for n, l in first_units] + [(len(big), None)])
    got = _run_comm(first, "gather_first_weights")
    take(first_units, got[:2])
    conv_full = jnp.transpose(got[2], (1, 2, 0, 3)).reshape(L, SCONV_K + CCONV_K, N_DEV * cw)
    sconv_full = conv_full[:, :SCONV_K]
    cconv_full = conv_full[:, SCONV_K:]

    sgu_bias = jnp.broadcast_to(sgu_b[:, :, :, None], sgu_b.shape + (GW,))
    pool_wt = jnp.swapaxes(pool_w, 2, 3)
    sgu_wt = jnp.swapaxes(sgu_w, 2, 3)

    def row(a, l):
        return a[l][None, :]

    saved = []
    xc = x[0]
    for l in range(L):
        sv = {}
        for tag in ("ffn1", None, "ffn2"):
            if tag is None:
                sv["x_mix"] = xc
                h = _rmsnorm_fwd(xc, row(mix_norm, l), "mix_norm_fwd")
                units, comm = carried(("proj", l))
                proj, extra = _matmul_fwd(h, W["w_in", l], "proj_fwd", comm)
                take(units, extra)
                units, comm = carried(("mixers", l))
                y, extra = _mixers_fwd(proj, pool_w[l], row(pool_scale, l), sconv_full[l], cconv_full[l],
                                       row(cconv_ln_g, l), row(cconv_ln_b, l), row(sgu_ln_g, l), row(sgu_ln_b, l),
                                       sgu_w[l], sgu_bias[l], "mixers_fwd", comm)
                take(units, extra)
                units, comm = carried(("merge", l))
                xc, merged, extra = _merge_fwd(y, proj, W["w_up", l], W["w_out", l], xc, "merge_fwd", comm)
                take(units, extra)
                sv.update(h_mix=h, proj=proj, y=y, merged=merged)
            else:
                sv["x_" + tag] = xc
                h = _rmsnorm_fwd(xc, row(P[tag + "_norm"], l), "ffn_norm_fwd")
                units, comm = carried((tag + "_up", l))
                ab, extra = _matmul_fwd(h, W[tag + "_w13", l], "ffn_up_fwd", comm, w_t=True, out_dtype=BF16)
                take(units, extra)
                units, comm = carried((tag + "_down", l))
                xc, extra = _swiglu_down(ab, W[tag + "_w2", l], xc, "ffn_down_fwd", comm)
                take(units, extra)
                sv.update({"h_" + tag: h, "ab_" + tag: ab})
        saved.append(sv)

    loss_part, dx, d_final = _final_loss(xc, final_norm[None, :], loss_target[0], "loss_head")
    loss = lax.psum(loss_part[0, 0], MESH_AXES)

    R = {}
    second = []

    def rest_of_sends():
        keys = [k for k, _, _ in second]
        comm = None
        if second:
            comm = _scatter_comm([g for _, g, _ in second], PEERS_OTHER_CORE, [r for _, _, r in second])
        second.clear()
        return keys, comm

    def settle(keys, arrays):
        for k, a in zip(keys, arrays):
            R[k] = a

    wide = ["ffn1_norm", "mix_norm", "ffn2_norm", "final_norm"]
    half = ["pool_scale", "cconv_ln_g", "cconv_ln_b", "sgu_ln_g", "sgu_ln_b"]
    narrow = ["pool_w", "sgu_w", "sgu_b"]
    small_names = wide + half + narrow
    small_g = [dict() for _ in range(L)]
    widths = []
    for n in small_names:
        if P[n].shape[-1] not in widths:
            widths.append(P[n].shape[-1])
    layout, conv_at = {}, {}

    def pack(width):
        def stack_layers(n):
            return jnp.stack([small_g[l][n] for l in range(L)], axis=0)

        parts, r0 = [], 0
        for n in small_names:
            if P[n].shape[-1] != width:
                continue
            g = d_final if n == "final_norm" else stack_layers(n).reshape(-1, width)
            layout[n] = (widths.index(width), r0, g.shape[0])
            parts.append(_pad_rows(g, -(-g.shape[0] // 8) * 8))
            r0 += parts[-1].shape[0]
        if width == N_DEV * cw:
            conv_g = jnp.concatenate([stack_layers("sconv_w"), stack_layers("cconv_w")], axis=1)
            conv_g = conv_g.reshape(L * (SCONV_K + CCONV_K), N_DEV * cw)
            conv_at.update(b=widths.index(width), r0=r0, rows=conv_g.shape[0])
            parts.append(_pad_rows(conv_g, -(-conv_g.shape[0] // 8) * 8))
        return jnp.concatenate(parts, axis=0)

    gathered_small = [None] * len(widths)
    for l in reversed(range(L)):
        sv = saved[l]
        sg = small_g[l]
        for tag in ("ffn2", None, "ffn1"):
            if tag is None:
                keys, comm = rest_of_sends()
                (dup, dproj, dy), extra = _merge_bwd(dx, sv["y"], sv["proj"], W["w_up", l], W["w_out", l],
                                                     "merge_bwd", comm)
                settle(keys, extra)
                g_out, _ = _matmul_tn(sv["merged"][None], dx[None], 1, "w_out_grad")
                g_up, _ = _matmul_tn(sv["y"], dup, 1, "w_up_grad")
                g_out = g_out.reshape(N_DEV, D // N_DEV, D)
                g_up = jnp.transpose(g_up.reshape(4, BW, N_DEV, D // N_DEV), (2, 0, 1, 3)).reshape(
                    N_DEV, 4 * BW, D // N_DEV)
                res, (R["w_out", l], R["w_up", l]) = _mixers_bwd(
                    sv["proj"], dy, dproj, pool_w[l], pool_wt[l], row(pool_scale, l), sconv_full[l], cconv_full[l],
                    row(cconv_ln_g, l), row(cconv_ln_b, l), row(sgu_ln_g, l), row(sgu_ln_b, l), sgu_w[l], sgu_wt[l],
                    sgu_bias[l], "mixers_bwd", _scatter_comm([g_out, g_up]))
                dproj = res[0]
                (sg["pool_w"], sg["pool_scale"], sg["sconv_w"], sg["cconv_w"], sg["cconv_ln_g"], sg["cconv_ln_b"],
                 sg["sgu_ln_g"], sg["sgu_ln_b"], sg["sgu_w"], dgb) = res[1:]
                sg["sgu_b"] = dgb[:, :, 0]
                comm = None
                if l == 0:
                    early = [w for w in widths if w != D]
                    comm = _gather_comm([pack(w) for w in early], [(b, None) for b in range(len(early))])
                g_in, extra = _matmul_tn(sv["h_mix"][None], dproj, N_DEV, "w_in_grad", comm)
                if l == 0:
                    for w, g in zip(early, extra):
                        gathered_small[widths.index(w)] = g
                dx, sg["mix_norm"], (r_in,) = _matmul_nt_normbwd(
                    dproj, W["w_in", l], sv["x_mix"], row(mix_norm, l), dx, "proj_bwd",
                    _scatter_comm([g_in], PEERS_SAME_CORE))
                second.append((("w_in", l), g_in, r_in))
            else:
                keys, comm = rest_of_sends()
                dab, sh, extra = _ffn_bwd_hidden(dx, W[tag + "_w2", l], sv["ab_" + tag], "ffn_hidden_bwd", comm)
                settle(keys, extra)
                g_w2, _ = _matmul_tn(sh[None], dx[None], 1, "ffn_w2_grad")
                g_w2 = g_w2.reshape(N_DEV, F // N_DEV, D)
                g_w13, (R[tag + "_w2", l],) = _matmul_tn(dab, sv["h_" + tag][None], 1, "ffn_w13_grad",
                                                         _scatter_comm([g_w2]), b_shared=True)
                g_w13 = g_w13.reshape(N_DEV, fs, D)
                dx, sg[tag + "_norm"], (r_w13,) = _matmul_nt_normbwd(
                    dab, W[tag + "_w13", l], sv["x_" + tag], row(P[tag + "_norm"], l), dx, "ffn_up_bwd",
                    _scatter_comm([g_w13], PEERS_SAME_CORE), w_t=True)
                second.append(((tag + "_w13", l), g_w13, r_w13))
    grad_x = dx[None]
    out = {}

    def as2d(n, a):
        if n == "final_norm":
            return a.reshape(1, D)
        return a.reshape(-1, a.shape[-1])

    for i, n in enumerate(["ffn2_w13", "w_in", "ffn2_w2", "w_up", "w_out", "ffn1_w2", "ffn1_w13"]):
        shp = P[n].shape
        if n.endswith("w13"):
            flat, back = (lambda a: jnp.swapaxes(a, 1, 2)), (lambda a: jnp.swapaxes(a, 1, 2))
        else:
            rows, cols = math.prod(shp[1:-1]), shp[-1]
            flat, back = (lambda a: a.reshape(L, rows, cols)), (lambda a: a.reshape(shp))
        keys, comm = [], None
        if i == 0:
            keys, comm = rest_of_sends()
        elif i == 1:
            comm = _gather_comm([pack(D)], [(0, None)])
        res, extra = _adamw_sharded([R[n, l] for l in range(L)], flat(P[n]), flat(P["m_" + n]), flat(P["v_" + n]),
                                    "adamw_sharded", comm)
        if i == 0:
            settle(keys, extra)
        elif i == 1:
            gathered_small[widths.index(D)] = extra[0]
        out[n] = tuple(back(a) for a in res)

    res = _adamw_replicated(gathered_small, [layout[n] for n in small_names],
                            [(as2d(n, P[n]), as2d(n, P["m_" + n]), as2d(n, P["v_" + n])) for n in small_names],
                            "adamw_replicated")
    for p, n in enumerate(small_names):
        out[n] = tuple(a.reshape(P[n].shape) for a in res[4 * p:4 * p + 4])
    conv_sum = res[4 * len(small_names) + conv_at["b"]][conv_at["r0"]:conv_at["r0"] + conv_at["rows"]]
    conv_mine = lax.dynamic_slice_in_dim(conv_sum, me * cw, cw, axis=1)

    def conv2d(a, b):
        return jnp.concatenate([a, b], axis=1).reshape(L * (SCONV_K + CCONV_K), cw)

    cd, cm, cv = _adamw_small(conv_mine, conv2d(sconv_w, cconv_w), conv2d(m_sconv_w, m_cconv_w),
                              conv2d(v_sconv_w, v_cconv_w), "adamw_conv")
    for n, sl in (("sconv_w", slice(0, SCONV_K)), ("cconv_w", slice(SCONV_K, SCONV_K + CCONV_K))):
        out[n] = tuple(a.reshape(L, SCONV_K + CCONV_K, cw)[:, sl] for a in (conv_mine, cd, cm, cv))

    order = ["ffn1_norm", "ffn1_w13", "ffn1_w2", "mix_norm", "w_in", "pool_w", "pool_scale", "sconv_w", "cconv_w",
             "cconv_ln_g", "cconv_ln_b", "sgu_ln_g", "sgu_ln_b", "sgu_w", "sgu_b", "w_up", "w_out", "ffn2_norm",
             "ffn2_w13", "ffn2_w2", "final_norm"]
    return (loss, grad_x, *[out[n][0] for n in order], *[out[n][1] for n in order],
            *[out[n][2] for n in order], *[out[n][3] for n in order])
```

```python
import functools
import math

import jax
import jax.numpy as jnp
from jax import lax
from jax.experimental import pallas as pl
from jax.experimental.pallas import tpu as pltpu

F32 = jnp.float32
BF16 = jnp.bfloat16
EPS = 1e-6
ADAM_LR = 0.001
ADAM_B1 = 0.9
ADAM_B2 = 0.999
ADAM_EPS = 1e-08
ADAM_WD = 0.01
ADAM_STEP = 10
SGU_BLOCK = 128
SGU_CHUNK = 64
SCONV_K = 3
CCONV_K = 31
HALO = 32
V7X_VMEM_LIMIT = 48 * 1024 * 1024
MESH_AXES = ("x", "y", "c")
N_DEV = 8
_GELU_C0 = math.sqrt(2.0 / math.pi)
_GELU_C1 = 0.044715

BS = pl.BlockSpec
SDS = jax.ShapeDtypeStruct
ANY = pl.BlockSpec(memory_space=pl.ANY)


def _tile(n, pref, align=128):
    if n <= pref:
        return n
    t = pref - pref % align
    while t > 0:
        if n % t == 0:
            return t
        t -= align
    return n


def _sig(v):
    return 1.0 / (1.0 + jnp.exp(-v))


def _gelu(v):
    t = jnp.tanh(_GELU_C0 * (v + _GELU_C1 * (v * v * v)))
    return 0.5 * v * (1.0 + t), t


def _gelu_grad(v, t):
    return 0.5 * (1.0 + t) + 0.5 * v * (1.0 - t * t) * (_GELU_C0 * (1.0 + 3.0 * _GELU_C1 * v * v))


def _ln_stats(v):
    mu = jnp.mean(v, axis=-1, keepdims=True)
    vc = v - mu
    var = jnp.mean(vc * vc, axis=-1, keepdims=True)
    rstd = lax.rsqrt(var + EPS)
    return vc * rstd, rstd


def _ln_bwd(dvh, vh, rstd):
    return rstd * (dvh - jnp.mean(dvh, axis=-1, keepdims=True) - vh * jnp.mean(dvh * vh, axis=-1, keepdims=True))


def _dot(a, b):
    return jnp.dot(a, b, preferred_element_type=F32)


def _dot_nt(a, b):
    return lax.dot_general(a, b, (((1,), (1,)), ((), ())), preferred_element_type=F32)


def _dot_tn(a, b):
    return lax.dot_general(a, b, (((0,), (0,)), ((), ())), preferred_element_type=F32)


def _mesh_pos():
    return lax.axis_index("x"), lax.axis_index("y"), lax.axis_index("c")


class _Comm:
    def __init__(self, ins, out_shape, sems, start, finish, aliases=None):
        self.ins, self.out_shape, self.sems, self.start, self.finish = ins, out_shape, sems, start, finish
        self.aliases = aliases or {}


def _gather_comm(shards, units):
    n_u = len(units)

    def tools(ins, dsts, sems):
        send_sems, recv_sems, local_sems = sems
        x, y, c = _mesh_pos()
        chips = [(1 - x, y), (x, 1 - y), (1 - x, 1 - y)]

        def src_of(o):
            t, l = units[o]
            return ins[t] if l is None else ins[t].at[l]

        def row(o, p):
            return dsts[o].at[4 * p[0] + 2 * p[1] + p[2]]

        def copy(o, k, block, to, own=False):
            return pltpu.make_async_remote_copy(
                src_ref=src_of(o) if own else row(o, block), dst_ref=row(o, block),
                send_sem=send_sems.at[o * 7 + k], recv_sem=recv_sems.at[o * 7 + k],
                device_id=to, device_id_type=pl.DeviceIdType.MESH)

        def local(o):
            return pltpu.make_async_copy(src_of(o), row(o, (x, y, c)), local_sems.at[o])

        def first(o):
            return [copy(o, 1 + j, (x, y, c), (*chip, c), own=True) for j, chip in enumerate(chips)] + [
                copy(o, 0, (x, y, c), (x, y, 1 - c), own=True)]

        return (x, y, c), chips, copy, local, first

    def start(ins, dsts, sems):
        _, _, _, local, first = tools(ins, dsts, sems)
        for o in range(n_u):
            local(o).start()
            for cp in first(o):
                cp.start()

    def finish(ins, dsts, sems):
        (x, y, c), chips, copy, local, first = tools(ins, dsts, sems)
        me, sibling = (x, y, c), (x, y, 1 - c)
        passed = []
        for o in range(n_u):
            for j, chip in enumerate(chips):
                copy(o, 1 + j, (*chip, c), me).wait_recv()
                cp = copy(o, 4 + j, (*chip, c), sibling)
                cp.start()
                passed.append(cp)
        for o in range(n_u):
            copy(o, 0, sibling, me).wait_recv()
            for j, chip in enumerate(chips):
                copy(o, 4 + j, (*chip, 1 - c), me).wait_recv()
        for o in range(n_u):
            for cp in first(o):
                cp.wait_send()
        for cp in passed:
            cp.wait_send()
        for o in range(n_u):
            local(o).wait()

    out_shape = []
    for t, l in units:
        shp = shards[t].shape if l is None else shards[t].shape[1:]
        out_shape.append(SDS((N_DEV,) + tuple(shp), shards[t].dtype))
    sems = [pltpu.SemaphoreType.DMA((7 * n_u,)), pltpu.SemaphoreType.DMA((7 * n_u,)), pltpu.SemaphoreType.DMA((n_u,))]
    return _Comm(list(shards), out_shape, sems, start, finish)


PEERS_ALL = (1, 2, 3, 4, 5, 6, 7)
PEERS_SAME_CORE = (1, 2, 4, 6)
PEERS_OTHER_CORE = (3, 5, 7)


def _scatter_comm(parts, peers=PEERS_ALL, into=None):
    n_u = len(parts)

    def tools(ins, dsts, sems):
        send_sems, recv_sems, local_sems = sems
        x, y, c = _mesh_pos()
        me = 4 * x + 2 * y + c

        def peer(k):
            return ((x + ((k >> 2) & 1)) % 2, (y + ((k >> 1) & 1)) % 2, (c + (k & 1)) % 2)

        def copy(u, k, wait=False):
            p = peer(k)
            pi = 4 * p[0] + 2 * p[1] + p[2]
            return pltpu.make_async_remote_copy(
                src_ref=ins[u].at[pi], dst_ref=dsts[u].at[pi if wait else me],
                send_sem=send_sems.at[u * 7 + k - 1], recv_sem=recv_sems.at[u * 7 + k - 1],
                device_id=p, device_id_type=pl.DeviceIdType.MESH)

        def local(u):
            return pltpu.make_async_copy(ins[u].at[me], dsts[u].at[me], local_sems.at[u])

        return copy, local

    def start(ins, dsts, sems):
        copy, local = tools(ins, dsts, sems)
        for u in range(n_u):
            if into is None:
                local(u).start()
            for k in peers:
                copy(u, k).start()

    def finish(ins, dsts, sems):
        copy, local = tools(ins, dsts, sems)
        for u in range(n_u):
            for k in peers:
                copy(u, k, wait=True).wait()
            if into is None:
                local(u).wait()

    sems = [pltpu.SemaphoreType.DMA((7 * n_u,)), pltpu.SemaphoreType.DMA((7 * n_u,)), pltpu.SemaphoreType.DMA((n_u,))]
    aliases = {} if into is None else {n_u + u: u for u in range(n_u)}
    return _Comm(list(parts) + list(into or []), [SDS(p.shape, p.dtype) for p in parts], sems, start, finish, aliases)


def _run_comm(comm, name):
    n_i, n_o = len(comm.ins), len(comm.out_shape)

    def body(*refs):
        comm.start(refs[:n_i], refs[n_i:n_i + n_o], refs[n_i + n_o:])
        comm.finish(refs[:n_i], refs[n_i:n_i + n_o], refs[n_i + n_o:])

    return pl.pallas_call(
        body, name=name, out_shape=comm.out_shape, in_specs=[ANY] * n_i, out_specs=[ANY] * n_o,
        scratch_shapes=comm.sems,
    )(*comm.ins)


def _pcall(body, name, grid, in_specs, out_specs, out_shape, scratch, sem, args, comm=None, aliases=None):
    n_i, n_o, n_s = len(in_specs), len(out_specs), len(scratch)
    aliases = aliases or {}
    if comm is None:
        res = pl.pallas_call(
            body, name=name, grid=grid, in_specs=in_specs, out_specs=out_specs, out_shape=out_shape,
            scratch_shapes=scratch, input_output_aliases=aliases,
            compiler_params=pltpu.CompilerParams(dimension_semantics=sem, vmem_limit_bytes=V7X_VMEM_LIMIT),
        )(*args)
        return res, []
    n_ci, n_co = len(comm.ins), len(comm.out_shape)

    def wrapped(*refs):
        ins = refs[:n_i]
        cins = refs[n_i:n_i + n_ci]
        outs = refs[n_i + n_ci:n_i + n_ci + n_o]
        couts = refs[n_i + n_ci + n_o:n_i + n_ci + n_o + n_co]
        rest = refs[n_i + n_ci + n_o + n_co:]
        ids = [pl.program_id(d) for d in range(len(grid))]
        first = functools.reduce(jnp.logical_and, [i == 0 for i in ids])
        last = functools.reduce(jnp.logical_and, [i == g - 1 for i, g in zip(ids, grid)])

        @pl.when(first)
        def _():
            comm.start(cins, couts, rest[n_s:])

        body(*ins, *outs, *rest[:n_s])

        @pl.when(last)
        def _():
            comm.finish(cins, couts, rest[n_s:])

    res = pl.pallas_call(
        wrapped, name=name, grid=grid, in_specs=list(in_specs) + [ANY] * n_ci,
        out_specs=list(out_specs) + [ANY] * n_co, out_shape=list(out_shape) + list(comm.out_shape),
        scratch_shapes=list(scratch) + list(comm.sems),
        input_output_aliases={**aliases, **{n_i + ci: n_o + co for ci, co in comm.aliases.items()}},
        compiler_params=pltpu.CompilerParams(dimension_semantics=("arbitrary",) * len(grid),
                                             vmem_limit_bytes=V7X_VMEM_LIMIT),
    )(*args, *comm.ins)
    return res[:n_o], res[n_o:]


def _rmsnorm_fwd(x, g, name):
    S, D = x.shape
    tm = _tile(S, 512, 16)

    def body(x_ref, g_ref, h_ref):
        xv = x_ref[...]
        r = lax.rsqrt(jnp.mean(xv * xv, axis=-1, keepdims=True) + EPS)
        h_ref[...] = (xv * r * g_ref[...]).astype(BF16)

    (h,), _ = _pcall(body, name, (S // tm,), [BS((tm, D), lambda i: (i, 0)), BS((1, D), lambda i: (0, 0))],
                     [BS((tm, D), lambda i: (i, 0))], [SDS((S, D), BF16)], [], ("parallel",), (x, g))
    return h


def _matmul_fwd(a, w, name, comm=None, w_t=False, out_dtype=F32):
    S, K = a.shape
    C = w.shape[0]
    Fc = w.shape[1] if w_t else w.shape[2]
    tn = _tile(Fc, 1408)
    tm = _tile(S, 1024, 16)

    def body(a_ref, w_ref, o_ref):
        p = _dot_nt(a_ref[...], w_ref[0]) if w_t else _dot(a_ref[...], w_ref[0])
        o_ref[0] = p.astype(out_dtype)

    w_spec = BS((1, tn, K), lambda c, n, i: (c, n, 0)) if w_t else BS((1, K, tn), lambda c, n, i: (c, 0, n))
    (o,), extra = _pcall(
        body, name, (C, Fc // tn, S // tm), [BS((tm, K), lambda c, n, i: (i, 0)), w_spec],
        [BS((1, tm, tn), lambda c, n, i: (c, i, n))], [SDS((C, S, Fc), out_dtype)], [],
        ("parallel", "parallel", "parallel"), (a, w), comm)
    return o, extra


def _swiglu_down(ab, w2, x, name, comm=None):
    _, S, F = ab.shape
    D = w2.shape[1]
    tk = _tile(F, 1408)
    tm = _tile(S, 512, 16)

    def body(ab_ref, w_ref, x_ref, o_ref):
        k = pl.program_id(1)
        a = ab_ref[0].astype(F32)
        s = a * _sig(a) * ab_ref[1].astype(F32)
        p = 0.5 * _dot(s.astype(BF16), w_ref[...])

        @pl.when(k == 0)
        def _():
            o_ref[...] = x_ref[...] + p

        @pl.when(k > 0)
        def _():
            o_ref[...] += p

    (o,), extra = _pcall(
        body, name, (S // tm, F // tk),
        [BS((2, tm, tk), lambda i, k: (0, i, k)), BS((tk, D), lambda i, k: (k, 0)), BS((tm, D), lambda i, k: (i, 0))],
        [BS((tm, D), lambda i, k: (i, 0))], [SDS((S, D), F32)], [], ("parallel", "arbitrary"), (ab, w2, x), comm)
    return o, extra


def _ffn_bwd_hidden(dy, w2, ab, name, comm=None):
    S, D = dy.shape
    F = w2.shape[0]
    tk = _tile(F, 1408)
    tm = _tile(S, 512, 16)
    te = _tile(tm, 128, 16)

    def body(dy_ref, w_ref, ab_ref, dab_ref, s_ref, ds_ref):
        ds_ref[...] = 0.5 * _dot_nt(dy_ref[...].astype(BF16), w_ref[...])
        for r0 in range(0, tm, te):
            rows = slice(r0, r0 + te)
            ds = ds_ref[rows, :]
            a = ab_ref[0, rows, :].astype(F32)
            b = ab_ref[1, rows, :].astype(F32)
            sg = _sig(a)
            sa = a * sg
            dab_ref[0, rows, :] = (ds * b * (sg * (1.0 + a * (1.0 - sg)))).astype(BF16)
            dab_ref[1, rows, :] = (ds * sa).astype(BF16)
            s_ref[rows, :] = (0.5 * (sa * b)).astype(BF16)

    (dab, sh), extra = _pcall(
        body, name, (F // tk, S // tm),
        [BS((tm, D), lambda k, i: (i, 0)), BS((tk, D), lambda k, i: (k, 0)), BS((2, tm, tk), lambda k, i: (0, i, k))],
        [BS((2, tm, tk), lambda k, i: (0, i, k)), BS((tm, tk), lambda k, i: (i, k))],
        [SDS((2, S, F), BF16), SDS((S, F), BF16)], [pltpu.VMEM((tm, tk), F32)], ("parallel", "parallel"),
        (dy, w2, ab), comm)
    return dab, sh, extra


def _matmul_tn(a, b, n_c, name, comm=None, b_shared=False):
    G, S, M = a.shape
    _, _, Fc = b.shape
    C = n_c
    tM = _tile(M, 1408)
    tn = _tile(Fc, 1408)
    ts = _tile(S, 1024, 16)
    n_s = S // ts

    def body(a_ref, b_ref, o_ref, acc):
        s = pl.program_id(4)
        p = _dot_tn(a_ref[0].astype(BF16), b_ref[0].astype(BF16))

        @pl.when(s == 0)
        def _():
            acc[...] = p

        @pl.when(s > 0)
        def _():
            acc[...] += p

        @pl.when(s == n_s - 1)
        def _():
            o_ref[0] = acc[...].astype(BF16)

    (o,), extra = _pcall(
        body, name, (G, M // tM, C, Fc // tn, n_s),
        [BS((1, ts, tM), lambda g, m, c, n, s: (g, s, m)), BS((1, ts, tn), lambda g, m, c, n, s: (c if b_shared else g * C + c, s, n))],
        [BS((1, tM, tn), lambda g, m, c, n, s: (g * C + c, m, n))], [SDS((G * C, M, Fc), BF16)],
        [pltpu.VMEM((tM, tn), F32)], ("parallel", "parallel", "parallel", "parallel", "arbitrary"), (a, b), comm)
    return o, extra


def _matmul_nt_normbwd(b, w, x, gam, dres, name, comm=None, w_t=False):
    C, S, Fc = b.shape
    D = w.shape[2] if w_t else w.shape[1]
    tk = _tile(Fc, 1408)
    tm = _tile(S, 1024, 16)
    te = _tile(tm, 256, 8)
    nk = Fc // tk

    def body(b_ref, w_ref, x_ref, g_ref, r_ref, dx_ref, dg_ref):
        i, c, k = pl.program_id(0), pl.program_id(1), pl.program_id(2)
        p = _dot(b_ref[0], w_ref[0]) if w_t else _dot_nt(b_ref[0], w_ref[0])
        first = jnp.logical_and(c == 0, k == 0)

        @pl.when(first)
        def _():
            dx_ref[...] = p

        @pl.when(jnp.logical_not(first))
        def _():
            dx_ref[...] += p

        @pl.when(jnp.logical_and(c == C - 1, k == nk - 1))
        def _():
            dgp = None
            for r0 in range(0, tm, te):
                rows = slice(r0, r0 + te)
                xv = x_ref[rows, :]
                r = lax.rsqrt(jnp.mean(xv * xv, axis=-1, keepdims=True) + EPS)
                xn = xv * r
                dh = dx_ref[rows, :]
                dxn = dh * g_ref[...]
                dx_ref[rows, :] = r_ref[rows, :] + r * (dxn - xn * jnp.mean(dxn * xn, axis=-1, keepdims=True))
                t = jnp.sum(dh * xn, axis=0, keepdims=True)
                dgp = t if dgp is None else dgp + t

            @pl.when(i == 0)
            def _():
                dg_ref[...] = dgp

            @pl.when(i > 0)
            def _():
                dg_ref[...] += dgp

    once = dict(pipeline_mode=pl.Buffered(1))
    (dx, dg), extra = _pcall(
        body, name, (S // tm, C, nk),
        [BS((1, tm, tk), lambda i, c, k: (c, i, k)),
         BS((1, tk, D), lambda i, c, k: (c, k, 0)) if w_t else BS((1, D, tk), lambda i, c, k: (c, 0, k)),
         BS((tm, D), lambda i, c, k: (i, 0), **once), BS((1, D), lambda i, c, k: (0, 0)),
         BS((tm, D), lambda i, c, k: (i, 0), **once)],
        [BS((tm, D), lambda i, c, k: (i, 0)), BS((1, D), lambda i, c, k: (0, 0))],
        [SDS((S, D), F32), SDS((1, D), F32)], [],
        ("arbitrary", "arbitrary", "arbitrary"), (b, w, x, gam, dres), comm)
    return dx, dg, extra


def _final_loss(x, gam, target, name):
    S, D = x.shape
    tm = _tile(S, 512, 8)

    def body(x_ref, g_ref, t_ref, loss_ref, dx_ref, dg_ref):
        i = pl.program_id(0)
        xv = x_ref[...]
        r = lax.rsqrt(jnp.mean(xv * xv, axis=-1, keepdims=True) + EPS)
        xn = xv * r
        err = xn * g_ref[...] - t_ref[...]
        part = 0.5 * jnp.sum(jnp.mean(err * err, axis=-1, keepdims=True), axis=0, keepdims=True)
        dy = err * (1.0 / D)
        dxn = dy * g_ref[...]
        dx_ref[...] = r * (dxn - xn * jnp.mean(dxn * xn, axis=-1, keepdims=True))
        dgp = jnp.sum(dy * xn, axis=0, keepdims=True)
        lp = jnp.broadcast_to(part, loss_ref.shape)

        @pl.when(i == 0)
        def _():
            dg_ref[...] = dgp
            loss_ref[...] = lp

        @pl.when(i > 0)
        def _():
            dg_ref[...] += dgp
            loss_ref[...] += lp

    res, _ = _pcall(
        body, name, (S // tm,),
        [BS((tm, D), lambda i: (i, 0)), BS((1, D), lambda i: (0, 0)), BS((tm, D), lambda i: (i, 0))],
        [BS((8, 128), lambda i: (0, 0)), BS((tm, D), lambda i: (i, 0)), BS((1, D), lambda i: (0, 0))],
        [SDS((8, 128), F32), SDS((S, D), F32), SDS((1, D), F32)], [], ("arbitrary",), (x, gam, target))
    return res


CONV_CHUNK = 32


def _fill_shifted(rot, n):
    for b in range(1, 8):
        rot[b, 0:n - 8, :] = rot[0, b:b + n - 8, :]


def _window(rot, off, r0, rows):
    b = off % 8
    return rot[b, off - b + r0:off - b + r0 + rows, :]


def _taps(rot, w_ref, offs, n_rows, out):
    for r0 in range(0, n_rows, CONV_CHUNK):
        acc = None
        for k, off in enumerate(offs):
            t = w_ref[k:k + 1, :] * _window(rot, off, r0, CONV_CHUNK)
            acc = t if acc is None else acc + t
        out[r0:r0 + CONV_CHUNK, :] = acc


def _tap_grads(rot, offs, g_plane, n_rows, dw_ref):
    for k, off in enumerate(offs):
        acc = None
        for r0 in range(0, n_rows, CONV_CHUNK):
            p = g_plane[0, r0:r0 + CONV_CHUNK, :] * _window(rot, off, r0, CONV_CHUNK)
            acc = p if acc is None else acc + p
        dw_ref[k:k + 1, :] += jnp.sum(acc, axis=0, keepdims=True)


def _sgu_masks():
    ii = lax.broadcasted_iota(jnp.int32, (SGU_BLOCK, SGU_BLOCK), 0) // SGU_CHUNK
    jj = lax.broadcasted_iota(jnp.int32, (SGU_BLOCK, SGU_BLOCK), 1) // SGU_CHUNK
    return jj <= ii, ii <= jj


def _mixers_fwd(proj, pool_w, pool_scale, sconv_w, cconv_w, cln_g, cln_b, sln_g, sln_b, sgu_w, sgu_bias, name,
                comm=None):
    _, S, D = proj.shape
    BW = D // 2
    GW = BW // 4
    TS = _tile(S, 256, SGU_BLOCK)
    H = HALO
    hb = TS // H

    def main(blk, col):
        return BS((1, TS, BW), lambda i: (blk, i, col))

    def back(blk, col):
        return BS((1, H, BW), lambda i: (blk, jnp.maximum(i * hb - 1, 0), col))

    def full(a):
        nd = a.ndim
        return BS(a.shape, lambda i: (0,) * nd)

    def body(pa_m, pa_b, xi_m, xi_b, bg_m, cg_m, cg_b, ca_m, ca_b, cb_m, cb_b, du_m, dv_m,
             pw, ps, sw, cw, clg, clb, slg, slb, gw, gbias, y_ref, e1, e2, e3):
        i = pl.program_id(0)
        nb = jnp.where(i > 0, 1.0, 0.0).astype(F32)
        rows = i * TS + lax.broadcasted_iota(jnp.int32, (TS, 1), 0)

        e1[0:H, :] = pa_b[0] * nb
        e1[H:H + TS, :] = pa_m[0]
        for g in range(4):
            cols = slice(g * GW, (g + 1) * GW)
            win = 2 << g
            wsum = e1[H:H + TS, cols]
            for k in range(1, win):
                wsum = wsum + e1[H - k:H - k + TS, cols]
            cnt = jnp.minimum(rows + 1, win).astype(F32)
            d = wsum / cnt - e1[H:H + TS, cols]
            yg = _dot(d.astype(BF16), pw[g].astype(BF16)) * ps[:, cols]
            y_ref[0, :, cols] = yg.astype(BF16)

        e2[0:H, :] = cg_b[0] * xi_b[0] * nb
        e2[H:H + TS, :] = cg_m[0] * xi_m[0]
        cz = sw[0:1, :] * e2[H - 2:H - 2 + TS, :]
        for k in range(1, SCONV_K):
            cz = cz + sw[k:k + 1, :] * e2[H - 2 + k:H - 2 + k + TS, :]
        y_ref[1] = (bg_m[0] * cz).astype(BF16)

        e3[0, 0:H, :] = ca_b[0] * _sig(cb_b[0]) * nb
        e3[0, H:H + TS, :] = ca_m[0] * _sig(cb_m[0])
        _fill_shifted(e3, H + TS)
        _taps(e3, cw, [H - (CCONV_K - 1) + k for k in range(CCONV_K)], TS, e1)
        yh, _ = _ln_stats(e1[0:TS, :])
        y2 = yh * clg[...] + clb[...]
        y_ref[2] = (y2 * _sig(y2)).astype(BF16)

        u, _ = _gelu(du_m[0])
        v, _ = _gelu(dv_m[0])
        vh, _ = _ln_stats(v)
        vn = vh * slg[...] + slb[...]
        mask, _ = _sgu_masks()
        for h in range(4):
            wm = jnp.where(mask, gw[h], 0.0).astype(BF16)
            cs = slice(h * GW, (h + 1) * GW)
            for n in range(TS // SGU_BLOCK):
                rs = slice(n * SGU_BLOCK, (n + 1) * SGU_BLOCK)
                z = _dot(wm, vn[rs, cs].astype(BF16)) + gbias[h]
                y_ref[3, rs, cs] = (u[rs, cs] * z).astype(BF16)

    args = [proj] * 13 + [pool_w, pool_scale, sconv_w, cconv_w, cln_g, cln_b, sln_g, sln_b, sgu_w, sgu_bias]
    in_specs = [main(0, 0), back(0, 0), main(0, 1), back(0, 1), main(1, 0), main(1, 1), back(1, 1),
                main(2, 0), back(2, 0), main(2, 1), back(2, 1), main(3, 0), main(3, 1)]
    in_specs += [full(a) for a in args[13:]]
    (y,), extra = _pcall(body, name, (S // TS,), in_specs, [BS((4, TS, BW), lambda i: (0, i, 0))],
                         [SDS((4, S, BW), BF16)],
                         [pltpu.VMEM((H + TS, BW), F32)] * 2 + [pltpu.VMEM((8, H + TS, BW), F32)], ("parallel",),
                         args, comm)
    return y, extra


def _mixers_bwd(proj, dy, dproj_gates, pool_w, pool_wt, pool_scale, sconv_w, cconv_w, cln_g, cln_b, sln_g, sln_b,
                sgu_w, sgu_wt, sgu_bias, name, comm=None):
    _, S, D = proj.shape
    BW = D // 2
    GW = BW // 4
    TS = _tile(S, 256, SGU_BLOCK)
    H = HALO
    hb = TS // H
    n_t = S // TS
    E = TS + H

    def main(blk, col):
        return BS((1, TS, BW), lambda i: (blk, i, col))

    def back(blk, col):
        return BS((1, H, BW), lambda i: (blk, jnp.maximum(i * hb - 1, 0), col))

    def front(blk, col):
        return BS((1, H, BW), lambda i: (blk, jnp.minimum((i + 1) * hb, S // H - 1), col))

    def full(a):
        nd = a.ndim
        return BS(a.shape, lambda i: (0,) * nd)

    def body(pa_b, pa_m, xi_b, xi_m, bg_m, bg_f, cg_b, cg_m, ca_b, ca_m, ca_f, cb_b, cb_m, cb_f, du_m, dv_m,
             dya_m, dya_f, dyb_m, dyb_f, dyc_m, dyc_f, dyd_m,
             pw, pwt, ps, sw, cw, clg, clb, slg, slb, gw, gwt, gbias, _gates_in,
             dp_ref, dpw, dps, dsw, dcw, dclg, dclb, dslg, dslb, dgw, dgb,
             e1, e2, e3, e4, e5, ra, rb):
        i = pl.program_id(0)
        nb = jnp.where(i > 0, 1.0, 0.0).astype(F32)
        nf = jnp.where(i < n_t - 1, 1.0, 0.0).astype(F32)
        rows_m = i * TS + lax.broadcasted_iota(jnp.int32, (TS, 1), 0)
        rows_e = i * TS + lax.broadcasted_iota(jnp.int32, (E, 1), 0)

        @pl.when(i == 0)
        def _():
            for r in (dpw, dps, dsw, dcw, dclg, dclb, dslg, dslb, dgw, dgb):
                r[...] = jnp.zeros(r.shape, F32)

        e1[0:H, :] = pa_b[0] * nb
        e1[H:H + TS, :] = pa_m[0]
        e2[0:TS, :] = dya_m[0] * ps[...]
        e2[TS:E, :] = dya_f[0] * ps[...] * nf
        for g in range(4):
            cols = slice(g * GW, (g + 1) * GW)
            win = 2 << g
            a_m = e1[H:H + TS, cols]
            wsum = a_m
            for k in range(1, win):
                wsum = wsum + e1[H - k:H - k + TS, cols]
            d = wsum / jnp.minimum(rows_m + 1, win).astype(F32) - a_m
            d16 = d.astype(BF16)
            dyp = e2[0:E, cols].astype(BF16)
            dd = _dot(dyp, pwt[g].astype(BF16))
            e3[0:E, cols] = dd / jnp.minimum(rows_e + 1, win).astype(F32)
            da = e3[0:TS, cols] - dd[0:TS]
            for k in range(1, win):
                da = da + e3[k:k + TS, cols]
            dp_ref[0, :, cols] = da.astype(BF16)
            ypre = _dot(d16, pw[g].astype(BF16))
            dps[:, cols] += jnp.sum(dya_m[0][:, cols] * ypre, axis=0, keepdims=True)
            dpw[g] += _dot(jnp.transpose(d).astype(BF16), dyp[0:TS])

        e4[0:H, :] = cg_b[0] * xi_b[0] * nb
        e4[H:H + TS, :] = cg_m[0] * xi_m[0]
        dyb = dyb_m[0]
        e5[0:TS, :] = dyb * bg_m[0]
        e5[TS:E, :] = dyb_f[0] * bg_f[0] * nf
        dcz = e5[0:TS, :]
        cz = None
        dz = None
        for k in range(SCONV_K):
            zk = e4[H - 2 + k:H - 2 + k + TS, :]
            wk = sw[k:k + 1, :]
            cz = wk * zk if cz is None else cz + wk * zk
            t = wk * e5[2 - k:2 - k + TS, :]
            dz = t if dz is None else dz + t
            dsw[k:k + 1, :] += jnp.sum(dcz * zk, axis=0, keepdims=True)
        dp_ref[0, :, BW:2 * BW] = (dz * cg_m[0]).astype(BF16)
        dp_ref[1, :, 0:BW] = (dyb * cz).astype(BF16)
        dp_ref[1, :, BW:2 * BW] = (dz * xi_m[0]).astype(BF16)

        sgm = _sig(cb_m[0])
        ra[0, 0:H, :] = ca_b[0] * _sig(cb_b[0]) * nb
        ra[0, H:H + TS, :] = ca_m[0] * sgm
        ra[0, H + TS:H + E, :] = ca_f[0] * _sig(cb_f[0]) * nf
        _fill_shifted(ra, H + E)
        fwd_offs = [H - (CCONV_K - 1) + k for k in range(CCONV_K)]
        _taps(ra, cw, fwd_offs, E, e4)
        yh, rstd = _ln_stats(e4[0:E, :])
        y2 = yh * clg[...] + clb[...]
        s2 = _sig(y2)
        e1[0:TS, :] = dyc_m[0]
        e1[TS:E, :] = dyc_f[0] * nf
        dy2 = e1[0:E, :] * (s2 * (1.0 + y2 * (1.0 - s2)))
        dclg[...] += jnp.sum((dy2 * yh)[0:TS], axis=0, keepdims=True)
        dclb[...] += jnp.sum(dy2[0:TS], axis=0, keepdims=True)
        rb[0, 0:E, :] = _ln_bwd(dy2 * clg[...], yh, rstd)
        _fill_shifted(rb, E)
        _taps(rb, cw, [CCONV_K - 1 - k for k in range(CCONV_K)], TS, e5)
        _tap_grads(ra, fwd_offs, rb, TS, dcw)
        dy0 = e5[0:TS, :]
        dp_ref[2, :, 0:BW] = (dy0 * sgm).astype(BF16)
        dp_ref[2, :, BW:2 * BW] = (dy0 * ca_m[0] * (sgm * (1.0 - sgm))).astype(BF16)

        pu = du_m[0]
        pv = dv_m[0]
        u, tu = _gelu(pu)
        v, tv = _gelu(pv)
        vh, vr = _ln_stats(v)
        vn = vh * slg[...] + slb[...]
        dyd = dyd_m[0]
        mask, mask_t = _sgu_masks()
        for h in range(4):
            wm = jnp.where(mask, gw[h], 0.0).astype(BF16)
            wmt = jnp.where(mask_t, gwt[h], 0.0).astype(BF16)
            cs = slice(h * GW, (h + 1) * GW)
            for n in range(TS // SGU_BLOCK):
                rs = slice(n * SGU_BLOCK, (n + 1) * SGU_BLOCK)
                vb = vn[rs, cs].astype(BF16)
                z = _dot(wm, vb) + gbias[h]
                dzb = dyd[rs, cs] * u[rs, cs]
                dz16 = dzb.astype(BF16)
                e3[rs, cs] = dyd[rs, cs] * z
                e4[rs, cs] = _dot(wmt, dz16)
                dgw[h] += jnp.where(mask, _dot_nt(dz16, vb), 0.0)
                dgb[h] += dzb
        dvn = e4[0:TS, :]
        dslg[...] += jnp.sum(dvn * vh, axis=0, keepdims=True)
        dslb[...] += jnp.sum(dvn, axis=0, keepdims=True)
        dv = _ln_bwd(dvn * slg[...], vh, vr)
        dp_ref[3, :, 0:BW] = (e3[0:TS, :] * _gelu_grad(pu, tu)).astype(BF16)
        dp_ref[3, :, BW:2 * BW] = (dv * _gelu_grad(pv, tv)).astype(BF16)

        @pl.when(i == n_t - 1)
        def _():
            for h in range(4):
                dgb[h] = jnp.broadcast_to(jnp.sum(dgb[h], axis=1, keepdims=True), dgb.shape[1:])

    params = [pool_w, pool_wt, pool_scale, sconv_w, cconv_w, cln_g, cln_b, sln_g, sln_b, sgu_w, sgu_wt, sgu_bias]
    args = [proj] * 16 + [dy] * 7 + params + [dproj_gates]
    in_specs = [back(0, 0), main(0, 0), back(0, 1), main(0, 1), main(1, 0), front(1, 0), back(1, 1), main(1, 1),
                back(2, 0), main(2, 0), front(2, 0), back(2, 1), main(2, 1), front(2, 1), main(3, 0), main(3, 1),
                main(0, 0), front(0, 0), main(1, 0), front(1, 0), main(2, 0), front(2, 0), main(3, 0)]
    in_specs += [full(a) for a in params] + [ANY]
    small = [SDS(pool_w.shape, F32), SDS(pool_scale.shape, F32), SDS(sconv_w.shape, F32), SDS(cconv_w.shape, F32),
             SDS(cln_g.shape, F32), SDS(cln_b.shape, F32), SDS(sln_g.shape, F32), SDS(sln_b.shape, F32),
             SDS(sgu_w.shape, F32), SDS(sgu_bias.shape, F32)]
    out_specs = [BS((4, TS, D), lambda i: (0, i, 0))] + [full(s) for s in small]
    return _pcall(body, name, (n_t,), in_specs, out_specs, [SDS(dproj_gates.shape, BF16)] + small,
                  [pltpu.VMEM((TS + 2 * H, BW), F32)] * 5 + [pltpu.VMEM((8, TS + 2 * H, BW), F32)] * 2,
                  ("arbitrary",), args, comm, aliases={len(args) - 1: 0})


def _merge_fwd(y, proj, w_up, w_out, x, name, comm=None):
    _, S, BW = y.shape
    D = x.shape[1]
    tm = _tile(S, 256, 16)

    def body(y_ref, pg_ref, wu_ref, wo_ref, x_ref, o_ref, m_ref):
        merged = None
        for g in range(4):
            t = _sig(pg_ref[g]) * _dot(y_ref[g], wu_ref[g])
            merged = t if merged is None else merged + t
        m16 = merged.astype(BF16)
        m_ref[...] = m16
        o_ref[...] = x_ref[...] + _dot(m16, wo_ref[...])

    (o, m), extra = _pcall(
        body, name, (S // tm,),
        [BS((4, tm, BW), lambda i: (0, i, 0)), BS((4, tm, D), lambda i: (1, i, 0)),
         BS((4, BW, D), lambda i: (0, 0, 0)), BS((D, D), lambda i: (0, 0)), BS((tm, D), lambda i: (i, 0))],
        [BS((tm, D), lambda i: (i, 0)), BS((tm, D), lambda i: (i, 0))],
        [SDS((S, D), F32), SDS((S, D), BF16)], [], ("parallel",), (y, proj, w_up, w_out, x), comm)
    return o, m, extra


def _merge_bwd(dx, y, proj, w_up, w_out, name, comm=None):
    _, S, BW = y.shape
    D = dx.shape[1]
    tm = _tile(S, 256, 16)

    def body(dx_ref, y_ref, pg_ref, wu_ref, wo_ref, dup_ref, dp_ref, dy_ref):
        dm = _dot_nt(dx_ref[...].astype(BF16), wo_ref[...])
        for g in range(4):
            gate = _sig(pg_ref[g])
            up = _dot(y_ref[g], wu_ref[g])
            dup = (dm * gate).astype(BF16)
            dup_ref[g] = dup
            dp_ref[g] = (dm * up * (gate * (1.0 - gate))).astype(BF16)
            dy_ref[g] = _dot_nt(dup, wu_ref[g])

    res, extra = _pcall(
        body, name, (S // tm,),
        [BS((tm, D), lambda i: (i, 0)), BS((4, tm, BW), lambda i: (0, i, 0)), BS((4, tm, D), lambda i: (1, i, 0)),
         BS((4, BW, D), lambda i: (0, 0, 0)), BS((D, D), lambda i: (0, 0))],
        [BS((4, tm, D), lambda i: (0, i, 0)), BS((4, tm, D), lambda i: (1, i, 0)), BS((4, tm, BW), lambda i: (0, i, 0))],
        [SDS((4, S, D), BF16), SDS((8, S, D), BF16), SDS((4, S, BW), F32)], [], ("parallel",),
        (dx, y, proj, w_up, w_out), comm)
    return res, extra


def _adamw(w, g, m, v):
    m = ADAM_B1 * m + (1.0 - ADAM_B1) * g
    v = ADAM_B2 * v + (1.0 - ADAM_B2) * (g * g)
    m_hat = m / (1.0 - ADAM_B1 ** ADAM_STEP)
    v_hat = v / (1.0 - ADAM_B2 ** ADAM_STEP)
    delta = -ADAM_LR * (m_hat / (jnp.sqrt(v_hat) + ADAM_EPS) + ADAM_WD * w)
    return delta, m, v


def _adamw_sharded(parts, w, m, v, name, comm=None):
    L, R, C = w.shape
    tr = _tile(R, 256, 16)

    def body(*refs):
        p_refs = refs[:L]
        w_ref, m_ref, v_ref, g_out, d_out, m_out, v_out = refs[L:]
        l = pl.program_id(0)
        g = None
        for d in range(N_DEV):
            t = p_refs[0][d].astype(F32)
            for j in range(1, L):
                t = jnp.where(l == j, p_refs[j][d].astype(F32), t)
            g = t if g is None else g + t
        dl, mn, vn = _adamw(w_ref[0], g, m_ref[0], v_ref[0])
        g_out[0] = g
        d_out[0] = dl
        m_out[0] = mn
        v_out[0] = vn

    def part_spec(j):
        return BS((N_DEV, tr, C), lambda l, r: (0, jnp.where(l == j, r, 0), 0))

    blk = BS((1, tr, C), lambda l, r: (l, r, 0))
    return _pcall(body, name, (L, R // tr), [part_spec(j) for j in range(L)] + [blk, blk, blk], [blk] * 4,
                  [SDS((L, R, C), F32)] * 4, [], ("parallel", "parallel"), (*parts, w, m, v), comm)


def _adamw_replicated(gathered, layout, wmv, name):
    n_b = len(gathered)
    n_p = len(layout)

    def body(*refs):
        bufs = refs[:n_b]
        prm = refs[n_b:n_b + 3 * n_p]
        outs = refs[n_b + 3 * n_p:n_b + 7 * n_p]
        sums = refs[n_b + 7 * n_p:]
        for b in range(n_b):
            s = bufs[b][0]
            for d in range(1, N_DEV):
                s = s + bufs[b][d]
            sums[b][...] = s
        for p, (b, r0, nr) in enumerate(layout):
            g = sums[b][r0:r0 + nr, :]
            d, mn, vn = _adamw(prm[3 * p][...], g, prm[3 * p + 1][...], prm[3 * p + 2][...])
            outs[4 * p][...] = g
            outs[4 * p + 1][...] = d
            outs[4 * p + 2][...] = mn
            outs[4 * p + 3][...] = vn

    flat = [a for t in wmv for a in t]
    out_shape = []
    for (w, _, _) in wmv:
        out_shape += [SDS(w.shape, F32)] * 4
    out_shape += [SDS(g.shape[1:], F32) for g in gathered]
    return pl.pallas_call(
        body, name=name, out_shape=out_shape,
        compiler_params=pltpu.CompilerParams(vmem_limit_bytes=V7X_VMEM_LIMIT),
    )(*gathered, *flat)


def _adamw_small(g, w, m, v, name):
    def body(g_ref, w_ref, m_ref, v_ref, d_out, m_out, v_out):
        d, mn, vn = _adamw(w_ref[...], g_ref[...], m_ref[...], v_ref[...])
        d_out[...] = d
        m_out[...] = mn
        v_out[...] = vn

    return pl.pallas_call(body, name=name, out_shape=[SDS(w.shape, F32)] * 3)(g, w, m, v)


def _pad_rows(a, rows):
    return jnp.pad(a, ((0, rows - a.shape[0]), (0, 0)))


def kernel(x, ffn1_norm, ffn1_w13, ffn1_w2, mix_norm, w_in, pool_w, pool_scale, sconv_w, cconv_w, cconv_ln_g, cconv_ln_b, sgu_ln_g, sgu_ln_b, sgu_w, sgu_b, w_up, w_out, ffn2_norm, ffn2_w13, ffn2_w2, final_norm, loss_target, m_ffn1_norm, m_ffn1_w13, m_ffn1_w2, m_mix_norm, m_w_in, m_pool_w, m_pool_scale, m_sconv_w, m_cconv_w, m_cconv_ln_g, m_cconv_ln_b, m_sgu_ln_g, m_sgu_ln_b, m_sgu_w, m_sgu_b, m_w_up, m_w_out, m_ffn2_norm, m_ffn2_w13, m_ffn2_w2, m_final_norm, v_ffn1_norm, v_ffn1_w13, v_ffn1_w2, v_mix_norm, v_w_in, v_pool_w, v_pool_scale, v_sconv_w, v_cconv_w, v_cconv_ln_g, v_cconv_ln_b, v_sgu_ln_g, v_sgu_ln_b, v_sgu_w, v_sgu_b, v_w_up, v_w_out, v_ffn2_norm, v_ffn2_w13, v_ffn2_w2, v_final_norm):
    P = dict(locals())
    L = ffn1_norm.shape[0]
    S, D = x.shape[1], x.shape[2]
    BW = D // 2
    GW = BW // 4
    F = ffn1_w2.shape[1] * N_DEV
    fs = ffn1_w13.shape[2]
    cw = sconv_w.shape[2]
    me = 4 * lax.axis_index("x") + 2 * lax.axis_index("y") + lax.axis_index("c")

    big = ["ffn1_w13", "ffn1_w2", "w_in", "w_up", "w_out", "ffn2_w13", "ffn2_w2"]
    shards = [(jnp.swapaxes(P[n], 1, 2) if n.endswith("w13") else P[n]).astype(BF16) for n in big]
    conv_local = jnp.concatenate([sconv_w, cconv_w], axis=1)

    def gather_of(units):
        return _gather_comm(shards, [(big.index(n), l) for n, l in units])

    def ready(n, g):
        if n.endswith("w13"):
            return g.reshape(2, F, D)
        if n.endswith("w2"):
            return g.reshape(F, D)
        if n == "w_up":
            return jnp.transpose(g, (1, 2, 0, 3)).reshape(4, BW, D)
        if n == "w_out":
            return g.reshape(D, D)
        return g

    W = {}

    def take(units, arrays):
        for (n, l), g in zip(units, arrays):
            W[n, l] = ready(n, g)

    first_units = [("ffn1_w13", 0), ("ffn1_w2", 0)]
    plan = {("ffn1_up", 0): [("w_in", 0)],
            ("ffn1_down", 0): [("w_up", 0), ("w_out", 0)],
            ("proj", 0): [("ffn2_w13", 0), ("ffn2_w2", 0)],
            ("mixers", 0): [("ffn1_w13", 1)], ("merge", 0): [("ffn1_w2", 1)],
            ("ffn2_up", 0): [("w_in", 1)], ("ffn2_down", 0): [("w_up", 1), ("w_out", 1)],
            ("ffn1_up", 1): [("ffn2_w13", 1)], ("ffn1_down", 1): [("ffn2_w2", 1)]}
    assert L <= 2

    def carried(key):
        units = [u for u in plan.get(key, []) if u[1] < L]
        return units, (gather_of(units) if units else None)

    first = _gather_comm(shards + [conv_local], [(big.index(n), l) for n, l in first_units] + [(len(big), None)])
    got = _run_comm(first, "gather_first_weights")
    take(first_units, got[:2])
    conv_full = jnp.transpose(got[2], (1, 2, 0, 3)).reshape(L, SCONV_K + CCONV_K, N_DEV * cw)
    sconv_full = conv_full[:, :SCONV_K]
    cconv_full = conv_full[:, SCONV_K:]

    sgu_bias = jnp.broadcast_to(sgu_b[:, :, :, None], sgu_b.shape + (GW,))
    pool_wt = jnp.swapaxes(pool_w, 2, 3)
    sgu_wt = jnp.swapaxes(sgu_w, 2, 3)

    def row(a, l):
        return a[l][None, :]

    saved = []
    xc = x[0]
    for l in range(L):
        sv = {}
        for tag in ("ffn1", None, "ffn2"):
            if tag is None:
                sv["x_mix"] = xc
                h = _rmsnorm_fwd(xc, row(mix_norm, l), "mix_norm_fwd")
                units, comm = carried(("proj", l))
                proj, extra = _matmul_fwd(h, W["w_in", l], "proj_fwd", comm)
                take(units, extra)
                units, comm = carried(("mixers", l))
                y, extra = _mixers_fwd(proj, pool_w[l], row(pool_scale, l), sconv_full[l], cconv_full[l],
                                       row(cconv_ln_g, l), row(cconv_ln_b, l), row(sgu_ln_g, l), row(sgu_ln_b, l),
                                       sgu_w[l], sgu_bias[l], "mixers_fwd", comm)
                take(units, extra)
                units, comm = carried(("merge", l))
                xc, merged, extra = _merge_fwd(y, proj, W["w_up", l], W["w_out", l], xc, "merge_fwd", comm)
                take(units, extra)
                sv.update(h_mix=h, proj=proj, y=y, merged=merged)
            else:
                sv["x_" + tag] = xc
                h = _rmsnorm_fwd(xc, row(P[tag + "_norm"], l), "ffn_norm_fwd")
                units, comm = carried((tag + "_up", l))
                ab, extra = _matmul_fwd(h, W[tag + "_w13", l], "ffn_up_fwd", comm, w_t=True, out_dtype=BF16)
                take(units, extra)
                units, comm = carried((tag + "_down", l))
                xc, extra = _swiglu_down(ab, W[tag + "_w2", l], xc, "ffn_down_fwd", comm)
                take(units, extra)
                sv.update({"h_" + tag: h, "ab_" + tag: ab})
        saved.append(sv)

    loss_part, dx, d_final = _final_loss(xc, final_norm[None, :], loss_target[0], "loss_head")
    loss = lax.psum(loss_part[0, 0], MESH_AXES)

    R = {}
    second = []

    def rest_of_sends():
        keys = [k for k, _, _ in second]
        comm = None
        if second:
            comm = _scatter_comm([g for _, g, _ in second], PEERS_OTHER_CORE, [r for _, _, r in second])
        second.clear()
        return keys, comm

    def settle(keys, arrays):
        for k, a in zip(keys, arrays):
            R[k] = a

    wide = ["ffn1_norm", "mix_norm", "ffn2_norm", "final_norm"]
    half = ["pool_scale", "cconv_ln_g", "cconv_ln_b", "sgu_ln_g", "sgu_ln_b"]
    narrow = ["pool_w", "sgu_w", "sgu_b"]
    small_names = wide + half + narrow
    small_g = [dict() for _ in range(L)]
    widths = []
    for n in small_names:
        if P[n].shape[-1] not in widths:
            widths.append(P[n].shape[-1])
    layout, conv_at = {}, {}

    def pack(width):
        def stack_layers(n):
            return jnp.stack([small_g[l][n] for l in range(L)], axis=0)

        parts, r0 = [], 0
        for n in small_names:
            if P[n].shape[-1] != width:
                continue
            g = d_final if n == "final_norm" else stack_layers(n).reshape(-1, width)
            layout[n] = (widths.index(width), r0, g.shape[0])
            parts.append(_pad_rows(g, -(-g.shape[0] // 8) * 8))
            r0 += parts[-1].shape[0]
        if width == N_DEV * cw:
            conv_g = jnp.concatenate([stack_layers("sconv_w"), stack_layers("cconv_w")], axis=1)
            conv_g = conv_g.reshape(L * (SCONV_K + CCONV_K), N_DEV * cw)
            conv_at.update(b=widths.index(width), r0=r0, rows=conv_g.shape[0])
            parts.append(_pad_rows(conv_g, -(-conv_g.shape[0] // 8) * 8))
        return jnp.concatenate(parts, axis=0)

    gathered_small = [None] * len(widths)
    for l in reversed(range(L)):
        sv = saved[l]
        sg = small_g[l]
        for tag in ("ffn2", None, "ffn1"):
            if tag is None:
                keys, comm = rest_of_sends()
                (dup, dproj, dy), extra = _merge_bwd(dx, sv["y"], sv["proj"], W["w_up", l], W["w_out", l],
                                                     "merge_bwd", comm)
                settle(keys, extra)
                g_out, _ = _matmul_tn(sv["merged"][None], dx[None], 1, "w_out_grad")
                g_up, _ = _matmul_tn(sv["y"], dup, 1, "w_up_grad")
                g_out = g_out.reshape(N_DEV, D // N_DEV, D)
                g_up = jnp.transpose(g_up.reshape(4, BW, N_DEV, D // N_DEV), (2, 0, 1, 3)).reshape(
                    N_DEV, 4 * BW, D // N_DEV)
                res, (R["w_out", l], R["w_up", l]) = _mixers_bwd(
                    sv["proj"], dy, dproj, pool_w[l], pool_wt[l], row(pool_scale, l), sconv_full[l], cconv_full[l],
                    row(cconv_ln_g, l), row(cconv_ln_b, l), row(sgu_ln_g, l), row(sgu_ln_b, l), sgu_w[l], sgu_wt[l],
                    sgu_bias[l], "mixers_bwd", _scatter_comm([g_out, g_up]))
                dproj = res[0]
                (sg["pool_w"], sg["pool_scale"], sg["sconv_w"], sg["cconv_w"], sg["cconv_ln_g"], sg["cconv_ln_b"],
                 sg["sgu_ln_g"], sg["sgu_ln_b"], sg["sgu_w"], dgb) = res[1:]
                sg["sgu_b"] = dgb[:, :, 0]
                comm = None
                if l == 0:
                    early = [w for w in widths if w != D]
                    comm = _gather_comm([pack(w) for w in early], [(b, None) for b in range(len(early))])
                g_in, extra = _matmul_tn(sv["h_mix"][None], dproj, N_DEV, "w_in_grad", comm)
                if l == 0:
                    for w, g in zip(early, extra):
                        gathered_small[widths.index(w)] = g
                dx, sg["mix_norm"], (r_in,) = _matmul_nt_normbwd(
                    dproj, W["w_in", l], sv["x_mix"], row(mix_norm, l), dx, "proj_bwd",
                    _scatter_comm([g_in], PEERS_SAME_CORE))
                second.append((("w_in", l), g_in, r_in))
            else:
                keys, comm = rest_of_sends()
                dab, sh, extra = _ffn_bwd_hidden(dx, W[tag + "_w2", l], sv["ab_" + tag], "ffn_hidden_bwd", comm)
                settle(keys, extra)
                g_w2, _ = _matmul_tn(sh[None], dx[None], 1, "ffn_w2_grad")
                g_w2 = g_w2.reshape(N_DEV, F // N_DEV, D)
                g_w13, (R[tag + "_w2", l],) = _matmul_tn(dab, sv["h_" + tag][None], 1, "ffn_w13_grad",
                                                         _scatter_comm([g_w2]), b_shared=True)
                g_w13 = g_w13.reshape(N_DEV, fs, D)
                dx, sg[tag + "_norm"], (r_w13,) = _matmul_nt_normbwd(
                    dab, W[tag + "_w13", l], sv["x_" + tag], row(P[tag + "_norm"], l), dx, "ffn_up_bwd",
                    _scatter_comm([g_w13], PEERS_SAME_CORE), w_t=True)
                second.append(((tag + "_w13", l), g_w13, r_w13))
    grad_x = dx[None]
    out = {}

    def as2d(n, a):
        if n == "final_norm":
            return a.reshape(1, D)
        return a.reshape(-1, a.shape[-1])

    for i, n in enumerate(["ffn2_w13", "w_in", "ffn2_w2", "w_up", "w_out", "ffn1_w2", "ffn1_w13"]):
        shp = P[n].shape
        if n.endswith("w13"):
            flat, back = (lambda a: jnp.swapaxes(a, 1, 2)), (lambda a: jnp.swapaxes(a, 1, 2))
        else:
            rows, cols = math.prod(shp[1:-1]), shp[-1]
            flat, back = (lambda a: a.reshape(L, rows, cols)), (lambda a: a.reshape(shp))
        keys, comm = [], None
        if i == 0:
            keys, comm = rest_of_sends()
        elif i == 1:
            comm = _gather_comm([pack(D)], [(0, None)])
        res, extra = _adamw_sharded([R[n, l] for l in range(L)], flat(P[n]), flat(P["m_" + n]), flat(P["v_" + n]),
                                    "adamw_sharded", comm)
        if i == 0:
            settle(keys, extra)
        elif i == 1:
            gathered_small[widths.index(D)] = extra[0]
        out[n] = tuple(back(a) for a in res)

    res = _adamw_replicated(gathered_small, [layout[n] for n in small_names],
                            [(as2d(n, P[n]), as2d(n, P["m_" + n]), as2d(n, P["v_" + n])) for n in small_names],
                            "adamw_replicated")
    for p, n in enumerate(small_names):
        out[n] = tuple(a.reshape(P[n].shape) for a in res[4 * p:4 * p + 4])
    conv_sum = res[4 * len(small_names) + conv_at["b"]][conv_at["r0"]:conv_at["r0"] + conv_at["rows"]]
    conv_mine = lax.dynamic_slice_in_dim(conv_sum, me * cw, cw, axis=1)

    def conv2d(a, b):
        return jnp.concatenate([a, b], axis=1).reshape(L * (SCONV_K + CCONV_K), cw)

    cd, cm, cv = _adamw_small(conv_mine, conv2d(sconv_w, cconv_w), conv2d(m_sconv_w, m_cconv_w),
                              conv2d(v_sconv_w, v_cconv_w), "adamw_conv")
    for n, sl in (("sconv_w", slice(0, SCONV_K)), ("cconv_w", slice(SCONV_K, SCONV_K + CCONV_K))):
        out[n] = tuple(a.reshape(L, SCONV_K + CCONV_K, cw)[:, sl] for a in (conv_mine, cd, cm, cv))

    order = ["ffn1_norm", "ffn1_w13", "ffn1_w2", "mix_norm", "w_in", "pool_w", "pool_scale", "sconv_w", "cconv_w",
             "cconv_ln_g", "cconv_ln_b", "sgu_ln_g", "sgu_ln_b", "sgu_w", "sgu_b", "w_up", "w_out", "ffn2_norm",
             "ffn2_w13", "ffn2_w2", "final_norm"]
    return (loss, grad_x, *[out[n][0] for n in order], *[out[n][1] for n in order],
            *[out[n][2] for n in order], *[out[n][3] for n in order])
```

```python
import functools
import math

import jax
import jax.numpy as jnp
from jax import lax
from jax.experimental import pallas as pl
from jax.experimental.pallas import tpu as pltpu

F32 = jnp.float32
BF16 = jnp.bfloat16
EPS = 1e-6
ADAM_LR = 0.001
ADAM_B1 = 0.9
ADAM_B2 = 0.999
ADAM_EPS = 1e-08
ADAM_WD = 0.01
ADAM_STEP = 10
SGU_BLOCK = 128
SGU_CHUNK = 64
SCONV_K = 3
CCONV_K = 31
HALO = 32
V7X_VMEM_LIMIT = 48 * 1024 * 1024
MESH_AXES = ("x", "y", "c")
N_DEV = 8
_GELU_C0 = math.sqrt(2.0 / math.pi)
_GELU_C1 = 0.044715

BS = pl.BlockSpec
SDS = jax.ShapeDtypeStruct
ANY = pl.BlockSpec(memory_space=pl.ANY)


def _tile(n, pref, align=128):
    if n <= pref:
        return n
    t = pref - pref % align
    while t > 0:
        if n % t == 0:
            return t
        t -= align
    return n


def _sig(v):
    return 1.0 / (1.0 + jnp.exp(-v))


def _gelu(v):
    t = jnp.tanh(_GELU_C0 * (v + _GELU_C1 * (v * v * v)))
    return 0.5 * v * (1.0 + t), t


def _gelu_grad(v, t):
    return 0.5 * (1.0 + t) + 0.5 * v * (1.0 - t * t) * (_GELU_C0 * (1.0 + 3.0 * _GELU_C1 * v * v))


def _ln_stats(v):
    mu = jnp.mean(v, axis=-1, keepdims=True)
    vc = v - mu
    var = jnp.mean(vc * vc, axis=-1, keepdims=True)
    rstd = lax.rsqrt(var + EPS)
    return vc * rstd, rstd


def _ln_bwd(dvh, vh, rstd):
    return rstd * (dvh - jnp.mean(dvh, axis=-1, keepdims=True) - vh * jnp.mean(dvh * vh, axis=-1, keepdims=True))


def _dot(a, b):
    return jnp.dot(a, b, preferred_element_type=F32)


def _dot_nt(a, b):
    return lax.dot_general(a, b, (((1,), (1,)), ((), ())), preferred_element_type=F32)


def _dot_tn(a, b):
    return lax.dot_general(a, b, (((0,), (0,)), ((), ())), preferred_element_type=F32)


def _mesh_pos():
    return lax.axis_index("x"), lax.axis_index("y"), lax.axis_index("c")


class _Comm:
    def __init__(self, ins, out_shape, sems, start, finish, aliases=None):
        self.ins, self.out_shape, self.sems, self.start, self.finish = ins, out_shape, sems, start, finish
        self.aliases = aliases or {}


def _gather_comm(shards, units):
    n_u = len(units)

    def tools(ins, dsts, sems):
        send_sems, recv_sems, local_sems = sems
        x, y, c = _mesh_pos()
        chips = [(1 - x, y), (x, 1 - y), (1 - x, 1 - y)]

        def src_of(o):
            t, l = units[o]
            return ins[t] if l is None else ins[t].at[l]

        def row(o, p):
            return dsts[o].at[4 * p[0] + 2 * p[1] + p[2]]

        def copy(o, k, block, to, own=False):
            return pltpu.make_async_remote_copy(
                src_ref=src_of(o) if own else row(o, block), dst_ref=row(o, block),
                send_sem=send_sems.at[o * 7 + k], recv_sem=recv_sems.at[o * 7 + k],
                device_id=to, device_id_type=pl.DeviceIdType.MESH)

        def local(o):
            return pltpu.make_async_copy(src_of(o), row(o, (x, y, c)), local_sems.at[o])

        def first(o):
            return [copy(o, 1 + j, (x, y, c), (*chip, c), own=True) for j, chip in enumerate(chips)] + [
                copy(o, 0, (x, y, c), (x, y, 1 - c), own=True)]

        return (x, y, c), chips, copy, local, first

    def start(ins, dsts, sems):
        _, _, _, local, first = tools(ins, dsts, sems)
        for o in range(n_u):
            local(o).start()
            for cp in first(o):
                cp.start()

    def finish(ins, dsts, sems):
        (x, y, c), chips, copy, local, first = tools(ins, dsts, sems)
        me, sibling = (x, y, c), (x, y, 1 - c)
        passed = []
        for o in range(n_u):
            for j, chip in enumerate(chips):
                copy(o, 1 + j, (*chip, c), me).wait_recv()
                cp = copy(o, 4 + j, (*chip, c), sibling)
                cp.start()
                passed.append(cp)
        for o in range(n_u):
            copy(o, 0, sibling, me).wait_recv()
            for j, chip in enumerate(chips):
                copy(o, 4 + j, (*chip, 1 - c), me).wait_recv()
        for o in range(n_u):
            for cp in first(o):
                cp.wait_send()
        for cp in passed:
            cp.wait_send()
        for o in range(n_u):
            local(o).wait()

    out_shape = []
    for t, l in units:
        shp = shards[t].shape if l is None else shards[t].shape[1:]
        out_shape.append(SDS((N_DEV,) + tuple(shp), shards[t].dtype))
    sems = [pltpu.SemaphoreType.DMA((7 * n_u,)), pltpu.SemaphoreType.DMA((7 * n_u,)), pltpu.SemaphoreType.DMA((n_u,))]
    return _Comm(list(shards), out_shape, sems, start, finish)


PEERS_ALL = (1, 2, 3, 4, 5, 6, 7)
PEERS_SAME_CORE = (1, 2, 4, 6)
PEERS_OTHER_CORE = (3, 5, 7)
PEERS_BUT_NEAR_OTHER = (1, 2, 4, 6, 7)
PEERS_NEAR_OTHER = (3, 5)


def _scatter_comm(parts, peers=PEERS_ALL, into=None):
    n_u = len(parts)

    def tools(ins, dsts, sems):
        send_sems, recv_sems, local_sems = sems
        x, y, c = _mesh_pos()
        me = 4 * x + 2 * y + c

        def peer(k):
            return ((x + ((k >> 2) & 1)) % 2, (y + ((k >> 1) & 1)) % 2, (c + (k & 1)) % 2)

        def copy(u, k, wait=False):
            p = peer(k)
            pi = 4 * p[0] + 2 * p[1] + p[2]
            return pltpu.make_async_remote_copy(
                src_ref=ins[u].at[pi], dst_ref=dsts[u].at[pi if wait else me],
                send_sem=send_sems.at[u * 7 + k - 1], recv_sem=recv_sems.at[u * 7 + k - 1],
                device_id=p, device_id_type=pl.DeviceIdType.MESH)

        def local(u):
            return pltpu.make_async_copy(ins[u].at[me], dsts[u].at[me], local_sems.at[u])

        return copy, local

    def start(ins, dsts, sems):
        copy, local = tools(ins, dsts, sems)
        for u in range(n_u):
            if into is None:
                local(u).start()
            for k in peers:
                copy(u, k).start()

    def finish(ins, dsts, sems):
        copy, local = tools(ins, dsts, sems)
        for u in range(n_u):
            for k in peers:
                copy(u, k, wait=True).wait()
            if into is None:
                local(u).wait()

    sems = [pltpu.SemaphoreType.DMA((7 * n_u,)), pltpu.SemaphoreType.DMA((7 * n_u,)), pltpu.SemaphoreType.DMA((n_u,))]
    aliases = {} if into is None else {n_u + u: u for u in range(n_u)}
    return _Comm(list(parts) + list(into or []), [SDS(p.shape, p.dtype) for p in parts], sems, start, finish, aliases)


def _run_comm(comm, name):
    n_i, n_o = len(comm.ins), len(comm.out_shape)

    def body(*refs):
        comm.start(refs[:n_i], refs[n_i:n_i + n_o], refs[n_i + n_o:])
        comm.finish(refs[:n_i], refs[n_i:n_i + n_o], refs[n_i + n_o:])

    return pl.pallas_call(
        body, name=name, out_shape=comm.out_shape, in_specs=[ANY] * n_i, out_specs=[ANY] * n_o,
        scratch_shapes=comm.sems,
    )(*comm.ins)


def _pcall(body, name, grid, in_specs, out_specs, out_shape, scratch, sem, args, comm=None, aliases=None):
    n_i, n_o, n_s = len(in_specs), len(out_specs), len(scratch)
    aliases = aliases or {}
    if comm is None:
        res = pl.pallas_call(
            body, name=name, grid=grid, in_specs=in_specs, out_specs=out_specs, out_shape=out_shape,
            scratch_shapes=scratch, input_output_aliases=aliases,
            compiler_params=pltpu.CompilerParams(dimension_semantics=sem, vmem_limit_bytes=V7X_VMEM_LIMIT),
        )(*args)
        return res, []
    n_ci, n_co = len(comm.ins), len(comm.out_shape)

    def wrapped(*refs):
        ins = refs[:n_i]
        cins = refs[n_i:n_i + n_ci]
        outs = refs[n_i + n_ci:n_i + n_ci + n_o]
        couts = refs[n_i + n_ci + n_o:n_i + n_ci + n_o + n_co]
        rest = refs[n_i + n_ci + n_o + n_co:]
        ids = [pl.program_id(d) for d in range(len(grid))]
        first = functools.reduce(jnp.logical_and, [i == 0 for i in ids])
        last = functools.reduce(jnp.logical_and, [i == g - 1 for i, g in zip(ids, grid)])

        @pl.when(first)
        def _():
            comm.start(cins, couts, rest[n_s:])

        body(*ins, *outs, *rest[:n_s])

        @pl.when(last)
        def _():
            comm.finish(cins, couts, rest[n_s:])

    res = pl.pallas_call(
        wrapped, name=name, grid=grid, in_specs=list(in_specs) + [ANY] * n_ci,
        out_specs=list(out_specs) + [ANY] * n_co, out_shape=list(out_shape) + list(comm.out_shape),
        scratch_shapes=list(scratch) + list(comm.sems),
        input_output_aliases={**aliases, **{n_i + ci: n_o + co for ci, co in comm.aliases.items()}},
        compiler_params=pltpu.CompilerParams(dimension_semantics=("arbitrary",) * len(grid),
                                             vmem_limit_bytes=V7X_VMEM_LIMIT),
    )(*args, *comm.ins)
    return res[:n_o], res[n_o:]


def _rmsnorm_fwd(x, g, name):
    S, D = x.shape
    tm = _tile(S, 512, 16)

    def body(x_ref, g_ref, h_ref):
        h_ref[...] = _rmsnorm_rows(x_ref[...], g_ref[...])

    (h,), _ = _pcall(body, name, (S // tm,), [BS((tm, D), lambda i: (i, 0)), BS((1, D), lambda i: (0, 0))],
                     [BS((tm, D), lambda i: (i, 0))], [SDS((S, D), BF16)], [], ("parallel",), (x, g))
    return h


def _matmul_fwd(a, w, name, comm=None, w_t=False, out_dtype=F32):
    S, K = a.shape
    C = w.shape[0]
    Fc = w.shape[1] if w_t else w.shape[2]
    tn = _tile(Fc, 1408)
    tm = _tile(S, 1024, 16)

    def body(a_ref, w_ref, o_ref):
        p = _dot_nt(a_ref[...], w_ref[0]) if w_t else _dot(a_ref[...], w_ref[0])
        o_ref[0] = p.astype(out_dtype)

    w_spec = BS((1, tn, K), lambda c, n, i: (c, n, 0)) if w_t else BS((1, K, tn), lambda c, n, i: (c, 0, n))
    (o,), extra = _pcall(
        body, name, (C, Fc // tn, S // tm), [BS((tm, K), lambda c, n, i: (i, 0)), w_spec],
        [BS((1, tm, tn), lambda c, n, i: (c, i, n))], [SDS((C, S, Fc), out_dtype)], [],
        ("parallel", "parallel", "parallel"), (a, w), comm)
    return o, extra


def _rmsnorm_rows(xv, g):
    r = lax.rsqrt(jnp.mean(xv * xv, axis=-1, keepdims=True) + EPS)
    return (xv * r * g).astype(BF16)


def _swiglu_down(ab, w2, x, g_next, name, comm=None):
    _, S, F = ab.shape
    D = w2.shape[1]
    tk = _tile(F, 1408)
    tm = _tile(S, 512, 16)
    nk = F // tk

    def body(ab_ref, w_ref, x_ref, g_ref, o_ref, h_ref):
        k = pl.program_id(1)
        a = ab_ref[0].astype(F32)
        s = a * _sig(a) * ab_ref[1].astype(F32)
        p = 0.5 * _dot(s.astype(BF16), w_ref[...])

        @pl.when(k == 0)
        def _():
            o_ref[...] = x_ref[...] + p

        @pl.when(k > 0)
        def _():
            o_ref[...] += p

        @pl.when(k == nk - 1)
        def _():
            h_ref[...] = _rmsnorm_rows(o_ref[...], g_ref[...])

    (o, h), extra = _pcall(
        body, name, (S // tm, nk),
        [BS((2, tm, tk), lambda i, k: (0, i, k)), BS((tk, D), lambda i, k: (k, 0)), BS((tm, D), lambda i, k: (i, 0)),
         BS((1, D), lambda i, k: (0, 0))],
        [BS((tm, D), lambda i, k: (i, 0)), BS((tm, D), lambda i, k: (i, 0))],
        [SDS((S, D), F32), SDS((S, D), BF16)], [], ("parallel", "arbitrary"), (ab, w2, x, g_next), comm)
    return o, h, extra


def _ffn_bwd_hidden(dy, w2, ab, name, comm=None):
    S, D = dy.shape
    F = w2.shape[0]
    tk = _tile(F, 1408)
    tm = _tile(S, 512, 16)
    te = _tile(tm, 128, 16)

    def body(dy_ref, w_ref, ab_ref, dab_ref, s_ref, ds_ref):
        ds_ref[...] = 0.5 * _dot_nt(dy_ref[...].astype(BF16), w_ref[...])
        for r0 in range(0, tm, te):
            rows = slice(r0, r0 + te)
            ds = ds_ref[rows, :]
            a = ab_ref[0, rows, :].astype(F32)
            b = ab_ref[1, rows, :].astype(F32)
            sg = _sig(a)
            sa = a * sg
            dab_ref[0, rows, :] = (ds * b * (sg * (1.0 + a * (1.0 - sg)))).astype(BF16)
            dab_ref[1, rows, :] = (ds * sa).astype(BF16)
            s_ref[rows, :] = (0.5 * (sa * b)).astype(BF16)

    (dab, sh), extra = _pcall(
        body, name, (F // tk, S // tm),
        [BS((tm, D), lambda k, i: (i, 0)), BS((tk, D), lambda k, i: (k, 0)), BS((2, tm, tk), lambda k, i: (0, i, k))],
        [BS((2, tm, tk), lambda k, i: (0, i, k)), BS((tm, tk), lambda k, i: (i, k))],
        [SDS((2, S, F), BF16), SDS((S, F), BF16)], [pltpu.VMEM((tm, tk), F32)], ("parallel", "parallel"),
        (dy, w2, ab), comm)
    return dab, sh, extra


def _matmul_tn(a, b, n_c, name, comm=None, b_shared=False):
    G, S, M = a.shape
    _, _, Fc = b.shape
    C = n_c
    tM = _tile(M, 1408)
    tn = _tile(Fc, 1408)
    ts = _tile(S, 1024, 16)
    n_s = S // ts

    def body(a_ref, b_ref, o_ref, acc):
        s = pl.program_id(4)
        p = _dot_tn(a_ref[0].astype(BF16), b_ref[0].astype(BF16))

        @pl.when(s == 0)
        def _():
            acc[...] = p

        @pl.when(s > 0)
        def _():
            acc[...] += p

        @pl.when(s == n_s - 1)
        def _():
            o_ref[0] = acc[...].astype(BF16)

    (o,), extra = _pcall(
        body, name, (G, M // tM, C, Fc // tn, n_s),
        [BS((1, ts, tM), lambda g, m, c, n, s: (g, s, m)), BS((1, ts, tn), lambda g, m, c, n, s: (c if b_shared else g * C + c, s, n))],
        [BS((1, tM, tn), lambda g, m, c, n, s: (g * C + c, m, n))], [SDS((G * C, M, Fc), BF16)],
        [pltpu.VMEM((tM, tn), F32)], ("parallel", "parallel", "parallel", "parallel", "arbitrary"), (a, b), comm)
    return o, extra


def _matmul_nt_normbwd(b, w, x, gam, dres, name, comm=None, w_t=False):
    C, S, Fc = b.shape
    D = w.shape[2] if w_t else w.shape[1]
    tk = _tile(Fc, 1408)
    tm = _tile(S, 1024, 16)
    te = _tile(tm, 256, 8)
    nk = Fc // tk

    def body(b_ref, w_ref, x_ref, g_ref, r_ref, dx_ref, dg_ref):
        i, c, k = pl.program_id(0), pl.program_id(1), pl.program_id(2)
        p = _dot(b_ref[0], w_ref[0]) if w_t else _dot_nt(b_ref[0], w_ref[0])
        first = jnp.logical_and(c == 0, k == 0)

        @pl.when(first)
        def _():
            dx_ref[...] = p

        @pl.when(jnp.logical_not(first))
        def _():
            dx_ref[...] += p

        @pl.when(jnp.logical_and(c == C - 1, k == nk - 1))
        def _():
            dgp = None
            for r0 in range(0, tm, te):
                rows = slice(r0, r0 + te)
                xv = x_ref[rows, :]
                r = lax.rsqrt(jnp.mean(xv * xv, axis=-1, keepdims=True) + EPS)
                xn = xv * r
                dh = dx_ref[rows, :]
                dxn = dh * g_ref[...]
                dx_ref[rows, :] = r_ref[rows, :] + r * (dxn - xn * jnp.mean(dxn * xn, axis=-1, keepdims=True))
                t = jnp.sum(dh * xn, axis=0, keepdims=True)
                dgp = t if dgp is None else dgp + t

            @pl.when(i == 0)
            def _():
                dg_ref[...] = dgp

            @pl.when(i > 0)
            def _():
                dg_ref[...] += dgp

    once = dict(pipeline_mode=pl.Buffered(1))
    (dx, dg), extra = _pcall(
        body, name, (S // tm, C, nk),
        [BS((1, tm, tk), lambda i, c, k: (c, i, k)),
         BS((1, tk, D), lambda i, c, k: (c, k, 0)) if w_t else BS((1, D, tk), lambda i, c, k: (c, 0, k)),
         BS((tm, D), lambda i, c, k: (i, 0), **once), BS((1, D), lambda i, c, k: (0, 0)),
         BS((tm, D), lambda i, c, k: (i, 0), **once)],
        [BS((tm, D), lambda i, c, k: (i, 0)), BS((1, D), lambda i, c, k: (0, 0))],
        [SDS((S, D), F32), SDS((1, D), F32)], [],
        ("arbitrary", "arbitrary", "arbitrary"), (b, w, x, gam, dres), comm)
    return dx, dg, extra


def _final_loss(x, gam, target, name):
    S, D = x.shape
    tm = _tile(S, 512, 8)

    def body(x_ref, g_ref, t_ref, loss_ref, dx_ref, dg_ref):
        i = pl.program_id(0)
        xv = x_ref[...]
        r = lax.rsqrt(jnp.mean(xv * xv, axis=-1, keepdims=True) + EPS)
        xn = xv * r
        err = xn * g_ref[...] - t_ref[...]
        part = 0.5 * jnp.sum(jnp.mean(err * err, axis=-1, keepdims=True), axis=0, keepdims=True)
        dy = err * (1.0 / D)
        dxn = dy * g_ref[...]
        dx_ref[...] = r * (dxn - xn * jnp.mean(dxn * xn, axis=-1, keepdims=True))
        dgp = jnp.sum(dy * xn, axis=0, keepdims=True)
        lp = jnp.broadcast_to(part, loss_ref.shape)

        @pl.when(i == 0)
        def _():
            dg_ref[...] = dgp
            loss_ref[...] = lp

        @pl.when(i > 0)
        def _():
            dg_ref[...] += dgp
            loss_ref[...] += lp

    res, _ = _pcall(
        body, name, (S // tm,),
        [BS((tm, D), lambda i: (i, 0)), BS((1, D), lambda i: (0, 0)), BS((tm, D), lambda i: (i, 0))],
        [BS((8, 128), lambda i: (0, 0)), BS((tm, D), lambda i: (i, 0)), BS((1, D), lambda i: (0, 0))],
        [SDS((8, 128), F32), SDS((S, D), F32), SDS((1, D), F32)], [], ("arbitrary",), (x, gam, target))
    return res


CONV_CHUNK = 32


def _fill_shifted(rot, n):
    for b in range(1, 8):
        rot[b, 0:n - 8, :] = rot[0, b:b + n - 8, :]


def _window(rot, off, r0, rows):
    b = off % 8
    return rot[b, off - b + r0:off - b + r0 + rows, :]


def _taps(rot, w_ref, offs, n_rows, out):
    for r0 in range(0, n_rows, CONV_CHUNK):
        acc = None
        for k, off in enumerate(offs):
            t = w_ref[k:k + 1, :] * _window(rot, off, r0, CONV_CHUNK)
            acc = t if acc is None else acc + t
        out[r0:r0 + CONV_CHUNK, :] = acc


def _tap_grads(rot, offs, g_plane, n_rows, dw_ref):
    for k, off in enumerate(offs):
        acc = None
        for r0 in range(0, n_rows, CONV_CHUNK):
            p = g_plane[0, r0:r0 + CONV_CHUNK, :] * _window(rot, off, r0, CONV_CHUNK)
            acc = p if acc is None else acc + p
        dw_ref[k:k + 1, :] += jnp.sum(acc, axis=0, keepdims=True)


def _sgu_masks():
    ii = lax.broadcasted_iota(jnp.int32, (SGU_BLOCK, SGU_BLOCK), 0) // SGU_CHUNK
    jj = lax.broadcasted_iota(jnp.int32, (SGU_BLOCK, SGU_BLOCK), 1) // SGU_CHUNK
    return jj <= ii, ii <= jj


def _mixers_fwd(proj, pool_w, pool_scale, sconv_w, cconv_w, cln_g, cln_b, sln_g, sln_b, sgu_w, sgu_bias, name,
                comm=None):
    _, S, D = proj.shape
    BW = D // 2
    GW = BW // 4
    TS = _tile(S, 256, SGU_BLOCK)
    H = HALO
    hb = TS // H

    def main(blk, col):
        return BS((1, TS, BW), lambda i: (blk, i, col))

    def back(blk, col):
        return BS((1, H, BW), lambda i: (blk, jnp.maximum(i * hb - 1, 0), col))

    def full(a):
        nd = a.ndim
        return BS(a.shape, lambda i: (0,) * nd)

    def body(pa_m, pa_b, xi_m, xi_b, bg_m, cg_m, cg_b, ca_m, ca_b, cb_m, cb_b, du_m, dv_m,
             pw, ps, sw, cw, clg, clb, slg, slb, gw, gbias, y_ref, e1, e2, e3):
        i = pl.program_id(0)
        nb = jnp.where(i > 0, 1.0, 0.0).astype(F32)
        rows = i * TS + lax.broadcasted_iota(jnp.int32, (TS, 1), 0)

        e1[0:H, :] = pa_b[0] * nb
        e1[H:H + TS, :] = pa_m[0]
        for g in range(4):
            cols = slice(g * GW, (g + 1) * GW)
            win = 2 << g
            wsum = e1[H:H + TS, cols]
            for k in range(1, win):
                wsum = wsum + e1[H - k:H - k + TS, cols]
            cnt = jnp.minimum(rows + 1, win).astype(F32)
            d = wsum / cnt - e1[H:H + TS, cols]
            yg = _dot(d.astype(BF16), pw[g].astype(BF16)) * ps[:, cols]
            y_ref[0, :, cols] = yg.astype(BF16)

        e2[0:H, :] = cg_b[0] * xi_b[0] * nb
        e2[H:H + TS, :] = cg_m[0] * xi_m[0]
        cz = sw[0:1, :] * e2[H - 2:H - 2 + TS, :]
        for k in range(1, SCONV_K):
            cz = cz + sw[k:k + 1, :] * e2[H - 2 + k:H - 2 + k + TS, :]
        y_ref[1] = (bg_m[0] * cz).astype(BF16)

        e3[0, 0:H, :] = ca_b[0] * _sig(cb_b[0]) * nb
        e3[0, H:H + TS, :] = ca_m[0] * _sig(cb_m[0])
        _fill_shifted(e3, H + TS)
        _taps(e3, cw, [H - (CCONV_K - 1) + k for k in range(CCONV_K)], TS, e1)
        yh, _ = _ln_stats(e1[0:TS, :])
        y2 = yh * clg[...] + clb[...]
        y_ref[2] = (y2 * _sig(y2)).astype(BF16)

        u, _ = _gelu(du_m[0])
        v, _ = _gelu(dv_m[0])
        vh, _ = _ln_stats(v)
        vn = vh * slg[...] + slb[...]
        mask, _ = _sgu_masks()
        for h in range(4):
            wm = jnp.where(mask, gw[h], 0.0).astype(BF16)
            cs = slice(h * GW, (h + 1) * GW)
            for n in range(TS // SGU_BLOCK):
                rs = slice(n * SGU_BLOCK, (n + 1) * SGU_BLOCK)
                z = _dot(wm, vn[rs, cs].astype(BF16)) + gbias[h]
                y_ref[3, rs, cs] = (u[rs, cs] * z).astype(BF16)

    args = [proj] * 13 + [pool_w, pool_scale, sconv_w, cconv_w, cln_g, cln_b, sln_g, sln_b, sgu_w, sgu_bias]
    in_specs = [main(0, 0), back(0, 0), main(0, 1), back(0, 1), main(1, 0), main(1, 1), back(1, 1),
                main(2, 0), back(2, 0), main(2, 1), back(2, 1), main(3, 0), main(3, 1)]
    in_specs += [full(a) for a in args[13:]]
    (y,), extra = _pcall(body, name, (S // TS,), in_specs, [BS((4, TS, BW), lambda i: (0, i, 0))],
                         [SDS((4, S, BW), BF16)],
                         [pltpu.VMEM((H + TS, BW), F32)] * 2 + [pltpu.VMEM((8, H + TS, BW), F32)], ("parallel",),
                         args, comm)
    return y, extra


def _mixers_bwd(proj, dy, dproj_gates, pool_w, pool_wt, pool_scale, sconv_w, cconv_w, cln_g, cln_b, sln_g, sln_b,
                sgu_w, sgu_wt, sgu_bias, name, comm=None):
    _, S, D = proj.shape
    BW = D // 2
    GW = BW // 4
    TS = _tile(S, 256, SGU_BLOCK)
    H = HALO
    hb = TS // H
    n_t = S // TS
    E = TS + H

    def main(blk, col):
        return BS((1, TS, BW), lambda i: (blk, i, col))

    def back(blk, col):
        return BS((1, H, BW), lambda i: (blk, jnp.maximum(i * hb - 1, 0), col))

    def front(blk, col):
        return BS((1, H, BW), lambda i: (blk, jnp.minimum((i + 1) * hb, S // H - 1), col))

    def full(a):
        nd = a.ndim
        return BS(a.shape, lambda i: (0,) * nd)

    def body(pa_b, pa_m, xi_b, xi_m, bg_m, bg_f, cg_b, cg_m, ca_b, ca_m, ca_f, cb_b, cb_m, cb_f, du_m, dv_m,
             dya_m, dya_f, dyb_m, dyb_f, dyc_m, dyc_f, dyd_m,
             pw, pwt, ps, sw, cw, clg, clb, slg, slb, gw, gwt, gbias, _gates_in,
             dp_ref, dpw, dps, dsw, dcw, dclg, dclb, dslg, dslb, dgw, dgb,
             e1, e2, e3, e4, e5, ra, rb):
        i = pl.program_id(0)
        nb = jnp.where(i > 0, 1.0, 0.0).astype(F32)
        nf = jnp.where(i < n_t - 1, 1.0, 0.0).astype(F32)
        rows_m = i * TS + lax.broadcasted_iota(jnp.int32, (TS, 1), 0)
        rows_e = i * TS + lax.broadcasted_iota(jnp.int32, (E, 1), 0)

        @pl.when(i == 0)
        def _():
            for r in (dpw, dps, dsw, dcw, dclg, dclb, dslg, dslb, dgw, dgb):
                r[...] = jnp.zeros(r.shape, F32)

        e1[0:H, :] = pa_b[0] * nb
        e1[H:H + TS, :] = pa_m[0]
        e2[0:TS, :] = dya_m[0] * ps[...]
        e2[TS:E, :] = dya_f[0] * ps[...] * nf
        for g in range(4):
            cols = slice(g * GW, (g + 1) * GW)
            win = 2 << g
            a_m = e1[H:H + TS, cols]
            wsum = a_m
            for k in range(1, win):
                wsum = wsum + e1[H - k:H - k + TS, cols]
            d = wsum / jnp.minimum(rows_m + 1, win).astype(F32) - a_m
            d16 = d.astype(BF16)
            dyp = e2[0:E, cols].astype(BF16)
            dd = _dot(dyp, pwt[g].astype(BF16))
            e3[0:E, cols] = dd / jnp.minimum(rows_e + 1, win).astype(F32)
            da = e3[0:TS, cols] - dd[0:TS]
            for k in range(1, win):
                da = da + e3[k:k + TS, cols]
            dp_ref[0, :, cols] = da.astype(BF16)
            ypre = _dot(d16, pw[g].astype(BF16))
            dps[:, cols] += jnp.sum(dya_m[0][:, cols] * ypre, axis=0, keepdims=True)
            dpw[g] += _dot(jnp.transpose(d).astype(BF16), dyp[0:TS])

        e4[0:H, :] = cg_b[0] * xi_b[0] * nb
        e4[H:H + TS, :] = cg_m[0] * xi_m[0]
        dyb = dyb_m[0]
        e5[0:TS, :] = dyb * bg_m[0]
        e5[TS:E, :] = dyb_f[0] * bg_f[0] * nf
        dcz = e5[0:TS, :]
        cz = None
        dz = None
        for k in range(SCONV_K):
            zk = e4[H - 2 + k:H - 2 + k + TS, :]
            wk = sw[k:k + 1, :]
            cz = wk * zk if cz is None else cz + wk * zk
            t = wk * e5[2 - k:2 - k + TS, :]
            dz = t if dz is None else dz + t
            dsw[k:k + 1, :] += jnp.sum(dcz * zk, axis=0, keepdims=True)
        dp_ref[0, :, BW:2 * BW] = (dz * cg_m[0]).astype(BF16)
        dp_ref[1, :, 0:BW] = (dyb * cz).astype(BF16)
        dp_ref[1, :, BW:2 * BW] = (dz * xi_m[0]).astype(BF16)

        sgm = _sig(cb_m[0])
        ra[0, 0:H, :] = ca_b[0] * _sig(cb_b[0]) * nb
        ra[0, H:H + TS, :] = ca_m[0] * sgm
        ra[0, H + TS:H + E, :] = ca_f[0] * _sig(cb_f[0]) * nf
        _fill_shifted(ra, H + E)
        fwd_offs = [H - (CCONV_K - 1) + k for k in range(CCONV_K)]
        _taps(ra, cw, fwd_offs, E, e4)
        yh, rstd = _ln_stats(e4[0:E, :])
        y2 = yh * clg[...] + clb[...]
        s2 = _sig(y2)
        e1[0:TS, :] = dyc_m[0]
        e1[TS:E, :] = dyc_f[0] * nf
        dy2 = e1[0:E, :] * (s2 * (1.0 + y2 * (1.0 - s2)))
        dclg[...] += jnp.sum((dy2 * yh)[0:TS], axis=0, keepdims=True)
        dclb[...] += jnp.sum(dy2[0:TS], axis=0, keepdims=True)
        rb[0, 0:E, :] = _ln_bwd(dy2 * clg[...], yh, rstd)
        _fill_shifted(rb, E)
        _taps(rb, cw, [CCONV_K - 1 - k for k in range(CCONV_K)], TS, e5)
        _tap_grads(ra, fwd_offs, rb, TS, dcw)
        dy0 = e5[0:TS, :]
        dp_ref[2, :, 0:BW] = (dy0 * sgm).astype(BF16)
        dp_ref[2, :, BW:2 * BW] = (dy0 * ca_m[0] * (sgm * (1.0 - sgm))).astype(BF16)

        pu = du_m[0]
        pv = dv_m[0]
        u, tu = _gelu(pu)
        v, tv = _gelu(pv)
        vh, vr = _ln_stats(v)
        vn = vh * slg[...] + slb[...]
        dyd = dyd_m[0]
        mask, mask_t = _sgu_masks()
        for h in range(4):
            wm = jnp.where(mask, gw[h], 0.0).astype(BF16)
            wmt = jnp.where(mask_t, gwt[h], 0.0).astype(BF16)
            cs = slice(h * GW, (h + 1) * GW)
            for n in range(TS // SGU_BLOCK):
                rs = slice(n * SGU_BLOCK, (n + 1) * SGU_BLOCK)
                vb = vn[rs, cs].astype(BF16)
                z = _dot(wm, vb) + gbias[h]
                dzb = dyd[rs, cs] * u[rs, cs]
                dz16 = dzb.astype(BF16)
                e3[rs, cs] = dyd[rs, cs] * z
                e4[rs, cs] = _dot(wmt, dz16)
                dgw[h] += jnp.where(mask, _dot_nt(dz16, vb), 0.0)
                dgb[h] += dzb
        dvn = e4[0:TS, :]
        dslg[...] += jnp.sum(dvn * vh, axis=0, keepdims=True)
        dslb[...] += jnp.sum(dvn, axis=0, keepdims=True)
        dv = _ln_bwd(dvn * slg[...], vh, vr)
        dp_ref[3, :, 0:BW] = (e3[0:TS, :] * _gelu_grad(pu, tu)).astype(BF16)
        dp_ref[3, :, BW:2 * BW] = (dv * _gelu_grad(pv, tv)).astype(BF16)

        @pl.when(i == n_t - 1)
        def _():
            for h in range(4):
                dgb[h] = jnp.broadcast_to(jnp.sum(dgb[h], axis=1, keepdims=True), dgb.shape[1:])

    params = [pool_w, pool_wt, pool_scale, sconv_w, cconv_w, cln_g, cln_b, sln_g, sln_b, sgu_w, sgu_wt, sgu_bias]
    args = [proj] * 16 + [dy] * 7 + params + [dproj_gates]
    in_specs = [back(0, 0), main(0, 0), back(0, 1), main(0, 1), main(1, 0), front(1, 0), back(1, 1), main(1, 1),
                back(2, 0), main(2, 0), front(2, 0), back(2, 1), main(2, 1), front(2, 1), main(3, 0), main(3, 1),
                main(0, 0), front(0, 0), main(1, 0), front(1, 0), main(2, 0), front(2, 0), main(3, 0)]
    in_specs += [full(a) for a in params] + [ANY]
    small = [SDS(pool_w.shape, F32), SDS(pool_scale.shape, F32), SDS(sconv_w.shape, F32), SDS(cconv_w.shape, F32),
             SDS(cln_g.shape, F32), SDS(cln_b.shape, F32), SDS(sln_g.shape, F32), SDS(sln_b.shape, F32),
             SDS(sgu_w.shape, F32), SDS(sgu_bias.shape, F32)]
    out_specs = [BS((4, TS, D), lambda i: (0, i, 0))] + [full(s) for s in small]
    return _pcall(body, name, (n_t,), in_specs, out_specs, [SDS(dproj_gates.shape, BF16)] + small,
                  [pltpu.VMEM((TS + 2 * H, BW), F32)] * 5 + [pltpu.VMEM((8, TS + 2 * H, BW), F32)] * 2,
                  ("arbitrary",), args, comm, aliases={len(args) - 1: 0})


def _merge_fwd(y, proj, w_up, w_out, x, g_next, name, comm=None):
    _, S, BW = y.shape
    D = x.shape[1]
    tm = _tile(S, 256, 16)

    def body(y_ref, pg_ref, wu_ref, wo_ref, x_ref, g_ref, o_ref, m_ref, h_ref):
        merged = None
        for g in range(4):
            t = _sig(pg_ref[g]) * _dot(y_ref[g], wu_ref[g])
            merged = t if merged is None else merged + t
        m16 = merged.astype(BF16)
        m_ref[...] = m16
        xn = x_ref[...] + _dot(m16, wo_ref[...])
        o_ref[...] = xn
        h_ref[...] = _rmsnorm_rows(xn, g_ref[...])

    (o, m, h), extra = _pcall(
        body, name, (S // tm,),
        [BS((4, tm, BW), lambda i: (0, i, 0)), BS((4, tm, D), lambda i: (1, i, 0)),
         BS((4, BW, D), lambda i: (0, 0, 0)), BS((D, D), lambda i: (0, 0)), BS((tm, D), lambda i: (i, 0)),
         BS((1, D), lambda i: (0, 0))],
        [BS((tm, D), lambda i: (i, 0)), BS((tm, D), lambda i: (i, 0)), BS((tm, D), lambda i: (i, 0))],
        [SDS((S, D), F32), SDS((S, D), BF16), SDS((S, D), BF16)], [], ("parallel",),
        (y, proj, w_up, w_out, x, g_next), comm)
    return o, m, h, extra


def _merge_bwd(dx, y, proj, w_up, w_out, name, comm=None):
    _, S, BW = y.shape
    D = dx.shape[1]
    tm = _tile(S, 256, 16)

    def body(dx_ref, y_ref, pg_ref, wu_ref, wo_ref, dup_ref, dp_ref, dy_ref):
        dm = _dot_nt(dx_ref[...].astype(BF16), wo_ref[...])
        for g in range(4):
            gate = _sig(pg_ref[g])
            up = _dot(y_ref[g], wu_ref[g])
            dup = (dm * gate).astype(BF16)
            dup_ref[g] = dup
            dp_ref[g] = (dm * up * (gate * (1.0 - gate))).astype(BF16)
            dy_ref[g] = _dot_nt(dup, wu_ref[g])

    res, extra = _pcall(
        body, name, (S // tm,),
        [BS((tm, D), lambda i: (i, 0)), BS((4, tm, BW), lambda i: (0, i, 0)), BS((4, tm, D), lambda i: (1, i, 0)),
         BS((4, BW, D), lambda i: (0, 0, 0)), BS((D, D), lambda i: (0, 0))],
        [BS((4, tm, D), lambda i: (0, i, 0)), BS((4, tm, D), lambda i: (1, i, 0)), BS((4, tm, BW), lambda i: (0, i, 0))],
        [SDS((4, S, D), BF16), SDS((8, S, D), BF16), SDS((4, S, BW), F32)], [], ("parallel",),
        (dx, y, proj, w_up, w_out), comm)
    return res, extra


def _adamw(w, g, m, v):
    m = ADAM_B1 * m + (1.0 - ADAM_B1) * g
    v = ADAM_B2 * v + (1.0 - ADAM_B2) * (g * g)
    m_hat = m / (1.0 - ADAM_B1 ** ADAM_STEP)
    v_hat = v / (1.0 - ADAM_B2 ** ADAM_STEP)
    delta = -ADAM_LR * (m_hat / (jnp.sqrt(v_hat) + ADAM_EPS) + ADAM_WD * w)
    return delta, m, v


def _adamw_sharded(parts, w, m, v, name, comm=None):
    L, R, C = w.shape
    tr = _tile(R, 256, 16)

    def body(*refs):
        p_refs = refs[:L]
        w_ref, m_ref, v_ref, g_out, d_out, m_out, v_out = refs[L:]
        l = pl.program_id(0)
        g = None
        for d in range(N_DEV):
            t = p_refs[0][d].astype(F32)
            for j in range(1, L):
                t = jnp.where(l == j, p_refs[j][d].astype(F32), t)
            g = t if g is None else g + t
        dl, mn, vn = _adamw(w_ref[0], g, m_ref[0], v_ref[0])
        g_out[0] = g
        d_out[0] = dl
        m_out[0] = mn
        v_out[0] = vn

    def part_spec(j):
        return BS((N_DEV, tr, C), lambda l, r: (0, jnp.where(l == j, r, 0), 0))

    blk = BS((1, tr, C), lambda l, r: (l, r, 0))
    return _pcall(body, name, (L, R // tr), [part_spec(j) for j in range(L)] + [blk, blk, blk], [blk] * 4,
                  [SDS((L, R, C), F32)] * 4, [], ("parallel", "parallel"), (*parts, w, m, v), comm)


def _adamw_replicated(gathered, layout, wmv, name):
    n_b = len(gathered)
    n_p = len(layout)

    def body(*refs):
        bufs = refs[:n_b]
        prm = refs[n_b:n_b + 3 * n_p]
        outs = refs[n_b + 3 * n_p:n_b + 7 * n_p]
        sums = refs[n_b + 7 * n_p:]
        for b in range(n_b):
            s = bufs[b][0]
            for d in range(1, N_DEV):
                s = s + bufs[b][d]
            sums[b][...] = s
        for p, (b, r0, nr) in enumerate(layout):
            g = sums[b][r0:r0 + nr, :]
            d, mn, vn = _adamw(prm[3 * p][...], g, prm[3 * p + 1][...], prm[3 * p + 2][...])
            outs[4 * p][...] = g
            outs[4 * p + 1][...] = d
            outs[4 * p + 2][...] = mn
            outs[4 * p + 3][...] = vn

    flat = [a for t in wmv for a in t]
    out_shape = []
    for (w, _, _) in wmv:
        out_shape += [SDS(w.shape, F32)] * 4
    out_shape += [SDS(g.shape[1:], F32) for g in gathered]
    return pl.pallas_call(
        body, name=name, out_shape=out_shape,
        compiler_params=pltpu.CompilerParams(vmem_limit_bytes=V7X_VMEM_LIMIT),
    )(*gathered, *flat)


def _adamw_small(g, w, m, v, name):
    def body(g_ref, w_ref, m_ref, v_ref, d_out, m_out, v_out):
        d, mn, vn = _adamw(w_ref[...], g_ref[...], m_ref[...], v_ref[...])
        d_out[...] = d
        m_out[...] = mn
        v_out[...] = vn

    return pl.pallas_call(body, name=name, out_shape=[SDS(w.shape, F32)] * 3)(g, w, m, v)


def _pad_rows(a, rows):
    return jnp.pad(a, ((0, rows - a.shape[0]), (0, 0)))


def kernel(x, ffn1_norm, ffn1_w13, ffn1_w2, mix_norm, w_in, pool_w, pool_scale, sconv_w, cconv_w, cconv_ln_g, cconv_ln_b, sgu_ln_g, sgu_ln_b, sgu_w, sgu_b, w_up, w_out, ffn2_norm, ffn2_w13, ffn2_w2, final_norm, loss_target, m_ffn1_norm, m_ffn1_w13, m_ffn1_w2, m_mix_norm, m_w_in, m_pool_w, m_pool_scale, m_sconv_w, m_cconv_w, m_cconv_ln_g, m_cconv_ln_b, m_sgu_ln_g, m_sgu_ln_b, m_sgu_w, m_sgu_b, m_w_up, m_w_out, m_ffn2_norm, m_ffn2_w13, m_ffn2_w2, m_final_norm, v_ffn1_norm, v_ffn1_w13, v_ffn1_w2, v_mix_norm, v_w_in, v_pool_w, v_pool_scale, v_sconv_w, v_cconv_w, v_cconv_ln_g, v_cconv_ln_b, v_sgu_ln_g, v_sgu_ln_b, v_sgu_w, v_sgu_b, v_w_up, v_w_out, v_ffn2_norm, v_ffn2_w13, v_ffn2_w2, v_final_norm):
    P = dict(locals())
    L = ffn1_norm.shape[0]
    S, D = x.shape[1], x.shape[2]
    BW = D // 2
    GW = BW // 4
    F = ffn1_w2.shape[1] * N_DEV
    fs = ffn1_w13.shape[2]
    cw = sconv_w.shape[2]
    me = 4 * lax.axis_index("x") + 2 * lax.axis_index("y") + lax.axis_index("c")

    big = ["ffn1_w13", "ffn1_w2", "w_in", "w_up", "w_out", "ffn2_w13", "ffn2_w2"]
    shards = [(jnp.swapaxes(P[n], 1, 2) if n.endswith("w13") else P[n]).astype(BF16) for n in big]
    conv_local = jnp.concatenate([sconv_w, cconv_w], axis=1)

    def gather_of(units):
        return _gather_comm(shards, [(big.index(n), l) for n, l in units])

    def ready(n, g):
        if n.endswith("w13"):
            return g.reshape(2, F, D)
        if n.endswith("w2"):
            return g.reshape(F, D)
        if n == "w_up":
            return jnp.transpose(g, (1, 2, 0, 3)).reshape(4, BW, D)
        if n == "w_out":
            return g.reshape(D, D)
        return g

    W = {}

    def take(units, arrays):
        for (n, l), g in zip(units, arrays):
            W[n, l] = ready(n, g)

    first_units = [("ffn1_w13", 0), ("ffn1_w2", 0)]
    plan = {("ffn1_up", 0): [("w_in", 0)],
            ("ffn1_down", 0): [("w_up", 0), ("w_out", 0)],
            ("proj", 0): [("ffn2_w13", 0), ("ffn2_w2", 0)],
            ("mixers", 0): [("ffn1_w13", 1)], ("merge", 0): [("ffn1_w2", 1)],
            ("ffn2_up", 0): [("w_in", 1)], ("ffn2_down", 0): [("w_up", 1), ("w_out", 1)],
            ("ffn1_up", 1): [("ffn2_w13", 1)], ("ffn1_down", 1): [("ffn2_w2", 1)]}
    assert L <= 2

    def carried(key):
        units = [u for u in plan.get(key, []) if u[1] < L]
        return units, (gather_of(units) if units else None)

    first = _gather_comm(shards + [conv_local], [(big.index(n), l) for n, l in first_units] + [(len(big), None)])
    got = _run_comm(first, "gather_first_weights")
    take(first_units, got[:2])
    conv_full = jnp.transpose(got[2], (1, 2, 0, 3)).reshape(L, SCONV_K + CCONV_K, N_DEV * cw)
    sconv_full = conv_full[:, :SCONV_K]
    cconv_full = conv_full[:, SCONV_K:]

    sgu_bias = jnp.broadcast_to(sgu_b[:, :, :, None], sgu_b.shape + (GW,))
    pool_wt = jnp.swapaxes(pool_w, 2, 3)
    sgu_wt = jnp.swapaxes(sgu_w, 2, 3)

    def row(a, l):
        return a[l][None, :]

    saved = []
    xc = x[0]
    h = _rmsnorm_fwd(xc, row(ffn1_norm, 0), "first_norm_fwd")
    for l in range(L):
        sv = {}
        for tag in ("ffn1", None, "ffn2"):
            if tag is None:
                sv["x_mix"] = xc
                units, comm = carried(("proj", l))
                proj, extra = _matmul_fwd(h, W["w_in", l], "proj_fwd", comm)
                take(units, extra)
                units, comm = carried(("mixers", l))
                y, extra = _mixers_fwd(proj, pool_w[l], row(pool_scale, l), sconv_full[l], cconv_full[l],
                                       row(cconv_ln_g, l), row(cconv_ln_b, l), row(sgu_ln_g, l), row(sgu_ln_b, l),
                                       sgu_w[l], sgu_bias[l], "mixers_fwd", comm)
                take(units, extra)
                units, comm = carried(("merge", l))
                sv.update(h_mix=h, proj=proj, y=y)
                xc, sv["merged"], h, extra = _merge_fwd(y, proj, W["w_up", l], W["w_out", l], xc, row(ffn2_norm, l),
                                                        "merge_fwd", comm)
                take(units, extra)
            else:
                sv["x_" + tag] = xc
                units, comm = carried((tag + "_up", l))
                ab, extra = _matmul_fwd(h, W[tag + "_w13", l], "ffn_up_fwd", comm, w_t=True, out_dtype=BF16)
                take(units, extra)
                units, comm = carried((tag + "_down", l))
                sv.update({"h_" + tag: h, "ab_" + tag: ab})
                if tag == "ffn1":
                    g_next = row(mix_norm, l)
                else:
                    g_next = row(ffn1_norm, l + 1) if l + 1 < L else final_norm[None, :]
                xc, h, extra = _swiglu_down(ab, W[tag + "_w2", l], xc, g_next, "ffn_down_fwd", comm)
                take(units, extra)
        saved.append(sv)

    loss_part, dx, d_final = _final_loss(xc, final_norm[None, :], loss_target[0], "loss_head")
    loss = lax.psum(loss_part[0, 0], MESH_AXES)

    R = {}
    second = []

    def rest_of_sends():
        keys = [e[0] for e in second]
        comm = None
        if second:
            assert len({e[3] for e in second}) == 1
            comm = _scatter_comm([e[1] for e in second], second[0][3], [e[2] for e in second])
        second.clear()
        return keys, comm

    def settle(keys, arrays):
        for k, a in zip(keys, arrays):
            R[k] = a

    wide = ["ffn1_norm", "mix_norm", "ffn2_norm", "final_norm"]
    half = ["pool_scale", "cconv_ln_g", "cconv_ln_b", "sgu_ln_g", "sgu_ln_b"]
    narrow = ["pool_w", "sgu_w", "sgu_b"]
    small_names = wide + half + narrow
    small_g = [dict() for _ in range(L)]
    widths = []
    for n in small_names:
        if P[n].shape[-1] not in widths:
            widths.append(P[n].shape[-1])
    layout, conv_at = {}, {}

    def pack(width):
        def stack_layers(n):
            return jnp.stack([small_g[l][n] for l in range(L)], axis=0)

        parts, r0 = [], 0
        for n in small_names:
            if P[n].shape[-1] != width:
                continue
            g = d_final if n == "final_norm" else stack_layers(n).reshape(-1, width)
            layout[n] = (widths.index(width), r0, g.shape[0])
            parts.append(_pad_rows(g, -(-g.shape[0] // 8) * 8))
            r0 += parts[-1].shape[0]
        if width == N_DEV * cw:
            conv_g = jnp.concatenate([stack_layers("sconv_w"), stack_layers("cconv_w")], axis=1)
            conv_g = conv_g.reshape(L * (SCONV_K + CCONV_K), N_DEV * cw)
            conv_at.update(b=widths.index(width), r0=r0, rows=conv_g.shape[0])
            parts.append(_pad_rows(conv_g, -(-conv_g.shape[0] // 8) * 8))
        return jnp.concatenate(parts, axis=0)

    gathered_small = [None] * len(widths)
    for l in reversed(range(L)):
        sv = saved[l]
        sg = small_g[l]
        for tag in ("ffn2", None, "ffn1"):
            if tag is None:
                keys, comm = rest_of_sends()
                (dup, dproj, dy), extra = _merge_bwd(dx, sv["y"], sv["proj"], W["w_up", l], W["w_out", l],
                                                     "merge_bwd", comm)
                settle(keys, extra)
                g_out, _ = _matmul_tn(sv["merged"][None], dx[None], 1, "w_out_grad")
                g_up, _ = _matmul_tn(sv["y"], dup, 1, "w_up_grad")
                g_out = g_out.reshape(N_DEV, D // N_DEV, D)
                g_up = jnp.transpose(g_up.reshape(4, BW, N_DEV, D // N_DEV), (2, 0, 1, 3)).reshape(
                    N_DEV, 4 * BW, D // N_DEV)
                res, (R["w_out", l], R["w_up", l]) = _mixers_bwd(
                    sv["proj"], dy, dproj, pool_w[l], pool_wt[l], row(pool_scale, l), sconv_full[l], cconv_full[l],
                    row(cconv_ln_g, l), row(cconv_ln_b, l), row(sgu_ln_g, l), row(sgu_ln_b, l), sgu_w[l], sgu_wt[l],
                    sgu_bias[l], "mixers_bwd", _scatter_comm([g_out, g_up]))
                dproj = res[0]
                (sg["pool_w"], sg["pool_scale"], sg["sconv_w"], sg["cconv_w"], sg["cconv_ln_g"], sg["cconv_ln_b"],
                 sg["sgu_ln_g"], sg["sgu_ln_b"], sg["sgu_w"], dgb) = res[1:]
                sg["sgu_b"] = dgb[:, :, 0]
                comm = None
                if l == 0:
                    early = [w for w in widths if w != D]
                    comm = _gather_comm([pack(w) for w in early], [(b, None) for b in range(len(early))])
                g_in, extra = _matmul_tn(sv["h_mix"][None], dproj, N_DEV, "w_in_grad", comm)
                if l == 0:
                    for w, g in zip(early, extra):
                        gathered_small[widths.index(w)] = g
                dx, sg["mix_norm"], (r_in,) = _matmul_nt_normbwd(
                    dproj, W["w_in", l], sv["x_mix"], row(mix_norm, l), dx, "proj_bwd",
                    _scatter_comm([g_in], PEERS_BUT_NEAR_OTHER))
                second.append((("w_in", l), g_in, r_in, PEERS_NEAR_OTHER))
            else:
                keys, comm = rest_of_sends()
                dab, sh, extra = _ffn_bwd_hidden(dx, W[tag + "_w2", l], sv["ab_" + tag], "ffn_hidden_bwd", comm)
                settle(keys, extra)
                g_w2, _ = _matmul_tn(sh[None], dx[None], 1, "ffn_w2_grad")
                g_w2 = g_w2.reshape(N_DEV, F // N_DEV, D)
                g_w13, (R[tag + "_w2", l],) = _matmul_tn(dab, sv["h_" + tag][None], 1, "ffn_w13_grad",
                                                         _scatter_comm([g_w2]), b_shared=True)
                g_w13 = g_w13.reshape(N_DEV, fs, D)
                last = tag == "ffn1" and l == 0
                now, later = (PEERS_BUT_NEAR_OTHER, PEERS_NEAR_OTHER) if last else (PEERS_SAME_CORE, PEERS_OTHER_CORE)
                dx, sg[tag + "_norm"], (r_w13,) = _matmul_nt_normbwd(
                    dab, W[tag + "_w13", l], sv["x_" + tag], row(P[tag + "_norm"], l), dx, "ffn_up_bwd",
                    _scatter_comm([g_w13], now), w_t=True)
                second.append(((tag + "_w13", l), g_w13, r_w13, later))
    grad_x = dx[None]
    out = {}

    def as2d(n, a):
        if n == "final_norm":
            return a.reshape(1, D)
        return a.reshape(-1, a.shape[-1])

    for i, n in enumerate(["ffn2_w13", "w_in", "ffn2_w2", "w_up", "w_out", "ffn1_w2", "ffn1_w13"]):
        shp = P[n].shape
        if n.endswith("w13"):
            flat, back = (lambda a: jnp.swapaxes(a, 1, 2)), (lambda a: jnp.swapaxes(a, 1, 2))
        else:
            rows, cols = math.prod(shp[1:-1]), shp[-1]
            flat, back = (lambda a: a.reshape(L, rows, cols)), (lambda a: a.reshape(shp))
        keys, comm = [], None
        if i == 0:
            keys, comm = rest_of_sends()
        elif i == 1:
            comm = _gather_comm([pack(D)], [(0, None)])
        res, extra = _adamw_sharded([R[n, l] for l in range(L)], flat(P[n]), flat(P["m_" + n]), flat(P["v_" + n]),
                                    "adamw_sharded", comm)
        if i == 0:
            settle(keys, extra)
        elif i == 1:
            gathered_small[widths.index(D)] = extra[0]
        out[n] = tuple(back(a) for a in res)

    res = _adamw_replicated(gathered_small, [layout[n] for n in small_names],
                            [(as2d(n, P[n]), as2d(n, P["m_" + n]), as2d(n, P["v_" + n])) for n in small_names],
                            "adamw_replicated")
    for p, n in enumerate(small_names):
        out[n] = tuple(a.reshape(P[n].shape) for a in res[4 * p:4 * p + 4])
    conv_sum = res[4 * len(small_names) + conv_at["b"]][conv_at["r0"]:conv_at["r0"] + conv_at["rows"]]
    conv_mine = lax.dynamic_slice_in_dim(conv_sum, me * cw, cw, axis=1)

    def conv2d(a, b):
        return jnp.concatenate([a, b], axis=1).reshape(L * (SCONV_K + CCONV_K), cw)

    cd, cm, cv = _adamw_small(conv_mine, conv2d(sconv_w, cconv_w), conv2d(m_sconv_w, m_cconv_w),
                              conv2d(v_sconv_w, v_cconv_w), "adamw_conv")
    for n, sl in (("sconv_w", slice(0, SCONV_K)), ("cconv_w", slice(SCONV_K, SCONV_K + CCONV_K))):
        out[n] = tuple(a.reshape(L, SCONV_K + CCONV_K, cw)[:, sl] for a in (conv_mine, cd, cm, cv))

    order = ["ffn1_norm", "ffn1_w13", "ffn1_w2", "mix_norm", "w_in", "pool_w", "pool_scale", "sconv_w", "cconv_w",
             "cconv_ln_g", "cconv_ln_b", "sgu_ln_g", "sgu_ln_b", "sgu_w", "sgu_b", "w_up", "w_out", "ffn2_norm",
             "ffn2_w13", "ffn2_w2", "final_norm"]
    return (loss, grad_x, *[out[n][0] for n in order], *[out[n][1] for n in order],
            *[out[n][2] for n in order], *[out[n][3] for n in order])
```

```python
import functools
import math

import jax
import jax.numpy as jnp
from jax import lax
from jax.experimental import pallas as pl
from jax.experimental.pallas import tpu as pltpu

F32 = jnp.float32
BF16 = jnp.bfloat16
EPS = 1e-6
ADAM_LR = 0.001
ADAM_B1 = 0.9
ADAM_B2 = 0.999
ADAM_EPS = 1e-08
ADAM_WD = 0.01
ADAM_STEP = 10
SGU_BLOCK = 128
SGU_CHUNK = 64
SCONV_K = 3
CCONV_K = 31
HALO = 32
V7X_VMEM_LIMIT = 48 * 1024 * 1024
MESH_AXES = ("x", "y", "c")
N_DEV = 8
_GELU_C0 = math.sqrt(2.0 / math.pi)
_GELU_C1 = 0.044715

BS = pl.BlockSpec
SDS = jax.ShapeDtypeStruct
ANY = pl.BlockSpec(memory_space=pl.ANY)


def _tile(n, pref, align=128):
    if n <= pref:
        return n
    t = pref - pref % align
    while t > 0:
        if n % t == 0:
            return t
        t -= align
    return n


def _sig(v):
    return 1.0 / (1.0 + jnp.exp(-v))


def _gelu(v):
    t = jnp.tanh(_GELU_C0 * (v + _GELU_C1 * (v * v * v)))
    return 0.5 * v * (1.0 + t), t


def _gelu_grad(v, t):
    return 0.5 * (1.0 + t) + 0.5 * v * (1.0 - t * t) * (_GELU_C0 * (1.0 + 3.0 * _GELU_C1 * v * v))


def _ln_stats(v):
    mu = jnp.mean(v, axis=-1, keepdims=True)
    vc = v - mu
    var = jnp.mean(vc * vc, axis=-1, keepdims=True)
    rstd = lax.rsqrt(var + EPS)
    return vc * rstd, rstd


def _ln_bwd(dvh, vh, rstd):
    return rstd * (dvh - jnp.mean(dvh, axis=-1, keepdims=True) - vh * jnp.mean(dvh * vh, axis=-1, keepdims=True))


def _dot(a, b):
    return jnp.dot(a, b, preferred_element_type=F32)


def _dot_nt(a, b):
    return lax.dot_general(a, b, (((1,), (1,)), ((), ())), preferred_element_type=F32)


def _dot_tn(a, b):
    return lax.dot_general(a, b, (((0,), (0,)), ((), ())), preferred_element_type=F32)


def _mesh_pos():
    return lax.axis_index("x"), lax.axis_index("y"), lax.axis_index("c")


class _Comm:
    def __init__(self, ins, out_shape, sems, start, finish, aliases=None, middle=None):
        self.ins, self.out_shape, self.sems, self.start, self.finish = ins, out_shape, sems, start, finish
        self.aliases = aliases or {}
        self.middle = middle


def _gather_comm(shards, units):
    n_u = len(units)
    out_shape = []
    for t, l in units:
        shp = shards[t].shape if l is None else shards[t].shape[1:]
        out_shape.append(SDS((N_DEV,) + tuple(shp), shards[t].dtype))

    def upper_rows(o):
        shp = out_shape[o].shape[1:]
        assert shp[0] >= 2
        return shp[0] // 2 if len(shp) > 2 or shp[0] < 32 else shp[0] // 32 * 16

    def tools(ins, dsts, sems):
        send_sems, recv_sems, local_sems = sems
        x, y, c = _mesh_pos()
        me, sib = (x, y, c), (x, y, 1 - c)
        xn, yn, dg = (1 - x, y, c), (x, 1 - y, c), (1 - x, 1 - y, c)

        def src_of(o):
            t, l = units[o]
            return ins[t] if l is None else ins[t].at[l]

        def row(o, p, part=None):
            r = dsts[o].at[4 * p[0] + 2 * p[1] + p[2]]
            if part is None:
                return r
            h = upper_rows(o)
            return r.at[pl.ds(0, h)] if part == "upper" else r.at[pl.ds(h, out_shape[o].shape[1] - h)]

        def copy(o, k, src, dst, to):
            return pltpu.make_async_remote_copy(
                src_ref=src, dst_ref=dst, send_sem=send_sems.at[o * 8 + k], recv_sem=recv_sems.at[o * 8 + k],
                device_id=to, device_id_type=pl.DeviceIdType.MESH)

        def send(o, k):
            if k < 3:
                return copy(o, k, src_of(o), row(o, me), (sib, xn, yn)[k])
            if k == 3:
                return copy(o, k, row(o, xn, "upper"), row(o, xn, "upper"), yn)
            if k == 4:
                return copy(o, k, row(o, yn, "lower"), row(o, yn, "lower"), xn)
            blk = (xn, yn, dg)[k - 5]
            return copy(o, k, row(o, blk), row(o, blk), sib)

        def landed(o, k):
            def other(p):
                return (p[0], p[1], 1 - c)

            dst = (row(o, sib), row(o, xn), row(o, yn), row(o, dg, "upper"), row(o, dg, "lower"),
                   row(o, other(xn)), row(o, other(yn)), row(o, other(dg)))[k]
            return copy(o, k, dst, dst, me)

        def local(o):
            return pltpu.make_async_copy(src_of(o), row(o, me), local_sems.at[o])

        return send, landed, local

    def start(ins, dsts, sems):
        send, _, local = tools(ins, dsts, sems)
        for o in range(n_u):
            local(o).start()
            for k in (1, 2, 0):
                send(o, k).start()

    def middle(ins, dsts, sems):
        send, landed, _ = tools(ins, dsts, sems)
        for o in range(n_u):
            landed(o, 1).wait_recv()
            send(o, 3).start()
            landed(o, 2).wait_recv()
            send(o, 4).start()
            send(o, 5).start()
            send(o, 6).start()

    def finish(ins, dsts, sems):
        send, landed, local = tools(ins, dsts, sems)
        for o in range(n_u):
            landed(o, 3).wait_recv()
            landed(o, 4).wait_recv()
            send(o, 7).start()
        for o in range(n_u):
            for k in (0, 5, 6, 7):
                landed(o, k).wait_recv()
        for o in range(n_u):
            for k in range(8):
                send(o, k).wait_send()
            local(o).wait()

    sems = [pltpu.SemaphoreType.DMA((8 * n_u,)), pltpu.SemaphoreType.DMA((8 * n_u,)), pltpu.SemaphoreType.DMA((n_u,))]
    return _Comm(list(shards), out_shape, sems, start, finish, middle=middle)


PEERS_ALL = (1, 2, 3, 4, 5, 6, 7)
PEERS_SAME_CORE = (1, 2, 4, 6)
PEERS_OTHER_CORE = (3, 5, 7)
PEERS_BUT_NEAR_OTHER = (1, 2, 4, 6, 7)
PEERS_NEAR_OTHER = (3, 5)


def _scatter_comm(parts, peers=PEERS_ALL, into=None):
    n_u = len(parts)

    def tools(ins, dsts, sems):
        send_sems, recv_sems, local_sems = sems
        x, y, c = _mesh_pos()
        me = 4 * x + 2 * y + c

        def peer(k):
            return ((x + ((k >> 2) & 1)) % 2, (y + ((k >> 1) & 1)) % 2, (c + (k & 1)) % 2)

        def copy(u, k, wait=False):
            p = peer(k)
            pi = 4 * p[0] + 2 * p[1] + p[2]
            return pltpu.make_async_remote_copy(
                src_ref=ins[u].at[pi], dst_ref=dsts[u].at[pi if wait else me],
                send_sem=send_sems.at[u * 7 + k - 1], recv_sem=recv_sems.at[u * 7 + k - 1],
                device_id=p, device_id_type=pl.DeviceIdType.MESH)

        def local(u):
            return pltpu.make_async_copy(ins[u].at[me], dsts[u].at[me], local_sems.at[u])

        return copy, local

    def start(ins, dsts, sems):
        copy, local = tools(ins, dsts, sems)
        for u in range(n_u):
            if into is None:
                local(u).start()
            for k in peers:
                copy(u, k).start()

    def finish(ins, dsts, sems):
        copy, local = tools(ins, dsts, sems)
        for u in range(n_u):
            for k in peers:
                copy(u, k, wait=True).wait()
            if into is None:
                local(u).wait()

    sems = [pltpu.SemaphoreType.DMA((7 * n_u,)), pltpu.SemaphoreType.DMA((7 * n_u,)), pltpu.SemaphoreType.DMA((n_u,))]
    aliases = {} if into is None else {n_u + u: u for u in range(n_u)}
    return _Comm(list(parts) + list(into or []), [SDS(p.shape, p.dtype) for p in parts], sems, start, finish, aliases)


def _run_comm(comm, name):
    n_i, n_o = len(comm.ins), len(comm.out_shape)

    def body(*refs):
        comm.start(refs[:n_i], refs[n_i:n_i + n_o], refs[n_i + n_o:])
        if comm.middle is not None:
            comm.middle(refs[:n_i], refs[n_i:n_i + n_o], refs[n_i + n_o:])
        comm.finish(refs[:n_i], refs[n_i:n_i + n_o], refs[n_i + n_o:])

    return pl.pallas_call(
        body, name=name, out_shape=comm.out_shape, in_specs=[ANY] * n_i, out_specs=[ANY] * n_o,
        scratch_shapes=comm.sems,
    )(*comm.ins)


def _pcall(body, name, grid, in_specs, out_specs, out_shape, scratch, sem, args, comm=None, aliases=None):
    n_i, n_o, n_s = len(in_specs), len(out_specs), len(scratch)
    aliases = aliases or {}
    if comm is None:
        res = pl.pallas_call(
            body, name=name, grid=grid, in_specs=in_specs, out_specs=out_specs, out_shape=out_shape,
            scratch_shapes=scratch, input_output_aliases=aliases,
            compiler_params=pltpu.CompilerParams(dimension_semantics=sem, vmem_limit_bytes=V7X_VMEM_LIMIT),
        )(*args)
        return res, []
    n_ci, n_co = len(comm.ins), len(comm.out_shape)

    def wrapped(*refs):
        ins = refs[:n_i]
        cins = refs[n_i:n_i + n_ci]
        outs = refs[n_i + n_ci:n_i + n_ci + n_o]
        couts = refs[n_i + n_ci + n_o:n_i + n_ci + n_o + n_co]
        rest = refs[n_i + n_ci + n_o + n_co:]
        step = 0
        for d, g in enumerate(grid):
            step = step * g + pl.program_id(d)
        n_steps = math.prod(grid)
        mid = (n_steps * 5) // 8
        staged = comm.middle is not None and 0 < mid < n_steps - 1

        @pl.when(step == 0)
        def _():
            comm.start(cins, couts, rest[n_s:])

        if staged:
            @pl.when(step == mid)
            def _():
                comm.middle(cins, couts, rest[n_s:])

        body(*ins, *outs, *rest[:n_s])

        @pl.when(step == n_steps - 1)
        def _():
            if comm.middle is not None and not staged:
                comm.middle(cins, couts, rest[n_s:])
            comm.finish(cins, couts, rest[n_s:])

    res = pl.pallas_call(
        wrapped, name=name, grid=grid, in_specs=list(in_specs) + [ANY] * n_ci,
        out_specs=list(out_specs) + [ANY] * n_co, out_shape=list(out_shape) + list(comm.out_shape),
        scratch_shapes=list(scratch) + list(comm.sems),
        input_output_aliases={**aliases, **{n_i + ci: n_o + co for ci, co in comm.aliases.items()}},
        compiler_params=pltpu.CompilerParams(dimension_semantics=("arbitrary",) * len(grid),
                                             vmem_limit_bytes=V7X_VMEM_LIMIT),
    )(*args, *comm.ins)
    return res[:n_o], res[n_o:]


def _rmsnorm_fwd(x, g, name):
    S, D = x.shape
    tm = _tile(S, 512, 16)

    def body(x_ref, g_ref, h_ref):
        h_ref[...] = _rmsnorm_rows(x_ref[...], g_ref[...])

    (h,), _ = _pcall(body, name, (S // tm,), [BS((tm, D), lambda i: (i, 0)), BS((1, D), lambda i: (0, 0))],
                     [BS((tm, D), lambda i: (i, 0))], [SDS((S, D), BF16)], [], ("parallel",), (x, g))
    return h


def _matmul_fwd(a, w, name, comm=None, w_t=False, out_dtype=F32):
    S, K = a.shape
    C = w.shape[0]
    Fc = w.shape[1] if w_t else w.shape[2]
    tn = _tile(Fc, 1408)
    tm = _tile(S, 1024, 16)

    def body(a_ref, w_ref, o_ref):
        p = _dot_nt(a_ref[...], w_ref[0]) if w_t else _dot(a_ref[...], w_ref[0])
        o_ref[0] = p.astype(out_dtype)

    w_spec = BS((1, tn, K), lambda c, n, i: (c, n, 0)) if w_t else BS((1, K, tn), lambda c, n, i: (c, 0, n))
    (o,), extra = _pcall(
        body, name, (C, Fc // tn, S // tm), [BS((tm, K), lambda c, n, i: (i, 0)), w_spec],
        [BS((1, tm, tn), lambda c, n, i: (c, i, n))], [SDS((C, S, Fc), out_dtype)], [],
        ("parallel", "parallel", "parallel"), (a, w), comm)
    return o, extra


def _rmsnorm_rows(xv, g):
    r = lax.rsqrt(jnp.mean(xv * xv, axis=-1, keepdims=True) + EPS)
    return (xv * r * g).astype(BF16)


def _swiglu_down(ab, w2, x, g_next, name, comm=None):
    _, S, F = ab.shape
    D = w2.shape[1]
    tk = _tile(F, 1408)
    tm = _tile(S, 512, 16)
    nk = F // tk

    def body(ab_ref, w_ref, x_ref, g_ref, o_ref, h_ref):
        k = pl.program_id(1)
        a = ab_ref[0].astype(F32)
        s = a * _sig(a) * ab_ref[1].astype(F32)
        p = 0.5 * _dot(s.astype(BF16), w_ref[...])

        @pl.when(k == 0)
        def _():
            o_ref[...] = x_ref[...] + p

        @pl.when(k > 0)
        def _():
            o_ref[...] += p

        @pl.when(k == nk - 1)
        def _():
            h_ref[...] = _rmsnorm_rows(o_ref[...], g_ref[...])

    (o, h), extra = _pcall(
        body, name, (S // tm, nk),
        [BS((2, tm, tk), lambda i, k: (0, i, k)), BS((tk, D), lambda i, k: (k, 0)), BS((tm, D), lambda i, k: (i, 0)),
         BS((1, D), lambda i, k: (0, 0))],
        [BS((tm, D), lambda i, k: (i, 0)), BS((tm, D), lambda i, k: (i, 0))],
        [SDS((S, D), F32), SDS((S, D), BF16)], [], ("parallel", "arbitrary"), (ab, w2, x, g_next), comm)
    return o, h, extra


def _ffn_bwd_hidden(dy, w2, ab, name, comm=None):
    S, D = dy.shape
    F = w2.shape[0]
    tk = _tile(F, 1408)
    tm = _tile(S, 512, 16)
    te = _tile(tm, 128, 16)

    def body(dy_ref, w_ref, ab_ref, dab_ref, s_ref, ds_ref):
        ds_ref[...] = 0.5 * _dot_nt(dy_ref[...].astype(BF16), w_ref[...])
        for r0 in range(0, tm, te):
            rows = slice(r0, r0 + te)
            ds = ds_ref[rows, :]
            a = ab_ref[0, rows, :].astype(F32)
            b = ab_ref[1, rows, :].astype(F32)
            sg = _sig(a)
            sa = a * sg
            dab_ref[0, rows, :] = (ds * b * (sg * (1.0 + a * (1.0 - sg)))).astype(BF16)
            dab_ref[1, rows, :] = (ds * sa).astype(BF16)
            s_ref[rows, :] = (0.5 * (sa * b)).astype(BF16)

    (dab, sh), extra = _pcall(
        body, name, (F // tk, S // tm),
        [BS((tm, D), lambda k, i: (i, 0)), BS((tk, D), lambda k, i: (k, 0)), BS((2, tm, tk), lambda k, i: (0, i, k))],
        [BS((2, tm, tk), lambda k, i: (0, i, k)), BS((tm, tk), lambda k, i: (i, k))],
        [SDS((2, S, F), BF16), SDS((S, F), BF16)], [pltpu.VMEM((tm, tk), F32)], ("parallel", "parallel"),
        (dy, w2, ab), comm)
    return dab, sh, extra


def _matmul_tn(a, b, n_c, name, comm=None, b_shared=False):
    G, S, M = a.shape
    _, _, Fc = b.shape
    C = n_c
    tM = _tile(M, 1408)
    tn = _tile(Fc, 1408)
    ts = _tile(S, 1024, 16)
    n_s = S // ts

    def body(a_ref, b_ref, o_ref, acc):
        s = pl.program_id(4)
        p = _dot_tn(a_ref[0].astype(BF16), b_ref[0].astype(BF16))

        @pl.when(s == 0)
        def _():
            acc[...] = p

        @pl.when(s > 0)
        def _():
            acc[...] += p

        @pl.when(s == n_s - 1)
        def _():
            o_ref[0] = acc[...].astype(BF16)

    (o,), extra = _pcall(
        body, name, (G, M // tM, C, Fc // tn, n_s),
        [BS((1, ts, tM), lambda g, m, c, n, s: (g, s, m)), BS((1, ts, tn), lambda g, m, c, n, s: (c if b_shared else g * C + c, s, n))],
        [BS((1, tM, tn), lambda g, m, c, n, s: (g * C + c, m, n))], [SDS((G * C, M, Fc), BF16)],
        [pltpu.VMEM((tM, tn), F32)], ("parallel", "parallel", "parallel", "parallel", "arbitrary"), (a, b), comm)
    return o, extra


def _matmul_nt_normbwd(b, w, x, gam, dres, name, comm=None, w_t=False):
    C, S, Fc = b.shape
    D = w.shape[2] if w_t else w.shape[1]
    tk = _tile(Fc, 1408)
    tm = _tile(S, 1024, 16)
    te = _tile(tm, 256, 8)
    nk = Fc // tk

    def body(b_ref, w_ref, x_ref, g_ref, r_ref, dx_ref, dg_ref):
        i, c, k = pl.program_id(0), pl.program_id(1), pl.program_id(2)
        p = _dot(b_ref[0], w_ref[0]) if w_t else _dot_nt(b_ref[0], w_ref[0])
        first = jnp.logical_and(c == 0, k == 0)

        @pl.when(first)
        def _():
            dx_ref[...] = p

        @pl.when(jnp.logical_not(first))
        def _():
            dx_ref[...] += p

        @pl.when(jnp.logical_and(c == C - 1, k == nk - 1))
        def _():
            dgp = None
            for r0 in range(0, tm, te):
                rows = slice(r0, r0 + te)
                xv = x_ref[rows, :]
                r = lax.rsqrt(jnp.mean(xv * xv, axis=-1, keepdims=True) + EPS)
                xn = xv * r
                dh = dx_ref[rows, :]
                dxn = dh * g_ref[...]
                dx_ref[rows, :] = r_ref[rows, :] + r * (dxn - xn * jnp.mean(dxn * xn, axis=-1, keepdims=True))
                t = jnp.sum(dh * xn, axis=0, keepdims=True)
                dgp = t if dgp is None else dgp + t

            @pl.when(i == 0)
            def _():
                dg_ref[...] = dgp

            @pl.when(i > 0)
            def _():
                dg_ref[...] += dgp

    once = dict(pipeline_mode=pl.Buffered(1))
    (dx, dg), extra = _pcall(
        body, name, (S // tm, C, nk),
        [BS((1, tm, tk), lambda i, c, k: (c, i, k)),
         BS((1, tk, D), lambda i, c, k: (c, k, 0)) if w_t else BS((1, D, tk), lambda i, c, k: (c, 0, k)),
         BS((tm, D), lambda i, c, k: (i, 0), **once), BS((1, D), lambda i, c, k: (0, 0)),
         BS((tm, D), lambda i, c, k: (i, 0), **once)],
        [BS((tm, D), lambda i, c, k: (i, 0)), BS((1, D), lambda i, c, k: (0, 0))],
        [SDS((S, D), F32), SDS((1, D), F32)], [],
        ("arbitrary", "arbitrary", "arbitrary"), (b, w, x, gam, dres), comm)
    return dx, dg, extra


def _final_loss(x, gam, target, name):
    S, D = x.shape
    tm = _tile(S, 512, 8)

    def body(x_ref, g_ref, t_ref, loss_ref, dx_ref, dg_ref):
        i = pl.program_id(0)
        xv = x_ref[...]
        r = lax.rsqrt(jnp.mean(xv * xv, axis=-1, keepdims=True) + EPS)
        xn = xv * r
        err = xn * g_ref[...] - t_ref[...]
        part = 0.5 * jnp.sum(jnp.mean(err * err, axis=-1, keepdims=True), axis=0, keepdims=True)
        dy = err * (1.0 / D)
        dxn = dy * g_ref[...]
        dx_ref[...] = r * (dxn - xn * jnp.mean(dxn * xn, axis=-1, keepdims=True))
        dgp = jnp.sum(dy * xn, axis=0, keepdims=True)
        lp = jnp.broadcast_to(part, loss_ref.shape)

        @pl.when(i == 0)
        def _():
            dg_ref[...] = dgp
            loss_ref[...] = lp

        @pl.when(i > 0)
        def _():
            dg_ref[...] += dgp
            loss_ref[...] += lp

    res, _ = _pcall(
        body, name, (S // tm,),
        [BS((tm, D), lambda i: (i, 0)), BS((1, D), lambda i: (0, 0)), BS((tm, D), lambda i: (i, 0))],
        [BS((8, 128), lambda i: (0, 0)), BS((tm, D), lambda i: (i, 0)), BS((1, D), lambda i: (0, 0))],
        [SDS((8, 128), F32), SDS((S, D), F32), SDS((1, D), F32)], [], ("arbitrary",), (x, gam, target))
    return res


CONV_CHUNK = 32


def _fill_shifted(rot, n):
    for b in range(1, 8):
        rot[b, 0:n - 8, :] = rot[0, b:b + n - 8, :]


def _window(rot, off, r0, rows):
    b = off % 8
    return rot[b, off - b + r0:off - b + r0 + rows, :]


def _taps(rot, w_ref, offs, n_rows, out):
    for r0 in range(0, n_rows, CONV_CHUNK):
        acc = None
        for k, off in enumerate(offs):
            t = w_ref[k:k + 1, :] * _window(rot, off, r0, CONV_CHUNK)
            acc = t if acc is None else acc + t
        out[r0:r0 + CONV_CHUNK, :] = acc


def _tap_grads(rot, offs, g_plane, n_rows, dw_ref):
    for k, off in enumerate(offs):
        acc = None
        for r0 in range(0, n_rows, CONV_CHUNK):
            p = g_plane[0, r0:r0 + CONV_CHUNK, :] * _window(rot, off, r0, CONV_CHUNK)
            acc = p if acc is None else acc + p
        dw_ref[k:k + 1, :] += jnp.sum(acc, axis=0, keepdims=True)


def _sgu_masks():
    ii = lax.broadcasted_iota(jnp.int32, (SGU_BLOCK, SGU_BLOCK), 0) // SGU_CHUNK
    jj = lax.broadcasted_iota(jnp.int32, (SGU_BLOCK, SGU_BLOCK), 1) // SGU_CHUNK
    return jj <= ii, ii <= jj


def _mixers_fwd(proj, pool_w, pool_scale, sconv_w, cconv_w, cln_g, cln_b, sln_g, sln_b, sgu_w, sgu_bias, name,
                comm=None):
    _, S, D = proj.shape
    BW = D // 2
    GW = BW // 4
    TS = _tile(S, 256, SGU_BLOCK)
    H = HALO
    hb = TS // H

    def main(blk, col):
        return BS((1, TS, BW), lambda i: (blk, i, col))

    def back(blk, col):
        return BS((1, H, BW), lambda i: (blk, jnp.maximum(i * hb - 1, 0), col))

    def full(a):
        nd = a.ndim
        return BS(a.shape, lambda i: (0,) * nd)

    def body(pa_m, pa_b, xi_m, xi_b, bg_m, cg_m, cg_b, ca_m, ca_b, cb_m, cb_b, du_m, dv_m,
             pw, ps, sw, cw, clg, clb, slg, slb, gw, gbias, y_ref, e1, e2, e3):
        i = pl.program_id(0)
        nb = jnp.where(i > 0, 1.0, 0.0).astype(F32)
        rows = i * TS + lax.broadcasted_iota(jnp.int32, (TS, 1), 0)

        e1[0:H, :] = pa_b[0] * nb
        e1[H:H + TS, :] = pa_m[0]
        for g in range(4):
            cols = slice(g * GW, (g + 1) * GW)
            win = 2 << g
            wsum = e1[H:H + TS, cols]
            for k in range(1, win):
                wsum = wsum + e1[H - k:H - k + TS, cols]
            cnt = jnp.minimum(rows + 1, win).astype(F32)
            d = wsum / cnt - e1[H:H + TS, cols]
            yg = _dot(d.astype(BF16), pw[g].astype(BF16)) * ps[:, cols]
            y_ref[0, :, cols] = yg.astype(BF16)

        e2[0:H, :] = cg_b[0] * xi_b[0] * nb
        e2[H:H + TS, :] = cg_m[0] * xi_m[0]
        cz = sw[0:1, :] * e2[H - 2:H - 2 + TS, :]
        for k in range(1, SCONV_K):
            cz = cz + sw[k:k + 1, :] * e2[H - 2 + k:H - 2 + k + TS, :]
        y_ref[1] = (bg_m[0] * cz).astype(BF16)

        e3[0, 0:H, :] = ca_b[0] * _sig(cb_b[0]) * nb
        e3[0, H:H + TS, :] = ca_m[0] * _sig(cb_m[0])
        _fill_shifted(e3, H + TS)
        _taps(e3, cw, [H - (CCONV_K - 1) + k for k in range(CCONV_K)], TS, e1)
        yh, _ = _ln_stats(e1[0:TS, :])
        y2 = yh * clg[...] + clb[...]
        y_ref[2] = (y2 * _sig(y2)).astype(BF16)

        u, _ = _gelu(du_m[0])
        v, _ = _gelu(dv_m[0])
        vh, _ = _ln_stats(v)
        vn = vh * slg[...] + slb[...]
        mask, _ = _sgu_masks()
        for h in range(4):
            wm = jnp.where(mask, gw[h], 0.0).astype(BF16)
            cs = slice(h * GW, (h + 1) * GW)
            for n in range(TS // SGU_BLOCK):
                rs = slice(n * SGU_BLOCK, (n + 1) * SGU_BLOCK)
                z = _dot(wm, vn[rs, cs].astype(BF16)) + gbias[h]
                y_ref[3, rs, cs] = (u[rs, cs] * z).astype(BF16)

    args = [proj] * 13 + [pool_w, pool_scale, sconv_w, cconv_w, cln_g, cln_b, sln_g, sln_b, sgu_w, sgu_bias]
    in_specs = [main(0, 0), back(0, 0), main(0, 1), back(0, 1), main(1, 0), main(1, 1), back(1, 1),
                main(2, 0), back(2, 0), main(2, 1), back(2, 1), main(3, 0), main(3, 1)]
    in_specs += [full(a) for a in args[13:]]
    (y,), extra = _pcall(body, name, (S // TS,), in_specs, [BS((4, TS, BW), lambda i: (0, i, 0))],
                         [SDS((4, S, BW), BF16)],
                         [pltpu.VMEM((H + TS, BW), F32)] * 2 + [pltpu.VMEM((8, H + TS, BW), F32)], ("parallel",),
                         args, comm)
    return y, extra


def _mixers_bwd(proj, dy, dproj_gates, pool_w, pool_wt, pool_scale, sconv_w, cconv_w, cln_g, cln_b, sln_g, sln_b,
                sgu_w, sgu_wt, sgu_bias, name, comm=None):
    _, S, D = proj.shape
    BW = D // 2
    GW = BW // 4
    TS = _tile(S, 256, SGU_BLOCK)
    H = HALO
    hb = TS // H
    n_t = S // TS
    E = TS + H

    def main(blk, col):
        return BS((1, TS, BW), lambda i: (blk, i, col))

    def back(blk, col):
        return BS((1, H, BW), lambda i: (blk, jnp.maximum(i * hb - 1, 0), col))

    def front(blk, col):
        return BS((1, H, BW), lambda i: (blk, jnp.minimum((i + 1) * hb, S // H - 1), col))

    def full(a):
        nd = a.ndim
        return BS(a.shape, lambda i: (0,) * nd)

    def body(pa_b, pa_m, xi_b, xi_m, bg_m, bg_f, cg_b, cg_m, ca_b, ca_m, ca_f, cb_b, cb_m, cb_f, du_m, dv_m,
             dya_m, dya_f, dyb_m, dyb_f, dyc_m, dyc_f, dyd_m,
             pw, pwt, ps, sw, cw, clg, clb, slg, slb, gw, gwt, gbias, _gates_in,
             dp_ref, dpw, dps, dsw, dcw, dclg, dclb, dslg, dslb, dgw, dgb,
             e1, e2, e3, e4, e5, ra, rb):
        i = pl.program_id(0)
        nb = jnp.where(i > 0, 1.0, 0.0).astype(F32)
        nf = jnp.where(i < n_t - 1, 1.0, 0.0).astype(F32)
        rows_m = i * TS + lax.broadcasted_iota(jnp.int32, (TS, 1), 0)
        rows_e = i * TS + lax.broadcasted_iota(jnp.int32, (E, 1), 0)

        @pl.when(i == 0)
        def _():
            for r in (dpw, dps, dsw, dcw, dclg, dclb, dslg, dslb, dgw, dgb):
                r[...] = jnp.zeros(r.shape, F32)

        e1[0:H, :] = pa_b[0] * nb
        e1[H:H + TS, :] = pa_m[0]
        e2[0:TS, :] = dya_m[0] * ps[...]
        e2[TS:E, :] = dya_f[0] * ps[...] * nf
        for g in range(4):
            cols = slice(g * GW, (g + 1) * GW)
            win = 2 << g
            a_m = e1[H:H + TS, cols]
            wsum = a_m
            for k in range(1, win):
                wsum = wsum + e1[H - k:H - k + TS, cols]
            d = wsum / jnp.minimum(rows_m + 1, win).astype(F32) - a_m
            d16 = d.astype(BF16)
            dyp = e2[0:E, cols].astype(BF16)
            dd = _dot(dyp, pwt[g].astype(BF16))
            e3[0:E, cols] = dd / jnp.minimum(rows_e + 1, win).astype(F32)
            da = e3[0:TS, cols] - dd[0:TS]
            for k in range(1, win):
                da = da + e3[k:k + TS, cols]
            dp_ref[0, :, cols] = da.astype(BF16)
            ypre = _dot(d16, pw[g].astype(BF16))
            dps[:, cols] += jnp.sum(dya_m[0][:, cols] * ypre, axis=0, keepdims=True)
            dpw[g] += _dot(jnp.transpose(d).astype(BF16), dyp[0:TS])

        e4[0:H, :] = cg_b[0] * xi_b[0] * nb
        e4[H:H + TS, :] = cg_m[0] * xi_m[0]
        dyb = dyb_m[0]
        e5[0:TS, :] = dyb * bg_m[0]
        e5[TS:E, :] = dyb_f[0] * bg_f[0] * nf
        dcz = e5[0:TS, :]
        cz = None
        dz = None
        for k in range(SCONV_K):
            zk = e4[H - 2 + k:H - 2 + k + TS, :]
            wk = sw[k:k + 1, :]
            cz = wk * zk if cz is None else cz + wk * zk
            t = wk * e5[2 - k:2 - k + TS, :]
            dz = t if dz is None else dz + t
            dsw[k:k + 1, :] += jnp.sum(dcz * zk, axis=0, keepdims=True)
        dp_ref[0, :, BW:2 * BW] = (dz * cg_m[0]).astype(BF16)
        dp_ref[1, :, 0:BW] = (dyb * cz).astype(BF16)
        dp_ref[1, :, BW:2 * BW] = (dz * xi_m[0]).astype(BF16)

        sgm = _sig(cb_m[0])
        ra[0, 0:H, :] = ca_b[0] * _sig(cb_b[0]) * nb
        ra[0, H:H + TS, :] = ca_m[0] * sgm
        ra[0, H + TS:H + E, :] = ca_f[0] * _sig(cb_f[0]) * nf
        _fill_shifted(ra, H + E)
        fwd_offs = [H - (CCONV_K - 1) + k for k in range(CCONV_K)]
        _taps(ra, cw, fwd_offs, E, e4)
        yh, rstd = _ln_stats(e4[0:E, :])
        y2 = yh * clg[...] + clb[...]
        s2 = _sig(y2)
        e1[0:TS, :] = dyc_m[0]
        e1[TS:E, :] = dyc_f[0] * nf
        dy2 = e1[0:E, :] * (s2 * (1.0 + y2 * (1.0 - s2)))
        dclg[...] += jnp.sum((dy2 * yh)[0:TS], axis=0, keepdims=True)
        dclb[...] += jnp.sum(dy2[0:TS], axis=0, keepdims=True)
        rb[0, 0:E, :] = _ln_bwd(dy2 * clg[...], yh, rstd)
        _fill_shifted(rb, E)
        _taps(rb, cw, [CCONV_K - 1 - k for k in range(CCONV_K)], TS, e5)
        _tap_grads(ra, fwd_offs, rb, TS, dcw)
        dy0 = e5[0:TS, :]
        dp_ref[2, :, 0:BW] = (dy0 * sgm).astype(BF16)
        dp_ref[2, :, BW:2 * BW] = (dy0 * ca_m[0] * (sgm * (1.0 - sgm))).astype(BF16)

        pu = du_m[0]
        pv = dv_m[0]
        u, tu = _gelu(pu)
        v, tv = _gelu(pv)
        vh, vr = _ln_stats(v)
        vn = vh * slg[...] + slb[...]
        dyd = dyd_m[0]
        mask, mask_t = _sgu_masks()
        for h in range(4):
            wm = jnp.where(mask, gw[h], 0.0).astype(BF16)
            wmt = jnp.where(mask_t, gwt[h], 0.0).astype(BF16)
            cs = slice(h * GW, (h + 1) * GW)
            for n in range(TS // SGU_BLOCK):
                rs = slice(n * SGU_BLOCK, (n + 1) * SGU_BLOCK)
                vb = vn[rs, cs].astype(BF16)
                z = _dot(wm, vb) + gbias[h]
                dzb = dyd[rs, cs] * u[rs, cs]
                dz16 = dzb.astype(BF16)
                e3[rs, cs] = dyd[rs, cs] * z
                e4[rs, cs] = _dot(wmt, dz16)
                dgw[h] += jnp.where(mask, _dot_nt(dz16, vb), 0.0)
                dgb[h] += dzb
        dvn = e4[0:TS, :]
        dslg[...] += jnp.sum(dvn * vh, axis=0, keepdims=True)
        dslb[...] += jnp.sum(dvn, axis=0, keepdims=True)
        dv = _ln_bwd(dvn * slg[...], vh, vr)
        dp_ref[3, :, 0:BW] = (e3[0:TS, :] * _gelu_grad(pu, tu)).astype(BF16)
        dp_ref[3, :, BW:2 * BW] = (dv * _gelu_grad(pv, tv)).astype(BF16)

        @pl.when(i == n_t - 1)
        def _():
            for h in range(4):
                dgb[h] = jnp.broadcast_to(jnp.sum(dgb[h], axis=1, keepdims=True), dgb.shape[1:])

    params = [pool_w, pool_wt, pool_scale, sconv_w, cconv_w, cln_g, cln_b, sln_g, sln_b, sgu_w, sgu_wt, sgu_bias]
    args = [proj] * 16 + [dy] * 7 + params + [dproj_gates]
    in_specs = [back(0, 0), main(0, 0), back(0, 1), main(0, 1), main(1, 0), front(1, 0), back(1, 1), main(1, 1),
                back(2, 0), main(2, 0), front(2, 0), back(2, 1), main(2, 1), front(2, 1), main(3, 0), main(3, 1),
                main(0, 0), front(0, 0), main(1, 0), front(1, 0), main(2, 0), front(2, 0), main(3, 0)]
    in_specs += [full(a) for a in params] + [ANY]
    small = [SDS(pool_w.shape, F32), SDS(pool_scale.shape, F32), SDS(sconv_w.shape, F32), SDS(cconv_w.shape, F32),
             SDS(cln_g.shape, F32), SDS(cln_b.shape, F32), SDS(sln_g.shape, F32), SDS(sln_b.shape, F32),
             SDS(sgu_w.shape, F32), SDS(sgu_bias.shape, F32)]
    out_specs = [BS((4, TS, D), lambda i: (0, i, 0))] + [full(s) for s in small]
    return _pcall(body, name, (n_t,), in_specs, out_specs, [SDS(dproj_gates.shape, BF16)] + small,
                  [pltpu.VMEM((TS + 2 * H, BW), F32)] * 5 + [pltpu.VMEM((8, TS + 2 * H, BW), F32)] * 2,
                  ("arbitrary",), args, comm, aliases={len(args) - 1: 0})


def _merge_fwd(y, proj, w_up, w_out, x, g_next, name, comm=None):
    _, S, BW = y.shape
    D = x.shape[1]
    tm = _tile(S, 256, 16)

    def body(y_ref, pg_ref, wu_ref, wo_ref, x_ref, g_ref, o_ref, m_ref, h_ref):
        merged = None
        for g in range(4):
            t = _sig(pg_ref[g]) * _dot(y_ref[g], wu_ref[g])
            merged = t if merged is None else merged + t
        m16 = merged.astype(BF16)
        m_ref[...] = m16
        xn = x_ref[...] + _dot(m16, wo_ref[...])
        o_ref[...] = xn
        h_ref[...] = _rmsnorm_rows(xn, g_ref[...])

    (o, m, h), extra = _pcall(
        body, name, (S // tm,),
        [BS((4, tm, BW), lambda i: (0, i, 0)), BS((4, tm, D), lambda i: (1, i, 0)),
         BS((4, BW, D), lambda i: (0, 0, 0)), BS((D, D), lambda i: (0, 0)), BS((tm, D), lambda i: (i, 0)),
         BS((1, D), lambda i: (0, 0))],
        [BS((tm, D), lambda i: (i, 0)), BS((tm, D), lambda i: (i, 0)), BS((tm, D), lambda i: (i, 0))],
        [SDS((S, D), F32), SDS((S, D), BF16), SDS((S, D), BF16)], [], ("parallel",),
        (y, proj, w_up, w_out, x, g_next), comm)
    return o, m, h, extra


def _merge_bwd(dx, y, proj, w_up, w_out, name, comm=None):
    _, S, BW = y.shape
    D = dx.shape[1]
    tm = _tile(S, 256, 16)

    def body(dx_ref, y_ref, pg_ref, wu_ref, wo_ref, dup_ref, dp_ref, dy_ref):
        dm = _dot_nt(dx_ref[...].astype(BF16), wo_ref[...])
        for g in range(4):
            gate = _sig(pg_ref[g])
            up = _dot(y_ref[g], wu_ref[g])
            dup = (dm * gate).astype(BF16)
            dup_ref[g] = dup
            dp_ref[g] = (dm * up * (gate * (1.0 - gate))).astype(BF16)
            dy_ref[g] = _dot_nt(dup, wu_ref[g])

    res, extra = _pcall(
        body, name, (S // tm,),
        [BS((tm, D), lambda i: (i, 0)), BS((4, tm, BW), lambda i: (0, i, 0)), BS((4, tm, D), lambda i: (1, i, 0)),
         BS((4, BW, D), lambda i: (0, 0, 0)), BS((D, D), lambda i: (0, 0))],
        [BS((4, tm, D), lambda i: (0, i, 0)), BS((4, tm, D), lambda i: (1, i, 0)), BS((4, tm, BW), lambda i: (0, i, 0))],
        [SDS((4, S, D), BF16), SDS((8, S, D), BF16), SDS((4, S, BW), F32)], [], ("parallel",),
        (dx, y, proj, w_up, w_out), comm)
    return res, extra


def _adamw(w, g, m, v):
    m = ADAM_B1 * m + (1.0 - ADAM_B1) * g
    v = ADAM_B2 * v + (1.0 - ADAM_B2) * (g * g)
    m_hat = m / (1.0 - ADAM_B1 ** ADAM_STEP)
    v_hat = v / (1.0 - ADAM_B2 ** ADAM_STEP)
    delta = -ADAM_LR * (m_hat / (jnp.sqrt(v_hat) + ADAM_EPS) + ADAM_WD * w)
    return delta, m, v


def _adamw_sharded(parts, w, m, v, name, comm=None):
    L, R, C = w.shape
    tr = _tile(R, 256, 16)

    def body(*refs):
        p_refs = refs[:L]
        w_ref, m_ref, v_ref, g_out, d_out, m_out, v_out = refs[L:]
        l = pl.program_id(0)
        g = None
        for d in range(N_DEV):
            t = p_refs[0][d].astype(F32)
            for j in range(1, L):
                t = jnp.where(l == j, p_refs[j][d].astype(F32), t)
            g = t if g is None else g + t
        dl, mn, vn = _adamw(w_ref[0], g, m_ref[0], v_ref[0])
        g_out[0] = g
        d_out[0] = dl
        m_out[0] = mn
        v_out[0] = vn

    def part_spec(j):
        return BS((N_DEV, tr, C), lambda l, r: (0, jnp.where(l == j, r, 0), 0))

    blk = BS((1, tr, C), lambda l, r: (l, r, 0))
    return _pcall(body, name, (L, R // tr), [part_spec(j) for j in range(L)] + [blk, blk, blk], [blk] * 4,
                  [SDS((L, R, C), F32)] * 4, [], ("parallel", "parallel"), (*parts, w, m, v), comm)


def _adamw_replicated(gathered, layout, wmv, name):
    n_b = len(gathered)
    n_p = len(layout)

    def body(*refs):
        bufs = refs[:n_b]
        prm = refs[n_b:n_b + 3 * n_p]
        outs = refs[n_b + 3 * n_p:n_b + 7 * n_p]
        sums = refs[n_b + 7 * n_p:]
        for b in range(n_b):
            s = bufs[b][0]
            for d in range(1, N_DEV):
                s = s + bufs[b][d]
            sums[b][...] = s
        for p, (b, r0, nr) in enumerate(layout):
            g = sums[b][r0:r0 + nr, :]
            d, mn, vn = _adamw(prm[3 * p][...], g, prm[3 * p + 1][...], prm[3 * p + 2][...])
            outs[4 * p][...] = g
            outs[4 * p + 1][...] = d
            outs[4 * p + 2][...] = mn
            outs[4 * p + 3][...] = vn

    flat = [a for t in wmv for a in t]
    out_shape = []
    for (w, _, _) in wmv:
        out_shape += [SDS(w.shape, F32)] * 4
    out_shape += [SDS(g.shape[1:], F32) for g in gathered]
    return pl.pallas_call(
        body, name=name, out_shape=out_shape,
        compiler_params=pltpu.CompilerParams(vmem_limit_bytes=V7X_VMEM_LIMIT),
    )(*gathered, *flat)


def _adamw_small(g, w, m, v, name):
    def body(g_ref, w_ref, m_ref, v_ref, d_out, m_out, v_out):
        d, mn, vn = _adamw(w_ref[...], g_ref[...], m_ref[...], v_ref[...])
        d_out[...] = d
        m_out[...] = mn
        v_out[...] = vn

    return pl.pallas_call(body, name=name, out_shape=[SDS(w.shape, F32)] * 3)(g, w, m, v)


def _pad_rows(a, rows):
    return jnp.pad(a, ((0, rows - a.shape[0]), (0, 0)))


def kernel(x, ffn1_norm, ffn1_w13, ffn1_w2, mix_norm, w_in, pool_w, pool_scale, sconv_w, cconv_w, cconv_ln_g, cconv_ln_b, sgu_ln_g, sgu_ln_b, sgu_w, sgu_b, w_up, w_out, ffn2_norm, ffn2_w13, ffn2_w2, final_norm, loss_target, m_ffn1_norm, m_ffn1_w13, m_ffn1_w2, m_mix_norm, m_w_in, m_pool_w, m_pool_scale, m_sconv_w, m_cconv_w, m_cconv_ln_g, m_cconv_ln_b, m_sgu_ln_g, m_sgu_ln_b, m_sgu_w, m_sgu_b, m_w_up, m_w_out, m_ffn2_norm, m_ffn2_w13, m_ffn2_w2, m_final_norm, v_ffn1_norm, v_ffn1_w13, v_ffn1_w2, v_mix_norm, v_w_in, v_pool_w, v_pool_scale, v_sconv_w, v_cconv_w, v_cconv_ln_g, v_cconv_ln_b, v_sgu_ln_g, v_sgu_ln_b, v_sgu_w, v_sgu_b, v_w_up, v_w_out, v_ffn2_norm, v_ffn2_w13, v_ffn2_w2, v_final_norm):
    P = dict(locals())
    L = ffn1_norm.shape[0]
    S, D = x.shape[1], x.shape[2]
    BW = D // 2
    GW = BW // 4
    F = ffn1_w2.shape[1] * N_DEV
    fs = ffn1_w13.shape[2]
    cw = sconv_w.shape[2]
    me = 4 * lax.axis_index("x") + 2 * lax.axis_index("y") + lax.axis_index("c")

    big = ["ffn1_w13", "ffn1_w2", "w_in", "w_up", "w_out", "ffn2_w13", "ffn2_w2"]
    shards = [(jnp.swapaxes(P[n], 1, 2) if n.endswith("w13") else P[n]).astype(BF16) for n in big]
    conv_local = jnp.concatenate([sconv_w, cconv_w], axis=1)

    def gather_of(units):
        return _gather_comm(shards, [(big.index(n), l) for n, l in units])

    def ready(n, g):
        if n.endswith("w13"):
            return g.reshape(2, F, D)
        if n.endswith("w2"):
            return g.reshape(F, D)
        if n == "w_up":
            return jnp.transpose(g, (1, 2, 0, 3)).reshape(4, BW, D)
        if n == "w_out":
            return g.reshape(D, D)
        return g

    W = {}

    def take(units, arrays):
        for (n, l), g in zip(units, arrays):
            W[n, l] = ready(n, g)

    first_units = [("ffn1_w13", 0), ("ffn1_w2", 0)]
    plan = {("ffn1_up", 0): [("w_in", 0)],
            ("ffn1_down", 0): [("w_up", 0), ("w_out", 0)],
            ("proj", 0): [("ffn2_w13", 0), ("ffn2_w2", 0)],
            ("mixers", 0): [("ffn1_w13", 1)], ("merge", 0): [("ffn1_w2", 1)],
            ("ffn2_up", 0): [("w_in", 1)], ("ffn2_down", 0): [("w_up", 1), ("w_out", 1)],
            ("ffn1_up", 1): [("ffn2_w13", 1)], ("ffn1_down", 1): [("ffn2_w2", 1)]}
    assert L <= 2

    def carried(key):
        units = [u for u in plan.get(key, []) if u[1] < L]
        return units, (gather_of(units) if units else None)

    first = _gather_comm(shards + [conv_local], [(big.index(n), l) for n, l in first_units] + [(len(big), None)])
    got = _run_comm(first, "gather_first_weights")
    take(first_units, got[:2])
    conv_full = jnp.transpose(got[2], (1, 2, 0, 3)).reshape(L, SCONV_K + CCONV_K, N_DEV * cw)
    sconv_full = conv_full[:, :SCONV_K]
    cconv_full = conv_full[:, SCONV_K:]

    sgu_bias = jnp.broadcast_to(sgu_b[:, :, :, None], sgu_b.shape + (GW,))
    pool_wt = jnp.swapaxes(pool_w, 2, 3)
    sgu_wt = jnp.swapaxes(sgu_w, 2, 3)

    def row(a, l):
        return a[l][None, :]

    saved = []
    xc = x[0]
    h = _rmsnorm_fwd(xc, row(ffn1_norm, 0), "first_norm_fwd")
    for l in range(L):
        sv = {}
        for tag in ("ffn1", None, "ffn2"):
            if tag is None:
                sv["x_mix"] = xc
                units, comm = carried(("proj", l))
                proj, extra = _matmul_fwd(h, W["w_in", l], "proj_fwd", comm)
                take(units, extra)
                units, comm = carried(("mixers", l))
                y, extra = _mixers_fwd(proj, pool_w[l], row(pool_scale, l), sconv_full[l], cconv_full[l],
                                       row(cconv_ln_g, l), row(cconv_ln_b, l), row(sgu_ln_g, l), row(sgu_ln_b, l),
                                       sgu_w[l], sgu_bias[l], "mixers_fwd", comm)
                take(units, extra)
                units, comm = carried(("merge", l))
                sv.update(h_mix=h, proj=proj, y=y)
                xc, sv["merged"], h, extra = _merge_fwd(y, proj, W["w_up", l], W["w_out", l], xc, row(ffn2_norm, l),
                                                        "merge_fwd", comm)
                take(units, extra)
            else:
                sv["x_" + tag] = xc
                units, comm = carried((tag + "_up", l))
                ab, extra = _matmul_fwd(h, W[tag + "_w13", l], "ffn_up_fwd", comm, w_t=True, out_dtype=BF16)
                take(units, extra)
                units, comm = carried((tag + "_down", l))
                sv.update({"h_" + tag: h, "ab_" + tag: ab})
                if tag == "ffn1":
                    g_next = row(mix_norm, l)
                else:
                    g_next = row(ffn1_norm, l + 1) if l + 1 < L else final_norm[None, :]
                xc, h, extra = _swiglu_down(ab, W[tag + "_w2", l], xc, g_next, "ffn_down_fwd", comm)
                take(units, extra)
        saved.append(sv)

    loss_part, dx, d_final = _final_loss(xc, final_norm[None, :], loss_target[0], "loss_head")
    loss = lax.psum(loss_part[0, 0], MESH_AXES)

    R = {}
    second = []

    def rest_of_sends():
        keys = [e[0] for e in second]
        comm = None
        if second:
            assert len({e[3] for e in second}) == 1
            comm = _scatter_comm([e[1] for e in second], second[0][3], [e[2] for e in second])
        second.clear()
        return keys, comm

    def settle(keys, arrays):
        for k, a in zip(keys, arrays):
            R[k] = a

    wide = ["ffn1_norm", "mix_norm", "ffn2_norm", "final_norm"]
    half = ["pool_scale", "cconv_ln_g", "cconv_ln_b", "sgu_ln_g", "sgu_ln_b"]
    narrow = ["pool_w", "sgu_w", "sgu_b"]
    small_names = wide + half + narrow
    small_g = [dict() for _ in range(L)]
    widths = []
    for n in small_names:
        if P[n].shape[-1] not in widths:
            widths.append(P[n].shape[-1])
    layout, conv_at = {}, {}

    def pack(width):
        def stack_layers(n):
            return jnp.stack([small_g[l][n] for l in range(L)], axis=0)

        parts, r0 = [], 0
        for n in small_names:
            if P[n].shape[-1] != width:
                continue
            g = d_final if n == "final_norm" else stack_layers(n).reshape(-1, width)
            layout[n] = (widths.index(width), r0, g.shape[0])
            parts.append(_pad_rows(g, -(-g.shape[0] // 8) * 8))
            r0 += parts[-1].shape[0]
        if width == N_DEV * cw:
            conv_g = jnp.concatenate([stack_layers("sconv_w"), stack_layers("cconv_w")], axis=1)
            conv_g = conv_g.reshape(L * (SCONV_K + CCONV_K), N_DEV * cw)
            conv_at.update(b=widths.index(width), r0=r0, rows=conv_g.shape[0])
            parts.append(_pad_rows(conv_g, -(-conv_g.shape[0] // 8) * 8))
        return jnp.concatenate(parts, axis=0)

    gathered_small = [None] * len(widths)
    for l in reversed(range(L)):
        sv = saved[l]
        sg = small_g[l]
        for tag in ("ffn2", None, "ffn1"):
            if tag is None:
                keys, comm = rest_of_sends()
                (dup, dproj, dy), extra = _merge_bwd(dx, sv["y"], sv["proj"], W["w_up", l], W["w_out", l],
                                                     "merge_bwd", comm)
                settle(keys, extra)
                g_out, _ = _matmul_tn(sv["merged"][None], dx[None], 1, "w_out_grad")
                g_up, _ = _matmul_tn(sv["y"], dup, 1, "w_up_grad")
                g_out = g_out.reshape(N_DEV, D // N_DEV, D)
                g_up = jnp.transpose(g_up.reshape(4, BW, N_DEV, D // N_DEV), (2, 0, 1, 3)).reshape(
                    N_DEV, 4 * BW, D // N_DEV)
                res, (R["w_out", l], R["w_up", l]) = _mixers_bwd(
                    sv["proj"], dy, dproj, pool_w[l], pool_wt[l], row(pool_scale, l), sconv_full[l], cconv_full[l],
                    row(cconv_ln_g, l), row(cconv_ln_b, l), row(sgu_ln_g, l), row(sgu_ln_b, l), sgu_w[l], sgu_wt[l],
                    sgu_bias[l], "mixers_bwd", _scatter_comm([g_out, g_up]))
                dproj = res[0]
                (sg["pool_w"], sg["pool_scale"], sg["sconv_w"], sg["cconv_w"], sg["cconv_ln_g"], sg["cconv_ln_b"],
                 sg["sgu_ln_g"], sg["sgu_ln_b"], sg["sgu_w"], dgb) = res[1:]
                sg["sgu_b"] = dgb[:, :, 0]
                comm = None
                if l == 0:
                    early = [w for w in widths if w != D]
                    comm = _gather_comm([pack(w) for w in early], [(b, None) for b in range(len(early))])
                g_in, extra = _matmul_tn(sv["h_mix"][None], dproj, N_DEV, "w_in_grad", comm)
                if l == 0:
                    for w, g in zip(early, extra):
                        gathered_small[widths.index(w)] = g
                dx, sg["mix_norm"], (r_in,) = _matmul_nt_normbwd(
                    dproj, W["w_in", l], sv["x_mix"], row(mix_norm, l), dx, "proj_bwd",
                    _scatter_comm([g_in], PEERS_BUT_NEAR_OTHER))
                second.append((("w_in", l), g_in, r_in, PEERS_NEAR_OTHER))
            else:
                keys, comm = rest_of_sends()
                dab, sh, extra = _ffn_bwd_hidden(dx, W[tag + "_w2", l], sv["ab_" + tag], "ffn_hidden_bwd", comm)
                settle(keys, extra)
                g_w2, _ = _matmul_tn(sh[None], dx[None], 1, "ffn_w2_grad")
                g_w2 = g_w2.reshape(N_DEV, F // N_DEV, D)
                g_w13, (R[tag + "_w2", l],) = _matmul_tn(dab, sv["h_" + tag][None], 1, "ffn_w13_grad",
                                                         _scatter_comm([g_w2]), b_shared=True)
                g_w13 = g_w13.reshape(N_DEV, fs, D)
                last = tag == "ffn1" and l == 0
                now, later = (PEERS_BUT_NEAR_OTHER, PEERS_NEAR_OTHER) if last else (PEERS_SAME_CORE, PEERS_OTHER_CORE)
                dx, sg[tag + "_norm"], (r_w13,) = _matmul_nt_normbwd(
                    dab, W[tag + "_w13", l], sv["x_" + tag], row(P[tag + "_norm"], l), dx, "ffn_up_bwd",
                    _scatter_comm([g_w13], now), w_t=True)
                second.append(((tag + "_w13", l), g_w13, r_w13, later))
    grad_x = dx[None]
    out = {}

    def as2d(n, a):
        if n == "final_norm":
            return a.reshape(1, D)
        return a.reshape(-1, a.shape[-1])

    for i, n in enumerate(["ffn2_w13", "w_in", "ffn2_w2", "w_up", "w_out", "ffn1_w2", "ffn1_w13"]):
        shp = P[n].shape
        if n.endswith("w13"):
            flat, back = (lambda a: jnp.swapaxes(a, 1, 2)), (lambda a: jnp.swapaxes(a, 1, 2))
        else:
            rows, cols = math.prod(shp[1:-1]), shp[-1]
            flat, back = (lambda a: a.reshape(L, rows, cols)), (lambda a: a.reshape(shp))
        keys, comm = [], None
        if i == 0:
            keys, comm = rest_of_sends()
        elif i == 1:
            comm = _gather_comm([pack(D)], [(0, None)])
        res, extra = _adamw_sharded([R[n, l] for l in range(L)], flat(P[n]), flat(P["m_" + n]), flat(P["v_" + n]),
                                    "adamw_sharded", comm)
        if i == 0:
            settle(keys, extra)
        elif i == 1:
            gathered_small[widths.index(D)] = extra[0]
        out[n] = tuple(back(a) for a in res)

    res = _adamw_replicated(gathered_small, [layout[n] for n in small_names],
                            [(as2d(n, P[n]), as2d(n, P["m_" + n]), as2d(n, P["v_" + n])) for n in small_names],
                            "adamw_replicated")
    for p, n in enumerate(small_names):
        out[n] = tuple(a.reshape(P[n].shape) for a in res[4 * p:4 * p + 4])
    conv_sum = res[4 * len(small_names) + conv_at["b"]][conv_at["r0"]:conv_at["r0"] + conv_at["rows"]]
    conv_mine = lax.dynamic_slice_in_dim(conv_sum, me * cw, cw, axis=1)

    def conv2d(a, b):
        return jnp.concatenate([a, b], axis=1).reshape(L * (SCONV_K + CCONV_K), cw)

    cd, cm, cv = _adamw_small(conv_mine, conv2d(sconv_w, cconv_w), conv2d(m_sconv_w, m_cconv_w),
                              conv2d(v_sconv_w, v_cconv_w), "adamw_conv")
    for n, sl in (("sconv_w", slice(0, SCONV_K)), ("cconv_w", slice(SCONV_K, SCONV_K + CCONV_K))):
        out[n] = tuple(a.reshape(L, SCONV_K + CCONV_K, cw)[:, sl] for a in (conv_mine, cd, cm, cv))

    order = ["ffn1_norm", "ffn1_w13", "ffn1_w2", "mix_norm", "w_in", "pool_w", "pool_scale", "sconv_w", "cconv_w",
             "cconv_ln_g", "cconv_ln_b", "sgu_ln_g", "sgu_ln_b", "sgu_w", "sgu_b", "w_up", "w_out", "ffn2_norm",
             "ffn2_w13", "ffn2_w2", "final_norm"]
    return (loss, grad_x, *[out[n][0] for n in order], *[out[n][1] for n in order],
            *[out[n][2] for n in order], *[out[n][3] for n in order])
```

```python
import functools
import math

import jax
import jax.numpy as jnp
from jax import lax
from jax.experimental import pallas as pl
from jax.experimental.pallas import tpu as pltpu

F32 = jnp.float32
BF16 = jnp.bfloat16
EPS = 1e-6
ADAM_LR = 0.001
ADAM_B1 = 0.9
ADAM_B2 = 0.999
ADAM_EPS = 1e-08
ADAM_WD = 0.01
ADAM_STEP = 10
SGU_BLOCK = 128
SGU_CHUNK = 64
SCONV_K = 3
CCONV_K = 31
HALO = 32
V7X_VMEM_LIMIT = 48 * 1024 * 1024
MESH_AXES = ("x", "y", "c")
N_DEV = 8
_GELU_C0 = math.sqrt(2.0 / math.pi)
_GELU_C1 = 0.044715

BS = pl.BlockSpec
SDS = jax.ShapeDtypeStruct
ANY = pl.BlockSpec(memory_space=pl.ANY)


def _tile(n, pref, align=128):
    if n <= pref:
        return n
    t = pref - pref % align
    while t > 0:
        if n % t == 0:
            return t
        t -= align
    return n


def _sig(v):
    return 1.0 / (1.0 + jnp.exp(-v))


def _gelu(v):
    t = jnp.tanh(_GELU_C0 * (v + _GELU_C1 * (v * v * v)))
    return 0.5 * v * (1.0 + t), t


def _gelu_grad(v, t):
    return 0.5 * (1.0 + t) + 0.5 * v * (1.0 - t * t) * (_GELU_C0 * (1.0 + 3.0 * _GELU_C1 * v * v))


def _ln_stats(v):
    mu = jnp.mean(v, axis=-1, keepdims=True)
    vc = v - mu
    var = jnp.mean(vc * vc, axis=-1, keepdims=True)
    rstd = lax.rsqrt(var + EPS)
    return vc * rstd, rstd


def _ln_bwd(dvh, vh, rstd):
    return rstd * (dvh - jnp.mean(dvh, axis=-1, keepdims=True) - vh * jnp.mean(dvh * vh, axis=-1, keepdims=True))


def _dot(a, b):
    return jnp.dot(a, b, preferred_element_type=F32)


def _dot_nt(a, b):
    return lax.dot_general(a, b, (((1,), (1,)), ((), ())), preferred_element_type=F32)


def _dot_tn(a, b):
    return lax.dot_general(a, b, (((0,), (0,)), ((), ())), preferred_element_type=F32)


def _mesh_pos():
    return lax.axis_index("x"), lax.axis_index("y"), lax.axis_index("c")


class _Comm:
    def __init__(self, ins, out_shape, sems, start, finish, aliases=None, middle=None):
        self.ins, self.out_shape, self.sems, self.start, self.finish = ins, out_shape, sems, start, finish
        self.aliases = aliases or {}
        self.middle = middle


def _gather_comm(shards, units):
    n_u = len(units)
    out_shape = []
    for t, l in units:
        shp = shards[t].shape if l is None else shards[t].shape[1:]
        out_shape.append(SDS((N_DEV,) + tuple(shp), shards[t].dtype))

    def upper_rows(o):
        shp = out_shape[o].shape[1:]
        assert shp[0] >= 2
        return shp[0] // 2 if len(shp) > 2 or shp[0] < 32 else shp[0] // 32 * 16

    def tools(ins, dsts, sems):
        send_sems, recv_sems, local_sems = sems
        x, y, c = _mesh_pos()
        me, sib = (x, y, c), (x, y, 1 - c)
        xn, yn, dg = (1 - x, y, c), (x, 1 - y, c), (1 - x, 1 - y, c)

        def src_of(o):
            t, l = units[o]
            return ins[t] if l is None else ins[t].at[l]

        def row(o, p, part=None):
            r = dsts[o].at[4 * p[0] + 2 * p[1] + p[2]]
            if part is None:
                return r
            h = upper_rows(o)
            return r.at[pl.ds(0, h)] if part == "upper" else r.at[pl.ds(h, out_shape[o].shape[1] - h)]

        def copy(o, k, src, dst, to):
            return pltpu.make_async_remote_copy(
                src_ref=src, dst_ref=dst, send_sem=send_sems.at[o * 8 + k], recv_sem=recv_sems.at[o * 8 + k],
                device_id=to, device_id_type=pl.DeviceIdType.MESH)

        def send(o, k):
            if k < 3:
                return copy(o, k, src_of(o), row(o, me), (sib, xn, yn)[k])
            if k == 3:
                return copy(o, k, row(o, xn, "upper"), row(o, xn, "upper"), yn)
            if k == 4:
                return copy(o, k, row(o, yn, "lower"), row(o, yn, "lower"), xn)
            blk = (xn, yn, dg)[k - 5]
            return copy(o, k, row(o, blk), row(o, blk), sib)

        def landed(o, k):
            def other(p):
                return (p[0], p[1], 1 - c)

            dst = (row(o, sib), row(o, xn), row(o, yn), row(o, dg, "upper"), row(o, dg, "lower"),
                   row(o, other(xn)), row(o, other(yn)), row(o, other(dg)))[k]
            return copy(o, k, dst, dst, me)

        def local(o):
            return pltpu.make_async_copy(src_of(o), row(o, me), local_sems.at[o])

        return send, landed, local

    def start(ins, dsts, sems):
        send, _, local = tools(ins, dsts, sems)
        for o in range(n_u):
            local(o).start()
            for k in (1, 2, 0):
                send(o, k).start()

    def middle(ins, dsts, sems):
        send, landed, _ = tools(ins, dsts, sems)
        for o in range(n_u):
            landed(o, 1).wait_recv()
            send(o, 3).start()
            landed(o, 2).wait_recv()
            send(o, 4).start()
            send(o, 5).start()
            send(o, 6).start()

    def finish(ins, dsts, sems):
        send, landed, local = tools(ins, dsts, sems)
        for o in range(n_u):
            landed(o, 3).wait_recv()
            landed(o, 4).wait_recv()
            send(o, 7).start()
        for o in range(n_u):
            for k in (0, 5, 6, 7):
                landed(o, k).wait_recv()
        for o in range(n_u):
            for k in range(8):
                send(o, k).wait_send()
            local(o).wait()

    sems = [pltpu.SemaphoreType.DMA((8 * n_u,)), pltpu.SemaphoreType.DMA((8 * n_u,)), pltpu.SemaphoreType.DMA((n_u,))]
    return _Comm(list(shards), out_shape, sems, start, finish, middle=middle)


PEERS_ALL = (1, 2, 3, 4, 5, 6, 7)
PEERS_SAME_CORE = (1, 2, 4, 6)
PEERS_OTHER_CORE = (3, 5, 7)
PEERS_BUT_NEAR_OTHER = (1, 2, 4, 6, 7)
PEERS_NEAR_OTHER = (3, 5)


def _scatter_comm(parts, peers=PEERS_ALL, into=None):
    n_u = len(parts)

    def tools(ins, dsts, sems):
        send_sems, recv_sems, local_sems = sems
        x, y, c = _mesh_pos()
        me = 4 * x + 2 * y + c

        def peer(k):
            return ((x + ((k >> 2) & 1)) % 2, (y + ((k >> 1) & 1)) % 2, (c + (k & 1)) % 2)

        def copy(u, k, wait=False):
            p = peer(k)
            pi = 4 * p[0] + 2 * p[1] + p[2]
            return pltpu.make_async_remote_copy(
                src_ref=ins[u].at[pi], dst_ref=dsts[u].at[pi if wait else me],
                send_sem=send_sems.at[u * 7 + k - 1], recv_sem=recv_sems.at[u * 7 + k - 1],
                device_id=p, device_id_type=pl.DeviceIdType.MESH)

        def local(u):
            return pltpu.make_async_copy(ins[u].at[me], dsts[u].at[me], local_sems.at[u])

        return copy, local

    def start(ins, dsts, sems):
        copy, local = tools(ins, dsts, sems)
        for u in range(n_u):
            if into is None:
                local(u).start()
            for k in peers:
                copy(u, k).start()

    def finish(ins, dsts, sems):
        copy, local = tools(ins, dsts, sems)
        for u in range(n_u):
            for k in peers:
                copy(u, k, wait=True).wait()
            if into is None:
                local(u).wait()

    sems = [pltpu.SemaphoreType.DMA((7 * n_u,)), pltpu.SemaphoreType.DMA((7 * n_u,)), pltpu.SemaphoreType.DMA((n_u,))]
    aliases = {} if into is None else {n_u + u: u for u in range(n_u)}
    return _Comm(list(parts) + list(into or []), [SDS(p.shape, p.dtype) for p in parts], sems, start, finish, aliases)


def _run_comm(comm, name):
    n_i, n_o = len(comm.ins), len(comm.out_shape)

    def body(*refs):
        comm.start(refs[:n_i], refs[n_i:n_i + n_o], refs[n_i + n_o:])
        if comm.middle is not None:
            comm.middle(refs[:n_i], refs[n_i:n_i + n_o], refs[n_i + n_o:])
        comm.finish(refs[:n_i], refs[n_i:n_i + n_o], refs[n_i + n_o:])

    return pl.pallas_call(
        body, name=name, out_shape=comm.out_shape, in_specs=[ANY] * n_i, out_specs=[ANY] * n_o,
        scratch_shapes=comm.sems,
    )(*comm.ins)


def _pcall(body, name, grid, in_specs, out_specs, out_shape, scratch, sem, args, comm=None, aliases=None):
    n_i, n_o, n_s = len(in_specs), len(out_specs), len(scratch)
    aliases = aliases or {}
    if comm is None:
        res = pl.pallas_call(
            body, name=name, grid=grid, in_specs=in_specs, out_specs=out_specs, out_shape=out_shape,
            scratch_shapes=scratch, input_output_aliases=aliases,
            compiler_params=pltpu.CompilerParams(dimension_semantics=sem, vmem_limit_bytes=V7X_VMEM_LIMIT),
        )(*args)
        return res, []
    n_ci, n_co = len(comm.ins), len(comm.out_shape)

    def wrapped(*refs):
        ins = refs[:n_i]
        cins = refs[n_i:n_i + n_ci]
        outs = refs[n_i + n_ci:n_i + n_ci + n_o]
        couts = refs[n_i + n_ci + n_o:n_i + n_ci + n_o + n_co]
        rest = refs[n_i + n_ci + n_o + n_co:]
        step = 0
        for d, g in enumerate(grid):
            step = step * g + pl.program_id(d)
        n_steps = math.prod(grid)
        mid = (n_steps * 5) // 8
        staged = comm.middle is not None and 0 < mid < n_steps - 1

        @pl.when(step == 0)
        def _():
            comm.start(cins, couts, rest[n_s:])

        if staged:
            @pl.when(step == mid)
            def _():
                comm.middle(cins, couts, rest[n_s:])

        body(*ins, *outs, *rest[:n_s])

        @pl.when(step == n_steps - 1)
        def _():
            if comm.middle is not None and not staged:
                comm.middle(cins, couts, rest[n_s:])
            comm.finish(cins, couts, rest[n_s:])

    res = pl.pallas_call(
        wrapped, name=name, grid=grid, in_specs=list(in_specs) + [ANY] * n_ci,
        out_specs=list(out_specs) + [ANY] * n_co, out_shape=list(out_shape) + list(comm.out_shape),
        scratch_shapes=list(scratch) + list(comm.sems),
        input_output_aliases={**aliases, **{n_i + ci: n_o + co for ci, co in comm.aliases.items()}},
        compiler_params=pltpu.CompilerParams(dimension_semantics=("arbitrary",) * len(grid),
                                             vmem_limit_bytes=V7X_VMEM_LIMIT),
    )(*args, *comm.ins)
    return res[:n_o], res[n_o:]


def _rmsnorm_fwd(x, g, name):
    S, D = x.shape
    tm = _tile(S, 512, 16)

    def body(x_ref, g_ref, h_ref):
        h_ref[...] = _rmsnorm_rows(x_ref[...], g_ref[...])

    (h,), _ = _pcall(body, name, (S // tm,), [BS((tm, D), lambda i: (i, 0)), BS((1, D), lambda i: (0, 0))],
                     [BS((tm, D), lambda i: (i, 0))], [SDS((S, D), BF16)], [], ("parallel",), (x, g))
    return h


def _matmul_fwd(a, w, name, comm=None, w_t=False, out_dtype=F32):
    S, K = a.shape
    C = w.shape[0]
    Fc = w.shape[1] if w_t else w.shape[2]
    tn = _tile(Fc, 1408)
    tm = _tile(S, 1024, 16)

    def body(a_ref, w_ref, o_ref):
        p = _dot_nt(a_ref[...], w_ref[0]) if w_t else _dot(a_ref[...], w_ref[0])
        o_ref[0] = p.astype(out_dtype)

    w_spec = BS((1, tn, K), lambda c, n, i: (c, n, 0)) if w_t else BS((1, K, tn), lambda c, n, i: (c, 0, n))
    (o,), extra = _pcall(
        body, name, (C, Fc // tn, S // tm), [BS((tm, K), lambda c, n, i: (i, 0)), w_spec],
        [BS((1, tm, tn), lambda c, n, i: (c, i, n))], [SDS((C, S, Fc), out_dtype)], [],
        ("parallel", "parallel", "parallel"), (a, w), comm)
    return o, extra


def _rmsnorm_rows(xv, g):
    r = lax.rsqrt(jnp.mean(xv * xv, axis=-1, keepdims=True) + EPS)
    return (xv * r * g).astype(BF16)


def _swiglu_down(ab, w2, x, g_next, name, comm=None):
    _, S, F = ab.shape
    D = w2.shape[1]
    tk = _tile(F, 1408)
    tm = _tile(S, 512, 16)
    nk = F // tk

    def body(ab_ref, w_ref, x_ref, g_ref, o_ref, h_ref):
        k = pl.program_id(1)
        a = ab_ref[0].astype(F32)
        s = a * _sig(a) * ab_ref[1].astype(F32)
        p = 0.5 * _dot(s.astype(BF16), w_ref[...])

        @pl.when(k == 0)
        def _():
            o_ref[...] = x_ref[...] + p

        @pl.when(k > 0)
        def _():
            o_ref[...] += p

        @pl.when(k == nk - 1)
        def _():
            h_ref[...] = _rmsnorm_rows(o_ref[...], g_ref[...])

    (o, h), extra = _pcall(
        body, name, (S // tm, nk),
        [BS((2, tm, tk), lambda i, k: (0, i, k)), BS((tk, D), lambda i, k: (k, 0)), BS((tm, D), lambda i, k: (i, 0)),
         BS((1, D), lambda i, k: (0, 0))],
        [BS((tm, D), lambda i, k: (i, 0)), BS((tm, D), lambda i, k: (i, 0))],
        [SDS((S, D), F32), SDS((S, D), BF16)], [], ("parallel", "arbitrary"), (ab, w2, x, g_next), comm)
    return o, h, extra


def _ffn_bwd_hidden(dy, w2, ab, name, comm=None):
    S, D = dy.shape
    F = w2.shape[0]
    tk = _tile(F, 1408)
    tm = _tile(S, 512, 16)
    te = _tile(tm, 128, 16)

    def body(dy_ref, w_ref, ab_ref, dab_ref, s_ref, ds_ref):
        ds_ref[...] = 0.5 * _dot_nt(dy_ref[...].astype(BF16), w_ref[...])
        for r0 in range(0, tm, te):
            rows = slice(r0, r0 + te)
            ds = ds_ref[rows, :]
            a = ab_ref[0, rows, :].astype(F32)
            b = ab_ref[1, rows, :].astype(F32)
            sg = _sig(a)
            sa = a * sg
            dab_ref[0, rows, :] = (ds * b * (sg * (1.0 + a * (1.0 - sg)))).astype(BF16)
            dab_ref[1, rows, :] = (ds * sa).astype(BF16)
            s_ref[rows, :] = (0.5 * (sa * b)).astype(BF16)

    (dab, sh), extra = _pcall(
        body, name, (F // tk, S // tm),
        [BS((tm, D), lambda k, i: (i, 0)), BS((tk, D), lambda k, i: (k, 0)), BS((2, tm, tk), lambda k, i: (0, i, k))],
        [BS((2, tm, tk), lambda k, i: (0, i, k)), BS((tm, tk), lambda k, i: (i, k))],
        [SDS((2, S, F), BF16), SDS((S, F), BF16)], [pltpu.VMEM((tm, tk), F32)], ("parallel", "parallel"),
        (dy, w2, ab), comm)
    return dab, sh, extra


def _matmul_tn(a, b, n_c, name, comm=None, b_shared=False):
    G, S, M = a.shape
    _, _, Fc = b.shape
    C = n_c
    tM = _tile(M, 1408)
    tn = _tile(Fc, 1408)
    ts = _tile(S, 1024, 16)
    n_s = S // ts

    def body(a_ref, b_ref, o_ref, acc):
        s = pl.program_id(4)
        p = _dot_tn(a_ref[0].astype(BF16), b_ref[0].astype(BF16))

        @pl.when(s == 0)
        def _():
            acc[...] = p

        @pl.when(s > 0)
        def _():
            acc[...] += p

        @pl.when(s == n_s - 1)
        def _():
            o_ref[0] = acc[...].astype(BF16)

    (o,), extra = _pcall(
        body, name, (G, M // tM, C, Fc // tn, n_s),
        [BS((1, ts, tM), lambda g, m, c, n, s: (g, s, m)), BS((1, ts, tn), lambda g, m, c, n, s: (c if b_shared else g * C + c, s, n))],
        [BS((1, tM, tn), lambda g, m, c, n, s: (g * C + c, m, n))], [SDS((G * C, M, Fc), BF16)],
        [pltpu.VMEM((tM, tn), F32)], ("parallel", "parallel", "parallel", "parallel", "arbitrary"), (a, b), comm)
    return o, extra


def _matmul_nt_normbwd(b, w, x, gam, dres, name, comm=None, w_t=False):
    C, S, Fc = b.shape
    D = w.shape[2] if w_t else w.shape[1]
    tk = _tile(Fc, 1408)
    tm = _tile(S, 1024, 16)
    te = _tile(tm, 256, 8)
    nk = Fc // tk

    def body(b_ref, w_ref, x_ref, g_ref, r_ref, dx_ref, dg_ref):
        i, c, k = pl.program_id(0), pl.program_id(1), pl.program_id(2)
        p = _dot(b_ref[0], w_ref[0]) if w_t else _dot_nt(b_ref[0], w_ref[0])
        first = jnp.logical_and(c == 0, k == 0)

        @pl.when(first)
        def _():
            dx_ref[...] = p

        @pl.when(jnp.logical_not(first))
        def _():
            dx_ref[...] += p

        @pl.when(jnp.logical_and(c == C - 1, k == nk - 1))
        def _():
            dgp = None
            for r0 in range(0, tm, te):
                rows = slice(r0, r0 + te)
                xv = x_ref[rows, :]
                r = lax.rsqrt(jnp.mean(xv * xv, axis=-1, keepdims=True) + EPS)
                xn = xv * r
                dh = dx_ref[rows, :]
                dxn = dh * g_ref[...]
                dx_ref[rows, :] = r_ref[rows, :] + r * (dxn - xn * jnp.mean(dxn * xn, axis=-1, keepdims=True))
                t = jnp.sum(dh * xn, axis=0, keepdims=True)
                dgp = t if dgp is None else dgp + t

            @pl.when(i == 0)
            def _():
                dg_ref[...] = dgp

            @pl.when(i > 0)
            def _():
                dg_ref[...] += dgp

    once = dict(pipeline_mode=pl.Buffered(1))
    (dx, dg), extra = _pcall(
        body, name, (S // tm, C, nk),
        [BS((1, tm, tk), lambda i, c, k: (c, i, k)),
         BS((1, tk, D), lambda i, c, k: (c, k, 0)) if w_t else BS((1, D, tk), lambda i, c, k: (c, 0, k)),
         BS((tm, D), lambda i, c, k: (i, 0), **once), BS((1, D), lambda i, c, k: (0, 0)),
         BS((tm, D), lambda i, c, k: (i, 0), **once)],
        [BS((tm, D), lambda i, c, k: (i, 0)), BS((1, D), lambda i, c, k: (0, 0))],
        [SDS((S, D), F32), SDS((1, D), F32)], [],
        ("arbitrary", "arbitrary", "arbitrary"), (b, w, x, gam, dres), comm)
    return dx, dg, extra


def _final_loss(x, gam, target, name):
    S, D = x.shape
    tm = _tile(S, 512, 8)

    def body(x_ref, g_ref, t_ref, loss_ref, dx_ref, dg_ref):
        i = pl.program_id(0)
        xv = x_ref[...]
        r = lax.rsqrt(jnp.mean(xv * xv, axis=-1, keepdims=True) + EPS)
        xn = xv * r
        err = xn * g_ref[...] - t_ref[...]
        part = 0.5 * jnp.sum(jnp.mean(err * err, axis=-1, keepdims=True), axis=0, keepdims=True)
        dy = err * (1.0 / D)
        dxn = dy * g_ref[...]
        dx_ref[...] = r * (dxn - xn * jnp.mean(dxn * xn, axis=-1, keepdims=True))
        dgp = jnp.sum(dy * xn, axis=0, keepdims=True)
        lp = jnp.broadcast_to(part, loss_ref.shape)

        @pl.when(i == 0)
        def _():
            dg_ref[...] = dgp
            loss_ref[...] = lp

        @pl.when(i > 0)
        def _():
            dg_ref[...] += dgp
            loss_ref[...] += lp

    res, _ = _pcall(
        body, name, (S // tm,),
        [BS((tm, D), lambda i: (i, 0)), BS((1, D), lambda i: (0, 0)), BS((tm, D), lambda i: (i, 0))],
        [BS((8, 128), lambda i: (0, 0)), BS((tm, D), lambda i: (i, 0)), BS((1, D), lambda i: (0, 0))],
        [SDS((8, 128), F32), SDS((S, D), F32), SDS((1, D), F32)], [], ("arbitrary",), (x, gam, target))
    return res


CONV_CHUNK = 32


def _fill_shifted(rot, n):
    for b in range(1, 8):
        rot[b, 0:n - 8, :] = rot[0, b:b + n - 8, :]


def _window(rot, off, r0, rows):
    b = off % 8
    return rot[b, off - b + r0:off - b + r0 + rows, :]


def _taps(rot, w_ref, offs, n_rows, out):
    for r0 in range(0, n_rows, CONV_CHUNK):
        acc = None
        for k, off in enumerate(offs):
            t = w_ref[k:k + 1, :] * _window(rot, off, r0, CONV_CHUNK)
            acc = t if acc is None else acc + t
        out[r0:r0 + CONV_CHUNK, :] = acc


def _tap_grads(rot, offs, g_plane, n_rows, dw_ref):
    for k, off in enumerate(offs):
        acc = None
        for r0 in range(0, n_rows, CONV_CHUNK):
            p = g_plane[0, r0:r0 + CONV_CHUNK, :] * _window(rot, off, r0, CONV_CHUNK)
            acc = p if acc is None else acc + p
        dw_ref[k:k + 1, :] += jnp.sum(acc, axis=0, keepdims=True)


def _sgu_masks():
    ii = lax.broadcasted_iota(jnp.int32, (SGU_BLOCK, SGU_BLOCK), 0) // SGU_CHUNK
    jj = lax.broadcasted_iota(jnp.int32, (SGU_BLOCK, SGU_BLOCK), 1) // SGU_CHUNK
    return jj <= ii, ii <= jj


def _mixers_fwd(proj, pool_w, pool_scale, sconv_w, cconv_w, cln_g, cln_b, sln_g, sln_b, sgu_w, sgu_bias, name,
                comm=None):
    _, S, D = proj.shape
    BW = D // 2
    GW = BW // 4
    TS = _tile(S, 256, SGU_BLOCK)
    H = HALO
    hb = TS // H

    def main(blk, col):
        return BS((1, TS, BW), lambda i: (blk, i, col))

    def back(blk, col):
        return BS((1, H, BW), lambda i: (blk, jnp.maximum(i * hb - 1, 0), col))

    def full(a):
        nd = a.ndim
        return BS(a.shape, lambda i: (0,) * nd)

    def body(pa_m, pa_b, xi_m, xi_b, bg_m, cg_m, cg_b, ca_m, ca_b, cb_m, cb_b, du_m, dv_m,
             pw, ps, sw, cw, clg, clb, slg, slb, gw, gbias, y_ref, y1_ref, e1, e2, e3):
        i = pl.program_id(0)
        nb = jnp.where(i > 0, 1.0, 0.0).astype(F32)
        rows = i * TS + lax.broadcasted_iota(jnp.int32, (TS, 1), 0)

        e1[0:H, :] = pa_b[0] * nb
        e1[H:H + TS, :] = pa_m[0]
        for g in range(4):
            cols = slice(g * GW, (g + 1) * GW)
            win = 2 << g
            wsum = e1[H:H + TS, cols]
            for k in range(1, win):
                wsum = wsum + e1[H - k:H - k + TS, cols]
            cnt = jnp.minimum(rows + 1, win).astype(F32)
            d = wsum / cnt - e1[H:H + TS, cols]
            yg = _dot(d.astype(BF16), pw[g].astype(BF16)) * ps[:, cols]
            y_ref[0, :, cols] = yg.astype(BF16)

        e2[0:H, :] = cg_b[0] * xi_b[0] * nb
        e2[H:H + TS, :] = cg_m[0] * xi_m[0]
        cz = sw[0:1, :] * e2[H - 2:H - 2 + TS, :]
        for k in range(1, SCONV_K):
            cz = cz + sw[k:k + 1, :] * e2[H - 2 + k:H - 2 + k + TS, :]
        y_ref[1] = (bg_m[0] * cz).astype(BF16)

        e3[0, 0:H, :] = ca_b[0] * _sig(cb_b[0]) * nb
        e3[0, H:H + TS, :] = ca_m[0] * _sig(cb_m[0])
        _fill_shifted(e3, H + TS)
        _taps(e3, cw, [H - (CCONV_K - 1) + k for k in range(CCONV_K)], TS, y1_ref)
        yh, _ = _ln_stats(y1_ref[...])
        y2 = yh * clg[...] + clb[...]
        y_ref[2] = (y2 * _sig(y2)).astype(BF16)

        u, _ = _gelu(du_m[0])
        v, _ = _gelu(dv_m[0])
        vh, _ = _ln_stats(v)
        vn = vh * slg[...] + slb[...]
        mask, _ = _sgu_masks()
        for h in range(4):
            wm = jnp.where(mask, gw[h], 0.0).astype(BF16)
            cs = slice(h * GW, (h + 1) * GW)
            for n in range(TS // SGU_BLOCK):
                rs = slice(n * SGU_BLOCK, (n + 1) * SGU_BLOCK)
                z = _dot(wm, vn[rs, cs].astype(BF16)) + gbias[h]
                y_ref[3, rs, cs] = (u[rs, cs] * z).astype(BF16)

    args = [proj] * 13 + [pool_w, pool_scale, sconv_w, cconv_w, cln_g, cln_b, sln_g, sln_b, sgu_w, sgu_bias]
    in_specs = [main(0, 0), back(0, 0), main(0, 1), back(0, 1), main(1, 0), main(1, 1), back(1, 1),
                main(2, 0), back(2, 0), main(2, 1), back(2, 1), main(3, 0), main(3, 1)]
    in_specs += [full(a) for a in args[13:]]
    (y, y1), extra = _pcall(body, name, (S // TS,), in_specs,
                            [BS((4, TS, BW), lambda i: (0, i, 0)), BS((TS, BW), lambda i: (i, 0))],
                            [SDS((4, S, BW), BF16), SDS((S, BW), F32)],
                            [pltpu.VMEM((H + TS, BW), F32)] * 2 + [pltpu.VMEM((8, H + TS, BW), F32)], ("parallel",),
                            args, comm)
    return y, y1, extra


def _mixers_bwd(proj, y1, dy, dproj_gates, pool_w, pool_wt, pool_scale, sconv_w, cconv_w, cln_g, cln_b, sln_g, sln_b,
                sgu_w, sgu_wt, sgu_bias, name, comm=None):
    _, S, D = proj.shape
    BW = D // 2
    GW = BW // 4
    TS = _tile(S, 256, SGU_BLOCK)
    H = HALO
    hb = TS // H
    n_t = S // TS
    E = TS + H

    def main(blk, col):
        return BS((1, TS, BW), lambda i: (blk, i, col))

    def back(blk, col):
        return BS((1, H, BW), lambda i: (blk, jnp.maximum(i * hb - 1, 0), col))

    def front(blk, col):
        return BS((1, H, BW), lambda i: (blk, jnp.minimum((i + 1) * hb, S // H - 1), col))

    def full(a):
        nd = a.ndim
        return BS(a.shape, lambda i: (0,) * nd)

    def body(pa_b, pa_m, xi_b, xi_m, bg_m, bg_f, cg_b, cg_m, ca_b, ca_m, cb_b, cb_m, du_m, dv_m, y1_m, y1_f,
             dya_m, dya_f, dyb_m, dyb_f, dyc_m, dyc_f, dyd_m,
             pw, pwt, ps, sw, cw, clg, clb, slg, slb, gw, gwt, gbias, _gates_in,
             dp_ref, dpw, dps, dsw, dcw, dclg, dclb, dslg, dslb, dgw, dgb,
             e1, e2, e3, e4, e5, ra, rb):
        i = pl.program_id(0)
        nb = jnp.where(i > 0, 1.0, 0.0).astype(F32)
        nf = jnp.where(i < n_t - 1, 1.0, 0.0).astype(F32)
        rows_m = i * TS + lax.broadcasted_iota(jnp.int32, (TS, 1), 0)
        rows_e = i * TS + lax.broadcasted_iota(jnp.int32, (E, 1), 0)

        @pl.when(i == 0)
        def _():
            for r in (dpw, dps, dsw, dcw, dclg, dclb, dslg, dslb, dgw, dgb):
                r[...] = jnp.zeros(r.shape, F32)

        e1[0:H, :] = pa_b[0] * nb
        e1[H:H + TS, :] = pa_m[0]
        e2[0:TS, :] = dya_m[0] * ps[...]
        e2[TS:E, :] = dya_f[0] * ps[...] * nf
        for g in range(4):
            cols = slice(g * GW, (g + 1) * GW)
            win = 2 << g
            a_m = e1[H:H + TS, cols]
            wsum = a_m
            for k in range(1, win):
                wsum = wsum + e1[H - k:H - k + TS, cols]
            d = wsum / jnp.minimum(rows_m + 1, win).astype(F32) - a_m
            d16 = d.astype(BF16)
            dyp = e2[0:E, cols].astype(BF16)
            dd = _dot(dyp, pwt[g].astype(BF16))
            e3[0:E, cols] = dd / jnp.minimum(rows_e + 1, win).astype(F32)
            da = e3[0:TS, cols] - dd[0:TS]
            for k in range(1, win):
                da = da + e3[k:k + TS, cols]
            dp_ref[0, :, cols] = da.astype(BF16)
            ypre = _dot(d16, pw[g].astype(BF16))
            dps[:, cols] += jnp.sum(dya_m[0][:, cols] * ypre, axis=0, keepdims=True)
            dpw[g] += _dot(jnp.transpose(d).astype(BF16), dyp[0:TS])

        e4[0:H, :] = cg_b[0] * xi_b[0] * nb
        e4[H:H + TS, :] = cg_m[0] * xi_m[0]
        dyb = dyb_m[0]
        e5[0:TS, :] = dyb * bg_m[0]
        e5[TS:E, :] = dyb_f[0] * bg_f[0] * nf
        dcz = e5[0:TS, :]
        cz = None
        dz = None
        for k in range(SCONV_K):
            zk = e4[H - 2 + k:H - 2 + k + TS, :]
            wk = sw[k:k + 1, :]
            cz = wk * zk if cz is None else cz + wk * zk
            t = wk * e5[2 - k:2 - k + TS, :]
            dz = t if dz is None else dz + t
            dsw[k:k + 1, :] += jnp.sum(dcz * zk, axis=0, keepdims=True)
        dp_ref[0, :, BW:2 * BW] = (dz * cg_m[0]).astype(BF16)
        dp_ref[1, :, 0:BW] = (dyb * cz).astype(BF16)
        dp_ref[1, :, BW:2 * BW] = (dz * xi_m[0]).astype(BF16)

        sgm = _sig(cb_m[0])
        ra[0, 0:H, :] = ca_b[0] * _sig(cb_b[0]) * nb
        ra[0, H:H + TS, :] = ca_m[0] * sgm
        _fill_shifted(ra, H + TS)
        fwd_offs = [H - (CCONV_K - 1) + k for k in range(CCONV_K)]
        e4[0:TS, :] = y1_m[...]
        e4[TS:E, :] = y1_f[...]
        yh, rstd = _ln_stats(e4[0:E, :])
        y2 = yh * clg[...] + clb[...]
        s2 = _sig(y2)
        e1[0:TS, :] = dyc_m[0]
        e1[TS:E, :] = dyc_f[0] * nf
        dy2 = e1[0:E, :] * (s2 * (1.0 + y2 * (1.0 - s2)))
        dclg[...] += jnp.sum((dy2 * yh)[0:TS], axis=0, keepdims=True)
        dclb[...] += jnp.sum(dy2[0:TS], axis=0, keepdims=True)
        rb[0, 0:E, :] = _ln_bwd(dy2 * clg[...], yh, rstd)
        _fill_shifted(rb, E)
        _taps(rb, cw, [CCONV_K - 1 - k for k in range(CCONV_K)], TS, e5)
        _tap_grads(ra, fwd_offs, rb, TS, dcw)
        dy0 = e5[0:TS, :]
        dp_ref[2, :, 0:BW] = (dy0 * sgm).astype(BF16)
        dp_ref[2, :, BW:2 * BW] = (dy0 * ca_m[0] * (sgm * (1.0 - sgm))).astype(BF16)

        pu = du_m[0]
        pv = dv_m[0]
        u, tu = _gelu(pu)
        v, tv = _gelu(pv)
        vh, vr = _ln_stats(v)
        vn = vh * slg[...] + slb[...]
        dyd = dyd_m[0]
        mask, mask_t = _sgu_masks()
        for h in range(4):
            wm = jnp.where(mask, gw[h], 0.0).astype(BF16)
            wmt = jnp.where(mask_t, gwt[h], 0.0).astype(BF16)
            cs = slice(h * GW, (h + 1) * GW)
            for n in range(TS // SGU_BLOCK):
                rs = slice(n * SGU_BLOCK, (n + 1) * SGU_BLOCK)
                vb = vn[rs, cs].astype(BF16)
                z = _dot(wm, vb) + gbias[h]
                dzb = dyd[rs, cs] * u[rs, cs]
                dz16 = dzb.astype(BF16)
                e3[rs, cs] = dyd[rs, cs] * z
                e4[rs, cs] = _dot(wmt, dz16)
                dgw[h] += jnp.where(mask, _dot_nt(dz16, vb), 0.0)
                dgb[h] += dzb
        dvn = e4[0:TS, :]
        dslg[...] += jnp.sum(dvn * vh, axis=0, keepdims=True)
        dslb[...] += jnp.sum(dvn, axis=0, keepdims=True)
        dv = _ln_bwd(dvn * slg[...], vh, vr)
        dp_ref[3, :, 0:BW] = (e3[0:TS, :] * _gelu_grad(pu, tu)).astype(BF16)
        dp_ref[3, :, BW:2 * BW] = (dv * _gelu_grad(pv, tv)).astype(BF16)

        @pl.when(i == n_t - 1)
        def _():
            for h in range(4):
                dgb[h] = jnp.broadcast_to(jnp.sum(dgb[h], axis=1, keepdims=True), dgb.shape[1:])

    params = [pool_w, pool_wt, pool_scale, sconv_w, cconv_w, cln_g, cln_b, sln_g, sln_b, sgu_w, sgu_wt, sgu_bias]
    args = [proj] * 14 + [y1] * 2 + [dy] * 7 + params + [dproj_gates]
    in_specs = [back(0, 0), main(0, 0), back(0, 1), main(0, 1), main(1, 0), front(1, 0), back(1, 1), main(1, 1),
                back(2, 0), main(2, 0), back(2, 1), main(2, 1), main(3, 0), main(3, 1),
                BS((TS, BW), lambda i: (i, 0)), BS((H, BW), lambda i: (jnp.minimum((i + 1) * hb, S // H - 1), 0)),
                main(0, 0), front(0, 0), main(1, 0), front(1, 0), main(2, 0), front(2, 0), main(3, 0)]
    in_specs += [full(a) for a in params] + [ANY]
    small = [SDS(pool_w.shape, F32), SDS(pool_scale.shape, F32), SDS(sconv_w.shape, F32), SDS(cconv_w.shape, F32),
             SDS(cln_g.shape, F32), SDS(cln_b.shape, F32), SDS(sln_g.shape, F32), SDS(sln_b.shape, F32),
             SDS(sgu_w.shape, F32), SDS(sgu_bias.shape, F32)]
    out_specs = [BS((4, TS, D), lambda i: (0, i, 0))] + [full(s) for s in small]
    return _pcall(body, name, (n_t,), in_specs, out_specs, [SDS(dproj_gates.shape, BF16)] + small,
                  [pltpu.VMEM((TS + 2 * H, BW), F32)] * 5 + [pltpu.VMEM((8, TS + 2 * H, BW), F32)] * 2,
                  ("arbitrary",), args, comm, aliases={len(args) - 1: 0})


def _merge_fwd(y, proj, w_up, w_out, x, g_next, name, comm=None):
    _, S, BW = y.shape
    D = x.shape[1]
    tm = _tile(S, 256, 16)

    def body(y_ref, pg_ref, wu_ref, wo_ref, x_ref, g_ref, o_ref, m_ref, h_ref):
        merged = None
        for g in range(4):
            t = _sig(pg_ref[g]) * _dot(y_ref[g], wu_ref[g])
            merged = t if merged is None else merged + t
        m16 = merged.astype(BF16)
        m_ref[...] = m16
        xn = x_ref[...] + _dot(m16, wo_ref[...])
        o_ref[...] = xn
        h_ref[...] = _rmsnorm_rows(xn, g_ref[...])

    (o, m, h), extra = _pcall(
        body, name, (S // tm,),
        [BS((4, tm, BW), lambda i: (0, i, 0)), BS((4, tm, D), lambda i: (1, i, 0)),
         BS((4, BW, D), lambda i: (0, 0, 0)), BS((D, D), lambda i: (0, 0)), BS((tm, D), lambda i: (i, 0)),
         BS((1, D), lambda i: (0, 0))],
        [BS((tm, D), lambda i: (i, 0)), BS((tm, D), lambda i: (i, 0)), BS((tm, D), lambda i: (i, 0))],
        [SDS((S, D), F32), SDS((S, D), BF16), SDS((S, D), BF16)], [], ("parallel",),
        (y, proj, w_up, w_out, x, g_next), comm)
    return o, m, h, extra


def _merge_bwd(dx, y, proj, w_up, w_out, name, comm=None):
    _, S, BW = y.shape
    D = dx.shape[1]
    tm = _tile(S, 256, 16)

    def body(dx_ref, y_ref, pg_ref, wu_ref, wo_ref, dup_ref, dp_ref, dy_ref):
        dm = _dot_nt(dx_ref[...].astype(BF16), wo_ref[...])
        for g in range(4):
            gate = _sig(pg_ref[g])
            up = _dot(y_ref[g], wu_ref[g])
            dup = (dm * gate).astype(BF16)
            dup_ref[g] = dup
            dp_ref[g] = (dm * up * (gate * (1.0 - gate))).astype(BF16)
            dy_ref[g] = _dot_nt(dup, wu_ref[g])

    res, extra = _pcall(
        body, name, (S // tm,),
        [BS((tm, D), lambda i: (i, 0)), BS((4, tm, BW), lambda i: (0, i, 0)), BS((4, tm, D), lambda i: (1, i, 0)),
         BS((4, BW, D), lambda i: (0, 0, 0)), BS((D, D), lambda i: (0, 0))],
        [BS((4, tm, D), lambda i: (0, i, 0)), BS((4, tm, D), lambda i: (1, i, 0)), BS((4, tm, BW), lambda i: (0, i, 0))],
        [SDS((4, S, D), BF16), SDS((8, S, D), BF16), SDS((4, S, BW), F32)], [], ("parallel",),
        (dx, y, proj, w_up, w_out), comm)
    return res, extra


def _adamw(w, g, m, v):
    m = ADAM_B1 * m + (1.0 - ADAM_B1) * g
    v = ADAM_B2 * v + (1.0 - ADAM_B2) * (g * g)
    m_hat = m / (1.0 - ADAM_B1 ** ADAM_STEP)
    v_hat = v / (1.0 - ADAM_B2 ** ADAM_STEP)
    delta = -ADAM_LR * (m_hat / (jnp.sqrt(v_hat) + ADAM_EPS) + ADAM_WD * w)
    return delta, m, v


def _adamw_sharded(parts, w, m, v, name, comm=None):
    L, R, C = w.shape
    tr = _tile(R, 256, 16)

    def body(*refs):
        p_refs = refs[:L]
        w_ref, m_ref, v_ref, g_out, d_out, m_out, v_out = refs[L:]
        l = pl.program_id(0)
        g = None
        for d in range(N_DEV):
            t = p_refs[0][d].astype(F32)
            for j in range(1, L):
                t = jnp.where(l == j, p_refs[j][d].astype(F32), t)
            g = t if g is None else g + t
        dl, mn, vn = _adamw(w_ref[0], g, m_ref[0], v_ref[0])
        g_out[0] = g
        d_out[0] = dl
        m_out[0] = mn
        v_out[0] = vn

    def part_spec(j):
        return BS((N_DEV, tr, C), lambda l, r: (0, jnp.where(l == j, r, 0), 0))

    blk = BS((1, tr, C), lambda l, r: (l, r, 0))
    return _pcall(body, name, (L, R // tr), [part_spec(j) for j in range(L)] + [blk, blk, blk], [blk] * 4,
                  [SDS((L, R, C), F32)] * 4, [], ("parallel", "parallel"), (*parts, w, m, v), comm)


def _adamw_replicated(gathered, layout, wmv, name):
    n_b = len(gathered)
    n_p = len(layout)

    def body(*refs):
        bufs = refs[:n_b]
        prm = refs[n_b:n_b + 3 * n_p]
        outs = refs[n_b + 3 * n_p:n_b + 7 * n_p]
        sums = refs[n_b + 7 * n_p:]
        for b in range(n_b):
            s = bufs[b][0]
            for d in range(1, N_DEV):
                s = s + bufs[b][d]
            sums[b][...] = s
        for p, (b, r0, nr) in enumerate(layout):
            g = sums[b][r0:r0 + nr, :]
            d, mn, vn = _adamw(prm[3 * p][...], g, prm[3 * p + 1][...], prm[3 * p + 2][...])
            outs[4 * p][...] = g
            outs[4 * p + 1][...] = d
            outs[4 * p + 2][...] = mn
            outs[4 * p + 3][...] = vn

    flat = [a for t in wmv for a in t]
    out_shape = []
    for (w, _, _) in wmv:
        out_shape += [SDS(w.shape, F32)] * 4
    out_shape += [SDS(g.shape[1:], F32) for g in gathered]
    return pl.pallas_call(
        body, name=name, out_shape=out_shape,
        compiler_params=pltpu.CompilerParams(vmem_limit_bytes=V7X_VMEM_LIMIT),
    )(*gathered, *flat)


def _adamw_small(g, w, m, v, name):
    def body(g_ref, w_ref, m_ref, v_ref, d_out, m_out, v_out):
        d, mn, vn = _adamw(w_ref[...], g_ref[...], m_ref[...], v_ref[...])
        d_out[...] = d
        m_out[...] = mn
        v_out[...] = vn

    return pl.pallas_call(body, name=name, out_shape=[SDS(w.shape, F32)] * 3)(g, w, m, v)


def _pad_rows(a, rows):
    return jnp.pad(a, ((0, rows - a.shape[0]), (0, 0)))


def kernel(x, ffn1_norm, ffn1_w13, ffn1_w2, mix_norm, w_in, pool_w, pool_scale, sconv_w, cconv_w, cconv_ln_g, cconv_ln_b, sgu_ln_g, sgu_ln_b, sgu_w, sgu_b, w_up, w_out, ffn2_norm, ffn2_w13, ffn2_w2, final_norm, loss_target, m_ffn1_norm, m_ffn1_w13, m_ffn1_w2, m_mix_norm, m_w_in, m_pool_w, m_pool_scale, m_sconv_w, m_cconv_w, m_cconv_ln_g, m_cconv_ln_b, m_sgu_ln_g, m_sgu_ln_b, m_sgu_w, m_sgu_b, m_w_up, m_w_out, m_ffn2_norm, m_ffn2_w13, m_ffn2_w2, m_final_norm, v_ffn1_norm, v_ffn1_w13, v_ffn1_w2, v_mix_norm, v_w_in, v_pool_w, v_pool_scale, v_sconv_w, v_cconv_w, v_cconv_ln_g, v_cconv_ln_b, v_sgu_ln_g, v_sgu_ln_b, v_sgu_w, v_sgu_b, v_w_up, v_w_out, v_ffn2_norm, v_ffn2_w13, v_ffn2_w2, v_final_norm):
    P = dict(locals())
    L = ffn1_norm.shape[0]
    S, D = x.shape[1], x.shape[2]
    BW = D // 2
    GW = BW // 4
    F = ffn1_w2.shape[1] * N_DEV
    fs = ffn1_w13.shape[2]
    cw = sconv_w.shape[2]
    me = 4 * lax.axis_index("x") + 2 * lax.axis_index("y") + lax.axis_index("c")

    big = ["ffn1_w13", "ffn1_w2", "w_in", "w_up", "w_out", "ffn2_w13", "ffn2_w2"]
    shards = [(jnp.swapaxes(P[n], 1, 2) if n.endswith("w13") else P[n]).astype(BF16) for n in big]
    conv_local = jnp.concatenate([sconv_w, cconv_w], axis=1)

    def gather_of(units):
        return _gather_comm(shards, [(big.index(n), l) for n, l in units])

    def ready(n, g):
        if n.endswith("w13"):
            return g.reshape(2, F, D)
        if n.endswith("w2"):
            return g.reshape(F, D)
        if n == "w_up":
            return jnp.transpose(g, (1, 2, 0, 3)).reshape(4, BW, D)
        if n == "w_out":
            return g.reshape(D, D)
        return g

    W = {}

    def take(units, arrays):
        for (n, l), g in zip(units, arrays):
            W[n, l] = ready(n, g)

    first_units = [("ffn1_w13", 0), ("ffn1_w2", 0)]
    plan = {("ffn1_up", 0): [("w_in", 0)],
            ("ffn1_down", 0): [("w_up", 0), ("w_out", 0)],
            ("proj", 0): [("ffn2_w13", 0), ("ffn2_w2", 0)],
            ("mixers", 0): [("ffn1_w13", 1)], ("merge", 0): [("ffn1_w2", 1)],
            ("ffn2_up", 0): [("w_in", 1)], ("ffn2_down", 0): [("w_up", 1), ("w_out", 1)],
            ("ffn1_up", 1): [("ffn2_w13", 1)], ("ffn1_down", 1): [("ffn2_w2", 1)]}
    assert L <= 2

    def carried(key):
        units = [u for u in plan.get(key, []) if u[1] < L]
        return units, (gather_of(units) if units else None)

    first = _gather_comm(shards + [conv_local], [(big.index(n), l) for n, l in first_units] + [(len(big), None)])
    got = _run_comm(first, "gather_first_weights")
    take(first_units, got[:2])
    conv_full = jnp.transpose(got[2], (1, 2, 0, 3)).reshape(L, SCONV_K + CCONV_K, N_DEV * cw)
    sconv_full = conv_full[:, :SCONV_K]
    cconv_full = conv_full[:, SCONV_K:]

    sgu_bias = jnp.broadcast_to(sgu_b[:, :, :, None], sgu_b.shape + (GW,))
    pool_wt = jnp.swapaxes(pool_w, 2, 3)
    sgu_wt = jnp.swapaxes(sgu_w, 2, 3)

    def row(a, l):
        return a[l][None, :]

    saved = []
    xc = x[0]
    h = _rmsnorm_fwd(xc, row(ffn1_norm, 0), "first_norm_fwd")
    for l in range(L):
        sv = {}
        for tag in ("ffn1", None, "ffn2"):
            if tag is None:
                sv["x_mix"] = xc
                units, comm = carried(("proj", l))
                proj, extra = _matmul_fwd(h, W["w_in", l], "proj_fwd", comm)
                take(units, extra)
                units, comm = carried(("mixers", l))
                y, sv["y1"], extra = _mixers_fwd(
                    proj, pool_w[l], row(pool_scale, l), sconv_full[l], cconv_full[l], row(cconv_ln_g, l),
                    row(cconv_ln_b, l), row(sgu_ln_g, l), row(sgu_ln_b, l), sgu_w[l], sgu_bias[l], "mixers_fwd", comm)
                take(units, extra)
                units, comm = carried(("merge", l))
                sv.update(h_mix=h, proj=proj, y=y)
                xc, sv["merged"], h, extra = _merge_fwd(y, proj, W["w_up", l], W["w_out", l], xc, row(ffn2_norm, l),
                                                        "merge_fwd", comm)
                take(units, extra)
            else:
                sv["x_" + tag] = xc
                units, comm = carried((tag + "_up", l))
                ab, extra = _matmul_fwd(h, W[tag + "_w13", l], "ffn_up_fwd", comm, w_t=True, out_dtype=BF16)
                take(units, extra)
                units, comm = carried((tag + "_down", l))
                sv.update({"h_" + tag: h, "ab_" + tag: ab})
                if tag == "ffn1":
                    g_next = row(mix_norm, l)
                else:
                    g_next = row(ffn1_norm, l + 1) if l + 1 < L else final_norm[None, :]
                xc, h, extra = _swiglu_down(ab, W[tag + "_w2", l], xc, g_next, "ffn_down_fwd", comm)
                take(units, extra)
        saved.append(sv)

    loss_part, dx, d_final = _final_loss(xc, final_norm[None, :], loss_target[0], "loss_head")
    loss = lax.psum(loss_part[0, 0], MESH_AXES)

    R = {}
    second = []

    def rest_of_sends():
        keys = [e[0] for e in second]
        comm = None
        if second:
            assert len({e[3] for e in second}) == 1
            comm = _scatter_comm([e[1] for e in second], second[0][3], [e[2] for e in second])
        second.clear()
        return keys, comm

    def settle(keys, arrays):
        for k, a in zip(keys, arrays):
            R[k] = a

    wide = ["ffn1_norm", "mix_norm", "ffn2_norm", "final_norm"]
    half = ["pool_scale", "cconv_ln_g", "cconv_ln_b", "sgu_ln_g", "sgu_ln_b"]
    narrow = ["pool_w", "sgu_w", "sgu_b"]
    small_names = wide + half + narrow
    small_g = [dict() for _ in range(L)]
    widths = []
    for n in small_names:
        if P[n].shape[-1] not in widths:
            widths.append(P[n].shape[-1])
    layout, conv_at = {}, {}

    def pack(width):
        def stack_layers(n):
            return jnp.stack([small_g[l][n] for l in range(L)], axis=0)

        parts, r0 = [], 0
        for n in small_names:
            if P[n].shape[-1] != width:
                continue
            g = d_final if n == "final_norm" else stack_layers(n).reshape(-1, width)
            layout[n] = (widths.index(width), r0, g.shape[0])
            parts.append(_pad_rows(g, -(-g.shape[0] // 8) * 8))
            r0 += parts[-1].shape[0]
        if width == N_DEV * cw:
            conv_g = jnp.concatenate([stack_layers("sconv_w"), stack_layers("cconv_w")], axis=1)
            conv_g = conv_g.reshape(L * (SCONV_K + CCONV_K), N_DEV * cw)
            conv_at.update(b=widths.index(width), r0=r0, rows=conv_g.shape[0])
            parts.append(_pad_rows(conv_g, -(-conv_g.shape[0] // 8) * 8))
        return jnp.concatenate(parts, axis=0)

    gathered_small = [None] * len(widths)
    for l in reversed(range(L)):
        sv = saved[l]
        sg = small_g[l]
        for tag in ("ffn2", None, "ffn1"):
            if tag is None:
                keys, comm = rest_of_sends()
                (dup, dproj, dy), extra = _merge_bwd(dx, sv["y"], sv["proj"], W["w_up", l], W["w_out", l],
                                                     "merge_bwd", comm)
                settle(keys, extra)
                g_out, _ = _matmul_tn(sv["merged"][None], dx[None], 1, "w_out_grad")
                g_up, _ = _matmul_tn(sv["y"], dup, 1, "w_up_grad")
                g_out = g_out.reshape(N_DEV, D // N_DEV, D)
                g_up = jnp.transpose(g_up.reshape(4, BW, N_DEV, D // N_DEV), (2, 0, 1, 3)).reshape(
                    N_DEV, 4 * BW, D // N_DEV)
                res, (R["w_out", l], R["w_up", l]) = _mixers_bwd(
                    sv["proj"], sv["y1"], dy, dproj, pool_w[l], pool_wt[l], row(pool_scale, l), sconv_full[l], cconv_full[l],
                    row(cconv_ln_g, l), row(cconv_ln_b, l), row(sgu_ln_g, l), row(sgu_ln_b, l), sgu_w[l], sgu_wt[l],
                    sgu_bias[l], "mixers_bwd", _scatter_comm([g_out, g_up]))
                dproj = res[0]
                (sg["pool_w"], sg["pool_scale"], sg["sconv_w"], sg["cconv_w"], sg["cconv_ln_g"], sg["cconv_ln_b"],
                 sg["sgu_ln_g"], sg["sgu_ln_b"], sg["sgu_w"], dgb) = res[1:]
                sg["sgu_b"] = dgb[:, :, 0]
                comm = None
                if l == 0:
                    early = [w for w in widths if w != D]
                    comm = _gather_comm([pack(w) for w in early], [(b, None) for b in range(len(early))])
                g_in, extra = _matmul_tn(sv["h_mix"][None], dproj, N_DEV, "w_in_grad", comm)
                if l == 0:
                    for w, g in zip(early, extra):
                        gathered_small[widths.index(w)] = g
                dx, sg["mix_norm"], (r_in,) = _matmul_nt_normbwd(
                    dproj, W["w_in", l], sv["x_mix"], row(mix_norm, l), dx, "proj_bwd",
                    _scatter_comm([g_in], PEERS_BUT_NEAR_OTHER))
                second.append((("w_in", l), g_in, r_in, PEERS_NEAR_OTHER))
            else:
                keys, comm = rest_of_sends()
                dab, sh, extra = _ffn_bwd_hidden(dx, W[tag + "_w2", l], sv["ab_" + tag], "ffn_hidden_bwd", comm)
                settle(keys, extra)
                g_w2, _ = _matmul_tn(sh[None], dx[None], 1, "ffn_w2_grad")
                g_w2 = g_w2.reshape(N_DEV, F // N_DEV, D)
                g_w13, (R[tag + "_w2", l],) = _matmul_tn(dab, sv["h_" + tag][None], 1, "ffn_w13_grad",
                                                         _scatter_comm([g_w2]), b_shared=True)
                g_w13 = g_w13.reshape(N_DEV, fs, D)
                last = tag == "ffn1" and l == 0
                now, later = (PEERS_BUT_NEAR_OTHER, PEERS_NEAR_OTHER) if last else (PEERS_SAME_CORE, PEERS_OTHER_CORE)
                dx, sg[tag + "_norm"], (r_w13,) = _matmul_nt_normbwd(
                    dab, W[tag + "_w13", l], sv["x_" + tag], row(P[tag + "_norm"], l), dx, "ffn_up_bwd",
                    _scatter_comm([g_w13], now), w_t=True)
                second.append(((tag + "_w13", l), g_w13, r_w13, later))
    grad_x = dx[None]
    out = {}

    def as2d(n, a):
        if n == "final_norm":
            return a.reshape(1, D)
        return a.reshape(-1, a.shape[-1])

    for i, n in enumerate(["w_out", "w_up", "ffn2_w13", "w_in", "ffn2_w2", "ffn1_w2", "ffn1_w13"]):
        shp = P[n].shape
        if n.endswith("w13"):
            flat, back = (lambda a: jnp.swapaxes(a, 1, 2)), (lambda a: jnp.swapaxes(a, 1, 2))
        else:
            rows, cols = math.prod(shp[1:-1]), shp[-1]
            flat, back = (lambda a: a.reshape(L, rows, cols)), (lambda a: a.reshape(shp))
        keys, comm = [], None
        if i == 0:
            keys, comm = rest_of_sends()
        elif i == 1:
            comm = _gather_comm([pack(D)], [(0, None)])
        res, extra = _adamw_sharded([R[n, l] for l in range(L)], flat(P[n]), flat(P["m_" + n]), flat(P["v_" + n]),
                                    "adamw_sharded", comm)
        if i == 0:
            settle(keys, extra)
        elif i == 1:
            gathered_small[widths.index(D)] = extra[0]
        out[n] = tuple(back(a) for a in res)

    res = _adamw_replicated(gathered_small, [layout[n] for n in small_names],
                            [(as2d(n, P[n]), as2d(n, P["m_" + n]), as2d(n, P["v_" + n])) for n in small_names],
                            "adamw_replicated")
    for p, n in enumerate(small_names):
        out[n] = tuple(a.reshape(P[n].shape) for a in res[4 * p:4 * p + 4])
    conv_sum = res[4 * len(small_names) + conv_at["b"]][conv_at["r0"]:conv_at["r0"] + conv_at["rows"]]
    conv_mine = lax.dynamic_slice_in_dim(conv_sum, me * cw, cw, axis=1)

    def conv2d(a, b):
        return jnp.concatenate([a, b], axis=1).reshape(L * (SCONV_K + CCONV_K), cw)

    cd, cm, cv = _adamw_small(conv_mine, conv2d(sconv_w, cconv_w), conv2d(m_sconv_w, m_cconv_w),
                              conv2d(v_sconv_w, v_cconv_w), "adamw_conv")
    for n, sl in (("sconv_w", slice(0, SCONV_K)), ("cconv_w", slice(SCONV_K, SCONV_K + CCONV_K))):
        out[n] = tuple(a.reshape(L, SCONV_K + CCONV_K, cw)[:, sl] for a in (conv_mine, cd, cm, cv))

    order = ["ffn1_norm", "ffn1_w13", "ffn1_w2", "mix_norm", "w_in", "pool_w", "pool_scale", "sconv_w", "cconv_w",
             "cconv_ln_g", "cconv_ln_b", "sgu_ln_g", "sgu_ln_b", "sgu_w", "sgu_b", "w_up", "w_out", "ffn2_norm",
             "ffn2_w13", "ffn2_w2", "final_norm"]
    return (loss, grad_x, *[out[n][0] for n in order], *[out[n][1] for n in order],
            *[out[n][2] for n in order], *[out[n][3] for n in order])
```

```python
import functools
import math

import jax
import jax.numpy as jnp
from jax import lax
from jax.experimental import pallas as pl
from jax.experimental.pallas import tpu as pltpu

F32 = jnp.float32
BF16 = jnp.bfloat16
EPS = 1e-6
ADAM_LR = 0.001
ADAM_B1 = 0.9
ADAM_B2 = 0.999
ADAM_EPS = 1e-08
ADAM_WD = 0.01
ADAM_STEP = 10
SGU_BLOCK = 128
SGU_CHUNK = 64
SCONV_K = 3
CCONV_K = 31
HALO = 32
V7X_VMEM_LIMIT = 48 * 1024 * 1024
LANES = 128
BF16_ROWS = 16
WIDE = 11 * LANES
ROWS_L, ROWS_M, ROWS_S = 1024, 512, 256
MESH_AXES = ("x", "y", "c")
N_DEV = 8
_GELU_C0 = math.sqrt(2.0 / math.pi)
_GELU_C1 = 0.044715

BS = pl.BlockSpec
SDS = jax.ShapeDtypeStruct
ANY = pl.BlockSpec(memory_space=pl.ANY)


def _tile(n, pref, align=128):
    if n <= pref:
        return n
    t = pref - pref % align
    while t > 0:
        if n % t == 0:
            return t
        t -= align
    return n


def _sig(v):
    return 1.0 / (1.0 + jnp.exp(-v))


def _gelu(v):
    t = jnp.tanh(_GELU_C0 * (v + _GELU_C1 * (v * v * v)))
    return 0.5 * v * (1.0 + t), t


def _gelu_grad(v, t):
    return 0.5 * (1.0 + t) + 0.5 * v * (1.0 - t * t) * (_GELU_C0 * (1.0 + 3.0 * _GELU_C1 * v * v))


def _ln_stats(v):
    mu = jnp.mean(v, axis=-1, keepdims=True)
    vc = v - mu
    var = jnp.mean(vc * vc, axis=-1, keepdims=True)
    rstd = lax.rsqrt(var + EPS)
    return vc * rstd, rstd


def _ln_bwd(dvh, vh, rstd):
    return rstd * (dvh - jnp.mean(dvh, axis=-1, keepdims=True) - vh * jnp.mean(dvh * vh, axis=-1, keepdims=True))


def _dot(a, b):
    return jnp.dot(a, b, preferred_element_type=F32)


def _dot_nt(a, b):
    return lax.dot_general(a, b, (((1,), (1,)), ((), ())), preferred_element_type=F32)


def _dot_tn(a, b):
    return lax.dot_general(a, b, (((0,), (0,)), ((), ())), preferred_element_type=F32)


def _mesh_pos():
    return lax.axis_index("x"), lax.axis_index("y"), lax.axis_index("c")


class _Comm:
    def __init__(self, ins, out_shape, sems, start, finish, aliases=None, middle=None):
        self.ins, self.out_shape, self.sems, self.start, self.finish = ins, out_shape, sems, start, finish
        self.aliases = aliases or {}
        self.middle = middle


def _gather_comm(shards, units):
    n_u = len(units)
    out_shape = []
    for t, l in units:
        shp = shards[t].shape if l is None else shards[t].shape[1:]
        out_shape.append(SDS((N_DEV,) + tuple(shp), shards[t].dtype))

    def upper_rows(o):
        shp = out_shape[o].shape[1:]
        assert shp[0] >= 2
        return shp[0] // 2 if len(shp) > 2 or shp[0] < 32 else shp[0] // 32 * 16

    def tools(ins, dsts, sems):
        send_sems, recv_sems, local_sems = sems
        x, y, c = _mesh_pos()
        me, sib = (x, y, c), (x, y, 1 - c)
        xn, yn, dg = (1 - x, y, c), (x, 1 - y, c), (1 - x, 1 - y, c)

        def src_of(o):
            t, l = units[o]
            return ins[t] if l is None else ins[t].at[l]

        def row(o, p, part=None):
            r = dsts[o].at[4 * p[0] + 2 * p[1] + p[2]]
            if part is None:
                return r
            h = upper_rows(o)
            return r.at[pl.ds(0, h)] if part == "upper" else r.at[pl.ds(h, out_shape[o].shape[1] - h)]

        def copy(o, k, src, dst, to):
            return pltpu.make_async_remote_copy(
                src_ref=src, dst_ref=dst, send_sem=send_sems.at[o * 8 + k], recv_sem=recv_sems.at[o * 8 + k],
                device_id=to, device_id_type=pl.DeviceIdType.MESH)

        def send(o, k):
            if k < 3:
                return copy(o, k, src_of(o), row(o, me), (sib, xn, yn)[k])
            if k == 3:
                return copy(o, k, row(o, xn, "upper"), row(o, xn, "upper"), yn)
            if k == 4:
                return copy(o, k, row(o, yn, "lower"), row(o, yn, "lower"), xn)
            blk = (xn, yn, dg)[k - 5]
            return copy(o, k, row(o, blk), row(o, blk), sib)

        def landed(o, k):
            def other(p):
                return (p[0], p[1], 1 - c)

            dst = (row(o, sib), row(o, xn), row(o, yn), row(o, dg, "upper"), row(o, dg, "lower"),
                   row(o, other(xn)), row(o, other(yn)), row(o, other(dg)))[k]
            return copy(o, k, dst, dst, me)

        def local(o):
            return pltpu.make_async_copy(src_of(o), row(o, me), local_sems.at[o])

        return send, landed, local

    def start(ins, dsts, sems):
        send, _, local = tools(ins, dsts, sems)
        for o in range(n_u):
            local(o).start()
            for k in (1, 2, 0):
                send(o, k).start()

    def middle(ins, dsts, sems):
        send, landed, _ = tools(ins, dsts, sems)
        for o in range(n_u):
            landed(o, 1).wait_recv()
            send(o, 3).start()
            landed(o, 2).wait_recv()
            send(o, 4).start()
            send(o, 5).start()
            send(o, 6).start()

    def finish(ins, dsts, sems):
        send, landed, local = tools(ins, dsts, sems)
        for o in range(n_u):
            landed(o, 3).wait_recv()
            landed(o, 4).wait_recv()
            send(o, 7).start()
        for o in range(n_u):
            for k in (0, 5, 6, 7):
                landed(o, k).wait_recv()
        for o in range(n_u):
            for k in range(8):
                send(o, k).wait_send()
            local(o).wait()

    sems = [pltpu.SemaphoreType.DMA((8 * n_u,)), pltpu.SemaphoreType.DMA((8 * n_u,)), pltpu.SemaphoreType.DMA((n_u,))]
    return _Comm(list(shards), out_shape, sems, start, finish, middle=middle)


PEERS_ALL = (1, 2, 3, 4, 5, 6, 7)
PEERS_SAME_CORE = (1, 2, 4, 6)
PEERS_OTHER_CORE = (3, 5, 7)
PEERS_BUT_NEAR_OTHER = (1, 2, 4, 6, 7)
PEERS_NEAR_OTHER = (3, 5)
PEERS_FAR_OTHER = (7,)


def _scatter_comm(parts, peers=PEERS_ALL, into=None):
    n_u = len(parts)

    def tools(ins, dsts, sems):
        send_sems, recv_sems, local_sems = sems
        x, y, c = _mesh_pos()
        me = 4 * x + 2 * y + c

        def peer(k):
            return ((x + ((k >> 2) & 1)) % 2, (y + ((k >> 1) & 1)) % 2, (c + (k & 1)) % 2)

        def copy(u, k, wait=False):
            p = peer(k)
            pi = 4 * p[0] + 2 * p[1] + p[2]
            return pltpu.make_async_remote_copy(
                src_ref=ins[u].at[pi], dst_ref=dsts[u].at[pi if wait else me],
                send_sem=send_sems.at[u * 7 + k - 1], recv_sem=recv_sems.at[u * 7 + k - 1],
                device_id=p, device_id_type=pl.DeviceIdType.MESH)

        def local(u):
            return pltpu.make_async_copy(ins[u].at[me], dsts[u].at[me], local_sems.at[u])

        return copy, local

    def start(ins, dsts, sems):
        copy, local = tools(ins, dsts, sems)
        for u in range(n_u):
            if into is None:
                local(u).start()
            for k in peers:
                copy(u, k).start()

    def finish(ins, dsts, sems):
        copy, local = tools(ins, dsts, sems)
        for u in range(n_u):
            for k in peers:
                copy(u, k, wait=True).wait()
            if into is None:
                local(u).wait()

    sems = [pltpu.SemaphoreType.DMA((7 * n_u,)), pltpu.SemaphoreType.DMA((7 * n_u,)), pltpu.SemaphoreType.DMA((n_u,))]
    aliases = {} if into is None else {n_u + u: u for u in range(n_u)}
    return _Comm(list(parts) + list(into or []), [SDS(p.shape, p.dtype) for p in parts], sems, start, finish, aliases)


def _run_comm(comm, name):
    n_i, n_o = len(comm.ins), len(comm.out_shape)

    def body(*refs):
        comm.start(refs[:n_i], refs[n_i:n_i + n_o], refs[n_i + n_o:])
        if comm.middle is not None:
            comm.middle(refs[:n_i], refs[n_i:n_i + n_o], refs[n_i + n_o:])
        comm.finish(refs[:n_i], refs[n_i:n_i + n_o], refs[n_i + n_o:])

    return pl.pallas_call(
        body, name=name, out_shape=comm.out_shape, in_specs=[ANY] * n_i, out_specs=[ANY] * n_o,
        scratch_shapes=comm.sems,
    )(*comm.ins)


def _pcall(body, name, grid, in_specs, out_specs, out_shape, scratch, sem, args, comm=None, aliases=None):
    n_i, n_o, n_s = len(in_specs), len(out_specs), len(scratch)
    aliases = aliases or {}
    if comm is None:
        res = pl.pallas_call(
            body, name=name, grid=grid, in_specs=in_specs, out_specs=out_specs, out_shape=out_shape,
            scratch_shapes=scratch, input_output_aliases=aliases,
            compiler_params=pltpu.CompilerParams(dimension_semantics=sem, vmem_limit_bytes=V7X_VMEM_LIMIT),
        )(*args)
        return res, []
    n_ci, n_co = len(comm.ins), len(comm.out_shape)

    def wrapped(*refs):
        ins = refs[:n_i]
        cins = refs[n_i:n_i + n_ci]
        outs = refs[n_i + n_ci:n_i + n_ci + n_o]
        couts = refs[n_i + n_ci + n_o:n_i + n_ci + n_o + n_co]
        rest = refs[n_i + n_ci + n_o + n_co:]
        step = 0
        for d, g in enumerate(grid):
            step = step * g + pl.program_id(d)
        n_steps = math.prod(grid)
        mid = (n_steps * 5) // 8
        staged = comm.middle is not None and 0 < mid < n_steps - 1

        @pl.when(step == 0)
        def _():
            comm.start(cins, couts, rest[n_s:])

        if staged:
            @pl.when(step == mid)
            def _():
                comm.middle(cins, couts, rest[n_s:])

        body(*ins, *outs, *rest[:n_s])

        @pl.when(step == n_steps - 1)
        def _():
            if comm.middle is not None and not staged:
                comm.middle(cins, couts, rest[n_s:])
            comm.finish(cins, couts, rest[n_s:])

    res = pl.pallas_call(
        wrapped, name=name, grid=grid, in_specs=list(in_specs) + [ANY] * n_ci,
        out_specs=list(out_specs) + [ANY] * n_co, out_shape=list(out_shape) + list(comm.out_shape),
        scratch_shapes=list(scratch) + list(comm.sems),
        input_output_aliases={**aliases, **{n_i + ci: n_o + co for ci, co in comm.aliases.items()}},
        compiler_params=pltpu.CompilerParams(dimension_semantics=("arbitrary",) * len(grid),
                                             vmem_limit_bytes=V7X_VMEM_LIMIT),
    )(*args, *comm.ins)
    return res[:n_o], res[n_o:]


def _rmsnorm_fwd(x, g, name):
    S, D = x.shape
    tm = _tile(S, ROWS_M, BF16_ROWS)

    def body(x_ref, g_ref, h_ref):
        h_ref[...] = _rmsnorm_rows(x_ref[...], g_ref[...])

    (h,), _ = _pcall(body, name, (S // tm,), [BS((tm, D), lambda i: (i, 0)), BS((1, D), lambda i: (0, 0))],
                     [BS((tm, D), lambda i: (i, 0))], [SDS((S, D), BF16)], [], ("parallel",), (x, g))
    return h


def _matmul_fwd(a, w, name, comm=None, w_t=False, out_dtype=F32):
    S, K = a.shape
    C = w.shape[0]
    Fc = w.shape[1] if w_t else w.shape[2]
    tn = _tile(Fc, WIDE)
    tm = _tile(S, ROWS_L, BF16_ROWS)

    def body(a_ref, w_ref, o_ref):
        p = _dot_nt(a_ref[...], w_ref[0]) if w_t else _dot(a_ref[...], w_ref[0])
        o_ref[0] = p.astype(out_dtype)

    w_spec = BS((1, tn, K), lambda c, n, i: (c, n, 0)) if w_t else BS((1, K, tn), lambda c, n, i: (c, 0, n))
    (o,), extra = _pcall(
        body, name, (C, Fc // tn, S // tm), [BS((tm, K), lambda c, n, i: (i, 0)), w_spec],
        [BS((1, tm, tn), lambda c, n, i: (c, i, n))], [SDS((C, S, Fc), out_dtype)], [],
        ("parallel", "parallel", "parallel"), (a, w), comm)
    return o, extra


def _rmsnorm_rows(xv, g):
    r = lax.rsqrt(jnp.mean(xv * xv, axis=-1, keepdims=True) + EPS)
    return (xv * r * g).astype(BF16)


def _swiglu_down(ab, w2, x, g_next, name, comm=None):
    _, S, F = ab.shape
    D = w2.shape[1]
    tk = _tile(F, WIDE)
    tm = _tile(S, ROWS_M, BF16_ROWS)
    nk = F // tk

    def body(ab_ref, w_ref, x_ref, g_ref, o_ref, h_ref):
        k = pl.program_id(1)
        a = ab_ref[0].astype(F32)
        s = a * _sig(a) * ab_ref[1].astype(F32)
        p = 0.5 * _dot(s.astype(BF16), w_ref[...])

        @pl.when(k == 0)
        def _():
            o_ref[...] = x_ref[...] + p

        @pl.when(k > 0)
        def _():
            o_ref[...] += p

        @pl.when(k == nk - 1)
        def _():
            h_ref[...] = _rmsnorm_rows(o_ref[...], g_ref[...])

    (o, h), extra = _pcall(
        body, name, (S // tm, nk),
        [BS((2, tm, tk), lambda i, k: (0, i, k)), BS((tk, D), lambda i, k: (k, 0)), BS((tm, D), lambda i, k: (i, 0)),
         BS((1, D), lambda i, k: (0, 0))],
        [BS((tm, D), lambda i, k: (i, 0)), BS((tm, D), lambda i, k: (i, 0))],
        [SDS((S, D), F32), SDS((S, D), BF16)], [], ("parallel", "arbitrary"), (ab, w2, x, g_next), comm)
    return o, h, extra


def _ffn_bwd_hidden(dy, w2, ab, name, comm=None):
    S, D = dy.shape
    F = w2.shape[0]
    tk = _tile(F, WIDE)
    tm = _tile(S, ROWS_M, BF16_ROWS)
    te = _tile(tm, 128, 16)

    def body(dy_ref, w_ref, ab_ref, dab_ref, s_ref, ds_ref):
        ds_ref[...] = 0.5 * _dot_nt(dy_ref[...].astype(BF16), w_ref[...])
        for r0 in range(0, tm, te):
            rows = slice(r0, r0 + te)
            ds = ds_ref[rows, :]
            a = ab_ref[0, rows, :].astype(F32)
            b = ab_ref[1, rows, :].astype(F32)
            sg = _sig(a)
            sa = a * sg
            dab_ref[0, rows, :] = (ds * b * (sg * (1.0 + a * (1.0 - sg)))).astype(BF16)
            dab_ref[1, rows, :] = (ds * sa).astype(BF16)
            s_ref[rows, :] = (0.5 * (sa * b)).astype(BF16)

    (dab, sh), extra = _pcall(
        body, name, (F // tk, S // tm),
        [BS((tm, D), lambda k, i: (i, 0)), BS((tk, D), lambda k, i: (k, 0)), BS((2, tm, tk), lambda k, i: (0, i, k))],
        [BS((2, tm, tk), lambda k, i: (0, i, k)), BS((tm, tk), lambda k, i: (i, k))],
        [SDS((2, S, F), BF16), SDS((S, F), BF16)], [pltpu.VMEM((tm, tk), F32)], ("parallel", "parallel"),
        (dy, w2, ab), comm)
    return dab, sh, extra


def _matmul_tn(a, b, n_c, name, comm=None, b_shared=False):
    G, S, M = a.shape
    _, _, Fc = b.shape
    C = n_c
    tM = _tile(M, WIDE)
    tn = _tile(Fc, WIDE)
    ts = _tile(S, ROWS_L, BF16_ROWS)
    n_s = S // ts

    def body(a_ref, b_ref, o_ref, acc):
        s = pl.program_id(4)
        p = _dot_tn(a_ref[0].astype(BF16), b_ref[0].astype(BF16))

        @pl.when(s == 0)
        def _():
            acc[...] = p

        @pl.when(s > 0)
        def _():
            acc[...] += p

        @pl.when(s == n_s - 1)
        def _():
            o_ref[0] = acc[...].astype(BF16)

    (o,), extra = _pcall(
        body, name, (G, M // tM, C, Fc // tn, n_s),
        [BS((1, ts, tM), lambda g, m, c, n, s: (g, s, m)), BS((1, ts, tn), lambda g, m, c, n, s: (c if b_shared else g * C + c, s, n))],
        [BS((1, tM, tn), lambda g, m, c, n, s: (g * C + c, m, n))], [SDS((G * C, M, Fc), BF16)],
        [pltpu.VMEM((tM, tn), F32)], ("parallel", "parallel", "parallel", "parallel", "arbitrary"), (a, b), comm)
    return o, extra


def _matmul_nt_normbwd(b, w, x, gam, dres, name, comm=None, w_t=False):
    C, S, Fc = b.shape
    D = w.shape[2] if w_t else w.shape[1]
    tk = _tile(Fc, WIDE)
    tm = _tile(S, ROWS_L, BF16_ROWS)
    te = _tile(tm, 256, 8)
    nk = Fc // tk

    def body(b_ref, w_ref, x_ref, g_ref, r_ref, dx_ref, dg_ref):
        i, c, k = pl.program_id(0), pl.program_id(1), pl.program_id(2)
        p = _dot(b_ref[0], w_ref[0]) if w_t else _dot_nt(b_ref[0], w_ref[0])
        first = jnp.logical_and(c == 0, k == 0)

        @pl.when(first)
        def _():
            dx_ref[...] = p

        @pl.when(jnp.logical_not(first))
        def _():
            dx_ref[...] += p

        @pl.when(jnp.logical_and(c == C - 1, k == nk - 1))
        def _():
            dgp = None
            for r0 in range(0, tm, te):
                rows = slice(r0, r0 + te)
                xv = x_ref[rows, :]
                r = lax.rsqrt(jnp.mean(xv * xv, axis=-1, keepdims=True) + EPS)
                xn = xv * r
                dh = dx_ref[rows, :]
                dxn = dh * g_ref[...]
                dx_ref[rows, :] = r_ref[rows, :] + r * (dxn - xn * jnp.mean(dxn * xn, axis=-1, keepdims=True))
                t = jnp.sum(dh * xn, axis=0, keepdims=True)
                dgp = t if dgp is None else dgp + t

            @pl.when(i == 0)
            def _():
                dg_ref[...] = dgp

            @pl.when(i > 0)
            def _():
                dg_ref[...] += dgp

    once = dict(pipeline_mode=pl.Buffered(1))
    (dx, dg), extra = _pcall(
        body, name, (S // tm, C, nk),
        [BS((1, tm, tk), lambda i, c, k: (c, i, k)),
         BS((1, tk, D), lambda i, c, k: (c, k, 0)) if w_t else BS((1, D, tk), lambda i, c, k: (c, 0, k)),
         BS((tm, D), lambda i, c, k: (i, 0), **once), BS((1, D), lambda i, c, k: (0, 0)),
         BS((tm, D), lambda i, c, k: (i, 0), **once)],
        [BS((tm, D), lambda i, c, k: (i, 0)), BS((1, D), lambda i, c, k: (0, 0))],
        [SDS((S, D), F32), SDS((1, D), F32)], [],
        ("arbitrary", "arbitrary", "arbitrary"), (b, w, x, gam, dres), comm)
    return dx, dg, extra


def _final_loss(x, gam, target, name):
    S, D = x.shape
    tm = _tile(S, 512, 8)

    def body(x_ref, g_ref, t_ref, loss_ref, dx_ref, dg_ref):
        i = pl.program_id(0)
        xv = x_ref[...]
        r = lax.rsqrt(jnp.mean(xv * xv, axis=-1, keepdims=True) + EPS)
        xn = xv * r
        err = xn * g_ref[...] - t_ref[...]
        part = 0.5 * jnp.sum(jnp.mean(err * err, axis=-1, keepdims=True), axis=0, keepdims=True)
        dy = err * (1.0 / D)
        dxn = dy * g_ref[...]
        dx_ref[...] = r * (dxn - xn * jnp.mean(dxn * xn, axis=-1, keepdims=True))
        dgp = jnp.sum(dy * xn, axis=0, keepdims=True)
        lp = jnp.broadcast_to(part, loss_ref.shape)

        @pl.when(i == 0)
        def _():
            dg_ref[...] = dgp
            loss_ref[...] = lp

        @pl.when(i > 0)
        def _():
            dg_ref[...] += dgp
            loss_ref[...] += lp

    res, _ = _pcall(
        body, name, (S // tm,),
        [BS((tm, D), lambda i: (i, 0)), BS((1, D), lambda i: (0, 0)), BS((tm, D), lambda i: (i, 0))],
        [BS((8, 128), lambda i: (0, 0)), BS((tm, D), lambda i: (i, 0)), BS((1, D), lambda i: (0, 0))],
        [SDS((8, 128), F32), SDS((S, D), F32), SDS((1, D), F32)], [], ("arbitrary",), (x, gam, target))
    return res


CONV_CHUNK = 32


def _fill_shifted(rot, n):
    for b in range(1, 8):
        rot[b, 0:n - 8, :] = rot[0, b:b + n - 8, :]


def _window(rot, off, r0, rows):
    b = off % 8
    return rot[b, off - b + r0:off - b + r0 + rows, :]


def _taps(rot, w_ref, offs, n_rows, out):
    for r0 in range(0, n_rows, CONV_CHUNK):
        acc = None
        for k, off in enumerate(offs):
            t = w_ref[k:k + 1, :] * _window(rot, off, r0, CONV_CHUNK)
            acc = t if acc is None else acc + t
        out[r0:r0 + CONV_CHUNK, :] = acc


def _tap_grads(rot, offs, g_plane, n_rows, dw_ref):
    for k, off in enumerate(offs):
        acc = None
        for r0 in range(0, n_rows, CONV_CHUNK):
            p = g_plane[0, r0:r0 + CONV_CHUNK, :] * _window(rot, off, r0, CONV_CHUNK)
            acc = p if acc is None else acc + p
        dw_ref[k:k + 1, :] += jnp.sum(acc, axis=0, keepdims=True)


def _sgu_masks():
    ii = lax.broadcasted_iota(jnp.int32, (SGU_BLOCK, SGU_BLOCK), 0) // SGU_CHUNK
    jj = lax.broadcasted_iota(jnp.int32, (SGU_BLOCK, SGU_BLOCK), 1) // SGU_CHUNK
    return jj <= ii, ii <= jj


def _mixers_fwd(proj, pool_w, pool_scale, sconv_w, cconv_w, cln_g, cln_b, sln_g, sln_b, sgu_w, sgu_bias, name,
                comm=None):
    _, S, D = proj.shape
    BW = D // 2
    GW = BW // 4
    TS = _tile(S, ROWS_S, SGU_BLOCK)
    H = HALO
    hb = TS // H

    def main(blk, col):
        return BS((1, TS, BW), lambda i: (blk, i, col))

    def back(blk, col):
        return BS((1, H, BW), lambda i: (blk, jnp.maximum(i * hb - 1, 0), col))

    def full(a):
        nd = a.ndim
        return BS(a.shape, lambda i: (0,) * nd)

    def body(pa_m, pa_b, xi_m, xi_b, bg_m, cg_m, cg_b, ca_m, ca_b, cb_m, cb_b, du_m, dv_m,
             pw, ps, sw, cw, clg, clb, slg, slb, gw, gbias, y_ref, y1_ref, e1, e2, e3):
        i = pl.program_id(0)
        nb = jnp.where(i > 0, 1.0, 0.0).astype(F32)
        rows = i * TS + lax.broadcasted_iota(jnp.int32, (TS, 1), 0)

        e1[0:H, :] = pa_b[0] * nb
        e1[H:H + TS, :] = pa_m[0]
        for g in range(4):
            cols = slice(g * GW, (g + 1) * GW)
            win = 2 << g
            wsum = e1[H:H + TS, cols]
            for k in range(1, win):
                wsum = wsum + e1[H - k:H - k + TS, cols]
            cnt = jnp.minimum(rows + 1, win).astype(F32)
            d = wsum / cnt - e1[H:H + TS, cols]
            yg = _dot(d.astype(BF16), pw[g].astype(BF16)) * ps[:, cols]
            y_ref[0, :, cols] = yg.astype(BF16)

        e2[0:H, :] = cg_b[0] * xi_b[0] * nb
        e2[H:H + TS, :] = cg_m[0] * xi_m[0]
        cz = sw[0:1, :] * e2[H - 2:H - 2 + TS, :]
        for k in range(1, SCONV_K):
            cz = cz + sw[k:k + 1, :] * e2[H - 2 + k:H - 2 + k + TS, :]
        y_ref[1] = (bg_m[0] * cz).astype(BF16)

        e3[0, 0:H, :] = ca_b[0] * _sig(cb_b[0]) * nb
        e3[0, H:H + TS, :] = ca_m[0] * _sig(cb_m[0])
        _fill_shifted(e3, H + TS)
        _taps(e3, cw, [H - (CCONV_K - 1) + k for k in range(CCONV_K)], TS, y1_ref)
        yh, _ = _ln_stats(y1_ref[...])
        y2 = yh * clg[...] + clb[...]
        y_ref[2] = (y2 * _sig(y2)).astype(BF16)

        u, _ = _gelu(du_m[0])
        v, _ = _gelu(dv_m[0])
        vh, _ = _ln_stats(v)
        vn = vh * slg[...] + slb[...]
        mask, _ = _sgu_masks()
        for h in range(4):
            wm = jnp.where(mask, gw[h], 0.0).astype(BF16)
            cs = slice(h * GW, (h + 1) * GW)
            for n in range(TS // SGU_BLOCK):
                rs = slice(n * SGU_BLOCK, (n + 1) * SGU_BLOCK)
                z = _dot(wm, vn[rs, cs].astype(BF16)) + gbias[h]
                y_ref[3, rs, cs] = (u[rs, cs] * z).astype(BF16)

    args = [proj] * 13 + [pool_w, pool_scale, sconv_w, cconv_w, cln_g, cln_b, sln_g, sln_b, sgu_w, sgu_bias]
    in_specs = [main(0, 0), back(0, 0), main(0, 1), back(0, 1), main(1, 0), main(1, 1), back(1, 1),
                main(2, 0), back(2, 0), main(2, 1), back(2, 1), main(3, 0), main(3, 1)]
    in_specs += [full(a) for a in args[13:]]
    (y, y1), extra = _pcall(body, name, (S // TS,), in_specs,
                            [BS((4, TS, BW), lambda i: (0, i, 0)), BS((TS, BW), lambda i: (i, 0))],
                            [SDS((4, S, BW), BF16), SDS((S, BW), F32)],
                            [pltpu.VMEM((H + TS, BW), F32)] * 2 + [pltpu.VMEM((8, H + TS, BW), F32)], ("parallel",),
                            args, comm)
    return y, y1, extra


def _mixers_bwd(proj, y1, dy, dproj_gates, pool_w, pool_wt, pool_scale, sconv_w, cconv_w, cln_g, cln_b, sln_g, sln_b,
                sgu_w, sgu_wt, sgu_bias, name, comm=None):
    _, S, D = proj.shape
    BW = D // 2
    GW = BW // 4
    TS = _tile(S, ROWS_S, SGU_BLOCK)
    H = HALO
    hb = TS // H
    n_t = S // TS
    E = TS + H

    def main(blk, col):
        return BS((1, TS, BW), lambda i: (blk, i, col))

    def back(blk, col):
        return BS((1, H, BW), lambda i: (blk, jnp.maximum(i * hb - 1, 0), col))

    def front(blk, col):
        return BS((1, H, BW), lambda i: (blk, jnp.minimum((i + 1) * hb, S // H - 1), col))

    def full(a):
        nd = a.ndim
        return BS(a.shape, lambda i: (0,) * nd)

    def body(pa_b, pa_m, xi_b, xi_m, bg_m, bg_f, cg_b, cg_m, ca_b, ca_m, cb_b, cb_m, du_m, dv_m, y1_m, y1_f,
             dya_m, dya_f, dyb_m, dyb_f, dyc_m, dyc_f, dyd_m,
             pw, pwt, ps, sw, cw, clg, clb, slg, slb, gw, gwt, gbias, _gates_in,
             dp_ref, dpw, dps, dsw, dcw, dclg, dclb, dslg, dslb, dgw, dgb,
             e1, e2, e3, e4, e5, ra, rb):
        i = pl.program_id(0)
        nb = jnp.where(i > 0, 1.0, 0.0).astype(F32)
        nf = jnp.where(i < n_t - 1, 1.0, 0.0).astype(F32)
        rows_m = i * TS + lax.broadcasted_iota(jnp.int32, (TS, 1), 0)
        rows_e = i * TS + lax.broadcasted_iota(jnp.int32, (E, 1), 0)

        @pl.when(i == 0)
        def _():
            for r in (dpw, dps, dsw, dcw, dclg, dclb, dslg, dslb, dgw, dgb):
                r[...] = jnp.zeros(r.shape, F32)

        e1[0:H, :] = pa_b[0] * nb
        e1[H:H + TS, :] = pa_m[0]
        e2[0:TS, :] = dya_m[0] * ps[...]
        e2[TS:E, :] = dya_f[0] * ps[...] * nf
        for g in range(4):
            cols = slice(g * GW, (g + 1) * GW)
            win = 2 << g
            a_m = e1[H:H + TS, cols]
            wsum = a_m
            for k in range(1, win):
                wsum = wsum + e1[H - k:H - k + TS, cols]
            d = wsum / jnp.minimum(rows_m + 1, win).astype(F32) - a_m
            d16 = d.astype(BF16)
            dyp = e2[0:E, cols].astype(BF16)
            dd = _dot(dyp, pwt[g].astype(BF16))
            e3[0:E, cols] = dd / jnp.minimum(rows_e + 1, win).astype(F32)
            da = e3[0:TS, cols] - dd[0:TS]
            for k in range(1, win):
                da = da + e3[k:k + TS, cols]
            dp_ref[0, :, cols] = da.astype(BF16)
            ypre = _dot(d16, pw[g].astype(BF16))
            dps[:, cols] += jnp.sum(dya_m[0][:, cols] * ypre, axis=0, keepdims=True)
            dpw[g] += _dot(jnp.transpose(d).astype(BF16), dyp[0:TS])

        e4[0:H, :] = cg_b[0] * xi_b[0] * nb
        e4[H:H + TS, :] = cg_m[0] * xi_m[0]
        dyb = dyb_m[0]
        e5[0:TS, :] = dyb * bg_m[0]
        e5[TS:E, :] = dyb_f[0] * bg_f[0] * nf
        dcz = e5[0:TS, :]
        cz = None
        dz = None
        for k in range(SCONV_K):
            zk = e4[H - 2 + k:H - 2 + k + TS, :]
            wk = sw[k:k + 1, :]
            cz = wk * zk if cz is None else cz + wk * zk
            t = wk * e5[2 - k:2 - k + TS, :]
            dz = t if dz is None else dz + t
            dsw[k:k + 1, :] += jnp.sum(dcz * zk, axis=0, keepdims=True)
        dp_ref[0, :, BW:2 * BW] = (dz * cg_m[0]).astype(BF16)
        dp_ref[1, :, 0:BW] = (dyb * cz).astype(BF16)
        dp_ref[1, :, BW:2 * BW] = (dz * xi_m[0]).astype(BF16)

        sgm = _sig(cb_m[0])
        ra[0, 0:H, :] = ca_b[0] * _sig(cb_b[0]) * nb
        ra[0, H:H + TS, :] = ca_m[0] * sgm
        _fill_shifted(ra, H + TS)
        fwd_offs = [H - (CCONV_K - 1) + k for k in range(CCONV_K)]
        e4[0:TS, :] = y1_m[...]
        e4[TS:E, :] = y1_f[...]
        yh, rstd = _ln_stats(e4[0:E, :])
        y2 = yh * clg[...] + clb[...]
        s2 = _sig(y2)
        e1[0:TS, :] = dyc_m[0]
        e1[TS:E, :] = dyc_f[0] * nf
        dy2 = e1[0:E, :] * (s2 * (1.0 + y2 * (1.0 - s2)))
        dclg[...] += jnp.sum((dy2 * yh)[0:TS], axis=0, keepdims=True)
        dclb[...] += jnp.sum(dy2[0:TS], axis=0, keepdims=True)
        rb[0, 0:E, :] = _ln_bwd(dy2 * clg[...], yh, rstd)
        _fill_shifted(rb, E)
        _taps(rb, cw, [CCONV_K - 1 - k for k in range(CCONV_K)], TS, e5)
        _tap_grads(ra, fwd_offs, rb, TS, dcw)
        dy0 = e5[0:TS, :]
        dp_ref[2, :, 0:BW] = (dy0 * sgm).astype(BF16)
        dp_ref[2, :, BW:2 * BW] = (dy0 * ca_m[0] * (sgm * (1.0 - sgm))).astype(BF16)

        pu = du_m[0]
        pv = dv_m[0]
        u, tu = _gelu(pu)
        v, tv = _gelu(pv)
        vh, vr = _ln_stats(v)
        vn = vh * slg[...] + slb[...]
        dyd = dyd_m[0]
        mask, mask_t = _sgu_masks()
        for h in range(4):
            wm = jnp.where(mask, gw[h], 0.0).astype(BF16)
            wmt = jnp.where(mask_t, gwt[h], 0.0).astype(BF16)
            cs = slice(h * GW, (h + 1) * GW)
            for n in range(TS // SGU_BLOCK):
                rs = slice(n * SGU_BLOCK, (n + 1) * SGU_BLOCK)
                vb = vn[rs, cs].astype(BF16)
                z = _dot(wm, vb) + gbias[h]
                dzb = dyd[rs, cs] * u[rs, cs]
                dz16 = dzb.astype(BF16)
                e3[rs, cs] = dyd[rs, cs] * z
                e4[rs, cs] = _dot(wmt, dz16)
                dgw[h] += jnp.where(mask, _dot_nt(dz16, vb), 0.0)
                dgb[h] += dzb
        dvn = e4[0:TS, :]
        dslg[...] += jnp.sum(dvn * vh, axis=0, keepdims=True)
        dslb[...] += jnp.sum(dvn, axis=0, keepdims=True)
        dv = _ln_bwd(dvn * slg[...], vh, vr)
        dp_ref[3, :, 0:BW] = (e3[0:TS, :] * _gelu_grad(pu, tu)).astype(BF16)
        dp_ref[3, :, BW:2 * BW] = (dv * _gelu_grad(pv, tv)).astype(BF16)

        @pl.when(i == n_t - 1)
        def _():
            for h in range(4):
                dgb[h] = jnp.broadcast_to(jnp.sum(dgb[h], axis=1, keepdims=True), dgb.shape[1:])

    params = [pool_w, pool_wt, pool_scale, sconv_w, cconv_w, cln_g, cln_b, sln_g, sln_b, sgu_w, sgu_wt, sgu_bias]
    args = [proj] * 14 + [y1] * 2 + [dy] * 7 + params + [dproj_gates]
    in_specs = [back(0, 0), main(0, 0), back(0, 1), main(0, 1), main(1, 0), front(1, 0), back(1, 1), main(1, 1),
                back(2, 0), main(2, 0), back(2, 1), main(2, 1), main(3, 0), main(3, 1),
                BS((TS, BW), lambda i: (i, 0)), BS((H, BW), lambda i: (jnp.minimum((i + 1) * hb, S // H - 1), 0)),
                main(0, 0), front(0, 0), main(1, 0), front(1, 0), main(2, 0), front(2, 0), main(3, 0)]
    in_specs += [full(a) for a in params] + [ANY]
    small = [SDS(pool_w.shape, F32), SDS(pool_scale.shape, F32), SDS(sconv_w.shape, F32), SDS(cconv_w.shape, F32),
             SDS(cln_g.shape, F32), SDS(cln_b.shape, F32), SDS(sln_g.shape, F32), SDS(sln_b.shape, F32),
             SDS(sgu_w.shape, F32), SDS(sgu_bias.shape, F32)]
    out_specs = [BS((4, TS, D), lambda i: (0, i, 0))] + [full(s) for s in small]
    return _pcall(body, name, (n_t,), in_specs, out_specs, [SDS(dproj_gates.shape, BF16)] + small,
                  [pltpu.VMEM((TS + 2 * H, BW), F32)] * 5 + [pltpu.VMEM((8, TS + 2 * H, BW), F32)] * 2,
                  ("arbitrary",), args, comm, aliases={len(args) - 1: 0})


def _merge_fwd(y, proj, w_up, w_out, x, g_next, name, comm=None):
    _, S, BW = y.shape
    D = x.shape[1]
    tm = _tile(S, ROWS_S, BF16_ROWS)

    def body(y_ref, pg_ref, wu_ref, wo_ref, x_ref, g_ref, o_ref, m_ref, h_ref):
        merged = None
        for g in range(4):
            t = _sig(pg_ref[g]) * _dot(y_ref[g], wu_ref[g])
            merged = t if merged is None else merged + t
        m16 = merged.astype(BF16)
        m_ref[...] = m16
        xn = x_ref[...] + _dot(m16, wo_ref[...])
        o_ref[...] = xn
        h_ref[...] = _rmsnorm_rows(xn, g_ref[...])

    (o, m, h), extra = _pcall(
        body, name, (S // tm,),
        [BS((4, tm, BW), lambda i: (0, i, 0)), BS((4, tm, D), lambda i: (1, i, 0)),
         BS((4, BW, D), lambda i: (0, 0, 0)), BS((D, D), lambda i: (0, 0)), BS((tm, D), lambda i: (i, 0)),
         BS((1, D), lambda i: (0, 0))],
        [BS((tm, D), lambda i: (i, 0)), BS((tm, D), lambda i: (i, 0)), BS((tm, D), lambda i: (i, 0))],
        [SDS((S, D), F32), SDS((S, D), BF16), SDS((S, D), BF16)], [], ("parallel",),
        (y, proj, w_up, w_out, x, g_next), comm)
    return o, m, h, extra


def _merge_bwd(dx, y, proj, w_up, w_out, name, comm=None):
    _, S, BW = y.shape
    D = dx.shape[1]
    tm = _tile(S, ROWS_S, BF16_ROWS)

    def body(dx_ref, y_ref, pg_ref, wu_ref, wo_ref, dup_ref, dp_ref, dy_ref):
        dm = _dot_nt(dx_ref[...].astype(BF16), wo_ref[...])
        for g in range(4):
            gate = _sig(pg_ref[g])
            up = _dot(y_ref[g], wu_ref[g])
            dup = (dm * gate).astype(BF16)
            dup_ref[g] = dup
            dp_ref[g] = (dm * up * (gate * (1.0 - gate))).astype(BF16)
            dy_ref[g] = _dot_nt(dup, wu_ref[g])

    res, extra = _pcall(
        body, name, (S // tm,),
        [BS((tm, D), lambda i: (i, 0)), BS((4, tm, BW), lambda i: (0, i, 0)), BS((4, tm, D), lambda i: (1, i, 0)),
         BS((4, BW, D), lambda i: (0, 0, 0)), BS((D, D), lambda i: (0, 0))],
        [BS((4, tm, D), lambda i: (0, i, 0)), BS((4, tm, D), lambda i: (1, i, 0)), BS((4, tm, BW), lambda i: (0, i, 0))],
        [SDS((4, S, D), BF16), SDS((8, S, D), BF16), SDS((4, S, BW), F32)], [], ("parallel",),
        (dx, y, proj, w_up, w_out), comm)
    return res, extra


def _adamw(w, g, m, v):
    m = ADAM_B1 * m + (1.0 - ADAM_B1) * g
    v = ADAM_B2 * v + (1.0 - ADAM_B2) * (g * g)
    m_hat = m / (1.0 - ADAM_B1 ** ADAM_STEP)
    v_hat = v / (1.0 - ADAM_B2 ** ADAM_STEP)
    delta = -ADAM_LR * (m_hat / (jnp.sqrt(v_hat) + ADAM_EPS) + ADAM_WD * w)
    return delta, m, v


def _adamw_sharded(parts, w, m, v, name, comm=None):
    L, R, C = w.shape
    tr = _tile(R, ROWS_S, BF16_ROWS)

    def body(*refs):
        p_refs = refs[:L]
        w_ref, m_ref, v_ref, g_out, d_out, m_out, v_out = refs[L:]
        l = pl.program_id(0)
        g = None
        for d in range(N_DEV):
            t = p_refs[0][d].astype(F32)
            for j in range(1, L):
                t = jnp.where(l == j, p_refs[j][d].astype(F32), t)
            g = t if g is None else g + t
        dl, mn, vn = _adamw(w_ref[0], g, m_ref[0], v_ref[0])
        g_out[0] = g
        d_out[0] = dl
        m_out[0] = mn
        v_out[0] = vn

    def part_spec(j):
        return BS((N_DEV, tr, C), lambda l, r: (0, jnp.where(l == j, r, 0), 0))

    blk = BS((1, tr, C), lambda l, r: (l, r, 0))
    return _pcall(body, name, (L, R // tr), [part_spec(j) for j in range(L)] + [blk, blk, blk], [blk] * 4,
                  [SDS((L, R, C), F32)] * 4, [], ("parallel", "parallel"), (*parts, w, m, v), comm)


def _adamw_replicated(gathered, layout, wmv, name):
    n_b = len(gathered)
    n_p = len(layout)

    def body(*refs):
        bufs = refs[:n_b]
        prm = refs[n_b:n_b + 3 * n_p]
        outs = refs[n_b + 3 * n_p:n_b + 7 * n_p]
        sums = refs[n_b + 7 * n_p:]
        for b in range(n_b):
            s = bufs[b][0]
            for d in range(1, N_DEV):
                s = s + bufs[b][d]
            sums[b][...] = s
        for p, (b, r0, nr) in enumerate(layout):
            g = sums[b][r0:r0 + nr, :]
            d, mn, vn = _adamw(prm[3 * p][...], g, prm[3 * p + 1][...], prm[3 * p + 2][...])
            outs[4 * p][...] = g
            outs[4 * p + 1][...] = d
            outs[4 * p + 2][...] = mn
            outs[4 * p + 3][...] = vn

    flat = [a for t in wmv for a in t]
    out_shape = []
    for (w, _, _) in wmv:
        out_shape += [SDS(w.shape, F32)] * 4
    out_shape += [SDS(g.shape[1:], F32) for g in gathered]
    return pl.pallas_call(
        body, name=name, out_shape=out_shape,
        compiler_params=pltpu.CompilerParams(vmem_limit_bytes=V7X_VMEM_LIMIT),
    )(*gathered, *flat)


def _adamw_small(g, w, m, v, name):
    def body(g_ref, w_ref, m_ref, v_ref, d_out, m_out, v_out):
        d, mn, vn = _adamw(w_ref[...], g_ref[...], m_ref[...], v_ref[...])
        d_out[...] = d
        m_out[...] = mn
        v_out[...] = vn

    return pl.pallas_call(body, name=name, out_shape=[SDS(w.shape, F32)] * 3)(g, w, m, v)


def _pad_rows(a, rows):
    return jnp.pad(a, ((0, rows - a.shape[0]), (0, 0)))


def kernel(x, ffn1_norm, ffn1_w13, ffn1_w2, mix_norm, w_in, pool_w, pool_scale, sconv_w, cconv_w, cconv_ln_g, cconv_ln_b, sgu_ln_g, sgu_ln_b, sgu_w, sgu_b, w_up, w_out, ffn2_norm, ffn2_w13, ffn2_w2, final_norm, loss_target, m_ffn1_norm, m_ffn1_w13, m_ffn1_w2, m_mix_norm, m_w_in, m_pool_w, m_pool_scale, m_sconv_w, m_cconv_w, m_cconv_ln_g, m_cconv_ln_b, m_sgu_ln_g, m_sgu_ln_b, m_sgu_w, m_sgu_b, m_w_up, m_w_out, m_ffn2_norm, m_ffn2_w13, m_ffn2_w2, m_final_norm, v_ffn1_norm, v_ffn1_w13, v_ffn1_w2, v_mix_norm, v_w_in, v_pool_w, v_pool_scale, v_sconv_w, v_cconv_w, v_cconv_ln_g, v_cconv_ln_b, v_sgu_ln_g, v_sgu_ln_b, v_sgu_w, v_sgu_b, v_w_up, v_w_out, v_ffn2_norm, v_ffn2_w13, v_ffn2_w2, v_final_norm):
    P = dict(locals())
    L = ffn1_norm.shape[0]
    S, D = x.shape[1], x.shape[2]
    BW = D // 2
    GW = BW // 4
    F = ffn1_w2.shape[1] * N_DEV
    fs = ffn1_w13.shape[2]
    cw = sconv_w.shape[2]
    me = 4 * lax.axis_index("x") + 2 * lax.axis_index("y") + lax.axis_index("c")

    big = ["ffn1_w13", "ffn1_w2", "w_in", "w_up", "w_out", "ffn2_w13", "ffn2_w2"]
    shards = [(jnp.swapaxes(P[n], 1, 2) if n.endswith("w13") else P[n]).astype(BF16) for n in big]
    conv_local = jnp.concatenate([sconv_w, cconv_w], axis=1)

    def gather_of(units):
        return _gather_comm(shards, [(big.index(n), l) for n, l in units])

    def ready(n, g):
        if n.endswith("w13"):
            return g.reshape(2, F, D)
        if n.endswith("w2"):
            return g.reshape(F, D)
        if n == "w_up":
            return jnp.transpose(g, (1, 2, 0, 3)).reshape(4, BW, D)
        if n == "w_out":
            return g.reshape(D, D)
        return g

    W = {}

    def take(units, arrays):
        for (n, l), g in zip(units, arrays):
            W[n, l] = ready(n, g)

    first_units = [("ffn1_w13", 0), ("ffn1_w2", 0)]
    plan = {("ffn1_up", 0): [("w_in", 0)],
            ("ffn1_down", 0): [("w_up", 0), ("w_out", 0)],
            ("proj", 0): [("ffn2_w13", 0), ("ffn2_w2", 0)],
            ("mixers", 0): [("ffn1_w13", 1)], ("merge", 0): [("ffn1_w2", 1)],
            ("ffn2_up", 0): [("w_in", 1)], ("ffn2_down", 0): [("w_up", 1), ("w_out", 1)],
            ("ffn1_up", 1): [("ffn2_w13", 1)], ("ffn1_down", 1): [("ffn2_w2", 1)]}
    assert L <= 2

    def carried(key):
        units = [u for u in plan.get(key, []) if u[1] < L]
        return units, (gather_of(units) if units else None)

    first = _gather_comm(shards + [conv_local], [(big.index(n), l) for n, l in first_units] + [(len(big), None)])
    got = _run_comm(first, "gather_first_weights")
    take(first_units, got[:2])
    conv_full = jnp.transpose(got[2], (1, 2, 0, 3)).reshape(L, SCONV_K + CCONV_K, N_DEV * cw)
    sconv_full = conv_full[:, :SCONV_K]
    cconv_full = conv_full[:, SCONV_K:]

    sgu_bias = jnp.broadcast_to(sgu_b[:, :, :, None], sgu_b.shape + (GW,))
    pool_wt = jnp.swapaxes(pool_w, 2, 3)
    sgu_wt = jnp.swapaxes(sgu_w, 2, 3)

    def row(a, l):
        return a[l][None, :]

    saved = []
    xc = x[0]
    h = _rmsnorm_fwd(xc, row(ffn1_norm, 0), "first_norm_fwd")
    for l in range(L):
        sv = {}
        for tag in ("ffn1", None, "ffn2"):
            if tag is None:
                sv["x_mix"] = xc
                units, comm = carried(("proj", l))
                proj, extra = _matmul_fwd(h, W["w_in", l], "proj_fwd", comm)
                take(units, extra)
                units, comm = carried(("mixers", l))
                y, sv["y1"], extra = _mixers_fwd(
                    proj, pool_w[l], row(pool_scale, l), sconv_full[l], cconv_full[l], row(cconv_ln_g, l),
                    row(cconv_ln_b, l), row(sgu_ln_g, l), row(sgu_ln_b, l), sgu_w[l], sgu_bias[l], "mixers_fwd", comm)
                take(units, extra)
                units, comm = carried(("merge", l))
                sv.update(h_mix=h, proj=proj, y=y)
                xc, sv["merged"], h, extra = _merge_fwd(y, proj, W["w_up", l], W["w_out", l], xc, row(ffn2_norm, l),
                                                        "merge_fwd", comm)
                take(units, extra)
            else:
                sv["x_" + tag] = xc
                units, comm = carried((tag + "_up", l))
                ab, extra = _matmul_fwd(h, W[tag + "_w13", l], "ffn_up_fwd", comm, w_t=True, out_dtype=BF16)
                take(units, extra)
                units, comm = carried((tag + "_down", l))
                sv.update({"h_" + tag: h, "ab_" + tag: ab})
                if tag == "ffn1":
                    g_next = row(mix_norm, l)
                else:
                    g_next = row(ffn1_norm, l + 1) if l + 1 < L else final_norm[None, :]
                xc, h, extra = _swiglu_down(ab, W[tag + "_w2", l], xc, g_next, "ffn_down_fwd", comm)
                take(units, extra)
        saved.append(sv)

    loss_part, dx, d_final = _final_loss(xc, final_norm[None, :], loss_target[0], "loss_head")
    loss = lax.psum(loss_part[0, 0], MESH_AXES)

    R = {}
    second = []

    def rest_of_sends():
        todo = list(second)
        second.clear()
        comm = None
        if todo:
            assert len({e[3][0] for e in todo}) == 1
            comm = _scatter_comm([e[1] for e in todo], todo[0][3][0], [e[2] for e in todo])
        return todo, comm

    def settle(todo, arrays):
        for (key, g, _, stages), a in zip(todo, arrays):
            if len(stages) > 1:
                second.append((key, g, a, stages[1:]))
            else:
                R[key] = a

    wide = ["ffn1_norm", "mix_norm", "ffn2_norm", "final_norm"]
    half = ["pool_scale", "cconv_ln_g", "cconv_ln_b", "sgu_ln_g", "sgu_ln_b"]
    narrow = ["pool_w", "sgu_w", "sgu_b"]
    small_names = wide + half + narrow
    small_g = [dict() for _ in range(L)]
    widths = []
    for n in small_names:
        if P[n].shape[-1] not in widths:
            widths.append(P[n].shape[-1])
    layout, conv_at = {}, {}

    def pack(width):
        def stack_layers(n):
            return jnp.stack([small_g[l][n] for l in range(L)], axis=0)

        parts, r0 = [], 0
        for n in small_names:
            if P[n].shape[-1] != width:
                continue
            g = d_final if n == "final_norm" else stack_layers(n).reshape(-1, width)
            layout[n] = (widths.index(width), r0, g.shape[0])
            parts.append(_pad_rows(g, -(-g.shape[0] // 8) * 8))
            r0 += parts[-1].shape[0]
        if width == N_DEV * cw:
            conv_g = jnp.concatenate([stack_layers("sconv_w"), stack_layers("cconv_w")], axis=1)
            conv_g = conv_g.reshape(L * (SCONV_K + CCONV_K), N_DEV * cw)
            conv_at.update(b=widths.index(width), r0=r0, rows=conv_g.shape[0])
            parts.append(_pad_rows(conv_g, -(-conv_g.shape[0] // 8) * 8))
        return jnp.concatenate(parts, axis=0)

    gathered_small = [None] * len(widths)
    for l in reversed(range(L)):
        sv = saved[l]
        sg = small_g[l]
        for tag in ("ffn2", None, "ffn1"):
            if tag is None:
                keys, comm = rest_of_sends()
                (dup, dproj, dy), extra = _merge_bwd(dx, sv["y"], sv["proj"], W["w_up", l], W["w_out", l],
                                                     "merge_bwd", comm)
                settle(keys, extra)
                g_out, _ = _matmul_tn(sv["merged"][None], dx[None], 1, "w_out_grad")
                g_up, _ = _matmul_tn(sv["y"], dup, 1, "w_up_grad")
                g_out = g_out.reshape(N_DEV, D // N_DEV, D)
                g_up = jnp.transpose(g_up.reshape(4, BW, N_DEV, D // N_DEV), (2, 0, 1, 3)).reshape(
                    N_DEV, 4 * BW, D // N_DEV)
                res, (R["w_out", l], R["w_up", l]) = _mixers_bwd(
                    sv["proj"], sv["y1"], dy, dproj, pool_w[l], pool_wt[l], row(pool_scale, l), sconv_full[l], cconv_full[l],
                    row(cconv_ln_g, l), row(cconv_ln_b, l), row(sgu_ln_g, l), row(sgu_ln_b, l), sgu_w[l], sgu_wt[l],
                    sgu_bias[l], "mixers_bwd", _scatter_comm([g_out, g_up]))
                dproj = res[0]
                (sg["pool_w"], sg["pool_scale"], sg["sconv_w"], sg["cconv_w"], sg["cconv_ln_g"], sg["cconv_ln_b"],
                 sg["sgu_ln_g"], sg["sgu_ln_b"], sg["sgu_w"], dgb) = res[1:]
                sg["sgu_b"] = dgb[:, :, 0]
                comm = None
                if l == 0:
                    early = [w for w in widths if w != D]
                    comm = _gather_comm([pack(w) for w in early], [(b, None) for b in range(len(early))])
                g_in, extra = _matmul_tn(sv["h_mix"][None], dproj, N_DEV, "w_in_grad", comm)
                if l == 0:
                    for w, g in zip(early, extra):
                        gathered_small[widths.index(w)] = g
                dx, sg["mix_norm"], (r_in,) = _matmul_nt_normbwd(
                    dproj, W["w_in", l], sv["x_mix"], row(mix_norm, l), dx, "proj_bwd",
                    _scatter_comm([g_in], PEERS_SAME_CORE))
                second.append((("w_in", l), g_in, r_in, (PEERS_NEAR_OTHER, PEERS_FAR_OTHER)))
            else:
                keys, comm = rest_of_sends()
                dab, sh, extra = _ffn_bwd_hidden(dx, W[tag + "_w2", l], sv["ab_" + tag], "ffn_hidden_bwd", comm)
                settle(keys, extra)
                keys, comm = rest_of_sends()
                g_w2, extra = _matmul_tn(sh[None], dx[None], 1, "ffn_w2_grad", comm)
                settle(keys, extra)
                g_w2 = g_w2.reshape(N_DEV, F // N_DEV, D)
                g_w13, (R[tag + "_w2", l],) = _matmul_tn(dab, sv["h_" + tag][None], 1, "ffn_w13_grad",
                                                         _scatter_comm([g_w2]), b_shared=True)
                g_w13 = g_w13.reshape(N_DEV, fs, D)
                last = tag == "ffn1" and l == 0
                now, later = (PEERS_BUT_NEAR_OTHER, PEERS_NEAR_OTHER) if last else (PEERS_SAME_CORE, PEERS_OTHER_CORE)
                dx, sg[tag + "_norm"], (r_w13,) = _matmul_nt_normbwd(
                    dab, W[tag + "_w13", l], sv["x_" + tag], row(P[tag + "_norm"], l), dx, "ffn_up_bwd",
                    _scatter_comm([g_w13], now), w_t=True)
                second.append(((tag + "_w13", l), g_w13, r_w13, (later,)))
    grad_x = dx[None]
    out = {}

    def as2d(n, a):
        if n == "final_norm":
            return a.reshape(1, D)
        return a.reshape(-1, a.shape[-1])

    for i, n in enumerate(["w_out", "w_up", "ffn2_w13", "w_in", "ffn2_w2", "ffn1_w2", "ffn1_w13"]):
        shp = P[n].shape
        if n.endswith("w13"):
            flat, back = (lambda a: jnp.swapaxes(a, 1, 2)), (lambda a: jnp.swapaxes(a, 1, 2))
        else:
            rows, cols = math.prod(shp[1:-1]), shp[-1]
            flat, back = (lambda a: a.reshape(L, rows, cols)), (lambda a: a.reshape(shp))
        keys, comm = [], None
        if i == 0:
            keys, comm = rest_of_sends()
        elif i == 1:
            comm = _gather_comm([pack(D)], [(0, None)])
        res, extra = _adamw_sharded([R[n, l] for l in range(L)], flat(P[n]), flat(P["m_" + n]), flat(P["v_" + n]),
                                    "adamw_sharded", comm)
        if i == 0:
            settle(keys, extra)
        elif i == 1:
            gathered_small[widths.index(D)] = extra[0]
        out[n] = tuple(back(a) for a in res)

    res = _adamw_replicated(gathered_small, [layout[n] for n in small_names],
                            [(as2d(n, P[n]), as2d(n, P["m_" + n]), as2d(n, P["v_" + n])) for n in small_names],
                            "adamw_replicated")
    for p, n in enumerate(small_names):
        out[n] = tuple(a.reshape(P[n].shape) for a in res[4 * p:4 * p + 4])
    conv_sum = res[4 * len(small_names) + conv_at["b"]][conv_at["r0"]:conv_at["r0"] + conv_at["rows"]]
    conv_mine = lax.dynamic_slice_in_dim(conv_sum, me * cw, cw, axis=1)

    def conv2d(a, b):
        return jnp.concatenate([a, b], axis=1).reshape(L * (SCONV_K + CCONV_K), cw)

    cd, cm, cv = _adamw_small(conv_mine, conv2d(sconv_w, cconv_w), conv2d(m_sconv_w, m_cconv_w),
                              conv2d(v_sconv_w, v_cconv_w), "adamw_conv")
    for n, sl in (("sconv_w", slice(0, SCONV_K)), ("cconv_w", slice(SCONV_K, SCONV_K + CCONV_K))):
        out[n] = tuple(a.reshape(L, SCONV_K + CCONV_K, cw)[:, sl] for a in (conv_mine, cd, cm, cv))

    order = ["ffn1_norm", "ffn1_w13", "ffn1_w2", "mix_norm", "w_in", "pool_w", "pool_scale", "sconv_w", "cconv_w",
             "cconv_ln_g", "cconv_ln_b", "sgu_ln_g", "sgu_ln_b", "sgu_w", "sgu_b", "w_up", "w_out", "ffn2_norm",
             "ffn2_w13", "ffn2_w2", "final_norm"]
    return (loss, grad_x, *[out[n][0] for n in order], *[out[n][1] for n in order],
            *[out[n][2] for n in order], *[out[n][3] for n in order])
```

```python
import functools
import math

import jax
import jax.numpy as jnp
from jax import lax
from jax.experimental import pallas as pl
from jax.experimental.pallas import tpu as pltpu

F32 = jnp.float32
BF16 = jnp.bfloat16
EPS = 1e-6
ADAM_LR = 0.001
ADAM_B1 = 0.9
ADAM_B2 = 0.999
ADAM_EPS = 1e-08
ADAM_WD = 0.01
ADAM_STEP = 10
SGU_BLOCK = 128
SGU_CHUNK = 64
SCONV_K = 3
CCONV_K = 31
HALO = 32
V7X_VMEM_LIMIT = 48 * 1024 * 1024
LANES = 128
BF16_ROWS = 16
WIDE = 11 * LANES
ROWS_L, ROWS_M, ROWS_S = 1024, 512, 256
MESH_AXES = ("x", "y", "c")
N_DEV = 8
_GELU_C0 = math.sqrt(2.0 / math.pi)
_GELU_C1 = 0.044715

BS = pl.BlockSpec
SDS = jax.ShapeDtypeStruct
ANY = pl.BlockSpec(memory_space=pl.ANY)


def _tile(n, pref, align=128):
    if n <= pref:
        return n
    t = pref - pref % align
    while t > 0:
        if n % t == 0:
            return t
        t -= align
    return n


def _sig(v):
    return 1.0 / (1.0 + jnp.exp(-v))


def _gelu(v):
    t = jnp.tanh(_GELU_C0 * (v + _GELU_C1 * (v * v * v)))
    return 0.5 * v * (1.0 + t), t


def _gelu_grad(v, t):
    return 0.5 * (1.0 + t) + 0.5 * v * (1.0 - t * t) * (_GELU_C0 * (1.0 + 3.0 * _GELU_C1 * v * v))


def _ln_stats(v):
    mu = jnp.mean(v, axis=-1, keepdims=True)
    vc = v - mu
    var = jnp.mean(vc * vc, axis=-1, keepdims=True)
    rstd = lax.rsqrt(var + EPS)
    return vc * rstd, rstd


def _ln_bwd(dvh, vh, rstd):
    return rstd * (dvh - jnp.mean(dvh, axis=-1, keepdims=True) - vh * jnp.mean(dvh * vh, axis=-1, keepdims=True))


def _dot(a, b):
    return jnp.dot(a, b, preferred_element_type=F32)


def _dot_nt(a, b):
    return lax.dot_general(a, b, (((1,), (1,)), ((), ())), preferred_element_type=F32)


def _dot_tn(a, b):
    return lax.dot_general(a, b, (((0,), (0,)), ((), ())), preferred_element_type=F32)


def _mesh_pos():
    return lax.axis_index("x"), lax.axis_index("y"), lax.axis_index("c")


class _Comm:
    def __init__(self, ins, out_shape, sems, start, finish, aliases=None, middle=None):
        self.ins, self.out_shape, self.sems, self.start, self.finish = ins, out_shape, sems, start, finish
        self.aliases = aliases or {}
        self.middle = middle


def _gather_comm(shards, units):
    n_u = len(units)
    out_shape = []
    for t, l in units:
        shp = shards[t].shape if l is None else shards[t].shape[1:]
        out_shape.append(SDS((N_DEV,) + tuple(shp), shards[t].dtype))

    def upper_rows(o):
        shp = out_shape[o].shape[1:]
        assert shp[0] >= 2
        return shp[0] // 2 if len(shp) > 2 or shp[0] < 32 else shp[0] // 32 * 16

    def tools(ins, dsts, sems):
        send_sems, recv_sems, local_sems = sems
        x, y, c = _mesh_pos()
        me, sib = (x, y, c), (x, y, 1 - c)
        xn, yn, dg = (1 - x, y, c), (x, 1 - y, c), (1 - x, 1 - y, c)

        def src_of(o):
            t, l = units[o]
            return ins[t] if l is None else ins[t].at[l]

        def row(o, p, part=None):
            r = dsts[o].at[4 * p[0] + 2 * p[1] + p[2]]
            if part is None:
                return r
            h = upper_rows(o)
            return r.at[pl.ds(0, h)] if part == "upper" else r.at[pl.ds(h, out_shape[o].shape[1] - h)]

        def copy(o, k, src, dst, to):
            return pltpu.make_async_remote_copy(
                src_ref=src, dst_ref=dst, send_sem=send_sems.at[o * 8 + k], recv_sem=recv_sems.at[o * 8 + k],
                device_id=to, device_id_type=pl.DeviceIdType.MESH)

        def send(o, k):
            if k < 3:
                return copy(o, k, src_of(o), row(o, me), (sib, xn, yn)[k])
            if k == 3:
                return copy(o, k, row(o, xn, "upper"), row(o, xn, "upper"), yn)
            if k == 4:
                return copy(o, k, row(o, yn, "lower"), row(o, yn, "lower"), xn)
            blk = (xn, yn, dg)[k - 5]
            return copy(o, k, row(o, blk), row(o, blk), sib)

        def landed(o, k):
            def other(p):
                return (p[0], p[1], 1 - c)

            dst = (row(o, sib), row(o, xn), row(o, yn), row(o, dg, "upper"), row(o, dg, "lower"),
                   row(o, other(xn)), row(o, other(yn)), row(o, other(dg)))[k]
            return copy(o, k, dst, dst, me)

        def local(o):
            return pltpu.make_async_copy(src_of(o), row(o, me), local_sems.at[o])

        return send, landed, local

    def start(ins, dsts, sems):
        send, _, local = tools(ins, dsts, sems)
        for o in range(n_u):
            local(o).start()
            for k in (1, 2, 0):
                send(o, k).start()

    def middle(ins, dsts, sems):
        send, landed, _ = tools(ins, dsts, sems)
        for o in range(n_u):
            landed(o, 1).wait_recv()
            send(o, 3).start()
            landed(o, 2).wait_recv()
            send(o, 4).start()
            send(o, 5).start()
            send(o, 6).start()

    def finish(ins, dsts, sems):
        send, landed, local = tools(ins, dsts, sems)
        for o in range(n_u):
            landed(o, 3).wait_recv()
            landed(o, 4).wait_recv()
            send(o, 7).start()
        for o in range(n_u):
            for k in (0, 5, 6, 7):
                landed(o, k).wait_recv()
        for o in range(n_u):
            for k in range(8):
                send(o, k).wait_send()
            local(o).wait()

    sems = [pltpu.SemaphoreType.DMA((8 * n_u,)), pltpu.SemaphoreType.DMA((8 * n_u,)), pltpu.SemaphoreType.DMA((n_u,))]
    return _Comm(list(shards), out_shape, sems, start, finish, middle=middle)


PEERS_ALL = (1, 2, 3, 4, 5, 6, 7)
PEERS_SAME_CORE = (1, 2, 4, 6)
PEERS_OTHER_CORE = (3, 5, 7)
PEERS_BUT_NEAR_OTHER = (1, 2, 4, 6, 7)
PEERS_NEAR_OTHER = (3, 5)
PEERS_FAR_OTHER = (7,)


def _scatter_comm(parts, peers=PEERS_ALL, into=None):
    n_u = len(parts)

    def tools(ins, dsts, sems):
        send_sems, recv_sems, local_sems = sems
        x, y, c = _mesh_pos()
        me = 4 * x + 2 * y + c

        def peer(k):
            return ((x + ((k >> 2) & 1)) % 2, (y + ((k >> 1) & 1)) % 2, (c + (k & 1)) % 2)

        def copy(u, k, wait=False):
            p = peer(k)
            pi = 4 * p[0] + 2 * p[1] + p[2]
            return pltpu.make_async_remote_copy(
                src_ref=ins[u].at[pi], dst_ref=dsts[u].at[pi if wait else me],
                send_sem=send_sems.at[u * 7 + k - 1], recv_sem=recv_sems.at[u * 7 + k - 1],
                device_id=p, device_id_type=pl.DeviceIdType.MESH)

        def local(u):
            return pltpu.make_async_copy(ins[u].at[me], dsts[u].at[me], local_sems.at[u])

        return copy, local

    def start(ins, dsts, sems):
        copy, local = tools(ins, dsts, sems)
        for u in range(n_u):
            if into is None:
                local(u).start()
            for k in peers:
                copy(u, k).start()

    def finish(ins, dsts, sems):
        copy, local = tools(ins, dsts, sems)
        for u in range(n_u):
            for k in peers:
                copy(u, k, wait=True).wait()
            if into is None:
                local(u).wait()

    sems = [pltpu.SemaphoreType.DMA((7 * n_u,)), pltpu.SemaphoreType.DMA((7 * n_u,)), pltpu.SemaphoreType.DMA((n_u,))]
    aliases = {} if into is None else {n_u + u: u for u in range(n_u)}
    return _Comm(list(parts) + list(into or []), [SDS(p.shape, p.dtype) for p in parts], sems, start, finish, aliases)


def _pcall(body, name, grid, in_specs, out_specs, out_shape, scratch, sem, args, comm=None, aliases=None):
    n_i, n_o, n_s = len(in_specs), len(out_specs), len(scratch)
    aliases = aliases or {}
    if comm is None:
        res = pl.pallas_call(
            body, name=name, grid=grid, in_specs=in_specs, out_specs=out_specs, out_shape=out_shape,
            scratch_shapes=scratch, input_output_aliases=aliases,
            compiler_params=pltpu.CompilerParams(dimension_semantics=sem, vmem_limit_bytes=V7X_VMEM_LIMIT),
        )(*args)
        return res, []
    n_ci, n_co = len(comm.ins), len(comm.out_shape)

    def wrapped(*refs):
        ins = refs[:n_i]
        cins = refs[n_i:n_i + n_ci]
        outs = refs[n_i + n_ci:n_i + n_ci + n_o]
        couts = refs[n_i + n_ci + n_o:n_i + n_ci + n_o + n_co]
        rest = refs[n_i + n_ci + n_o + n_co:]
        step = 0
        for d, g in enumerate(grid):
            step = step * g + pl.program_id(d)
        n_steps = math.prod(grid)
        mid = (n_steps * 5) // 8
        staged = comm.middle is not None and 0 < mid < n_steps - 1

        @pl.when(step == 0)
        def _():
            comm.start(cins, couts, rest[n_s:])

        if staged:
            @pl.when(step == mid)
            def _():
                comm.middle(cins, couts, rest[n_s:])

        body(*ins, *outs, *rest[:n_s])

        @pl.when(step == n_steps - 1)
        def _():
            if comm.middle is not None and not staged:
                comm.middle(cins, couts, rest[n_s:])
            comm.finish(cins, couts, rest[n_s:])

    res = pl.pallas_call(
        wrapped, name=name, grid=grid, in_specs=list(in_specs) + [ANY] * n_ci,
        out_specs=list(out_specs) + [ANY] * n_co, out_shape=list(out_shape) + list(comm.out_shape),
        scratch_shapes=list(scratch) + list(comm.sems),
        input_output_aliases={**aliases, **{n_i + ci: n_o + co for ci, co in comm.aliases.items()}},
        compiler_params=pltpu.CompilerParams(dimension_semantics=("arbitrary",) * len(grid),
                                             vmem_limit_bytes=V7X_VMEM_LIMIT),
    )(*args, *comm.ins)
    return res[:n_o], res[n_o:]


def _rmsnorm_fwd(x, g, name, comm=None):
    S, D = x.shape
    tm = _tile(S, ROWS_M, BF16_ROWS)

    def body(x_ref, g_ref, h_ref):
        h_ref[...] = _rmsnorm_rows(x_ref[...], g_ref[...])

    (h,), extra = _pcall(body, name, (S // tm,), [BS((tm, D), lambda i: (i, 0)), BS((1, D), lambda i: (0, 0))],
                         [BS((tm, D), lambda i: (i, 0))], [SDS((S, D), BF16)], [], ("parallel",), (x, g), comm)
    return h, extra


def _matmul_fwd(a, w, name, comm=None, w_t=False, out_dtype=F32):
    S, K = a.shape
    C = w.shape[0]
    Fc = w.shape[1] if w_t else w.shape[2]
    tn = _tile(Fc, WIDE)
    tm = _tile(S, 2 * ROWS_L, BF16_ROWS)

    def body(a_ref, w_ref, o_ref):
        p = _dot_nt(a_ref[...], w_ref[0]) if w_t else _dot(a_ref[...], w_ref[0])
        o_ref[0] = p.astype(out_dtype)

    w_spec = BS((1, tn, K), lambda c, n, i: (c, n, 0)) if w_t else BS((1, K, tn), lambda c, n, i: (c, 0, n))
    (o,), extra = _pcall(
        body, name, (C, Fc // tn, S // tm), [BS((tm, K), lambda c, n, i: (i, 0)), w_spec],
        [BS((1, tm, tn), lambda c, n, i: (c, i, n))], [SDS((C, S, Fc), out_dtype)], [],
        ("parallel", "parallel", "parallel"), (a, w), comm)
    return o, extra


def _rmsnorm_rows(xv, g):
    r = lax.rsqrt(jnp.mean(xv * xv, axis=-1, keepdims=True) + EPS)
    return (xv * r * g).astype(BF16)


def _swiglu_down(ab, w2, x, g_next, name, comm=None):
    _, S, F = ab.shape
    D = w2.shape[1]
    tk = _tile(F, WIDE)
    tm = _tile(S, ROWS_M, BF16_ROWS)
    nk = F // tk

    def body(ab_ref, w_ref, x_ref, g_ref, o_ref, h_ref):
        k = pl.program_id(1)
        a = ab_ref[0].astype(F32)
        s = a * _sig(a) * ab_ref[1].astype(F32)
        p = 0.5 * _dot(s.astype(BF16), w_ref[...])

        @pl.when(k == 0)
        def _():
            o_ref[...] = x_ref[...] + p

        @pl.when(k > 0)
        def _():
            o_ref[...] += p

        @pl.when(k == nk - 1)
        def _():
            h_ref[...] = _rmsnorm_rows(o_ref[...], g_ref[...])

    (o, h), extra = _pcall(
        body, name, (S // tm, nk),
        [BS((2, tm, tk), lambda i, k: (0, i, k)), BS((tk, D), lambda i, k: (k, 0)), BS((tm, D), lambda i, k: (i, 0)),
         BS((1, D), lambda i, k: (0, 0))],
        [BS((tm, D), lambda i, k: (i, 0)), BS((tm, D), lambda i, k: (i, 0))],
        [SDS((S, D), F32), SDS((S, D), BF16)], [], ("parallel", "arbitrary"), (ab, w2, x, g_next), comm)
    return o, h, extra


def _ffn_bwd_hidden(dy, w2, ab, name, comm=None):
    S, D = dy.shape
    F = w2.shape[0]
    tk = _tile(F, WIDE)
    tm = _tile(S, ROWS_M, BF16_ROWS)
    te = _tile(tm, 128, 16)

    def body(dy_ref, w_ref, ab_ref, dab_ref, s_ref, ds_ref):
        ds_ref[...] = 0.5 * _dot_nt(dy_ref[...].astype(BF16), w_ref[...])
        for r0 in range(0, tm, te):
            rows = slice(r0, r0 + te)
            ds = ds_ref[rows, :]
            a = ab_ref[0, rows, :].astype(F32)
            b = ab_ref[1, rows, :].astype(F32)
            sg = _sig(a)
            sa = a * sg
            dab_ref[0, rows, :] = (ds * b * (sg * (1.0 + a * (1.0 - sg)))).astype(BF16)
            dab_ref[1, rows, :] = (ds * sa).astype(BF16)
            s_ref[rows, :] = (0.5 * (sa * b)).astype(BF16)

    (dab, sh), extra = _pcall(
        body, name, (F // tk, S // tm),
        [BS((tm, D), lambda k, i: (i, 0)), BS((tk, D), lambda k, i: (k, 0)), BS((2, tm, tk), lambda k, i: (0, i, k))],
        [BS((2, tm, tk), lambda k, i: (0, i, k)), BS((tm, tk), lambda k, i: (i, k))],
        [SDS((2, S, F), BF16), SDS((S, F), BF16)], [pltpu.VMEM((tm, tk), F32)], ("parallel", "parallel"),
        (dy, w2, ab), comm)
    return dab, sh, extra


def _matmul_tn(a, b, n_c, name, comm=None, b_shared=False):
    G, S, M = a.shape
    _, _, Fc = b.shape
    C = n_c
    tM = _tile(M, WIDE)
    tn = _tile(Fc, WIDE)
    ts = _tile(S, ROWS_L, BF16_ROWS)
    n_s = S // ts

    def body(a_ref, b_ref, o_ref, acc):
        s = pl.program_id(4)
        p = _dot_tn(a_ref[0].astype(BF16), b_ref[0].astype(BF16))

        @pl.when(s == 0)
        def _():
            acc[...] = p

        @pl.when(s > 0)
        def _():
            acc[...] += p

        @pl.when(s == n_s - 1)
        def _():
            o_ref[0] = acc[...].astype(BF16)

    (o,), extra = _pcall(
        body, name, (G, M // tM, C, Fc // tn, n_s),
        [BS((1, ts, tM), lambda g, m, c, n, s: (g, s, m)), BS((1, ts, tn), lambda g, m, c, n, s: (c if b_shared else g * C + c, s, n))],
        [BS((1, tM, tn), lambda g, m, c, n, s: (g * C + c, m, n))], [SDS((G * C, M, Fc), BF16)],
        [pltpu.VMEM((tM, tn), F32)], ("parallel", "parallel", "parallel", "parallel", "arbitrary"), (a, b), comm)
    return o, extra


def _matmul_nt_normbwd(b, w, x, gam, dres, name, comm=None, w_t=False):
    C, S, Fc = b.shape
    D = w.shape[2] if w_t else w.shape[1]
    tk = _tile(Fc, WIDE)
    tm = _tile(S, ROWS_L, BF16_ROWS)
    te = _tile(tm, 256, 8)
    nk = Fc // tk

    def body(b_ref, w_ref, x_ref, g_ref, r_ref, dx_ref, dg_ref):
        i, c, k = pl.program_id(0), pl.program_id(1), pl.program_id(2)
        p = _dot(b_ref[0], w_ref[0]) if w_t else _dot_nt(b_ref[0], w_ref[0])
        first = jnp.logical_and(c == 0, k == 0)

        @pl.when(first)
        def _():
            dx_ref[...] = p

        @pl.when(jnp.logical_not(first))
        def _():
            dx_ref[...] += p

        @pl.when(jnp.logical_and(c == C - 1, k == nk - 1))
        def _():
            dgp = None
            for r0 in range(0, tm, te):
                rows = slice(r0, r0 + te)
                xv = x_ref[rows, :]
                r = lax.rsqrt(jnp.mean(xv * xv, axis=-1, keepdims=True) + EPS)
                xn = xv * r
                dh = dx_ref[rows, :]
                dxn = dh * g_ref[...]
                dx_ref[rows, :] = r_ref[rows, :] + r * (dxn - xn * jnp.mean(dxn * xn, axis=-1, keepdims=True))
                t = jnp.sum(dh * xn, axis=0, keepdims=True)
                dgp = t if dgp is None else dgp + t

            @pl.when(i == 0)
            def _():
                dg_ref[...] = dgp

            @pl.when(i > 0)
            def _():
                dg_ref[...] += dgp

    once = dict(pipeline_mode=pl.Buffered(1))
    (dx, dg), extra = _pcall(
        body, name, (S // tm, C, nk),
        [BS((1, tm, tk), lambda i, c, k: (c, i, k)),
         BS((1, tk, D), lambda i, c, k: (c, k, 0)) if w_t else BS((1, D, tk), lambda i, c, k: (c, 0, k)),
         BS((tm, D), lambda i, c, k: (i, 0), **once), BS((1, D), lambda i, c, k: (0, 0)),
         BS((tm, D), lambda i, c, k: (i, 0), **once)],
        [BS((tm, D), lambda i, c, k: (i, 0)), BS((1, D), lambda i, c, k: (0, 0))],
        [SDS((S, D), F32), SDS((1, D), F32)], [],
        ("arbitrary", "arbitrary", "arbitrary"), (b, w, x, gam, dres), comm)
    return dx, dg, extra


def _final_loss(x, gam, target, name):
    S, D = x.shape
    tm = _tile(S, 512, 8)

    def body(x_ref, g_ref, t_ref, loss_ref, dx_ref, dg_ref):
        i = pl.program_id(0)
        xv = x_ref[...]
        r = lax.rsqrt(jnp.mean(xv * xv, axis=-1, keepdims=True) + EPS)
        xn = xv * r
        err = xn * g_ref[...] - t_ref[...]
        part = 0.5 * jnp.sum(jnp.mean(err * err, axis=-1, keepdims=True), axis=0, keepdims=True)
        dy = err * (1.0 / D)
        dxn = dy * g_ref[...]
        dx_ref[...] = r * (dxn - xn * jnp.mean(dxn * xn, axis=-1, keepdims=True))
        dgp = jnp.sum(dy * xn, axis=0, keepdims=True)
        lp = jnp.broadcast_to(part, loss_ref.shape)

        @pl.when(i == 0)
        def _():
            dg_ref[...] = dgp
            loss_ref[...] = lp

        @pl.when(i > 0)
        def _():
            dg_ref[...] += dgp
            loss_ref[...] += lp

    res, _ = _pcall(
        body, name, (S // tm,),
        [BS((tm, D), lambda i: (i, 0)), BS((1, D), lambda i: (0, 0)), BS((tm, D), lambda i: (i, 0))],
        [BS((8, 128), lambda i: (0, 0)), BS((tm, D), lambda i: (i, 0)), BS((1, D), lambda i: (0, 0))],
        [SDS((8, 128), F32), SDS((S, D), F32), SDS((1, D), F32)], [], ("arbitrary",), (x, gam, target))
    return res


CONV_CHUNK = 32


def _fill_shifted(rot, n):
    for b in range(1, 8):
        rot[b, 0:n - 8, :] = rot[0, b:b + n - 8, :]


def _window(rot, off, r0, rows):
    b = off % 8
    return rot[b, off - b + r0:off - b + r0 + rows, :]


def _taps(rot, w_ref, offs, n_rows, out):
    for r0 in range(0, n_rows, CONV_CHUNK):
        acc = None
        for k, off in enumerate(offs):
            t = w_ref[k:k + 1, :] * _window(rot, off, r0, CONV_CHUNK)
            acc = t if acc is None else acc + t
        out[r0:r0 + CONV_CHUNK, :] = acc


def _tap_grads(rot, offs, g_plane, n_rows, dw_ref):
    for k, off in enumerate(offs):
        acc = None
        for r0 in range(0, n_rows, CONV_CHUNK):
            p = g_plane[0, r0:r0 + CONV_CHUNK, :] * _window(rot, off, r0, CONV_CHUNK)
            acc = p if acc is None else acc + p
        dw_ref[k:k + 1, :] += jnp.sum(acc, axis=0, keepdims=True)


def _sgu_masks():
    ii = lax.broadcasted_iota(jnp.int32, (SGU_BLOCK, SGU_BLOCK), 0) // SGU_CHUNK
    jj = lax.broadcasted_iota(jnp.int32, (SGU_BLOCK, SGU_BLOCK), 1) // SGU_CHUNK
    return jj <= ii, ii <= jj


def _mixers_fwd(proj, pool_w, pool_scale, sconv_w, cconv_w, cln_g, cln_b, sln_g, sln_b, sgu_w, sgu_bias, name,
                comm=None):
    _, S, D = proj.shape
    BW = D // 2
    GW = BW // 4
    TS = _tile(S, ROWS_S, SGU_BLOCK)
    H = HALO
    hb = TS // H

    def main(blk, col):
        return BS((1, TS, BW), lambda i: (blk, i, col))

    def back(blk, col):
        return BS((1, H, BW), lambda i: (blk, jnp.maximum(i * hb - 1, 0), col))

    def full(a):
        nd = a.ndim
        return BS(a.shape, lambda i: (0,) * nd)

    def body(pa_m, pa_b, xi_m, xi_b, bg_m, cg_m, cg_b, ca_m, ca_b, cb_m, cb_b, du_m, dv_m,
             pw, ps, sw, cw, clg, clb, slg, slb, gw, gbias, y_ref, y1_ref, e1, e2, e3):
        i = pl.program_id(0)
        nb = jnp.where(i > 0, 1.0, 0.0).astype(F32)
        rows = i * TS + lax.broadcasted_iota(jnp.int32, (TS, 1), 0)

        e1[0:H, :] = pa_b[0] * nb
        e1[H:H + TS, :] = pa_m[0]
        for g in range(4):
            cols = slice(g * GW, (g + 1) * GW)
            win = 2 << g
            wsum = e1[H:H + TS, cols]
            for k in range(1, win):
                wsum = wsum + e1[H - k:H - k + TS, cols]
            cnt = jnp.minimum(rows + 1, win).astype(F32)
            d = wsum / cnt - e1[H:H + TS, cols]
            yg = _dot(d.astype(BF16), pw[g].astype(BF16)) * ps[:, cols]
            y_ref[0, :, cols] = yg.astype(BF16)

        e2[0:H, :] = cg_b[0] * xi_b[0] * nb
        e2[H:H + TS, :] = cg_m[0] * xi_m[0]
        cz = sw[0:1, :] * e2[H - 2:H - 2 + TS, :]
        for k in range(1, SCONV_K):
            cz = cz + sw[k:k + 1, :] * e2[H - 2 + k:H - 2 + k + TS, :]
        y_ref[1] = (bg_m[0] * cz).astype(BF16)

        e3[0, 0:H, :] = ca_b[0] * _sig(cb_b[0]) * nb
        e3[0, H:H + TS, :] = ca_m[0] * _sig(cb_m[0])
        _fill_shifted(e3, H + TS)
        _taps(e3, cw, [H - (CCONV_K - 1) + k for k in range(CCONV_K)], TS, y1_ref)
        yh, _ = _ln_stats(y1_ref[...])
        y2 = yh * clg[...] + clb[...]
        y_ref[2] = (y2 * _sig(y2)).astype(BF16)

        u, _ = _gelu(du_m[0])
        v, _ = _gelu(dv_m[0])
        vh, _ = _ln_stats(v)
        vn = vh * slg[...] + slb[...]
        mask, _ = _sgu_masks()
        for h in range(4):
            wm = jnp.where(mask, gw[h], 0.0).astype(BF16)
            cs = slice(h * GW, (h + 1) * GW)
            for n in range(TS // SGU_BLOCK):
                rs = slice(n * SGU_BLOCK, (n + 1) * SGU_BLOCK)
                z = _dot(wm, vn[rs, cs].astype(BF16)) + gbias[h]
                y_ref[3, rs, cs] = (u[rs, cs] * z).astype(BF16)

    args = [proj] * 13 + [pool_w, pool_scale, sconv_w, cconv_w, cln_g, cln_b, sln_g, sln_b, sgu_w, sgu_bias]
    in_specs = [main(0, 0), back(0, 0), main(0, 1), back(0, 1), main(1, 0), main(1, 1), back(1, 1),
                main(2, 0), back(2, 0), main(2, 1), back(2, 1), main(3, 0), main(3, 1)]
    in_specs += [full(a) for a in args[13:]]
    (y, y1), extra = _pcall(body, name, (S // TS,), in_specs,
                            [BS((4, TS, BW), lambda i: (0, i, 0)), BS((TS, BW), lambda i: (i, 0))],
                            [SDS((4, S, BW), BF16), SDS((S, BW), F32)],
                            [pltpu.VMEM((H + TS, BW), F32)] * 2 + [pltpu.VMEM((8, H + TS, BW), F32)], ("parallel",),
                            args, comm)
    return y, y1, extra


def _mixers_bwd(proj, y1, dy, dproj_gates, pool_w, pool_wt, pool_scale, sconv_w, cconv_w, cln_g, cln_b, sln_g, sln_b,
                sgu_w, sgu_wt, sgu_bias, name, comm=None):
    _, S, D = proj.shape
    BW = D // 2
    GW = BW // 4
    TS = _tile(S, ROWS_S, SGU_BLOCK)
    H = HALO
    hb = TS // H
    n_t = S // TS
    E = TS + H

    def main(blk, col):
        return BS((1, TS, BW), lambda i: (blk, i, col))

    def back(blk, col):
        return BS((1, H, BW), lambda i: (blk, jnp.maximum(i * hb - 1, 0), col))

    def front(blk, col):
        return BS((1, H, BW), lambda i: (blk, jnp.minimum((i + 1) * hb, S // H - 1), col))

    def full(a):
        nd = a.ndim
        return BS(a.shape, lambda i: (0,) * nd)

    def body(pa_b, pa_m, xi_b, xi_m, bg_m, bg_f, cg_b, cg_m, ca_b, ca_m, cb_b, cb_m, du_m, dv_m, y1_m, y1_f,
             dya_m, dya_f, dyb_m, dyb_f, dyc_m, dyc_f, dyd_m,
             pw, pwt, ps, sw, cw, clg, clb, slg, slb, gw, gwt, gbias, _gates_in,
             dp_ref, dpw, dps, dsw, dcw, dclg, dclb, dslg, dslb, dgw, dgb,
             e1, e2, e3, e4, e5, ra, rb):
        i = pl.program_id(0)
        nb = jnp.where(i > 0, 1.0, 0.0).astype(F32)
        nf = jnp.where(i < n_t - 1, 1.0, 0.0).astype(F32)
        rows_m = i * TS + lax.broadcasted_iota(jnp.int32, (TS, 1), 0)
        rows_e = i * TS + lax.broadcasted_iota(jnp.int32, (E, 1), 0)

        @pl.when(i == 0)
        def _():
            for r in (dpw, dps, dsw, dcw, dclg, dclb, dslg, dslb, dgw, dgb):
                r[...] = jnp.zeros(r.shape, F32)

        e1[0:H, :] = pa_b[0] * nb
        e1[H:H + TS, :] = pa_m[0]
        e2[0:TS, :] = dya_m[0] * ps[...]
        e2[TS:E, :] = dya_f[0] * ps[...] * nf
        for g in range(4):
            cols = slice(g * GW, (g + 1) * GW)
            win = 2 << g
            a_m = e1[H:H + TS, cols]
            wsum = a_m
            for k in range(1, win):
                wsum = wsum + e1[H - k:H - k + TS, cols]
            d = wsum / jnp.minimum(rows_m + 1, win).astype(F32) - a_m
            d16 = d.astype(BF16)
            dyp = e2[0:E, cols].astype(BF16)
            dd = _dot(dyp, pwt[g].astype(BF16))
            e3[0:E, cols] = dd / jnp.minimum(rows_e + 1, win).astype(F32)
            da = e3[0:TS, cols] - dd[0:TS]
            for k in range(1, win):
                da = da + e3[k:k + TS, cols]
            dp_ref[0, :, cols] = da.astype(BF16)
            ypre = _dot(d16, pw[g].astype(BF16))
            dps[:, cols] += jnp.sum(dya_m[0][:, cols] * ypre, axis=0, keepdims=True)
            dpw[g] += _dot(jnp.transpose(d).astype(BF16), dyp[0:TS])

        e4[0:H, :] = cg_b[0] * xi_b[0] * nb
        e4[H:H + TS, :] = cg_m[0] * xi_m[0]
        dyb = dyb_m[0]
        e5[0:TS, :] = dyb * bg_m[0]
        e5[TS:E, :] = dyb_f[0] * bg_f[0] * nf
        dcz = e5[0:TS, :]
        cz = None
        dz = None
        for k in range(SCONV_K):
            zk = e4[H - 2 + k:H - 2 + k + TS, :]
            wk = sw[k:k + 1, :]
            cz = wk * zk if cz is None else cz + wk * zk
            t = wk * e5[2 - k:2 - k + TS, :]
            dz = t if dz is None else dz + t
            dsw[k:k + 1, :] += jnp.sum(dcz * zk, axis=0, keepdims=True)
        dp_ref[0, :, BW:2 * BW] = (dz * cg_m[0]).astype(BF16)
        dp_ref[1, :, 0:BW] = (dyb * cz).astype(BF16)
        dp_ref[1, :, BW:2 * BW] = (dz * xi_m[0]).astype(BF16)

        sgm = _sig(cb_m[0])
        ra[0, 0:H, :] = ca_b[0] * _sig(cb_b[0]) * nb
        ra[0, H:H + TS, :] = ca_m[0] * sgm
        _fill_shifted(ra, H + TS)
        fwd_offs = [H - (CCONV_K - 1) + k for k in range(CCONV_K)]
        e4[0:TS, :] = y1_m[...]
        e4[TS:E, :] = y1_f[...]
        yh, rstd = _ln_stats(e4[0:E, :])
        y2 = yh * clg[...] + clb[...]
        s2 = _sig(y2)
        e1[0:TS, :] = dyc_m[0]
        e1[TS:E, :] = dyc_f[0] * nf
        dy2 = e1[0:E, :] * (s2 * (1.0 + y2 * (1.0 - s2)))
        dclg[...] += jnp.sum((dy2 * yh)[0:TS], axis=0, keepdims=True)
        dclb[...] += jnp.sum(dy2[0:TS], axis=0, keepdims=True)
        rb[0, 0:E, :] = _ln_bwd(dy2 * clg[...], yh, rstd)
        _fill_shifted(rb, E)
        _taps(rb, cw, [CCONV_K - 1 - k for k in range(CCONV_K)], TS, e5)
        _tap_grads(ra, fwd_offs, rb, TS, dcw)
        dy0 = e5[0:TS, :]
        dp_ref[2, :, 0:BW] = (dy0 * sgm).astype(BF16)
        dp_ref[2, :, BW:2 * BW] = (dy0 * ca_m[0] * (sgm * (1.0 - sgm))).astype(BF16)

        pu = du_m[0]
        pv = dv_m[0]
        u, tu = _gelu(pu)
        v, tv = _gelu(pv)
        vh, vr = _ln_stats(v)
        vn = vh * slg[...] + slb[...]
        dyd = dyd_m[0]
        mask, mask_t = _sgu_masks()
        for h in range(4):
            wm = jnp.where(mask, gw[h], 0.0).astype(BF16)
            wmt = jnp.where(mask_t, gwt[h], 0.0).astype(BF16)
            cs = slice(h * GW, (h + 1) * GW)
            for n in range(TS // SGU_BLOCK):
                rs = slice(n * SGU_BLOCK, (n + 1) * SGU_BLOCK)
                vb = vn[rs, cs].astype(BF16)
                z = _dot(wm, vb) + gbias[h]
                dzb = dyd[rs, cs] * u[rs, cs]
                dz16 = dzb.astype(BF16)
                e3[rs, cs] = dyd[rs, cs] * z
                e4[rs, cs] = _dot(wmt, dz16)
                dgw[h] += jnp.where(mask, _dot_nt(dz16, vb), 0.0)
                dgb[h] += dzb
        dvn = e4[0:TS, :]
        dslg[...] += jnp.sum(dvn * vh, axis=0, keepdims=True)
        dslb[...] += jnp.sum(dvn, axis=0, keepdims=True)
        dv = _ln_bwd(dvn * slg[...], vh, vr)
        dp_ref[3, :, 0:BW] = (e3[0:TS, :] * _gelu_grad(pu, tu)).astype(BF16)
        dp_ref[3, :, BW:2 * BW] = (dv * _gelu_grad(pv, tv)).astype(BF16)

        @pl.when(i == n_t - 1)
        def _():
            for h in range(4):
                dgb[h] = jnp.broadcast_to(jnp.sum(dgb[h], axis=1, keepdims=True), dgb.shape[1:])

    params = [pool_w, pool_wt, pool_scale, sconv_w, cconv_w, cln_g, cln_b, sln_g, sln_b, sgu_w, sgu_wt, sgu_bias]
    args = [proj] * 14 + [y1] * 2 + [dy] * 7 + params + [dproj_gates]
    in_specs = [back(0, 0), main(0, 0), back(0, 1), main(0, 1), main(1, 0), front(1, 0), back(1, 1), main(1, 1),
                back(2, 0), main(2, 0), back(2, 1), main(2, 1), main(3, 0), main(3, 1),
                BS((TS, BW), lambda i: (i, 0)), BS((H, BW), lambda i: (jnp.minimum((i + 1) * hb, S // H - 1), 0)),
                main(0, 0), front(0, 0), main(1, 0), front(1, 0), main(2, 0), front(2, 0), main(3, 0)]
    in_specs += [full(a) for a in params] + [ANY]
    small = [SDS(pool_w.shape, F32), SDS(pool_scale.shape, F32), SDS(sconv_w.shape, F32), SDS(cconv_w.shape, F32),
             SDS(cln_g.shape, F32), SDS(cln_b.shape, F32), SDS(sln_g.shape, F32), SDS(sln_b.shape, F32),
             SDS(sgu_w.shape, F32), SDS(sgu_bias.shape, F32)]
    out_specs = [BS((4, TS, D), lambda i: (0, i, 0))] + [full(s) for s in small]
    return _pcall(body, name, (n_t,), in_specs, out_specs, [SDS(dproj_gates.shape, BF16)] + small,
                  [pltpu.VMEM((TS + 2 * H, BW), F32)] * 5 + [pltpu.VMEM((8, TS + 2 * H, BW), F32)] * 2,
                  ("arbitrary",), args, comm, aliases={len(args) - 1: 0})


def _merge_fwd(y, proj, w_up, w_out, x, g_next, name, comm=None):
    _, S, BW = y.shape
    D = x.shape[1]
    tm = _tile(S, ROWS_M, BF16_ROWS)

    def body(y_ref, pg_ref, wu_ref, wo_ref, x_ref, g_ref, o_ref, m_ref, h_ref):
        merged = None
        for g in range(4):
            t = _sig(pg_ref[g]) * _dot(y_ref[g], wu_ref[g])
            merged = t if merged is None else merged + t
        m16 = merged.astype(BF16)
        m_ref[...] = m16
        xn = x_ref[...] + _dot(m16, wo_ref[...])
        o_ref[...] = xn
        h_ref[...] = _rmsnorm_rows(xn, g_ref[...])

    once = dict(pipeline_mode=pl.Buffered(1))
    (o, m, h), extra = _pcall(
        body, name, (S // tm,),
        [BS((4, tm, BW), lambda i: (0, i, 0)), BS((4, tm, D), lambda i: (1, i, 0)),
         BS((4, BW, D), lambda i: (0, 0, 0), **once), BS((D, D), lambda i: (0, 0), **once),
         BS((tm, D), lambda i: (i, 0)), BS((1, D), lambda i: (0, 0))],
        [BS((tm, D), lambda i: (i, 0)), BS((tm, D), lambda i: (i, 0)), BS((tm, D), lambda i: (i, 0))],
        [SDS((S, D), F32), SDS((S, D), BF16), SDS((S, D), BF16)], [], ("parallel",),
        (y, proj, w_up, w_out, x, g_next), comm)
    return o, m, h, extra


def _merge_bwd(dx, y, proj, w_up, w_out, name, comm=None):
    _, S, BW = y.shape
    D = dx.shape[1]
    tm = _tile(S, ROWS_S, BF16_ROWS)

    def body(dx_ref, y_ref, pg_ref, wu_ref, wo_ref, dup_ref, dp_ref, dy_ref):
        dm = _dot_nt(dx_ref[...].astype(BF16), wo_ref[...])
        for g in range(4):
            gate = _sig(pg_ref[g])
            up = _dot(y_ref[g], wu_ref[g])
            dup = (dm * gate).astype(BF16)
            dup_ref[g] = dup
            dp_ref[g] = (dm * up * (gate * (1.0 - gate))).astype(BF16)
            dy_ref[g] = _dot_nt(dup, wu_ref[g])

    res, extra = _pcall(
        body, name, (S // tm,),
        [BS((tm, D), lambda i: (i, 0)), BS((4, tm, BW), lambda i: (0, i, 0)), BS((4, tm, D), lambda i: (1, i, 0)),
         BS((4, BW, D), lambda i: (0, 0, 0)), BS((D, D), lambda i: (0, 0))],
        [BS((4, tm, D), lambda i: (0, i, 0)), BS((4, tm, D), lambda i: (1, i, 0)), BS((4, tm, BW), lambda i: (0, i, 0))],
        [SDS((4, S, D), BF16), SDS((8, S, D), BF16), SDS((4, S, BW), F32)], [], ("parallel",),
        (dx, y, proj, w_up, w_out), comm)
    return res, extra


def _adamw(w, g, m, v):
    m = ADAM_B1 * m + (1.0 - ADAM_B1) * g
    v = ADAM_B2 * v + (1.0 - ADAM_B2) * (g * g)
    m_hat = m / (1.0 - ADAM_B1 ** ADAM_STEP)
    v_hat = v / (1.0 - ADAM_B2 ** ADAM_STEP)
    delta = -ADAM_LR * (m_hat / (jnp.sqrt(v_hat) + ADAM_EPS) + ADAM_WD * w)
    return delta, m, v


def _adamw_sharded(parts, w, m, v, name, comm=None):
    L, R, C = w.shape
    tr = _tile(R, ROWS_S, BF16_ROWS)

    def body(*refs):
        p_refs = refs[:L]
        w_ref, m_ref, v_ref, g_out, d_out, m_out, v_out = refs[L:]
        l = pl.program_id(0)
        g = None
        for d in range(N_DEV):
            t = p_refs[0][d].astype(F32)
            for j in range(1, L):
                t = jnp.where(l == j, p_refs[j][d].astype(F32), t)
            g = t if g is None else g + t
        dl, mn, vn = _adamw(w_ref[0], g, m_ref[0], v_ref[0])
        g_out[0] = g
        d_out[0] = dl
        m_out[0] = mn
        v_out[0] = vn

    def part_spec(j):
        return BS((N_DEV, tr, C), lambda l, r: (0, jnp.where(l == j, r, 0), 0))

    blk = BS((1, tr, C), lambda l, r: (l, r, 0))
    return _pcall(body, name, (L, R // tr), [part_spec(j) for j in range(L)] + [blk, blk, blk], [blk] * 4,
                  [SDS((L, R, C), F32)] * 4, [], ("parallel", "parallel"), (*parts, w, m, v), comm)


def _adamw_replicated(gathered, layout, wmv, name):
    n_b = len(gathered)
    n_p = len(layout)

    def body(*refs):
        bufs = refs[:n_b]
        prm = refs[n_b:n_b + 3 * n_p]
        outs = refs[n_b + 3 * n_p:n_b + 7 * n_p]
        sums = refs[n_b + 7 * n_p:]
        for b in range(n_b):
            s = bufs[b][0]
            for d in range(1, N_DEV):
                s = s + bufs[b][d]
            sums[b][...] = s
        for p, (b, r0, nr) in enumerate(layout):
            g = sums[b][r0:r0 + nr, :]
            d, mn, vn = _adamw(prm[3 * p][...], g, prm[3 * p + 1][...], prm[3 * p + 2][...])
            outs[4 * p][...] = g
            outs[4 * p + 1][...] = d
            outs[4 * p + 2][...] = mn
            outs[4 * p + 3][...] = vn

    flat = [a for t in wmv for a in t]
    out_shape = []
    for (w, _, _) in wmv:
        out_shape += [SDS(w.shape, F32)] * 4
    out_shape += [SDS(g.shape[1:], F32) for g in gathered]
    return pl.pallas_call(
        body, name=name, out_shape=out_shape,
        compiler_params=pltpu.CompilerParams(vmem_limit_bytes=V7X_VMEM_LIMIT),
    )(*gathered, *flat)


def _adamw_small(g, w, m, v, name):
    def body(g_ref, w_ref, m_ref, v_ref, d_out, m_out, v_out):
        d, mn, vn = _adamw(w_ref[...], g_ref[...], m_ref[...], v_ref[...])
        d_out[...] = d
        m_out[...] = mn
        v_out[...] = vn

    return pl.pallas_call(body, name=name, out_shape=[SDS(w.shape, F32)] * 3)(g, w, m, v)


def _pad_rows(a, rows):
    return jnp.pad(a, ((0, rows - a.shape[0]), (0, 0)))


def kernel(x, ffn1_norm, ffn1_w13, ffn1_w2, mix_norm, w_in, pool_w, pool_scale, sconv_w, cconv_w, cconv_ln_g, cconv_ln_b, sgu_ln_g, sgu_ln_b, sgu_w, sgu_b, w_up, w_out, ffn2_norm, ffn2_w13, ffn2_w2, final_norm, loss_target, m_ffn1_norm, m_ffn1_w13, m_ffn1_w2, m_mix_norm, m_w_in, m_pool_w, m_pool_scale, m_sconv_w, m_cconv_w, m_cconv_ln_g, m_cconv_ln_b, m_sgu_ln_g, m_sgu_ln_b, m_sgu_w, m_sgu_b, m_w_up, m_w_out, m_ffn2_norm, m_ffn2_w13, m_ffn2_w2, m_final_norm, v_ffn1_norm, v_ffn1_w13, v_ffn1_w2, v_mix_norm, v_w_in, v_pool_w, v_pool_scale, v_sconv_w, v_cconv_w, v_cconv_ln_g, v_cconv_ln_b, v_sgu_ln_g, v_sgu_ln_b, v_sgu_w, v_sgu_b, v_w_up, v_w_out, v_ffn2_norm, v_ffn2_w13, v_ffn2_w2, v_final_norm):
    P = dict(locals())
    L = ffn1_norm.shape[0]
    S, D = x.shape[1], x.shape[2]
    BW = D // 2
    GW = BW // 4
    F = ffn1_w2.shape[1] * N_DEV
    fs = ffn1_w13.shape[2]
    cw = sconv_w.shape[2]
    me = 4 * lax.axis_index("x") + 2 * lax.axis_index("y") + lax.axis_index("c")

    big = ["ffn1_w13", "ffn1_w2", "w_in", "w_up", "w_out", "ffn2_w13", "ffn2_w2"]
    shards = [(jnp.swapaxes(P[n], 1, 2) if n.endswith("w13") else P[n]).astype(BF16) for n in big]
    conv_local = jnp.concatenate([sconv_w, cconv_w], axis=1)

    def gather_of(units):
        return _gather_comm(shards, [(big.index(n), l) for n, l in units])

    def ready(n, g):
        if n.endswith("w13"):
            return g.reshape(2, F, D)
        if n.endswith("w2"):
            return g.reshape(F, D)
        if n == "w_up":
            return jnp.transpose(g, (1, 2, 0, 3)).reshape(4, BW, D)
        if n == "w_out":
            return g.reshape(D, D)
        return g

    W = {}

    def take(units, arrays):
        for (n, l), g in zip(units, arrays):
            W[n, l] = ready(n, g)

    first_units = [("ffn1_w13", 0), ("ffn1_w2", 0)]
    plan = {("ffn1_up", 0): [("w_in", 0)],
            ("ffn1_down", 0): [("w_up", 0), ("w_out", 0)],
            ("proj", 0): [("ffn2_w13", 0), ("ffn2_w2", 0)],
            ("mixers", 0): [("ffn1_w13", 1)], ("merge", 0): [("ffn1_w2", 1)],
            ("ffn2_up", 0): [("w_in", 1)], ("ffn2_down", 0): [("w_up", 1), ("w_out", 1)],
            ("ffn1_up", 1): [("ffn2_w13", 1)], ("ffn1_down", 1): [("ffn2_w2", 1)]}
    assert L <= 2

    def carried(key):
        units = [u for u in plan.get(key, []) if u[1] < L]
        return units, (gather_of(units) if units else None)

    first = _gather_comm(shards + [conv_local], [(big.index(n), l) for n, l in first_units] + [(len(big), None)])
    h, got = _rmsnorm_fwd(x[0], ffn1_norm[0][None, :], "first_norm_fwd", first)
    take(first_units, got[:2])
    conv_full = jnp.transpose(got[2], (1, 2, 0, 3)).reshape(L, SCONV_K + CCONV_K, N_DEV * cw)
    sconv_full = conv_full[:, :SCONV_K]
    cconv_full = conv_full[:, SCONV_K:]

    sgu_bias = jnp.broadcast_to(sgu_b[:, :, :, None], sgu_b.shape + (GW,))
    pool_wt = jnp.swapaxes(pool_w, 2, 3)
    sgu_wt = jnp.swapaxes(sgu_w, 2, 3)

    def row(a, l):
        return a[l][None, :]

    saved = []
    xc = x[0]
    for l in range(L):
        sv = {}
        for tag in ("ffn1", None, "ffn2"):
            if tag is None:
                sv["x_mix"] = xc
                units, comm = carried(("proj", l))
                proj, extra = _matmul_fwd(h, W["w_in", l], "proj_fwd", comm)
                take(units, extra)
                units, comm = carried(("mixers", l))
                y, sv["y1"], extra = _mixers_fwd(
                    proj, pool_w[l], row(pool_scale, l), sconv_full[l], cconv_full[l], row(cconv_ln_g, l),
                    row(cconv_ln_b, l), row(sgu_ln_g, l), row(sgu_ln_b, l), sgu_w[l], sgu_bias[l], "mixers_fwd", comm)
                take(units, extra)
                units, comm = carried(("merge", l))
                sv.update(h_mix=h, proj=proj, y=y)
                xc, sv["merged"], h, extra = _merge_fwd(y, proj, W["w_up", l], W["w_out", l], xc, row(ffn2_norm, l),
                                                        "merge_fwd", comm)
                take(units, extra)
            else:
                sv["x_" + tag] = xc
                units, comm = carried((tag + "_up", l))
                ab, extra = _matmul_fwd(h, W[tag + "_w13", l], "ffn_up_fwd", comm, w_t=True, out_dtype=BF16)
                take(units, extra)
                units, comm = carried((tag + "_down", l))
                sv.update({"h_" + tag: h, "ab_" + tag: ab})
                if tag == "ffn1":
                    g_next = row(mix_norm, l)
                else:
                    g_next = row(ffn1_norm, l + 1) if l + 1 < L else final_norm[None, :]
                xc, h, extra = _swiglu_down(ab, W[tag + "_w2", l], xc, g_next, "ffn_down_fwd", comm)
                take(units, extra)
        saved.append(sv)

    loss_part, dx, d_final = _final_loss(xc, final_norm[None, :], loss_target[0], "loss_head")
    loss = lax.psum(loss_part[0, 0], MESH_AXES)

    R = {}
    second = []

    def rest_of_sends():
        todo = list(second)
        second.clear()
        comm = None
        if todo:
            assert len({e[3][0] for e in todo}) == 1
            comm = _scatter_comm([e[1] for e in todo], todo[0][3][0], [e[2] for e in todo])
        return todo, comm

    def settle(todo, arrays):
        for (key, g, _, stages), a in zip(todo, arrays):
            if len(stages) > 1:
                second.append((key, g, a, stages[1:]))
            else:
                R[key] = a

    wide = ["ffn1_norm", "mix_norm", "ffn2_norm", "final_norm"]
    half = ["pool_scale", "cconv_ln_g", "cconv_ln_b", "sgu_ln_g", "sgu_ln_b"]
    narrow = ["pool_w", "sgu_w", "sgu_b"]
    small_names = wide + half + narrow
    small_g = [dict() for _ in range(L)]
    widths = []
    for n in small_names:
        if P[n].shape[-1] not in widths:
            widths.append(P[n].shape[-1])
    layout, conv_at = {}, {}

    def pack(width):
        def stack_layers(n):
            return jnp.stack([small_g[l][n] for l in range(L)], axis=0)

        parts, r0 = [], 0
        for n in small_names:
            if P[n].shape[-1] != width:
                continue
            g = d_final if n == "final_norm" else stack_layers(n).reshape(-1, width)
            layout[n] = (widths.index(width), r0, g.shape[0])
            parts.append(_pad_rows(g, -(-g.shape[0] // 8) * 8))
            r0 += parts[-1].shape[0]
        if width == N_DEV * cw:
            conv_g = jnp.concatenate([stack_layers("sconv_w"), stack_layers("cconv_w")], axis=1)
            conv_g = conv_g.reshape(L * (SCONV_K + CCONV_K), N_DEV * cw)
            conv_at.update(b=widths.index(width), r0=r0, rows=conv_g.shape[0])
            parts.append(_pad_rows(conv_g, -(-conv_g.shape[0] // 8) * 8))
        return jnp.concatenate(parts, axis=0)

    gathered_small = [None] * len(widths)
    for l in reversed(range(L)):
        sv = saved[l]
        sg = small_g[l]
        for tag in ("ffn2", None, "ffn1"):
            if tag is None:
                keys, comm = rest_of_sends()
                (dup, dproj, dy), extra = _merge_bwd(dx, sv["y"], sv["proj"], W["w_up", l], W["w_out", l],
                                                     "merge_bwd", comm)
                settle(keys, extra)
                g_out, _ = _matmul_tn(sv["merged"][None], dx[None], 1, "w_out_grad")
                g_up, _ = _matmul_tn(sv["y"], dup, 1, "w_up_grad")
                g_out = g_out.reshape(N_DEV, D // N_DEV, D)
                g_up = jnp.transpose(g_up.reshape(4, BW, N_DEV, D // N_DEV), (2, 0, 1, 3)).reshape(
                    N_DEV, 4 * BW, D // N_DEV)
                res, (R["w_out", l], R["w_up", l]) = _mixers_bwd(
                    sv["proj"], sv["y1"], dy, dproj, pool_w[l], pool_wt[l], row(pool_scale, l), sconv_full[l], cconv_full[l],
                    row(cconv_ln_g, l), row(cconv_ln_b, l), row(sgu_ln_g, l), row(sgu_ln_b, l), sgu_w[l], sgu_wt[l],
                    sgu_bias[l], "mixers_bwd", _scatter_comm([g_out, g_up]))
                dproj = res[0]
                (sg["pool_w"], sg["pool_scale"], sg["sconv_w"], sg["cconv_w"], sg["cconv_ln_g"], sg["cconv_ln_b"],
                 sg["sgu_ln_g"], sg["sgu_ln_b"], sg["sgu_w"], dgb) = res[1:]
                sg["sgu_b"] = dgb[:, :, 0]
                comm = None
                if l == 0:
                    early = [w for w in widths if w != D]
                    comm = _gather_comm([pack(w) for w in early], [(b, None) for b in range(len(early))])
                g_in, extra = _matmul_tn(sv["h_mix"][None], dproj, N_DEV, "w_in_grad", comm)
                if l == 0:
                    for w, g in zip(early, extra):
                        gathered_small[widths.index(w)] = g
                dx, sg["mix_norm"], (r_in,) = _matmul_nt_normbwd(
                    dproj, W["w_in", l], sv["x_mix"], row(mix_norm, l), dx, "proj_bwd",
                    _scatter_comm([g_in], PEERS_SAME_CORE))
                second.append((("w_in", l), g_in, r_in, (PEERS_NEAR_OTHER, PEERS_FAR_OTHER)))
            else:
                keys, comm = rest_of_sends()
                dab, sh, extra = _ffn_bwd_hidden(dx, W[tag + "_w2", l], sv["ab_" + tag], "ffn_hidden_bwd", comm)
                settle(keys, extra)
                keys, comm = rest_of_sends()
                g_w2, extra = _matmul_tn(sh[None], dx[None], 1, "ffn_w2_grad", comm)
                settle(keys, extra)
                g_w2 = g_w2.reshape(N_DEV, F // N_DEV, D)
                g_w13, (R[tag + "_w2", l],) = _matmul_tn(dab, sv["h_" + tag][None], 1, "ffn_w13_grad",
                                                         _scatter_comm([g_w2]), b_shared=True)
                g_w13 = g_w13.reshape(N_DEV, fs, D)
                last = tag == "ffn1" and l == 0
                now, later = (PEERS_BUT_NEAR_OTHER, PEERS_NEAR_OTHER) if last else (PEERS_SAME_CORE, PEERS_OTHER_CORE)
                dx, sg[tag + "_norm"], (r_w13,) = _matmul_nt_normbwd(
                    dab, W[tag + "_w13", l], sv["x_" + tag], row(P[tag + "_norm"], l), dx, "ffn_up_bwd",
                    _scatter_comm([g_w13], now), w_t=True)
                second.append(((tag + "_w13", l), g_w13, r_w13, (later,)))
    grad_x = dx[None]
    out = {}

    def as2d(n, a):
        if n == "final_norm":
            return a.reshape(1, D)
        return a.reshape(-1, a.shape[-1])

    for i, n in enumerate(["w_out", "w_up", "ffn2_w13", "w_in", "ffn2_w2", "ffn1_w2", "ffn1_w13"]):
        shp = P[n].shape
        if n.endswith("w13"):
            flat, back = (lambda a: jnp.swapaxes(a, 1, 2)), (lambda a: jnp.swapaxes(a, 1, 2))
        else:
            rows, cols = math.prod(shp[1:-1]), shp[-1]
            flat, back = (lambda a: a.reshape(L, rows, cols)), (lambda a: a.reshape(shp))
        keys, comm = [], None
        if i == 0:
            keys, comm = rest_of_sends()
        elif i == 1:
            comm = _gather_comm([pack(D)], [(0, None)])
        res, extra = _adamw_sharded([R[n, l] for l in range(L)], flat(P[n]), flat(P["m_" + n]), flat(P["v_" + n]),
                                    "adamw_sharded", comm)
        if i == 0:
            settle(keys, extra)
        elif i == 1:
            gathered_small[widths.index(D)] = extra[0]
        out[n] = tuple(back(a) for a in res)

    res = _adamw_replicated(gathered_small, [layout[n] for n in small_names],
                            [(as2d(n, P[n]), as2d(n, P["m_" + n]), as2d(n, P["v_" + n])) for n in small_names],
                            "adamw_replicated")
    for p, n in enumerate(small_names):
        out[n] = tuple(a.reshape(P[n].shape) for a in res[4 * p:4 * p + 4])
    conv_sum = res[4 * len(small_names) + conv_at["b"]][conv_at["r0"]:conv_at["r0"] + conv_at["rows"]]
    conv_mine = lax.dynamic_slice_in_dim(conv_sum, me * cw, cw, axis=1)

    def conv2d(a, b):
        return jnp.concatenate([a, b], axis=1).reshape(L * (SCONV_K + CCONV_K), cw)

    cd, cm, cv = _adamw_small(conv_mine, conv2d(sconv_w, cconv_w), conv2d(m_sconv_w, m_cconv_w),
                              conv2d(v_sconv_w, v_cconv_w), "adamw_conv")
    for n, sl in (("sconv_w", slice(0, SCONV_K)), ("cconv_w", slice(SCONV_K, SCONV_K + CCONV_K))):
        out[n] = tuple(a.reshape(L, SCONV_K + CCONV_K, cw)[:, sl] for a in (conv_mine, cd, cm, cv))

    order = ["ffn1_norm", "ffn1_w13", "ffn1_w2", "mix_norm", "w_in", "pool_w", "pool_scale", "sconv_w", "cconv_w",
             "cconv_ln_g", "cconv_ln_b", "sgu_ln_g", "sgu_ln_b", "sgu_w", "sgu_b", "w_up", "w_out", "ffn2_norm",
             "ffn2_w13", "ffn2_w2", "final_norm"]
    return (loss, grad_x, *[out[n][0] for n in order], *[out[n][1] for n in order],
            *[out[n][2] for n in order], *[out[n][3] for n in order])
```

```python
import functools
import math

import jax
import jax.numpy as jnp
from jax import lax
from jax.experimental import pallas as pl
from jax.experimental.pallas import tpu as pltpu

F32 = jnp.float32
BF16 = jnp.bfloat16
EPS = 1e-6
ADAM_LR = 0.001
ADAM_B1 = 0.9
ADAM_B2 = 0.999
ADAM_EPS = 1e-08
ADAM_WD = 0.01
ADAM_STEP = 10
SGU_BLOCK = 128
SGU_CHUNK = 64
SCONV_K = 3
CCONV_K = 31
HALO = 32
V7X_VMEM_LIMIT = 48 * 1024 * 1024
LANES = 128
BF16_ROWS = 16
WIDE = 11 * LANES
ROWS_L, ROWS_M, ROWS_S = 1024, 512, 256
ELEMENTWISE_ROWS = 32
MESH_AXES = ("x", "y", "c")
N_DEV = 8
_GELU_C0 = math.sqrt(2.0 / math.pi)
_GELU_C1 = 0.044715

BS = pl.BlockSpec
SDS = jax.ShapeDtypeStruct
ANY = pl.BlockSpec(memory_space=pl.ANY)


def _tile(n, pref, align=128):
    if n <= pref:
        return n
    t = pref - pref % align
    while t > 0:
        if n % t == 0:
            return t
        t -= align
    return n


def _sig(v):
    return 1.0 / (1.0 + jnp.exp(-v))


def _gelu(v):
    t = jnp.tanh(_GELU_C0 * (v + _GELU_C1 * (v * v * v)))
    return 0.5 * v * (1.0 + t), t


def _gelu_grad(v, t):
    return 0.5 * (1.0 + t) + 0.5 * v * (1.0 - t * t) * (_GELU_C0 * (1.0 + 3.0 * _GELU_C1 * v * v))


def _ln_stats(v):
    mu = jnp.mean(v, axis=-1, keepdims=True)
    vc = v - mu
    var = jnp.mean(vc * vc, axis=-1, keepdims=True)
    rstd = lax.rsqrt(var + EPS)
    return vc * rstd, rstd


def _ln_bwd(dvh, vh, rstd):
    return rstd * (dvh - jnp.mean(dvh, axis=-1, keepdims=True) - vh * jnp.mean(dvh * vh, axis=-1, keepdims=True))


def _dot(a, b):
    return jnp.dot(a, b, preferred_element_type=F32)


def _dot_nt(a, b):
    return lax.dot_general(a, b, (((1,), (1,)), ((), ())), preferred_element_type=F32)


def _dot_tn(a, b):
    return lax.dot_general(a, b, (((0,), (0,)), ((), ())), preferred_element_type=F32)


def _mesh_pos():
    return lax.axis_index("x"), lax.axis_index("y"), lax.axis_index("c")


class _Comm:
    def __init__(self, ins, out_shape, sems, start, finish, aliases=None, middle=None):
        self.ins, self.out_shape, self.sems, self.start, self.finish = ins, out_shape, sems, start, finish
        self.aliases = aliases or {}
        self.middle = middle


def _gather_comm(shards, units):
    n_u = len(units)
    out_shape = []
    for t, l in units:
        shp = shards[t].shape if l is None else shards[t].shape[1:]
        out_shape.append(SDS((N_DEV,) + tuple(shp), shards[t].dtype))

    def upper_rows(o):
        shp = out_shape[o].shape[1:]
        assert shp[0] >= 2
        return shp[0] // 2 if len(shp) > 2 or shp[0] < 32 else shp[0] // 32 * 16

    def tools(ins, dsts, sems):
        send_sems, recv_sems, local_sems = sems
        x, y, c = _mesh_pos()
        me, sib = (x, y, c), (x, y, 1 - c)
        xn, yn, dg = (1 - x, y, c), (x, 1 - y, c), (1 - x, 1 - y, c)

        def src_of(o):
            t, l = units[o]
            return ins[t] if l is None else ins[t].at[l]

        def row(o, p, part=None):
            r = dsts[o].at[4 * p[0] + 2 * p[1] + p[2]]
            if part is None:
                return r
            h = upper_rows(o)
            return r.at[pl.ds(0, h)] if part == "upper" else r.at[pl.ds(h, out_shape[o].shape[1] - h)]

        def copy(o, k, src, dst, to):
            return pltpu.make_async_remote_copy(
                src_ref=src, dst_ref=dst, send_sem=send_sems.at[o * 8 + k], recv_sem=recv_sems.at[o * 8 + k],
                device_id=to, device_id_type=pl.DeviceIdType.MESH)

        def send(o, k):
            if k < 3:
                return copy(o, k, src_of(o), row(o, me), (sib, xn, yn)[k])
            if k == 3:
                return copy(o, k, row(o, xn, "upper"), row(o, xn, "upper"), yn)
            if k == 4:
                return copy(o, k, row(o, yn, "lower"), row(o, yn, "lower"), xn)
            blk = (xn, yn, dg)[k - 5]
            return copy(o, k, row(o, blk), row(o, blk), sib)

        def landed(o, k):
            def other(p):
                return (p[0], p[1], 1 - c)

            dst = (row(o, sib), row(o, xn), row(o, yn), row(o, dg, "upper"), row(o, dg, "lower"),
                   row(o, other(xn)), row(o, other(yn)), row(o, other(dg)))[k]
            return copy(o, k, dst, dst, me)

        def local(o):
            return pltpu.make_async_copy(src_of(o), row(o, me), local_sems.at[o])

        return send, landed, local

    def start(ins, dsts, sems):
        send, _, local = tools(ins, dsts, sems)
        for o in range(n_u):
            local(o).start()
            for k in (1, 2, 0):
                send(o, k).start()

    def middle(ins, dsts, sems):
        send, landed, _ = tools(ins, dsts, sems)
        for o in range(n_u):
            landed(o, 1).wait_recv()
            send(o, 3).start()
            landed(o, 2).wait_recv()
            send(o, 4).start()
            send(o, 5).start()
            send(o, 6).start()

    def finish(ins, dsts, sems):
        send, landed, local = tools(ins, dsts, sems)
        for o in range(n_u):
            landed(o, 3).wait_recv()
            landed(o, 4).wait_recv()
            send(o, 7).start()
        for o in range(n_u):
            for k in (0, 5, 6, 7):
                landed(o, k).wait_recv()
        for o in range(n_u):
            for k in range(8):
                send(o, k).wait_send()
            local(o).wait()

    sems = [pltpu.SemaphoreType.DMA((8 * n_u,)), pltpu.SemaphoreType.DMA((8 * n_u,)), pltpu.SemaphoreType.DMA((n_u,))]
    return _Comm(list(shards), out_shape, sems, start, finish, middle=middle)


PEERS_ALL = (1, 2, 3, 4, 5, 6, 7)
PEERS_SAME_CORE = (1, 2, 4, 6)
PEERS_OTHER_CORE = (3, 5, 7)
PEERS_BUT_NEAR_OTHER = (1, 2, 4, 6, 7)
PEERS_NEAR_OTHER = (3, 5)
PEERS_FAR_OTHER = (7,)


def _scatter_comm(parts, peers=PEERS_ALL, into=None):
    n_u = len(parts)

    def tools(ins, dsts, sems):
        send_sems, recv_sems, local_sems = sems
        x, y, c = _mesh_pos()
        me = 4 * x + 2 * y + c

        def peer(k):
            return ((x + ((k >> 2) & 1)) % 2, (y + ((k >> 1) & 1)) % 2, (c + (k & 1)) % 2)

        def copy(u, k, wait=False):
            p = peer(k)
            pi = 4 * p[0] + 2 * p[1] + p[2]
            return pltpu.make_async_remote_copy(
                src_ref=ins[u].at[pi], dst_ref=dsts[u].at[pi if wait else me],
                send_sem=send_sems.at[u * 7 + k - 1], recv_sem=recv_sems.at[u * 7 + k - 1],
                device_id=p, device_id_type=pl.DeviceIdType.MESH)

        def local(u):
            return pltpu.make_async_copy(ins[u].at[me], dsts[u].at[me], local_sems.at[u])

        return copy, local

    def start(ins, dsts, sems):
        copy, local = tools(ins, dsts, sems)
        for u in range(n_u):
            if into is None:
                local(u).start()
            for k in peers:
                copy(u, k).start()

    def finish(ins, dsts, sems):
        copy, local = tools(ins, dsts, sems)
        for u in range(n_u):
            for k in peers:
                copy(u, k, wait=True).wait()
            if into is None:
                local(u).wait()

    sems = [pltpu.SemaphoreType.DMA((7 * n_u,)), pltpu.SemaphoreType.DMA((7 * n_u,)), pltpu.SemaphoreType.DMA((n_u,))]
    aliases = {} if into is None else {n_u + u: u for u in range(n_u)}
    return _Comm(list(parts) + list(into or []), [SDS(p.shape, p.dtype) for p in parts], sems, start, finish, aliases)


def _pcall(body, name, grid, in_specs, out_specs, out_shape, scratch, sem, args, comm=None, aliases=None):
    n_i, n_o, n_s = len(in_specs), len(out_specs), len(scratch)
    aliases = aliases or {}
    if comm is None:
        res = pl.pallas_call(
            body, name=name, grid=grid, in_specs=in_specs, out_specs=out_specs, out_shape=out_shape,
            scratch_shapes=scratch, input_output_aliases=aliases,
            compiler_params=pltpu.CompilerParams(dimension_semantics=sem, vmem_limit_bytes=V7X_VMEM_LIMIT),
        )(*args)
        return res, []
    n_ci, n_co = len(comm.ins), len(comm.out_shape)

    def wrapped(*refs):
        ins = refs[:n_i]
        cins = refs[n_i:n_i + n_ci]
        outs = refs[n_i + n_ci:n_i + n_ci + n_o]
        couts = refs[n_i + n_ci + n_o:n_i + n_ci + n_o + n_co]
        rest = refs[n_i + n_ci + n_o + n_co:]
        step = 0
        for d, g in enumerate(grid):
            step = step * g + pl.program_id(d)
        n_steps = math.prod(grid)
        mid = (n_steps * 5) // 8
        staged = comm.middle is not None and 0 < mid < n_steps - 1

        @pl.when(step == 0)
        def _():
            comm.start(cins, couts, rest[n_s:])

        if staged:
            @pl.when(step == mid)
            def _():
                comm.middle(cins, couts, rest[n_s:])

        body(*ins, *outs, *rest[:n_s])

        @pl.when(step == n_steps - 1)
        def _():
            if comm.middle is not None and not staged:
                comm.middle(cins, couts, rest[n_s:])
            comm.finish(cins, couts, rest[n_s:])

    res = pl.pallas_call(
        wrapped, name=name, grid=grid, in_specs=list(in_specs) + [ANY] * n_ci,
        out_specs=list(out_specs) + [ANY] * n_co, out_shape=list(out_shape) + list(comm.out_shape),
        scratch_shapes=list(scratch) + list(comm.sems),
        input_output_aliases={**aliases, **{n_i + ci: n_o + co for ci, co in comm.aliases.items()}},
        compiler_params=pltpu.CompilerParams(dimension_semantics=("arbitrary",) * len(grid),
                                             vmem_limit_bytes=V7X_VMEM_LIMIT),
    )(*args, *comm.ins)
    return res[:n_o], res[n_o:]


def _rmsnorm_fwd(x, g, name, comm=None):
    S, D = x.shape
    tm = _tile(S, ROWS_M, BF16_ROWS)

    def body(x_ref, g_ref, h_ref):
        h_ref[...] = _rmsnorm_rows(x_ref[...], g_ref[...])

    (h,), extra = _pcall(body, name, (S // tm,), [BS((tm, D), lambda i: (i, 0)), BS((1, D), lambda i: (0, 0))],
                         [BS((tm, D), lambda i: (i, 0))], [SDS((S, D), BF16)], [], ("parallel",), (x, g), comm)
    return h, extra


def _matmul_fwd(a, w, name, comm=None, w_t=False, out_dtype=F32):
    S, K = a.shape
    C = w.shape[0]
    Fc = w.shape[1] if w_t else w.shape[2]
    tn = _tile(Fc, WIDE)
    tm = _tile(S, 2 * ROWS_L, BF16_ROWS)

    def body(a_ref, w_ref, o_ref):
        p = _dot_nt(a_ref[...], w_ref[0]) if w_t else _dot(a_ref[...], w_ref[0])
        o_ref[0] = p.astype(out_dtype)

    w_spec = BS((1, tn, K), lambda c, n, i: (c, n, 0)) if w_t else BS((1, K, tn), lambda c, n, i: (c, 0, n))
    (o,), extra = _pcall(
        body, name, (C, Fc // tn, S // tm), [BS((tm, K), lambda c, n, i: (i, 0)), w_spec],
        [BS((1, tm, tn), lambda c, n, i: (c, i, n))], [SDS((C, S, Fc), out_dtype)], [],
        ("parallel", "parallel", "parallel"), (a, w), comm)
    return o, extra


def _rmsnorm_rows(xv, g):
    r = lax.rsqrt(jnp.mean(xv * xv, axis=-1, keepdims=True) + EPS)
    return (xv * r * g).astype(BF16)


def _swiglu_down(ab, w2, x, g_next, name, comm=None):
    _, S, F = ab.shape
    D = w2.shape[1]
    tk = _tile(F, WIDE)
    tm = _tile(S, ROWS_M, BF16_ROWS)
    nk = F // tk

    te = _tile(tm, ELEMENTWISE_ROWS, BF16_ROWS)

    def body(ab_ref, w_ref, x_ref, g_ref, o_ref, h_ref, s_ref):
        k = pl.program_id(1)
        for r0 in range(0, tm, te):
            rows = slice(r0, r0 + te)
            a = ab_ref[0, rows, :].astype(F32)
            s_ref[rows, :] = (a * _sig(a) * ab_ref[1, rows, :].astype(F32)).astype(BF16)
        p = 0.5 * _dot(s_ref[...], w_ref[...])

        @pl.when(k == 0)
        def _():
            o_ref[...] = x_ref[...] + p

        @pl.when(k > 0)
        def _():
            o_ref[...] += p

        @pl.when(k == nk - 1)
        def _():
            h_ref[...] = _rmsnorm_rows(o_ref[...], g_ref[...])

    (o, h), extra = _pcall(
        body, name, (S // tm, nk),
        [BS((2, tm, tk), lambda i, k: (0, i, k)), BS((tk, D), lambda i, k: (k, 0)), BS((tm, D), lambda i, k: (i, 0)),
         BS((1, D), lambda i, k: (0, 0))],
        [BS((tm, D), lambda i, k: (i, 0)), BS((tm, D), lambda i, k: (i, 0))],
        [SDS((S, D), F32), SDS((S, D), BF16)], [pltpu.VMEM((tm, tk), BF16)], ("parallel", "arbitrary"),
        (ab, w2, x, g_next), comm)
    return o, h, extra


def _ffn_bwd_hidden(dy, w2, ab, name, comm=None):
    S, D = dy.shape
    F = w2.shape[0]
    tk = _tile(F, WIDE)
    tm = _tile(S, ROWS_M, BF16_ROWS)
    te = _tile(tm, ELEMENTWISE_ROWS, BF16_ROWS)

    def body(dy_ref, w_ref, ab_ref, dab_ref, s_ref, ds_ref):
        ds_ref[...] = 0.5 * _dot_nt(dy_ref[...].astype(BF16), w_ref[...])
        for r0 in range(0, tm, te):
            rows = slice(r0, r0 + te)
            ds = ds_ref[rows, :]
            a = ab_ref[0, rows, :].astype(F32)
            b = ab_ref[1, rows, :].astype(F32)
            sg = _sig(a)
            sa = a * sg
            dab_ref[0, rows, :] = (ds * b * (sg * (1.0 + a * (1.0 - sg)))).astype(BF16)
            dab_ref[1, rows, :] = (ds * sa).astype(BF16)
            s_ref[rows, :] = (0.5 * (sa * b)).astype(BF16)

    (dab, sh), extra = _pcall(
        body, name, (F // tk, S // tm),
        [BS((tm, D), lambda k, i: (i, 0)), BS((tk, D), lambda k, i: (k, 0)), BS((2, tm, tk), lambda k, i: (0, i, k))],
        [BS((2, tm, tk), lambda k, i: (0, i, k)), BS((tm, tk), lambda k, i: (i, k))],
        [SDS((2, S, F), BF16), SDS((S, F), BF16)], [pltpu.VMEM((tm, tk), F32)], ("parallel", "parallel"),
        (dy, w2, ab), comm)
    return dab, sh, extra


def _matmul_tn(a, b, n_c, name, comm=None, b_shared=False):
    G, S, M = a.shape
    _, _, Fc = b.shape
    C = n_c
    tM = _tile(M, WIDE)
    tn = _tile(Fc, WIDE)
    ts = _tile(S, 2 * ROWS_L if b.dtype == BF16 else ROWS_L, BF16_ROWS)
    n_s = S // ts

    def body(a_ref, b_ref, o_ref, acc):
        s = pl.program_id(4)
        p = _dot_tn(a_ref[0].astype(BF16), b_ref[0].astype(BF16))

        @pl.when(s == 0)
        def _():
            acc[...] = p

        @pl.when(s > 0)
        def _():
            acc[...] += p

        @pl.when(s == n_s - 1)
        def _():
            o_ref[0] = acc[...].astype(BF16)

    (o,), extra = _pcall(
        body, name, (G, M // tM, C, Fc // tn, n_s),
        [BS((1, ts, tM), lambda g, m, c, n, s: (g, s, m)), BS((1, ts, tn), lambda g, m, c, n, s: (c if b_shared else g * C + c, s, n))],
        [BS((1, tM, tn), lambda g, m, c, n, s: (g * C + c, m, n))], [SDS((G * C, M, Fc), BF16)],
        [pltpu.VMEM((tM, tn), F32)], ("parallel", "parallel", "parallel", "parallel", "arbitrary"), (a, b), comm)
    return o, extra


def _matmul_nt_normbwd(b, w, x, gam, dres, name, comm=None, w_t=False):
    C, S, Fc = b.shape
    D = w.shape[2] if w_t else w.shape[1]
    tk = _tile(Fc, WIDE)
    tm = _tile(S, ROWS_L, BF16_ROWS)
    te = _tile(tm, 256, 8)
    nk = Fc // tk

    def body(b_ref, w_ref, x_ref, g_ref, r_ref, dx_ref, dg_ref):
        i, c, k = pl.program_id(0), pl.program_id(1), pl.program_id(2)
        p = _dot(b_ref[0], w_ref[0]) if w_t else _dot_nt(b_ref[0], w_ref[0])
        first = jnp.logical_and(c == 0, k == 0)

        @pl.when(first)
        def _():
            dx_ref[...] = p

        @pl.when(jnp.logical_not(first))
        def _():
            dx_ref[...] += p

        @pl.when(jnp.logical_and(c == C - 1, k == nk - 1))
        def _():
            dgp = None
            for r0 in range(0, tm, te):
                rows = slice(r0, r0 + te)
                xv = x_ref[rows, :]
                r = lax.rsqrt(jnp.mean(xv * xv, axis=-1, keepdims=True) + EPS)
                xn = xv * r
                dh = dx_ref[rows, :]
                dxn = dh * g_ref[...]
                dx_ref[rows, :] = r_ref[rows, :] + r * (dxn - xn * jnp.mean(dxn * xn, axis=-1, keepdims=True))
                t = jnp.sum(dh * xn, axis=0, keepdims=True)
                dgp = t if dgp is None else dgp + t

            @pl.when(i == 0)
            def _():
                dg_ref[...] = dgp

            @pl.when(i > 0)
            def _():
                dg_ref[...] += dgp

    once = dict(pipeline_mode=pl.Buffered(1))
    (dx, dg), extra = _pcall(
        body, name, (S // tm, C, nk),
        [BS((1, tm, tk), lambda i, c, k: (c, i, k)),
         BS((1, tk, D), lambda i, c, k: (c, k, 0)) if w_t else BS((1, D, tk), lambda i, c, k: (c, 0, k)),
         BS((tm, D), lambda i, c, k: (i, 0), **once), BS((1, D), lambda i, c, k: (0, 0)),
         BS((tm, D), lambda i, c, k: (i, 0), **once)],
        [BS((tm, D), lambda i, c, k: (i, 0)), BS((1, D), lambda i, c, k: (0, 0))],
        [SDS((S, D), F32), SDS((1, D), F32)], [],
        ("arbitrary", "arbitrary", "arbitrary"), (b, w, x, gam, dres), comm)
    return dx, dg, extra


def _final_loss(x, gam, target, name):
    S, D = x.shape
    tm = _tile(S, 512, 8)

    def body(x_ref, g_ref, t_ref, loss_ref, dx_ref, dg_ref):
        i = pl.program_id(0)
        xv = x_ref[...]
        r = lax.rsqrt(jnp.mean(xv * xv, axis=-1, keepdims=True) + EPS)
        xn = xv * r
        err = xn * g_ref[...] - t_ref[...]
        part = 0.5 * jnp.sum(jnp.mean(err * err, axis=-1, keepdims=True), axis=0, keepdims=True)
        dy = err * (1.0 / D)
        dxn = dy * g_ref[...]
        dx_ref[...] = r * (dxn - xn * jnp.mean(dxn * xn, axis=-1, keepdims=True))
        dgp = jnp.sum(dy * xn, axis=0, keepdims=True)
        lp = jnp.broadcast_to(part, loss_ref.shape)

        @pl.when(i == 0)
        def _():
            dg_ref[...] = dgp
            loss_ref[...] = lp

        @pl.when(i > 0)
        def _():
            dg_ref[...] += dgp
            loss_ref[...] += lp

    res, _ = _pcall(
        body, name, (S // tm,),
        [BS((tm, D), lambda i: (i, 0)), BS((1, D), lambda i: (0, 0)), BS((tm, D), lambda i: (i, 0))],
        [BS((8, 128), lambda i: (0, 0)), BS((tm, D), lambda i: (i, 0)), BS((1, D), lambda i: (0, 0))],
        [SDS((8, 128), F32), SDS((S, D), F32), SDS((1, D), F32)], [], ("arbitrary",), (x, gam, target))
    return res


CONV_CHUNK = 32


def _fill_shifted(rot, n):
    for b in range(1, 8):
        rot[b, 0:n - 8, :] = rot[0, b:b + n - 8, :]


def _window(rot, off, r0, rows):
    b = off % 8
    return rot[b, off - b + r0:off - b + r0 + rows, :]


def _taps(rot, w_ref, offs, n_rows, out):
    for r0 in range(0, n_rows, CONV_CHUNK):
        acc = None
        for k, off in enumerate(offs):
            t = w_ref[k:k + 1, :] * _window(rot, off, r0, CONV_CHUNK)
            acc = t if acc is None else acc + t
        out[r0:r0 + CONV_CHUNK, :] = acc


def _tap_grads(rot, offs, g_plane, n_rows, dw_ref):
    for k, off in enumerate(offs):
        acc = None
        for r0 in range(0, n_rows, CONV_CHUNK):
            p = g_plane[0, r0:r0 + CONV_CHUNK, :] * _window(rot, off, r0, CONV_CHUNK)
            acc = p if acc is None else acc + p
        dw_ref[k:k + 1, :] += jnp.sum(acc, axis=0, keepdims=True)


def _sgu_masks():
    ii = lax.broadcasted_iota(jnp.int32, (SGU_BLOCK, SGU_BLOCK), 0) // SGU_CHUNK
    jj = lax.broadcasted_iota(jnp.int32, (SGU_BLOCK, SGU_BLOCK), 1) // SGU_CHUNK
    return jj <= ii, ii <= jj


def _mixers_fwd(proj, pool_w, pool_scale, sconv_w, cconv_w, cln_g, cln_b, sln_g, sln_b, sgu_w, sgu_bias, name,
                comm=None):
    _, S, D = proj.shape
    BW = D // 2
    GW = BW // 4
    TS = _tile(S, ROWS_S, SGU_BLOCK)
    H = HALO
    hb = TS // H

    def main(blk, col):
        return BS((1, TS, BW), lambda i: (blk, i, col))

    def back(blk, col):
        return BS((1, H, BW), lambda i: (blk, jnp.maximum(i * hb - 1, 0), col))

    def full(a):
        nd = a.ndim
        return BS(a.shape, lambda i: (0,) * nd)

    def body(pa_m, pa_b, xi_m, xi_b, bg_m, cg_m, cg_b, ca_m, ca_b, cb_m, cb_b, du_m, dv_m,
             pw, ps, sw, cw, clg, clb, slg, slb, gw, gbias, y_ref, y1_ref, e1, e2, e3):
        i = pl.program_id(0)
        nb = jnp.where(i > 0, 1.0, 0.0).astype(F32)
        rows = i * TS + lax.broadcasted_iota(jnp.int32, (TS, 1), 0)

        e1[0:H, :] = pa_b[0] * nb
        e1[H:H + TS, :] = pa_m[0]
        for g in range(4):
            cols = slice(g * GW, (g + 1) * GW)
            win = 2 << g
            wsum = e1[H:H + TS, cols]
            for k in range(1, win):
                wsum = wsum + e1[H - k:H - k + TS, cols]
            cnt = jnp.minimum(rows + 1, win).astype(F32)
            d = wsum / cnt - e1[H:H + TS, cols]
            yg = _dot(d.astype(BF16), pw[g].astype(BF16)) * ps[:, cols]
            y_ref[0, :, cols] = yg.astype(BF16)

        e2[0:H, :] = cg_b[0] * xi_b[0] * nb
        e2[H:H + TS, :] = cg_m[0] * xi_m[0]
        cz = sw[0:1, :] * e2[H - 2:H - 2 + TS, :]
        for k in range(1, SCONV_K):
            cz = cz + sw[k:k + 1, :] * e2[H - 2 + k:H - 2 + k + TS, :]
        y_ref[1] = (bg_m[0] * cz).astype(BF16)

        e3[0, 0:H, :] = ca_b[0] * _sig(cb_b[0]) * nb
        e3[0, H:H + TS, :] = ca_m[0] * _sig(cb_m[0])
        _fill_shifted(e3, H + TS)
        _taps(e3, cw, [H - (CCONV_K - 1) + k for k in range(CCONV_K)], TS, y1_ref)
        yh, _ = _ln_stats(y1_ref[...])
        y2 = yh * clg[...] + clb[...]
        y_ref[2] = (y2 * _sig(y2)).astype(BF16)

        u, _ = _gelu(du_m[0])
        v, _ = _gelu(dv_m[0])
        vh, _ = _ln_stats(v)
        vn = vh * slg[...] + slb[...]
        mask, _ = _sgu_masks()
        for h in range(4):
            wm = jnp.where(mask, gw[h], 0.0).astype(BF16)
            cs = slice(h * GW, (h + 1) * GW)
            for n in range(TS // SGU_BLOCK):
                rs = slice(n * SGU_BLOCK, (n + 1) * SGU_BLOCK)
                z = _dot(wm, vn[rs, cs].astype(BF16)) + gbias[h]
                y_ref[3, rs, cs] = (u[rs, cs] * z).astype(BF16)

    args = [proj] * 13 + [pool_w, pool_scale, sconv_w, cconv_w, cln_g, cln_b, sln_g, sln_b, sgu_w, sgu_bias]
    in_specs = [main(0, 0), back(0, 0), main(0, 1), back(0, 1), main(1, 0), main(1, 1), back(1, 1),
                main(2, 0), back(2, 0), main(2, 1), back(2, 1), main(3, 0), main(3, 1)]
    in_specs += [full(a) for a in args[13:]]
    (y, y1), extra = _pcall(body, name, (S // TS,), in_specs,
                            [BS((4, TS, BW), lambda i: (0, i, 0)), BS((TS, BW), lambda i: (i, 0))],
                            [SDS((4, S, BW), BF16), SDS((S, BW), F32)],
                            [pltpu.VMEM((H + TS, BW), F32)] * 2 + [pltpu.VMEM((8, H + TS, BW), F32)], ("parallel",),
                            args, comm)
    return y, y1, extra


def _mixers_bwd(proj, y1, dy, dproj_gates, pool_w, pool_wt, pool_scale, sconv_w, cconv_w, cln_g, cln_b, sln_g, sln_b,
                sgu_w, sgu_wt, sgu_bias, name, comm=None):
    _, S, D = proj.shape
    BW = D // 2
    GW = BW // 4
    TS = _tile(S, ROWS_S, SGU_BLOCK)
    H = HALO
    hb = TS // H
    n_t = S // TS
    E = TS + H

    def main(blk, col):
        return BS((1, TS, BW), lambda i: (blk, i, col))

    def back(blk, col):
        return BS((1, H, BW), lambda i: (blk, jnp.maximum(i * hb - 1, 0), col))

    def front(blk, col):
        return BS((1, H, BW), lambda i: (blk, jnp.minimum((i + 1) * hb, S // H - 1), col))

    def full(a):
        nd = a.ndim
        return BS(a.shape, lambda i: (0,) * nd)

    def body(pa_b, pa_m, xi_b, xi_m, bg_m, bg_f, cg_b, cg_m, ca_b, ca_m, cb_b, cb_m, du_m, dv_m, y1_m, y1_f,
             dya_m, dya_f, dyb_m, dyb_f, dyc_m, dyc_f, dyd_m,
             pw, pwt, ps, sw, cw, clg, clb, slg, slb, gw, gwt, gbias, _gates_in,
             dp_ref, dpw, dps, dsw, dcw, dclg, dclb, dslg, dslb, dgw, dgb,
             e1, e2, e3, e4, e5, ra, rb):
        i = pl.program_id(0)
        nb = jnp.where(i > 0, 1.0, 0.0).astype(F32)
        nf = jnp.where(i < n_t - 1, 1.0, 0.0).astype(F32)
        rows_m = i * TS + lax.broadcasted_iota(jnp.int32, (TS, 1), 0)
        rows_e = i * TS + lax.broadcasted_iota(jnp.int32, (E, 1), 0)

        @pl.when(i == 0)
        def _():
            for r in (dpw, dps, dsw, dcw, dclg, dclb, dslg, dslb, dgw, dgb):
                r[...] = jnp.zeros(r.shape, F32)

        e1[0:H, :] = pa_b[0] * nb
        e1[H:H + TS, :] = pa_m[0]
        e2[0:TS, :] = dya_m[0] * ps[...]
        e2[TS:E, :] = dya_f[0] * ps[...] * nf
        for g in range(4):
            cols = slice(g * GW, (g + 1) * GW)
            win = 2 << g
            a_m = e1[H:H + TS, cols]
            wsum = a_m
            for k in range(1, win):
                wsum = wsum + e1[H - k:H - k + TS, cols]
            d = wsum / jnp.minimum(rows_m + 1, win).astype(F32) - a_m
            d16 = d.astype(BF16)
            dyp = e2[0:E, cols].astype(BF16)
            dd = _dot(dyp, pwt[g].astype(BF16))
            e3[0:E, cols] = dd / jnp.minimum(rows_e + 1, win).astype(F32)
            da = e3[0:TS, cols] - dd[0:TS]
            for k in range(1, win):
                da = da + e3[k:k + TS, cols]
            dp_ref[0, :, cols] = da.astype(BF16)
            ypre = _dot(d16, pw[g].astype(BF16))
            dps[:, cols] += jnp.sum(dya_m[0][:, cols] * ypre, axis=0, keepdims=True)
            dpw[g] += _dot(jnp.transpose(d).astype(BF16), dyp[0:TS])

        e4[0:H, :] = cg_b[0] * xi_b[0] * nb
        e4[H:H + TS, :] = cg_m[0] * xi_m[0]
        dyb = dyb_m[0]
        e5[0:TS, :] = dyb * bg_m[0]
        e5[TS:E, :] = dyb_f[0] * bg_f[0] * nf
        dcz = e5[0:TS, :]
        cz = None
        dz = None
        for k in range(SCONV_K):
            zk = e4[H - 2 + k:H - 2 + k + TS, :]
            wk = sw[k:k + 1, :]
            cz = wk * zk if cz is None else cz + wk * zk
            t = wk * e5[2 - k:2 - k + TS, :]
            dz = t if dz is None else dz + t
            dsw[k:k + 1, :] += jnp.sum(dcz * zk, axis=0, keepdims=True)
        dp_ref[0, :, BW:2 * BW] = (dz * cg_m[0]).astype(BF16)
        dp_ref[1, :, 0:BW] = (dyb * cz).astype(BF16)
        dp_ref[1, :, BW:2 * BW] = (dz * xi_m[0]).astype(BF16)

        sgm = _sig(cb_m[0])
        ra[0, 0:H, :] = ca_b[0] * _sig(cb_b[0]) * nb
        ra[0, H:H + TS, :] = ca_m[0] * sgm
        _fill_shifted(ra, H + TS)
        fwd_offs = [H - (CCONV_K - 1) + k for k in range(CCONV_K)]
        e4[0:TS, :] = y1_m[...]
        e4[TS:E, :] = y1_f[...]
        yh, rstd = _ln_stats(e4[0:E, :])
        y2 = yh * clg[...] + clb[...]
        s2 = _sig(y2)
        e1[0:TS, :] = dyc_m[0]
        e1[TS:E, :] = dyc_f[0] * nf
        dy2 = e1[0:E, :] * (s2 * (1.0 + y2 * (1.0 - s2)))
        dclg[...] += jnp.sum((dy2 * yh)[0:TS], axis=0, keepdims=True)
        dclb[...] += jnp.sum(dy2[0:TS], axis=0, keepdims=True)
        rb[0, 0:E, :] = _ln_bwd(dy2 * clg[...], yh, rstd)
        _fill_shifted(rb, E)
        _taps(rb, cw, [CCONV_K - 1 - k for k in range(CCONV_K)], TS, e5)
        _tap_grads(ra, fwd_offs, rb, TS, dcw)
        dy0 = e5[0:TS, :]
        dp_ref[2, :, 0:BW] = (dy0 * sgm).astype(BF16)
        dp_ref[2, :, BW:2 * BW] = (dy0 * ca_m[0] * (sgm * (1.0 - sgm))).astype(BF16)

        pu = du_m[0]
        pv = dv_m[0]
        u, tu = _gelu(pu)
        v, tv = _gelu(pv)
        vh, vr = _ln_stats(v)
        vn = vh * slg[...] + slb[...]
        dyd = dyd_m[0]
        mask, mask_t = _sgu_masks()
        for h in range(4):
            wm = jnp.where(mask, gw[h], 0.0).astype(BF16)
            wmt = jnp.where(mask_t, gwt[h], 0.0).astype(BF16)
            cs = slice(h * GW, (h + 1) * GW)
            for n in range(TS // SGU_BLOCK):
                rs = slice(n * SGU_BLOCK, (n + 1) * SGU_BLOCK)
                vb = vn[rs, cs].astype(BF16)
                z = _dot(wm, vb) + gbias[h]
                dzb = dyd[rs, cs] * u[rs, cs]
                dz16 = dzb.astype(BF16)
                e3[rs, cs] = dyd[rs, cs] * z
                e4[rs, cs] = _dot(wmt, dz16)
                dgw[h] += jnp.where(mask, _dot_nt(dz16, vb), 0.0)
                dgb[h] += dzb
        dvn = e4[0:TS, :]
        dslg[...] += jnp.sum(dvn * vh, axis=0, keepdims=True)
        dslb[...] += jnp.sum(dvn, axis=0, keepdims=True)
        dv = _ln_bwd(dvn * slg[...], vh, vr)
        dp_ref[3, :, 0:BW] = (e3[0:TS, :] * _gelu_grad(pu, tu)).astype(BF16)
        dp_ref[3, :, BW:2 * BW] = (dv * _gelu_grad(pv, tv)).astype(BF16)

        @pl.when(i == n_t - 1)
        def _():
            for h in range(4):
                dgb[h] = jnp.broadcast_to(jnp.sum(dgb[h], axis=1, keepdims=True), dgb.shape[1:])

    params = [pool_w, pool_wt, pool_scale, sconv_w, cconv_w, cln_g, cln_b, sln_g, sln_b, sgu_w, sgu_wt, sgu_bias]
    args = [proj] * 14 + [y1] * 2 + [dy] * 7 + params + [dproj_gates]
    in_specs = [back(0, 0), main(0, 0), back(0, 1), main(0, 1), main(1, 0), front(1, 0), back(1, 1), main(1, 1),
                back(2, 0), main(2, 0), back(2, 1), main(2, 1), main(3, 0), main(3, 1),
                BS((TS, BW), lambda i: (i, 0)), BS((H, BW), lambda i: (jnp.minimum((i + 1) * hb, S // H - 1), 0)),
                main(0, 0), front(0, 0), main(1, 0), front(1, 0), main(2, 0), front(2, 0), main(3, 0)]
    in_specs += [full(a) for a in params] + [ANY]
    small = [SDS(pool_w.shape, F32), SDS(pool_scale.shape, F32), SDS(sconv_w.shape, F32), SDS(cconv_w.shape, F32),
             SDS(cln_g.shape, F32), SDS(cln_b.shape, F32), SDS(sln_g.shape, F32), SDS(sln_b.shape, F32),
             SDS(sgu_w.shape, F32), SDS(sgu_bias.shape, F32)]
    out_specs = [BS((4, TS, D), lambda i: (0, i, 0))] + [full(s) for s in small]
    return _pcall(body, name, (n_t,), in_specs, out_specs, [SDS(dproj_gates.shape, BF16)] + small,
                  [pltpu.VMEM((TS + 2 * H, BW), F32)] * 5 + [pltpu.VMEM((8, TS + 2 * H, BW), F32)] * 2,
                  ("arbitrary",), args, comm, aliases={len(args) - 1: 0})


def _merge_fwd(y, proj, w_up, w_out, x, g_next, name, comm=None):
    _, S, BW = y.shape
    D = x.shape[1]
    tm = _tile(S, ROWS_M, BF16_ROWS)

    def body(y_ref, pg_ref, wu_ref, wo_ref, x_ref, g_ref, o_ref, m_ref, h_ref):
        merged = None
        for g in range(4):
            t = _sig(pg_ref[g]) * _dot(y_ref[g], wu_ref[g])
            merged = t if merged is None else merged + t
        m16 = merged.astype(BF16)
        m_ref[...] = m16
        xn = x_ref[...] + _dot(m16, wo_ref[...])
        o_ref[...] = xn
        h_ref[...] = _rmsnorm_rows(xn, g_ref[...])

    once = dict(pipeline_mode=pl.Buffered(1))
    (o, m, h), extra = _pcall(
        body, name, (S // tm,),
        [BS((4, tm, BW), lambda i: (0, i, 0)), BS((4, tm, D), lambda i: (1, i, 0)),
         BS((4, BW, D), lambda i: (0, 0, 0), **once), BS((D, D), lambda i: (0, 0), **once),
         BS((tm, D), lambda i: (i, 0)), BS((1, D), lambda i: (0, 0))],
        [BS((tm, D), lambda i: (i, 0)), BS((tm, D), lambda i: (i, 0)), BS((tm, D), lambda i: (i, 0))],
        [SDS((S, D), F32), SDS((S, D), BF16), SDS((S, D), BF16)], [], ("parallel",),
        (y, proj, w_up, w_out, x, g_next), comm)
    return o, m, h, extra


def _merge_bwd(dx, y, proj, w_up, w_out, name, comm=None):
    _, S, BW = y.shape
    D = dx.shape[1]
    tm = _tile(S, ROWS_S, BF16_ROWS)

    def body(dx_ref, y_ref, pg_ref, wu_ref, wo_ref, dup_ref, dp_ref, dy_ref):
        dm = _dot_nt(dx_ref[...].astype(BF16), wo_ref[...])
        for g in range(4):
            gate = _sig(pg_ref[g])
            up = _dot(y_ref[g], wu_ref[g])
            dup = (dm * gate).astype(BF16)
            dup_ref[g] = dup
            dp_ref[g] = (dm * up * (gate * (1.0 - gate))).astype(BF16)
            dy_ref[g] = _dot_nt(dup, wu_ref[g])

    res, extra = _pcall(
        body, name, (S // tm,),
        [BS((tm, D), lambda i: (i, 0)), BS((4, tm, BW), lambda i: (0, i, 0)), BS((4, tm, D), lambda i: (1, i, 0)),
         BS((4, BW, D), lambda i: (0, 0, 0)), BS((D, D), lambda i: (0, 0))],
        [BS((4, tm, D), lambda i: (0, i, 0)), BS((4, tm, D), lambda i: (1, i, 0)), BS((4, tm, BW), lambda i: (0, i, 0))],
        [SDS((4, S, D), BF16), SDS((8, S, D), BF16), SDS((4, S, BW), F32)], [], ("parallel",),
        (dx, y, proj, w_up, w_out), comm)
    return res, extra


def _adamw(w, g, m, v):
    m = ADAM_B1 * m + (1.0 - ADAM_B1) * g
    v = ADAM_B2 * v + (1.0 - ADAM_B2) * (g * g)
    m_hat = m / (1.0 - ADAM_B1 ** ADAM_STEP)
    v_hat = v / (1.0 - ADAM_B2 ** ADAM_STEP)
    delta = -ADAM_LR * (m_hat / (jnp.sqrt(v_hat) + ADAM_EPS) + ADAM_WD * w)
    return delta, m, v


def _adamw_sharded(parts, w, m, v, name, comm=None):
    L, R, C = w.shape
    tr = _tile(R, ROWS_S, BF16_ROWS)

    def body(*refs):
        p_refs = refs[:L]
        w_ref, m_ref, v_ref, g_out, d_out, m_out, v_out = refs[L:]
        l = pl.program_id(0)
        g = None
        for d in range(N_DEV):
            t = p_refs[0][d].astype(F32)
            for j in range(1, L):
                t = jnp.where(l == j, p_refs[j][d].astype(F32), t)
            g = t if g is None else g + t
        dl, mn, vn = _adamw(w_ref[0], g, m_ref[0], v_ref[0])
        g_out[0] = g
        d_out[0] = dl
        m_out[0] = mn
        v_out[0] = vn

    def part_spec(j):
        return BS((N_DEV, tr, C), lambda l, r: (0, jnp.where(l == j, r, 0), 0))

    blk = BS((1, tr, C), lambda l, r: (l, r, 0))
    return _pcall(body, name, (L, R // tr), [part_spec(j) for j in range(L)] + [blk, blk, blk], [blk] * 4,
                  [SDS((L, R, C), F32)] * 4, [], ("parallel", "parallel"), (*parts, w, m, v), comm)


def _adamw_replicated(gathered, layout, wmv, name):
    n_b = len(gathered)
    n_p = len(layout)

    def body(*refs):
        bufs = refs[:n_b]
        prm = refs[n_b:n_b + 3 * n_p]
        outs = refs[n_b + 3 * n_p:n_b + 7 * n_p]
        sums = refs[n_b + 7 * n_p:]
        for b in range(n_b):
            s = bufs[b][0]
            for d in range(1, N_DEV):
                s = s + bufs[b][d]
            sums[b][...] = s
        for p, (b, r0, nr) in enumerate(layout):
            g = sums[b][r0:r0 + nr, :]
            d, mn, vn = _adamw(prm[3 * p][...], g, prm[3 * p + 1][...], prm[3 * p + 2][...])
            outs[4 * p][...] = g
            outs[4 * p + 1][...] = d
            outs[4 * p + 2][...] = mn
            outs[4 * p + 3][...] = vn

    flat = [a for t in wmv for a in t]
    out_shape = []
    for (w, _, _) in wmv:
        out_shape += [SDS(w.shape, F32)] * 4
    out_shape += [SDS(g.shape[1:], F32) for g in gathered]
    return pl.pallas_call(
        body, name=name, out_shape=out_shape,
        compiler_params=pltpu.CompilerParams(vmem_limit_bytes=V7X_VMEM_LIMIT),
    )(*gathered, *flat)


def _adamw_small(g, w, m, v, name):
    def body(g_ref, w_ref, m_ref, v_ref, d_out, m_out, v_out):
        d, mn, vn = _adamw(w_ref[...], g_ref[...], m_ref[...], v_ref[...])
        d_out[...] = d
        m_out[...] = mn
        v_out[...] = vn

    return pl.pallas_call(body, name=name, out_shape=[SDS(w.shape, F32)] * 3)(g, w, m, v)


def _pad_rows(a, rows):
    return jnp.pad(a, ((0, rows - a.shape[0]), (0, 0)))


def kernel(x, ffn1_norm, ffn1_w13, ffn1_w2, mix_norm, w_in, pool_w, pool_scale, sconv_w, cconv_w, cconv_ln_g, cconv_ln_b, sgu_ln_g, sgu_ln_b, sgu_w, sgu_b, w_up, w_out, ffn2_norm, ffn2_w13, ffn2_w2, final_norm, loss_target, m_ffn1_norm, m_ffn1_w13, m_ffn1_w2, m_mix_norm, m_w_in, m_pool_w, m_pool_scale, m_sconv_w, m_cconv_w, m_cconv_ln_g, m_cconv_ln_b, m_sgu_ln_g, m_sgu_ln_b, m_sgu_w, m_sgu_b, m_w_up, m_w_out, m_ffn2_norm, m_ffn2_w13, m_ffn2_w2, m_final_norm, v_ffn1_norm, v_ffn1_w13, v_ffn1_w2, v_mix_norm, v_w_in, v_pool_w, v_pool_scale, v_sconv_w, v_cconv_w, v_cconv_ln_g, v_cconv_ln_b, v_sgu_ln_g, v_sgu_ln_b, v_sgu_w, v_sgu_b, v_w_up, v_w_out, v_ffn2_norm, v_ffn2_w13, v_ffn2_w2, v_final_norm):
    P = dict(locals())
    L = ffn1_norm.shape[0]
    S, D = x.shape[1], x.shape[2]
    BW = D // 2
    GW = BW // 4
    F = ffn1_w2.shape[1] * N_DEV
    fs = ffn1_w13.shape[2]
    cw = sconv_w.shape[2]
    me = 4 * lax.axis_index("x") + 2 * lax.axis_index("y") + lax.axis_index("c")

    big = ["ffn1_w13", "ffn1_w2", "w_in", "w_up", "w_out", "ffn2_w13", "ffn2_w2"]
    shards = [(jnp.swapaxes(P[n], 1, 2) if n.endswith("w13") else P[n]).astype(BF16) for n in big]
    conv_local = jnp.concatenate([sconv_w, cconv_w], axis=1)

    def gather_of(units):
        return _gather_comm(shards, [(big.index(n), l) for n, l in units])

    def ready(n, g):
        if n.endswith("w13"):
            return g.reshape(2, F, D)
        if n.endswith("w2"):
            return g.reshape(F, D)
        if n == "w_up":
            return jnp.transpose(g, (1, 2, 0, 3)).reshape(4, BW, D)
        if n == "w_out":
            return g.reshape(D, D)
        return g

    W = {}

    def take(units, arrays):
        for (n, l), g in zip(units, arrays):
            W[n, l] = ready(n, g)

    first_units = [("ffn1_w13", 0), ("ffn1_w2", 0)]
    plan = {("ffn1_up", 0): [("w_in", 0)],
            ("ffn1_down", 0): [("w_up", 0), ("w_out", 0)],
            ("proj", 0): [("ffn2_w13", 0), ("ffn2_w2", 0)],
            ("mixers", 0): [("ffn1_w13", 1)], ("merge", 0): [("ffn1_w2", 1)],
            ("ffn2_up", 0): [("w_in", 1)], ("ffn2_down", 0): [("w_up", 1), ("w_out", 1)],
            ("ffn1_up", 1): [("ffn2_w13", 1)], ("ffn1_down", 1): [("ffn2_w2", 1)]}
    assert L <= 2

    def carried(key):
        units = [u for u in plan.get(key, []) if u[1] < L]
        return units, (gather_of(units) if units else None)

    first = _gather_comm(shards + [conv_local], [(big.index(n), l) for n, l in first_units] + [(len(big), None)])
    h, got = _rmsnorm_fwd(x[0], ffn1_norm[0][None, :], "first_norm_fwd", first)
    take(first_units, got[:2])
    conv_full = jnp.transpose(got[2], (1, 2, 0, 3)).reshape(L, SCONV_K + CCONV_K, N_DEV * cw)
    sconv_full = conv_full[:, :SCONV_K]
    cconv_full = conv_full[:, SCONV_K:]

    sgu_bias = jnp.broadcast_to(sgu_b[:, :, :, None], sgu_b.shape + (GW,))
    pool_wt = jnp.swapaxes(pool_w, 2, 3)
    sgu_wt = jnp.swapaxes(sgu_w, 2, 3)

    def row(a, l):
        return a[l][None, :]

    saved = []
    xc = x[0]
    for l in range(L):
        sv = {}
        for tag in ("ffn1", None, "ffn2"):
            if tag is None:
                sv["x_mix"] = xc
                units, comm = carried(("proj", l))
                proj, extra = _matmul_fwd(h, W["w_in", l], "proj_fwd", comm)
                take(units, extra)
                units, comm = carried(("mixers", l))
                y, sv["y1"], extra = _mixers_fwd(
                    proj, pool_w[l], row(pool_scale, l), sconv_full[l], cconv_full[l], row(cconv_ln_g, l),
                    row(cconv_ln_b, l), row(sgu_ln_g, l), row(sgu_ln_b, l), sgu_w[l], sgu_bias[l], "mixers_fwd", comm)
                take(units, extra)
                units, comm = carried(("merge", l))
                sv.update(h_mix=h, proj=proj, y=y)
                xc, sv["merged"], h, extra = _merge_fwd(y, proj, W["w_up", l], W["w_out", l], xc, row(ffn2_norm, l),
                                                        "merge_fwd", comm)
                take(units, extra)
            else:
                sv["x_" + tag] = xc
                units, comm = carried((tag + "_up", l))
                ab, extra = _matmul_fwd(h, W[tag + "_w13", l], "ffn_up_fwd", comm, w_t=True, out_dtype=BF16)
                take(units, extra)
                units, comm = carried((tag + "_down", l))
                sv.update({"h_" + tag: h, "ab_" + tag: ab})
                if tag == "ffn1":
                    g_next = row(mix_norm, l)
                else:
                    g_next = row(ffn1_norm, l + 1) if l + 1 < L else final_norm[None, :]
                xc, h, extra = _swiglu_down(ab, W[tag + "_w2", l], xc, g_next, "ffn_down_fwd", comm)
                take(units, extra)
        saved.append(sv)

    loss_part, dx, d_final = _final_loss(xc, final_norm[None, :], loss_target[0], "loss_head")
    loss = lax.psum(loss_part[0, 0], MESH_AXES)

    R = {}
    second = []

    def rest_of_sends():
        todo = list(second)
        second.clear()
        comm = None
        if todo:
            assert len({e[3][0] for e in todo}) == 1
            comm = _scatter_comm([e[1] for e in todo], todo[0][3][0], [e[2] for e in todo])
        return todo, comm

    def settle(todo, arrays):
        for (key, g, _, stages), a in zip(todo, arrays):
            if len(stages) > 1:
                second.append((key, g, a, stages[1:]))
            else:
                R[key] = a

    wide = ["ffn1_norm", "mix_norm", "ffn2_norm", "final_norm"]
    half = ["pool_scale", "cconv_ln_g", "cconv_ln_b", "sgu_ln_g", "sgu_ln_b"]
    narrow = ["pool_w", "sgu_w", "sgu_b"]
    small_names = wide + half + narrow
    small_g = [dict() for _ in range(L)]
    widths = []
    for n in small_names:
        if P[n].shape[-1] not in widths:
            widths.append(P[n].shape[-1])
    layout, conv_at = {}, {}

    def pack(width):
        def stack_layers(n):
            return jnp.stack([small_g[l][n] for l in range(L)], axis=0)

        parts, r0 = [], 0
        for n in small_names:
            if P[n].shape[-1] != width:
                continue
            g = d_final if n == "final_norm" else stack_layers(n).reshape(-1, width)
            layout[n] = (widths.index(width), r0, g.shape[0])
            parts.append(_pad_rows(g, -(-g.shape[0] // 8) * 8))
            r0 += parts[-1].shape[0]
        if width == N_DEV * cw:
            conv_g = jnp.concatenate([stack_layers("sconv_w"), stack_layers("cconv_w")], axis=1)
            conv_g = conv_g.reshape(L * (SCONV_K + CCONV_K), N_DEV * cw)
            conv_at.update(b=widths.index(width), r0=r0, rows=conv_g.shape[0])
            parts.append(_pad_rows(conv_g, -(-conv_g.shape[0] // 8) * 8))
        return jnp.concatenate(parts, axis=0)

    gathered_small = [None] * len(widths)
    for l in reversed(range(L)):
        sv = saved[l]
        sg = small_g[l]
        for tag in ("ffn2", None, "ffn1"):
            if tag is None:
                keys, comm = rest_of_sends()
                (dup, dproj, dy), extra = _merge_bwd(dx, sv["y"], sv["proj"], W["w_up", l], W["w_out", l],
                                                     "merge_bwd", comm)
                settle(keys, extra)
                g_out, _ = _matmul_tn(sv["merged"][None], dx[None], 1, "w_out_grad")
                g_up, _ = _matmul_tn(sv["y"], dup, 1, "w_up_grad")
                g_out = g_out.reshape(N_DEV, D // N_DEV, D)
                g_up = jnp.transpose(g_up.reshape(4, BW, N_DEV, D // N_DEV), (2, 0, 1, 3)).reshape(
                    N_DEV, 4 * BW, D // N_DEV)
                res, (R["w_out", l], R["w_up", l]) = _mixers_bwd(
                    sv["proj"], sv["y1"], dy, dproj, pool_w[l], pool_wt[l], row(pool_scale, l), sconv_full[l], cconv_full[l],
                    row(cconv_ln_g, l), row(cconv_ln_b, l), row(sgu_ln_g, l), row(sgu_ln_b, l), sgu_w[l], sgu_wt[l],
                    sgu_bias[l], "mixers_bwd", _scatter_comm([g_out, g_up]))
                dproj = res[0]
                (sg["pool_w"], sg["pool_scale"], sg["sconv_w"], sg["cconv_w"], sg["cconv_ln_g"], sg["cconv_ln_b"],
                 sg["sgu_ln_g"], sg["sgu_ln_b"], sg["sgu_w"], dgb) = res[1:]
                sg["sgu_b"] = dgb[:, :, 0]
                comm = None
                if l == 0:
                    early = [w for w in widths if w != D]
                    comm = _gather_comm([pack(w) for w in early], [(b, None) for b in range(len(early))])
                g_in, extra = _matmul_tn(sv["h_mix"][None], dproj, N_DEV, "w_in_grad", comm)
                if l == 0:
                    for w, g in zip(early, extra):
                        gathered_small[widths.index(w)] = g
                dx, sg["mix_norm"], (r_in,) = _matmul_nt_normbwd(
                    dproj, W["w_in", l], sv["x_mix"], row(mix_norm, l), dx, "proj_bwd",
                    _scatter_comm([g_in], PEERS_SAME_CORE))
                second.append((("w_in", l), g_in, r_in, (PEERS_NEAR_OTHER, PEERS_FAR_OTHER)))
            else:
                keys, comm = rest_of_sends()
                dab, sh, extra = _ffn_bwd_hidden(dx, W[tag + "_w2", l], sv["ab_" + tag], "ffn_hidden_bwd", comm)
                settle(keys, extra)
                keys, comm = rest_of_sends()
                g_w2, extra = _matmul_tn(sh[None], dx[None], 1, "ffn_w2_grad", comm)
                settle(keys, extra)
                g_w2 = g_w2.reshape(N_DEV, F // N_DEV, D)
                g_w13, (R[tag + "_w2", l],) = _matmul_tn(dab, sv["h_" + tag][None], 1, "ffn_w13_grad",
                                                         _scatter_comm([g_w2]), b_shared=True)
                g_w13 = g_w13.reshape(N_DEV, fs, D)
                last = tag == "ffn1" and l == 0
                now, later = (PEERS_BUT_NEAR_OTHER, PEERS_NEAR_OTHER) if last else (PEERS_SAME_CORE, PEERS_OTHER_CORE)
                dx, sg[tag + "_norm"], (r_w13,) = _matmul_nt_normbwd(
                    dab, W[tag + "_w13", l], sv["x_" + tag], row(P[tag + "_norm"], l), dx, "ffn_up_bwd",
                    _scatter_comm([g_w13], now), w_t=True)
                second.append(((tag + "_w13", l), g_w13, r_w13, (later,)))
    grad_x = dx[None]
    out = {}

    def as2d(n, a):
        if n == "final_norm":
            return a.reshape(1, D)
        return a.reshape(-1, a.shape[-1])

    for i, n in enumerate(["w_out", "w_up", "ffn2_w13", "w_in", "ffn2_w2", "ffn1_w2", "ffn1_w13"]):
        shp = P[n].shape
        if n.endswith("w13"):
            flat, back = (lambda a: jnp.swapaxes(a, 1, 2)), (lambda a: jnp.swapaxes(a, 1, 2))
        else:
            rows, cols = math.prod(shp[1:-1]), shp[-1]
            flat, back = (lambda a: a.reshape(L, rows, cols)), (lambda a: a.reshape(shp))
        keys, comm = [], None
        if i == 0:
            keys, comm = rest_of_sends()
        elif i == 1:
            comm = _gather_comm([pack(D)], [(0, None)])
        res, extra = _adamw_sharded([R[n, l] for l in range(L)], flat(P[n]), flat(P["m_" + n]), flat(P["v_" + n]),
                                    "adamw_sharded", comm)
        if i == 0:
            settle(keys, extra)
        elif i == 1:
            gathered_small[widths.index(D)] = extra[0]
        out[n] = tuple(back(a) for a in res)

    res = _adamw_replicated(gathered_small, [layout[n] for n in small_names],
                            [(as2d(n, P[n]), as2d(n, P["m_" + n]), as2d(n, P["v_" + n])) for n in small_names],
                            "adamw_replicated")
    for p, n in enumerate(small_names):
        out[n] = tuple(a.reshape(P[n].shape) for a in res[4 * p:4 * p + 4])
    conv_sum = res[4 * len(small_names) + conv_at["b"]][conv_at["r0"]:conv_at["r0"] + conv_at["rows"]]
    conv_mine = lax.dynamic_slice_in_dim(conv_sum, me * cw, cw, axis=1)

    def conv2d(a, b):
        return jnp.concatenate([a, b], axis=1).reshape(L * (SCONV_K + CCONV_K), cw)

    cd, cm, cv = _adamw_small(conv_mine, conv2d(sconv_w, cconv_w), conv2d(m_sconv_w, m_cconv_w),
                              conv2d(v_sconv_w, v_cconv_w), "adamw_conv")
    for n, sl in (("sconv_w", slice(0, SCONV_K)), ("cconv_w", slice(SCONV_K, SCONV_K + CCONV_K))):
        out[n] = tuple(a.reshape(L, SCONV_K + CCONV_K, cw)[:, sl] for a in (conv_mine, cd, cm, cv))

    order = ["ffn1_norm", "ffn1_w13", "ffn1_w2", "mix_norm", "w_in", "pool_w", "pool_scale", "sconv_w", "cconv_w",
             "cconv_ln_g", "cconv_ln_b", "sgu_ln_g", "sgu_ln_b", "sgu_w", "sgu_b", "w_up", "w_out", "ffn2_norm",
             "ffn2_w13", "ffn2_w2", "final_norm"]
    return (loss, grad_x, *[out[n][0] for n in order], *[out[n][1] for n in order],
            *[out[n][2] for n in order], *[out[n][3] for n in order])
```

```python
import functools
import math

import jax
import jax.numpy as jnp
from jax import lax
from jax.experimental import pallas as pl
from jax.experimental.pallas import tpu as pltpu

F32 = jnp.float32
BF16 = jnp.bfloat16
EPS = 1e-6
ADAM_LR = 0.001
ADAM_B1 = 0.9
ADAM_B2 = 0.999
ADAM_EPS = 1e-08
ADAM_WD = 0.01
ADAM_STEP = 10
SGU_BLOCK = 128
SGU_CHUNK = 64
SCONV_K = 3
CCONV_K = 31
HALO = 32
V7X_VMEM_LIMIT = 48 * 1024 * 1024
LANES = 128
BF16_ROWS = 16
WIDE = 11 * LANES
ROWS_L, ROWS_M, ROWS_S = 1024, 512, 256
ELEMENTWISE_ROWS = 32
MESH_AXES = ("x", "y", "c")
N_DEV = 8
_GELU_C0 = math.sqrt(2.0 / math.pi)
_GELU_C1 = 0.044715

BS = pl.BlockSpec
SDS = jax.ShapeDtypeStruct
ANY = pl.BlockSpec(memory_space=pl.ANY)


def _tile(n, pref, align=128):
    if n <= pref:
        return n
    t = pref - pref % align
    while t > 0:
        if n % t == 0:
            return t
        t -= align
    return n


def _sig(v):
    return 1.0 / (1.0 + jnp.exp(-v))


def _gelu(v):
    t = jnp.tanh(_GELU_C0 * (v + _GELU_C1 * (v * v * v)))
    return 0.5 * v * (1.0 + t), t


def _gelu_grad(v, t):
    return 0.5 * (1.0 + t) + 0.5 * v * (1.0 - t * t) * (_GELU_C0 * (1.0 + 3.0 * _GELU_C1 * v * v))


def _ln_stats(v):
    mu = jnp.mean(v, axis=-1, keepdims=True)
    vc = v - mu
    var = jnp.mean(vc * vc, axis=-1, keepdims=True)
    rstd = lax.rsqrt(var + EPS)
    return vc * rstd, rstd


def _ln_bwd(dvh, vh, rstd):
    return rstd * (dvh - jnp.mean(dvh, axis=-1, keepdims=True) - vh * jnp.mean(dvh * vh, axis=-1, keepdims=True))


def _dot(a, b):
    return jnp.dot(a, b, preferred_element_type=F32)


def _dot_nt(a, b):
    return lax.dot_general(a, b, (((1,), (1,)), ((), ())), preferred_element_type=F32)


def _dot_tn(a, b):
    return lax.dot_general(a, b, (((0,), (0,)), ((), ())), preferred_element_type=F32)


def _mesh_pos():
    return lax.axis_index("x"), lax.axis_index("y"), lax.axis_index("c")


class _Comm:
    def __init__(self, ins, out_shape, sems, start, finish, aliases=None, middle=None):
        self.ins, self.out_shape, self.sems, self.start, self.finish = ins, out_shape, sems, start, finish
        self.aliases = aliases or {}
        self.middle = middle


def _gather_comm(shards, units):
    n_u = len(units)
    out_shape = []
    for t, l in units:
        shp = shards[t].shape if l is None else shards[t].shape[1:]
        out_shape.append(SDS((N_DEV,) + tuple(shp), shards[t].dtype))

    def upper_rows(o):
        shp = out_shape[o].shape[1:]
        assert shp[0] >= 2
        return shp[0] // 2 if len(shp) > 2 or shp[0] < 32 else shp[0] // 32 * 16

    def tools(ins, dsts, sems):
        send_sems, recv_sems, local_sems = sems
        x, y, c = _mesh_pos()
        me, sib = (x, y, c), (x, y, 1 - c)
        xn, yn, dg = (1 - x, y, c), (x, 1 - y, c), (1 - x, 1 - y, c)

        def src_of(o):
            t, l = units[o]
            return ins[t] if l is None else ins[t].at[l]

        def row(o, p, part=None):
            r = dsts[o].at[4 * p[0] + 2 * p[1] + p[2]]
            if part is None:
                return r
            h = upper_rows(o)
            return r.at[pl.ds(0, h)] if part == "upper" else r.at[pl.ds(h, out_shape[o].shape[1] - h)]

        def copy(o, k, src, dst, to):
            return pltpu.make_async_remote_copy(
                src_ref=src, dst_ref=dst, send_sem=send_sems.at[o * 8 + k], recv_sem=recv_sems.at[o * 8 + k],
                device_id=to, device_id_type=pl.DeviceIdType.MESH)

        def send(o, k):
            if k < 3:
                return copy(o, k, src_of(o), row(o, me), (sib, xn, yn)[k])
            if k == 3:
                return copy(o, k, row(o, xn, "upper"), row(o, xn, "upper"), yn)
            if k == 4:
                return copy(o, k, row(o, yn, "lower"), row(o, yn, "lower"), xn)
            blk = (xn, yn, dg)[k - 5]
            return copy(o, k, row(o, blk), row(o, blk), sib)

        def landed(o, k):
            def other(p):
                return (p[0], p[1], 1 - c)

            dst = (row(o, sib), row(o, xn), row(o, yn), row(o, dg, "upper"), row(o, dg, "lower"),
                   row(o, other(xn)), row(o, other(yn)), row(o, other(dg)))[k]
            return copy(o, k, dst, dst, me)

        def local(o):
            return pltpu.make_async_copy(src_of(o), row(o, me), local_sems.at[o])

        return send, landed, local

    def start(ins, dsts, sems):
        send, _, local = tools(ins, dsts, sems)
        for o in range(n_u):
            local(o).start()
            for k in (1, 2, 0):
                send(o, k).start()

    def middle(ins, dsts, sems):
        send, landed, _ = tools(ins, dsts, sems)
        for o in range(n_u):
            landed(o, 1).wait_recv()
            send(o, 3).start()
            landed(o, 2).wait_recv()
            send(o, 4).start()
            send(o, 5).start()
            send(o, 6).start()

    def finish(ins, dsts, sems):
        send, landed, local = tools(ins, dsts, sems)
        for o in range(n_u):
            landed(o, 3).wait_recv()
            landed(o, 4).wait_recv()
            send(o, 7).start()
        for o in range(n_u):
            for k in (0, 5, 6, 7):
                landed(o, k).wait_recv()
        for o in range(n_u):
            for k in range(8):
                send(o, k).wait_send()
            local(o).wait()

    sems = [pltpu.SemaphoreType.DMA((8 * n_u,)), pltpu.SemaphoreType.DMA((8 * n_u,)), pltpu.SemaphoreType.DMA((n_u,))]
    return _Comm(list(shards), out_shape, sems, start, finish, middle=middle)


PEERS_ALL = (1, 2, 3, 4, 5, 6, 7)
PEERS_SAME_CORE = (1, 2, 4, 6)
PEERS_OTHER_CORE = (3, 5, 7)
PEERS_BUT_NEAR_OTHER = (1, 2, 4, 6, 7)
PEERS_NEAR_OTHER = (3, 5)
PEERS_FAR_OTHER = (7,)


def _scatter_comm(parts, peers=PEERS_ALL, into=None):
    n_u = len(parts)

    def tools(ins, dsts, sems):
        send_sems, recv_sems, local_sems = sems
        x, y, c = _mesh_pos()
        me = 4 * x + 2 * y + c

        def peer(k):
            return ((x + ((k >> 2) & 1)) % 2, (y + ((k >> 1) & 1)) % 2, (c + (k & 1)) % 2)

        def copy(u, k, wait=False):
            p = peer(k)
            pi = 4 * p[0] + 2 * p[1] + p[2]
            return pltpu.make_async_remote_copy(
                src_ref=ins[u].at[pi], dst_ref=dsts[u].at[pi if wait else me],
                send_sem=send_sems.at[u * 7 + k - 1], recv_sem=recv_sems.at[u * 7 + k - 1],
                device_id=p, device_id_type=pl.DeviceIdType.MESH)

        def local(u):
            return pltpu.make_async_copy(ins[u].at[me], dsts[u].at[me], local_sems.at[u])

        return copy, local

    def start(ins, dsts, sems):
        copy, local = tools(ins, dsts, sems)
        for u in range(n_u):
            if into is None:
                local(u).start()
            for k in peers:
                copy(u, k).start()

    def finish(ins, dsts, sems):
        copy, local = tools(ins, dsts, sems)
        for u in range(n_u):
            for k in peers:
                copy(u, k, wait=True).wait()
            if into is None:
                local(u).wait()

    sems = [pltpu.SemaphoreType.DMA((7 * n_u,)), pltpu.SemaphoreType.DMA((7 * n_u,)), pltpu.SemaphoreType.DMA((n_u,))]
    aliases = {} if into is None else {n_u + u: u for u in range(n_u)}
    return _Comm(list(parts) + list(into or []), [SDS(p.shape, p.dtype) for p in parts], sems, start, finish, aliases)


def _pcall(body, name, grid, in_specs, out_specs, out_shape, scratch, sem, args, comm=None, aliases=None):
    n_i, n_o, n_s = len(in_specs), len(out_specs), len(scratch)
    aliases = aliases or {}
    if comm is None:
        res = pl.pallas_call(
            body, name=name, grid=grid, in_specs=in_specs, out_specs=out_specs, out_shape=out_shape,
            scratch_shapes=scratch, input_output_aliases=aliases,
            compiler_params=pltpu.CompilerParams(dimension_semantics=sem, vmem_limit_bytes=V7X_VMEM_LIMIT),
        )(*args)
        return res, []
    n_ci, n_co = len(comm.ins), len(comm.out_shape)

    def wrapped(*refs):
        ins = refs[:n_i]
        cins = refs[n_i:n_i + n_ci]
        outs = refs[n_i + n_ci:n_i + n_ci + n_o]
        couts = refs[n_i + n_ci + n_o:n_i + n_ci + n_o + n_co]
        rest = refs[n_i + n_ci + n_o + n_co:]
        step = 0
        for d, g in enumerate(grid):
            step = step * g + pl.program_id(d)
        n_steps = math.prod(grid)
        mid = (n_steps * 5) // 8
        staged = comm.middle is not None and 0 < mid < n_steps - 1

        @pl.when(step == 0)
        def _():
            comm.start(cins, couts, rest[n_s:])

        if staged:
            @pl.when(step == mid)
            def _():
                comm.middle(cins, couts, rest[n_s:])

        body(*ins, *outs, *rest[:n_s])

        @pl.when(step == n_steps - 1)
        def _():
            if comm.middle is not None and not staged:
                comm.middle(cins, couts, rest[n_s:])
            comm.finish(cins, couts, rest[n_s:])

    res = pl.pallas_call(
        wrapped, name=name, grid=grid, in_specs=list(in_specs) + [ANY] * n_ci,
        out_specs=list(out_specs) + [ANY] * n_co, out_shape=list(out_shape) + list(comm.out_shape),
        scratch_shapes=list(scratch) + list(comm.sems),
        input_output_aliases={**aliases, **{n_i + ci: n_o + co for ci, co in comm.aliases.items()}},
        compiler_params=pltpu.CompilerParams(dimension_semantics=("arbitrary",) * len(grid),
                                             vmem_limit_bytes=V7X_VMEM_LIMIT),
    )(*args, *comm.ins)
    return res[:n_o], res[n_o:]


def _rmsnorm_fwd(x, g, name, comm=None):
    S, D = x.shape
    tm = _tile(S, ROWS_M, BF16_ROWS)

    def body(x_ref, g_ref, h_ref):
        h_ref[...] = _rmsnorm_rows(x_ref[...], g_ref[...])

    (h,), extra = _pcall(body, name, (S // tm,), [BS((tm, D), lambda i: (i, 0)), BS((1, D), lambda i: (0, 0))],
                         [BS((tm, D), lambda i: (i, 0))], [SDS((S, D), BF16)], [], ("parallel",), (x, g), comm)
    return h, extra


def _matmul_fwd(a, w, name, comm=None, w_t=False, out_dtype=F32):
    S, K = a.shape
    C = w.shape[0]
    Fc = w.shape[1] if w_t else w.shape[2]
    tn = _tile(Fc, WIDE)
    tm = _tile(S, 2 * ROWS_L, BF16_ROWS)

    def body(a_ref, w_ref, o_ref):
        p = _dot_nt(a_ref[...], w_ref[0]) if w_t else _dot(a_ref[...], w_ref[0])
        o_ref[0] = p.astype(out_dtype)

    w_spec = BS((1, tn, K), lambda c, n, i: (c, n, 0)) if w_t else BS((1, K, tn), lambda c, n, i: (c, 0, n))
    (o,), extra = _pcall(
        body, name, (C, Fc // tn, S // tm), [BS((tm, K), lambda c, n, i: (i, 0)), w_spec],
        [BS((1, tm, tn), lambda c, n, i: (c, i, n))], [SDS((C, S, Fc), out_dtype)], [],
        ("parallel", "parallel", "parallel"), (a, w), comm)
    return o, extra


def _rmsnorm_rows(xv, g):
    r = lax.rsqrt(jnp.mean(xv * xv, axis=-1, keepdims=True) + EPS)
    return (xv * r * g).astype(BF16)


def _swiglu_down(ab, w2, x, g_next, name, comm=None):
    _, S, F = ab.shape
    D = w2.shape[1]
    tk = _tile(F, WIDE)
    tm = _tile(S, ROWS_M, BF16_ROWS)
    nk = F // tk

    te = _tile(tm, ELEMENTWISE_ROWS, BF16_ROWS)

    def body(ab_ref, w_ref, x_ref, g_ref, o_ref, h_ref, s_ref):
        k = pl.program_id(1)
        for r0 in range(0, tm, te):
            rows = slice(r0, r0 + te)
            a = ab_ref[0, rows, :].astype(F32)
            s_ref[rows, :] = (a * _sig(a) * ab_ref[1, rows, :].astype(F32)).astype(BF16)
        p = 0.5 * _dot(s_ref[...], w_ref[...])

        @pl.when(k == 0)
        def _():
            o_ref[...] = x_ref[...] + p

        @pl.when(k > 0)
        def _():
            o_ref[...] += p

        @pl.when(k == nk - 1)
        def _():
            h_ref[...] = _rmsnorm_rows(o_ref[...], g_ref[...])

    (o, h), extra = _pcall(
        body, name, (S // tm, nk),
        [BS((2, tm, tk), lambda i, k: (0, i, k)), BS((tk, D), lambda i, k: (k, 0)), BS((tm, D), lambda i, k: (i, 0)),
         BS((1, D), lambda i, k: (0, 0))],
        [BS((tm, D), lambda i, k: (i, 0)), BS((tm, D), lambda i, k: (i, 0))],
        [SDS((S, D), F32), SDS((S, D), BF16)], [pltpu.VMEM((tm, tk), BF16)], ("parallel", "arbitrary"),
        (ab, w2, x, g_next), comm)
    return o, h, extra


def _ffn_bwd_hidden(dy, w2, ab, name, comm=None):
    S, D = dy.shape
    F = w2.shape[0]
    tk = _tile(F, WIDE)
    tm = _tile(S, ROWS_M, BF16_ROWS)
    te = _tile(tm, ELEMENTWISE_ROWS, BF16_ROWS)

    def body(dy_ref, w_ref, ab_ref, dab_ref, s_ref, ds_ref):
        ds_ref[...] = 0.5 * _dot_nt(dy_ref[...].astype(BF16), w_ref[...])
        for r0 in range(0, tm, te):
            rows = slice(r0, r0 + te)
            ds = ds_ref[rows, :]
            a = ab_ref[0, rows, :].astype(F32)
            b = ab_ref[1, rows, :].astype(F32)
            sg = _sig(a)
            sa = a * sg
            dab_ref[0, rows, :] = (ds * b * (sg * (1.0 + a * (1.0 - sg)))).astype(BF16)
            dab_ref[1, rows, :] = (ds * sa).astype(BF16)
            s_ref[rows, :] = (0.5 * (sa * b)).astype(BF16)

    (dab, sh), extra = _pcall(
        body, name, (F // tk, S // tm),
        [BS((tm, D), lambda k, i: (i, 0)), BS((tk, D), lambda k, i: (k, 0)), BS((2, tm, tk), lambda k, i: (0, i, k))],
        [BS((2, tm, tk), lambda k, i: (0, i, k)), BS((tm, tk), lambda k, i: (i, k))],
        [SDS((2, S, F), BF16), SDS((S, F), BF16)], [pltpu.VMEM((tm, tk), F32)], ("parallel", "parallel"),
        (dy, w2, ab), comm)
    return dab, sh, extra


def _matmul_tn(a, b, n_c, name, comm=None, b_shared=False):
    G, S, M = a.shape
    _, _, Fc = b.shape
    C = n_c
    tM = _tile(M, WIDE)
    tn = _tile(Fc, WIDE)
    ts = _tile(S, 2 * ROWS_L if b.dtype == BF16 else ROWS_L, BF16_ROWS)
    n_s = S // ts

    def body(a_ref, b_ref, o_ref, acc):
        s = pl.program_id(4)
        p = _dot_tn(a_ref[0].astype(BF16), b_ref[0].astype(BF16))

        @pl.when(s == 0)
        def _():
            acc[...] = p

        @pl.when(s > 0)
        def _():
            acc[...] += p

        @pl.when(s == n_s - 1)
        def _():
            o_ref[0] = acc[...].astype(BF16)

    (o,), extra = _pcall(
        body, name, (G, M // tM, C, Fc // tn, n_s),
        [BS((1, ts, tM), lambda g, m, c, n, s: (g, s, m)), BS((1, ts, tn), lambda g, m, c, n, s: (c if b_shared else g * C + c, s, n))],
        [BS((1, tM, tn), lambda g, m, c, n, s: (g * C + c, m, n))], [SDS((G * C, M, Fc), BF16)],
        [pltpu.VMEM((tM, tn), F32)], ("parallel", "parallel", "parallel", "parallel", "arbitrary"), (a, b), comm)
    return o, extra


def _matmul_nt_normbwd(b, w, x, gam, dres, name, comm=None, w_t=False):
    C, S, Fc = b.shape
    D = w.shape[2] if w_t else w.shape[1]
    tk = _tile(Fc, WIDE)
    tm = _tile(S, ROWS_L, BF16_ROWS)
    te = _tile(tm, 256, 8)
    nk = Fc // tk

    def body(b_ref, w_ref, x_ref, g_ref, r_ref, dx_ref, dg_ref, dx16_ref):
        i, c, k = pl.program_id(0), pl.program_id(1), pl.program_id(2)
        p = _dot(b_ref[0], w_ref[0]) if w_t else _dot_nt(b_ref[0], w_ref[0])
        first = jnp.logical_and(c == 0, k == 0)

        @pl.when(first)
        def _():
            dx_ref[...] = p

        @pl.when(jnp.logical_not(first))
        def _():
            dx_ref[...] += p

        @pl.when(jnp.logical_and(c == C - 1, k == nk - 1))
        def _():
            dgp = None
            for r0 in range(0, tm, te):
                rows = slice(r0, r0 + te)
                xv = x_ref[rows, :]
                r = lax.rsqrt(jnp.mean(xv * xv, axis=-1, keepdims=True) + EPS)
                xn = xv * r
                dh = dx_ref[rows, :]
                dxn = dh * g_ref[...]
                dxv = r_ref[rows, :] + r * (dxn - xn * jnp.mean(dxn * xn, axis=-1, keepdims=True))
                dx_ref[rows, :] = dxv
                dx16_ref[rows, :] = dxv.astype(BF16)
                t = jnp.sum(dh * xn, axis=0, keepdims=True)
                dgp = t if dgp is None else dgp + t

            @pl.when(i == 0)
            def _():
                dg_ref[...] = dgp

            @pl.when(i > 0)
            def _():
                dg_ref[...] += dgp

    once = dict(pipeline_mode=pl.Buffered(1))
    (dx, dg, dx16), extra = _pcall(
        body, name, (S // tm, C, nk),
        [BS((1, tm, tk), lambda i, c, k: (c, i, k)),
         BS((1, tk, D), lambda i, c, k: (c, k, 0)) if w_t else BS((1, D, tk), lambda i, c, k: (c, 0, k)),
         BS((tm, D), lambda i, c, k: (i, 0), **once), BS((1, D), lambda i, c, k: (0, 0)),
         BS((tm, D), lambda i, c, k: (i, 0), **once)],
        [BS((tm, D), lambda i, c, k: (i, 0)), BS((1, D), lambda i, c, k: (0, 0)), BS((tm, D), lambda i, c, k: (i, 0))],
        [SDS((S, D), F32), SDS((1, D), F32), SDS((S, D), BF16)], [],
        ("arbitrary", "arbitrary", "arbitrary"), (b, w, x, gam, dres), comm)
    return dx, dx16, dg, extra


def _final_loss(x, gam, target, name):
    S, D = x.shape
    tm = _tile(S, 512, 8)

    def body(x_ref, g_ref, t_ref, loss_ref, dx_ref, dg_ref, dx16_ref):
        i = pl.program_id(0)
        xv = x_ref[...]
        r = lax.rsqrt(jnp.mean(xv * xv, axis=-1, keepdims=True) + EPS)
        xn = xv * r
        err = xn * g_ref[...] - t_ref[...]
        part = 0.5 * jnp.sum(jnp.mean(err * err, axis=-1, keepdims=True), axis=0, keepdims=True)
        dy = err * (1.0 / D)
        dxn = dy * g_ref[...]
        dxv = r * (dxn - xn * jnp.mean(dxn * xn, axis=-1, keepdims=True))
        dx_ref[...] = dxv
        dx16_ref[...] = dxv.astype(BF16)
        dgp = jnp.sum(dy * xn, axis=0, keepdims=True)
        lp = jnp.broadcast_to(part, loss_ref.shape)

        @pl.when(i == 0)
        def _():
            dg_ref[...] = dgp
            loss_ref[...] = lp

        @pl.when(i > 0)
        def _():
            dg_ref[...] += dgp
            loss_ref[...] += lp

    res, _ = _pcall(
        body, name, (S // tm,),
        [BS((tm, D), lambda i: (i, 0)), BS((1, D), lambda i: (0, 0)), BS((tm, D), lambda i: (i, 0))],
        [BS((8, 128), lambda i: (0, 0)), BS((tm, D), lambda i: (i, 0)), BS((1, D), lambda i: (0, 0)),
         BS((tm, D), lambda i: (i, 0))],
        [SDS((8, 128), F32), SDS((S, D), F32), SDS((1, D), F32), SDS((S, D), BF16)], [], ("arbitrary",),
        (x, gam, target))
    return res


CONV_CHUNK = 32


def _fill_shifted(rot, n):
    for b in range(1, 8):
        rot[b, 0:n - 8, :] = rot[0, b:b + n - 8, :]


def _window(rot, off, r0, rows):
    b = off % 8
    return rot[b, off - b + r0:off - b + r0 + rows, :]


def _taps(rot, w_ref, offs, n_rows, out):
    for r0 in range(0, n_rows, CONV_CHUNK):
        acc = None
        for k, off in enumerate(offs):
            t = w_ref[k:k + 1, :] * _window(rot, off, r0, CONV_CHUNK)
            acc = t if acc is None else acc + t
        out[r0:r0 + CONV_CHUNK, :] = acc


def _tap_grads(rot, offs, g_plane, n_rows, dw_ref):
    for k, off in enumerate(offs):
        acc = None
        for r0 in range(0, n_rows, CONV_CHUNK):
            p = g_plane[0, r0:r0 + CONV_CHUNK, :] * _window(rot, off, r0, CONV_CHUNK)
            acc = p if acc is None else acc + p
        dw_ref[k:k + 1, :] += jnp.sum(acc, axis=0, keepdims=True)


def _sgu_masks():
    ii = lax.broadcasted_iota(jnp.int32, (SGU_BLOCK, SGU_BLOCK), 0) // SGU_CHUNK
    jj = lax.broadcasted_iota(jnp.int32, (SGU_BLOCK, SGU_BLOCK), 1) // SGU_CHUNK
    return jj <= ii, ii <= jj


def _mixers_fwd(proj, pool_w, pool_scale, sconv_w, cconv_w, cln_g, cln_b, sln_g, sln_b, sgu_w, sgu_bias, name,
                comm=None):
    _, S, D = proj.shape
    BW = D // 2
    GW = BW // 4
    TS = _tile(S, ROWS_S, SGU_BLOCK)
    H = HALO
    hb = TS // H

    def main(blk, col):
        return BS((1, TS, BW), lambda i: (blk, i, col))

    def back(blk, col):
        return BS((1, H, BW), lambda i: (blk, jnp.maximum(i * hb - 1, 0), col))

    def full(a):
        nd = a.ndim
        return BS(a.shape, lambda i: (0,) * nd)

    def body(pa_m, pa_b, xi_m, xi_b, bg_m, cg_m, cg_b, ca_m, ca_b, cb_m, cb_b, du_m, dv_m,
             pw, ps, sw, cw, clg, clb, slg, slb, gw, gbias, y_ref, y1_ref, e1, e2, e3):
        i = pl.program_id(0)
        nb = jnp.where(i > 0, 1.0, 0.0).astype(F32)
        rows = i * TS + lax.broadcasted_iota(jnp.int32, (TS, 1), 0)

        e1[0:H, :] = pa_b[0] * nb
        e1[H:H + TS, :] = pa_m[0]
        for g in range(4):
            cols = slice(g * GW, (g + 1) * GW)
            win = 2 << g
            wsum = e1[H:H + TS, cols]
            for k in range(1, win):
                wsum = wsum + e1[H - k:H - k + TS, cols]
            cnt = jnp.minimum(rows + 1, win).astype(F32)
            d = wsum / cnt - e1[H:H + TS, cols]
            yg = _dot(d.astype(BF16), pw[g].astype(BF16)) * ps[:, cols]
            y_ref[0, :, cols] = yg.astype(BF16)

        e2[0:H, :] = cg_b[0] * xi_b[0] * nb
        e2[H:H + TS, :] = cg_m[0] * xi_m[0]
        cz = sw[0:1, :] * e2[H - 2:H - 2 + TS, :]
        for k in range(1, SCONV_K):
            cz = cz + sw[k:k + 1, :] * e2[H - 2 + k:H - 2 + k + TS, :]
        y_ref[1] = (bg_m[0] * cz).astype(BF16)

        e3[0, 0:H, :] = ca_b[0] * _sig(cb_b[0]) * nb
        e3[0, H:H + TS, :] = ca_m[0] * _sig(cb_m[0])
        _fill_shifted(e3, H + TS)
        _taps(e3, cw, [H - (CCONV_K - 1) + k for k in range(CCONV_K)], TS, y1_ref)
        yh, _ = _ln_stats(y1_ref[...])
        y2 = yh * clg[...] + clb[...]
        y_ref[2] = (y2 * _sig(y2)).astype(BF16)

        u, _ = _gelu(du_m[0])
        v, _ = _gelu(dv_m[0])
        vh, _ = _ln_stats(v)
        vn = vh * slg[...] + slb[...]
        mask, _ = _sgu_masks()
        for h in range(4):
            wm = jnp.where(mask, gw[h], 0.0).astype(BF16)
            cs = slice(h * GW, (h + 1) * GW)
            for n in range(TS // SGU_BLOCK):
                rs = slice(n * SGU_BLOCK, (n + 1) * SGU_BLOCK)
                z = _dot(wm, vn[rs, cs].astype(BF16)) + gbias[h]
                y_ref[3, rs, cs] = (u[rs, cs] * z).astype(BF16)

    args = [proj] * 13 + [pool_w, pool_scale, sconv_w, cconv_w, cln_g, cln_b, sln_g, sln_b, sgu_w, sgu_bias]
    in_specs = [main(0, 0), back(0, 0), main(0, 1), back(0, 1), main(1, 0), main(1, 1), back(1, 1),
                main(2, 0), back(2, 0), main(2, 1), back(2, 1), main(3, 0), main(3, 1)]
    in_specs += [full(a) for a in args[13:]]
    (y, y1), extra = _pcall(body, name, (S // TS,), in_specs,
                            [BS((4, TS, BW), lambda i: (0, i, 0)), BS((TS, BW), lambda i: (i, 0))],
                            [SDS((4, S, BW), BF16), SDS((S, BW), F32)],
                            [pltpu.VMEM((H + TS, BW), F32)] * 2 + [pltpu.VMEM((8, H + TS, BW), F32)], ("parallel",),
                            args, comm)
    return y, y1, extra


def _mixers_bwd(proj, y1, dy, dproj_gates, pool_w, pool_wt, pool_scale, sconv_w, cconv_w, cln_g, cln_b, sln_g, sln_b,
                sgu_w, sgu_wt, sgu_bias, name, comm=None):
    _, S, D = proj.shape
    BW = D // 2
    GW = BW // 4
    TS = _tile(S, ROWS_S, SGU_BLOCK)
    H = HALO
    hb = TS // H
    n_t = S // TS
    E = TS + H

    def main(blk, col):
        return BS((1, TS, BW), lambda i: (blk, i, col))

    def back(blk, col):
        return BS((1, H, BW), lambda i: (blk, jnp.maximum(i * hb - 1, 0), col))

    def front(blk, col):
        return BS((1, H, BW), lambda i: (blk, jnp.minimum((i + 1) * hb, S // H - 1), col))

    def full(a):
        nd = a.ndim
        return BS(a.shape, lambda i: (0,) * nd)

    def body(pa_b, pa_m, xi_b, xi_m, bg_m, bg_f, cg_b, cg_m, ca_b, ca_m, cb_b, cb_m, du_m, dv_m, y1_m, y1_f,
             dya_m, dya_f, dyb_m, dyb_f, dyc_m, dyc_f, dyd_m,
             pw, pwt, ps, sw, cw, clg, clb, slg, slb, gw, gwt, gbias, _gates_in,
             dp_ref, dpw, dps, dsw, dcw, dclg, dclb, dslg, dslb, dgw, dgb,
             e1, e2, e3, e4, e5, ra, rb):
        i = pl.program_id(0)
        nb = jnp.where(i > 0, 1.0, 0.0).astype(F32)
        nf = jnp.where(i < n_t - 1, 1.0, 0.0).astype(F32)
        rows_m = i * TS + lax.broadcasted_iota(jnp.int32, (TS, 1), 0)
        rows_e = i * TS + lax.broadcasted_iota(jnp.int32, (E, 1), 0)

        @pl.when(i == 0)
        def _():
            for r in (dpw, dps, dsw, dcw, dclg, dclb, dslg, dslb, dgw, dgb):
                r[...] = jnp.zeros(r.shape, F32)

        e1[0:H, :] = pa_b[0] * nb
        e1[H:H + TS, :] = pa_m[0]
        e2[0:TS, :] = dya_m[0] * ps[...]
        e2[TS:E, :] = dya_f[0] * ps[...] * nf
        for g in range(4):
            cols = slice(g * GW, (g + 1) * GW)
            win = 2 << g
            a_m = e1[H:H + TS, cols]
            wsum = a_m
            for k in range(1, win):
                wsum = wsum + e1[H - k:H - k + TS, cols]
            d = wsum / jnp.minimum(rows_m + 1, win).astype(F32) - a_m
            d16 = d.astype(BF16)
            dyp = e2[0:E, cols].astype(BF16)
            dd = _dot(dyp, pwt[g].astype(BF16))
            e3[0:E, cols] = dd / jnp.minimum(rows_e + 1, win).astype(F32)
            da = e3[0:TS, cols] - dd[0:TS]
            for k in range(1, win):
                da = da + e3[k:k + TS, cols]
            dp_ref[0, :, cols] = da.astype(BF16)
            ypre = _dot(d16, pw[g].astype(BF16))
            dps[:, cols] += jnp.sum(dya_m[0][:, cols] * ypre, axis=0, keepdims=True)
            dpw[g] += _dot(jnp.transpose(d).astype(BF16), dyp[0:TS])

        e4[0:H, :] = cg_b[0] * xi_b[0] * nb
        e4[H:H + TS, :] = cg_m[0] * xi_m[0]
        dyb = dyb_m[0]
        e5[0:TS, :] = dyb * bg_m[0]
        e5[TS:E, :] = dyb_f[0] * bg_f[0] * nf
        dcz = e5[0:TS, :]
        cz = None
        dz = None
        for k in range(SCONV_K):
            zk = e4[H - 2 + k:H - 2 + k + TS, :]
            wk = sw[k:k + 1, :]
            cz = wk * zk if cz is None else cz + wk * zk
            t = wk * e5[2 - k:2 - k + TS, :]
            dz = t if dz is None else dz + t
            dsw[k:k + 1, :] += jnp.sum(dcz * zk, axis=0, keepdims=True)
        dp_ref[0, :, BW:2 * BW] = (dz * cg_m[0]).astype(BF16)
        dp_ref[1, :, 0:BW] = (dyb * cz).astype(BF16)
        dp_ref[1, :, BW:2 * BW] = (dz * xi_m[0]).astype(BF16)

        sgm = _sig(cb_m[0])
        ra[0, 0:H, :] = ca_b[0] * _sig(cb_b[0]) * nb
        ra[0, H:H + TS, :] = ca_m[0] * sgm
        _fill_shifted(ra, H + TS)
        fwd_offs = [H - (CCONV_K - 1) + k for k in range(CCONV_K)]
        e4[0:TS, :] = y1_m[...]
        e4[TS:E, :] = y1_f[...]
        yh, rstd = _ln_stats(e4[0:E, :])
        y2 = yh * clg[...] + clb[...]
        s2 = _sig(y2)
        e1[0:TS, :] = dyc_m[0]
        e1[TS:E, :] = dyc_f[0] * nf
        dy2 = e1[0:E, :] * (s2 * (1.0 + y2 * (1.0 - s2)))
        dclg[...] += jnp.sum((dy2 * yh)[0:TS], axis=0, keepdims=True)
        dclb[...] += jnp.sum(dy2[0:TS], axis=0, keepdims=True)
        rb[0, 0:E, :] = _ln_bwd(dy2 * clg[...], yh, rstd)
        _fill_shifted(rb, E)
        _taps(rb, cw, [CCONV_K - 1 - k for k in range(CCONV_K)], TS, e5)
        _tap_grads(ra, fwd_offs, rb, TS, dcw)
        dy0 = e5[0:TS, :]
        dp_ref[2, :, 0:BW] = (dy0 * sgm).astype(BF16)
        dp_ref[2, :, BW:2 * BW] = (dy0 * ca_m[0] * (sgm * (1.0 - sgm))).astype(BF16)

        pu = du_m[0]
        pv = dv_m[0]
        u, tu = _gelu(pu)
        v, tv = _gelu(pv)
        vh, vr = _ln_stats(v)
        vn = vh * slg[...] + slb[...]
        dyd = dyd_m[0]
        mask, mask_t = _sgu_masks()
        for h in range(4):
            wm = jnp.where(mask, gw[h], 0.0).astype(BF16)
            wmt = jnp.where(mask_t, gwt[h], 0.0).astype(BF16)
            cs = slice(h * GW, (h + 1) * GW)
            for n in range(TS // SGU_BLOCK):
                rs = slice(n * SGU_BLOCK, (n + 1) * SGU_BLOCK)
                vb = vn[rs, cs].astype(BF16)
                z = _dot(wm, vb) + gbias[h]
                dzb = dyd[rs, cs] * u[rs, cs]
                dz16 = dzb.astype(BF16)
                e3[rs, cs] = dyd[rs, cs] * z
                e4[rs, cs] = _dot(wmt, dz16)
                dgw[h] += jnp.where(mask, _dot_nt(dz16, vb), 0.0)
                dgb[h] += dzb
        dvn = e4[0:TS, :]
        dslg[...] += jnp.sum(dvn * vh, axis=0, keepdims=True)
        dslb[...] += jnp.sum(dvn, axis=0, keepdims=True)
        dv = _ln_bwd(dvn * slg[...], vh, vr)
        dp_ref[3, :, 0:BW] = (e3[0:TS, :] * _gelu_grad(pu, tu)).astype(BF16)
        dp_ref[3, :, BW:2 * BW] = (dv * _gelu_grad(pv, tv)).astype(BF16)

        @pl.when(i == n_t - 1)
        def _():
            for h in range(4):
                dgb[h] = jnp.broadcast_to(jnp.sum(dgb[h], axis=1, keepdims=True), dgb.shape[1:])

    params = [pool_w, pool_wt, pool_scale, sconv_w, cconv_w, cln_g, cln_b, sln_g, sln_b, sgu_w, sgu_wt, sgu_bias]
    args = [proj] * 14 + [y1] * 2 + [dy] * 7 + params + [dproj_gates]
    in_specs = [back(0, 0), main(0, 0), back(0, 1), main(0, 1), main(1, 0), front(1, 0), back(1, 1), main(1, 1),
                back(2, 0), main(2, 0), back(2, 1), main(2, 1), main(3, 0), main(3, 1),
                BS((TS, BW), lambda i: (i, 0)), BS((H, BW), lambda i: (jnp.minimum((i + 1) * hb, S // H - 1), 0)),
                main(0, 0), front(0, 0), main(1, 0), front(1, 0), main(2, 0), front(2, 0), main(3, 0)]
    in_specs += [full(a) for a in params] + [ANY]
    small = [SDS(pool_w.shape, F32), SDS(pool_scale.shape, F32), SDS(sconv_w.shape, F32), SDS(cconv_w.shape, F32),
             SDS(cln_g.shape, F32), SDS(cln_b.shape, F32), SDS(sln_g.shape, F32), SDS(sln_b.shape, F32),
             SDS(sgu_w.shape, F32), SDS(sgu_bias.shape, F32)]
    out_specs = [BS((4, TS, D), lambda i: (0, i, 0))] + [full(s) for s in small]
    return _pcall(body, name, (n_t,), in_specs, out_specs, [SDS(dproj_gates.shape, BF16)] + small,
                  [pltpu.VMEM((TS + 2 * H, BW), F32)] * 5 + [pltpu.VMEM((8, TS + 2 * H, BW), F32)] * 2,
                  ("arbitrary",), args, comm, aliases={len(args) - 1: 0})


def _merge_fwd(y, proj, w_up, w_out, x, g_next, name, comm=None):
    _, S, BW = y.shape
    D = x.shape[1]
    tm = _tile(S, ROWS_M, BF16_ROWS)

    def body(y_ref, pg_ref, wu_ref, wo_ref, x_ref, g_ref, o_ref, m_ref, h_ref):
        merged = None
        for g in range(4):
            t = _sig(pg_ref[g]) * _dot(y_ref[g], wu_ref[g])
            merged = t if merged is None else merged + t
        m16 = merged.astype(BF16)
        m_ref[...] = m16
        xn = x_ref[...] + _dot(m16, wo_ref[...])
        o_ref[...] = xn
        h_ref[...] = _rmsnorm_rows(xn, g_ref[...])

    once = dict(pipeline_mode=pl.Buffered(1))
    (o, m, h), extra = _pcall(
        body, name, (S // tm,),
        [BS((4, tm, BW), lambda i: (0, i, 0)), BS((4, tm, D), lambda i: (1, i, 0)),
         BS((4, BW, D), lambda i: (0, 0, 0), **once), BS((D, D), lambda i: (0, 0), **once),
         BS((tm, D), lambda i: (i, 0)), BS((1, D), lambda i: (0, 0))],
        [BS((tm, D), lambda i: (i, 0)), BS((tm, D), lambda i: (i, 0)), BS((tm, D), lambda i: (i, 0))],
        [SDS((S, D), F32), SDS((S, D), BF16), SDS((S, D), BF16)], [], ("parallel",),
        (y, proj, w_up, w_out, x, g_next), comm)
    return o, m, h, extra


def _merge_bwd(dx, y, proj, w_up, w_out, name, comm=None):
    _, S, BW = y.shape
    D = dx.shape[1]
    tm = _tile(S, ROWS_S, BF16_ROWS)

    def body(dx_ref, y_ref, pg_ref, wu_ref, wo_ref, dup_ref, dp_ref, dy_ref):
        dm = _dot_nt(dx_ref[...].astype(BF16), wo_ref[...])
        for g in range(4):
            gate = _sig(pg_ref[g])
            up = _dot(y_ref[g], wu_ref[g])
            dup = (dm * gate).astype(BF16)
            dup_ref[g] = dup
            dp_ref[g] = (dm * up * (gate * (1.0 - gate))).astype(BF16)
            dy_ref[g] = _dot_nt(dup, wu_ref[g])

    res, extra = _pcall(
        body, name, (S // tm,),
        [BS((tm, D), lambda i: (i, 0)), BS((4, tm, BW), lambda i: (0, i, 0)), BS((4, tm, D), lambda i: (1, i, 0)),
         BS((4, BW, D), lambda i: (0, 0, 0)), BS((D, D), lambda i: (0, 0))],
        [BS((4, tm, D), lambda i: (0, i, 0)), BS((4, tm, D), lambda i: (1, i, 0)), BS((4, tm, BW), lambda i: (0, i, 0))],
        [SDS((4, S, D), BF16), SDS((8, S, D), BF16), SDS((4, S, BW), F32)], [], ("parallel",),
        (dx, y, proj, w_up, w_out), comm)
    return res, extra


def _adamw(w, g, m, v):
    m = ADAM_B1 * m + (1.0 - ADAM_B1) * g
    v = ADAM_B2 * v + (1.0 - ADAM_B2) * (g * g)
    m_hat = m / (1.0 - ADAM_B1 ** ADAM_STEP)
    v_hat = v / (1.0 - ADAM_B2 ** ADAM_STEP)
    delta = -ADAM_LR * (m_hat / (jnp.sqrt(v_hat) + ADAM_EPS) + ADAM_WD * w)
    return delta, m, v


def _adamw_sharded(parts, w, m, v, name, comm=None):
    L, R, C = w.shape
    tr = _tile(R, ROWS_S, BF16_ROWS)

    def body(*refs):
        p_refs = refs[:L]
        w_ref, m_ref, v_ref, g_out, d_out, m_out, v_out = refs[L:]
        l = pl.program_id(0)
        g = None
        for d in range(N_DEV):
            t = p_refs[0][d].astype(F32)
            for j in range(1, L):
                t = jnp.where(l == j, p_refs[j][d].astype(F32), t)
            g = t if g is None else g + t
        dl, mn, vn = _adamw(w_ref[0], g, m_ref[0], v_ref[0])
        g_out[0] = g
        d_out[0] = dl
        m_out[0] = mn
        v_out[0] = vn

    def part_spec(j):
        return BS((N_DEV, tr, C), lambda l, r: (0, jnp.where(l == j, r, 0), 0))

    blk = BS((1, tr, C), lambda l, r: (l, r, 0))
    return _pcall(body, name, (L, R // tr), [part_spec(j) for j in range(L)] + [blk, blk, blk], [blk] * 4,
                  [SDS((L, R, C), F32)] * 4, [], ("parallel", "parallel"), (*parts, w, m, v), comm)


def _adamw_replicated(gathered, layout, wmv, name):
    n_b = len(gathered)
    n_p = len(layout)

    def body(*refs):
        bufs = refs[:n_b]
        prm = refs[n_b:n_b + 3 * n_p]
        outs = refs[n_b + 3 * n_p:n_b + 7 * n_p]
        sums = refs[n_b + 7 * n_p:]
        for b in range(n_b):
            s = bufs[b][0]
            for d in range(1, N_DEV):
                s = s + bufs[b][d]
            sums[b][...] = s
        for p, (b, r0, nr) in enumerate(layout):
            g = sums[b][r0:r0 + nr, :]
            d, mn, vn = _adamw(prm[3 * p][...], g, prm[3 * p + 1][...], prm[3 * p + 2][...])
            outs[4 * p][...] = g
            outs[4 * p + 1][...] = d
            outs[4 * p + 2][...] = mn
            outs[4 * p + 3][...] = vn

    flat = [a for t in wmv for a in t]
    out_shape = []
    for (w, _, _) in wmv:
        out_shape += [SDS(w.shape, F32)] * 4
    out_shape += [SDS(g.shape[1:], F32) for g in gathered]
    return pl.pallas_call(
        body, name=name, out_shape=out_shape,
        compiler_params=pltpu.CompilerParams(vmem_limit_bytes=V7X_VMEM_LIMIT),
    )(*gathered, *flat)


def _adamw_small(g, w, m, v, name):
    def body(g_ref, w_ref, m_ref, v_ref, d_out, m_out, v_out):
        d, mn, vn = _adamw(w_ref[...], g_ref[...], m_ref[...], v_ref[...])
        d_out[...] = d
        m_out[...] = mn
        v_out[...] = vn

    return pl.pallas_call(body, name=name, out_shape=[SDS(w.shape, F32)] * 3)(g, w, m, v)


def _pad_rows(a, rows):
    return jnp.pad(a, ((0, rows - a.shape[0]), (0, 0)))


def kernel(x, ffn1_norm, ffn1_w13, ffn1_w2, mix_norm, w_in, pool_w, pool_scale, sconv_w, cconv_w, cconv_ln_g, cconv_ln_b, sgu_ln_g, sgu_ln_b, sgu_w, sgu_b, w_up, w_out, ffn2_norm, ffn2_w13, ffn2_w2, final_norm, loss_target, m_ffn1_norm, m_ffn1_w13, m_ffn1_w2, m_mix_norm, m_w_in, m_pool_w, m_pool_scale, m_sconv_w, m_cconv_w, m_cconv_ln_g, m_cconv_ln_b, m_sgu_ln_g, m_sgu_ln_b, m_sgu_w, m_sgu_b, m_w_up, m_w_out, m_ffn2_norm, m_ffn2_w13, m_ffn2_w2, m_final_norm, v_ffn1_norm, v_ffn1_w13, v_ffn1_w2, v_mix_norm, v_w_in, v_pool_w, v_pool_scale, v_sconv_w, v_cconv_w, v_cconv_ln_g, v_cconv_ln_b, v_sgu_ln_g, v_sgu_ln_b, v_sgu_w, v_sgu_b, v_w_up, v_w_out, v_ffn2_norm, v_ffn2_w13, v_ffn2_w2, v_final_norm):
    P = dict(locals())
    L = ffn1_norm.shape[0]
    S, D = x.shape[1], x.shape[2]
    BW = D // 2
    GW = BW // 4
    F = ffn1_w2.shape[1] * N_DEV
    fs = ffn1_w13.shape[2]
    cw = sconv_w.shape[2]
    me = 4 * lax.axis_index("x") + 2 * lax.axis_index("y") + lax.axis_index("c")

    big = ["ffn1_w13", "ffn1_w2", "w_in", "w_up", "w_out", "ffn2_w13", "ffn2_w2"]
    shards = [(jnp.swapaxes(P[n], 1, 2) if n.endswith("w13") else P[n]).astype(BF16) for n in big]
    conv_local = jnp.concatenate([sconv_w, cconv_w], axis=1)

    def gather_of(units):
        return _gather_comm(shards, [(big.index(n), l) for n, l in units])

    def ready(n, g):
        if n.endswith("w13"):
            return g.reshape(2, F, D)
        if n.endswith("w2"):
            return g.reshape(F, D)
        if n == "w_up":
            return jnp.transpose(g, (1, 2, 0, 3)).reshape(4, BW, D)
        if n == "w_out":
            return g.reshape(D, D)
        return g

    W = {}

    def take(units, arrays):
        for (n, l), g in zip(units, arrays):
            W[n, l] = ready(n, g)

    first_units = [("ffn1_w13", 0), ("ffn1_w2", 0)]
    plan = {("ffn1_up", 0): [("w_in", 0)],
            ("ffn1_down", 0): [("w_up", 0), ("w_out", 0)],
            ("proj", 0): [("ffn2_w13", 0), ("ffn2_w2", 0)],
            ("mixers", 0): [("ffn1_w13", 1)], ("merge", 0): [("ffn1_w2", 1)],
            ("ffn2_up", 0): [("w_in", 1)], ("ffn2_down", 0): [("w_up", 1), ("w_out", 1)],
            ("ffn1_up", 1): [("ffn2_w13", 1)], ("ffn1_down", 1): [("ffn2_w2", 1)]}
    assert L <= 2

    def carried(key):
        units = [u for u in plan.get(key, []) if u[1] < L]
        return units, (gather_of(units) if units else None)

    first = _gather_comm(shards + [conv_local], [(big.index(n), l) for n, l in first_units] + [(len(big), None)])
    h, got = _rmsnorm_fwd(x[0], ffn1_norm[0][None, :], "first_norm_fwd", first)
    take(first_units, got[:2])
    conv_full = jnp.transpose(got[2], (1, 2, 0, 3)).reshape(L, SCONV_K + CCONV_K, N_DEV * cw)
    sconv_full = conv_full[:, :SCONV_K]
    cconv_full = conv_full[:, SCONV_K:]

    sgu_bias = jnp.broadcast_to(sgu_b[:, :, :, None], sgu_b.shape + (GW,))
    pool_wt = jnp.swapaxes(pool_w, 2, 3)
    sgu_wt = jnp.swapaxes(sgu_w, 2, 3)

    def row(a, l):
        return a[l][None, :]

    saved = []
    xc = x[0]
    for l in range(L):
        sv = {}
        for tag in ("ffn1", None, "ffn2"):
            if tag is None:
                sv["x_mix"] = xc
                units, comm = carried(("proj", l))
                proj, extra = _matmul_fwd(h, W["w_in", l], "proj_fwd", comm)
                take(units, extra)
                units, comm = carried(("mixers", l))
                y, sv["y1"], extra = _mixers_fwd(
                    proj, pool_w[l], row(pool_scale, l), sconv_full[l], cconv_full[l], row(cconv_ln_g, l),
                    row(cconv_ln_b, l), row(sgu_ln_g, l), row(sgu_ln_b, l), sgu_w[l], sgu_bias[l], "mixers_fwd", comm)
                take(units, extra)
                units, comm = carried(("merge", l))
                sv.update(h_mix=h, proj=proj, y=y)
                xc, sv["merged"], h, extra = _merge_fwd(y, proj, W["w_up", l], W["w_out", l], xc, row(ffn2_norm, l),
                                                        "merge_fwd", comm)
                take(units, extra)
            else:
                sv["x_" + tag] = xc
                units, comm = carried((tag + "_up", l))
                ab, extra = _matmul_fwd(h, W[tag + "_w13", l], "ffn_up_fwd", comm, w_t=True, out_dtype=BF16)
                take(units, extra)
                units, comm = carried((tag + "_down", l))
                sv.update({"h_" + tag: h, "ab_" + tag: ab})
                if tag == "ffn1":
                    g_next = row(mix_norm, l)
                else:
                    g_next = row(ffn1_norm, l + 1) if l + 1 < L else final_norm[None, :]
                xc, h, extra = _swiglu_down(ab, W[tag + "_w2", l], xc, g_next, "ffn_down_fwd", comm)
                take(units, extra)
        saved.append(sv)

    loss_part, dx, d_final, dx16 = _final_loss(xc, final_norm[None, :], loss_target[0], "loss_head")
    loss = lax.psum(loss_part[0, 0], MESH_AXES)

    R = {}
    second = []

    def rest_of_sends():
        todo = list(second)
        second.clear()
        comm = None
        if todo:
            assert len({e[3][0] for e in todo}) == 1
            comm = _scatter_comm([e[1] for e in todo], todo[0][3][0], [e[2] for e in todo])
        return todo, comm

    def settle(todo, arrays):
        for (key, g, _, stages), a in zip(todo, arrays):
            if len(stages) > 1:
                second.append((key, g, a, stages[1:]))
            else:
                R[key] = a

    wide = ["ffn1_norm", "mix_norm", "ffn2_norm", "final_norm"]
    half = ["pool_scale", "cconv_ln_g", "cconv_ln_b", "sgu_ln_g", "sgu_ln_b"]
    narrow = ["pool_w", "sgu_w", "sgu_b"]
    small_names = wide + half + narrow
    small_g = [dict() for _ in range(L)]
    widths = []
    for n in small_names:
        if P[n].shape[-1] not in widths:
            widths.append(P[n].shape[-1])
    layout, conv_at = {}, {}

    def pack(width):
        def stack_layers(n):
            return jnp.stack([small_g[l][n] for l in range(L)], axis=0)

        parts, r0 = [], 0
        for n in small_names:
            if P[n].shape[-1] != width:
                continue
            g = d_final if n == "final_norm" else stack_layers(n).reshape(-1, width)
            layout[n] = (widths.index(width), r0, g.shape[0])
            parts.append(_pad_rows(g, -(-g.shape[0] // 8) * 8))
            r0 += parts[-1].shape[0]
        if width == N_DEV * cw:
            conv_g = jnp.concatenate([stack_layers("sconv_w"), stack_layers("cconv_w")], axis=1)
            conv_g = conv_g.reshape(L * (SCONV_K + CCONV_K), N_DEV * cw)
            conv_at.update(b=widths.index(width), r0=r0, rows=conv_g.shape[0])
            parts.append(_pad_rows(conv_g, -(-conv_g.shape[0] // 8) * 8))
        return jnp.concatenate(parts, axis=0)

    gathered_small = [None] * len(widths)
    for l in reversed(range(L)):
        sv = saved[l]
        sg = small_g[l]
        for tag in ("ffn2", None, "ffn1"):
            if tag is None:
                keys, comm = rest_of_sends()
                (dup, dproj, dy), extra = _merge_bwd(dx16, sv["y"], sv["proj"], W["w_up", l], W["w_out", l],
                                                     "merge_bwd", comm)
                settle(keys, extra)
                g_out, _ = _matmul_tn(sv["merged"][None], dx16[None], 1, "w_out_grad")
                g_up, _ = _matmul_tn(sv["y"], dup, 1, "w_up_grad")
                g_out = g_out.reshape(N_DEV, D // N_DEV, D)
                g_up = jnp.transpose(g_up.reshape(4, BW, N_DEV, D // N_DEV), (2, 0, 1, 3)).reshape(
                    N_DEV, 4 * BW, D // N_DEV)
                res, (R["w_out", l], R["w_up", l]) = _mixers_bwd(
                    sv["proj"], sv["y1"], dy, dproj, pool_w[l], pool_wt[l], row(pool_scale, l), sconv_full[l], cconv_full[l],
                    row(cconv_ln_g, l), row(cconv_ln_b, l), row(sgu_ln_g, l), row(sgu_ln_b, l), sgu_w[l], sgu_wt[l],
                    sgu_bias[l], "mixers_bwd", _scatter_comm([g_out, g_up]))
                dproj = res[0]
                (sg["pool_w"], sg["pool_scale"], sg["sconv_w"], sg["cconv_w"], sg["cconv_ln_g"], sg["cconv_ln_b"],
                 sg["sgu_ln_g"], sg["sgu_ln_b"], sg["sgu_w"], dgb) = res[1:]
                sg["sgu_b"] = dgb[:, :, 0]
                comm = None
                if l == 0:
                    early = [w for w in widths if w != D]
                    comm = _gather_comm([pack(w) for w in early], [(b, None) for b in range(len(early))])
                g_in, extra = _matmul_tn(sv["h_mix"][None], dproj, N_DEV, "w_in_grad", comm)
                if l == 0:
                    for w, g in zip(early, extra):
                        gathered_small[widths.index(w)] = g
                dx, dx16, sg["mix_norm"], (r_in,) = _matmul_nt_normbwd(
                    dproj, W["w_in", l], sv["x_mix"], row(mix_norm, l), dx, "proj_bwd",
                    _scatter_comm([g_in], PEERS_SAME_CORE))
                second.append((("w_in", l), g_in, r_in, (PEERS_NEAR_OTHER, PEERS_FAR_OTHER)))
            else:
                keys, comm = rest_of_sends()
                dab, sh, extra = _ffn_bwd_hidden(dx16, W[tag + "_w2", l], sv["ab_" + tag], "ffn_hidden_bwd", comm)
                settle(keys, extra)
                keys, comm = rest_of_sends()
                g_w2, extra = _matmul_tn(sh[None], dx16[None], 1, "ffn_w2_grad", comm)
                settle(keys, extra)
                g_w2 = g_w2.reshape(N_DEV, F // N_DEV, D)
                g_w13, (R[tag + "_w2", l],) = _matmul_tn(dab, sv["h_" + tag][None], 1, "ffn_w13_grad",
                                                         _scatter_comm([g_w2]), b_shared=True)
                g_w13 = g_w13.reshape(N_DEV, fs, D)
                last = tag == "ffn1" and l == 0
                now, later = (PEERS_BUT_NEAR_OTHER, PEERS_NEAR_OTHER) if last else (PEERS_SAME_CORE, PEERS_OTHER_CORE)
                dx, dx16, sg[tag + "_norm"], (r_w13,) = _matmul_nt_normbwd(
                    dab, W[tag + "_w13", l], sv["x_" + tag], row(P[tag + "_norm"], l), dx, "ffn_up_bwd",
                    _scatter_comm([g_w13], now), w_t=True)
                second.append(((tag + "_w13", l), g_w13, r_w13, (later,)))
    grad_x = dx[None]
    out = {}

    def as2d(n, a):
        if n == "final_norm":
            return a.reshape(1, D)
        return a.reshape(-1, a.shape[-1])

    for i, n in enumerate(["w_out", "w_up", "ffn2_w13", "w_in", "ffn2_w2", "ffn1_w2", "ffn1_w13"]):
        shp = P[n].shape
        if n.endswith("w13"):
            flat, back = (lambda a: jnp.swapaxes(a, 1, 2)), (lambda a: jnp.swapaxes(a, 1, 2))
        else:
            rows, cols = math.prod(shp[1:-1]), shp[-1]
            flat, back = (lambda a: a.reshape(L, rows, cols)), (lambda a: a.reshape(shp))
        keys, comm = [], None
        if i == 0:
            keys, comm = rest_of_sends()
        elif i == 1:
            comm = _gather_comm([pack(D)], [(0, None)])
        res, extra = _adamw_sharded([R[n, l] for l in range(L)], flat(P[n]), flat(P["m_" + n]), flat(P["v_" + n]),
                                    "adamw_sharded", comm)
        if i == 0:
            settle(keys, extra)
        elif i == 1:
            gathered_small[widths.index(D)] = extra[0]
        out[n] = tuple(back(a) for a in res)

    res = _adamw_replicated(gathered_small, [layout[n] for n in small_names],
                            [(as2d(n, P[n]), as2d(n, P["m_" + n]), as2d(n, P["v_" + n])) for n in small_names],
                            "adamw_replicated")
    for p, n in enumerate(small_names):
        out[n] = tuple(a.reshape(P[n].shape) for a in res[4 * p:4 * p + 4])
    conv_sum = res[4 * len(small_names) + conv_at["b"]][conv_at["r0"]:conv_at["r0"] + conv_at["rows"]]
    conv_mine = lax.dynamic_slice_in_dim(conv_sum, me * cw, cw, axis=1)

    def conv2d(a, b):
        return jnp.concatenate([a, b], axis=1).reshape(L * (SCONV_K + CCONV_K), cw)

    cd, cm, cv = _adamw_small(conv_mine, conv2d(sconv_w, cconv_w), conv2d(m_sconv_w, m_cconv_w),
                              conv2d(v_sconv_w, v_cconv_w), "adamw_conv")
    for n, sl in (("sconv_w", slice(0, SCONV_K)), ("cconv_w", slice(SCONV_K, SCONV_K + CCONV_K))):
        out[n] = tuple(a.reshape(L, SCONV_K + CCONV_K, cw)[:, sl] for a in (conv_mine, cd, cm, cv))

    order = ["ffn1_norm", "ffn1_w13", "ffn1_w2", "mix_norm", "w_in", "pool_w", "pool_scale", "sconv_w", "cconv_w",
             "cconv_ln_g", "cconv_ln_b", "sgu_ln_g", "sgu_ln_b", "sgu_w", "sgu_b", "w_up", "w_out", "ffn2_norm",
             "ffn2_w13", "ffn2_w2", "final_norm"]
    return (loss, grad_x, *[out[n][0] for n in order], *[out[n][1] for n in order],
            *[out[n][2] for n in order], *[out[n][3] for n in order])
```

```python
import functools
import math

import jax
import jax.numpy as jnp
from jax import lax
from jax.experimental import pallas as pl
from jax.experimental.pallas import tpu as pltpu

F32 = jnp.float32
BF16 = jnp.bfloat16
EPS = 1e-6
ADAM_LR = 0.001
ADAM_B1 = 0.9
ADAM_B2 = 0.999
ADAM_EPS = 1e-08
ADAM_WD = 0.01
ADAM_STEP = 10
SGU_BLOCK = 128
SGU_CHUNK = 64
SCONV_K = 3
CCONV_K = 31
HALO = 32
V7X_VMEM_LIMIT = 48 * 1024 * 1024
LANES = 128
BF16_ROWS = 16
WIDE = 11 * LANES
ROWS_L, ROWS_M, ROWS_S = 1024, 512, 256
ELEMENTWISE_ROWS = 32
MESH_AXES = ("x", "y", "c")
N_DEV = 8
_GELU_C0 = math.sqrt(2.0 / math.pi)
_GELU_C1 = 0.044715

BS = pl.BlockSpec
SDS = jax.ShapeDtypeStruct
ANY = pl.BlockSpec(memory_space=pl.ANY)


def _tile(n, pref, align=128):
    if n <= pref:
        return n
    t = pref - pref % align
    while t > 0:
        if n % t == 0:
            return t
        t -= align
    return n


def _sig(v):
    return 1.0 / (1.0 + jnp.exp(-v))


def _gelu(v):
    t = jnp.tanh(_GELU_C0 * (v + _GELU_C1 * (v * v * v)))
    return 0.5 * v * (1.0 + t), t


def _gelu_grad(v, t):
    return 0.5 * (1.0 + t) + 0.5 * v * (1.0 - t * t) * (_GELU_C0 * (1.0 + 3.0 * _GELU_C1 * v * v))


def _ln_stats(v):
    mu = jnp.mean(v, axis=-1, keepdims=True)
    vc = v - mu
    var = jnp.mean(vc * vc, axis=-1, keepdims=True)
    rstd = lax.rsqrt(var + EPS)
    return vc * rstd, rstd


def _ln_bwd(dvh, vh, rstd):
    return rstd * (dvh - jnp.mean(dvh, axis=-1, keepdims=True) - vh * jnp.mean(dvh * vh, axis=-1, keepdims=True))


def _dot(a, b):
    return jnp.dot(a, b, preferred_element_type=F32)


def _dot_nt(a, b):
    return lax.dot_general(a, b, (((1,), (1,)), ((), ())), preferred_element_type=F32)


def _dot_tn(a, b):
    return lax.dot_general(a, b, (((0,), (0,)), ((), ())), preferred_element_type=F32)


def _mesh_pos():
    return lax.axis_index("x"), lax.axis_index("y"), lax.axis_index("c")


class _Comm:
    def __init__(self, ins, out_shape, sems, start, finish, aliases=None, middle=None):
        self.ins, self.out_shape, self.sems, self.start, self.finish = ins, out_shape, sems, start, finish
        self.aliases = aliases or {}
        self.middle = middle


def _gather_comm(shards, units):
    n_u = len(units)
    out_shape = []
    for t, l in units:
        shp = shards[t].shape if l is None else shards[t].shape[1:]
        out_shape.append(SDS((N_DEV,) + tuple(shp), shards[t].dtype))

    def upper_rows(o):
        shp = out_shape[o].shape[1:]
        assert shp[0] >= 2
        return shp[0] // 2 if len(shp) > 2 or shp[0] < 32 else shp[0] // 32 * 16

    def tools(ins, dsts, sems):
        send_sems, recv_sems, local_sems = sems
        x, y, c = _mesh_pos()
        me, sib = (x, y, c), (x, y, 1 - c)
        xn, yn, dg = (1 - x, y, c), (x, 1 - y, c), (1 - x, 1 - y, c)

        def src_of(o):
            t, l = units[o]
            return ins[t] if l is None else ins[t].at[l]

        def row(o, p, part=None):
            r = dsts[o].at[4 * p[0] + 2 * p[1] + p[2]]
            if part is None:
                return r
            h = upper_rows(o)
            return r.at[pl.ds(0, h)] if part == "upper" else r.at[pl.ds(h, out_shape[o].shape[1] - h)]

        def copy(o, k, src, dst, to):
            return pltpu.make_async_remote_copy(
                src_ref=src, dst_ref=dst, send_sem=send_sems.at[o * 8 + k], recv_sem=recv_sems.at[o * 8 + k],
                device_id=to, device_id_type=pl.DeviceIdType.MESH)

        def send(o, k):
            if k < 3:
                return copy(o, k, src_of(o), row(o, me), (sib, xn, yn)[k])
            if k == 3:
                return copy(o, k, row(o, xn, "upper"), row(o, xn, "upper"), yn)
            if k == 4:
                return copy(o, k, row(o, yn, "lower"), row(o, yn, "lower"), xn)
            blk = (xn, yn, dg)[k - 5]
            return copy(o, k, row(o, blk), row(o, blk), sib)

        def landed(o, k):
            def other(p):
                return (p[0], p[1], 1 - c)

            dst = (row(o, sib), row(o, xn), row(o, yn), row(o, dg, "upper"), row(o, dg, "lower"),
                   row(o, other(xn)), row(o, other(yn)), row(o, other(dg)))[k]
            return copy(o, k, dst, dst, me)

        def local(o):
            return pltpu.make_async_copy(src_of(o), row(o, me), local_sems.at[o])

        return send, landed, local

    def start(ins, dsts, sems):
        send, _, local = tools(ins, dsts, sems)
        for o in range(n_u):
            local(o).start()
            for k in (1, 2, 0):
                send(o, k).start()

    def middle(ins, dsts, sems):
        send, landed, _ = tools(ins, dsts, sems)
        for o in range(n_u):
            landed(o, 1).wait_recv()
            send(o, 3).start()
            landed(o, 2).wait_recv()
            send(o, 4).start()
            send(o, 5).start()
            send(o, 6).start()

    def finish(ins, dsts, sems):
        send, landed, local = tools(ins, dsts, sems)
        for o in range(n_u):
            landed(o, 3).wait_recv()
            landed(o, 4).wait_recv()
            send(o, 7).start()
        for o in range(n_u):
            for k in (0, 5, 6, 7):
                landed(o, k).wait_recv()
        for o in range(n_u):
            for k in range(8):
                send(o, k).wait_send()
            local(o).wait()

    sems = [pltpu.SemaphoreType.DMA((8 * n_u,)), pltpu.SemaphoreType.DMA((8 * n_u,)), pltpu.SemaphoreType.DMA((n_u,))]
    return _Comm(list(shards), out_shape, sems, start, finish, middle=middle)


PEERS_ALL = (1, 2, 3, 4, 5, 6, 7)
PEERS_SAME_CORE = (1, 2, 4, 6)
PEERS_OTHER_CORE = (3, 5, 7)
PEERS_BUT_NEAR_OTHER = (1, 2, 4, 6, 7)
PEERS_NEAR_OTHER = (3, 5)
PEERS_FAR_OTHER = (7,)


def _scatter_comm(parts, peers=PEERS_ALL, into=None):
    n_u = len(parts)

    def tools(ins, dsts, sems):
        send_sems, recv_sems, local_sems = sems
        x, y, c = _mesh_pos()
        me = 4 * x + 2 * y + c

        def peer(k):
            return ((x + ((k >> 2) & 1)) % 2, (y + ((k >> 1) & 1)) % 2, (c + (k & 1)) % 2)

        def copy(u, k, wait=False):
            p = peer(k)
            pi = 4 * p[0] + 2 * p[1] + p[2]
            return pltpu.make_async_remote_copy(
                src_ref=ins[u].at[pi], dst_ref=dsts[u].at[pi if wait else me],
                send_sem=send_sems.at[u * 7 + k - 1], recv_sem=recv_sems.at[u * 7 + k - 1],
                device_id=p, device_id_type=pl.DeviceIdType.MESH)

        def local(u):
            return pltpu.make_async_copy(ins[u].at[me], dsts[u].at[me], local_sems.at[u])

        return copy, local

    def start(ins, dsts, sems):
        copy, local = tools(ins, dsts, sems)
        for u in range(n_u):
            if into is None:
                local(u).start()
            for k in peers:
                copy(u, k).start()

    def finish(ins, dsts, sems):
        copy, local = tools(ins, dsts, sems)
        for u in range(n_u):
            for k in peers:
                copy(u, k, wait=True).wait()
            if into is None:
                local(u).wait()

    sems = [pltpu.SemaphoreType.DMA((7 * n_u,)), pltpu.SemaphoreType.DMA((7 * n_u,)), pltpu.SemaphoreType.DMA((n_u,))]
    aliases = {} if into is None else {n_u + u: u for u in range(n_u)}
    return _Comm(list(parts) + list(into or []), [SDS(p.shape, p.dtype) for p in parts], sems, start, finish, aliases)


def _pcall(body, name, grid, in_specs, out_specs, out_shape, scratch, sem, args, comm=None, aliases=None):
    n_i, n_o, n_s = len(in_specs), len(out_specs), len(scratch)
    aliases = aliases or {}
    if comm is None:
        res = pl.pallas_call(
            body, name=name, grid=grid, in_specs=in_specs, out_specs=out_specs, out_shape=out_shape,
            scratch_shapes=scratch, input_output_aliases=aliases,
            compiler_params=pltpu.CompilerParams(dimension_semantics=sem, vmem_limit_bytes=V7X_VMEM_LIMIT),
        )(*args)
        return res, []
    n_ci, n_co = len(comm.ins), len(comm.out_shape)

    def wrapped(*refs):
        ins = refs[:n_i]
        cins = refs[n_i:n_i + n_ci]
        outs = refs[n_i + n_ci:n_i + n_ci + n_o]
        couts = refs[n_i + n_ci + n_o:n_i + n_ci + n_o + n_co]
        rest = refs[n_i + n_ci + n_o + n_co:]
        step = 0
        for d, g in enumerate(grid):
            step = step * g + pl.program_id(d)
        n_steps = math.prod(grid)
        mid = (n_steps * 5) // 8
        staged = comm.middle is not None and 0 < mid < n_steps - 1

        @pl.when(step == 0)
        def _():
            comm.start(cins, couts, rest[n_s:])

        if staged:
            @pl.when(step == mid)
            def _():
                comm.middle(cins, couts, rest[n_s:])

        body(*ins, *outs, *rest[:n_s])

        @pl.when(step == n_steps - 1)
        def _():
            if comm.middle is not None and not staged:
                comm.middle(cins, couts, rest[n_s:])
            comm.finish(cins, couts, rest[n_s:])

    res = pl.pallas_call(
        wrapped, name=name, grid=grid, in_specs=list(in_specs) + [ANY] * n_ci,
        out_specs=list(out_specs) + [ANY] * n_co, out_shape=list(out_shape) + list(comm.out_shape),
        scratch_shapes=list(scratch) + list(comm.sems),
        input_output_aliases={**aliases, **{n_i + ci: n_o + co for ci, co in comm.aliases.items()}},
        compiler_params=pltpu.CompilerParams(dimension_semantics=("arbitrary",) * len(grid),
                                             vmem_limit_bytes=V7X_VMEM_LIMIT),
    )(*args, *comm.ins)
    return res[:n_o], res[n_o:]


def _rmsnorm_fwd(x, g, name, comm=None):
    S, D = x.shape
    tm = _tile(S, ROWS_M, BF16_ROWS)

    def body(x_ref, g_ref, h_ref):
        h_ref[...] = _rmsnorm_rows(x_ref[...], g_ref[...])

    (h,), extra = _pcall(body, name, (S // tm,), [BS((tm, D), lambda i: (i, 0)), BS((1, D), lambda i: (0, 0))],
                         [BS((tm, D), lambda i: (i, 0))], [SDS((S, D), BF16)], [], ("parallel",), (x, g), comm)
    return h, extra


def _matmul_fwd(a, w, name, comm=None, w_t=False, out_dtype=F32):
    S, K = a.shape
    C = w.shape[0]
    Fc = w.shape[1] if w_t else w.shape[2]
    tn = _tile(Fc, WIDE)
    tm = _tile(S, 2 * ROWS_L, BF16_ROWS)

    def body(a_ref, w_ref, o_ref):
        p = _dot_nt(a_ref[...], w_ref[0]) if w_t else _dot(a_ref[...], w_ref[0])
        o_ref[0] = p.astype(out_dtype)

    w_spec = BS((1, tn, K), lambda c, n, i: (c, n, 0)) if w_t else BS((1, K, tn), lambda c, n, i: (c, 0, n))
    (o,), extra = _pcall(
        body, name, (C, Fc // tn, S // tm), [BS((tm, K), lambda c, n, i: (i, 0)), w_spec],
        [BS((1, tm, tn), lambda c, n, i: (c, i, n))], [SDS((C, S, Fc), out_dtype)], [],
        ("parallel", "parallel", "parallel"), (a, w), comm)
    return o, extra


def _rmsnorm_rows(xv, g):
    r = lax.rsqrt(jnp.mean(xv * xv, axis=-1, keepdims=True) + EPS)
    return (xv * r * g).astype(BF16)


def _swiglu_down(ab, w2, x, g_next, name, comm=None):
    _, S, F = ab.shape
    D = w2.shape[1]
    tk = _tile(F, WIDE)
    tm = _tile(S, ROWS_M, BF16_ROWS)
    nk = F // tk

    te = _tile(tm, ELEMENTWISE_ROWS, BF16_ROWS)

    def body(ab_ref, w_ref, x_ref, g_ref, o_ref, h_ref, s_ref):
        k = pl.program_id(1)
        for r0 in range(0, tm, te):
            rows = slice(r0, r0 + te)
            a = ab_ref[0, rows, :].astype(F32)
            s_ref[rows, :] = (a * _sig(a) * ab_ref[1, rows, :].astype(F32)).astype(BF16)
        p = 0.5 * _dot(s_ref[...], w_ref[...])

        @pl.when(k == 0)
        def _():
            o_ref[...] = x_ref[...] + p

        @pl.when(k > 0)
        def _():
            o_ref[...] += p

        @pl.when(k == nk - 1)
        def _():
            h_ref[...] = _rmsnorm_rows(o_ref[...], g_ref[...])

    (o, h), extra = _pcall(
        body, name, (S // tm, nk),
        [BS((2, tm, tk), lambda i, k: (0, i, k)), BS((tk, D), lambda i, k: (k, 0)), BS((tm, D), lambda i, k: (i, 0)),
         BS((1, D), lambda i, k: (0, 0))],
        [BS((tm, D), lambda i, k: (i, 0)), BS((tm, D), lambda i, k: (i, 0))],
        [SDS((S, D), F32), SDS((S, D), BF16)], [pltpu.VMEM((tm, tk), BF16)], ("parallel", "arbitrary"),
        (ab, w2, x, g_next), comm)
    return o, h, extra


def _ffn_bwd_hidden(dy, w2, ab, name, comm=None):
    S, D = dy.shape
    F = w2.shape[0]
    tk = _tile(F, WIDE)
    tm = _tile(S, ROWS_M, BF16_ROWS)
    te = _tile(tm, ELEMENTWISE_ROWS, BF16_ROWS)

    def body(dy_ref, w_ref, ab_ref, dab_ref, s_ref, ds_ref):
        ds_ref[...] = 0.5 * _dot_nt(dy_ref[...].astype(BF16), w_ref[...])
        for r0 in range(0, tm, te):
            rows = slice(r0, r0 + te)
            ds = ds_ref[rows, :]
            a = ab_ref[0, rows, :].astype(F32)
            b = ab_ref[1, rows, :].astype(F32)
            sg = _sig(a)
            sa = a * sg
            dab_ref[0, rows, :] = (ds * b * (sg * (1.0 + a * (1.0 - sg)))).astype(BF16)
            dab_ref[1, rows, :] = (ds * sa).astype(BF16)
            s_ref[rows, :] = (0.5 * (sa * b)).astype(BF16)

    (dab, sh), extra = _pcall(
        body, name, (F // tk, S // tm),
        [BS((tm, D), lambda k, i: (i, 0)), BS((tk, D), lambda k, i: (k, 0)), BS((2, tm, tk), lambda k, i: (0, i, k))],
        [BS((2, tm, tk), lambda k, i: (0, i, k)), BS((tm, tk), lambda k, i: (i, k))],
        [SDS((2, S, F), BF16), SDS((S, F), BF16)], [pltpu.VMEM((tm, tk), F32)], ("parallel", "parallel"),
        (dy, w2, ab), comm)
    return dab, sh, extra


def _matmul_tn(a, b, n_c, name, comm=None, b_shared=False):
    G, S, M = a.shape
    _, _, Fc = b.shape
    C = n_c
    tM = _tile(M, WIDE)
    tn = _tile(Fc, WIDE)
    ts = _tile(S, 2 * ROWS_L if b.dtype == BF16 else ROWS_L, BF16_ROWS)
    n_s = S // ts

    def body(a_ref, b_ref, o_ref, acc):
        s = pl.program_id(4)
        p = _dot_tn(a_ref[0].astype(BF16), b_ref[0].astype(BF16))

        @pl.when(s == 0)
        def _():
            acc[...] = p

        @pl.when(s > 0)
        def _():
            acc[...] += p

        @pl.when(s == n_s - 1)
        def _():
            o_ref[0] = acc[...].astype(BF16)

    (o,), extra = _pcall(
        body, name, (G, M // tM, C, Fc // tn, n_s),
        [BS((1, ts, tM), lambda g, m, c, n, s: (g, s, m)), BS((1, ts, tn), lambda g, m, c, n, s: (c if b_shared else g * C + c, s, n))],
        [BS((1, tM, tn), lambda g, m, c, n, s: (g * C + c, m, n))], [SDS((G * C, M, Fc), BF16)],
        [pltpu.VMEM((tM, tn), F32)], ("parallel", "parallel", "parallel", "parallel", "arbitrary"), (a, b), comm)
    return o, extra


def _matmul_nt_normbwd(b, w, x, gam, dres, name, comm=None, w_t=False):
    C, S, Fc = b.shape
    D = w.shape[2] if w_t else w.shape[1]
    tk = _tile(Fc, WIDE)
    tm = _tile(S, ROWS_L, BF16_ROWS)
    te = _tile(tm, 256, 8)
    nk = Fc // tk

    def body(b_ref, w_ref, x_ref, g_ref, r_ref, dx_ref, dg_ref):
        i, c, k = pl.program_id(0), pl.program_id(1), pl.program_id(2)
        p = _dot(b_ref[0], w_ref[0]) if w_t else _dot_nt(b_ref[0], w_ref[0])
        first = jnp.logical_and(c == 0, k == 0)

        @pl.when(first)
        def _():
            dx_ref[...] = p

        @pl.when(jnp.logical_not(first))
        def _():
            dx_ref[...] += p

        @pl.when(jnp.logical_and(c == C - 1, k == nk - 1))
        def _():
            dgp = None
            for r0 in range(0, tm, te):
                rows = slice(r0, r0 + te)
                xv = x_ref[rows, :]
                r = lax.rsqrt(jnp.mean(xv * xv, axis=-1, keepdims=True) + EPS)
                xn = xv * r
                dh = dx_ref[rows, :]
                dxn = dh * g_ref[...]
                dx_ref[rows, :] = r_ref[rows, :] + r * (dxn - xn * jnp.mean(dxn * xn, axis=-1, keepdims=True))
                t = jnp.sum(dh * xn, axis=0, keepdims=True)
                dgp = t if dgp is None else dgp + t

            @pl.when(i == 0)
            def _():
                dg_ref[...] = dgp

            @pl.when(i > 0)
            def _():
                dg_ref[...] += dgp

    once = dict(pipeline_mode=pl.Buffered(1))
    (dx, dg), extra = _pcall(
        body, name, (S // tm, C, nk),
        [BS((1, tm, tk), lambda i, c, k: (c, i, k)),
         BS((1, tk, D), lambda i, c, k: (c, k, 0)) if w_t else BS((1, D, tk), lambda i, c, k: (c, 0, k)),
         BS((tm, D), lambda i, c, k: (i, 0), **once), BS((1, D), lambda i, c, k: (0, 0)),
         BS((tm, D), lambda i, c, k: (i, 0), **once)],
        [BS((tm, D), lambda i, c, k: (i, 0)), BS((1, D), lambda i, c, k: (0, 0))],
        [SDS((S, D), F32), SDS((1, D), F32)], [],
        ("arbitrary", "arbitrary", "arbitrary"), (b, w, x, gam, dres), comm)
    return dx, dg, extra


def _final_loss(x, gam, target, name):
    S, D = x.shape
    tm = _tile(S, 512, 8)

    def body(x_ref, g_ref, t_ref, loss_ref, dx_ref, dg_ref):
        i = pl.program_id(0)
        xv = x_ref[...]
        r = lax.rsqrt(jnp.mean(xv * xv, axis=-1, keepdims=True) + EPS)
        xn = xv * r
        err = xn * g_ref[...] - t_ref[...]
        part = 0.5 * jnp.sum(jnp.mean(err * err, axis=-1, keepdims=True), axis=0, keepdims=True)
        dy = err * (1.0 / D)
        dxn = dy * g_ref[...]
        dx_ref[...] = r * (dxn - xn * jnp.mean(dxn * xn, axis=-1, keepdims=True))
        dgp = jnp.sum(dy * xn, axis=0, keepdims=True)
        lp = jnp.broadcast_to(part, loss_ref.shape)

        @pl.when(i == 0)
        def _():
            dg_ref[...] = dgp
            loss_ref[...] = lp

        @pl.when(i > 0)
        def _():
            dg_ref[...] += dgp
            loss_ref[...] += lp

    res, _ = _pcall(
        body, name, (S // tm,),
        [BS((tm, D), lambda i: (i, 0)), BS((1, D), lambda i: (0, 0)), BS((tm, D), lambda i: (i, 0))],
        [BS((8, 128), lambda i: (0, 0)), BS((tm, D), lambda i: (i, 0)), BS((1, D), lambda i: (0, 0))],
        [SDS((8, 128), F32), SDS((S, D), F32), SDS((1, D), F32)], [], ("arbitrary",), (x, gam, target))
    return res


CONV_CHUNK = 32


def _fill_shifted(rot, n):
    for b in range(1, 8):
        rot[b, 0:n - 8, :] = rot[0, b:b + n - 8, :]


def _window(rot, off, r0, rows):
    b = off % 8
    return rot[b, off - b + r0:off - b + r0 + rows, :]


def _taps(rot, w_ref, offs, n_rows, out):
    for r0 in range(0, n_rows, CONV_CHUNK):
        acc = None
        for k, off in enumerate(offs):
            t = w_ref[k:k + 1, :] * _window(rot, off, r0, CONV_CHUNK)
            acc = t if acc is None else acc + t
        out[r0:r0 + CONV_CHUNK, :] = acc


def _tap_grads(rot, offs, g_plane, n_rows, dw_ref):
    for k, off in enumerate(offs):
        acc = None
        for r0 in range(0, n_rows, CONV_CHUNK):
            p = g_plane[0, r0:r0 + CONV_CHUNK, :] * _window(rot, off, r0, CONV_CHUNK)
            acc = p if acc is None else acc + p
        dw_ref[k:k + 1, :] += jnp.sum(acc, axis=0, keepdims=True)


def _sgu_masks():
    ii = lax.broadcasted_iota(jnp.int32, (SGU_BLOCK, SGU_BLOCK), 0) // SGU_CHUNK
    jj = lax.broadcasted_iota(jnp.int32, (SGU_BLOCK, SGU_BLOCK), 1) // SGU_CHUNK
    return jj <= ii, ii <= jj


def _mixers_fwd(proj, pool_w, pool_scale, sconv_w, cconv_w, cln_g, cln_b, sln_g, sln_b, sgu_w, sgu_bias, name,
                comm=None):
    _, S, D = proj.shape
    BW = D // 2
    GW = BW // 4
    TS = _tile(S, ROWS_S, SGU_BLOCK)
    H = HALO
    hb = TS // H

    def main(blk, col):
        return BS((1, TS, BW), lambda i: (blk, i, col))

    def back(blk, col):
        return BS((1, H, BW), lambda i: (blk, jnp.maximum(i * hb - 1, 0), col))

    def full(a):
        nd = a.ndim
        return BS(a.shape, lambda i: (0,) * nd)

    def body(pa_m, pa_b, xi_m, xi_b, bg_m, cg_m, cg_b, ca_m, ca_b, cb_m, cb_b, du_m, dv_m,
             pw, ps, sw, cw, clg, clb, slg, slb, gw, gbias, y_ref, y1_ref, e1, e2, e3):
        i = pl.program_id(0)
        nb = jnp.where(i > 0, 1.0, 0.0).astype(F32)
        rows = i * TS + lax.broadcasted_iota(jnp.int32, (TS, 1), 0)

        e1[0:H, :] = pa_b[0] * nb
        e1[H:H + TS, :] = pa_m[0]
        for g in range(4):
            cols = slice(g * GW, (g + 1) * GW)
            win = 2 << g
            wsum = e1[H:H + TS, cols]
            for k in range(1, win):
                wsum = wsum + e1[H - k:H - k + TS, cols]
            cnt = jnp.minimum(rows + 1, win).astype(F32)
            d = wsum / cnt - e1[H:H + TS, cols]
            yg = _dot(d.astype(BF16), pw[g].astype(BF16)) * ps[:, cols]
            y_ref[0, :, cols] = yg.astype(BF16)

        e2[0:H, :] = cg_b[0] * xi_b[0] * nb
        e2[H:H + TS, :] = cg_m[0] * xi_m[0]
        cz = sw[0:1, :] * e2[H - 2:H - 2 + TS, :]
        for k in range(1, SCONV_K):
            cz = cz + sw[k:k + 1, :] * e2[H - 2 + k:H - 2 + k + TS, :]
        y_ref[1] = (bg_m[0] * cz).astype(BF16)

        e3[0, 0:H, :] = ca_b[0] * _sig(cb_b[0]) * nb
        e3[0, H:H + TS, :] = ca_m[0] * _sig(cb_m[0])
        _fill_shifted(e3, H + TS)
        _taps(e3, cw, [H - (CCONV_K - 1) + k for k in range(CCONV_K)], TS, y1_ref)
        yh, _ = _ln_stats(y1_ref[...])
        y2 = yh * clg[...] + clb[...]
        y_ref[2] = (y2 * _sig(y2)).astype(BF16)

        u, _ = _gelu(du_m[0])
        v, _ = _gelu(dv_m[0])
        vh, _ = _ln_stats(v)
        vn = vh * slg[...] + slb[...]
        mask, _ = _sgu_masks()
        for h in range(4):
            wm = jnp.where(mask, gw[h], 0.0).astype(BF16)
            cs = slice(h * GW, (h + 1) * GW)
            for n in range(TS // SGU_BLOCK):
                rs = slice(n * SGU_BLOCK, (n + 1) * SGU_BLOCK)
                z = _dot(wm, vn[rs, cs].astype(BF16)) + gbias[h]
                y_ref[3, rs, cs] = (u[rs, cs] * z).astype(BF16)

    args = [proj] * 13 + [pool_w, pool_scale, sconv_w, cconv_w, cln_g, cln_b, sln_g, sln_b, sgu_w, sgu_bias]
    in_specs = [main(0, 0), back(0, 0), main(0, 1), back(0, 1), main(1, 0), main(1, 1), back(1, 1),
                main(2, 0), back(2, 0), main(2, 1), back(2, 1), main(3, 0), main(3, 1)]
    in_specs += [full(a) for a in args[13:]]
    (y, y1), extra = _pcall(body, name, (S // TS,), in_specs,
                            [BS((4, TS, BW), lambda i: (0, i, 0)), BS((TS, BW), lambda i: (i, 0))],
                            [SDS((4, S, BW), BF16), SDS((S, BW), F32)],
                            [pltpu.VMEM((H + TS, BW), F32)] * 2 + [pltpu.VMEM((8, H + TS, BW), F32)], ("parallel",),
                            args, comm)
    return y, y1, extra


def _mixers_bwd(proj, y1, dy, dproj_gates, pool_w, pool_wt, pool_scale, sconv_w, cconv_w, cln_g, cln_b, sln_g, sln_b,
                sgu_w, sgu_wt, sgu_bias, name, comm=None):
    _, S, D = proj.shape
    BW = D // 2
    GW = BW // 4
    TS = _tile(S, ROWS_S, SGU_BLOCK)
    H = HALO
    hb = TS // H
    n_t = S // TS
    E = TS + H

    def main(blk, col):
        return BS((1, TS, BW), lambda i: (blk, i, col))

    def back(blk, col):
        return BS((1, H, BW), lambda i: (blk, jnp.maximum(i * hb - 1, 0), col))

    def front(blk, col):
        return BS((1, H, BW), lambda i: (blk, jnp.minimum((i + 1) * hb, S // H - 1), col))

    def full(a):
        nd = a.ndim
        return BS(a.shape, lambda i: (0,) * nd)

    def body(pa_b, pa_m, xi_b, xi_m, bg_m, bg_f, cg_b, cg_m, ca_b, ca_m, cb_b, cb_m, du_m, dv_m, y1_m, y1_f,
             dya_m, dya_f, dyb_m, dyb_f, dyc_m, dyc_f, dyd_m,
             pw, pwt, ps, sw, cw, clg, clb, slg, slb, gw, gwt, gbias, _gates_in,
             dp_ref, dpw, dps, dsw, dcw, dclg, dclb, dslg, dslb, dgw, dgb,
             e1, e2, e3, e4, e5, ra, rb):
        i = pl.program_id(0)
        nb = jnp.where(i > 0, 1.0, 0.0).astype(F32)
        nf = jnp.where(i < n_t - 1, 1.0, 0.0).astype(F32)
        rows_m = i * TS + lax.broadcasted_iota(jnp.int32, (TS, 1), 0)
        rows_e = i * TS + lax.broadcasted_iota(jnp.int32, (E, 1), 0)

        @pl.when(i == 0)
        def _():
            for r in (dpw, dps, dsw, dcw, dclg, dclb, dslg, dslb, dgw, dgb):
                r[...] = jnp.zeros(r.shape, F32)

        e1[0:H, :] = pa_b[0] * nb
        e1[H:H + TS, :] = pa_m[0]
        e2[0:TS, :] = dya_m[0] * ps[...]
        e2[TS:E, :] = dya_f[0] * ps[...] * nf
        for g in range(4):
            cols = slice(g * GW, (g + 1) * GW)
            win = 2 << g
            a_m = e1[H:H + TS, cols]
            wsum = a_m
            for k in range(1, win):
                wsum = wsum + e1[H - k:H - k + TS, cols]
            d = wsum / jnp.minimum(rows_m + 1, win).astype(F32) - a_m
            d16 = d.astype(BF16)
            dyp = e2[0:E, cols].astype(BF16)
            dd = _dot(dyp, pwt[g].astype(BF16))
            e3[0:E, cols] = dd / jnp.minimum(rows_e + 1, win).astype(F32)
            da = e3[0:TS, cols] - dd[0:TS]
            for k in range(1, win):
                da = da + e3[k:k + TS, cols]
            dp_ref[0, :, cols] = da.astype(BF16)
            ypre = _dot(d16, pw[g].astype(BF16))
            dps[:, cols] += jnp.sum(dya_m[0][:, cols] * ypre, axis=0, keepdims=True)
            dpw[g] += _dot(jnp.transpose(d).astype(BF16), dyp[0:TS])

        e4[0:H, :] = cg_b[0] * xi_b[0] * nb
        e4[H:H + TS, :] = cg_m[0] * xi_m[0]
        dyb = dyb_m[0]
        e5[0:TS, :] = dyb * bg_m[0]
        e5[TS:E, :] = dyb_f[0] * bg_f[0] * nf
        dcz = e5[0:TS, :]
        cz = None
        dz = None
        for k in range(SCONV_K):
            zk = e4[H - 2 + k:H - 2 + k + TS, :]
            wk = sw[k:k + 1, :]
            cz = wk * zk if cz is None else cz + wk * zk
            t = wk * e5[2 - k:2 - k + TS, :]
            dz = t if dz is None else dz + t
            dsw[k:k + 1, :] += jnp.sum(dcz * zk, axis=0, keepdims=True)
        dp_ref[0, :, BW:2 * BW] = (dz * cg_m[0]).astype(BF16)
        dp_ref[1, :, 0:BW] = (dyb * cz).astype(BF16)
        dp_ref[1, :, BW:2 * BW] = (dz * xi_m[0]).astype(BF16)

        sgm = _sig(cb_m[0])
        ra[0, 0:H, :] = ca_b[0] * _sig(cb_b[0]) * nb
        ra[0, H:H + TS, :] = ca_m[0] * sgm
        _fill_shifted(ra, H + TS)
        fwd_offs = [H - (CCONV_K - 1) + k for k in range(CCONV_K)]
        e4[0:TS, :] = y1_m[...]
        e4[TS:E, :] = y1_f[...]
        yh, rstd = _ln_stats(e4[0:E, :])
        y2 = yh * clg[...] + clb[...]
        s2 = _sig(y2)
        e1[0:TS, :] = dyc_m[0]
        e1[TS:E, :] = dyc_f[0] * nf
        dy2 = e1[0:E, :] * (s2 * (1.0 + y2 * (1.0 - s2)))
        dclg[...] += jnp.sum((dy2 * yh)[0:TS], axis=0, keepdims=True)
        dclb[...] += jnp.sum(dy2[0:TS], axis=0, keepdims=True)
        rb[0, 0:E, :] = _ln_bwd(dy2 * clg[...], yh, rstd)
        _fill_shifted(rb, E)
        _taps(rb, cw, [CCONV_K - 1 - k for k in range(CCONV_K)], TS, e5)
        _tap_grads(ra, fwd_offs, rb, TS, dcw)
        dy0 = e5[0:TS, :]
        dp_ref[2, :, 0:BW] = (dy0 * sgm).astype(BF16)
        dp_ref[2, :, BW:2 * BW] = (dy0 * ca_m[0] * (sgm * (1.0 - sgm))).astype(BF16)

        pu = du_m[0]
        pv = dv_m[0]
        u, tu = _gelu(pu)
        v, tv = _gelu(pv)
        vh, vr = _ln_stats(v)
        vn = vh * slg[...] + slb[...]
        dyd = dyd_m[0]
        mask, mask_t = _sgu_masks()
        for h in range(4):
            wm = jnp.where(mask, gw[h], 0.0).astype(BF16)
            wmt = jnp.where(mask_t, gwt[h], 0.0).astype(BF16)
            cs = slice(h * GW, (h + 1) * GW)
            for n in range(TS // SGU_BLOCK):
                rs = slice(n * SGU_BLOCK, (n + 1) * SGU_BLOCK)
                vb = vn[rs, cs].astype(BF16)
                z = _dot(wm, vb) + gbias[h]
                dzb = dyd[rs, cs] * u[rs, cs]
                dz16 = dzb.astype(BF16)
                e3[rs, cs] = dyd[rs, cs] * z
                e4[rs, cs] = _dot(wmt, dz16)
                dgw[h] += jnp.where(mask, _dot_nt(dz16, vb), 0.0)
                dgb[h] += dzb
        dvn = e4[0:TS, :]
        dslg[...] += jnp.sum(dvn * vh, axis=0, keepdims=True)
        dslb[...] += jnp.sum(dvn, axis=0, keepdims=True)
        dv = _ln_bwd(dvn * slg[...], vh, vr)
        dp_ref[3, :, 0:BW] = (e3[0:TS, :] * _gelu_grad(pu, tu)).astype(BF16)
        dp_ref[3, :, BW:2 * BW] = (dv * _gelu_grad(pv, tv)).astype(BF16)

        @pl.when(i == n_t - 1)
        def _():
            for h in range(4):
                dgb[h] = jnp.broadcast_to(jnp.sum(dgb[h], axis=1, keepdims=True), dgb.shape[1:])

    params = [pool_w, pool_wt, pool_scale, sconv_w, cconv_w, cln_g, cln_b, sln_g, sln_b, sgu_w, sgu_wt, sgu_bias]
    args = [proj] * 14 + [y1] * 2 + [dy] * 7 + params + [dproj_gates]
    in_specs = [back(0, 0), main(0, 0), back(0, 1), main(0, 1), main(1, 0), front(1, 0), back(1, 1), main(1, 1),
                back(2, 0), main(2, 0), back(2, 1), main(2, 1), main(3, 0), main(3, 1),
                BS((TS, BW), lambda i: (i, 0)), BS((H, BW), lambda i: (jnp.minimum((i + 1) * hb, S // H - 1), 0)),
                main(0, 0), front(0, 0), main(1, 0), front(1, 0), main(2, 0), front(2, 0), main(3, 0)]
    in_specs += [full(a) for a in params] + [ANY]
    small = [SDS(pool_w.shape, F32), SDS(pool_scale.shape, F32), SDS(sconv_w.shape, F32), SDS(cconv_w.shape, F32),
             SDS(cln_g.shape, F32), SDS(cln_b.shape, F32), SDS(sln_g.shape, F32), SDS(sln_b.shape, F32),
             SDS(sgu_w.shape, F32), SDS(sgu_bias.shape, F32)]
    out_specs = [BS((4, TS, D), lambda i: (0, i, 0))] + [full(s) for s in small]
    return _pcall(body, name, (n_t,), in_specs, out_specs, [SDS(dproj_gates.shape, BF16)] + small,
                  [pltpu.VMEM((TS + 2 * H, BW), F32)] * 5 + [pltpu.VMEM((8, TS + 2 * H, BW), F32)] * 2,
                  ("arbitrary",), args, comm, aliases={len(args) - 1: 0})


def _merge_fwd(y, proj, w_up, w_out, x, g_next, name, comm=None):
    _, S, BW = y.shape
    D = x.shape[1]
    tm = _tile(S, ROWS_M, BF16_ROWS)

    def body(y_ref, pg_ref, wu_ref, wo_ref, x_ref, g_ref, o_ref, m_ref, h_ref):
        merged = None
        for g in range(4):
            t = _sig(pg_ref[g]) * _dot(y_ref[g], wu_ref[g])
            merged = t if merged is None else merged + t
        m16 = merged.astype(BF16)
        m_ref[...] = m16
        xn = x_ref[...] + _dot(m16, wo_ref[...])
        o_ref[...] = xn
        h_ref[...] = _rmsnorm_rows(xn, g_ref[...])

    once = dict(pipeline_mode=pl.Buffered(1))
    (o, m, h), extra = _pcall(
        body, name, (S // tm,),
        [BS((4, tm, BW), lambda i: (0, i, 0)), BS((4, tm, D), lambda i: (1, i, 0)),
         BS((4, BW, D), lambda i: (0, 0, 0), **once), BS((D, D), lambda i: (0, 0), **once),
         BS((tm, D), lambda i: (i, 0)), BS((1, D), lambda i: (0, 0))],
        [BS((tm, D), lambda i: (i, 0)), BS((tm, D), lambda i: (i, 0)), BS((tm, D), lambda i: (i, 0))],
        [SDS((S, D), F32), SDS((S, D), BF16), SDS((S, D), BF16)], [], ("parallel",),
        (y, proj, w_up, w_out, x, g_next), comm)
    return o, m, h, extra


def _merge_bwd(dx, y, proj, w_up, w_out, name, comm=None):
    _, S, BW = y.shape
    D = dx.shape[1]
    tm = _tile(S, ROWS_S, BF16_ROWS)

    def body(dx_ref, y_ref, pg_ref, wu_ref, wo_ref, dup_ref, dp_ref, dy_ref):
        dm = _dot_nt(dx_ref[...].astype(BF16), wo_ref[...])
        for g in range(4):
            gate = _sig(pg_ref[g])
            up = _dot(y_ref[g], wu_ref[g])
            dup = (dm * gate).astype(BF16)
            dup_ref[g] = dup
            dp_ref[g] = (dm * up * (gate * (1.0 - gate))).astype(BF16)
            dy_ref[g] = _dot_nt(dup, wu_ref[g])

    res, extra = _pcall(
        body, name, (S // tm,),
        [BS((tm, D), lambda i: (i, 0)), BS((4, tm, BW), lambda i: (0, i, 0)), BS((4, tm, D), lambda i: (1, i, 0)),
         BS((4, BW, D), lambda i: (0, 0, 0)), BS((D, D), lambda i: (0, 0))],
        [BS((4, tm, D), lambda i: (0, i, 0)), BS((4, tm, D), lambda i: (1, i, 0)), BS((4, tm, BW), lambda i: (0, i, 0))],
        [SDS((4, S, D), BF16), SDS((8, S, D), BF16), SDS((4, S, BW), F32)], [], ("parallel",),
        (dx, y, proj, w_up, w_out), comm)
    return res, extra


def _adamw(w, g, m, v):
    m = ADAM_B1 * m + (1.0 - ADAM_B1) * g
    v = ADAM_B2 * v + (1.0 - ADAM_B2) * (g * g)
    m_hat = m / (1.0 - ADAM_B1 ** ADAM_STEP)
    v_hat = v / (1.0 - ADAM_B2 ** ADAM_STEP)
    delta = -ADAM_LR * (m_hat / (jnp.sqrt(v_hat) + ADAM_EPS) + ADAM_WD * w)
    return delta, m, v


def _adamw_sharded(parts, w, m, v, name, comm=None):
    L, R, C = w.shape
    tr = _tile(R, ROWS_S, BF16_ROWS)

    def body(*refs):
        p_refs = refs[:L]
        w_ref, m_ref, v_ref, g_out, d_out, m_out, v_out = refs[L:]
        l = pl.program_id(0)
        g = None
        for d in range(N_DEV):
            t = p_refs[0][d].astype(F32)
            for j in range(1, L):
                t = jnp.where(l == j, p_refs[j][d].astype(F32), t)
            g = t if g is None else g + t
        dl, mn, vn = _adamw(w_ref[0], g, m_ref[0], v_ref[0])
        g_out[0] = g
        d_out[0] = dl
        m_out[0] = mn
        v_out[0] = vn

    def part_spec(j):
        return BS((N_DEV, tr, C), lambda l, r: (0, jnp.where(l == j, r, 0), 0))

    blk = BS((1, tr, C), lambda l, r: (l, r, 0))
    return _pcall(body, name, (L, R // tr), [part_spec(j) for j in range(L)] + [blk, blk, blk], [blk] * 4,
                  [SDS((L, R, C), F32)] * 4, [], ("parallel", "parallel"), (*parts, w, m, v), comm)


def _adamw_replicated(gathered, layout, wmv, name):
    n_b = len(gathered)
    n_p = len(layout)

    def body(*refs):
        bufs = refs[:n_b]
        prm = refs[n_b:n_b + 3 * n_p]
        outs = refs[n_b + 3 * n_p:n_b + 7 * n_p]
        sums = refs[n_b + 7 * n_p:]
        for b in range(n_b):
            s = bufs[b][0]
            for d in range(1, N_DEV):
                s = s + bufs[b][d]
            sums[b][...] = s
        for p, (b, r0, nr) in enumerate(layout):
            g = sums[b][r0:r0 + nr, :]
            d, mn, vn = _adamw(prm[3 * p][...], g, prm[3 * p + 1][...], prm[3 * p + 2][...])
            outs[4 * p][...] = g
            outs[4 * p + 1][...] = d
            outs[4 * p + 2][...] = mn
            outs[4 * p + 3][...] = vn

    flat = [a for t in wmv for a in t]
    out_shape = []
    for (w, _, _) in wmv:
        out_shape += [SDS(w.shape, F32)] * 4
    out_shape += [SDS(g.shape[1:], F32) for g in gathered]
    return pl.pallas_call(
        body, name=name, out_shape=out_shape,
        compiler_params=pltpu.CompilerParams(vmem_limit_bytes=V7X_VMEM_LIMIT),
    )(*gathered, *flat)


def _adamw_small(g, w, m, v, name):
    def body(g_ref, w_ref, m_ref, v_ref, d_out, m_out, v_out):
        d, mn, vn = _adamw(w_ref[...], g_ref[...], m_ref[...], v_ref[...])
        d_out[...] = d
        m_out[...] = mn
        v_out[...] = vn

    return pl.pallas_call(body, name=name, out_shape=[SDS(w.shape, F32)] * 3)(g, w, m, v)


def _pad_rows(a, rows):
    return jnp.pad(a, ((0, rows - a.shape[0]), (0, 0)))


def kernel(x, ffn1_norm, ffn1_w13, ffn1_w2, mix_norm, w_in, pool_w, pool_scale, sconv_w, cconv_w, cconv_ln_g, cconv_ln_b, sgu_ln_g, sgu_ln_b, sgu_w, sgu_b, w_up, w_out, ffn2_norm, ffn2_w13, ffn2_w2, final_norm, loss_target, m_ffn1_norm, m_ffn1_w13, m_ffn1_w2, m_mix_norm, m_w_in, m_pool_w, m_pool_scale, m_sconv_w, m_cconv_w, m_cconv_ln_g, m_cconv_ln_b, m_sgu_ln_g, m_sgu_ln_b, m_sgu_w, m_sgu_b, m_w_up, m_w_out, m_ffn2_norm, m_ffn2_w13, m_ffn2_w2, m_final_norm, v_ffn1_norm, v_ffn1_w13, v_ffn1_w2, v_mix_norm, v_w_in, v_pool_w, v_pool_scale, v_sconv_w, v_cconv_w, v_cconv_ln_g, v_cconv_ln_b, v_sgu_ln_g, v_sgu_ln_b, v_sgu_w, v_sgu_b, v_w_up, v_w_out, v_ffn2_norm, v_ffn2_w13, v_ffn2_w2, v_final_norm):
    P = dict(locals())
    L = ffn1_norm.shape[0]
    S, D = x.shape[1], x.shape[2]
    BW = D // 2
    GW = BW // 4
    F = ffn1_w2.shape[1] * N_DEV
    fs = ffn1_w13.shape[2]
    cw = sconv_w.shape[2]
    me = 4 * lax.axis_index("x") + 2 * lax.axis_index("y") + lax.axis_index("c")

    big = ["ffn1_w13", "ffn1_w2", "w_in", "w_up", "w_out", "ffn2_w13", "ffn2_w2"]
    shards = [(jnp.swapaxes(P[n], 1, 2) if n.endswith("w13") else P[n]).astype(BF16) for n in big]
    conv_local = jnp.concatenate([sconv_w, cconv_w], axis=1)

    def gather_of(units):
        return _gather_comm(shards, [(big.index(n), l) for n, l in units])

    def ready(n, g):
        if n.endswith("w13"):
            return g.reshape(2, F, D)
        if n.endswith("w2"):
            return g.reshape(F, D)
        if n == "w_up":
            return jnp.transpose(g, (1, 2, 0, 3)).reshape(4, BW, D)
        if n == "w_out":
            return g.reshape(D, D)
        return g

    W = {}

    def take(units, arrays):
        for (n, l), g in zip(units, arrays):
            W[n, l] = ready(n, g)

    first_units = [("ffn1_w13", 0), ("ffn1_w2", 0)]
    plan = {("ffn1_up", 0): [("w_in", 0)],
            ("ffn1_down", 0): [("w_up", 0), ("w_out", 0)],
            ("proj", 0): [("w_in", 1)],
            ("mixers", 0): [("ffn2_w13", 0)], ("merge", 0): [("ffn2_w2", 0)],
            ("ffn2_up", 0): [("ffn1_w13", 1)], ("ffn2_down", 0): [("ffn1_w2", 1)],
            ("ffn1_up", 1): [("w_up", 1), ("w_out", 1)],
            ("proj", 1): [("ffn2_w13", 1)], ("mixers", 1): [("ffn2_w2", 1)]}
    assert L <= 2

    def carried(key):
        units = [u for u in plan.get(key, []) if u[1] < L]
        return units, (gather_of(units) if units else None)

    first = _gather_comm(shards + [conv_local], [(big.index(n), l) for n, l in first_units] + [(len(big), None)])
    h, got = _rmsnorm_fwd(x[0], ffn1_norm[0][None, :], "first_norm_fwd", first)
    take(first_units, got[:2])
    conv_full = jnp.transpose(got[2], (1, 2, 0, 3)).reshape(L, SCONV_K + CCONV_K, N_DEV * cw)
    sconv_full = conv_full[:, :SCONV_K]
    cconv_full = conv_full[:, SCONV_K:]

    sgu_bias = jnp.broadcast_to(sgu_b[:, :, :, None], sgu_b.shape + (GW,))
    pool_wt = jnp.swapaxes(pool_w, 2, 3)
    sgu_wt = jnp.swapaxes(sgu_w, 2, 3)

    def row(a, l):
        return a[l][None, :]

    saved = []
    xc = x[0]
    for l in range(L):
        sv = {}
        for tag in ("ffn1", None, "ffn2"):
            if tag is None:
                sv["x_mix"] = xc
                units, comm = carried(("proj", l))
                proj, extra = _matmul_fwd(h, W["w_in", l], "proj_fwd", comm)
                take(units, extra)
                units, comm = carried(("mixers", l))
                y, sv["y1"], extra = _mixers_fwd(
                    proj, pool_w[l], row(pool_scale, l), sconv_full[l], cconv_full[l], row(cconv_ln_g, l),
                    row(cconv_ln_b, l), row(sgu_ln_g, l), row(sgu_ln_b, l), sgu_w[l], sgu_bias[l], "mixers_fwd", comm)
                take(units, extra)
                units, comm = carried(("merge", l))
                sv.update(h_mix=h, proj=proj, y=y)
                xc, sv["merged"], h, extra = _merge_fwd(y, proj, W["w_up", l], W["w_out", l], xc, row(ffn2_norm, l),
                                                        "merge_fwd", comm)
                take(units, extra)
            else:
                sv["x_" + tag] = xc
                units, comm = carried((tag + "_up", l))
                ab, extra = _matmul_fwd(h, W[tag + "_w13", l], "ffn_up_fwd", comm, w_t=True, out_dtype=BF16)
                take(units, extra)
                units, comm = carried((tag + "_down", l))
                sv.update({"h_" + tag: h, "ab_" + tag: ab})
                if tag == "ffn1":
                    g_next = row(mix_norm, l)
                else:
                    g_next = row(ffn1_norm, l + 1) if l + 1 < L else final_norm[None, :]
                xc, h, extra = _swiglu_down(ab, W[tag + "_w2", l], xc, g_next, "ffn_down_fwd", comm)
                take(units, extra)
        saved.append(sv)

    loss_part, dx, d_final = _final_loss(xc, final_norm[None, :], loss_target[0], "loss_head")
    loss = lax.psum(loss_part[0, 0], MESH_AXES)

    R = {}
    second = []

    def rest_of_sends():
        todo = list(second)
        second.clear()
        comm = None
        if todo:
            assert len({e[3][0] for e in todo}) == 1
            comm = _scatter_comm([e[1] for e in todo], todo[0][3][0], [e[2] for e in todo])
        return todo, comm

    def settle(todo, arrays):
        for (key, g, _, stages), a in zip(todo, arrays):
            if len(stages) > 1:
                second.append((key, g, a, stages[1:]))
            else:
                R[key] = a

    wide = ["ffn1_norm", "mix_norm", "ffn2_norm", "final_norm"]
    half = ["pool_scale", "cconv_ln_g", "cconv_ln_b", "sgu_ln_g", "sgu_ln_b"]
    narrow = ["pool_w", "sgu_w", "sgu_b"]
    small_names = wide + half + narrow
    small_g = [dict() for _ in range(L)]
    widths = []
    for n in small_names:
        if P[n].shape[-1] not in widths:
            widths.append(P[n].shape[-1])
    layout, conv_at = {}, {}

    def pack(width):
        def stack_layers(n):
            return jnp.stack([small_g[l][n] for l in range(L)], axis=0)

        parts, r0 = [], 0
        for n in small_names:
            if P[n].shape[-1] != width:
                continue
            g = d_final if n == "final_norm" else stack_layers(n).reshape(-1, width)
            layout[n] = (widths.index(width), r0, g.shape[0])
            parts.append(_pad_rows(g, -(-g.shape[0] // 8) * 8))
            r0 += parts[-1].shape[0]
        if width == N_DEV * cw:
            conv_g = jnp.concatenate([stack_layers("sconv_w"), stack_layers("cconv_w")], axis=1)
            conv_g = conv_g.reshape(L * (SCONV_K + CCONV_K), N_DEV * cw)
            conv_at.update(b=widths.index(width), r0=r0, rows=conv_g.shape[0])
            parts.append(_pad_rows(conv_g, -(-conv_g.shape[0] // 8) * 8))
        return jnp.concatenate(parts, axis=0)

    gathered_small = [None] * len(widths)
    for l in reversed(range(L)):
        sv = saved[l]
        sg = small_g[l]
        for tag in ("ffn2", None, "ffn1"):
            if tag is None:
                keys, comm = rest_of_sends()
                (dup, dproj, dy), extra = _merge_bwd(dx, sv["y"], sv["proj"], W["w_up", l], W["w_out", l],
                                                     "merge_bwd", comm)
                settle(keys, extra)
                g_out, _ = _matmul_tn(sv["merged"][None], dx[None], 1, "w_out_grad")
                g_up, _ = _matmul_tn(sv["y"], dup, 1, "w_up_grad")
                g_out = g_out.reshape(N_DEV, D // N_DEV, D)
                g_up = jnp.transpose(g_up.reshape(4, BW, N_DEV, D // N_DEV), (2, 0, 1, 3)).reshape(
                    N_DEV, 4 * BW, D // N_DEV)
                res, (R["w_out", l], R["w_up", l]) = _mixers_bwd(
                    sv["proj"], sv["y1"], dy, dproj, pool_w[l], pool_wt[l], row(pool_scale, l), sconv_full[l], cconv_full[l],
                    row(cconv_ln_g, l), row(cconv_ln_b, l), row(sgu_ln_g, l), row(sgu_ln_b, l), sgu_w[l], sgu_wt[l],
                    sgu_bias[l], "mixers_bwd", _scatter_comm([g_out, g_up]))
                dproj = res[0]
                (sg["pool_w"], sg["pool_scale"], sg["sconv_w"], sg["cconv_w"], sg["cconv_ln_g"], sg["cconv_ln_b"],
                 sg["sgu_ln_g"], sg["sgu_ln_b"], sg["sgu_w"], dgb) = res[1:]
                sg["sgu_b"] = dgb[:, :, 0]
                comm = None
                if l == 0:
                    early = [w for w in widths if w != D]
                    comm = _gather_comm([pack(w) for w in early], [(b, None) for b in range(len(early))])
                g_in, extra = _matmul_tn(sv["h_mix"][None], dproj, N_DEV, "w_in_grad", comm)
                if l == 0:
                    for w, g in zip(early, extra):
                        gathered_small[widths.index(w)] = g
                dx, sg["mix_norm"], (r_in,) = _matmul_nt_normbwd(
                    dproj, W["w_in", l], sv["x_mix"], row(mix_norm, l), dx, "proj_bwd",
                    _scatter_comm([g_in], PEERS_SAME_CORE))
                second.append((("w_in", l), g_in, r_in, (PEERS_NEAR_OTHER, PEERS_FAR_OTHER)))
            else:
                keys, comm = rest_of_sends()
                dab, sh, extra = _ffn_bwd_hidden(dx, W[tag + "_w2", l], sv["ab_" + tag], "ffn_hidden_bwd", comm)
                settle(keys, extra)
                keys, comm = rest_of_sends()
                g_w2, extra = _matmul_tn(sh[None], dx[None], 1, "ffn_w2_grad", comm)
                settle(keys, extra)
                g_w2 = g_w2.reshape(N_DEV, F // N_DEV, D)
                g_w13, (R[tag + "_w2", l],) = _matmul_tn(dab, sv["h_" + tag][None], 1, "ffn_w13_grad",
                                                         _scatter_comm([g_w2]), b_shared=True)
                g_w13 = g_w13.reshape(N_DEV, fs, D)
                last = tag == "ffn1" and l == 0
                now, later = (PEERS_BUT_NEAR_OTHER, PEERS_NEAR_OTHER) if last else (PEERS_SAME_CORE, PEERS_OTHER_CORE)
                dx, sg[tag + "_norm"], (r_w13,) = _matmul_nt_normbwd(
                    dab, W[tag + "_w13", l], sv["x_" + tag], row(P[tag + "_norm"], l), dx, "ffn_up_bwd",
                    _scatter_comm([g_w13], now), w_t=True)
                second.append(((tag + "_w13", l), g_w13, r_w13, (later,)))
    grad_x = dx[None]
    out = {}

    def as2d(n, a):
        if n == "final_norm":
            return a.reshape(1, D)
        return a.reshape(-1, a.shape[-1])

    for i, n in enumerate(["w_out", "w_up", "ffn2_w13", "w_in", "ffn2_w2", "ffn1_w2", "ffn1_w13"]):
        shp = P[n].shape
        if n.endswith("w13"):
            flat, back = (lambda a: jnp.swapaxes(a, 1, 2)), (lambda a: jnp.swapaxes(a, 1, 2))
        else:
            rows, cols = math.prod(shp[1:-1]), shp[-1]
            flat, back = (lambda a: a.reshape(L, rows, cols)), (lambda a: a.reshape(shp))
        keys, comm = [], None
        if i == 0:
            keys, comm = rest_of_sends()
        elif i == 1:
            comm = _gather_comm([pack(D)], [(0, None)])
        res, extra = _adamw_sharded([R[n, l] for l in range(L)], flat(P[n]), flat(P["m_" + n]), flat(P["v_" + n]),
                                    "adamw_sharded", comm)
        if i == 0:
            settle(keys, extra)
        elif i == 1:
            gathered_small[widths.index(D)] = extra[0]
        out[n] = tuple(back(a) for a in res)

    res = _adamw_replicated(gathered_small, [layout[n] for n in small_names],
                            [(as2d(n, P[n]), as2d(n, P["m_" + n]), as2d(n, P["v_" + n])) for n in small_names],
                            "adamw_replicated")
    for p, n in enumerate(small_names):
        out[n] = tuple(a.reshape(P[n].shape) for a in res[4 * p:4 * p + 4])
    conv_sum = res[4 * len(small_names) + conv_at["b"]][conv_at["r0"]:conv_at["r0"] + conv_at["rows"]]
    conv_mine = lax.dynamic_slice_in_dim(conv_sum, me * cw, cw, axis=1)

    def conv2d(a, b):
        return jnp.concatenate([a, b], axis=1).reshape(L * (SCONV_K + CCONV_K), cw)

    cd, cm, cv = _adamw_small(conv_mine, conv2d(sconv_w, cconv_w), conv2d(m_sconv_w, m_cconv_w),
                              conv2d(v_sconv_w, v_cconv_w), "adamw_conv")
    for n, sl in (("sconv_w", slice(0, SCONV_K)), ("cconv_w", slice(SCONV_K, SCONV_K + CCONV_K))):
        out[n] = tuple(a.reshape(L, SCONV_K + CCONV_K, cw)[:, sl] for a in (conv_mine, cd, cm, cv))

    order = ["ffn1_norm", "ffn1_w13", "ffn1_w2", "mix_norm", "w_in", "pool_w", "pool_scale", "sconv_w", "cconv_w",
             "cconv_ln_g", "cconv_ln_b", "sgu_ln_g", "sgu_ln_b", "sgu_w", "sgu_b", "w_up", "w_out", "ffn2_norm",
             "ffn2_w13", "ffn2_w2", "final_norm"]
    return (loss, grad_x, *[out[n][0] for n in order], *[out[n][1] for n in order],
            *[out[n][2] for n in order], *[out[n][3] for n in order])
```

```python
import functools
import math

import jax
import jax.numpy as jnp
from jax import lax
from jax.experimental import pallas as pl
from jax.experimental.pallas import tpu as pltpu

F32 = jnp.float32
BF16 = jnp.bfloat16
EPS = 1e-6
ADAM_LR = 0.001
ADAM_B1 = 0.9
ADAM_B2 = 0.999
ADAM_EPS = 1e-08
ADAM_WD = 0.01
ADAM_STEP = 10
SGU_BLOCK = 128
SGU_CHUNK = 64
SCONV_K = 3
CCONV_K = 31
HALO = 32
V7X_VMEM_LIMIT = 48 * 1024 * 1024
LANES = 128
BF16_ROWS = 16
WIDE = 11 * LANES
ROWS_L, ROWS_M, ROWS_S = 1024, 512, 256
ELEMENTWISE_ROWS = 32
MESH_AXES = ("x", "y", "c")
N_DEV = 8
_GELU_C0 = math.sqrt(2.0 / math.pi)
_GELU_C1 = 0.044715

BS = pl.BlockSpec
SDS = jax.ShapeDtypeStruct
ANY = pl.BlockSpec(memory_space=pl.ANY)


def _tile(n, pref, align=128):
    if n <= pref:
        return n
    t = pref - pref % align
    while t > 0:
        if n % t == 0:
            return t
        t -= align
    return n


def _sig(v):
    return 1.0 / (1.0 + jnp.exp(-v))


def _gelu(v):
    t = jnp.tanh(_GELU_C0 * (v + _GELU_C1 * (v * v * v)))
    return 0.5 * v * (1.0 + t), t


def _gelu_grad(v, t):
    return 0.5 * (1.0 + t) + 0.5 * v * (1.0 - t * t) * (_GELU_C0 * (1.0 + 3.0 * _GELU_C1 * v * v))


def _ln_stats(v):
    mu = jnp.mean(v, axis=-1, keepdims=True)
    vc = v - mu
    var = jnp.mean(vc * vc, axis=-1, keepdims=True)
    rstd = lax.rsqrt(var + EPS)
    return vc * rstd, rstd


def _ln_bwd(dvh, vh, rstd):
    return rstd * (dvh - jnp.mean(dvh, axis=-1, keepdims=True) - vh * jnp.mean(dvh * vh, axis=-1, keepdims=True))


def _dot(a, b):
    return jnp.dot(a, b, preferred_element_type=F32)


def _dot_nt(a, b):
    return lax.dot_general(a, b, (((1,), (1,)), ((), ())), preferred_element_type=F32)


def _dot_tn(a, b):
    return lax.dot_general(a, b, (((0,), (0,)), ((), ())), preferred_element_type=F32)


def _mesh_pos():
    return lax.axis_index("x"), lax.axis_index("y"), lax.axis_index("c")


class _Comm:
    def __init__(self, ins, out_shape, sems, start, finish, aliases=None, middle=None):
        self.ins, self.out_shape, self.sems, self.start, self.finish = ins, out_shape, sems, start, finish
        self.aliases = aliases or {}
        self.middle = middle


def _gather_comm(shards, units):
    n_u = len(units)
    out_shape = []
    for t, l in units:
        shp = shards[t].shape if l is None else shards[t].shape[1:]
        out_shape.append(SDS((N_DEV,) + tuple(shp), shards[t].dtype))

    def upper_rows(o):
        shp = out_shape[o].shape[1:]
        assert shp[0] >= 2
        return shp[0] // 2 if len(shp) > 2 or shp[0] < 32 else shp[0] // 32 * 16

    def tools(ins, dsts, sems):
        send_sems, recv_sems, local_sems = sems
        x, y, c = _mesh_pos()
        me, sib = (x, y, c), (x, y, 1 - c)
        xn, yn, dg = (1 - x, y, c), (x, 1 - y, c), (1 - x, 1 - y, c)

        def src_of(o):
            t, l = units[o]
            return ins[t] if l is None else ins[t].at[l]

        def row(o, p, part=None):
            r = dsts[o].at[4 * p[0] + 2 * p[1] + p[2]]
            if part is None:
                return r
            h = upper_rows(o)
            return r.at[pl.ds(0, h)] if part == "upper" else r.at[pl.ds(h, out_shape[o].shape[1] - h)]

        def copy(o, k, src, dst, to):
            return pltpu.make_async_remote_copy(
                src_ref=src, dst_ref=dst, send_sem=send_sems.at[o * 8 + k], recv_sem=recv_sems.at[o * 8 + k],
                device_id=to, device_id_type=pl.DeviceIdType.MESH)

        def send(o, k):
            if k < 3:
                return copy(o, k, src_of(o), row(o, me), (sib, xn, yn)[k])
            if k == 3:
                return copy(o, k, row(o, xn, "upper"), row(o, xn, "upper"), yn)
            if k == 4:
                return copy(o, k, row(o, yn, "lower"), row(o, yn, "lower"), xn)
            blk = (xn, yn, dg)[k - 5]
            return copy(o, k, row(o, blk), row(o, blk), sib)

        def landed(o, k):
            def other(p):
                return (p[0], p[1], 1 - c)

            dst = (row(o, sib), row(o, xn), row(o, yn), row(o, dg, "upper"), row(o, dg, "lower"),
                   row(o, other(xn)), row(o, other(yn)), row(o, other(dg)))[k]
            return copy(o, k, dst, dst, me)

        def local(o):
            return pltpu.make_async_copy(src_of(o), row(o, me), local_sems.at[o])

        return send, landed, local

    def start(ins, dsts, sems):
        send, _, local = tools(ins, dsts, sems)
        for o in range(n_u):
            local(o).start()
            for k in (1, 2, 0):
                send(o, k).start()

    def middle(ins, dsts, sems):
        send, landed, _ = tools(ins, dsts, sems)
        for o in range(n_u):
            landed(o, 1).wait_recv()
            send(o, 3).start()
            landed(o, 2).wait_recv()
            send(o, 4).start()
            send(o, 5).start()
            send(o, 6).start()

    def finish(ins, dsts, sems):
        send, landed, local = tools(ins, dsts, sems)
        for o in range(n_u):
            landed(o, 3).wait_recv()
            landed(o, 4).wait_recv()
            send(o, 7).start()
        for o in range(n_u):
            for k in (0, 5, 6, 7):
                landed(o, k).wait_recv()
        for o in range(n_u):
            for k in range(8):
                send(o, k).wait_send()
            local(o).wait()

    sems = [pltpu.SemaphoreType.DMA((8 * n_u,)), pltpu.SemaphoreType.DMA((8 * n_u,)), pltpu.SemaphoreType.DMA((n_u,))]
    return _Comm(list(shards), out_shape, sems, start, finish, middle=middle)


PEERS_ALL = (1, 2, 3, 4, 5, 6, 7)
PEERS_SAME_CORE = (1, 2, 4, 6)
PEERS_OTHER_CORE = (3, 5, 7)
PEERS_BUT_NEAR_OTHER = (1, 2, 4, 6, 7)
PEERS_NEAR_OTHER = (3, 5)
PEERS_FAR_OTHER = (7,)


def _scatter_comm(parts, peers=PEERS_ALL, into=None):
    n_u = len(parts)

    def tools(ins, dsts, sems):
        send_sems, recv_sems, local_sems = sems
        x, y, c = _mesh_pos()
        me = 4 * x + 2 * y + c

        def peer(k):
            return ((x + ((k >> 2) & 1)) % 2, (y + ((k >> 1) & 1)) % 2, (c + (k & 1)) % 2)

        def copy(u, k, wait=False):
            p = peer(k)
            pi = 4 * p[0] + 2 * p[1] + p[2]
            return pltpu.make_async_remote_copy(
                src_ref=ins[u].at[pi], dst_ref=dsts[u].at[pi if wait else me],
                send_sem=send_sems.at[u * 7 + k - 1], recv_sem=recv_sems.at[u * 7 + k - 1],
                device_id=p, device_id_type=pl.DeviceIdType.MESH)

        def local(u):
            return pltpu.make_async_copy(ins[u].at[me], dsts[u].at[me], local_sems.at[u])

        return copy, local

    def start(ins, dsts, sems):
        copy, local = tools(ins, dsts, sems)
        for u in range(n_u):
            if into is None:
                local(u).start()
            for k in peers:
                copy(u, k).start()

    def finish(ins, dsts, sems):
        copy, local = tools(ins, dsts, sems)
        for u in range(n_u):
            for k in peers:
                copy(u, k, wait=True).wait()
            if into is None:
                local(u).wait()

    sems = [pltpu.SemaphoreType.DMA((7 * n_u,)), pltpu.SemaphoreType.DMA((7 * n_u,)), pltpu.SemaphoreType.DMA((n_u,))]
    aliases = {} if into is None else {n_u + u: u for u in range(n_u)}
    return _Comm(list(parts) + list(into or []), [SDS(p.shape, p.dtype) for p in parts], sems, start, finish, aliases)


def _join_comm(a, b):
    n_i, n_o, n_s = len(a.ins), len(a.out_shape), len(a.sems)

    def both(stage):
        def run(ins, outs, sems):
            for comm, part in ((a, (ins[:n_i], outs[:n_o], sems[:n_s])), (b, (ins[n_i:], outs[n_o:], sems[n_s:]))):
                if getattr(comm, stage) is not None:
                    getattr(comm, stage)(*part)
        return run

    aliases = {**a.aliases, **{n_i + ci: n_o + co for ci, co in b.aliases.items()}}
    return _Comm(a.ins + b.ins, a.out_shape + b.out_shape, a.sems + b.sems, both("start"), both("finish"), aliases,
                 both("middle"))


def _pcall(body, name, grid, in_specs, out_specs, out_shape, scratch, sem, args, comm=None, aliases=None):
    n_i, n_o, n_s = len(in_specs), len(out_specs), len(scratch)
    aliases = aliases or {}
    if comm is None:
        res = pl.pallas_call(
            body, name=name, grid=grid, in_specs=in_specs, out_specs=out_specs, out_shape=out_shape,
            scratch_shapes=scratch, input_output_aliases=aliases,
            compiler_params=pltpu.CompilerParams(dimension_semantics=sem, vmem_limit_bytes=V7X_VMEM_LIMIT),
        )(*args)
        return res, []
    n_ci, n_co = len(comm.ins), len(comm.out_shape)

    def wrapped(*refs):
        ins = refs[:n_i]
        cins = refs[n_i:n_i + n_ci]
        outs = refs[n_i + n_ci:n_i + n_ci + n_o]
        couts = refs[n_i + n_ci + n_o:n_i + n_ci + n_o + n_co]
        rest = refs[n_i + n_ci + n_o + n_co:]
        step = 0
        for d, g in enumerate(grid):
            step = step * g + pl.program_id(d)
        n_steps = math.prod(grid)
        mid = (n_steps * 5) // 8
        staged = comm.middle is not None and 0 < mid < n_steps - 1

        @pl.when(step == 0)
        def _():
            comm.start(cins, couts, rest[n_s:])

        if staged:
            @pl.when(step == mid)
            def _():
                comm.middle(cins, couts, rest[n_s:])

        body(*ins, *outs, *rest[:n_s])

        @pl.when(step == n_steps - 1)
        def _():
            if comm.middle is not None and not staged:
                comm.middle(cins, couts, rest[n_s:])
            comm.finish(cins, couts, rest[n_s:])

    res = pl.pallas_call(
        wrapped, name=name, grid=grid, in_specs=list(in_specs) + [ANY] * n_ci,
        out_specs=list(out_specs) + [ANY] * n_co, out_shape=list(out_shape) + list(comm.out_shape),
        scratch_shapes=list(scratch) + list(comm.sems),
        input_output_aliases={**aliases, **{n_i + ci: n_o + co for ci, co in comm.aliases.items()}},
        compiler_params=pltpu.CompilerParams(dimension_semantics=("arbitrary",) * len(grid),
                                             vmem_limit_bytes=V7X_VMEM_LIMIT),
    )(*args, *comm.ins)
    return res[:n_o], res[n_o:]


def _rmsnorm_fwd(x, g, name, comm=None):
    S, D = x.shape
    tm = _tile(S, ROWS_M, BF16_ROWS)

    def body(x_ref, g_ref, h_ref):
        h_ref[...] = _rmsnorm_rows(x_ref[...], g_ref[...])

    (h,), extra = _pcall(body, name, (S // tm,), [BS((tm, D), lambda i: (i, 0)), BS((1, D), lambda i: (0, 0))],
                         [BS((tm, D), lambda i: (i, 0))], [SDS((S, D), BF16)], [], ("parallel",), (x, g), comm)
    return h, extra


def _matmul_fwd(a, w, name, comm=None, w_t=False, out_dtype=F32):
    S, K = a.shape
    C = w.shape[0]
    Fc = w.shape[1] if w_t else w.shape[2]
    tn = _tile(Fc, WIDE)
    tm = _tile(S, 2 * ROWS_L, BF16_ROWS)

    def body(a_ref, w_ref, o_ref):
        p = _dot_nt(a_ref[...], w_ref[0]) if w_t else _dot(a_ref[...], w_ref[0])
        o_ref[0] = p.astype(out_dtype)

    w_spec = BS((1, tn, K), lambda c, n, i: (c, n, 0)) if w_t else BS((1, K, tn), lambda c, n, i: (c, 0, n))
    (o,), extra = _pcall(
        body, name, (C, Fc // tn, S // tm), [BS((tm, K), lambda c, n, i: (i, 0)), w_spec],
        [BS((1, tm, tn), lambda c, n, i: (c, i, n))], [SDS((C, S, Fc), out_dtype)], [],
        ("parallel", "parallel", "parallel"), (a, w), comm)
    return o, extra


def _rmsnorm_rows(xv, g):
    r = lax.rsqrt(jnp.mean(xv * xv, axis=-1, keepdims=True) + EPS)
    return (xv * r * g).astype(BF16)


def _swiglu_down(ab, w2, x, g_next, name, comm=None):
    _, S, F = ab.shape
    D = w2.shape[1]
    tk = _tile(F, WIDE)
    tm = _tile(S, ROWS_M, BF16_ROWS)
    nk = F // tk

    te = _tile(tm, ELEMENTWISE_ROWS, BF16_ROWS)

    def body(ab_ref, w_ref, x_ref, g_ref, o_ref, h_ref, s_ref):
        k = pl.program_id(1)
        for r0 in range(0, tm, te):
            rows = slice(r0, r0 + te)
            a = ab_ref[0, rows, :].astype(F32)
            s_ref[rows, :] = (a * _sig(a) * ab_ref[1, rows, :].astype(F32)).astype(BF16)
        p = 0.5 * _dot(s_ref[...], w_ref[...])

        @pl.when(k == 0)
        def _():
            o_ref[...] = x_ref[...] + p

        @pl.when(k > 0)
        def _():
            o_ref[...] += p

        @pl.when(k == nk - 1)
        def _():
            h_ref[...] = _rmsnorm_rows(o_ref[...], g_ref[...])

    (o, h), extra = _pcall(
        body, name, (S // tm, nk),
        [BS((2, tm, tk), lambda i, k: (0, i, k)), BS((tk, D), lambda i, k: (k, 0)), BS((tm, D), lambda i, k: (i, 0)),
         BS((1, D), lambda i, k: (0, 0))],
        [BS((tm, D), lambda i, k: (i, 0)), BS((tm, D), lambda i, k: (i, 0))],
        [SDS((S, D), F32), SDS((S, D), BF16)], [pltpu.VMEM((tm, tk), BF16)], ("parallel", "arbitrary"),
        (ab, w2, x, g_next), comm)
    return o, h, extra


def _ffn_bwd_hidden(dy, w2, ab, name, comm=None):
    S, D = dy.shape
    F = w2.shape[0]
    tk = _tile(F, WIDE)
    tm = _tile(S, ROWS_M, BF16_ROWS)
    te = _tile(tm, ELEMENTWISE_ROWS, BF16_ROWS)

    def body(dy_ref, w_ref, ab_ref, dab_ref, s_ref, ds_ref):
        ds_ref[...] = 0.5 * _dot_nt(dy_ref[...].astype(BF16), w_ref[...])
        for r0 in range(0, tm, te):
            rows = slice(r0, r0 + te)
            ds = ds_ref[rows, :]
            a = ab_ref[0, rows, :].astype(F32)
            b = ab_ref[1, rows, :].astype(F32)
            sg = _sig(a)
            sa = a * sg
            dab_ref[0, rows, :] = (ds * b * (sg * (1.0 + a * (1.0 - sg)))).astype(BF16)
            dab_ref[1, rows, :] = (ds * sa).astype(BF16)
            s_ref[rows, :] = (0.5 * (sa * b)).astype(BF16)

    (dab, sh), extra = _pcall(
        body, name, (F // tk, S // tm),
        [BS((tm, D), lambda k, i: (i, 0)), BS((tk, D), lambda k, i: (k, 0)), BS((2, tm, tk), lambda k, i: (0, i, k))],
        [BS((2, tm, tk), lambda k, i: (0, i, k)), BS((tm, tk), lambda k, i: (i, k))],
        [SDS((2, S, F), BF16), SDS((S, F), BF16)], [pltpu.VMEM((tm, tk), F32)], ("parallel", "parallel"),
        (dy, w2, ab), comm)
    return dab, sh, extra


def _matmul_tn(a, b, n_c, name, comm=None, b_shared=False):
    G, S, M = a.shape
    _, _, Fc = b.shape
    C = n_c
    tM = _tile(M, WIDE)
    tn = _tile(Fc, WIDE)
    ts = _tile(S, 2 * ROWS_L if b.dtype == BF16 else ROWS_L, BF16_ROWS)
    n_s = S // ts

    def body(a_ref, b_ref, o_ref, acc):
        s = pl.program_id(4)
        p = _dot_tn(a_ref[0].astype(BF16), b_ref[0].astype(BF16))

        @pl.when(s == 0)
        def _():
            acc[...] = p

        @pl.when(s > 0)
        def _():
            acc[...] += p

        @pl.when(s == n_s - 1)
        def _():
            o_ref[0] = acc[...].astype(BF16)

    (o,), extra = _pcall(
        body, name, (G, M // tM, C, Fc // tn, n_s),
        [BS((1, ts, tM), lambda g, m, c, n, s: (g, s, m)), BS((1, ts, tn), lambda g, m, c, n, s: (c if b_shared else g * C + c, s, n))],
        [BS((1, tM, tn), lambda g, m, c, n, s: (g * C + c, m, n))], [SDS((G * C, M, Fc), BF16)],
        [pltpu.VMEM((tM, tn), F32)], ("parallel", "parallel", "parallel", "parallel", "arbitrary"), (a, b), comm)
    return o, extra


def _matmul_nt_normbwd(b, w, x, gam, dres, name, comm=None, w_t=False):
    C, S, Fc = b.shape
    D = w.shape[2] if w_t else w.shape[1]
    tk = _tile(Fc, WIDE)
    tm = _tile(S, ROWS_L, BF16_ROWS)
    te = _tile(tm, 256, 8)
    nk = Fc // tk

    def body(b_ref, w_ref, x_ref, g_ref, r_ref, dx_ref, dg_ref):
        i, c, k = pl.program_id(0), pl.program_id(1), pl.program_id(2)
        p = _dot(b_ref[0], w_ref[0]) if w_t else _dot_nt(b_ref[0], w_ref[0])
        first = jnp.logical_and(c == 0, k == 0)

        @pl.when(first)
        def _():
            dx_ref[...] = p

        @pl.when(jnp.logical_not(first))
        def _():
            dx_ref[...] += p

        @pl.when(jnp.logical_and(c == C - 1, k == nk - 1))
        def _():
            dgp = None
            for r0 in range(0, tm, te):
                rows = slice(r0, r0 + te)
                xv = x_ref[rows, :]
                r = lax.rsqrt(jnp.mean(xv * xv, axis=-1, keepdims=True) + EPS)
                xn = xv * r
                dh = dx_ref[rows, :]
                dxn = dh * g_ref[...]
                dx_ref[rows, :] = r_ref[rows, :] + r * (dxn - xn * jnp.mean(dxn * xn, axis=-1, keepdims=True))
                t = jnp.sum(dh * xn, axis=0, keepdims=True)
                dgp = t if dgp is None else dgp + t

            @pl.when(i == 0)
            def _():
                dg_ref[...] = dgp

            @pl.when(i > 0)
            def _():
                dg_ref[...] += dgp

    once = dict(pipeline_mode=pl.Buffered(1))
    (dx, dg), extra = _pcall(
        body, name, (S // tm, C, nk),
        [BS((1, tm, tk), lambda i, c, k: (c, i, k)),
         BS((1, tk, D), lambda i, c, k: (c, k, 0)) if w_t else BS((1, D, tk), lambda i, c, k: (c, 0, k)),
         BS((tm, D), lambda i, c, k: (i, 0), **once), BS((1, D), lambda i, c, k: (0, 0)),
         BS((tm, D), lambda i, c, k: (i, 0), **once)],
        [BS((tm, D), lambda i, c, k: (i, 0)), BS((1, D), lambda i, c, k: (0, 0))],
        [SDS((S, D), F32), SDS((1, D), F32)], [],
        ("arbitrary", "arbitrary", "arbitrary"), (b, w, x, gam, dres), comm)
    return dx, dg, extra


def _final_loss(x, gam, target, name):
    S, D = x.shape
    tm = _tile(S, 512, 8)

    def body(x_ref, g_ref, t_ref, loss_ref, dx_ref, dg_ref):
        i = pl.program_id(0)
        xv = x_ref[...]
        r = lax.rsqrt(jnp.mean(xv * xv, axis=-1, keepdims=True) + EPS)
        xn = xv * r
        err = xn * g_ref[...] - t_ref[...]
        part = 0.5 * jnp.sum(jnp.mean(err * err, axis=-1, keepdims=True), axis=0, keepdims=True)
        dy = err * (1.0 / D)
        dxn = dy * g_ref[...]
        dx_ref[...] = r * (dxn - xn * jnp.mean(dxn * xn, axis=-1, keepdims=True))
        dgp = jnp.sum(dy * xn, axis=0, keepdims=True)
        lp = jnp.broadcast_to(part, loss_ref.shape)

        @pl.when(i == 0)
        def _():
            dg_ref[...] = dgp
            loss_ref[...] = lp

        @pl.when(i > 0)
        def _():
            dg_ref[...] += dgp
            loss_ref[...] += lp

    res, _ = _pcall(
        body, name, (S // tm,),
        [BS((tm, D), lambda i: (i, 0)), BS((1, D), lambda i: (0, 0)), BS((tm, D), lambda i: (i, 0))],
        [BS((8, 128), lambda i: (0, 0)), BS((tm, D), lambda i: (i, 0)), BS((1, D), lambda i: (0, 0))],
        [SDS((8, 128), F32), SDS((S, D), F32), SDS((1, D), F32)], [], ("arbitrary",), (x, gam, target))
    return res


CONV_CHUNK = 32


def _fill_shifted(rot, n):
    for b in range(1, 8):
        rot[b, 0:n - 8, :] = rot[0, b:b + n - 8, :]


def _window(rot, off, r0, rows):
    b = off % 8
    return rot[b, off - b + r0:off - b + r0 + rows, :]


def _taps(rot, w_ref, offs, n_rows, out):
    for r0 in range(0, n_rows, CONV_CHUNK):
        acc = None
        for k, off in enumerate(offs):
            t = w_ref[k:k + 1, :] * _window(rot, off, r0, CONV_CHUNK)
            acc = t if acc is None else acc + t
        out[r0:r0 + CONV_CHUNK, :] = acc


def _tap_grads(rot, offs, g_plane, n_rows, dw_ref):
    for k, off in enumerate(offs):
        acc = None
        for r0 in range(0, n_rows, CONV_CHUNK):
            p = g_plane[0, r0:r0 + CONV_CHUNK, :] * _window(rot, off, r0, CONV_CHUNK)
            acc = p if acc is None else acc + p
        dw_ref[k:k + 1, :] += jnp.sum(acc, axis=0, keepdims=True)


def _sgu_masks():
    ii = lax.broadcasted_iota(jnp.int32, (SGU_BLOCK, SGU_BLOCK), 0) // SGU_CHUNK
    jj = lax.broadcasted_iota(jnp.int32, (SGU_BLOCK, SGU_BLOCK), 1) // SGU_CHUNK
    return jj <= ii, ii <= jj


def _mixers_fwd(proj, pool_w, pool_scale, sconv_w, cconv_w, cln_g, cln_b, sln_g, sln_b, sgu_w, sgu_bias, name,
                comm=None):
    _, S, D = proj.shape
    BW = D // 2
    GW = BW // 4
    TS = _tile(S, ROWS_S, SGU_BLOCK)
    H = HALO
    hb = TS // H

    def main(blk, col):
        return BS((1, TS, BW), lambda i: (blk, i, col))

    def back(blk, col):
        return BS((1, H, BW), lambda i: (blk, jnp.maximum(i * hb - 1, 0), col))

    def full(a):
        nd = a.ndim
        return BS(a.shape, lambda i: (0,) * nd)

    def body(pa_m, pa_b, xi_m, xi_b, bg_m, cg_m, cg_b, ca_m, ca_b, cb_m, cb_b, du_m, dv_m,
             pw, ps, sw, cw, clg, clb, slg, slb, gw, gbias, y_ref, y1_ref, e1, e2, e3):
        i = pl.program_id(0)
        nb = jnp.where(i > 0, 1.0, 0.0).astype(F32)
        rows = i * TS + lax.broadcasted_iota(jnp.int32, (TS, 1), 0)

        e1[0:H, :] = pa_b[0] * nb
        e1[H:H + TS, :] = pa_m[0]
        for g in range(4):
            cols = slice(g * GW, (g + 1) * GW)
            win = 2 << g
            wsum = e1[H:H + TS, cols]
            for k in range(1, win):
                wsum = wsum + e1[H - k:H - k + TS, cols]
            cnt = jnp.minimum(rows + 1, win).astype(F32)
            d = wsum / cnt - e1[H:H + TS, cols]
            yg = _dot(d.astype(BF16), pw[g].astype(BF16)) * ps[:, cols]
            y_ref[0, :, cols] = yg.astype(BF16)

        e2[0:H, :] = cg_b[0] * xi_b[0] * nb
        e2[H:H + TS, :] = cg_m[0] * xi_m[0]
        cz = sw[0:1, :] * e2[H - 2:H - 2 + TS, :]
        for k in range(1, SCONV_K):
            cz = cz + sw[k:k + 1, :] * e2[H - 2 + k:H - 2 + k + TS, :]
        y_ref[1] = (bg_m[0] * cz).astype(BF16)

        e3[0, 0:H, :] = ca_b[0] * _sig(cb_b[0]) * nb
        e3[0, H:H + TS, :] = ca_m[0] * _sig(cb_m[0])
        _fill_shifted(e3, H + TS)
        _taps(e3, cw, [H - (CCONV_K - 1) + k for k in range(CCONV_K)], TS, y1_ref)
        yh, _ = _ln_stats(y1_ref[...])
        y2 = yh * clg[...] + clb[...]
        y_ref[2] = (y2 * _sig(y2)).astype(BF16)

        u, _ = _gelu(du_m[0])
        v, _ = _gelu(dv_m[0])
        vh, _ = _ln_stats(v)
        vn = vh * slg[...] + slb[...]
        mask, _ = _sgu_masks()
        for h in range(4):
            wm = jnp.where(mask, gw[h], 0.0).astype(BF16)
            cs = slice(h * GW, (h + 1) * GW)
            for n in range(TS // SGU_BLOCK):
                rs = slice(n * SGU_BLOCK, (n + 1) * SGU_BLOCK)
                z = _dot(wm, vn[rs, cs].astype(BF16)) + gbias[h]
                y_ref[3, rs, cs] = (u[rs, cs] * z).astype(BF16)

    args = [proj] * 13 + [pool_w, pool_scale, sconv_w, cconv_w, cln_g, cln_b, sln_g, sln_b, sgu_w, sgu_bias]
    in_specs = [main(0, 0), back(0, 0), main(0, 1), back(0, 1), main(1, 0), main(1, 1), back(1, 1),
                main(2, 0), back(2, 0), main(2, 1), back(2, 1), main(3, 0), main(3, 1)]
    in_specs += [full(a) for a in args[13:]]
    (y, y1), extra = _pcall(body, name, (S // TS,), in_specs,
                            [BS((4, TS, BW), lambda i: (0, i, 0)), BS((TS, BW), lambda i: (i, 0))],
                            [SDS((4, S, BW), BF16), SDS((S, BW), F32)],
                            [pltpu.VMEM((H + TS, BW), F32)] * 2 + [pltpu.VMEM((8, H + TS, BW), F32)], ("parallel",),
                            args, comm)
    return y, y1, extra


def _mixers_bwd(proj, y1, dy, dproj_gates, pool_w, pool_wt, pool_scale, sconv_w, cconv_w, cln_g, cln_b, sln_g, sln_b,
                sgu_w, sgu_wt, sgu_bias, name, comm=None):
    _, S, D = proj.shape
    BW = D // 2
    GW = BW // 4
    TS = _tile(S, ROWS_S, SGU_BLOCK)
    H = HALO
    hb = TS // H
    n_t = S // TS
    E = TS + H

    def main(blk, col):
        return BS((1, TS, BW), lambda i: (blk, i, col))

    def back(blk, col):
        return BS((1, H, BW), lambda i: (blk, jnp.maximum(i * hb - 1, 0), col))

    def front(blk, col):
        return BS((1, H, BW), lambda i: (blk, jnp.minimum((i + 1) * hb, S // H - 1), col))

    def full(a):
        nd = a.ndim
        return BS(a.shape, lambda i: (0,) * nd)

    def body(pa_b, pa_m, xi_b, xi_m, bg_m, bg_f, cg_b, cg_m, ca_b, ca_m, cb_b, cb_m, du_m, dv_m, y1_m, y1_f,
             dya_m, dya_f, dyb_m, dyb_f, dyc_m, dyc_f, dyd_m,
             pw, pwt, ps, sw, cw, clg, clb, slg, slb, gw, gwt, gbias, _gates_in,
             dp_ref, dpw, dps, dsw, dcw, dclg, dclb, dslg, dslb, dgw, dgb,
             e1, e2, e3, e4, e5, ra, rb):
        i = pl.program_id(0)
        nb = jnp.where(i > 0, 1.0, 0.0).astype(F32)
        nf = jnp.where(i < n_t - 1, 1.0, 0.0).astype(F32)
        rows_m = i * TS + lax.broadcasted_iota(jnp.int32, (TS, 1), 0)
        rows_e = i * TS + lax.broadcasted_iota(jnp.int32, (E, 1), 0)

        @pl.when(i == 0)
        def _():
            for r in (dpw, dps, dsw, dcw, dclg, dclb, dslg, dslb, dgw, dgb):
                r[...] = jnp.zeros(r.shape, F32)

        e1[0:H, :] = pa_b[0] * nb
        e1[H:H + TS, :] = pa_m[0]
        e2[0:TS, :] = dya_m[0] * ps[...]
        e2[TS:E, :] = dya_f[0] * ps[...] * nf
        for g in range(4):
            cols = slice(g * GW, (g + 1) * GW)
            win = 2 << g
            a_m = e1[H:H + TS, cols]
            wsum = a_m
            for k in range(1, win):
                wsum = wsum + e1[H - k:H - k + TS, cols]
            d = wsum / jnp.minimum(rows_m + 1, win).astype(F32) - a_m
            d16 = d.astype(BF16)
            dyp = e2[0:E, cols].astype(BF16)
            dd = _dot(dyp, pwt[g].astype(BF16))
            e3[0:E, cols] = dd / jnp.minimum(rows_e + 1, win).astype(F32)
            da = e3[0:TS, cols] - dd[0:TS]
            for k in range(1, win):
                da = da + e3[k:k + TS, cols]
            dp_ref[0, :, cols] = da.astype(BF16)
            ypre = _dot(d16, pw[g].astype(BF16))
            dps[:, cols] += jnp.sum(dya_m[0][:, cols] * ypre, axis=0, keepdims=True)
            dpw[g] += _dot(jnp.transpose(d).astype(BF16), dyp[0:TS])

        e4[0:H, :] = cg_b[0] * xi_b[0] * nb
        e4[H:H + TS, :] = cg_m[0] * xi_m[0]
        dyb = dyb_m[0]
        e5[0:TS, :] = dyb * bg_m[0]
        e5[TS:E, :] = dyb_f[0] * bg_f[0] * nf
        dcz = e5[0:TS, :]
        cz = None
        dz = None
        for k in range(SCONV_K):
            zk = e4[H - 2 + k:H - 2 + k + TS, :]
            wk = sw[k:k + 1, :]
            cz = wk * zk if cz is None else cz + wk * zk
            t = wk * e5[2 - k:2 - k + TS, :]
            dz = t if dz is None else dz + t
            dsw[k:k + 1, :] += jnp.sum(dcz * zk, axis=0, keepdims=True)
        dp_ref[0, :, BW:2 * BW] = (dz * cg_m[0]).astype(BF16)
        dp_ref[1, :, 0:BW] = (dyb * cz).astype(BF16)
        dp_ref[1, :, BW:2 * BW] = (dz * xi_m[0]).astype(BF16)

        sgm = _sig(cb_m[0])
        ra[0, 0:H, :] = ca_b[0] * _sig(cb_b[0]) * nb
        ra[0, H:H + TS, :] = ca_m[0] * sgm
        _fill_shifted(ra, H + TS)
        fwd_offs = [H - (CCONV_K - 1) + k for k in range(CCONV_K)]
        e4[0:TS, :] = y1_m[...]
        e4[TS:E, :] = y1_f[...]
        yh, rstd = _ln_stats(e4[0:E, :])
        y2 = yh * clg[...] + clb[...]
        s2 = _sig(y2)
        e1[0:TS, :] = dyc_m[0]
        e1[TS:E, :] = dyc_f[0] * nf
        dy2 = e1[0:E, :] * (s2 * (1.0 + y2 * (1.0 - s2)))
        dclg[...] += jnp.sum((dy2 * yh)[0:TS], axis=0, keepdims=True)
        dclb[...] += jnp.sum(dy2[0:TS], axis=0, keepdims=True)
        rb[0, 0:E, :] = _ln_bwd(dy2 * clg[...], yh, rstd)
        _fill_shifted(rb, E)
        _taps(rb, cw, [CCONV_K - 1 - k for k in range(CCONV_K)], TS, e5)
        _tap_grads(ra, fwd_offs, rb, TS, dcw)
        dy0 = e5[0:TS, :]
        dp_ref[2, :, 0:BW] = (dy0 * sgm).astype(BF16)
        dp_ref[2, :, BW:2 * BW] = (dy0 * ca_m[0] * (sgm * (1.0 - sgm))).astype(BF16)

        pu = du_m[0]
        pv = dv_m[0]
        u, tu = _gelu(pu)
        v, tv = _gelu(pv)
        vh, vr = _ln_stats(v)
        vn = vh * slg[...] + slb[...]
        dyd = dyd_m[0]
        mask, mask_t = _sgu_masks()
        for h in range(4):
            wm = jnp.where(mask, gw[h], 0.0).astype(BF16)
            wmt = jnp.where(mask_t, gwt[h], 0.0).astype(BF16)
            cs = slice(h * GW, (h + 1) * GW)
            for n in range(TS // SGU_BLOCK):
                rs = slice(n * SGU_BLOCK, (n + 1) * SGU_BLOCK)
                vb = vn[rs, cs].astype(BF16)
                z = _dot(wm, vb) + gbias[h]
                dzb = dyd[rs, cs] * u[rs, cs]
                dz16 = dzb.astype(BF16)
                e3[rs, cs] = dyd[rs, cs] * z
                e4[rs, cs] = _dot(wmt, dz16)
                dgw[h] += jnp.where(mask, _dot_nt(dz16, vb), 0.0)
                dgb[h] += dzb
        dvn = e4[0:TS, :]
        dslg[...] += jnp.sum(dvn * vh, axis=0, keepdims=True)
        dslb[...] += jnp.sum(dvn, axis=0, keepdims=True)
        dv = _ln_bwd(dvn * slg[...], vh, vr)
        dp_ref[3, :, 0:BW] = (e3[0:TS, :] * _gelu_grad(pu, tu)).astype(BF16)
        dp_ref[3, :, BW:2 * BW] = (dv * _gelu_grad(pv, tv)).astype(BF16)

        @pl.when(i == n_t - 1)
        def _():
            for h in range(4):
                dgb[h] = jnp.broadcast_to(jnp.sum(dgb[h], axis=1, keepdims=True), dgb.shape[1:])

    params = [pool_w, pool_wt, pool_scale, sconv_w, cconv_w, cln_g, cln_b, sln_g, sln_b, sgu_w, sgu_wt, sgu_bias]
    args = [proj] * 14 + [y1] * 2 + [dy] * 7 + params + [dproj_gates]
    in_specs = [back(0, 0), main(0, 0), back(0, 1), main(0, 1), main(1, 0), front(1, 0), back(1, 1), main(1, 1),
                back(2, 0), main(2, 0), back(2, 1), main(2, 1), main(3, 0), main(3, 1),
                BS((TS, BW), lambda i: (i, 0)), BS((H, BW), lambda i: (jnp.minimum((i + 1) * hb, S // H - 1), 0)),
                main(0, 0), front(0, 0), main(1, 0), front(1, 0), main(2, 0), front(2, 0), main(3, 0)]
    in_specs += [full(a) for a in params] + [ANY]
    small = [SDS(pool_w.shape, F32), SDS(pool_scale.shape, F32), SDS(sconv_w.shape, F32), SDS(cconv_w.shape, F32),
             SDS(cln_g.shape, F32), SDS(cln_b.shape, F32), SDS(sln_g.shape, F32), SDS(sln_b.shape, F32),
             SDS(sgu_w.shape, F32), SDS(sgu_bias.shape, F32)]
    out_specs = [BS((4, TS, D), lambda i: (0, i, 0))] + [full(s) for s in small]
    return _pcall(body, name, (n_t,), in_specs, out_specs, [SDS(dproj_gates.shape, BF16)] + small,
                  [pltpu.VMEM((TS + 2 * H, BW), F32)] * 5 + [pltpu.VMEM((8, TS + 2 * H, BW), F32)] * 2,
                  ("arbitrary",), args, comm, aliases={len(args) - 1: 0})


def _merge_fwd(y, proj, w_up, w_out, x, g_next, name, comm=None):
    _, S, BW = y.shape
    D = x.shape[1]
    tm = _tile(S, ROWS_M, BF16_ROWS)

    def body(y_ref, pg_ref, wu_ref, wo_ref, x_ref, g_ref, o_ref, m_ref, h_ref):
        merged = None
        for g in range(4):
            t = _sig(pg_ref[g]) * _dot(y_ref[g], wu_ref[g])
            merged = t if merged is None else merged + t
        m16 = merged.astype(BF16)
        m_ref[...] = m16
        xn = x_ref[...] + _dot(m16, wo_ref[...])
        o_ref[...] = xn
        h_ref[...] = _rmsnorm_rows(xn, g_ref[...])

    once = dict(pipeline_mode=pl.Buffered(1))
    (o, m, h), extra = _pcall(
        body, name, (S // tm,),
        [BS((4, tm, BW), lambda i: (0, i, 0)), BS((4, tm, D), lambda i: (1, i, 0)),
         BS((4, BW, D), lambda i: (0, 0, 0), **once), BS((D, D), lambda i: (0, 0), **once),
         BS((tm, D), lambda i: (i, 0)), BS((1, D), lambda i: (0, 0))],
        [BS((tm, D), lambda i: (i, 0)), BS((tm, D), lambda i: (i, 0)), BS((tm, D), lambda i: (i, 0))],
        [SDS((S, D), F32), SDS((S, D), BF16), SDS((S, D), BF16)], [], ("parallel",),
        (y, proj, w_up, w_out, x, g_next), comm)
    return o, m, h, extra


def _merge_bwd(dx, y, proj, w_up, w_out, name, comm=None):
    _, S, BW = y.shape
    D = dx.shape[1]
    tm = _tile(S, ROWS_S, BF16_ROWS)

    def body(dx_ref, y_ref, pg_ref, wu_ref, wo_ref, dup_ref, dp_ref, dy_ref):
        dm = _dot_nt(dx_ref[...].astype(BF16), wo_ref[...])
        for g in range(4):
            gate = _sig(pg_ref[g])
            up = _dot(y_ref[g], wu_ref[g])
            dup = (dm * gate).astype(BF16)
            dup_ref[g] = dup
            dp_ref[g] = (dm * up * (gate * (1.0 - gate))).astype(BF16)
            dy_ref[g] = _dot_nt(dup, wu_ref[g])

    res, extra = _pcall(
        body, name, (S // tm,),
        [BS((tm, D), lambda i: (i, 0)), BS((4, tm, BW), lambda i: (0, i, 0)), BS((4, tm, D), lambda i: (1, i, 0)),
         BS((4, BW, D), lambda i: (0, 0, 0)), BS((D, D), lambda i: (0, 0))],
        [BS((4, tm, D), lambda i: (0, i, 0)), BS((4, tm, D), lambda i: (1, i, 0)), BS((4, tm, BW), lambda i: (0, i, 0))],
        [SDS((4, S, D), BF16), SDS((8, S, D), BF16), SDS((4, S, BW), F32)], [], ("parallel",),
        (dx, y, proj, w_up, w_out), comm)
    return res, extra


def _adamw(w, g, m, v):
    m = ADAM_B1 * m + (1.0 - ADAM_B1) * g
    v = ADAM_B2 * v + (1.0 - ADAM_B2) * (g * g)
    m_hat = m / (1.0 - ADAM_B1 ** ADAM_STEP)
    v_hat = v / (1.0 - ADAM_B2 ** ADAM_STEP)
    delta = -ADAM_LR * (m_hat / (jnp.sqrt(v_hat) + ADAM_EPS) + ADAM_WD * w)
    return delta, m, v


def _adamw_sharded(parts, w, m, v, name, comm=None):
    L, R, C = w.shape
    tr = _tile(R, ROWS_S, BF16_ROWS)

    def body(*refs):
        p_refs = refs[:L]
        w_ref, m_ref, v_ref, g_out, d_out, m_out, v_out = refs[L:]
        l = pl.program_id(0)
        g = None
        for d in range(N_DEV):
            t = p_refs[0][d].astype(F32)
            for j in range(1, L):
                t = jnp.where(l == j, p_refs[j][d].astype(F32), t)
            g = t if g is None else g + t
        dl, mn, vn = _adamw(w_ref[0], g, m_ref[0], v_ref[0])
        g_out[0] = g
        d_out[0] = dl
        m_out[0] = mn
        v_out[0] = vn

    def part_spec(j):
        return BS((N_DEV, tr, C), lambda l, r: (0, jnp.where(l == j, r, 0), 0))

    blk = BS((1, tr, C), lambda l, r: (l, r, 0))
    return _pcall(body, name, (L, R // tr), [part_spec(j) for j in range(L)] + [blk, blk, blk], [blk] * 4,
                  [SDS((L, R, C), F32)] * 4, [], ("parallel", "parallel"), (*parts, w, m, v), comm)


def _adamw_replicated(gathered, layout, wmv, name):
    n_b = len(gathered)
    n_p = len(layout)

    def body(*refs):
        bufs = refs[:n_b]
        prm = refs[n_b:n_b + 3 * n_p]
        outs = refs[n_b + 3 * n_p:n_b + 7 * n_p]
        sums = refs[n_b + 7 * n_p:]
        for b in range(n_b):
            s = bufs[b][0]
            for d in range(1, N_DEV):
                s = s + bufs[b][d]
            sums[b][...] = s
        for p, (b, r0, nr) in enumerate(layout):
            g = sums[b][r0:r0 + nr, :]
            d, mn, vn = _adamw(prm[3 * p][...], g, prm[3 * p + 1][...], prm[3 * p + 2][...])
            outs[4 * p][...] = g
            outs[4 * p + 1][...] = d
            outs[4 * p + 2][...] = mn
            outs[4 * p + 3][...] = vn

    flat = [a for t in wmv for a in t]
    out_shape = []
    for (w, _, _) in wmv:
        out_shape += [SDS(w.shape, F32)] * 4
    out_shape += [SDS(g.shape[1:], F32) for g in gathered]
    return pl.pallas_call(
        body, name=name, out_shape=out_shape,
        compiler_params=pltpu.CompilerParams(vmem_limit_bytes=V7X_VMEM_LIMIT),
    )(*gathered, *flat)


def _adamw_small(g, w, m, v, name):
    def body(g_ref, w_ref, m_ref, v_ref, d_out, m_out, v_out):
        d, mn, vn = _adamw(w_ref[...], g_ref[...], m_ref[...], v_ref[...])
        d_out[...] = d
        m_out[...] = mn
        v_out[...] = vn

    return pl.pallas_call(body, name=name, out_shape=[SDS(w.shape, F32)] * 3)(g, w, m, v)


def _pad_rows(a, rows):
    return jnp.pad(a, ((0, rows - a.shape[0]), (0, 0)))


def kernel(x, ffn1_norm, ffn1_w13, ffn1_w2, mix_norm, w_in, pool_w, pool_scale, sconv_w, cconv_w, cconv_ln_g, cconv_ln_b, sgu_ln_g, sgu_ln_b, sgu_w, sgu_b, w_up, w_out, ffn2_norm, ffn2_w13, ffn2_w2, final_norm, loss_target, m_ffn1_norm, m_ffn1_w13, m_ffn1_w2, m_mix_norm, m_w_in, m_pool_w, m_pool_scale, m_sconv_w, m_cconv_w, m_cconv_ln_g, m_cconv_ln_b, m_sgu_ln_g, m_sgu_ln_b, m_sgu_w, m_sgu_b, m_w_up, m_w_out, m_ffn2_norm, m_ffn2_w13, m_ffn2_w2, m_final_norm, v_ffn1_norm, v_ffn1_w13, v_ffn1_w2, v_mix_norm, v_w_in, v_pool_w, v_pool_scale, v_sconv_w, v_cconv_w, v_cconv_ln_g, v_cconv_ln_b, v_sgu_ln_g, v_sgu_ln_b, v_sgu_w, v_sgu_b, v_w_up, v_w_out, v_ffn2_norm, v_ffn2_w13, v_ffn2_w2, v_final_norm):
    P = dict(locals())
    L = ffn1_norm.shape[0]
    S, D = x.shape[1], x.shape[2]
    BW = D // 2
    GW = BW // 4
    F = ffn1_w2.shape[1] * N_DEV
    fs = ffn1_w13.shape[2]
    cw = sconv_w.shape[2]
    me = 4 * lax.axis_index("x") + 2 * lax.axis_index("y") + lax.axis_index("c")

    big = ["ffn1_w13", "ffn1_w2", "w_in", "w_up", "w_out", "ffn2_w13", "ffn2_w2"]
    shards = [(jnp.swapaxes(P[n], 1, 2) if n.endswith("w13") else P[n]).astype(BF16) for n in big]
    conv_local = jnp.concatenate([sconv_w, cconv_w], axis=1)

    def gather_of(units):
        return _gather_comm(shards, [(big.index(n), l) for n, l in units])

    def ready(n, g):
        if n.endswith("w13"):
            return g.reshape(2, F, D)
        if n.endswith("w2"):
            return g.reshape(F, D)
        if n == "w_up":
            return jnp.transpose(g, (1, 2, 0, 3)).reshape(4, BW, D)
        if n == "w_out":
            return g.reshape(D, D)
        return g

    W = {}

    def take(units, arrays):
        for (n, l), g in zip(units, arrays):
            W[n, l] = ready(n, g)

    first_units = [("ffn1_w13", 0), ("ffn1_w2", 0)]
    plan = {("ffn1_up", 0): [("w_in", 0)],
            ("ffn1_down", 0): [("w_up", 0), ("w_out", 0)],
            ("proj", 0): [("w_in", 1)],
            ("mixers", 0): [("ffn2_w13", 0)], ("merge", 0): [("ffn2_w2", 0)],
            ("ffn2_up", 0): [("ffn1_w13", 1)], ("ffn2_down", 0): [("ffn1_w2", 1)],
            ("ffn1_up", 1): [("w_up", 1), ("w_out", 1)],
            ("proj", 1): [("ffn2_w13", 1)], ("mixers", 1): [("ffn2_w2", 1)]}
    assert L <= 2

    def carried(key):
        units = [u for u in plan.get(key, []) if u[1] < L]
        return units, (gather_of(units) if units else None)

    first = _gather_comm(shards + [conv_local], [(big.index(n), l) for n, l in first_units] + [(len(big), None)])
    h, got = _rmsnorm_fwd(x[0], ffn1_norm[0][None, :], "first_norm_fwd", first)
    take(first_units, got[:2])
    conv_full = jnp.transpose(got[2], (1, 2, 0, 3)).reshape(L, SCONV_K + CCONV_K, N_DEV * cw)
    sconv_full = conv_full[:, :SCONV_K]
    cconv_full = conv_full[:, SCONV_K:]

    sgu_bias = jnp.broadcast_to(sgu_b[:, :, :, None], sgu_b.shape + (GW,))
    pool_wt = jnp.swapaxes(pool_w, 2, 3)
    sgu_wt = jnp.swapaxes(sgu_w, 2, 3)

    def row(a, l):
        return a[l][None, :]

    saved = []
    xc = x[0]
    for l in range(L):
        sv = {}
        for tag in ("ffn1", None, "ffn2"):
            if tag is None:
                sv["x_mix"] = xc
                units, comm = carried(("proj", l))
                proj, extra = _matmul_fwd(h, W["w_in", l], "proj_fwd", comm)
                take(units, extra)
                units, comm = carried(("mixers", l))
                y, sv["y1"], extra = _mixers_fwd(
                    proj, pool_w[l], row(pool_scale, l), sconv_full[l], cconv_full[l], row(cconv_ln_g, l),
                    row(cconv_ln_b, l), row(sgu_ln_g, l), row(sgu_ln_b, l), sgu_w[l], sgu_bias[l], "mixers_fwd", comm)
                take(units, extra)
                units, comm = carried(("merge", l))
                sv.update(h_mix=h, proj=proj, y=y)
                xc, sv["merged"], h, extra = _merge_fwd(y, proj, W["w_up", l], W["w_out", l], xc, row(ffn2_norm, l),
                                                        "merge_fwd", comm)
                take(units, extra)
            else:
                sv["x_" + tag] = xc
                units, comm = carried((tag + "_up", l))
                ab, extra = _matmul_fwd(h, W[tag + "_w13", l], "ffn_up_fwd", comm, w_t=True, out_dtype=BF16)
                take(units, extra)
                units, comm = carried((tag + "_down", l))
                sv.update({"h_" + tag: h, "ab_" + tag: ab})
                if tag == "ffn1":
                    g_next = row(mix_norm, l)
                else:
                    g_next = row(ffn1_norm, l + 1) if l + 1 < L else final_norm[None, :]
                xc, h, extra = _swiglu_down(ab, W[tag + "_w2", l], xc, g_next, "ffn_down_fwd", comm)
                take(units, extra)
        saved.append(sv)

    loss_part, dx, d_final = _final_loss(xc, final_norm[None, :], loss_target[0], "loss_head")
    loss = lax.psum(loss_part[0, 0], MESH_AXES)

    R = {}
    second = []

    def rest_of_sends():
        todo = list(second)
        second.clear()
        comm = None
        if todo:
            assert len({e[3][0] for e in todo}) == 1
            comm = _scatter_comm([e[1] for e in todo], todo[0][3][0], [e[2] for e in todo])
        return todo, comm

    def settle(todo, arrays):
        for (key, g, _, stages), a in zip(todo, arrays):
            if len(stages) > 1:
                second.append((key, g, a, stages[1:]))
            else:
                R[key] = a

    wide = ["ffn1_norm", "mix_norm", "ffn2_norm", "final_norm"]
    half = ["pool_scale", "cconv_ln_g", "cconv_ln_b", "sgu_ln_g", "sgu_ln_b"]
    narrow = ["pool_w", "sgu_w", "sgu_b"]
    small_names = wide + half + narrow
    small_g = [dict() for _ in range(L)]
    widths = []
    for n in small_names:
        if P[n].shape[-1] not in widths:
            widths.append(P[n].shape[-1])
    layout, conv_at = {}, {}

    def pack(width):
        def stack_layers(n):
            return jnp.stack([small_g[l][n] for l in range(L)], axis=0)

        parts, r0 = [], 0
        for n in small_names:
            if P[n].shape[-1] != width:
                continue
            g = d_final if n == "final_norm" else stack_layers(n).reshape(-1, width)
            layout[n] = (widths.index(width), r0, g.shape[0])
            parts.append(_pad_rows(g, -(-g.shape[0] // 8) * 8))
            r0 += parts[-1].shape[0]
        if width == N_DEV * cw:
            conv_g = jnp.concatenate([stack_layers("sconv_w"), stack_layers("cconv_w")], axis=1)
            conv_g = conv_g.reshape(L * (SCONV_K + CCONV_K), N_DEV * cw)
            conv_at.update(b=widths.index(width), r0=r0, rows=conv_g.shape[0])
            parts.append(_pad_rows(conv_g, -(-conv_g.shape[0] // 8) * 8))
        return jnp.concatenate(parts, axis=0)

    gathered_small = [None] * len(widths)
    for l in reversed(range(L)):
        sv = saved[l]
        sg = small_g[l]
        for tag in ("ffn2", None, "ffn1"):
            if tag is None:
                keys, comm = rest_of_sends()
                (dup, dproj, dy), extra = _merge_bwd(dx, sv["y"], sv["proj"], W["w_up", l], W["w_out", l],
                                                     "merge_bwd", comm)
                settle(keys, extra)
                g_out, _ = _matmul_tn(sv["merged"][None], dx[None], 1, "w_out_grad")
                g_up, _ = _matmul_tn(sv["y"], dup, 1, "w_up_grad")
                g_out = g_out.reshape(N_DEV, D // N_DEV, D)
                g_up = jnp.transpose(g_up.reshape(4, BW, N_DEV, D // N_DEV), (2, 0, 1, 3)).reshape(
                    N_DEV, 4 * BW, D // N_DEV)
                res, (R["w_out", l], R["w_up", l]) = _mixers_bwd(
                    sv["proj"], sv["y1"], dy, dproj, pool_w[l], pool_wt[l], row(pool_scale, l), sconv_full[l], cconv_full[l],
                    row(cconv_ln_g, l), row(cconv_ln_b, l), row(sgu_ln_g, l), row(sgu_ln_b, l), sgu_w[l], sgu_wt[l],
                    sgu_bias[l], "mixers_bwd", _scatter_comm([g_out, g_up]))
                dproj = res[0]
                (sg["pool_w"], sg["pool_scale"], sg["sconv_w"], sg["cconv_w"], sg["cconv_ln_g"], sg["cconv_ln_b"],
                 sg["sgu_ln_g"], sg["sgu_ln_b"], sg["sgu_w"], dgb) = res[1:]
                sg["sgu_b"] = dgb[:, :, 0]
                comm = None
                if l == 0:
                    early = [w for w in widths if w != D]
                    comm = _gather_comm([pack(w) for w in early], [(b, None) for b in range(len(early))])
                g_in, extra = _matmul_tn(sv["h_mix"][None], dproj, N_DEV, "w_in_grad", comm)
                if l == 0:
                    for w, g in zip(early, extra):
                        gathered_small[widths.index(w)] = g
                dx, sg["mix_norm"], (r_in,) = _matmul_nt_normbwd(
                    dproj, W["w_in", l], sv["x_mix"], row(mix_norm, l), dx, "proj_bwd",
                    _scatter_comm([g_in], PEERS_SAME_CORE))
                second.append((("w_in", l), g_in, r_in, (PEERS_NEAR_OTHER, PEERS_FAR_OTHER)))
            else:
                keys, comm = rest_of_sends()
                dab, sh, extra = _ffn_bwd_hidden(dx, W[tag + "_w2", l], sv["ab_" + tag], "ffn_hidden_bwd", comm)
                settle(keys, extra)
                keys, comm = rest_of_sends()
                g_w2, extra = _matmul_tn(sh[None], dx[None], 1, "ffn_w2_grad", comm)
                settle(keys, extra)
                g_w2 = g_w2.reshape(N_DEV, F // N_DEV, D)
                g_w13, (R[tag + "_w2", l],) = _matmul_tn(dab, sv["h_" + tag][None], 1, "ffn_w13_grad",
                                                         _scatter_comm([g_w2]), b_shared=True)
                g_w13 = g_w13.reshape(N_DEV, fs, D)
                last = tag == "ffn1" and l == 0
                now, later = (PEERS_BUT_NEAR_OTHER, PEERS_NEAR_OTHER) if last else (PEERS_SAME_CORE, PEERS_OTHER_CORE)
                dx, sg[tag + "_norm"], (r_w13,) = _matmul_nt_normbwd(
                    dab, W[tag + "_w13", l], sv["x_" + tag], row(P[tag + "_norm"], l), dx, "ffn_up_bwd",
                    _scatter_comm([g_w13], now), w_t=True)
                second.append(((tag + "_w13", l), g_w13, r_w13, (later,)))
    grad_x = dx[None]
    out = {}

    def as2d(n, a):
        if n == "final_norm":
            return a.reshape(1, D)
        return a.reshape(-1, a.shape[-1])

    for i, n in enumerate(["w_out", "w_up", "ffn2_w13", "w_in", "ffn2_w2", "ffn1_w2", "ffn1_w13"]):
        shp = P[n].shape
        if n.endswith("w13"):
            flat, back = (lambda a: jnp.swapaxes(a, 1, 2)), (lambda a: jnp.swapaxes(a, 1, 2))
        else:
            rows, cols = math.prod(shp[1:-1]), shp[-1]
            flat, back = (lambda a: a.reshape(L, rows, cols)), (lambda a: a.reshape(shp))
        keys, comm = [], None
        if i == 0:
            keys, sends = rest_of_sends()
            comm = _join_comm(sends, _gather_comm([pack(D)], [(0, None)]))
        res, extra = _adamw_sharded([R[n, l] for l in range(L)], flat(P[n]), flat(P["m_" + n]), flat(P["v_" + n]),
                                    "adamw_sharded", comm)
        if i == 0:
            settle(keys, extra[:len(keys)])
            gathered_small[widths.index(D)] = extra[len(keys)]
        out[n] = tuple(back(a) for a in res)

    res = _adamw_replicated(gathered_small, [layout[n] for n in small_names],
                            [(as2d(n, P[n]), as2d(n, P["m_" + n]), as2d(n, P["v_" + n])) for n in small_names],
                            "adamw_replicated")
    for p, n in enumerate(small_names):
        out[n] = tuple(a.reshape(P[n].shape) for a in res[4 * p:4 * p + 4])
    conv_sum = res[4 * len(small_names) + conv_at["b"]][conv_at["r0"]:conv_at["r0"] + conv_at["rows"]]
    conv_mine = lax.dynamic_slice_in_dim(conv_sum, me * cw, cw, axis=1)

    def conv2d(a, b):
        return jnp.concatenate([a, b], axis=1).reshape(L * (SCONV_K + CCONV_K), cw)

    cd, cm, cv = _adamw_small(conv_mine, conv2d(sconv_w, cconv_w), conv2d(m_sconv_w, m_cconv_w),
                              conv2d(v_sconv_w, v_cconv_w), "adamw_conv")
    for n, sl in (("sconv_w", slice(0, SCONV_K)), ("cconv_w", slice(SCONV_K, SCONV_K + CCONV_K))):
        out[n] = tuple(a.reshape(L, SCONV_K + CCONV_K, cw)[:, sl] for a in (conv_mine, cd, cm, cv))

    order = ["ffn1_norm", "ffn1_w13", "ffn1_w2", "mix_norm", "w_in", "pool_w", "pool_scale", "sconv_w", "cconv_w",
             "cconv_ln_g", "cconv_ln_b", "sgu_ln_g", "sgu_ln_b", "sgu_w", "sgu_b", "w_up", "w_out", "ffn2_norm",
             "ffn2_w13", "ffn2_w2", "final_norm"]
    return (loss, grad_x, *[out[n][0] for n in order], *[out[n][1] for n in order],
            *[out[n][2] for n in order], *[out[n][3] for n in order])
```

```python
import math

import jax
import jax.numpy as jnp
from jax import lax
from jax.experimental import pallas as pl
from jax.experimental.pallas import tpu as pltpu

F32 = jnp.float32
BF16 = jnp.bfloat16
EPS = 1e-6
ADAM_LR = 0.001
ADAM_B1 = 0.9
ADAM_B2 = 0.999
ADAM_EPS = 1e-08
ADAM_WD = 0.01
ADAM_STEP = 10
SGU_BLOCK = 128
SGU_CHUNK = 64
SCONV_K = 3
CCONV_K = 31
HALO = 32
V7X_VMEM_LIMIT = 48 * 1024 * 1024
LANES = 128
BF16_ROWS = 16
WIDE = 11 * LANES
ROWS_L, ROWS_M, ROWS_S = 1024, 512, 256
ELEMENTWISE_ROWS = 32
MESH_AXES = ("x", "y", "c")
N_DEV = 8
_GELU_C0 = math.sqrt(2.0 / math.pi)
_GELU_C1 = 0.044715

BS = pl.BlockSpec
SDS = jax.ShapeDtypeStruct
ANY = pl.BlockSpec(memory_space=pl.ANY)


def _tile(n, pref, align=128):
    if n <= pref:
        return n
    t = pref - pref % align
    while t > 0:
        if n % t == 0:
            return t
        t -= align
    return n


def _sig(v):
    return 1.0 / (1.0 + jnp.exp(-v))


def _gelu(v):
    t = jnp.tanh(_GELU_C0 * (v + _GELU_C1 * (v * v * v)))
    return 0.5 * v * (1.0 + t), t


def _gelu_grad(v, t):
    return 0.5 * (1.0 + t) + 0.5 * v * (1.0 - t * t) * (_GELU_C0 * (1.0 + 3.0 * _GELU_C1 * v * v))


def _ln_stats(v):
    mu = jnp.mean(v, axis=-1, keepdims=True)
    vc = v - mu
    var = jnp.mean(vc * vc, axis=-1, keepdims=True)
    rstd = lax.rsqrt(var + EPS)
    return vc * rstd, rstd


def _ln_bwd(dvh, vh, rstd):
    return rstd * (dvh - jnp.mean(dvh, axis=-1, keepdims=True) - vh * jnp.mean(dvh * vh, axis=-1, keepdims=True))


def _dot(a, b):
    return jnp.dot(a, b, preferred_element_type=F32)


def _dot_nt(a, b):
    return lax.dot_general(a, b, (((1,), (1,)), ((), ())), preferred_element_type=F32)


def _dot_tn(a, b):
    return lax.dot_general(a, b, (((0,), (0,)), ((), ())), preferred_element_type=F32)


def _mesh_pos():
    return lax.axis_index("x"), lax.axis_index("y"), lax.axis_index("c")


class _Comm:
    def __init__(self, ins, out_shape, sems, start, finish, aliases=None, middle=None):
        self.ins, self.out_shape, self.sems, self.start, self.finish = ins, out_shape, sems, start, finish
        self.aliases = aliases or {}
        self.middle = middle


def _gather_comm(shards, units):
    n_u = len(units)
    out_shape = []
    for t, l in units:
        shp = shards[t].shape if l is None else shards[t].shape[1:]
        out_shape.append(SDS((N_DEV,) + tuple(shp), shards[t].dtype))

    def upper_rows(o):
        shp = out_shape[o].shape[1:]
        assert shp[0] >= 2
        return shp[0] // 2 if len(shp) > 2 or shp[0] < 32 else shp[0] // 32 * 16

    def tools(ins, dsts, sems):
        send_sems, recv_sems, local_sems = sems
        x, y, c = _mesh_pos()
        me, sib = (x, y, c), (x, y, 1 - c)
        xn, yn, dg = (1 - x, y, c), (x, 1 - y, c), (1 - x, 1 - y, c)

        def src_of(o):
            t, l = units[o]
            return ins[t] if l is None else ins[t].at[l]

        def row(o, p, part=None):
            r = dsts[o].at[4 * p[0] + 2 * p[1] + p[2]]
            if part is None:
                return r
            h = upper_rows(o)
            return r.at[pl.ds(0, h)] if part == "upper" else r.at[pl.ds(h, out_shape[o].shape[1] - h)]

        def copy(o, k, src, dst, to):
            return pltpu.make_async_remote_copy(
                src_ref=src, dst_ref=dst, send_sem=send_sems.at[o * 8 + k], recv_sem=recv_sems.at[o * 8 + k],
                device_id=to, device_id_type=pl.DeviceIdType.MESH)

        def send(o, k):
            if k < 3:
                return copy(o, k, src_of(o), row(o, me), (sib, xn, yn)[k])
            if k == 3:
                return copy(o, k, row(o, xn, "upper"), row(o, xn, "upper"), yn)
            if k == 4:
                return copy(o, k, row(o, yn, "lower"), row(o, yn, "lower"), xn)
            blk = (xn, yn, dg)[k - 5]
            return copy(o, k, row(o, blk), row(o, blk), sib)

        def landed(o, k):
            def other(p):
                return (p[0], p[1], 1 - c)

            dst = (row(o, sib), row(o, xn), row(o, yn), row(o, dg, "upper"), row(o, dg, "lower"),
                   row(o, other(xn)), row(o, other(yn)), row(o, other(dg)))[k]
            return copy(o, k, dst, dst, me)

        def local(o):
            return pltpu.make_async_copy(src_of(o), row(o, me), local_sems.at[o])

        return send, landed, local

    def start(ins, dsts, sems):
        send, _, local = tools(ins, dsts, sems)
        for o in range(n_u):
            local(o).start()
            for k in (1, 2, 0):
                send(o, k).start()

    def middle(ins, dsts, sems):
        send, landed, _ = tools(ins, dsts, sems)
        for o in range(n_u):
            landed(o, 1).wait_recv()
            send(o, 3).start()
            landed(o, 2).wait_recv()
            send(o, 4).start()
            send(o, 5).start()
            send(o, 6).start()

    def finish(ins, dsts, sems):
        send, landed, local = tools(ins, dsts, sems)
        for o in range(n_u):
            landed(o, 3).wait_recv()
            landed(o, 4).wait_recv()
            send(o, 7).start()
        for o in range(n_u):
            for k in (0, 5, 6, 7):
                landed(o, k).wait_recv()
        for o in range(n_u):
            for k in range(8):
                send(o, k).wait_send()
            local(o).wait()

    sems = [pltpu.SemaphoreType.DMA((8 * n_u,)), pltpu.SemaphoreType.DMA((8 * n_u,)), pltpu.SemaphoreType.DMA((n_u,))]
    return _Comm(list(shards), out_shape, sems, start, finish, middle=middle)


PEERS_ALL = (1, 2, 3, 4, 5, 6, 7)
PEERS_SAME_CORE = (1, 2, 4, 6)
PEERS_OTHER_CORE = (3, 5, 7)
PEERS_BUT_NEAR_OTHER = (1, 2, 4, 6, 7)
PEERS_NEAR_OTHER = (3, 5)
PEERS_FAR_OTHER = (7,)


def _scatter_comm(parts, peers=PEERS_ALL, into=None):
    n_u = len(parts)

    def tools(ins, dsts, sems):
        send_sems, recv_sems, local_sems = sems
        x, y, c = _mesh_pos()
        me = 4 * x + 2 * y + c

        def peer(k):
            return ((x + ((k >> 2) & 1)) % 2, (y + ((k >> 1) & 1)) % 2, (c + (k & 1)) % 2)

        def copy(u, k, wait=False):
            p = peer(k)
            pi = 4 * p[0] + 2 * p[1] + p[2]
            return pltpu.make_async_remote_copy(
                src_ref=ins[u].at[pi], dst_ref=dsts[u].at[pi if wait else me],
                send_sem=send_sems.at[u * 7 + k - 1], recv_sem=recv_sems.at[u * 7 + k - 1],
                device_id=p, device_id_type=pl.DeviceIdType.MESH)

        def local(u):
            return pltpu.make_async_copy(ins[u].at[me], dsts[u].at[me], local_sems.at[u])

        return copy, local

    def start(ins, dsts, sems):
        copy, local = tools(ins, dsts, sems)
        for u in range(n_u):
            if into is None:
                local(u).start()
            for k in peers:
                copy(u, k).start()

    def finish(ins, dsts, sems):
        copy, local = tools(ins, dsts, sems)
        for u in range(n_u):
            for k in peers:
                copy(u, k, wait=True).wait()
            if into is None:
                local(u).wait()

    sems = [pltpu.SemaphoreType.DMA((7 * n_u,)), pltpu.SemaphoreType.DMA((7 * n_u,)), pltpu.SemaphoreType.DMA((n_u,))]
    aliases = {} if into is None else {n_u + u: u for u in range(n_u)}
    return _Comm(list(parts) + list(into or []), [SDS(p.shape, p.dtype) for p in parts], sems, start, finish, aliases)


def _join_comm(a, b):
    n_i, n_o, n_s = len(a.ins), len(a.out_shape), len(a.sems)

    def both(stage):
        def run(ins, outs, sems):
            for comm, part in ((a, (ins[:n_i], outs[:n_o], sems[:n_s])), (b, (ins[n_i:], outs[n_o:], sems[n_s:]))):
                if getattr(comm, stage) is not None:
                    getattr(comm, stage)(*part)
        return run

    aliases = {**a.aliases, **{n_i + ci: n_o + co for ci, co in b.aliases.items()}}
    return _Comm(a.ins + b.ins, a.out_shape + b.out_shape, a.sems + b.sems, both("start"), both("finish"), aliases,
                 both("middle"))


def _pcall(body, name, grid, in_specs, out_specs, out_shape, scratch, sem, args, comm=None, aliases=None):
    n_i, n_o, n_s = len(in_specs), len(out_specs), len(scratch)
    aliases = aliases or {}
    if comm is None:
        res = pl.pallas_call(
            body, name=name, grid=grid, in_specs=in_specs, out_specs=out_specs, out_shape=out_shape,
            scratch_shapes=scratch, input_output_aliases=aliases,
            compiler_params=pltpu.CompilerParams(dimension_semantics=sem, vmem_limit_bytes=V7X_VMEM_LIMIT),
        )(*args)
        return res, []
    n_ci, n_co = len(comm.ins), len(comm.out_shape)

    def wrapped(*refs):
        ins = refs[:n_i]
        cins = refs[n_i:n_i + n_ci]
        outs = refs[n_i + n_ci:n_i + n_ci + n_o]
        couts = refs[n_i + n_ci + n_o:n_i + n_ci + n_o + n_co]
        rest = refs[n_i + n_ci + n_o + n_co:]
        step = 0
        for d, g in enumerate(grid):
            step = step * g + pl.program_id(d)
        n_steps = math.prod(grid)
        mid = (n_steps * 5) // 8
        staged = comm.middle is not None and 0 < mid < n_steps - 1

        @pl.when(step == 0)
        def _():
            comm.start(cins, couts, rest[n_s:])

        if staged:
            @pl.when(step == mid)
            def _():
                comm.middle(cins, couts, rest[n_s:])

        body(*ins, *outs, *rest[:n_s])

        @pl.when(step == n_steps - 1)
        def _():
            if comm.middle is not None and not staged:
                comm.middle(cins, couts, rest[n_s:])
            comm.finish(cins, couts, rest[n_s:])

    res = pl.pallas_call(
        wrapped, name=name, grid=grid, in_specs=list(in_specs) + [ANY] * n_ci,
        out_specs=list(out_specs) + [ANY] * n_co, out_shape=list(out_shape) + list(comm.out_shape),
        scratch_shapes=list(scratch) + list(comm.sems),
        input_output_aliases={**aliases, **{n_i + ci: n_o + co for ci, co in comm.aliases.items()}},
        compiler_params=pltpu.CompilerParams(dimension_semantics=("arbitrary",) * len(grid),
                                             vmem_limit_bytes=V7X_VMEM_LIMIT),
    )(*args, *comm.ins)
    return res[:n_o], res[n_o:]


def _rmsnorm_fwd(x, g, name, comm=None):
    S, D = x.shape
    tm = _tile(S, ROWS_M, BF16_ROWS)

    def body(x_ref, g_ref, h_ref):
        h_ref[...] = _rmsnorm_rows(x_ref[...], g_ref[...])

    (h,), extra = _pcall(body, name, (S // tm,), [BS((tm, D), lambda i: (i, 0)), BS((1, D), lambda i: (0, 0))],
                         [BS((tm, D), lambda i: (i, 0))], [SDS((S, D), BF16)], [], ("parallel",), (x, g), comm)
    return h, extra


def _matmul_fwd(a, w, name, comm=None, w_t=False, out_dtype=F32):
    S, K = a.shape
    C = w.shape[0]
    Fc = w.shape[1] if w_t else w.shape[2]
    tn = _tile(Fc, WIDE)
    tm = _tile(S, 2 * ROWS_L, BF16_ROWS)

    def body(a_ref, w_ref, o_ref):
        p = _dot_nt(a_ref[...], w_ref[0]) if w_t else _dot(a_ref[...], w_ref[0])
        o_ref[0] = p.astype(out_dtype)

    w_spec = BS((1, tn, K), lambda c, n, i: (c, n, 0)) if w_t else BS((1, K, tn), lambda c, n, i: (c, 0, n))
    (o,), extra = _pcall(
        body, name, (C, Fc // tn, S // tm), [BS((tm, K), lambda c, n, i: (i, 0)), w_spec],
        [BS((1, tm, tn), lambda c, n, i: (c, i, n))], [SDS((C, S, Fc), out_dtype)], [],
        ("parallel", "parallel", "parallel"), (a, w), comm)
    return o, extra


def _rmsnorm_rows(xv, g):
    r = lax.rsqrt(jnp.mean(xv * xv, axis=-1, keepdims=True) + EPS)
    return (xv * r * g).astype(BF16)


def _swiglu_down(ab, w2, x, g_next, name, comm=None):
    _, S, F = ab.shape
    D = w2.shape[1]
    tk = _tile(F, WIDE)
    tm = _tile(S, ROWS_M, BF16_ROWS)
    nk = F // tk

    te = _tile(tm, ELEMENTWISE_ROWS, BF16_ROWS)

    def body(ab_ref, w_ref, x_ref, g_ref, o_ref, h_ref, s_ref):
        k = pl.program_id(1)
        for r0 in range(0, tm, te):
            rows = slice(r0, r0 + te)
            a = ab_ref[0, rows, :].astype(F32)
            s_ref[rows, :] = (a * _sig(a) * ab_ref[1, rows, :].astype(F32)).astype(BF16)
        p = 0.5 * _dot(s_ref[...], w_ref[...])

        @pl.when(k == 0)
        def _():
            o_ref[...] = x_ref[...] + p

        @pl.when(k > 0)
        def _():
            o_ref[...] += p

        @pl.when(k == nk - 1)
        def _():
            h_ref[...] = _rmsnorm_rows(o_ref[...], g_ref[...])

    (o, h), extra = _pcall(
        body, name, (S // tm, nk),
        [BS((2, tm, tk), lambda i, k: (0, i, k)), BS((tk, D), lambda i, k: (k, 0)), BS((tm, D), lambda i, k: (i, 0)),
         BS((1, D), lambda i, k: (0, 0))],
        [BS((tm, D), lambda i, k: (i, 0)), BS((tm, D), lambda i, k: (i, 0))],
        [SDS((S, D), F32), SDS((S, D), BF16)], [pltpu.VMEM((tm, tk), BF16)], ("parallel", "arbitrary"),
        (ab, w2, x, g_next), comm)
    return o, h, extra


def _ffn_bwd_hidden(dy, w2, ab, name, comm=None):
    S, D = dy.shape
    F = w2.shape[0]
    tk = _tile(F, WIDE)
    tm = _tile(S, ROWS_M, BF16_ROWS)
    te = _tile(tm, ELEMENTWISE_ROWS, BF16_ROWS)

    def body(dy_ref, w_ref, ab_ref, dab_ref, s_ref, ds_ref):
        ds_ref[...] = 0.5 * _dot_nt(dy_ref[...].astype(BF16), w_ref[...])
        for r0 in range(0, tm, te):
            rows = slice(r0, r0 + te)
            ds = ds_ref[rows, :]
            a = ab_ref[0, rows, :].astype(F32)
            b = ab_ref[1, rows, :].astype(F32)
            sg = _sig(a)
            sa = a * sg
            dab_ref[0, rows, :] = (ds * b * (sg * (1.0 + a * (1.0 - sg)))).astype(BF16)
            dab_ref[1, rows, :] = (ds * sa).astype(BF16)
            s_ref[rows, :] = (0.5 * (sa * b)).astype(BF16)

    (dab, sh), extra = _pcall(
        body, name, (F // tk, S // tm),
        [BS((tm, D), lambda k, i: (i, 0)), BS((tk, D), lambda k, i: (k, 0)), BS((2, tm, tk), lambda k, i: (0, i, k))],
        [BS((2, tm, tk), lambda k, i: (0, i, k)), BS((tm, tk), lambda k, i: (i, k))],
        [SDS((2, S, F), BF16), SDS((S, F), BF16)], [pltpu.VMEM((tm, tk), F32)], ("parallel", "parallel"),
        (dy, w2, ab), comm)
    return dab, sh, extra


def _matmul_tn(a, b, n_c, name, comm=None, b_shared=False):
    G, S, M = a.shape
    _, _, Fc = b.shape
    C = n_c
    tM = _tile(M, WIDE)
    tn = _tile(Fc, WIDE)
    ts = _tile(S, 2 * ROWS_L if b.dtype == BF16 else ROWS_L, BF16_ROWS)
    n_s = S // ts

    def body(a_ref, b_ref, o_ref, acc):
        s = pl.program_id(4)
        p = _dot_tn(a_ref[0].astype(BF16), b_ref[0].astype(BF16))

        @pl.when(s == 0)
        def _():
            acc[...] = p

        @pl.when(s > 0)
        def _():
            acc[...] += p

        @pl.when(s == n_s - 1)
        def _():
            o_ref[0] = acc[...].astype(BF16)

    (o,), extra = _pcall(
        body, name, (G, M // tM, C, Fc // tn, n_s),
        [BS((1, ts, tM), lambda g, m, c, n, s: (g, s, m)), BS((1, ts, tn), lambda g, m, c, n, s: (c if b_shared else g * C + c, s, n))],
        [BS((1, tM, tn), lambda g, m, c, n, s: (g * C + c, m, n))], [SDS((G * C, M, Fc), BF16)],
        [pltpu.VMEM((tM, tn), F32)], ("parallel", "parallel", "parallel", "parallel", "arbitrary"), (a, b), comm)
    return o, extra


def _matmul_nt_normbwd(b, w, x, gam, dres, name, comm=None, w_t=False):
    C, S, Fc = b.shape
    D = w.shape[2] if w_t else w.shape[1]
    tk = _tile(Fc, WIDE)
    tm = _tile(S, ROWS_L, BF16_ROWS)
    te = _tile(tm, 256, 8)
    nk = Fc // tk

    def body(b_ref, w_ref, x_ref, g_ref, r_ref, dx_ref, dg_ref):
        i, c, k = pl.program_id(0), pl.program_id(1), pl.program_id(2)
        p = _dot(b_ref[0], w_ref[0]) if w_t else _dot_nt(b_ref[0], w_ref[0])
        first = jnp.logical_and(c == 0, k == 0)

        @pl.when(first)
        def _():
            dx_ref[...] = p

        @pl.when(jnp.logical_not(first))
        def _():
            dx_ref[...] += p

        @pl.when(jnp.logical_and(c == C - 1, k == nk - 1))
        def _():
            dgp = None
            for r0 in range(0, tm, te):
                rows = slice(r0, r0 + te)
                xv = x_ref[rows, :]
                r = lax.rsqrt(jnp.mean(xv * xv, axis=-1, keepdims=True) + EPS)
                xn = xv * r
                dh = dx_ref[rows, :]
                dxn = dh * g_ref[...]
                dx_ref[rows, :] = r_ref[rows, :] + r * (dxn - xn * jnp.mean(dxn * xn, axis=-1, keepdims=True))
                t = jnp.sum(dh * xn, axis=0, keepdims=True)
                dgp = t if dgp is None else dgp + t

            @pl.when(i == 0)
            def _():
                dg_ref[...] = dgp

            @pl.when(i > 0)
            def _():
                dg_ref[...] += dgp

    once = dict(pipeline_mode=pl.Buffered(1))
    (dx, dg), extra = _pcall(
        body, name, (S // tm, C, nk),
        [BS((1, tm, tk), lambda i, c, k: (c, i, k)),
         BS((1, tk, D), lambda i, c, k: (c, k, 0)) if w_t else BS((1, D, tk), lambda i, c, k: (c, 0, k)),
         BS((tm, D), lambda i, c, k: (i, 0), **once), BS((1, D), lambda i, c, k: (0, 0)),
         BS((tm, D), lambda i, c, k: (i, 0), **once)],
        [BS((tm, D), lambda i, c, k: (i, 0)), BS((1, D), lambda i, c, k: (0, 0))],
        [SDS((S, D), F32), SDS((1, D), F32)], [],
        ("arbitrary", "arbitrary", "arbitrary"), (b, w, x, gam, dres), comm)
    return dx, dg, extra


def _final_loss(x, gam, target, name):
    S, D = x.shape
    tm = _tile(S, 512, 8)

    def body(x_ref, g_ref, t_ref, loss_ref, dx_ref, dg_ref):
        i = pl.program_id(0)
        xv = x_ref[...]
        r = lax.rsqrt(jnp.mean(xv * xv, axis=-1, keepdims=True) + EPS)
        xn = xv * r
        err = xn * g_ref[...] - t_ref[...]
        part = 0.5 * jnp.sum(jnp.mean(err * err, axis=-1, keepdims=True), axis=0, keepdims=True)
        dy = err * (1.0 / D)
        dxn = dy * g_ref[...]
        dx_ref[...] = r * (dxn - xn * jnp.mean(dxn * xn, axis=-1, keepdims=True))
        dgp = jnp.sum(dy * xn, axis=0, keepdims=True)
        lp = jnp.broadcast_to(part, loss_ref.shape)

        @pl.when(i == 0)
        def _():
            dg_ref[...] = dgp
            loss_ref[...] = lp

        @pl.when(i > 0)
        def _():
            dg_ref[...] += dgp
            loss_ref[...] += lp

    res, _ = _pcall(
        body, name, (S // tm,),
        [BS((tm, D), lambda i: (i, 0)), BS((1, D), lambda i: (0, 0)), BS((tm, D), lambda i: (i, 0))],
        [BS((8, 128), lambda i: (0, 0)), BS((tm, D), lambda i: (i, 0)), BS((1, D), lambda i: (0, 0))],
        [SDS((8, 128), F32), SDS((S, D), F32), SDS((1, D), F32)], [], ("arbitrary",), (x, gam, target))
    return res


CONV_CHUNK = 32


def _fill_shifted(rot, n):
    for b in range(1, 8):
        rot[b, 0:n - 8, :] = rot[0, b:b + n - 8, :]


def _window(rot, off, r0, rows):
    b = off % 8
    return rot[b, off - b + r0:off - b + r0 + rows, :]


def _taps(rot, w_ref, offs, n_rows, out):
    for r0 in range(0, n_rows, CONV_CHUNK):
        acc = None
        for k, off in enumerate(offs):
            t = w_ref[k:k + 1, :] * _window(rot, off, r0, CONV_CHUNK)
            acc = t if acc is None else acc + t
        out[r0:r0 + CONV_CHUNK, :] = acc


def _tap_grads(rot, offs, g_plane, n_rows, dw_ref):
    for k, off in enumerate(offs):
        acc = None
        for r0 in range(0, n_rows, CONV_CHUNK):
            p = g_plane[0, r0:r0 + CONV_CHUNK, :] * _window(rot, off, r0, CONV_CHUNK)
            acc = p if acc is None else acc + p
        dw_ref[k:k + 1, :] += jnp.sum(acc, axis=0, keepdims=True)


def _sgu_masks():
    ii = lax.broadcasted_iota(jnp.int32, (SGU_BLOCK, SGU_BLOCK), 0) // SGU_CHUNK
    jj = lax.broadcasted_iota(jnp.int32, (SGU_BLOCK, SGU_BLOCK), 1) // SGU_CHUNK
    return jj <= ii, ii <= jj


def _mixers_fwd(proj, pool_w, pool_scale, sconv_w, cconv_w, cln_g, cln_b, sln_g, sln_b, sgu_w, sgu_bias, name,
                comm=None):
    _, S, D = proj.shape
    BW = D // 2
    GW = BW // 4
    TS = _tile(S, ROWS_S, SGU_BLOCK)
    H = HALO
    hb = TS // H

    def main(blk, col):
        return BS((1, TS, BW), lambda i: (blk, i, col))

    def back(blk, col):
        return BS((1, H, BW), lambda i: (blk, jnp.maximum(i * hb - 1, 0), col))

    def full(a):
        nd = a.ndim
        return BS(a.shape, lambda i: (0,) * nd)

    def body(pa_m, pa_b, xi_m, xi_b, bg_m, cg_m, cg_b, ca_m, ca_b, cb_m, cb_b, du_m, dv_m,
             pw, ps, sw, cw, clg, clb, slg, slb, gw, gbias, y_ref, y1_ref, e1, e2, e3):
        i = pl.program_id(0)
        nb = jnp.where(i > 0, 1.0, 0.0).astype(F32)
        rows = i * TS + lax.broadcasted_iota(jnp.int32, (TS, 1), 0)

        e1[0:H, :] = pa_b[0] * nb
        e1[H:H + TS, :] = pa_m[0]
        for g in range(4):
            cols = slice(g * GW, (g + 1) * GW)
            win = 2 << g
            wsum = e1[H:H + TS, cols]
            for k in range(1, win):
                wsum = wsum + e1[H - k:H - k + TS, cols]
            cnt = jnp.minimum(rows + 1, win).astype(F32)
            d = wsum / cnt - e1[H:H + TS, cols]
            yg = _dot(d.astype(BF16), pw[g].astype(BF16)) * ps[:, cols]
            y_ref[0, :, cols] = yg.astype(BF16)

        e2[0:H, :] = cg_b[0] * xi_b[0] * nb
        e2[H:H + TS, :] = cg_m[0] * xi_m[0]
        cz = sw[0:1, :] * e2[H - 2:H - 2 + TS, :]
        for k in range(1, SCONV_K):
            cz = cz + sw[k:k + 1, :] * e2[H - 2 + k:H - 2 + k + TS, :]
        y_ref[1] = (bg_m[0] * cz).astype(BF16)

        e3[0, 0:H, :] = ca_b[0] * _sig(cb_b[0]) * nb
        e3[0, H:H + TS, :] = ca_m[0] * _sig(cb_m[0])
        _fill_shifted(e3, H + TS)
        _taps(e3, cw, [H - (CCONV_K - 1) + k for k in range(CCONV_K)], TS, y1_ref)
        yh, _ = _ln_stats(y1_ref[...])
        y2 = yh * clg[...] + clb[...]
        y_ref[2] = (y2 * _sig(y2)).astype(BF16)

        u, _ = _gelu(du_m[0])
        v, _ = _gelu(dv_m[0])
        vh, _ = _ln_stats(v)
        vn = vh * slg[...] + slb[...]
        mask, _ = _sgu_masks()
        for h in range(4):
            wm = jnp.where(mask, gw[h], 0.0).astype(BF16)
            cs = slice(h * GW, (h + 1) * GW)
            for n in range(TS // SGU_BLOCK):
                rs = slice(n * SGU_BLOCK, (n + 1) * SGU_BLOCK)
                z = _dot(wm, vn[rs, cs].astype(BF16)) + gbias[h]
                y_ref[3, rs, cs] = (u[rs, cs] * z).astype(BF16)

    args = [proj] * 13 + [pool_w, pool_scale, sconv_w, cconv_w, cln_g, cln_b, sln_g, sln_b, sgu_w, sgu_bias]
    in_specs = [main(0, 0), back(0, 0), main(0, 1), back(0, 1), main(1, 0), main(1, 1), back(1, 1),
                main(2, 0), back(2, 0), main(2, 1), back(2, 1), main(3, 0), main(3, 1)]
    in_specs += [full(a) for a in args[13:]]
    (y, y1), extra = _pcall(body, name, (S // TS,), in_specs,
                            [BS((4, TS, BW), lambda i: (0, i, 0)), BS((TS, BW), lambda i: (i, 0))],
                            [SDS((4, S, BW), BF16), SDS((S, BW), F32)],
                            [pltpu.VMEM((H + TS, BW), F32)] * 2 + [pltpu.VMEM((8, H + TS, BW), F32)], ("parallel",),
                            args, comm)
    return y, y1, extra


def _mixers_bwd(proj, y1, dy, dproj_gates, pool_w, pool_wt, pool_scale, sconv_w, cconv_w, cln_g, cln_b, sln_g, sln_b,
                sgu_w, sgu_wt, sgu_bias, name, comm=None):
    _, S, D = proj.shape
    BW = D // 2
    GW = BW // 4
    TS = _tile(S, ROWS_S, SGU_BLOCK)
    H = HALO
    hb = TS // H
    n_t = S // TS
    E = TS + H

    def main(blk, col):
        return BS((1, TS, BW), lambda i: (blk, i, col))

    def back(blk, col):
        return BS((1, H, BW), lambda i: (blk, jnp.maximum(i * hb - 1, 0), col))

    def front(blk, col):
        return BS((1, H, BW), lambda i: (blk, jnp.minimum((i + 1) * hb, S // H - 1), col))

    def full(a):
        nd = a.ndim
        return BS(a.shape, lambda i: (0,) * nd)

    def body(pa_b, pa_m, xi_b, xi_m, bg_m, bg_f, cg_b, cg_m, ca_b, ca_m, cb_b, cb_m, du_m, dv_m, y1_m, y1_f,
             dya_m, dya_f, dyb_m, dyb_f, dyc_m, dyc_f, dyd_m,
             pw, pwt, ps, sw, cw, clg, clb, slg, slb, gw, gwt, gbias, _gates_in,
             dp_ref, dpw, dps, dsw, dcw, dclg, dclb, dslg, dslb, dgw, dgb,
             e1, e2, e3, e4, e5, ra, rb):
        i = pl.program_id(0)
        nb = jnp.where(i > 0, 1.0, 0.0).astype(F32)
        nf = jnp.where(i < n_t - 1, 1.0, 0.0).astype(F32)
        rows_m = i * TS + lax.broadcasted_iota(jnp.int32, (TS, 1), 0)
        rows_e = i * TS + lax.broadcasted_iota(jnp.int32, (E, 1), 0)

        @pl.when(i == 0)
        def _():
            for r in (dpw, dps, dsw, dcw, dclg, dclb, dslg, dslb, dgw, dgb):
                r[...] = jnp.zeros(r.shape, F32)

        e1[0:H, :] = pa_b[0] * nb
        e1[H:H + TS, :] = pa_m[0]
        e2[0:TS, :] = dya_m[0] * ps[...]
        e2[TS:E, :] = dya_f[0] * ps[...] * nf
        for g in range(4):
            cols = slice(g * GW, (g + 1) * GW)
            win = 2 << g
            a_m = e1[H:H + TS, cols]
            wsum = a_m
            for k in range(1, win):
                wsum = wsum + e1[H - k:H - k + TS, cols]
            d = wsum / jnp.minimum(rows_m + 1, win).astype(F32) - a_m
            d16 = d.astype(BF16)
            dyp = e2[0:E, cols].astype(BF16)
            dd = _dot(dyp, pwt[g].astype(BF16))
            e3[0:E, cols] = dd / jnp.minimum(rows_e + 1, win).astype(F32)
            da = e3[0:TS, cols] - dd[0:TS]
            for k in range(1, win):
                da = da + e3[k:k + TS, cols]
            dp_ref[0, :, cols] = da.astype(BF16)
            ypre = _dot(d16, pw[g].astype(BF16))
            dps[:, cols] += jnp.sum(dya_m[0][:, cols] * ypre, axis=0, keepdims=True)
            dpw[g] += _dot(jnp.transpose(d).astype(BF16), dyp[0:TS])

        e4[0:H, :] = cg_b[0] * xi_b[0] * nb
        e4[H:H + TS, :] = cg_m[0] * xi_m[0]
        dyb = dyb_m[0]
        e5[0:TS, :] = dyb * bg_m[0]
        e5[TS:E, :] = dyb_f[0] * bg_f[0] * nf
        dcz = e5[0:TS, :]
        cz = None
        dz = None
        for k in range(SCONV_K):
            zk = e4[H - 2 + k:H - 2 + k + TS, :]
            wk = sw[k:k + 1, :]
            cz = wk * zk if cz is None else cz + wk * zk
            t = wk * e5[2 - k:2 - k + TS, :]
            dz = t if dz is None else dz + t
            dsw[k:k + 1, :] += jnp.sum(dcz * zk, axis=0, keepdims=True)
        dp_ref[0, :, BW:2 * BW] = (dz * cg_m[0]).astype(BF16)
        dp_ref[1, :, 0:BW] = (dyb * cz).astype(BF16)
        dp_ref[1, :, BW:2 * BW] = (dz * xi_m[0]).astype(BF16)

        sgm = _sig(cb_m[0])
        ra[0, 0:H, :] = ca_b[0] * _sig(cb_b[0]) * nb
        ra[0, H:H + TS, :] = ca_m[0] * sgm
        _fill_shifted(ra, H + TS)
        fwd_offs = [H - (CCONV_K - 1) + k for k in range(CCONV_K)]
        e4[0:TS, :] = y1_m[...]
        e4[TS:E, :] = y1_f[...]
        yh, rstd = _ln_stats(e4[0:E, :])
        y2 = yh * clg[...] + clb[...]
        s2 = _sig(y2)
        e1[0:TS, :] = dyc_m[0]
        e1[TS:E, :] = dyc_f[0] * nf
        dy2 = e1[0:E, :] * (s2 * (1.0 + y2 * (1.0 - s2)))
        dclg[...] += jnp.sum((dy2 * yh)[0:TS], axis=0, keepdims=True)
        dclb[...] += jnp.sum(dy2[0:TS], axis=0, keepdims=True)
        rb[0, 0:E, :] = _ln_bwd(dy2 * clg[...], yh, rstd)
        _fill_shifted(rb, E)
        _taps(rb, cw, [CCONV_K - 1 - k for k in range(CCONV_K)], TS, e5)
        _tap_grads(ra, fwd_offs, rb, TS, dcw)
        dy0 = e5[0:TS, :]
        dp_ref[2, :, 0:BW] = (dy0 * sgm).astype(BF16)
        dp_ref[2, :, BW:2 * BW] = (dy0 * ca_m[0] * (sgm * (1.0 - sgm))).astype(BF16)

        pu = du_m[0]
        pv = dv_m[0]
        u, tu = _gelu(pu)
        v, tv = _gelu(pv)
        vh, vr = _ln_stats(v)
        vn = vh * slg[...] + slb[...]
        dyd = dyd_m[0]
        mask, mask_t = _sgu_masks()
        for h in range(4):
            wm = jnp.where(mask, gw[h], 0.0).astype(BF16)
            wmt = jnp.where(mask_t, gwt[h], 0.0).astype(BF16)
            cs = slice(h * GW, (h + 1) * GW)
            for n in range(TS // SGU_BLOCK):
                rs = slice(n * SGU_BLOCK, (n + 1) * SGU_BLOCK)
                vb = vn[rs, cs].astype(BF16)
                z = _dot(wm, vb) + gbias[h]
                dzb = dyd[rs, cs] * u[rs, cs]
                dz16 = dzb.astype(BF16)
                e3[rs, cs] = dyd[rs, cs] * z
                e4[rs, cs] = _dot(wmt, dz16)
                dgw[h] += jnp.where(mask, _dot_nt(dz16, vb), 0.0)
                dgb[h] += dzb
        dvn = e4[0:TS, :]
        dslg[...] += jnp.sum(dvn * vh, axis=0, keepdims=True)
        dslb[...] += jnp.sum(dvn, axis=0, keepdims=True)
        dv = _ln_bwd(dvn * slg[...], vh, vr)
        dp_ref[3, :, 0:BW] = (e3[0:TS, :] * _gelu_grad(pu, tu)).astype(BF16)
        dp_ref[3, :, BW:2 * BW] = (dv * _gelu_grad(pv, tv)).astype(BF16)

        @pl.when(i == n_t - 1)
        def _():
            for h in range(4):
                dgb[h] = jnp.broadcast_to(jnp.sum(dgb[h], axis=1, keepdims=True), dgb.shape[1:])

    params = [pool_w, pool_wt, pool_scale, sconv_w, cconv_w, cln_g, cln_b, sln_g, sln_b, sgu_w, sgu_wt, sgu_bias]
    args = [proj] * 14 + [y1] * 2 + [dy] * 7 + params + [dproj_gates]
    in_specs = [back(0, 0), main(0, 0), back(0, 1), main(0, 1), main(1, 0), front(1, 0), back(1, 1), main(1, 1),
                back(2, 0), main(2, 0), back(2, 1), main(2, 1), main(3, 0), main(3, 1),
                BS((TS, BW), lambda i: (i, 0)), BS((H, BW), lambda i: (jnp.minimum((i + 1) * hb, S // H - 1), 0)),
                main(0, 0), front(0, 0), main(1, 0), front(1, 0), main(2, 0), front(2, 0), main(3, 0)]
    in_specs += [full(a) for a in params] + [ANY]
    small = [SDS(pool_w.shape, F32), SDS(pool_scale.shape, F32), SDS(sconv_w.shape, F32), SDS(cconv_w.shape, F32),
             SDS(cln_g.shape, F32), SDS(cln_b.shape, F32), SDS(sln_g.shape, F32), SDS(sln_b.shape, F32),
             SDS(sgu_w.shape, F32), SDS(sgu_bias.shape, F32)]
    out_specs = [BS((4, TS, D), lambda i: (0, i, 0))] + [full(s) for s in small]
    return _pcall(body, name, (n_t,), in_specs, out_specs, [SDS(dproj_gates.shape, BF16)] + small,
                  [pltpu.VMEM((TS + 2 * H, BW), F32)] * 5 + [pltpu.VMEM((8, TS + 2 * H, BW), F32)] * 2,
                  ("arbitrary",), args, comm, aliases={len(args) - 1: 0})


def _merge_fwd(y, proj, w_up, w_out, x, g_next, name, comm=None):
    _, S, BW = y.shape
    D = x.shape[1]
    tm = _tile(S, ROWS_M, BF16_ROWS)

    def body(y_ref, pg_ref, wu_ref, wo_ref, x_ref, g_ref, o_ref, m_ref, h_ref):
        merged = None
        for g in range(4):
            t = _sig(pg_ref[g]) * _dot(y_ref[g], wu_ref[g])
            merged = t if merged is None else merged + t
        m16 = merged.astype(BF16)
        m_ref[...] = m16
        xn = x_ref[...] + _dot(m16, wo_ref[...])
        o_ref[...] = xn
        h_ref[...] = _rmsnorm_rows(xn, g_ref[...])

    once = dict(pipeline_mode=pl.Buffered(1))
    (o, m, h), extra = _pcall(
        body, name, (S // tm,),
        [BS((4, tm, BW), lambda i: (0, i, 0)), BS((4, tm, D), lambda i: (1, i, 0)),
         BS((4, BW, D), lambda i: (0, 0, 0), **once), BS((D, D), lambda i: (0, 0), **once),
         BS((tm, D), lambda i: (i, 0)), BS((1, D), lambda i: (0, 0))],
        [BS((tm, D), lambda i: (i, 0)), BS((tm, D), lambda i: (i, 0)), BS((tm, D), lambda i: (i, 0))],
        [SDS((S, D), F32), SDS((S, D), BF16), SDS((S, D), BF16)], [], ("parallel",),
        (y, proj, w_up, w_out, x, g_next), comm)
    return o, m, h, extra


def _merge_bwd(dx, y, proj, w_up, w_out, name, comm=None):
    _, S, BW = y.shape
    D = dx.shape[1]
    tm = _tile(S, ROWS_S, BF16_ROWS)

    def body(dx_ref, y_ref, pg_ref, wu_ref, wo_ref, dup_ref, dp_ref, dy_ref):
        dm = _dot_nt(dx_ref[...].astype(BF16), wo_ref[...])
        for g in range(4):
            gate = _sig(pg_ref[g])
            up = _dot(y_ref[g], wu_ref[g])
            dup = (dm * gate).astype(BF16)
            dup_ref[g] = dup
            dp_ref[g] = (dm * up * (gate * (1.0 - gate))).astype(BF16)
            dy_ref[g] = _dot_nt(dup, wu_ref[g])

    res, extra = _pcall(
        body, name, (S // tm,),
        [BS((tm, D), lambda i: (i, 0)), BS((4, tm, BW), lambda i: (0, i, 0)), BS((4, tm, D), lambda i: (1, i, 0)),
         BS((4, BW, D), lambda i: (0, 0, 0)), BS((D, D), lambda i: (0, 0))],
        [BS((4, tm, D), lambda i: (0, i, 0)), BS((4, tm, D), lambda i: (1, i, 0)), BS((4, tm, BW), lambda i: (0, i, 0))],
        [SDS((4, S, D), BF16), SDS((8, S, D), BF16), SDS((4, S, BW), F32)], [], ("parallel",),
        (dx, y, proj, w_up, w_out), comm)
    return res, extra


def _adamw(w, g, m, v):
    m = ADAM_B1 * m + (1.0 - ADAM_B1) * g
    v = ADAM_B2 * v + (1.0 - ADAM_B2) * (g * g)
    m_hat = m / (1.0 - ADAM_B1 ** ADAM_STEP)
    v_hat = v / (1.0 - ADAM_B2 ** ADAM_STEP)
    delta = -ADAM_LR * (m_hat / (jnp.sqrt(v_hat) + ADAM_EPS) + ADAM_WD * w)
    return delta, m, v


def _adamw_sharded(parts, w, m, v, name, comm=None):
    L, R, C = w.shape
    tr = _tile(R, ROWS_S, BF16_ROWS)

    def body(*refs):
        p_refs = refs[:L]
        w_ref, m_ref, v_ref, g_out, d_out, m_out, v_out = refs[L:]
        l = pl.program_id(0)
        g = None
        for d in range(N_DEV):
            t = p_refs[0][d].astype(F32)
            for j in range(1, L):
                t = jnp.where(l == j, p_refs[j][d].astype(F32), t)
            g = t if g is None else g + t
        dl, mn, vn = _adamw(w_ref[0], g, m_ref[0], v_ref[0])
        g_out[0] = g
        d_out[0] = dl
        m_out[0] = mn
        v_out[0] = vn

    def part_spec(j):
        return BS((N_DEV, tr, C), lambda l, r: (0, jnp.where(l == j, r, 0), 0))

    blk = BS((1, tr, C), lambda l, r: (l, r, 0))
    return _pcall(body, name, (L, R // tr), [part_spec(j) for j in range(L)] + [blk, blk, blk], [blk] * 4,
                  [SDS((L, R, C), F32)] * 4, [], ("parallel", "parallel"), (*parts, w, m, v), comm)


def _adamw_replicated(gathered, layout, wmv, name):
    n_b = len(gathered)
    n_p = len(layout)

    def body(*refs):
        bufs = refs[:n_b]
        prm = refs[n_b:n_b + 3 * n_p]
        outs = refs[n_b + 3 * n_p:n_b + 7 * n_p]
        sums = refs[n_b + 7 * n_p:]
        for b in range(n_b):
            s = bufs[b][0]
            for d in range(1, N_DEV):
                s = s + bufs[b][d]
            sums[b][...] = s
        for p, (b, r0, nr) in enumerate(layout):
            g = sums[b][r0:r0 + nr, :]
            d, mn, vn = _adamw(prm[3 * p][...], g, prm[3 * p + 1][...], prm[3 * p + 2][...])
            outs[4 * p][...] = g
            outs[4 * p + 1][...] = d
            outs[4 * p + 2][...] = mn
            outs[4 * p + 3][...] = vn

    flat = [a for t in wmv for a in t]
    out_shape = []
    for (w, _, _) in wmv:
        out_shape += [SDS(w.shape, F32)] * 4
    out_shape += [SDS(g.shape[1:], F32) for g in gathered]
    return pl.pallas_call(
        body, name=name, out_shape=out_shape,
        compiler_params=pltpu.CompilerParams(vmem_limit_bytes=V7X_VMEM_LIMIT),
    )(*gathered, *flat)


def _adamw_small(g, w, m, v, name):
    def body(g_ref, w_ref, m_ref, v_ref, d_out, m_out, v_out):
        d, mn, vn = _adamw(w_ref[...], g_ref[...], m_ref[...], v_ref[...])
        d_out[...] = d
        m_out[...] = mn
        v_out[...] = vn

    return pl.pallas_call(body, name=name, out_shape=[SDS(w.shape, F32)] * 3)(g, w, m, v)


def _pad_rows(a, rows):
    return jnp.pad(a, ((0, rows - a.shape[0]), (0, 0)))


def kernel(x, ffn1_norm, ffn1_w13, ffn1_w2, mix_norm, w_in, pool_w, pool_scale, sconv_w, cconv_w, cconv_ln_g, cconv_ln_b, sgu_ln_g, sgu_ln_b, sgu_w, sgu_b, w_up, w_out, ffn2_norm, ffn2_w13, ffn2_w2, final_norm, loss_target, m_ffn1_norm, m_ffn1_w13, m_ffn1_w2, m_mix_norm, m_w_in, m_pool_w, m_pool_scale, m_sconv_w, m_cconv_w, m_cconv_ln_g, m_cconv_ln_b, m_sgu_ln_g, m_sgu_ln_b, m_sgu_w, m_sgu_b, m_w_up, m_w_out, m_ffn2_norm, m_ffn2_w13, m_ffn2_w2, m_final_norm, v_ffn1_norm, v_ffn1_w13, v_ffn1_w2, v_mix_norm, v_w_in, v_pool_w, v_pool_scale, v_sconv_w, v_cconv_w, v_cconv_ln_g, v_cconv_ln_b, v_sgu_ln_g, v_sgu_ln_b, v_sgu_w, v_sgu_b, v_w_up, v_w_out, v_ffn2_norm, v_ffn2_w13, v_ffn2_w2, v_final_norm):
    P = dict(locals())
    L = ffn1_norm.shape[0]
    S, D = x.shape[1], x.shape[2]
    BW = D // 2
    GW = BW // 4
    F = ffn1_w2.shape[1] * N_DEV
    fs = ffn1_w13.shape[2]
    cw = sconv_w.shape[2]
    me = 4 * lax.axis_index("x") + 2 * lax.axis_index("y") + lax.axis_index("c")

    big = ["ffn1_w13", "ffn1_w2", "w_in", "w_up", "w_out", "ffn2_w13", "ffn2_w2"]
    shards = [(jnp.swapaxes(P[n], 1, 2) if n.endswith("w13") else P[n]).astype(BF16) for n in big]
    conv_local = jnp.concatenate([sconv_w, cconv_w], axis=1)

    def gather_of(units):
        return _gather_comm(shards, [(big.index(n), l) for n, l in units])

    def ready(n, g):
        if n.endswith("w13"):
            return g.reshape(2, F, D)
        if n.endswith("w2"):
            return g.reshape(F, D)
        if n == "w_up":
            return jnp.transpose(g, (1, 2, 0, 3)).reshape(4, BW, D)
        if n == "w_out":
            return g.reshape(D, D)
        return g

    W = {}

    def take(units, arrays):
        for (n, l), g in zip(units, arrays):
            W[n, l] = ready(n, g)

    first_units = [("ffn1_w13", 0)]
    plan = {("ffn1_up", 0): [("ffn1_w2", 0), ("w_up", 0), ("w_out", 0)],
            ("ffn1_down", 0): [("w_in", 0)],
            ("proj", 0): [("w_in", 1)],
            ("mixers", 0): [("ffn2_w13", 0)], ("merge", 0): [("ffn2_w2", 0)],
            ("ffn2_up", 0): [("ffn1_w13", 1)], ("ffn2_down", 0): [("ffn1_w2", 1)],
            ("ffn1_up", 1): [("w_up", 1), ("w_out", 1)],
            ("proj", 1): [("ffn2_w13", 1)], ("mixers", 1): [("ffn2_w2", 1)]}
    assert L <= 2

    def carried(key):
        units = [u for u in plan.get(key, []) if u[1] < L]
        return units, (gather_of(units) if units else None)

    first = _gather_comm(shards + [conv_local], [(big.index(n), l) for n, l in first_units] + [(len(big), None)])
    h, got = _rmsnorm_fwd(x[0], ffn1_norm[0][None, :], "first_norm_fwd", first)
    take(first_units, got[:1])
    conv_full = jnp.transpose(got[1], (1, 2, 0, 3)).reshape(L, SCONV_K + CCONV_K, N_DEV * cw)
    sconv_full = conv_full[:, :SCONV_K]
    cconv_full = conv_full[:, SCONV_K:]

    sgu_bias = jnp.broadcast_to(sgu_b[:, :, :, None], sgu_b.shape + (GW,))
    pool_wt = jnp.swapaxes(pool_w, 2, 3)
    sgu_wt = jnp.swapaxes(sgu_w, 2, 3)

    def row(a, l):
        return a[l][None, :]

    saved = []
    xc = x[0]
    for l in range(L):
        sv = {}
        for tag in ("ffn1", None, "ffn2"):
            if tag is None:
                sv["x_mix"] = xc
                units, comm = carried(("proj", l))
                proj, extra = _matmul_fwd(h, W["w_in", l], "proj_fwd", comm)
                take(units, extra)
                units, comm = carried(("mixers", l))
                y, sv["y1"], extra = _mixers_fwd(
                    proj, pool_w[l], row(pool_scale, l), sconv_full[l], cconv_full[l], row(cconv_ln_g, l),
                    row(cconv_ln_b, l), row(sgu_ln_g, l), row(sgu_ln_b, l), sgu_w[l], sgu_bias[l], "mixers_fwd", comm)
                take(units, extra)
                units, comm = carried(("merge", l))
                sv.update(h_mix=h, proj=proj, y=y)
                xc, sv["merged"], h, extra = _merge_fwd(y, proj, W["w_up", l], W["w_out", l], xc, row(ffn2_norm, l),
                                                        "merge_fwd", comm)
                take(units, extra)
            else:
                sv["x_" + tag] = xc
                units, comm = carried((tag + "_up", l))
                ab, extra = _matmul_fwd(h, W[tag + "_w13", l], "ffn_up_fwd", comm, w_t=True, out_dtype=BF16)
                take(units, extra)
                units, comm = carried((tag + "_down", l))
                sv.update({"h_" + tag: h, "ab_" + tag: ab})
                if tag == "ffn1":
                    g_next = row(mix_norm, l)
                else:
                    g_next = row(ffn1_norm, l + 1) if l + 1 < L else final_norm[None, :]
                xc, h, extra = _swiglu_down(ab, W[tag + "_w2", l], xc, g_next, "ffn_down_fwd", comm)
                take(units, extra)
        saved.append(sv)

    loss_part, dx, d_final = _final_loss(xc, final_norm[None, :], loss_target[0], "loss_head")
    loss = lax.psum(loss_part[0, 0], MESH_AXES)

    R = {}
    second = []

    def rest_of_sends():
        todo = list(second)
        second.clear()
        comm = None
        if todo:
            assert len({e[3][0] for e in todo}) == 1
            comm = _scatter_comm([e[1] for e in todo], todo[0][3][0], [e[2] for e in todo])
        return todo, comm

    def settle(todo, arrays):
        for (key, g, _, stages), a in zip(todo, arrays):
            if len(stages) > 1:
                second.append((key, g, a, stages[1:]))
            else:
                R[key] = a

    wide = ["ffn1_norm", "mix_norm", "ffn2_norm", "final_norm"]
    half = ["pool_scale", "cconv_ln_g", "cconv_ln_b", "sgu_ln_g", "sgu_ln_b"]
    narrow = ["pool_w", "sgu_w", "sgu_b"]
    small_names = wide + half + narrow
    small_g = [dict() for _ in range(L)]
    widths = []
    for n in small_names:
        if P[n].shape[-1] not in widths:
            widths.append(P[n].shape[-1])
    layout, conv_at = {}, {}

    def pack(width):
        def stack_layers(n):
            return jnp.stack([small_g[l][n] for l in range(L)], axis=0)

        parts, r0 = [], 0
        for n in small_names:
            if P[n].shape[-1] != width:
                continue
            g = d_final if n == "final_norm" else stack_layers(n).reshape(-1, width)
            layout[n] = (widths.index(width), r0, g.shape[0])
            parts.append(_pad_rows(g, -(-g.shape[0] // 8) * 8))
            r0 += parts[-1].shape[0]
        if width == N_DEV * cw:
            conv_g = jnp.concatenate([stack_layers("sconv_w"), stack_layers("cconv_w")], axis=1)
            conv_g = conv_g.reshape(L * (SCONV_K + CCONV_K), N_DEV * cw)
            conv_at.update(b=widths.index(width), r0=r0, rows=conv_g.shape[0])
            parts.append(_pad_rows(conv_g, -(-conv_g.shape[0] // 8) * 8))
        return jnp.concatenate(parts, axis=0)

    gathered_small = [None] * len(widths)
    for l in reversed(range(L)):
        sv = saved[l]
        sg = small_g[l]
        for tag in ("ffn2", None, "ffn1"):
            if tag is None:
                keys, comm = rest_of_sends()
                (dup, dproj, dy), extra = _merge_bwd(dx, sv["y"], sv["proj"], W["w_up", l], W["w_out", l],
                                                     "merge_bwd", comm)
                settle(keys, extra)
                g_out, _ = _matmul_tn(sv["merged"][None], dx[None], 1, "w_out_grad")
                g_up, _ = _matmul_tn(sv["y"], dup, 1, "w_up_grad")
                g_out = g_out.reshape(N_DEV, D // N_DEV, D)
                g_up = jnp.transpose(g_up.reshape(4, BW, N_DEV, D // N_DEV), (2, 0, 1, 3)).reshape(
                    N_DEV, 4 * BW, D // N_DEV)
                res, (R["w_out", l], R["w_up", l]) = _mixers_bwd(
                    sv["proj"], sv["y1"], dy, dproj, pool_w[l], pool_wt[l], row(pool_scale, l), sconv_full[l], cconv_full[l],
                    row(cconv_ln_g, l), row(cconv_ln_b, l), row(sgu_ln_g, l), row(sgu_ln_b, l), sgu_w[l], sgu_wt[l],
                    sgu_bias[l], "mixers_bwd", _scatter_comm([g_out, g_up]))
                dproj = res[0]
                (sg["pool_w"], sg["pool_scale"], sg["sconv_w"], sg["cconv_w"], sg["cconv_ln_g"], sg["cconv_ln_b"],
                 sg["sgu_ln_g"], sg["sgu_ln_b"], sg["sgu_w"], dgb) = res[1:]
                sg["sgu_b"] = dgb[:, :, 0]
                comm = None
                if l == 0:
                    early = [w for w in widths if w != D]
                    comm = _gather_comm([pack(w) for w in early], [(b, None) for b in range(len(early))])
                g_in, extra = _matmul_tn(sv["h_mix"][None], dproj, N_DEV, "w_in_grad", comm)
                if l == 0:
                    for w, g in zip(early, extra):
                        gathered_small[widths.index(w)] = g
                dx, sg["mix_norm"], (r_in,) = _matmul_nt_normbwd(
                    dproj, W["w_in", l], sv["x_mix"], row(mix_norm, l), dx, "proj_bwd",
                    _scatter_comm([g_in], PEERS_SAME_CORE))
                second.append((("w_in", l), g_in, r_in, (PEERS_NEAR_OTHER, PEERS_FAR_OTHER)))
            else:
                keys, comm = rest_of_sends()
                dab, sh, extra = _ffn_bwd_hidden(dx, W[tag + "_w2", l], sv["ab_" + tag], "ffn_hidden_bwd", comm)
                settle(keys, extra)
                keys, comm = rest_of_sends()
                g_w2, extra = _matmul_tn(sh[None], dx[None], 1, "ffn_w2_grad", comm)
                settle(keys, extra)
                g_w2 = g_w2.reshape(N_DEV, F // N_DEV, D)
                g_w13, (R[tag + "_w2", l],) = _matmul_tn(dab, sv["h_" + tag][None], 1, "ffn_w13_grad",
                                                         _scatter_comm([g_w2]), b_shared=True)
                g_w13 = g_w13.reshape(N_DEV, fs, D)
                last = tag == "ffn1" and l == 0
                now, later = (PEERS_BUT_NEAR_OTHER, PEERS_NEAR_OTHER) if last else (PEERS_SAME_CORE, PEERS_OTHER_CORE)
                dx, sg[tag + "_norm"], (r_w13,) = _matmul_nt_normbwd(
                    dab, W[tag + "_w13", l], sv["x_" + tag], row(P[tag + "_norm"], l), dx, "ffn_up_bwd",
                    _scatter_comm([g_w13], now), w_t=True)
                second.append(((tag + "_w13", l), g_w13, r_w13, (later,)))
    grad_x = dx[None]
    out = {}

    def as2d(n, a):
        if n == "final_norm":
            return a.reshape(1, D)
        return a.reshape(-1, a.shape[-1])

    for i, n in enumerate(["w_out", "w_up", "ffn2_w13", "w_in", "ffn2_w2", "ffn1_w2", "ffn1_w13"]):
        shp = P[n].shape
        if n.endswith("w13"):
            flat, back = (lambda a: jnp.swapaxes(a, 1, 2)), (lambda a: jnp.swapaxes(a, 1, 2))
        else:
            rows, cols = math.prod(shp[1:-1]), shp[-1]
            flat, back = (lambda a: a.reshape(L, rows, cols)), (lambda a: a.reshape(shp))
        keys, comm = [], None
        if i == 0:
            keys, sends = rest_of_sends()
            comm = _join_comm(sends, _gather_comm([pack(D)], [(0, None)]))
        res, extra = _adamw_sharded([R[n, l] for l in range(L)], flat(P[n]), flat(P["m_" + n]), flat(P["v_" + n]),
                                    "adamw_sharded", comm)
        if i == 0:
            settle(keys, extra[:len(keys)])
            gathered_small[widths.index(D)] = extra[len(keys)]
        out[n] = tuple(back(a) for a in res)

    res = _adamw_replicated(gathered_small, [layout[n] for n in small_names],
                            [(as2d(n, P[n]), as2d(n, P["m_" + n]), as2d(n, P["v_" + n])) for n in small_names],
                            "adamw_replicated")
    for p, n in enumerate(small_names):
        out[n] = tuple(a.reshape(P[n].shape) for a in res[4 * p:4 * p + 4])
    conv_sum = res[4 * len(small_names) + conv_at["b"]][conv_at["r0"]:conv_at["r0"] + conv_at["rows"]]
    conv_mine = lax.dynamic_slice_in_dim(conv_sum, me * cw, cw, axis=1)

    def conv2d(a, b):
        return jnp.concatenate([a, b], axis=1).reshape(L * (SCONV_K + CCONV_K), cw)

    cd, cm, cv = _adamw_small(conv_mine, conv2d(sconv_w, cconv_w), conv2d(m_sconv_w, m_cconv_w),
                              conv2d(v_sconv_w, v_cconv_w), "adamw_conv")
    for n, sl in (("sconv_w", slice(0, SCONV_K)), ("cconv_w", slice(SCONV_K, SCONV_K + CCONV_K))):
        out[n] = tuple(a.reshape(L, SCONV_K + CCONV_K, cw)[:, sl] for a in (conv_mine, cd, cm, cv))

    order = ["ffn1_norm", "ffn1_w13", "ffn1_w2", "mix_norm", "w_in", "pool_w", "pool_scale", "sconv_w", "cconv_w",
             "cconv_ln_g", "cconv_ln_b", "sgu_ln_g", "sgu_ln_b", "sgu_w", "sgu_b", "w_up", "w_out", "ffn2_norm",
             "ffn2_w13", "ffn2_w2", "final_norm"]
    return (loss, grad_x, *[out[n][0] for n in order], *[out[n][1] for n in order],
            *[out[n][2] for n in order], *[out[n][3] for n in order])
```

```python
import math

import jax
import jax.numpy as jnp
from jax import lax
from jax.experimental import pallas as pl
from jax.experimental.pallas import tpu as pltpu

F32 = jnp.float32
BF16 = jnp.bfloat16
EPS = 1e-6
ADAM_LR = 0.001
ADAM_B1 = 0.9
ADAM_B2 = 0.999
ADAM_EPS = 1e-08
ADAM_WD = 0.01
ADAM_STEP = 10
SGU_BLOCK = 128
SGU_CHUNK = 64
SCONV_K = 3
CCONV_K = 31
HALO = 32
V7X_VMEM_LIMIT = 48 * 1024 * 1024
LANES = 128
BF16_ROWS = 16
WIDE = 11 * LANES
ROWS_L, ROWS_M, ROWS_S = 1024, 512, 256
ELEMENTWISE_ROWS = 32
MESH_AXES = ("x", "y", "c")
N_DEV = 8
_GELU_C0 = math.sqrt(2.0 / math.pi)
_GELU_C1 = 0.044715

BS = pl.BlockSpec
SDS = jax.ShapeDtypeStruct
ANY = pl.BlockSpec(memory_space=pl.ANY)


def _tile(n, pref, align=128):
    if n <= pref:
        return n
    t = pref - pref % align
    while t > 0:
        if n % t == 0:
            return t
        t -= align
    return n


def _sig(v):
    return 1.0 / (1.0 + jnp.exp(-v))


def _gelu(v):
    t = jnp.tanh(_GELU_C0 * (v + _GELU_C1 * (v * v * v)))
    return 0.5 * v * (1.0 + t), t


def _gelu_grad(v, t):
    return 0.5 * (1.0 + t) + 0.5 * v * (1.0 - t * t) * (_GELU_C0 * (1.0 + 3.0 * _GELU_C1 * v * v))


def _ln_stats(v):
    mu = jnp.mean(v, axis=-1, keepdims=True)
    vc = v - mu
    var = jnp.mean(vc * vc, axis=-1, keepdims=True)
    rstd = lax.rsqrt(var + EPS)
    return vc * rstd, rstd


def _ln_bwd(dvh, vh, rstd):
    return rstd * (dvh - jnp.mean(dvh, axis=-1, keepdims=True) - vh * jnp.mean(dvh * vh, axis=-1, keepdims=True))


def _dot(a, b):
    return jnp.dot(a, b, preferred_element_type=F32)


def _dot_nt(a, b):
    return lax.dot_general(a, b, (((1,), (1,)), ((), ())), preferred_element_type=F32)


def _dot_tn(a, b):
    return lax.dot_general(a, b, (((0,), (0,)), ((), ())), preferred_element_type=F32)


def _mesh_pos():
    return lax.axis_index("x"), lax.axis_index("y"), lax.axis_index("c")


class _Comm:
    def __init__(self, ins, out_shape, sems, start, finish, aliases=None, middle=None):
        self.ins, self.out_shape, self.sems, self.start, self.finish = ins, out_shape, sems, start, finish
        self.aliases = aliases or {}
        self.middle = middle


def _gather_comm(shards, units):
    n_u = len(units)
    out_shape = []
    for t, l in units:
        shp = shards[t].shape if l is None else shards[t].shape[1:]
        out_shape.append(SDS((N_DEV,) + tuple(shp), shards[t].dtype))

    def upper_rows(o):
        shp = out_shape[o].shape[1:]
        assert shp[0] >= 2
        return shp[0] // 2 if len(shp) > 2 or shp[0] < 32 else shp[0] // 32 * 16

    def tools(ins, dsts, sems):
        send_sems, recv_sems, local_sems = sems
        x, y, c = _mesh_pos()
        me, sib = (x, y, c), (x, y, 1 - c)
        xn, yn, dg = (1 - x, y, c), (x, 1 - y, c), (1 - x, 1 - y, c)

        def src_of(o):
            t, l = units[o]
            return ins[t] if l is None else ins[t].at[l]

        def row(o, p, part=None):
            r = dsts[o].at[4 * p[0] + 2 * p[1] + p[2]]
            if part is None:
                return r
            h = upper_rows(o)
            return r.at[pl.ds(0, h)] if part == "upper" else r.at[pl.ds(h, out_shape[o].shape[1] - h)]

        def copy(o, k, src, dst, to):
            return pltpu.make_async_remote_copy(
                src_ref=src, dst_ref=dst, send_sem=send_sems.at[o * 8 + k], recv_sem=recv_sems.at[o * 8 + k],
                device_id=to, device_id_type=pl.DeviceIdType.MESH)

        def send(o, k):
            if k < 3:
                return copy(o, k, src_of(o), row(o, me), (sib, xn, yn)[k])
            if k == 3:
                return copy(o, k, row(o, xn, "upper"), row(o, xn, "upper"), yn)
            if k == 4:
                return copy(o, k, row(o, yn, "lower"), row(o, yn, "lower"), xn)
            blk = (xn, yn, dg)[k - 5]
            return copy(o, k, row(o, blk), row(o, blk), sib)

        def landed(o, k):
            def other(p):
                return (p[0], p[1], 1 - c)

            dst = (row(o, sib), row(o, xn), row(o, yn), row(o, dg, "upper"), row(o, dg, "lower"),
                   row(o, other(xn)), row(o, other(yn)), row(o, other(dg)))[k]
            return copy(o, k, dst, dst, me)

        def local(o):
            return pltpu.make_async_copy(src_of(o), row(o, me), local_sems.at[o])

        return send, landed, local

    def start(ins, dsts, sems):
        send, _, local = tools(ins, dsts, sems)
        for o in range(n_u):
            local(o).start()
            for k in (1, 2, 0):
                send(o, k).start()

    def middle(ins, dsts, sems):
        send, landed, _ = tools(ins, dsts, sems)
        for o in range(n_u):
            landed(o, 1).wait_recv()
            send(o, 3).start()
            landed(o, 2).wait_recv()
            send(o, 4).start()
            send(o, 5).start()
            send(o, 6).start()

    def finish(ins, dsts, sems):
        send, landed, local = tools(ins, dsts, sems)
        for o in range(n_u):
            landed(o, 3).wait_recv()
            landed(o, 4).wait_recv()
            send(o, 7).start()
        for o in range(n_u):
            for k in (0, 5, 6, 7):
                landed(o, k).wait_recv()
        for o in range(n_u):
            for k in range(8):
                send(o, k).wait_send()
            local(o).wait()

    sems = [pltpu.SemaphoreType.DMA((8 * n_u,)), pltpu.SemaphoreType.DMA((8 * n_u,)), pltpu.SemaphoreType.DMA((n_u,))]
    return _Comm(list(shards), out_shape, sems, start, finish, middle=middle)


PEERS_ALL = (1, 2, 3, 4, 5, 6, 7)
PEERS_SAME_CORE = (1, 2, 4, 6)
PEERS_OTHER_CORE = (3, 5, 7)
PEERS_BUT_NEAR_OTHER = (1, 2, 4, 6, 7)
PEERS_NEAR_OTHER = (3, 5)
PEERS_FAR_OTHER = (7,)


def _scatter_comm(parts, peers=PEERS_ALL, into=None):
    n_u = len(parts)

    def tools(ins, dsts, sems):
        send_sems, recv_sems, local_sems = sems
        x, y, c = _mesh_pos()
        me = 4 * x + 2 * y + c

        def peer(k):
            return ((x + ((k >> 2) & 1)) % 2, (y + ((k >> 1) & 1)) % 2, (c + (k & 1)) % 2)

        def copy(u, k, wait=False):
            p = peer(k)
            pi = 4 * p[0] + 2 * p[1] + p[2]
            return pltpu.make_async_remote_copy(
                src_ref=ins[u].at[pi], dst_ref=dsts[u].at[pi if wait else me],
                send_sem=send_sems.at[u * 7 + k - 1], recv_sem=recv_sems.at[u * 7 + k - 1],
                device_id=p, device_id_type=pl.DeviceIdType.MESH)

        def local(u):
            return pltpu.make_async_copy(ins[u].at[me], dsts[u].at[me], local_sems.at[u])

        return copy, local

    def start(ins, dsts, sems):
        copy, local = tools(ins, dsts, sems)
        for u in range(n_u):
            if into is None:
                local(u).start()
            for k in peers:
                copy(u, k).start()

    def finish(ins, dsts, sems):
        copy, local = tools(ins, dsts, sems)
        for u in range(n_u):
            for k in peers:
                copy(u, k, wait=True).wait()
            if into is None:
                local(u).wait()

    sems = [pltpu.SemaphoreType.DMA((7 * n_u,)), pltpu.SemaphoreType.DMA((7 * n_u,)), pltpu.SemaphoreType.DMA((n_u,))]
    aliases = {} if into is None else {n_u + u: u for u in range(n_u)}
    return _Comm(list(parts) + list(into or []), [SDS(p.shape, p.dtype) for p in parts], sems, start, finish, aliases)


def _join_comm(a, b):
    n_i, n_o, n_s = len(a.ins), len(a.out_shape), len(a.sems)

    def both(stage):
        def run(ins, outs, sems):
            for comm, part in ((a, (ins[:n_i], outs[:n_o], sems[:n_s])), (b, (ins[n_i:], outs[n_o:], sems[n_s:]))):
                if getattr(comm, stage) is not None:
                    getattr(comm, stage)(*part)
        return run

    aliases = {**a.aliases, **{n_i + ci: n_o + co for ci, co in b.aliases.items()}}
    return _Comm(a.ins + b.ins, a.out_shape + b.out_shape, a.sems + b.sems, both("start"), both("finish"), aliases,
                 both("middle"))


def _pcall(body, name, grid, in_specs, out_specs, out_shape, scratch, sem, args, comm=None, aliases=None):
    n_i, n_o, n_s = len(in_specs), len(out_specs), len(scratch)
    aliases = aliases or {}
    if comm is None:
        res = pl.pallas_call(
            body, name=name, grid=grid, in_specs=in_specs, out_specs=out_specs, out_shape=out_shape,
            scratch_shapes=scratch, input_output_aliases=aliases,
            compiler_params=pltpu.CompilerParams(dimension_semantics=sem, vmem_limit_bytes=V7X_VMEM_LIMIT),
        )(*args)
        return res, []
    n_ci, n_co = len(comm.ins), len(comm.out_shape)

    def wrapped(*refs):
        ins = refs[:n_i]
        cins = refs[n_i:n_i + n_ci]
        outs = refs[n_i + n_ci:n_i + n_ci + n_o]
        couts = refs[n_i + n_ci + n_o:n_i + n_ci + n_o + n_co]
        rest = refs[n_i + n_ci + n_o + n_co:]
        step = 0
        for d, g in enumerate(grid):
            step = step * g + pl.program_id(d)
        n_steps = math.prod(grid)
        mid = (n_steps * 5) // 8
        staged = comm.middle is not None and 0 < mid < n_steps - 1

        @pl.when(step == 0)
        def _():
            comm.start(cins, couts, rest[n_s:])

        if staged:
            @pl.when(step == mid)
            def _():
                comm.middle(cins, couts, rest[n_s:])

        body(*ins, *outs, *rest[:n_s])

        @pl.when(step == n_steps - 1)
        def _():
            if comm.middle is not None and not staged:
                comm.middle(cins, couts, rest[n_s:])
            comm.finish(cins, couts, rest[n_s:])

    res = pl.pallas_call(
        wrapped, name=name, grid=grid, in_specs=list(in_specs) + [ANY] * n_ci,
        out_specs=list(out_specs) + [ANY] * n_co, out_shape=list(out_shape) + list(comm.out_shape),
        scratch_shapes=list(scratch) + list(comm.sems),
        input_output_aliases={**aliases, **{n_i + ci: n_o + co for ci, co in comm.aliases.items()}},
        compiler_params=pltpu.CompilerParams(dimension_semantics=("arbitrary",) * len(grid),
                                             vmem_limit_bytes=V7X_VMEM_LIMIT),
    )(*args, *comm.ins)
    return res[:n_o], res[n_o:]


def _rmsnorm_fwd(x, g, name, comm=None):
    S, D = x.shape
    tm = _tile(S, ROWS_M, BF16_ROWS)

    def body(x_ref, g_ref, h_ref):
        h_ref[...] = _rmsnorm_rows(x_ref[...], g_ref[...])

    (h,), extra = _pcall(body, name, (S // tm,), [BS((tm, D), lambda i: (i, 0)), BS((1, D), lambda i: (0, 0))],
                         [BS((tm, D), lambda i: (i, 0))], [SDS((S, D), BF16)], [], ("parallel",), (x, g), comm)
    return h, extra


def _matmul_fwd(a, w, name, comm=None, w_t=False, out_dtype=F32):
    S, K = a.shape
    C = w.shape[0]
    Fc = w.shape[1] if w_t else w.shape[2]
    tn = _tile(Fc, WIDE)
    tm = _tile(S, 2 * ROWS_L, BF16_ROWS)

    def body(a_ref, w_ref, o_ref):
        p = _dot_nt(a_ref[...], w_ref[0]) if w_t else _dot(a_ref[...], w_ref[0])
        o_ref[0] = p.astype(out_dtype)

    w_spec = BS((1, tn, K), lambda c, n, i: (c, n, 0)) if w_t else BS((1, K, tn), lambda c, n, i: (c, 0, n))
    (o,), extra = _pcall(
        body, name, (C, Fc // tn, S // tm), [BS((tm, K), lambda c, n, i: (i, 0)), w_spec],
        [BS((1, tm, tn), lambda c, n, i: (c, i, n))], [SDS((C, S, Fc), out_dtype)], [],
        ("parallel", "parallel", "parallel"), (a, w), comm)
    return o, extra


def _rmsnorm_rows(xv, g):
    r = lax.rsqrt(jnp.mean(xv * xv, axis=-1, keepdims=True) + EPS)
    return (xv * r * g).astype(BF16)


def _swiglu_down(ab, w2, x, g_next, name, comm=None):
    _, S, F = ab.shape
    D = w2.shape[1]
    tk = _tile(F, WIDE)
    tm = _tile(S, ROWS_M, BF16_ROWS)
    nk = F // tk

    te = _tile(tm, ELEMENTWISE_ROWS, BF16_ROWS)

    def body(ab_ref, w_ref, x_ref, g_ref, o_ref, h_ref, s_ref):
        k = pl.program_id(1)
        for r0 in range(0, tm, te):
            rows = slice(r0, r0 + te)
            a = ab_ref[0, rows, :].astype(F32)
            s_ref[rows, :] = (a * _sig(a) * ab_ref[1, rows, :].astype(F32)).astype(BF16)
        p = 0.5 * _dot(s_ref[...], w_ref[...])

        @pl.when(k == 0)
        def _():
            o_ref[...] = x_ref[...] + p

        @pl.when(k > 0)
        def _():
            o_ref[...] += p

        @pl.when(k == nk - 1)
        def _():
            h_ref[...] = _rmsnorm_rows(o_ref[...], g_ref[...])

    (o, h), extra = _pcall(
        body, name, (S // tm, nk),
        [BS((2, tm, tk), lambda i, k: (0, i, k)), BS((tk, D), lambda i, k: (k, 0)), BS((tm, D), lambda i, k: (i, 0)),
         BS((1, D), lambda i, k: (0, 0))],
        [BS((tm, D), lambda i, k: (i, 0)), BS((tm, D), lambda i, k: (i, 0))],
        [SDS((S, D), F32), SDS((S, D), BF16)], [pltpu.VMEM((tm, tk), BF16)], ("parallel", "arbitrary"),
        (ab, w2, x, g_next), comm)
    return o, h, extra


def _ffn_bwd_hidden(dy, w2, ab, name, comm=None):
    S, D = dy.shape
    F = w2.shape[0]
    tk = _tile(F, WIDE)
    tm = _tile(S, ROWS_M, BF16_ROWS)
    te = _tile(tm, ELEMENTWISE_ROWS, BF16_ROWS)

    def body(dy_ref, w_ref, ab_ref, dab_ref, s_ref, ds_ref):
        ds_ref[...] = 0.5 * _dot_nt(dy_ref[...].astype(BF16), w_ref[...])
        for r0 in range(0, tm, te):
            rows = slice(r0, r0 + te)
            ds = ds_ref[rows, :]
            a = ab_ref[0, rows, :].astype(F32)
            b = ab_ref[1, rows, :].astype(F32)
            sg = _sig(a)
            sa = a * sg
            dab_ref[0, rows, :] = (ds * b * (sg * (1.0 + a * (1.0 - sg)))).astype(BF16)
            dab_ref[1, rows, :] = (ds * sa).astype(BF16)
            s_ref[rows, :] = (0.5 * (sa * b)).astype(BF16)

    (dab, sh), extra = _pcall(
        body, name, (F // tk, S // tm),
        [BS((tm, D), lambda k, i: (i, 0)), BS((tk, D), lambda k, i: (k, 0)), BS((2, tm, tk), lambda k, i: (0, i, k))],
        [BS((2, tm, tk), lambda k, i: (0, i, k)), BS((tm, tk), lambda k, i: (i, k))],
        [SDS((2, S, F), BF16), SDS((S, F), BF16)], [pltpu.VMEM((tm, tk), F32)], ("parallel", "parallel"),
        (dy, w2, ab), comm)
    return dab, sh, extra


def _matmul_tn(a, b, n_c, name, comm=None, b_shared=False):
    G, S, M = a.shape
    _, _, Fc = b.shape
    C = n_c
    tM = _tile(M, WIDE)
    tn = _tile(Fc, WIDE)
    ts = _tile(S, 2 * ROWS_L if b.dtype == BF16 else ROWS_L, BF16_ROWS)
    n_s = S // ts

    def body(a_ref, b_ref, o_ref, acc):
        s = pl.program_id(4)
        p = _dot_tn(a_ref[0].astype(BF16), b_ref[0].astype(BF16))

        @pl.when(s == 0)
        def _():
            acc[...] = p

        @pl.when(s > 0)
        def _():
            acc[...] += p

        @pl.when(s == n_s - 1)
        def _():
            o_ref[0] = acc[...].astype(BF16)

    (o,), extra = _pcall(
        body, name, (G, M // tM, C, Fc // tn, n_s),
        [BS((1, ts, tM), lambda g, m, c, n, s: (g, s, m)), BS((1, ts, tn), lambda g, m, c, n, s: (c if b_shared else g * C + c, s, n))],
        [BS((1, tM, tn), lambda g, m, c, n, s: (g * C + c, m, n))], [SDS((G * C, M, Fc), BF16)],
        [pltpu.VMEM((tM, tn), F32)], ("parallel", "parallel", "parallel", "parallel", "arbitrary"), (a, b), comm)
    return o, extra


def _matmul_nt_normbwd(b, w, x, gam, dres, name, comm=None, w_t=False):
    C, S, Fc = b.shape
    D = w.shape[2] if w_t else w.shape[1]
    tk = _tile(Fc, WIDE)
    tm = _tile(S, ROWS_L, BF16_ROWS)
    te = _tile(tm, 256, 8)
    nk = Fc // tk

    def body(b_ref, w_ref, x_ref, g_ref, r_ref, dx_ref, dg_ref):
        i, c, k = pl.program_id(0), pl.program_id(1), pl.program_id(2)
        p = _dot(b_ref[0], w_ref[0]) if w_t else _dot_nt(b_ref[0], w_ref[0])
        first = jnp.logical_and(c == 0, k == 0)

        @pl.when(first)
        def _():
            dx_ref[...] = p

        @pl.when(jnp.logical_not(first))
        def _():
            dx_ref[...] += p

        @pl.when(jnp.logical_and(c == C - 1, k == nk - 1))
        def _():
            dgp = None
            for r0 in range(0, tm, te):
                rows = slice(r0, r0 + te)
                xv = x_ref[rows, :]
                r = lax.rsqrt(jnp.mean(xv * xv, axis=-1, keepdims=True) + EPS)
                xn = xv * r
                dh = dx_ref[rows, :]
                dxn = dh * g_ref[...]
                dx_ref[rows, :] = r_ref[rows, :] + r * (dxn - xn * jnp.mean(dxn * xn, axis=-1, keepdims=True))
                t = jnp.sum(dh * xn, axis=0, keepdims=True)
                dgp = t if dgp is None else dgp + t

            @pl.when(i == 0)
            def _():
                dg_ref[...] = dgp

            @pl.when(i > 0)
            def _():
                dg_ref[...] += dgp

    once = dict(pipeline_mode=pl.Buffered(1))
    (dx, dg), extra = _pcall(
        body, name, (S // tm, C, nk),
        [BS((1, tm, tk), lambda i, c, k: (c, i, k)),
         BS((1, tk, D), lambda i, c, k: (c, k, 0)) if w_t else BS((1, D, tk), lambda i, c, k: (c, 0, k)),
         BS((tm, D), lambda i, c, k: (i, 0), **once), BS((1, D), lambda i, c, k: (0, 0)),
         BS((tm, D), lambda i, c, k: (i, 0), **once)],
        [BS((tm, D), lambda i, c, k: (i, 0)), BS((1, D), lambda i, c, k: (0, 0))],
        [SDS((S, D), F32), SDS((1, D), F32)], [],
        ("arbitrary", "arbitrary", "arbitrary"), (b, w, x, gam, dres), comm)
    return dx, dg, extra


def _final_loss(x, gam, target, name):
    S, D = x.shape
    tm = _tile(S, 512, 8)

    def body(x_ref, g_ref, t_ref, loss_ref, dx_ref, dg_ref):
        i = pl.program_id(0)
        xv = x_ref[...]
        r = lax.rsqrt(jnp.mean(xv * xv, axis=-1, keepdims=True) + EPS)
        xn = xv * r
        err = xn * g_ref[...] - t_ref[...]
        part = 0.5 * jnp.sum(jnp.mean(err * err, axis=-1, keepdims=True), axis=0, keepdims=True)
        dy = err * (1.0 / D)
        dxn = dy * g_ref[...]
        dx_ref[...] = r * (dxn - xn * jnp.mean(dxn * xn, axis=-1, keepdims=True))
        dgp = jnp.sum(dy * xn, axis=0, keepdims=True)
        lp = jnp.broadcast_to(part, loss_ref.shape)

        @pl.when(i == 0)
        def _():
            dg_ref[...] = dgp
            loss_ref[...] = lp

        @pl.when(i > 0)
        def _():
            dg_ref[...] += dgp
            loss_ref[...] += lp

    res, _ = _pcall(
        body, name, (S // tm,),
        [BS((tm, D), lambda i: (i, 0)), BS((1, D), lambda i: (0, 0)), BS((tm, D), lambda i: (i, 0))],
        [BS((8, 128), lambda i: (0, 0)), BS((tm, D), lambda i: (i, 0)), BS((1, D), lambda i: (0, 0))],
        [SDS((8, 128), F32), SDS((S, D), F32), SDS((1, D), F32)], [], ("arbitrary",), (x, gam, target))
    return res


CONV_CHUNK = 32


def _fill_shifted(rot, n):
    for b in range(1, 8):
        rot[b, 0:n - 8, :] = rot[0, b:b + n - 8, :]


def _window(rot, off, r0, rows):
    b = off % 8
    return rot[b, off - b + r0:off - b + r0 + rows, :]


def _taps(rot, w_ref, offs, n_rows, out):
    for r0 in range(0, n_rows, CONV_CHUNK):
        acc = None
        for k, off in enumerate(offs):
            t = w_ref[k:k + 1, :] * _window(rot, off, r0, CONV_CHUNK)
            acc = t if acc is None else acc + t
        out[r0:r0 + CONV_CHUNK, :] = acc


def _tap_grads(rot, offs, g_plane, n_rows, dw_ref):
    for k, off in enumerate(offs):
        acc = None
        for r0 in range(0, n_rows, CONV_CHUNK):
            p = g_plane[0, r0:r0 + CONV_CHUNK, :] * _window(rot, off, r0, CONV_CHUNK)
            acc = p if acc is None else acc + p
        dw_ref[k:k + 1, :] += jnp.sum(acc, axis=0, keepdims=True)


def _sgu_masks():
    ii = lax.broadcasted_iota(jnp.int32, (SGU_BLOCK, SGU_BLOCK), 0) // SGU_CHUNK
    jj = lax.broadcasted_iota(jnp.int32, (SGU_BLOCK, SGU_BLOCK), 1) // SGU_CHUNK
    return jj <= ii, ii <= jj


def _mixers_fwd(proj, pool_w, pool_scale, sconv_w, cconv_w, cln_g, cln_b, sln_g, sln_b, sgu_w, sgu_bias, name,
                comm=None):
    _, S, D = proj.shape
    BW = D // 2
    GW = BW // 4
    TS = _tile(S, ROWS_S, SGU_BLOCK)
    H = HALO
    hb = TS // H

    def main(blk, col):
        return BS((1, TS, BW), lambda i: (blk, i, col))

    def back(blk, col):
        return BS((1, H, BW), lambda i: (blk, jnp.maximum(i * hb - 1, 0), col))

    def full(a):
        nd = a.ndim
        return BS(a.shape, lambda i: (0,) * nd)

    def body(pa_m, pa_b, xi_m, xi_b, bg_m, cg_m, cg_b, ca_m, ca_b, cb_m, cb_b, du_m, dv_m,
             pw, ps, sw, cw, clg, clb, slg, slb, gw, gbias, y_ref, y1_ref, e1, e2, e3):
        i = pl.program_id(0)
        nb = jnp.where(i > 0, 1.0, 0.0).astype(F32)
        rows = i * TS + lax.broadcasted_iota(jnp.int32, (TS, 1), 0)

        e1[0:H, :] = pa_b[0] * nb
        e1[H:H + TS, :] = pa_m[0]
        for g in range(4):
            cols = slice(g * GW, (g + 1) * GW)
            win = 2 << g
            wsum = e1[H:H + TS, cols]
            for k in range(1, win):
                wsum = wsum + e1[H - k:H - k + TS, cols]
            cnt = jnp.minimum(rows + 1, win).astype(F32)
            d = wsum / cnt - e1[H:H + TS, cols]
            yg = _dot(d.astype(BF16), pw[g].astype(BF16)) * ps[:, cols]
            y_ref[0, :, cols] = yg.astype(BF16)

        e2[0:H, :] = cg_b[0] * xi_b[0] * nb
        e2[H:H + TS, :] = cg_m[0] * xi_m[0]
        cz = sw[0:1, :] * e2[H - 2:H - 2 + TS, :]
        for k in range(1, SCONV_K):
            cz = cz + sw[k:k + 1, :] * e2[H - 2 + k:H - 2 + k + TS, :]
        y_ref[1] = (bg_m[0] * cz).astype(BF16)

        e3[0, 0:H, :] = ca_b[0] * _sig(cb_b[0]) * nb
        e3[0, H:H + TS, :] = ca_m[0] * _sig(cb_m[0])
        _fill_shifted(e3, H + TS)
        _taps(e3, cw, [H - (CCONV_K - 1) + k for k in range(CCONV_K)], TS, y1_ref)
        yh, _ = _ln_stats(y1_ref[...])
        y2 = yh * clg[...] + clb[...]
        y_ref[2] = (y2 * _sig(y2)).astype(BF16)

        u, _ = _gelu(du_m[0])
        v, _ = _gelu(dv_m[0])
        vh, _ = _ln_stats(v)
        vn = vh * slg[...] + slb[...]
        mask, _ = _sgu_masks()
        for h in range(4):
            wm = jnp.where(mask, gw[h], 0.0).astype(BF16)
            cs = slice(h * GW, (h + 1) * GW)
            for n in range(TS // SGU_BLOCK):
                rs = slice(n * SGU_BLOCK, (n + 1) * SGU_BLOCK)
                z = _dot(wm, vn[rs, cs].astype(BF16)) + gbias[h]
                y_ref[3, rs, cs] = (u[rs, cs] * z).astype(BF16)

    args = [proj] * 13 + [pool_w, pool_scale, sconv_w, cconv_w, cln_g, cln_b, sln_g, sln_b, sgu_w, sgu_bias]
    in_specs = [main(0, 0), back(0, 0), main(0, 1), back(0, 1), main(1, 0), main(1, 1), back(1, 1),
                main(2, 0), back(2, 0), main(2, 1), back(2, 1), main(3, 0), main(3, 1)]
    in_specs += [full(a) for a in args[13:]]
    (y, y1), extra = _pcall(body, name, (S // TS,), in_specs,
                            [BS((4, TS, BW), lambda i: (0, i, 0)), BS((TS, BW), lambda i: (i, 0))],
                            [SDS((4, S, BW), BF16), SDS((S, BW), F32)],
                            [pltpu.VMEM((H + TS, BW), F32)] * 2 + [pltpu.VMEM((8, H + TS, BW), F32)], ("parallel",),
                            args, comm)
    return y, y1, extra


def _mixers_bwd(proj, y1, dy, dproj_gates, pool_w, pool_wt, pool_scale, sconv_w, cconv_w, cln_g, cln_b, sln_g, sln_b,
                sgu_w, sgu_wt, sgu_bias, name, comm=None):
    _, S, D = proj.shape
    BW = D // 2
    GW = BW // 4
    TS = _tile(S, ROWS_S, SGU_BLOCK)
    H = HALO
    hb = TS // H
    n_t = S // TS
    E = TS + H

    def main(blk, col):
        return BS((1, TS, BW), lambda i: (blk, i, col))

    def back(blk, col):
        return BS((1, H, BW), lambda i: (blk, jnp.maximum(i * hb - 1, 0), col))

    def front(blk, col):
        return BS((1, H, BW), lambda i: (blk, jnp.minimum((i + 1) * hb, S // H - 1), col))

    def full(a):
        nd = a.ndim
        return BS(a.shape, lambda i: (0,) * nd)

    def body(pa_b, pa_m, xi_b, xi_m, bg_m, bg_f, cg_b, cg_m, ca_b, ca_m, cb_b, cb_m, du_m, dv_m, y1_m, y1_f,
             dya_m, dya_f, dyb_m, dyb_f, dyc_m, dyc_f, dyd_m,
             pw, pwt, ps, sw, cw, clg, clb, slg, slb, gw, gwt, gbias, _gates_in,
             dp_ref, dpw, dps, dsw, dcw, dclg, dclb, dslg, dslb, dgw, dgb,
             e1, e2, e3, e4, e5, ra, rb):
        i = pl.program_id(0)
        nb = jnp.where(i > 0, 1.0, 0.0).astype(F32)
        nf = jnp.where(i < n_t - 1, 1.0, 0.0).astype(F32)
        rows_m = i * TS + lax.broadcasted_iota(jnp.int32, (TS, 1), 0)
        rows_e = i * TS + lax.broadcasted_iota(jnp.int32, (E, 1), 0)

        @pl.when(i == 0)
        def _():
            for r in (dpw, dps, dsw, dcw, dclg, dclb, dslg, dslb, dgw, dgb):
                r[...] = jnp.zeros(r.shape, F32)

        e1[0:H, :] = pa_b[0] * nb
        e1[H:H + TS, :] = pa_m[0]
        e2[0:TS, :] = dya_m[0] * ps[...]
        e2[TS:E, :] = dya_f[0] * ps[...] * nf
        for g in range(4):
            cols = slice(g * GW, (g + 1) * GW)
            win = 2 << g
            a_m = e1[H:H + TS, cols]
            wsum = a_m
            for k in range(1, win):
                wsum = wsum + e1[H - k:H - k + TS, cols]
            d = wsum / jnp.minimum(rows_m + 1, win).astype(F32) - a_m
            d16 = d.astype(BF16)
            dyp = e2[0:E, cols].astype(BF16)
            dd = _dot(dyp, pwt[g].astype(BF16))
            e3[0:E, cols] = dd / jnp.minimum(rows_e + 1, win).astype(F32)
            da = e3[0:TS, cols] - dd[0:TS]
            for k in range(1, win):
                da = da + e3[k:k + TS, cols]
            dp_ref[0, :, cols] = da.astype(BF16)
            ypre = _dot(d16, pw[g].astype(BF16))
            dps[:, cols] += jnp.sum(dya_m[0][:, cols] * ypre, axis=0, keepdims=True)
            dpw[g] += _dot(jnp.transpose(d).astype(BF16), dyp[0:TS])

        e4[0:H, :] = cg_b[0] * xi_b[0] * nb
        e4[H:H + TS, :] = cg_m[0] * xi_m[0]
        dyb = dyb_m[0]
        e5[0:TS, :] = dyb * bg_m[0]
        e5[TS:E, :] = dyb_f[0] * bg_f[0] * nf
        dcz = e5[0:TS, :]
        cz = None
        dz = None
        for k in range(SCONV_K):
            zk = e4[H - 2 + k:H - 2 + k + TS, :]
            wk = sw[k:k + 1, :]
            cz = wk * zk if cz is None else cz + wk * zk
            t = wk * e5[2 - k:2 - k + TS, :]
            dz = t if dz is None else dz + t
            dsw[k:k + 1, :] += jnp.sum(dcz * zk, axis=0, keepdims=True)
        dp_ref[0, :, BW:2 * BW] = (dz * cg_m[0]).astype(BF16)
        dp_ref[1, :, 0:BW] = (dyb * cz).astype(BF16)
        dp_ref[1, :, BW:2 * BW] = (dz * xi_m[0]).astype(BF16)

        sgm = _sig(cb_m[0])
        ra[0, 0:H, :] = ca_b[0] * _sig(cb_b[0]) * nb
        ra[0, H:H + TS, :] = ca_m[0] * sgm
        _fill_shifted(ra, H + TS)
        fwd_offs = [H - (CCONV_K - 1) + k for k in range(CCONV_K)]
        e4[0:TS, :] = y1_m[...]
        e4[TS:E, :] = y1_f[...]
        yh, rstd = _ln_stats(e4[0:E, :])
        y2 = yh * clg[...] + clb[...]
        s2 = _sig(y2)
        e1[0:TS, :] = dyc_m[0]
        e1[TS:E, :] = dyc_f[0] * nf
        dy2 = e1[0:E, :] * (s2 * (1.0 + y2 * (1.0 - s2)))
        dclg[...] += jnp.sum((dy2 * yh)[0:TS], axis=0, keepdims=True)
        dclb[...] += jnp.sum(dy2[0:TS], axis=0, keepdims=True)
        rb[0, 0:E, :] = _ln_bwd(dy2 * clg[...], yh, rstd)
        _fill_shifted(rb, E)
        _taps(rb, cw, [CCONV_K - 1 - k for k in range(CCONV_K)], TS, e5)
        _tap_grads(ra, fwd_offs, rb, TS, dcw)
        dy0 = e5[0:TS, :]
        dp_ref[2, :, 0:BW] = (dy0 * sgm).astype(BF16)
        dp_ref[2, :, BW:2 * BW] = (dy0 * ca_m[0] * (sgm * (1.0 - sgm))).astype(BF16)

        pu = du_m[0]
        pv = dv_m[0]
        u, tu = _gelu(pu)
        v, tv = _gelu(pv)
        vh, vr = _ln_stats(v)
        vn = vh * slg[...] + slb[...]
        dyd = dyd_m[0]
        mask, mask_t = _sgu_masks()
        for h in range(4):
            wm = jnp.where(mask, gw[h], 0.0).astype(BF16)
            wmt = jnp.where(mask_t, gwt[h], 0.0).astype(BF16)
            cs = slice(h * GW, (h + 1) * GW)
            for n in range(TS // SGU_BLOCK):
                rs = slice(n * SGU_BLOCK, (n + 1) * SGU_BLOCK)
                vb = vn[rs, cs].astype(BF16)
                z = _dot(wm, vb) + gbias[h]
                dzb = dyd[rs, cs] * u[rs, cs]
                dz16 = dzb.astype(BF16)
                e3[rs, cs] = dyd[rs, cs] * z
                e4[rs, cs] = _dot(wmt, dz16)
                dgw[h] += jnp.where(mask, _dot_nt(dz16, vb), 0.0)
                dgb[h] += dzb
        dvn = e4[0:TS, :]
        dslg[...] += jnp.sum(dvn * vh, axis=0, keepdims=True)
        dslb[...] += jnp.sum(dvn, axis=0, keepdims=True)
        dv = _ln_bwd(dvn * slg[...], vh, vr)
        dp_ref[3, :, 0:BW] = (e3[0:TS, :] * _gelu_grad(pu, tu)).astype(BF16)
        dp_ref[3, :, BW:2 * BW] = (dv * _gelu_grad(pv, tv)).astype(BF16)

        @pl.when(i == n_t - 1)
        def _():
            for h in range(4):
                dgb[h] = jnp.broadcast_to(jnp.sum(dgb[h], axis=1, keepdims=True), dgb.shape[1:])

    params = [pool_w, pool_wt, pool_scale, sconv_w, cconv_w, cln_g, cln_b, sln_g, sln_b, sgu_w, sgu_wt, sgu_bias]
    args = [proj] * 14 + [y1] * 2 + [dy] * 7 + params + [dproj_gates]
    in_specs = [back(0, 0), main(0, 0), back(0, 1), main(0, 1), main(1, 0), front(1, 0), back(1, 1), main(1, 1),
                back(2, 0), main(2, 0), back(2, 1), main(2, 1), main(3, 0), main(3, 1),
                BS((TS, BW), lambda i: (i, 0)), BS((H, BW), lambda i: (jnp.minimum((i + 1) * hb, S // H - 1), 0)),
                main(0, 0), front(0, 0), main(1, 0), front(1, 0), main(2, 0), front(2, 0), main(3, 0)]
    in_specs += [full(a) for a in params] + [ANY]
    small = [SDS(pool_w.shape, F32), SDS(pool_scale.shape, F32), SDS(sconv_w.shape, F32), SDS(cconv_w.shape, F32),
             SDS(cln_g.shape, F32), SDS(cln_b.shape, F32), SDS(sln_g.shape, F32), SDS(sln_b.shape, F32),
             SDS(sgu_w.shape, F32), SDS(sgu_bias.shape, F32)]
    out_specs = [BS((4, TS, D), lambda i: (0, i, 0))] + [full(s) for s in small]
    return _pcall(body, name, (n_t,), in_specs, out_specs, [SDS(dproj_gates.shape, BF16)] + small,
                  [pltpu.VMEM((TS + 2 * H, BW), F32)] * 5 + [pltpu.VMEM((8, TS + 2 * H, BW), F32)] * 2,
                  ("arbitrary",), args, comm, aliases={len(args) - 1: 0})


def _merge_fwd(y, proj, w_up, w_out, x, g_next, name, comm=None):
    _, S, BW = y.shape
    D = x.shape[1]
    tm = _tile(S, ROWS_M, BF16_ROWS)

    te = _tile(tm, ELEMENTWISE_ROWS, BF16_ROWS)

    def body(y_ref, pg_ref, wu_ref, wo_ref, x_ref, g_ref, o_ref, m_ref, h_ref, up_ref, acc_ref):
        for g in range(4):
            up_ref[...] = _dot(y_ref[g], wu_ref[g])
            for r0 in range(0, tm, te):
                rows = slice(r0, r0 + te)
                t = _sig(pg_ref[g, rows, :]) * up_ref[rows, :]
                if g == 0:
                    acc_ref[rows, :] = t
                elif g < 3:
                    acc_ref[rows, :] += t
                else:
                    m_ref[rows, :] = (acc_ref[rows, :] + t).astype(BF16)
        xn = x_ref[...] + _dot(m_ref[...], wo_ref[...])
        o_ref[...] = xn
        h_ref[...] = _rmsnorm_rows(xn, g_ref[...])

    once = dict(pipeline_mode=pl.Buffered(1))
    (o, m, h), extra = _pcall(
        body, name, (S // tm,),
        [BS((4, tm, BW), lambda i: (0, i, 0)), BS((4, tm, D), lambda i: (1, i, 0)),
         BS((4, BW, D), lambda i: (0, 0, 0), **once), BS((D, D), lambda i: (0, 0), **once),
         BS((tm, D), lambda i: (i, 0)), BS((1, D), lambda i: (0, 0))],
        [BS((tm, D), lambda i: (i, 0)), BS((tm, D), lambda i: (i, 0)), BS((tm, D), lambda i: (i, 0))],
        [SDS((S, D), F32), SDS((S, D), BF16), SDS((S, D), BF16)], [pltpu.VMEM((tm, D), F32)] * 2, ("parallel",),
        (y, proj, w_up, w_out, x, g_next), comm)
    return o, m, h, extra


def _merge_bwd(dx, y, proj, w_up, w_out, name, comm=None):
    _, S, BW = y.shape
    D = dx.shape[1]
    tm = _tile(S, ROWS_S, BF16_ROWS)

    te = _tile(tm, ELEMENTWISE_ROWS, BF16_ROWS)

    def body(dx_ref, y_ref, pg_ref, wu_ref, wo_ref, dup_ref, dp_ref, dy_ref, dm_ref, up_ref):
        dm_ref[...] = _dot_nt(dx_ref[...].astype(BF16), wo_ref[...])
        for g in range(4):
            up_ref[...] = _dot(y_ref[g], wu_ref[g])
            for r0 in range(0, tm, te):
                rows = slice(r0, r0 + te)
                gate = _sig(pg_ref[g, rows, :])
                dm = dm_ref[rows, :]
                dup_ref[g, rows, :] = (dm * gate).astype(BF16)
                dp_ref[g, rows, :] = (dm * up_ref[rows, :] * (gate * (1.0 - gate))).astype(BF16)
            dy_ref[g] = _dot_nt(dup_ref[g], wu_ref[g])

    res, extra = _pcall(
        body, name, (S // tm,),
        [BS((tm, D), lambda i: (i, 0)), BS((4, tm, BW), lambda i: (0, i, 0)), BS((4, tm, D), lambda i: (1, i, 0)),
         BS((4, BW, D), lambda i: (0, 0, 0)), BS((D, D), lambda i: (0, 0))],
        [BS((4, tm, D), lambda i: (0, i, 0)), BS((4, tm, D), lambda i: (1, i, 0)), BS((4, tm, BW), lambda i: (0, i, 0))],
        [SDS((4, S, D), BF16), SDS((8, S, D), BF16), SDS((4, S, BW), F32)], [pltpu.VMEM((tm, D), F32)] * 2,
        ("parallel",), (dx, y, proj, w_up, w_out), comm)
    return res, extra


def _adamw(w, g, m, v):
    m = ADAM_B1 * m + (1.0 - ADAM_B1) * g
    v = ADAM_B2 * v + (1.0 - ADAM_B2) * (g * g)
    m_hat = m / (1.0 - ADAM_B1 ** ADAM_STEP)
    v_hat = v / (1.0 - ADAM_B2 ** ADAM_STEP)
    delta = -ADAM_LR * (m_hat / (jnp.sqrt(v_hat) + ADAM_EPS) + ADAM_WD * w)
    return delta, m, v


def _adamw_sharded(parts, w, m, v, name, comm=None):
    L, R, C = w.shape
    tr = _tile(R, ROWS_S, BF16_ROWS)

    def body(*refs):
        p_refs = refs[:L]
        w_ref, m_ref, v_ref, g_out, d_out, m_out, v_out = refs[L:]
        l = pl.program_id(0)
        g = None
        for d in range(N_DEV):
            t = p_refs[0][d].astype(F32)
            for j in range(1, L):
                t = jnp.where(l == j, p_refs[j][d].astype(F32), t)
            g = t if g is None else g + t
        dl, mn, vn = _adamw(w_ref[0], g, m_ref[0], v_ref[0])
        g_out[0] = g
        d_out[0] = dl
        m_out[0] = mn
        v_out[0] = vn

    def part_spec(j):
        return BS((N_DEV, tr, C), lambda l, r: (0, jnp.where(l == j, r, 0), 0))

    blk = BS((1, tr, C), lambda l, r: (l, r, 0))
    return _pcall(body, name, (L, R // tr), [part_spec(j) for j in range(L)] + [blk, blk, blk], [blk] * 4,
                  [SDS((L, R, C), F32)] * 4, [], ("parallel", "parallel"), (*parts, w, m, v), comm)


def _adamw_replicated(gathered, layout, wmv, name):
    n_b = len(gathered)
    n_p = len(layout)

    def body(*refs):
        bufs = refs[:n_b]
        prm = refs[n_b:n_b + 3 * n_p]
        outs = refs[n_b + 3 * n_p:n_b + 7 * n_p]
        sums = refs[n_b + 7 * n_p:]
        for b in range(n_b):
            s = bufs[b][0]
            for d in range(1, N_DEV):
                s = s + bufs[b][d]
            sums[b][...] = s
        for p, (b, r0, nr) in enumerate(layout):
            g = sums[b][r0:r0 + nr, :]
            d, mn, vn = _adamw(prm[3 * p][...], g, prm[3 * p + 1][...], prm[3 * p + 2][...])
            outs[4 * p][...] = g
            outs[4 * p + 1][...] = d
            outs[4 * p + 2][...] = mn
            outs[4 * p + 3][...] = vn

    flat = [a for t in wmv for a in t]
    out_shape = []
    for (w, _, _) in wmv:
        out_shape += [SDS(w.shape, F32)] * 4
    out_shape += [SDS(g.shape[1:], F32) for g in gathered]
    return pl.pallas_call(
        body, name=name, out_shape=out_shape,
        compiler_params=pltpu.CompilerParams(vmem_limit_bytes=V7X_VMEM_LIMIT),
    )(*gathered, *flat)


def _adamw_small(g, w, m, v, name):
    def body(g_ref, w_ref, m_ref, v_ref, d_out, m_out, v_out):
        d, mn, vn = _adamw(w_ref[...], g_ref[...], m_ref[...], v_ref[...])
        d_out[...] = d
        m_out[...] = mn
        v_out[...] = vn

    return pl.pallas_call(body, name=name, out_shape=[SDS(w.shape, F32)] * 3)(g, w, m, v)


def _pad_rows(a, rows):
    return jnp.pad(a, ((0, rows - a.shape[0]), (0, 0)))


def kernel(x, ffn1_norm, ffn1_w13, ffn1_w2, mix_norm, w_in, pool_w, pool_scale, sconv_w, cconv_w, cconv_ln_g, cconv_ln_b, sgu_ln_g, sgu_ln_b, sgu_w, sgu_b, w_up, w_out, ffn2_norm, ffn2_w13, ffn2_w2, final_norm, loss_target, m_ffn1_norm, m_ffn1_w13, m_ffn1_w2, m_mix_norm, m_w_in, m_pool_w, m_pool_scale, m_sconv_w, m_cconv_w, m_cconv_ln_g, m_cconv_ln_b, m_sgu_ln_g, m_sgu_ln_b, m_sgu_w, m_sgu_b, m_w_up, m_w_out, m_ffn2_norm, m_ffn2_w13, m_ffn2_w2, m_final_norm, v_ffn1_norm, v_ffn1_w13, v_ffn1_w2, v_mix_norm, v_w_in, v_pool_w, v_pool_scale, v_sconv_w, v_cconv_w, v_cconv_ln_g, v_cconv_ln_b, v_sgu_ln_g, v_sgu_ln_b, v_sgu_w, v_sgu_b, v_w_up, v_w_out, v_ffn2_norm, v_ffn2_w13, v_ffn2_w2, v_final_norm):
    P = dict(locals())
    L = ffn1_norm.shape[0]
    S, D = x.shape[1], x.shape[2]
    BW = D // 2
    GW = BW // 4
    F = ffn1_w2.shape[1] * N_DEV
    fs = ffn1_w13.shape[2]
    cw = sconv_w.shape[2]
    me = 4 * lax.axis_index("x") + 2 * lax.axis_index("y") + lax.axis_index("c")

    big = ["ffn1_w13", "ffn1_w2", "w_in", "w_up", "w_out", "ffn2_w13", "ffn2_w2"]
    shards = [(jnp.swapaxes(P[n], 1, 2) if n.endswith("w13") else P[n]).astype(BF16) for n in big]
    conv_local = jnp.concatenate([sconv_w, cconv_w], axis=1)

    def gather_of(units):
        return _gather_comm(shards, [(big.index(n), l) for n, l in units])

    def ready(n, g):
        if n.endswith("w13"):
            return g.reshape(2, F, D)
        if n.endswith("w2"):
            return g.reshape(F, D)
        if n == "w_up":
            return jnp.transpose(g, (1, 2, 0, 3)).reshape(4, BW, D)
        if n == "w_out":
            return g.reshape(D, D)
        return g

    W = {}

    def take(units, arrays):
        for (n, l), g in zip(units, arrays):
            W[n, l] = ready(n, g)

    first_units = [("ffn1_w13", 0)]
    plan = {("ffn1_up", 0): [("ffn1_w2", 0), ("w_up", 0), ("w_out", 0)],
            ("ffn1_down", 0): [("w_in", 0)],
            ("proj", 0): [("w_in", 1)],
            ("mixers", 0): [("ffn2_w13", 0)], ("merge", 0): [("ffn2_w2", 0)],
            ("ffn2_up", 0): [("ffn1_w13", 1)], ("ffn2_down", 0): [("ffn1_w2", 1)],
            ("ffn1_up", 1): [("w_up", 1), ("w_out", 1)],
            ("proj", 1): [("ffn2_w13", 1)], ("mixers", 1): [("ffn2_w2", 1)]}
    assert L <= 2

    def carried(key):
        units = [u for u in plan.get(key, []) if u[1] < L]
        return units, (gather_of(units) if units else None)

    first = _gather_comm(shards + [conv_local], [(big.index(n), l) for n, l in first_units] + [(len(big), None)])
    h, got = _rmsnorm_fwd(x[0], ffn1_norm[0][None, :], "first_norm_fwd", first)
    take(first_units, got[:1])
    conv_full = jnp.transpose(got[1], (1, 2, 0, 3)).reshape(L, SCONV_K + CCONV_K, N_DEV * cw)
    sconv_full = conv_full[:, :SCONV_K]
    cconv_full = conv_full[:, SCONV_K:]

    sgu_bias = jnp.broadcast_to(sgu_b[:, :, :, None], sgu_b.shape + (GW,))
    pool_wt = jnp.swapaxes(pool_w, 2, 3)
    sgu_wt = jnp.swapaxes(sgu_w, 2, 3)

    def row(a, l):
        return a[l][None, :]

    saved = []
    xc = x[0]
    for l in range(L):
        sv = {}
        for tag in ("ffn1", None, "ffn2"):
            if tag is None:
                sv["x_mix"] = xc
                units, comm = carried(("proj", l))
                proj, extra = _matmul_fwd(h, W["w_in", l], "proj_fwd", comm)
                take(units, extra)
                units, comm = carried(("mixers", l))
                y, sv["y1"], extra = _mixers_fwd(
                    proj, pool_w[l], row(pool_scale, l), sconv_full[l], cconv_full[l], row(cconv_ln_g, l),
                    row(cconv_ln_b, l), row(sgu_ln_g, l), row(sgu_ln_b, l), sgu_w[l], sgu_bias[l], "mixers_fwd", comm)
                take(units, extra)
                units, comm = carried(("merge", l))
                sv.update(h_mix=h, proj=proj, y=y)
                xc, sv["merged"], h, extra = _merge_fwd(y, proj, W["w_up", l], W["w_out", l], xc, row(ffn2_norm, l),
                                                        "merge_fwd", comm)
                take(units, extra)
            else:
                sv["x_" + tag] = xc
                units, comm = carried((tag + "_up", l))
                ab, extra = _matmul_fwd(h, W[tag + "_w13", l], "ffn_up_fwd", comm, w_t=True, out_dtype=BF16)
                take(units, extra)
                units, comm = carried((tag + "_down", l))
                sv.update({"h_" + tag: h, "ab_" + tag: ab})
                if tag == "ffn1":
                    g_next = row(mix_norm, l)
                else:
                    g_next = row(ffn1_norm, l + 1) if l + 1 < L else final_norm[None, :]
                xc, h, extra = _swiglu_down(ab, W[tag + "_w2", l], xc, g_next, "ffn_down_fwd", comm)
                take(units, extra)
        saved.append(sv)

    loss_part, dx, d_final = _final_loss(xc, final_norm[None, :], loss_target[0], "loss_head")
    loss = lax.psum(loss_part[0, 0], MESH_AXES)

    R = {}
    second = []

    def rest_of_sends():
        todo = list(second)
        second.clear()
        comm = None
        if todo:
            assert len({e[3][0] for e in todo}) == 1
            comm = _scatter_comm([e[1] for e in todo], todo[0][3][0], [e[2] for e in todo])
        return todo, comm

    def settle(todo, arrays):
        for (key, g, _, stages), a in zip(todo, arrays):
            if len(stages) > 1:
                second.append((key, g, a, stages[1:]))
            else:
                R[key] = a

    wide = ["ffn1_norm", "mix_norm", "ffn2_norm", "final_norm"]
    half = ["pool_scale", "cconv_ln_g", "cconv_ln_b", "sgu_ln_g", "sgu_ln_b"]
    narrow = ["pool_w", "sgu_w", "sgu_b"]
    small_names = wide + half + narrow
    small_g = [dict() for _ in range(L)]
    widths = []
    for n in small_names:
        if P[n].shape[-1] not in widths:
            widths.append(P[n].shape[-1])
    layout, conv_at = {}, {}

    def pack(width):
        def stack_layers(n):
            return jnp.stack([small_g[l][n] for l in range(L)], axis=0)

        parts, r0 = [], 0
        for n in small_names:
            if P[n].shape[-1] != width:
                continue
            g = d_final if n == "final_norm" else stack_layers(n).reshape(-1, width)
            layout[n] = (widths.index(width), r0, g.shape[0])
            parts.append(_pad_rows(g, -(-g.shape[0] // 8) * 8))
            r0 += parts[-1].shape[0]
        if width == N_DEV * cw:
            conv_g = jnp.concatenate([stack_layers("sconv_w"), stack_layers("cconv_w")], axis=1)
            conv_g = conv_g.reshape(L * (SCONV_K + CCONV_K), N_DEV * cw)
            conv_at.update(b=widths.index(width), r0=r0, rows=conv_g.shape[0])
            parts.append(_pad_rows(conv_g, -(-conv_g.shape[0] // 8) * 8))
        return jnp.concatenate(parts, axis=0)

    gathered_small = [None] * len(widths)
    for l in reversed(range(L)):
        sv = saved[l]
        sg = small_g[l]
        for tag in ("ffn2", None, "ffn1"):
            if tag is None:
                keys, comm = rest_of_sends()
                (dup, dproj, dy), extra = _merge_bwd(dx, sv["y"], sv["proj"], W["w_up", l], W["w_out", l],
                                                     "merge_bwd", comm)
                settle(keys, extra)
                g_out, _ = _matmul_tn(sv["merged"][None], dx[None], 1, "w_out_grad")
                g_up, _ = _matmul_tn(sv["y"], dup, 1, "w_up_grad")
                g_out = g_out.reshape(N_DEV, D // N_DEV, D)
                g_up = jnp.transpose(g_up.reshape(4, BW, N_DEV, D // N_DEV), (2, 0, 1, 3)).reshape(
                    N_DEV, 4 * BW, D // N_DEV)
                res, (R["w_out", l], R["w_up", l]) = _mixers_bwd(
                    sv["proj"], sv["y1"], dy, dproj, pool_w[l], pool_wt[l], row(pool_scale, l), sconv_full[l], cconv_full[l],
                    row(cconv_ln_g, l), row(cconv_ln_b, l), row(sgu_ln_g, l), row(sgu_ln_b, l), sgu_w[l], sgu_wt[l],
                    sgu_bias[l], "mixers_bwd", _scatter_comm([g_out, g_up]))
                dproj = res[0]
                (sg["pool_w"], sg["pool_scale"], sg["sconv_w"], sg["cconv_w"], sg["cconv_ln_g"], sg["cconv_ln_b"],
                 sg["sgu_ln_g"], sg["sgu_ln_b"], sg["sgu_w"], dgb) = res[1:]
                sg["sgu_b"] = dgb[:, :, 0]
                comm = None
                if l == 0:
                    early = [w for w in widths if w != D]
                    comm = _gather_comm([pack(w) for w in early], [(b, None) for b in range(len(early))])
                g_in, extra = _matmul_tn(sv["h_mix"][None], dproj, N_DEV, "w_in_grad", comm)
                if l == 0:
                    for w, g in zip(early, extra):
                        gathered_small[widths.index(w)] = g
                dx, sg["mix_norm"], (r_in,) = _matmul_nt_normbwd(
                    dproj, W["w_in", l], sv["x_mix"], row(mix_norm, l), dx, "proj_bwd",
                    _scatter_comm([g_in], PEERS_SAME_CORE))
                second.append((("w_in", l), g_in, r_in, (PEERS_NEAR_OTHER, PEERS_FAR_OTHER)))
            else:
                keys, comm = rest_of_sends()
                dab, sh, extra = _ffn_bwd_hidden(dx, W[tag + "_w2", l], sv["ab_" + tag], "ffn_hidden_bwd", comm)
                settle(keys, extra)
                keys, comm = rest_of_sends()
                g_w2, extra = _matmul_tn(sh[None], dx[None], 1, "ffn_w2_grad", comm)
                settle(keys, extra)
                g_w2 = g_w2.reshape(N_DEV, F // N_DEV, D)
                g_w13, (R[tag + "_w2", l],) = _matmul_tn(dab, sv["h_" + tag][None], 1, "ffn_w13_grad",
                                                         _scatter_comm([g_w2]), b_shared=True)
                g_w13 = g_w13.reshape(N_DEV, fs, D)
                last = tag == "ffn1" and l == 0
                now, later = (PEERS_BUT_NEAR_OTHER, PEERS_NEAR_OTHER) if last else (PEERS_SAME_CORE, PEERS_OTHER_CORE)
                dx, sg[tag + "_norm"], (r_w13,) = _matmul_nt_normbwd(
                    dab, W[tag + "_w13", l], sv["x_" + tag], row(P[tag + "_norm"], l), dx, "ffn_up_bwd",
                    _scatter_comm([g_w13], now), w_t=True)
                second.append(((tag + "_w13", l), g_w13, r_w13, (later,)))
    grad_x = dx[None]
    out = {}

    def as2d(n, a):
        if n == "final_norm":
            return a.reshape(1, D)
        return a.reshape(-1, a.shape[-1])

    for i, n in enumerate(["w_out", "w_up", "ffn2_w13", "w_in", "ffn2_w2", "ffn1_w2", "ffn1_w13"]):
        shp = P[n].shape
        if n.endswith("w13"):
            flat, back = (lambda a: jnp.swapaxes(a, 1, 2)), (lambda a: jnp.swapaxes(a, 1, 2))
        else:
            rows, cols = math.prod(shp[1:-1]), shp[-1]
            flat, back = (lambda a: a.reshape(L, rows, cols)), (lambda a: a.reshape(shp))
        keys, comm = [], None
        if i == 0:
            keys, sends = rest_of_sends()
            comm = _join_comm(sends, _gather_comm([pack(D)], [(0, None)]))
        res, extra = _adamw_sharded([R[n, l] for l in range(L)], flat(P[n]), flat(P["m_" + n]), flat(P["v_" + n]),
                                    "adamw_sharded", comm)
        if i == 0:
            settle(keys, extra[:len(keys)])
            gathered_small[widths.index(D)] = extra[len(keys)]
        out[n] = tuple(back(a) for a in res)

    res = _adamw_replicated(gathered_small, [layout[n] for n in small_names],
                            [(as2d(n, P[n]), as2d(n, P["m_" + n]), as2d(n, P["v_" + n])) for n in small_names],
                            "adamw_replicated")
    for p, n in enumerate(small_names):
        out[n] = tuple(a.reshape(P[n].shape) for a in res[4 * p:4 * p + 4])
    conv_sum = res[4 * len(small_names) + conv_at["b"]][conv_at["r0"]:conv_at["r0"] + conv_at["rows"]]
    conv_mine = lax.dynamic_slice_in_dim(conv_sum, me * cw, cw, axis=1)

    def conv2d(a, b):
        return jnp.concatenate([a, b], axis=1).reshape(L * (SCONV_K + CCONV_K), cw)

    cd, cm, cv = _adamw_small(conv_mine, conv2d(sconv_w, cconv_w), conv2d(m_sconv_w, m_cconv_w),
                              conv2d(v_sconv_w, v_cconv_w), "adamw_conv")
    for n, sl in (("sconv_w", slice(0, SCONV_K)), ("cconv_w", slice(SCONV_K, SCONV_K + CCONV_K))):
        out[n] = tuple(a.reshape(L, SCONV_K + CCONV_K, cw)[:, sl] for a in (conv_mine, cd, cm, cv))

    order = ["ffn1_norm", "ffn1_w13", "ffn1_w2", "mix_norm", "w_in", "pool_w", "pool_scale", "sconv_w", "cconv_w",
             "cconv_ln_g", "cconv_ln_b", "sgu_ln_g", "sgu_ln_b", "sgu_w", "sgu_b", "w_up", "w_out", "ffn2_norm",
             "ffn2_w13", "ffn2_w2", "final_norm"]
    return (loss, grad_x, *[out[n][0] for n in order], *[out[n][1] for n in order],
            *[out[n][2] for n in order], *[out[n][3] for n in order])
```

```python
import math

import jax
import jax.numpy as jnp
from jax import lax
from jax.experimental import pallas as pl
from jax.experimental.pallas import tpu as pltpu

F32 = jnp.float32
BF16 = jnp.bfloat16
EPS = 1e-6
ADAM_LR = 0.001
ADAM_B1 = 0.9
ADAM_B2 = 0.999
ADAM_EPS = 1e-08
ADAM_WD = 0.01
ADAM_STEP = 10
SGU_BLOCK = 128
SGU_CHUNK = 64
SCONV_K = 3
CCONV_K = 31
HALO = 32
V7X_VMEM_LIMIT = 48 * 1024 * 1024
LANES = 128
BF16_ROWS = 16
WIDE = 11 * LANES
ROWS_L, ROWS_M, ROWS_S = 1024, 512, 256
ELEMENTWISE_ROWS = 32
MESH_AXES = ("x", "y", "c")
N_DEV = 8
_GELU_C0 = math.sqrt(2.0 / math.pi)
_GELU_C1 = 0.044715

BS = pl.BlockSpec
SDS = jax.ShapeDtypeStruct
ANY = pl.BlockSpec(memory_space=pl.ANY)


def _tile(n, pref, align=128):
    if n <= pref:
        return n
    t = pref - pref % align
    while t > 0:
        if n % t == 0:
            return t
        t -= align
    return n


def _sig(v):
    return 1.0 / (1.0 + jnp.exp(-v))


def _gelu(v):
    t = jnp.tanh(_GELU_C0 * (v + _GELU_C1 * (v * v * v)))
    return 0.5 * v * (1.0 + t), t


def _gelu_grad(v, t):
    return 0.5 * (1.0 + t) + 0.5 * v * (1.0 - t * t) * (_GELU_C0 * (1.0 + 3.0 * _GELU_C1 * v * v))


def _ln_stats(v):
    mu = jnp.mean(v, axis=-1, keepdims=True)
    vc = v - mu
    var = jnp.mean(vc * vc, axis=-1, keepdims=True)
    rstd = lax.rsqrt(var + EPS)
    return vc * rstd, rstd


def _ln_bwd(dvh, vh, rstd):
    return rstd * (dvh - jnp.mean(dvh, axis=-1, keepdims=True) - vh * jnp.mean(dvh * vh, axis=-1, keepdims=True))


def _dot(a, b):
    return jnp.dot(a, b, preferred_element_type=F32)


def _dot_nt(a, b):
    return lax.dot_general(a, b, (((1,), (1,)), ((), ())), preferred_element_type=F32)


def _dot_tn(a, b):
    return lax.dot_general(a, b, (((0,), (0,)), ((), ())), preferred_element_type=F32)


def _mesh_pos():
    return lax.axis_index("x"), lax.axis_index("y"), lax.axis_index("c")


class _Comm:
    def __init__(self, ins, out_shape, sems, start, finish, aliases=None, middle=None):
        self.ins, self.out_shape, self.sems, self.start, self.finish = ins, out_shape, sems, start, finish
        self.aliases = aliases or {}
        self.middle = middle


def _gather_comm(shards, units):
    n_u = len(units)
    out_shape = []
    for t, l in units:
        shp = shards[t].shape if l is None else shards[t].shape[1:]
        out_shape.append(SDS((N_DEV,) + tuple(shp), shards[t].dtype))

    def upper_rows(o):
        shp = out_shape[o].shape[1:]
        assert shp[0] >= 2
        return shp[0] // 2 if len(shp) > 2 or shp[0] < 32 else shp[0] // 32 * 16

    def tools(ins, dsts, sems):
        send_sems, recv_sems, local_sems = sems
        x, y, c = _mesh_pos()
        me, sib = (x, y, c), (x, y, 1 - c)
        xn, yn, dg = (1 - x, y, c), (x, 1 - y, c), (1 - x, 1 - y, c)

        def src_of(o):
            t, l = units[o]
            return ins[t] if l is None else ins[t].at[l]

        def row(o, p, part=None):
            r = dsts[o].at[4 * p[0] + 2 * p[1] + p[2]]
            if part is None:
                return r
            h = upper_rows(o)
            return r.at[pl.ds(0, h)] if part == "upper" else r.at[pl.ds(h, out_shape[o].shape[1] - h)]

        def copy(o, k, src, dst, to):
            return pltpu.make_async_remote_copy(
                src_ref=src, dst_ref=dst, send_sem=send_sems.at[o * 8 + k], recv_sem=recv_sems.at[o * 8 + k],
                device_id=to, device_id_type=pl.DeviceIdType.MESH)

        def send(o, k):
            if k < 3:
                return copy(o, k, src_of(o), row(o, me), (sib, xn, yn)[k])
            if k == 3:
                return copy(o, k, row(o, xn, "upper"), row(o, xn, "upper"), yn)
            if k == 4:
                return copy(o, k, row(o, yn, "lower"), row(o, yn, "lower"), xn)
            blk = (xn, yn, dg)[k - 5]
            return copy(o, k, row(o, blk), row(o, blk), sib)

        def landed(o, k):
            def other(p):
                return (p[0], p[1], 1 - c)

            dst = (row(o, sib), row(o, xn), row(o, yn), row(o, dg, "upper"), row(o, dg, "lower"),
                   row(o, other(xn)), row(o, other(yn)), row(o, other(dg)))[k]
            return copy(o, k, dst, dst, me)

        def local(o):
            return pltpu.make_async_copy(src_of(o), row(o, me), local_sems.at[o])

        return send, landed, local

    def start(ins, dsts, sems):
        send, _, local = tools(ins, dsts, sems)
        for o in range(n_u):
            local(o).start()
            for k in (1, 2, 0):
                send(o, k).start()

    def middle(ins, dsts, sems):
        send, landed, _ = tools(ins, dsts, sems)
        for o in range(n_u):
            landed(o, 1).wait_recv()
            send(o, 3).start()
            landed(o, 2).wait_recv()
            send(o, 4).start()
            send(o, 5).start()
            send(o, 6).start()

    def finish(ins, dsts, sems):
        send, landed, local = tools(ins, dsts, sems)
        for o in range(n_u):
            landed(o, 3).wait_recv()
            landed(o, 4).wait_recv()
            send(o, 7).start()
        for o in range(n_u):
            for k in (0, 5, 6, 7):
                landed(o, k).wait_recv()
        for o in range(n_u):
            for k in range(8):
                send(o, k).wait_send()
            local(o).wait()

    sems = [pltpu.SemaphoreType.DMA((8 * n_u,)), pltpu.SemaphoreType.DMA((8 * n_u,)), pltpu.SemaphoreType.DMA((n_u,))]
    return _Comm(list(shards), out_shape, sems, start, finish, middle=middle)


PEERS_ALL = (1, 2, 3, 4, 5, 6, 7)
PEERS_SAME_CORE = (1, 2, 4, 6)
PEERS_OTHER_CORE = (3, 5, 7)
PEERS_BUT_NEAR_OTHER = (1, 2, 4, 6, 7)
PEERS_NEAR_OTHER = (3, 5)
PEERS_FAR_OTHER = (7,)


def _scatter_comm(parts, peers=PEERS_ALL, into=None):
    n_u = len(parts)

    def tools(ins, dsts, sems):
        send_sems, recv_sems, local_sems = sems
        x, y, c = _mesh_pos()
        me = 4 * x + 2 * y + c

        def peer(k):
            return ((x + ((k >> 2) & 1)) % 2, (y + ((k >> 1) & 1)) % 2, (c + (k & 1)) % 2)

        def copy(u, k, wait=False):
            p = peer(k)
            pi = 4 * p[0] + 2 * p[1] + p[2]
            return pltpu.make_async_remote_copy(
                src_ref=ins[u].at[pi], dst_ref=dsts[u].at[pi if wait else me],
                send_sem=send_sems.at[u * 7 + k - 1], recv_sem=recv_sems.at[u * 7 + k - 1],
                device_id=p, device_id_type=pl.DeviceIdType.MESH)

        def local(u):
            return pltpu.make_async_copy(ins[u].at[me], dsts[u].at[me], local_sems.at[u])

        return copy, local

    def start(ins, dsts, sems):
        copy, local = tools(ins, dsts, sems)
        for u in range(n_u):
            for k in sorted(peers, key=lambda k: k == 1):
                copy(u, k).start()
            if into is None:
                local(u).start()

    def finish(ins, dsts, sems):
        copy, local = tools(ins, dsts, sems)
        for u in range(n_u):
            for k in peers:
                copy(u, k, wait=True).wait()
            if into is None:
                local(u).wait()

    sems = [pltpu.SemaphoreType.DMA((7 * n_u,)), pltpu.SemaphoreType.DMA((7 * n_u,)), pltpu.SemaphoreType.DMA((n_u,))]
    aliases = {} if into is None else {n_u + u: u for u in range(n_u)}
    return _Comm(list(parts) + list(into or []), [SDS(p.shape, p.dtype) for p in parts], sems, start, finish, aliases)


def _join_comm(a, b):
    n_i, n_o, n_s = len(a.ins), len(a.out_shape), len(a.sems)

    def both(stage):
        def run(ins, outs, sems):
            for comm, part in ((a, (ins[:n_i], outs[:n_o], sems[:n_s])), (b, (ins[n_i:], outs[n_o:], sems[n_s:]))):
                if getattr(comm, stage) is not None:
                    getattr(comm, stage)(*part)
        return run

    aliases = {**a.aliases, **{n_i + ci: n_o + co for ci, co in b.aliases.items()}}
    return _Comm(a.ins + b.ins, a.out_shape + b.out_shape, a.sems + b.sems, both("start"), both("finish"), aliases,
                 both("middle"))


def _pcall(body, name, grid, in_specs, out_specs, out_shape, scratch, sem, args, comm=None, aliases=None):
    n_i, n_o, n_s = len(in_specs), len(out_specs), len(scratch)
    aliases = aliases or {}
    if comm is None:
        res = pl.pallas_call(
            body, name=name, grid=grid, in_specs=in_specs, out_specs=out_specs, out_shape=out_shape,
            scratch_shapes=scratch, input_output_aliases=aliases,
            compiler_params=pltpu.CompilerParams(dimension_semantics=sem, vmem_limit_bytes=V7X_VMEM_LIMIT),
        )(*args)
        return res, []
    n_ci, n_co = len(comm.ins), len(comm.out_shape)

    def wrapped(*refs):
        ins = refs[:n_i]
        cins = refs[n_i:n_i + n_ci]
        outs = refs[n_i + n_ci:n_i + n_ci + n_o]
        couts = refs[n_i + n_ci + n_o:n_i + n_ci + n_o + n_co]
        rest = refs[n_i + n_ci + n_o + n_co:]
        step = 0
        for d, g in enumerate(grid):
            step = step * g + pl.program_id(d)
        n_steps = math.prod(grid)
        mid = (n_steps * 5) // 8
        staged = comm.middle is not None and 0 < mid < n_steps - 1

        @pl.when(step == 0)
        def _():
            comm.start(cins, couts, rest[n_s:])

        if staged:
            @pl.when(step == mid)
            def _():
                comm.middle(cins, couts, rest[n_s:])

        body(*ins, *outs, *rest[:n_s])

        @pl.when(step == n_steps - 1)
        def _():
            if comm.middle is not None and not staged:
                comm.middle(cins, couts, rest[n_s:])
            comm.finish(cins, couts, rest[n_s:])

    res = pl.pallas_call(
        wrapped, name=name, grid=grid, in_specs=list(in_specs) + [ANY] * n_ci,
        out_specs=list(out_specs) + [ANY] * n_co, out_shape=list(out_shape) + list(comm.out_shape),
        scratch_shapes=list(scratch) + list(comm.sems),
        input_output_aliases={**aliases, **{n_i + ci: n_o + co for ci, co in comm.aliases.items()}},
        compiler_params=pltpu.CompilerParams(dimension_semantics=("arbitrary",) * len(grid),
                                             vmem_limit_bytes=V7X_VMEM_LIMIT),
    )(*args, *comm.ins)
    return res[:n_o], res[n_o:]


def _rmsnorm_fwd(x, g, name, comm=None):
    S, D = x.shape
    tm = _tile(S, ROWS_M, BF16_ROWS)

    def body(x_ref, g_ref, h_ref):
        h_ref[...] = _rmsnorm_rows(x_ref[...], g_ref[...])

    (h,), extra = _pcall(body, name, (S // tm,), [BS((tm, D), lambda i: (i, 0)), BS((1, D), lambda i: (0, 0))],
                         [BS((tm, D), lambda i: (i, 0))], [SDS((S, D), BF16)], [], ("parallel",), (x, g), comm)
    return h, extra


def _matmul_fwd(a, w, name, comm=None, w_t=False, out_dtype=F32):
    S, K = a.shape
    C = w.shape[0]
    Fc = w.shape[1] if w_t else w.shape[2]
    tn = _tile(Fc, WIDE)
    tm = _tile(S, 2 * ROWS_L, BF16_ROWS)

    def body(a_ref, w_ref, o_ref):
        p = _dot_nt(a_ref[...], w_ref[0]) if w_t else _dot(a_ref[...], w_ref[0])
        o_ref[0] = p.astype(out_dtype)

    w_spec = BS((1, tn, K), lambda c, n, i: (c, n, 0)) if w_t else BS((1, K, tn), lambda c, n, i: (c, 0, n))
    (o,), extra = _pcall(
        body, name, (C, Fc // tn, S // tm), [BS((tm, K), lambda c, n, i: (i, 0)), w_spec],
        [BS((1, tm, tn), lambda c, n, i: (c, i, n))], [SDS((C, S, Fc), out_dtype)], [],
        ("parallel", "parallel", "parallel"), (a, w), comm)
    return o, extra


def _rmsnorm_rows(xv, g):
    r = lax.rsqrt(jnp.mean(xv * xv, axis=-1, keepdims=True) + EPS)
    return (xv * r * g).astype(BF16)


def _swiglu_down(ab, w2, x, g_next, name, comm=None):
    _, S, F = ab.shape
    D = w2.shape[1]
    tk = _tile(F, WIDE)
    tm = _tile(S, ROWS_M, BF16_ROWS)
    nk = F // tk

    te = _tile(tm, ELEMENTWISE_ROWS, BF16_ROWS)

    def body(ab_ref, w_ref, x_ref, g_ref, o_ref, h_ref, s_ref):
        k = pl.program_id(1)
        for r0 in range(0, tm, te):
            rows = slice(r0, r0 + te)
            a = ab_ref[0, rows, :].astype(F32)
            s_ref[rows, :] = (a * _sig(a) * ab_ref[1, rows, :].astype(F32)).astype(BF16)
        p = 0.5 * _dot(s_ref[...], w_ref[...])

        @pl.when(k == 0)
        def _():
            o_ref[...] = x_ref[...] + p

        @pl.when(k > 0)
        def _():
            o_ref[...] += p

        @pl.when(k == nk - 1)
        def _():
            h_ref[...] = _rmsnorm_rows(o_ref[...], g_ref[...])

    (o, h), extra = _pcall(
        body, name, (S // tm, nk),
        [BS((2, tm, tk), lambda i, k: (0, i, k)), BS((tk, D), lambda i, k: (k, 0)), BS((tm, D), lambda i, k: (i, 0)),
         BS((1, D), lambda i, k: (0, 0))],
        [BS((tm, D), lambda i, k: (i, 0)), BS((tm, D), lambda i, k: (i, 0))],
        [SDS((S, D), F32), SDS((S, D), BF16)], [pltpu.VMEM((tm, tk), BF16)], ("parallel", "arbitrary"),
        (ab, w2, x, g_next), comm)
    return o, h, extra


def _ffn_bwd_hidden(dy, w2, ab, name, comm=None):
    S, D = dy.shape
    F = w2.shape[0]
    tk = _tile(F, WIDE)
    tm = _tile(S, ROWS_M, BF16_ROWS)
    te = _tile(tm, ELEMENTWISE_ROWS, BF16_ROWS)

    def body(dy_ref, w_ref, ab_ref, dab_ref, s_ref, ds_ref):
        ds_ref[...] = 0.5 * _dot_nt(dy_ref[...].astype(BF16), w_ref[...])
        for r0 in range(0, tm, te):
            rows = slice(r0, r0 + te)
            ds = ds_ref[rows, :]
            a = ab_ref[0, rows, :].astype(F32)
            b = ab_ref[1, rows, :].astype(F32)
            sg = _sig(a)
            sa = a * sg
            dab_ref[0, rows, :] = (ds * b * (sg * (1.0 + a * (1.0 - sg)))).astype(BF16)
            dab_ref[1, rows, :] = (ds * sa).astype(BF16)
            s_ref[rows, :] = (0.5 * (sa * b)).astype(BF16)

    (dab, sh), extra = _pcall(
        body, name, (F // tk, S // tm),
        [BS((tm, D), lambda k, i: (i, 0)), BS((tk, D), lambda k, i: (k, 0)), BS((2, tm, tk), lambda k, i: (0, i, k))],
        [BS((2, tm, tk), lambda k, i: (0, i, k)), BS((tm, tk), lambda k, i: (i, k))],
        [SDS((2, S, F), BF16), SDS((S, F), BF16)], [pltpu.VMEM((tm, tk), F32)], ("parallel", "parallel"),
        (dy, w2, ab), comm)
    return dab, sh, extra


def _matmul_tn(a, b, n_c, name, comm=None, b_shared=False):
    G, S, M = a.shape
    _, _, Fc = b.shape
    C = n_c
    tM = _tile(M, WIDE)
    tn = _tile(Fc, WIDE)
    ts = _tile(S, 2 * ROWS_L if b.dtype == BF16 else ROWS_L, BF16_ROWS)
    n_s = S // ts

    def body(a_ref, b_ref, o_ref, acc):
        s = pl.program_id(4)
        p = _dot_tn(a_ref[0].astype(BF16), b_ref[0].astype(BF16))

        @pl.when(s == 0)
        def _():
            acc[...] = p

        @pl.when(s > 0)
        def _():
            acc[...] += p

        @pl.when(s == n_s - 1)
        def _():
            o_ref[0] = acc[...].astype(BF16)

    (o,), extra = _pcall(
        body, name, (G, M // tM, C, Fc // tn, n_s),
        [BS((1, ts, tM), lambda g, m, c, n, s: (g, s, m)), BS((1, ts, tn), lambda g, m, c, n, s: (c if b_shared else g * C + c, s, n))],
        [BS((1, tM, tn), lambda g, m, c, n, s: (g * C + c, m, n))], [SDS((G * C, M, Fc), BF16)],
        [pltpu.VMEM((tM, tn), F32)], ("parallel", "parallel", "parallel", "parallel", "arbitrary"), (a, b), comm)
    return o, extra


def _matmul_nt_normbwd(b, w, x, gam, dres, name, comm=None, w_t=False):
    C, S, Fc = b.shape
    D = w.shape[2] if w_t else w.shape[1]
    tk = _tile(Fc, WIDE)
    tm = _tile(S, ROWS_L, BF16_ROWS)
    te = _tile(tm, 256, 8)
    nk = Fc // tk

    def body(b_ref, w_ref, x_ref, g_ref, r_ref, dx_ref, dg_ref):
        i, c, k = pl.program_id(0), pl.program_id(1), pl.program_id(2)
        p = _dot(b_ref[0], w_ref[0]) if w_t else _dot_nt(b_ref[0], w_ref[0])
        first = jnp.logical_and(c == 0, k == 0)

        @pl.when(first)
        def _():
            dx_ref[...] = p

        @pl.when(jnp.logical_not(first))
        def _():
            dx_ref[...] += p

        @pl.when(jnp.logical_and(c == C - 1, k == nk - 1))
        def _():
            dgp = None
            for r0 in range(0, tm, te):
                rows = slice(r0, r0 + te)
                xv = x_ref[rows, :]
                r = lax.rsqrt(jnp.mean(xv * xv, axis=-1, keepdims=True) + EPS)
                xn = xv * r
                dh = dx_ref[rows, :]
                dxn = dh * g_ref[...]
                dx_ref[rows, :] = r_ref[rows, :] + r * (dxn - xn * jnp.mean(dxn * xn, axis=-1, keepdims=True))
                t = jnp.sum(dh * xn, axis=0, keepdims=True)
                dgp = t if dgp is None else dgp + t

            @pl.when(i == 0)
            def _():
                dg_ref[...] = dgp

            @pl.when(i > 0)
            def _():
                dg_ref[...] += dgp

    once = dict(pipeline_mode=pl.Buffered(1))
    (dx, dg), extra = _pcall(
        body, name, (S // tm, C, nk),
        [BS((1, tm, tk), lambda i, c, k: (c, i, k)),
         BS((1, tk, D), lambda i, c, k: (c, k, 0)) if w_t else BS((1, D, tk), lambda i, c, k: (c, 0, k)),
         BS((tm, D), lambda i, c, k: (i, 0), **once), BS((1, D), lambda i, c, k: (0, 0)),
         BS((tm, D), lambda i, c, k: (i, 0), **once)],
        [BS((tm, D), lambda i, c, k: (i, 0)), BS((1, D), lambda i, c, k: (0, 0))],
        [SDS((S, D), F32), SDS((1, D), F32)], [],
        ("arbitrary", "arbitrary", "arbitrary"), (b, w, x, gam, dres), comm)
    return dx, dg, extra


def _final_loss(x, gam, target, name):
    S, D = x.shape
    tm = _tile(S, 512, 8)

    def body(x_ref, g_ref, t_ref, loss_ref, dx_ref, dg_ref):
        i = pl.program_id(0)
        xv = x_ref[...]
        r = lax.rsqrt(jnp.mean(xv * xv, axis=-1, keepdims=True) + EPS)
        xn = xv * r
        err = xn * g_ref[...] - t_ref[...]
        part = 0.5 * jnp.sum(jnp.mean(err * err, axis=-1, keepdims=True), axis=0, keepdims=True)
        dy = err * (1.0 / D)
        dxn = dy * g_ref[...]
        dx_ref[...] = r * (dxn - xn * jnp.mean(dxn * xn, axis=-1, keepdims=True))
        dgp = jnp.sum(dy * xn, axis=0, keepdims=True)
        lp = jnp.broadcast_to(part, loss_ref.shape)

        @pl.when(i == 0)
        def _():
            dg_ref[...] = dgp
            loss_ref[...] = lp

        @pl.when(i > 0)
        def _():
            dg_ref[...] += dgp
            loss_ref[...] += lp

    res, _ = _pcall(
        body, name, (S // tm,),
        [BS((tm, D), lambda i: (i, 0)), BS((1, D), lambda i: (0, 0)), BS((tm, D), lambda i: (i, 0))],
        [BS((8, 128), lambda i: (0, 0)), BS((tm, D), lambda i: (i, 0)), BS((1, D), lambda i: (0, 0))],
        [SDS((8, 128), F32), SDS((S, D), F32), SDS((1, D), F32)], [], ("arbitrary",), (x, gam, target))
    return res


CONV_CHUNK = 32


def _fill_shifted(rot, n):
    for b in range(1, 8):
        rot[b, 0:n - 8, :] = rot[0, b:b + n - 8, :]


def _window(rot, off, r0, rows):
    b = off % 8
    return rot[b, off - b + r0:off - b + r0 + rows, :]


def _taps(rot, w_ref, offs, n_rows, out):
    for r0 in range(0, n_rows, CONV_CHUNK):
        acc = None
        for k, off in enumerate(offs):
            t = w_ref[k:k + 1, :] * _window(rot, off, r0, CONV_CHUNK)
            acc = t if acc is None else acc + t
        out[r0:r0 + CONV_CHUNK, :] = acc


def _tap_grads(rot, offs, g_plane, n_rows, dw_ref):
    for k, off in enumerate(offs):
        acc = None
        for r0 in range(0, n_rows, CONV_CHUNK):
            p = g_plane[0, r0:r0 + CONV_CHUNK, :] * _window(rot, off, r0, CONV_CHUNK)
            acc = p if acc is None else acc + p
        dw_ref[k:k + 1, :] += jnp.sum(acc, axis=0, keepdims=True)


def _sgu_masks():
    ii = lax.broadcasted_iota(jnp.int32, (SGU_BLOCK, SGU_BLOCK), 0) // SGU_CHUNK
    jj = lax.broadcasted_iota(jnp.int32, (SGU_BLOCK, SGU_BLOCK), 1) // SGU_CHUNK
    return jj <= ii, ii <= jj


def _mixers_fwd(proj, pool_w, pool_scale, sconv_w, cconv_w, cln_g, cln_b, sln_g, sln_b, sgu_w, sgu_bias, name,
                comm=None):
    _, S, D = proj.shape
    BW = D // 2
    GW = BW // 4
    TS = _tile(S, ROWS_S, SGU_BLOCK)
    H = HALO
    hb = TS // H

    def main(blk, col):
        return BS((1, TS, BW), lambda i: (blk, i, col))

    def back(blk, col):
        return BS((1, H, BW), lambda i: (blk, jnp.maximum(i * hb - 1, 0), col))

    def full(a):
        nd = a.ndim
        return BS(a.shape, lambda i: (0,) * nd)

    def body(pa_m, pa_b, xi_m, xi_b, bg_m, cg_m, cg_b, ca_m, ca_b, cb_m, cb_b, du_m, dv_m,
             pw, ps, sw, cw, clg, clb, slg, slb, gw, gbias, y_ref, y1_ref, e1, e2, e3):
        i = pl.program_id(0)
        nb = jnp.where(i > 0, 1.0, 0.0).astype(F32)
        rows = i * TS + lax.broadcasted_iota(jnp.int32, (TS, 1), 0)

        e1[0:H, :] = pa_b[0] * nb
        e1[H:H + TS, :] = pa_m[0]
        for g in range(4):
            cols = slice(g * GW, (g + 1) * GW)
            win = 2 << g
            wsum = e1[H:H + TS, cols]
            for k in range(1, win):
                wsum = wsum + e1[H - k:H - k + TS, cols]
            cnt = jnp.minimum(rows + 1, win).astype(F32)
            d = wsum / cnt - e1[H:H + TS, cols]
            yg = _dot(d.astype(BF16), pw[g].astype(BF16)) * ps[:, cols]
            y_ref[0, :, cols] = yg.astype(BF16)

        e2[0:H, :] = cg_b[0] * xi_b[0] * nb
        e2[H:H + TS, :] = cg_m[0] * xi_m[0]
        cz = sw[0:1, :] * e2[H - 2:H - 2 + TS, :]
        for k in range(1, SCONV_K):
            cz = cz + sw[k:k + 1, :] * e2[H - 2 + k:H - 2 + k + TS, :]
        y_ref[1] = (bg_m[0] * cz).astype(BF16)

        e3[0, 0:H, :] = ca_b[0] * _sig(cb_b[0]) * nb
        e3[0, H:H + TS, :] = ca_m[0] * _sig(cb_m[0])
        _fill_shifted(e3, H + TS)
        _taps(e3, cw, [H - (CCONV_K - 1) + k for k in range(CCONV_K)], TS, y1_ref)
        yh, _ = _ln_stats(y1_ref[...])
        y2 = yh * clg[...] + clb[...]
        y_ref[2] = (y2 * _sig(y2)).astype(BF16)

        u, _ = _gelu(du_m[0])
        v, _ = _gelu(dv_m[0])
        vh, _ = _ln_stats(v)
        vn = vh * slg[...] + slb[...]
        mask, _ = _sgu_masks()
        for h in range(4):
            wm = jnp.where(mask, gw[h], 0.0).astype(BF16)
            cs = slice(h * GW, (h + 1) * GW)
            for n in range(TS // SGU_BLOCK):
                rs = slice(n * SGU_BLOCK, (n + 1) * SGU_BLOCK)
                z = _dot(wm, vn[rs, cs].astype(BF16)) + gbias[h]
                y_ref[3, rs, cs] = (u[rs, cs] * z).astype(BF16)

    args = [proj] * 13 + [pool_w, pool_scale, sconv_w, cconv_w, cln_g, cln_b, sln_g, sln_b, sgu_w, sgu_bias]
    in_specs = [main(0, 0), back(0, 0), main(0, 1), back(0, 1), main(1, 0), main(1, 1), back(1, 1),
                main(2, 0), back(2, 0), main(2, 1), back(2, 1), main(3, 0), main(3, 1)]
    in_specs += [full(a) for a in args[13:]]
    (y, y1), extra = _pcall(body, name, (S // TS,), in_specs,
                            [BS((4, TS, BW), lambda i: (0, i, 0)), BS((TS, BW), lambda i: (i, 0))],
                            [SDS((4, S, BW), BF16), SDS((S, BW), F32)],
                            [pltpu.VMEM((H + TS, BW), F32)] * 2 + [pltpu.VMEM((8, H + TS, BW), F32)], ("parallel",),
                            args, comm)
    return y, y1, extra


def _mixers_bwd(proj, y1, dy, dproj_gates, pool_w, pool_wt, pool_scale, sconv_w, cconv_w, cln_g, cln_b, sln_g, sln_b,
                sgu_w, sgu_wt, sgu_bias, name, comm=None):
    _, S, D = proj.shape
    BW = D // 2
    GW = BW // 4
    TS = _tile(S, ROWS_S, SGU_BLOCK)
    H = HALO
    hb = TS // H
    n_t = S // TS
    E = TS + H

    def main(blk, col):
        return BS((1, TS, BW), lambda i: (blk, i, col))

    def back(blk, col):
        return BS((1, H, BW), lambda i: (blk, jnp.maximum(i * hb - 1, 0), col))

    def front(blk, col):
        return BS((1, H, BW), lambda i: (blk, jnp.minimum((i + 1) * hb, S // H - 1), col))

    def full(a):
        nd = a.ndim
        return BS(a.shape, lambda i: (0,) * nd)

    def body(pa_b, pa_m, xi_b, xi_m, bg_m, bg_f, cg_b, cg_m, ca_b, ca_m, cb_b, cb_m, du_m, dv_m, y1_m, y1_f,
             dya_m, dya_f, dyb_m, dyb_f, dyc_m, dyc_f, dyd_m,
             pw, pwt, ps, sw, cw, clg, clb, slg, slb, gw, gwt, gbias, _gates_in,
             dp_ref, dpw, dps, dsw, dcw, dclg, dclb, dslg, dslb, dgw, dgb,
             e1, e2, e3, e4, e5, ra, rb):
        i = pl.program_id(0)
        nb = jnp.where(i > 0, 1.0, 0.0).astype(F32)
        nf = jnp.where(i < n_t - 1, 1.0, 0.0).astype(F32)
        rows_m = i * TS + lax.broadcasted_iota(jnp.int32, (TS, 1), 0)
        rows_e = i * TS + lax.broadcasted_iota(jnp.int32, (E, 1), 0)

        @pl.when(i == 0)
        def _():
            for r in (dpw, dps, dsw, dcw, dclg, dclb, dslg, dslb, dgw, dgb):
                r[...] = jnp.zeros(r.shape, F32)

        e1[0:H, :] = pa_b[0] * nb
        e1[H:H + TS, :] = pa_m[0]
        e2[0:TS, :] = dya_m[0] * ps[...]
        e2[TS:E, :] = dya_f[0] * ps[...] * nf
        for g in range(4):
            cols = slice(g * GW, (g + 1) * GW)
            win = 2 << g
            a_m = e1[H:H + TS, cols]
            wsum = a_m
            for k in range(1, win):
                wsum = wsum + e1[H - k:H - k + TS, cols]
            d = wsum / jnp.minimum(rows_m + 1, win).astype(F32) - a_m
            d16 = d.astype(BF16)
            dyp = e2[0:E, cols].astype(BF16)
            dd = _dot(dyp, pwt[g].astype(BF16))
            e3[0:E, cols] = dd / jnp.minimum(rows_e + 1, win).astype(F32)
            da = e3[0:TS, cols] - dd[0:TS]
            for k in range(1, win):
                da = da + e3[k:k + TS, cols]
            dp_ref[0, :, cols] = da.astype(BF16)
            ypre = _dot(d16, pw[g].astype(BF16))
            dps[:, cols] += jnp.sum(dya_m[0][:, cols] * ypre, axis=0, keepdims=True)
            dpw[g] += _dot(jnp.transpose(d).astype(BF16), dyp[0:TS])

        e4[0:H, :] = cg_b[0] * xi_b[0] * nb
        e4[H:H + TS, :] = cg_m[0] * xi_m[0]
        dyb = dyb_m[0]
        e5[0:TS, :] = dyb * bg_m[0]
        e5[TS:E, :] = dyb_f[0] * bg_f[0] * nf
        dcz = e5[0:TS, :]
        cz = None
        dz = None
        for k in range(SCONV_K):
            zk = e4[H - 2 + k:H - 2 + k + TS, :]
            wk = sw[k:k + 1, :]
            cz = wk * zk if cz is None else cz + wk * zk
            t = wk * e5[2 - k:2 - k + TS, :]
            dz = t if dz is None else dz + t
            dsw[k:k + 1, :] += jnp.sum(dcz * zk, axis=0, keepdims=True)
        dp_ref[0, :, BW:2 * BW] = (dz * cg_m[0]).astype(BF16)
        dp_ref[1, :, 0:BW] = (dyb * cz).astype(BF16)
        dp_ref[1, :, BW:2 * BW] = (dz * xi_m[0]).astype(BF16)

        sgm = _sig(cb_m[0])
        ra[0, 0:H, :] = ca_b[0] * _sig(cb_b[0]) * nb
        ra[0, H:H + TS, :] = ca_m[0] * sgm
        _fill_shifted(ra, H + TS)
        fwd_offs = [H - (CCONV_K - 1) + k for k in range(CCONV_K)]
        e4[0:TS, :] = y1_m[...]
        e4[TS:E, :] = y1_f[...]
        yh, rstd = _ln_stats(e4[0:E, :])
        y2 = yh * clg[...] + clb[...]
        s2 = _sig(y2)
        e1[0:TS, :] = dyc_m[0]
        e1[TS:E, :] = dyc_f[0] * nf
        dy2 = e1[0:E, :] * (s2 * (1.0 + y2 * (1.0 - s2)))
        dclg[...] += jnp.sum((dy2 * yh)[0:TS], axis=0, keepdims=True)
        dclb[...] += jnp.sum(dy2[0:TS], axis=0, keepdims=True)
        rb[0, 0:E, :] = _ln_bwd(dy2 * clg[...], yh, rstd)
        _fill_shifted(rb, E)
        _taps(rb, cw, [CCONV_K - 1 - k for k in range(CCONV_K)], TS, e5)
        _tap_grads(ra, fwd_offs, rb, TS, dcw)
        dy0 = e5[0:TS, :]
        dp_ref[2, :, 0:BW] = (dy0 * sgm).astype(BF16)
        dp_ref[2, :, BW:2 * BW] = (dy0 * ca_m[0] * (sgm * (1.0 - sgm))).astype(BF16)

        pu = du_m[0]
        pv = dv_m[0]
        u, tu = _gelu(pu)
        v, tv = _gelu(pv)
        vh, vr = _ln_stats(v)
        vn = vh * slg[...] + slb[...]
        dyd = dyd_m[0]
        mask, mask_t = _sgu_masks()
        for h in range(4):
            wm = jnp.where(mask, gw[h], 0.0).astype(BF16)
            wmt = jnp.where(mask_t, gwt[h], 0.0).astype(BF16)
            cs = slice(h * GW, (h + 1) * GW)
            for n in range(TS // SGU_BLOCK):
                rs = slice(n * SGU_BLOCK, (n + 1) * SGU_BLOCK)
                vb = vn[rs, cs].astype(BF16)
                z = _dot(wm, vb) + gbias[h]
                dzb = dyd[rs, cs] * u[rs, cs]
                dz16 = dzb.astype(BF16)
                e3[rs, cs] = dyd[rs, cs] * z
                e4[rs, cs] = _dot(wmt, dz16)
                dgw[h] += jnp.where(mask, _dot_nt(dz16, vb), 0.0)
                dgb[h] += dzb
        dvn = e4[0:TS, :]
        dslg[...] += jnp.sum(dvn * vh, axis=0, keepdims=True)
        dslb[...] += jnp.sum(dvn, axis=0, keepdims=True)
        dv = _ln_bwd(dvn * slg[...], vh, vr)
        dp_ref[3, :, 0:BW] = (e3[0:TS, :] * _gelu_grad(pu, tu)).astype(BF16)
        dp_ref[3, :, BW:2 * BW] = (dv * _gelu_grad(pv, tv)).astype(BF16)

        @pl.when(i == n_t - 1)
        def _():
            for h in range(4):
                dgb[h] = jnp.broadcast_to(jnp.sum(dgb[h], axis=1, keepdims=True), dgb.shape[1:])

    params = [pool_w, pool_wt, pool_scale, sconv_w, cconv_w, cln_g, cln_b, sln_g, sln_b, sgu_w, sgu_wt, sgu_bias]
    args = [proj] * 14 + [y1] * 2 + [dy] * 7 + params + [dproj_gates]
    in_specs = [back(0, 0), main(0, 0), back(0, 1), main(0, 1), main(1, 0), front(1, 0), back(1, 1), main(1, 1),
                back(2, 0), main(2, 0), back(2, 1), main(2, 1), main(3, 0), main(3, 1),
                BS((TS, BW), lambda i: (i, 0)), BS((H, BW), lambda i: (jnp.minimum((i + 1) * hb, S // H - 1), 0)),
                main(0, 0), front(0, 0), main(1, 0), front(1, 0), main(2, 0), front(2, 0), main(3, 0)]
    in_specs += [full(a) for a in params] + [ANY]
    small = [SDS(pool_w.shape, F32), SDS(pool_scale.shape, F32), SDS(sconv_w.shape, F32), SDS(cconv_w.shape, F32),
             SDS(cln_g.shape, F32), SDS(cln_b.shape, F32), SDS(sln_g.shape, F32), SDS(sln_b.shape, F32),
             SDS(sgu_w.shape, F32), SDS(sgu_bias.shape, F32)]
    out_specs = [BS((4, TS, D), lambda i: (0, i, 0))] + [full(s) for s in small]
    return _pcall(body, name, (n_t,), in_specs, out_specs, [SDS(dproj_gates.shape, BF16)] + small,
                  [pltpu.VMEM((TS + 2 * H, BW), F32)] * 5 + [pltpu.VMEM((8, TS + 2 * H, BW), F32)] * 2,
                  ("arbitrary",), args, comm, aliases={len(args) - 1: 0})


def _merge_fwd(y, proj, w_up, w_out, x, g_next, name, comm=None):
    _, S, BW = y.shape
    D = x.shape[1]
    tm = _tile(S, ROWS_M, BF16_ROWS)

    def body(y_ref, pg_ref, wu_ref, wo_ref, x_ref, g_ref, o_ref, m_ref, h_ref):
        merged = None
        for g in range(4):
            t = _sig(pg_ref[g]) * _dot(y_ref[g], wu_ref[g])
            merged = t if merged is None else merged + t
        m16 = merged.astype(BF16)
        m_ref[...] = m16
        xn = x_ref[...] + _dot(m16, wo_ref[...])
        o_ref[...] = xn
        h_ref[...] = _rmsnorm_rows(xn, g_ref[...])

    once = dict(pipeline_mode=pl.Buffered(1))
    (o, m, h), extra = _pcall(
        body, name, (S // tm,),
        [BS((4, tm, BW), lambda i: (0, i, 0)), BS((4, tm, D), lambda i: (1, i, 0)),
         BS((4, BW, D), lambda i: (0, 0, 0), **once), BS((D, D), lambda i: (0, 0), **once),
         BS((tm, D), lambda i: (i, 0)), BS((1, D), lambda i: (0, 0))],
        [BS((tm, D), lambda i: (i, 0)), BS((tm, D), lambda i: (i, 0)), BS((tm, D), lambda i: (i, 0))],
        [SDS((S, D), F32), SDS((S, D), BF16), SDS((S, D), BF16)], [], ("parallel",),
        (y, proj, w_up, w_out, x, g_next), comm)
    return o, m, h, extra


def _merge_bwd(dx, y, proj, w_up, w_out, name, comm=None):
    _, S, BW = y.shape
    D = dx.shape[1]
    tm = _tile(S, ROWS_S, BF16_ROWS)

    def body(dx_ref, y_ref, pg_ref, wu_ref, wo_ref, dup_ref, dp_ref, dy_ref):
        dm = _dot_nt(dx_ref[...].astype(BF16), wo_ref[...])
        for g in range(4):
            gate = _sig(pg_ref[g])
            up = _dot(y_ref[g], wu_ref[g])
            dup = (dm * gate).astype(BF16)
            dup_ref[g] = dup
            dp_ref[g] = (dm * up * (gate * (1.0 - gate))).astype(BF16)
            dy_ref[g] = _dot_nt(dup, wu_ref[g])

    res, extra = _pcall(
        body, name, (S // tm,),
        [BS((tm, D), lambda i: (i, 0)), BS((4, tm, BW), lambda i: (0, i, 0)), BS((4, tm, D), lambda i: (1, i, 0)),
         BS((4, BW, D), lambda i: (0, 0, 0)), BS((D, D), lambda i: (0, 0))],
        [BS((4, tm, D), lambda i: (0, i, 0)), BS((4, tm, D), lambda i: (1, i, 0)), BS((4, tm, BW), lambda i: (0, i, 0))],
        [SDS((4, S, D), BF16), SDS((8, S, D), BF16), SDS((4, S, BW), F32)], [], ("parallel",),
        (dx, y, proj, w_up, w_out), comm)
    return res, extra


def _adamw(w, g, m, v):
    m = ADAM_B1 * m + (1.0 - ADAM_B1) * g
    v = ADAM_B2 * v + (1.0 - ADAM_B2) * (g * g)
    m_hat = m / (1.0 - ADAM_B1 ** ADAM_STEP)
    v_hat = v / (1.0 - ADAM_B2 ** ADAM_STEP)
    delta = -ADAM_LR * (m_hat / (jnp.sqrt(v_hat) + ADAM_EPS) + ADAM_WD * w)
    return delta, m, v


def _adamw_sharded(parts, w, m, v, name, comm=None):
    L, R, C = w.shape
    tr = _tile(R, ROWS_S, BF16_ROWS)

    def body(*refs):
        p_refs = refs[:L]
        w_ref, m_ref, v_ref, g_out, d_out, m_out, v_out = refs[L:]
        l = pl.program_id(0)
        g = None
        for d in range(N_DEV):
            t = p_refs[0][d].astype(F32)
            for j in range(1, L):
                t = jnp.where(l == j, p_refs[j][d].astype(F32), t)
            g = t if g is None else g + t
        dl, mn, vn = _adamw(w_ref[0], g, m_ref[0], v_ref[0])
        g_out[0] = g
        d_out[0] = dl
        m_out[0] = mn
        v_out[0] = vn

    def part_spec(j):
        return BS((N_DEV, tr, C), lambda l, r: (0, jnp.where(l == j, r, 0), 0))

    blk = BS((1, tr, C), lambda l, r: (l, r, 0))
    return _pcall(body, name, (L, R // tr), [part_spec(j) for j in range(L)] + [blk, blk, blk], [blk] * 4,
                  [SDS((L, R, C), F32)] * 4, [], ("parallel", "parallel"), (*parts, w, m, v), comm)


def _adamw_replicated(gathered, layout, wmv, name):
    n_b = len(gathered)
    n_p = len(layout)

    def body(*refs):
        bufs = refs[:n_b]
        prm = refs[n_b:n_b + 3 * n_p]
        outs = refs[n_b + 3 * n_p:n_b + 7 * n_p]
        sums = refs[n_b + 7 * n_p:]
        for b in range(n_b):
            s = bufs[b][0]
            for d in range(1, N_DEV):
                s = s + bufs[b][d]
            sums[b][...] = s
        for p, (b, r0, nr) in enumerate(layout):
            g = sums[b][r0:r0 + nr, :]
            d, mn, vn = _adamw(prm[3 * p][...], g, prm[3 * p + 1][...], prm[3 * p + 2][...])
            outs[4 * p][...] = g
            outs[4 * p + 1][...] = d
            outs[4 * p + 2][...] = mn
            outs[4 * p + 3][...] = vn

    flat = [a for t in wmv for a in t]
    out_shape = []
    for (w, _, _) in wmv:
        out_shape += [SDS(w.shape, F32)] * 4
    out_shape += [SDS(g.shape[1:], F32) for g in gathered]
    return pl.pallas_call(
        body, name=name, out_shape=out_shape,
        compiler_params=pltpu.CompilerParams(vmem_limit_bytes=V7X_VMEM_LIMIT),
    )(*gathered, *flat)


def _adamw_small(g, w, m, v, name):
    def body(g_ref, w_ref, m_ref, v_ref, d_out, m_out, v_out):
        d, mn, vn = _adamw(w_ref[...], g_ref[...], m_ref[...], v_ref[...])
        d_out[...] = d
        m_out[...] = mn
        v_out[...] = vn

    return pl.pallas_call(body, name=name, out_shape=[SDS(w.shape, F32)] * 3)(g, w, m, v)


def _pad_rows(a, rows):
    return jnp.pad(a, ((0, rows - a.shape[0]), (0, 0)))


def kernel(x, ffn1_norm, ffn1_w13, ffn1_w2, mix_norm, w_in, pool_w, pool_scale, sconv_w, cconv_w, cconv_ln_g, cconv_ln_b, sgu_ln_g, sgu_ln_b, sgu_w, sgu_b, w_up, w_out, ffn2_norm, ffn2_w13, ffn2_w2, final_norm, loss_target, m_ffn1_norm, m_ffn1_w13, m_ffn1_w2, m_mix_norm, m_w_in, m_pool_w, m_pool_scale, m_sconv_w, m_cconv_w, m_cconv_ln_g, m_cconv_ln_b, m_sgu_ln_g, m_sgu_ln_b, m_sgu_w, m_sgu_b, m_w_up, m_w_out, m_ffn2_norm, m_ffn2_w13, m_ffn2_w2, m_final_norm, v_ffn1_norm, v_ffn1_w13, v_ffn1_w2, v_mix_norm, v_w_in, v_pool_w, v_pool_scale, v_sconv_w, v_cconv_w, v_cconv_ln_g, v_cconv_ln_b, v_sgu_ln_g, v_sgu_ln_b, v_sgu_w, v_sgu_b, v_w_up, v_w_out, v_ffn2_norm, v_ffn2_w13, v_ffn2_w2, v_final_norm):
    P = dict(locals())
    L = ffn1_norm.shape[0]
    S, D = x.shape[1], x.shape[2]
    BW = D // 2
    GW = BW // 4
    F = ffn1_w2.shape[1] * N_DEV
    fs = ffn1_w13.shape[2]
    cw = sconv_w.shape[2]
    me = 4 * lax.axis_index("x") + 2 * lax.axis_index("y") + lax.axis_index("c")

    big = ["ffn1_w13", "ffn1_w2", "w_in", "w_up", "w_out", "ffn2_w13", "ffn2_w2"]
    shards = [(jnp.swapaxes(P[n], 1, 2) if n.endswith("w13") else P[n]).astype(BF16) for n in big]
    conv_local = jnp.concatenate([sconv_w, cconv_w], axis=1)

    def gather_of(units):
        return _gather_comm(shards, [(big.index(n), l) for n, l in units])

    def ready(n, g):
        if n.endswith("w13"):
            return g.reshape(2, F, D)
        if n.endswith("w2"):
            return g.reshape(F, D)
        if n == "w_up":
            return jnp.transpose(g, (1, 2, 0, 3)).reshape(4, BW, D)
        if n == "w_out":
            return g.reshape(D, D)
        return g

    W = {}

    def take(units, arrays):
        for (n, l), g in zip(units, arrays):
            W[n, l] = ready(n, g)

    first_units = [("ffn1_w13", 0)]
    plan = {("ffn1_up", 0): [("ffn1_w2", 0), ("w_up", 0), ("w_out", 0)],
            ("ffn1_down", 0): [("w_in", 0)],
            ("proj", 0): [("w_in", 1)],
            ("mixers", 0): [("ffn2_w13", 0)], ("merge", 0): [("ffn2_w2", 0)],
            ("ffn2_up", 0): [("ffn1_w13", 1)], ("ffn2_down", 0): [("ffn1_w2", 1)],
            ("ffn1_up", 1): [("w_up", 1), ("w_out", 1)],
            ("proj", 1): [("ffn2_w13", 1)], ("mixers", 1): [("ffn2_w2", 1)]}
    assert L <= 2

    def carried(key):
        units = [u for u in plan.get(key, []) if u[1] < L]
        return units, (gather_of(units) if units else None)

    first = _gather_comm(shards + [conv_local], [(big.index(n), l) for n, l in first_units] + [(len(big), None)])
    h, got = _rmsnorm_fwd(x[0], ffn1_norm[0][None, :], "first_norm_fwd", first)
    take(first_units, got[:1])
    conv_full = jnp.transpose(got[1], (1, 2, 0, 3)).reshape(L, SCONV_K + CCONV_K, N_DEV * cw)
    sconv_full = conv_full[:, :SCONV_K]
    cconv_full = conv_full[:, SCONV_K:]

    sgu_bias = jnp.broadcast_to(sgu_b[:, :, :, None], sgu_b.shape + (GW,))
    pool_wt = jnp.swapaxes(pool_w, 2, 3)
    sgu_wt = jnp.swapaxes(sgu_w, 2, 3)

    def row(a, l):
        return a[l][None, :]

    saved = []
    xc = x[0]
    for l in range(L):
        sv = {}
        for tag in ("ffn1", None, "ffn2"):
            if tag is None:
                sv["x_mix"] = xc
                units, comm = carried(("proj", l))
                proj, extra = _matmul_fwd(h, W["w_in", l], "proj_fwd", comm)
                take(units, extra)
                units, comm = carried(("mixers", l))
                y, sv["y1"], extra = _mixers_fwd(
                    proj, pool_w[l], row(pool_scale, l), sconv_full[l], cconv_full[l], row(cconv_ln_g, l),
                    row(cconv_ln_b, l), row(sgu_ln_g, l), row(sgu_ln_b, l), sgu_w[l], sgu_bias[l], "mixers_fwd", comm)
                take(units, extra)
                units, comm = carried(("merge", l))
                sv.update(h_mix=h, proj=proj, y=y)
                xc, sv["merged"], h, extra = _merge_fwd(y, proj, W["w_up", l], W["w_out", l], xc, row(ffn2_norm, l),
                                                        "merge_fwd", comm)
                take(units, extra)
            else:
                sv["x_" + tag] = xc
                units, comm = carried((tag + "_up", l))
                ab, extra = _matmul_fwd(h, W[tag + "_w13", l], "ffn_up_fwd", comm, w_t=True, out_dtype=BF16)
                take(units, extra)
                units, comm = carried((tag + "_down", l))
                sv.update({"h_" + tag: h, "ab_" + tag: ab})
                if tag == "ffn1":
                    g_next = row(mix_norm, l)
                else:
                    g_next = row(ffn1_norm, l + 1) if l + 1 < L else final_norm[None, :]
                xc, h, extra = _swiglu_down(ab, W[tag + "_w2", l], xc, g_next, "ffn_down_fwd", comm)
                take(units, extra)
        saved.append(sv)

    loss_part, dx, d_final = _final_loss(xc, final_norm[None, :], loss_target[0], "loss_head")
    loss = lax.psum(loss_part[0, 0], MESH_AXES)

    R = {}
    second = []

    def rest_of_sends():
        todo = list(second)
        second.clear()
        comm = None
        if todo:
            assert len({e[3][0] for e in todo}) == 1
            comm = _scatter_comm([e[1] for e in todo], todo[0][3][0], [e[2] for e in todo])
        return todo, comm

    def settle(todo, arrays):
        for (key, g, _, stages), a in zip(todo, arrays):
            if len(stages) > 1:
                second.append((key, g, a, stages[1:]))
            else:
                R[key] = a

    wide = ["ffn1_norm", "mix_norm", "ffn2_norm", "final_norm"]
    half = ["pool_scale", "cconv_ln_g", "cconv_ln_b", "sgu_ln_g", "sgu_ln_b"]
    narrow = ["pool_w", "sgu_w", "sgu_b"]
    small_names = wide + half + narrow
    small_g = [dict() for _ in range(L)]
    widths = []
    for n in small_names:
        if P[n].shape[-1] not in widths:
            widths.append(P[n].shape[-1])
    layout, conv_at = {}, {}

    def pack(width):
        def stack_layers(n):
            return jnp.stack([small_g[l][n] for l in range(L)], axis=0)

        parts, r0 = [], 0
        for n in small_names:
            if P[n].shape[-1] != width:
                continue
            g = d_final if n == "final_norm" else stack_layers(n).reshape(-1, width)
            layout[n] = (widths.index(width), r0, g.shape[0])
            parts.append(_pad_rows(g, -(-g.shape[0] // 8) * 8))
            r0 += parts[-1].shape[0]
        if width == N_DEV * cw:
            conv_g = jnp.concatenate([stack_layers("sconv_w"), stack_layers("cconv_w")], axis=1)
            conv_g = conv_g.reshape(L * (SCONV_K + CCONV_K), N_DEV * cw)
            conv_at.update(b=widths.index(width), r0=r0, rows=conv_g.shape[0])
            parts.append(_pad_rows(conv_g, -(-conv_g.shape[0] // 8) * 8))
        return jnp.concatenate(parts, axis=0)

    gathered_small = [None] * len(widths)
    for l in reversed(range(L)):
        sv = saved[l]
        sg = small_g[l]
        for tag in ("ffn2", None, "ffn1"):
            if tag is None:
                keys, comm = rest_of_sends()
                (dup, dproj, dy), extra = _merge_bwd(dx, sv["y"], sv["proj"], W["w_up", l], W["w_out", l],
                                                     "merge_bwd", comm)
                settle(keys, extra)
                g_out, _ = _matmul_tn(sv["merged"][None], dx[None], 1, "w_out_grad")
                g_up, _ = _matmul_tn(sv["y"], dup, 1, "w_up_grad")
                g_out = g_out.reshape(N_DEV, D // N_DEV, D)
                g_up = jnp.transpose(g_up.reshape(4, BW, N_DEV, D // N_DEV), (2, 0, 1, 3)).reshape(
                    N_DEV, 4 * BW, D // N_DEV)
                res, (R["w_out", l], R["w_up", l]) = _mixers_bwd(
                    sv["proj"], sv["y1"], dy, dproj, pool_w[l], pool_wt[l], row(pool_scale, l), sconv_full[l], cconv_full[l],
                    row(cconv_ln_g, l), row(cconv_ln_b, l), row(sgu_ln_g, l), row(sgu_ln_b, l), sgu_w[l], sgu_wt[l],
                    sgu_bias[l], "mixers_bwd", _scatter_comm([g_out, g_up]))
                dproj = res[0]
                (sg["pool_w"], sg["pool_scale"], sg["sconv_w"], sg["cconv_w"], sg["cconv_ln_g"], sg["cconv_ln_b"],
                 sg["sgu_ln_g"], sg["sgu_ln_b"], sg["sgu_w"], dgb) = res[1:]
                sg["sgu_b"] = dgb[:, :, 0]
                comm = None
                if l == 0:
                    early = [w for w in widths if w != D]
                    comm = _gather_comm([pack(w) for w in early], [(b, None) for b in range(len(early))])
                g_in, extra = _matmul_tn(sv["h_mix"][None], dproj, N_DEV, "w_in_grad", comm)
                if l == 0:
                    for w, g in zip(early, extra):
                        gathered_small[widths.index(w)] = g
                dx, sg["mix_norm"], (r_in,) = _matmul_nt_normbwd(
                    dproj, W["w_in", l], sv["x_mix"], row(mix_norm, l), dx, "proj_bwd",
                    _scatter_comm([g_in], PEERS_SAME_CORE))
                second.append((("w_in", l), g_in, r_in, (PEERS_NEAR_OTHER, PEERS_FAR_OTHER)))
            else:
                keys, comm = rest_of_sends()
                dab, sh, extra = _ffn_bwd_hidden(dx, W[tag + "_w2", l], sv["ab_" + tag], "ffn_hidden_bwd", comm)
                settle(keys, extra)
                keys, comm = rest_of_sends()
                g_w2, extra = _matmul_tn(sh[None], dx[None], 1, "ffn_w2_grad", comm)
                settle(keys, extra)
                g_w2 = g_w2.reshape(N_DEV, F // N_DEV, D)
                g_w13, (R[tag + "_w2", l],) = _matmul_tn(dab, sv["h_" + tag][None], 1, "ffn_w13_grad",
                                                         _scatter_comm([g_w2]), b_shared=True)
                g_w13 = g_w13.reshape(N_DEV, fs, D)
                last = tag == "ffn1" and l == 0
                now, later = (PEERS_BUT_NEAR_OTHER, PEERS_NEAR_OTHER) if last else (PEERS_SAME_CORE, PEERS_OTHER_CORE)
                dx, sg[tag + "_norm"], (r_w13,) = _matmul_nt_normbwd(
                    dab, W[tag + "_w13", l], sv["x_" + tag], row(P[tag + "_norm"], l), dx, "ffn_up_bwd",
                    _scatter_comm([g_w13], now), w_t=True)
                second.append(((tag + "_w13", l), g_w13, r_w13, (later,)))
    grad_x = dx[None]
    out = {}

    def as2d(n, a):
        if n == "final_norm":
            return a.reshape(1, D)
        return a.reshape(-1, a.shape[-1])

    for i, n in enumerate(["w_out", "w_up", "ffn2_w13", "w_in", "ffn2_w2", "ffn1_w2", "ffn1_w13"]):
        shp = P[n].shape
        if n.endswith("w13"):
            flat, back = (lambda a: jnp.swapaxes(a, 1, 2)), (lambda a: jnp.swapaxes(a, 1, 2))
        else:
            rows, cols = math.prod(shp[1:-1]), shp[-1]
            flat, back = (lambda a: a.reshape(L, rows, cols)), (lambda a: a.reshape(shp))
        keys, comm = [], None
        if i == 0:
            keys, sends = rest_of_sends()
            comm = _join_comm(sends, _gather_comm([pack(D)], [(0, None)]))
        res, extra = _adamw_sharded([R[n, l] for l in range(L)], flat(P[n]), flat(P["m_" + n]), flat(P["v_" + n]),
                                    "adamw_sharded", comm)
        if i == 0:
            settle(keys, extra[:len(keys)])
            gathered_small[widths.index(D)] = extra[len(keys)]
        out[n] = tuple(back(a) for a in res)

    res = _adamw_replicated(gathered_small, [layout[n] for n in small_names],
                            [(as2d(n, P[n]), as2d(n, P["m_" + n]), as2d(n, P["v_" + n])) for n in small_names],
                            "adamw_replicated")
    for p, n in enumerate(small_names):
        out[n] = tuple(a.reshape(P[n].shape) for a in res[4 * p:4 * p + 4])
    conv_sum = res[4 * len(small_names) + conv_at["b"]][conv_at["r0"]:conv_at["r0"] + conv_at["rows"]]
    conv_mine = lax.dynamic_slice_in_dim(conv_sum, me * cw, cw, axis=1)

    def conv2d(a, b):
        return jnp.concatenate([a, b], axis=1).reshape(L * (SCONV_K + CCONV_K), cw)

    cd, cm, cv = _adamw_small(conv_mine, conv2d(sconv_w, cconv_w), conv2d(m_sconv_w, m_cconv_w),
                              conv2d(v_sconv_w, v_cconv_w), "adamw_conv")
    for n, sl in (("sconv_w", slice(0, SCONV_K)), ("cconv_w", slice(SCONV_K, SCONV_K + CCONV_K))):
        out[n] = tuple(a.reshape(L, SCONV_K + CCONV_K, cw)[:, sl] for a in (conv_mine, cd, cm, cv))

    order = ["ffn1_norm", "ffn1_w13", "ffn1_w2", "mix_norm", "w_in", "pool_w", "pool_scale", "sconv_w", "cconv_w",
             "cconv_ln_g", "cconv_ln_b", "sgu_ln_g", "sgu_ln_b", "sgu_w", "sgu_b", "w_up", "w_out", "ffn2_norm",
             "ffn2_w13", "ffn2_w2", "final_norm"]
    return (loss, grad_x, *[out[n][0] for n in order], *[out[n][1] for n in order],
            *[out[n][2] for n in order], *[out[n][3] for n in order])
```
